```python
import jax
import jax.numpy as jnp
from jax import lax
import numpy as np

D_MODEL = 1024
BATCH = 8
SEQ = 4096
DEPTH = 4

FFN_DIM = 2816
NORM_EPS = 1e-6
A_HEADS = 8
A_HEAD_DIM = 64
A_WIDTH = A_HEADS * A_HEAD_DIM
ROT_DIM = A_HEAD_DIM // 4
ROPE_THETA = 500000.0
DILATED_PATTERNS = ((128, 1), (512, 4), (2048, 16))
WIN_BLOCK = 128
CONV_CHANNELS = D_MODEL - A_WIDTH
CONV_WIDTH = 31
HYB_IN = 3 * A_WIDTH + 2 * CONV_CHANNELS
GDN_HEADS = 8
GDN_KEY_DIM = 128
GDN_VALUE_DIM = 128
GDN_QK_W = GDN_HEADS * GDN_KEY_DIM
GDN_V_W = GDN_HEADS * GDN_VALUE_DIM
GDN_QKV_W = 2 * GDN_QK_W + GDN_V_W
GDN_SHORT_CONV = 4
GDN_CHUNK = 64
GDN_IN = GDN_QKV_W + GDN_V_W + 2 * GDN_HEADS
N_EVEN = (DEPTH + 1) // 2
N_ODD = DEPTH // 2

kernel_name = "hybrid_dilated_conformer_gdn_trunk"


def rms_norm(x, g):
    xf = x.astype(jnp.float32)
    y = xf * lax.rsqrt(jnp.mean(xf * xf, axis=-1, keepdims=True) + NORM_EPS)
    return (y * g.astype(jnp.float32)).astype(x.dtype)


def layer_norm(x, g, b):
    xf = x.astype(jnp.float32)
    mu = jnp.mean(xf, axis=-1, keepdims=True)
    xc = xf - mu
    y = xc * lax.rsqrt(jnp.mean(xc * xc, axis=-1, keepdims=True) + NORM_EPS)
    return (y * g.astype(jnp.float32) + b.astype(jnp.float32)).astype(x.dtype)


def l2_normalize(x):
    xf = x.astype(jnp.float32)
    return xf * lax.rsqrt(jnp.sum(xf * xf, axis=-1, keepdims=True) + NORM_EPS)


def swiglu_ffn(h, w_in, w_out):
    gate, up = jnp.split(h @ w_in, 2, axis=-1)
    return (jax.nn.silu(gate) * up) @ w_out


def causal_depthwise_conv(x, w):
    width, ch = w.shape
    return lax.conv_general_dilated(
        x, w[:, None, :].astype(x.dtype), window_strides=(1,), padding=[(width - 1, 0)],
        dimension_numbers=("NWC", "WIO", "NWC"), feature_group_count=ch)


def rotary_angles(positions):
    inv_freq = jnp.power(jnp.float32(ROPE_THETA),
                         -jnp.arange(0, ROT_DIM, 2, dtype=jnp.float32) / ROT_DIM)
    ang = positions.astype(jnp.float32)[..., None] * inv_freq
    return jnp.cos(ang)[:, :, None, :], jnp.sin(ang)[:, :, None, :]


def apply_partial_rotary(x, cos, sin):
    half = ROT_DIM // 2
    xr = x[..., :ROT_DIM].astype(jnp.float32)
    x1, x2 = xr[..., :half], xr[..., half:]
    rot = jnp.concatenate([x1 * cos - x2 * sin, x2 * cos + x1 * sin], axis=-1).astype(x.dtype)
    return jnp.concatenate([rot, x[..., ROT_DIM:]], axis=-1)


def dilated_window_branch(q, k, v, window, dilation):
    B, S, H, hd = q.shape
    span = window // dilation
    L = S // dilation
    nb = -(-L // WIN_BLOCK)
    Lp = nb * WIN_BLOCK

    def to_blocks(t):
        t = jnp.moveaxis(t.reshape(B, L, dilation, H, hd), 2, 1)
        t = jnp.pad(t, ((0, 0), (0, 0), (0, Lp - L), (0, 0), (0, 0)))
        return t.reshape(B, dilation, nb, WIN_BLOCK, H, hd)

    def with_prev(t):
        prev = jnp.pad(t, ((0, 0), (0, 0), (1, 0), (0, 0), (0, 0), (0, 0)))[:, :, :nb]
        return jnp.concatenate([prev, t], axis=3)

    qb = to_blocks(q)
    kb = with_prev(to_blocks(k))
    vb = with_prev(to_blocks(v))
    s = jnp.einsum("brnqhd,brnkhd->brnhqk", qb, kb).astype(jnp.float32) * (hd ** -0.5)
    blk = jnp.arange(nb)[:, None, None] * WIN_BLOCK
    q_idx = blk + jnp.arange(WIN_BLOCK)[None, :, None]
    k_idx = blk - WIN_BLOCK + jnp.arange(2 * WIN_BLOCK)[None, None, :]
    dist = q_idx - k_idx
    allowed = (dist >= 0) & (dist <= span) & (k_idx >= 0)
    s = jnp.where(allowed[:, None], s, -jnp.inf)
    m = jnp.max(s, axis=-1, keepdims=True)
    p = jnp.exp(s - m)
    den = jnp.sum(p, axis=-1, keepdims=True)
    o = jnp.einsum("brnhqk,brnkhd->brnqhd", (p / den).astype(v.dtype), vb)
    lse = jnp.swapaxes((m + jnp.log(den))[..., 0], -1, -2)

    def from_blocks(t):
        t = t.reshape((B, dilation, Lp) + t.shape[4:])[:, :, :L]
        return jnp.moveaxis(t, 1, 2).reshape((B, S) + t.shape[3:])

    return from_blocks(o), from_blocks(lse)


def hybrid_mixer(h, cos, sin, w_in, dw_w, dw_b, ln_g, ln_b, w_out):
    B, S, _ = h.shape
    proj = h @ w_in
    heads = lambda t: t.reshape(B, S, A_HEADS, A_HEAD_DIM)
    q = apply_partial_rotary(heads(proj[..., :A_WIDTH]), cos, sin)
    k = apply_partial_rotary(heads(proj[..., A_WIDTH:2 * A_WIDTH]), cos, sin)
    v = heads(proj[..., 2 * A_WIDTH:3 * A_WIDTH])
    outs, lses = [], []
    for window, dilation in DILATED_PATTERNS:
        o_g, lse_g = dilated_window_branch(q, k, v, window, dilation)
        outs.append(o_g)
        lses.append(lse_g)
    mix_w = jax.nn.softmax(jnp.stack(lses), axis=0)
    attn = jnp.einsum("gbsh,gbshd->bshd", mix_w, jnp.stack(outs).astype(jnp.float32))
    attn = attn.reshape(B, S, A_WIDTH).astype(h.dtype)
    u = proj[..., 3 * A_WIDTH:]
    glu = u[..., :CONV_CHANNELS] * jax.nn.sigmoid(u[..., CONV_CHANNELS:])
    c = causal_depthwise_conv(glu, dw_w) + dw_b
    c = jax.nn.silu(layer_norm(c, ln_g, ln_b))
    return jnp.concatenate([attn, c], axis=-1) @ w_out


def gated_delta_rule(q, k, v, g, beta):
    B, S, H, dk = q.shape
    C = GDN_CHUNK
    N = S // C

    def chunked(t):
        return jnp.moveaxis(t.reshape((B, N, C, H) + t.shape[3:]), 3, 1)

    q = chunked(q) * (dk ** -0.5)
    k = chunked(k)
    v = chunked(v)
    beta = chunked(beta)
    g = jnp.cumsum(chunked(g), axis=-1)
    causal = jnp.tril(jnp.ones((C, C), dtype=bool))
    strict = jnp.tril(jnp.ones((C, C), dtype=bool), -1)
    decay = jnp.exp(jnp.where(causal, g[..., :, None] - g[..., None, :], -jnp.inf))
    k_beta = k * beta[..., None]
    l_mat = jnp.where(strict, jnp.einsum("bhnik,bhnjk->bhnij", k_beta, k) * decay, 0.0)
    eye = jnp.eye(C, dtype=q.dtype)
    t_inv = lax.linalg.triangular_solve(l_mat + eye, jnp.broadcast_to(eye, l_mat.shape),
                                        left_side=True, lower=True, unit_diagonal=True)
    u = jnp.einsum("bhnij,bhnjv->bhniv", t_inv, v * beta[..., None])
    w = jnp.einsum("bhnij,bhnjk->bhnik", t_inv, k_beta * jnp.exp(g)[..., None])
    attn = jnp.where(causal, jnp.einsum("bhnik,bhnjk->bhnij", q, k) * decay, 0.0)
    q_dec = q * jnp.exp(g)[..., None]
    g_last = g[..., -1]
    k_dec = k * jnp.exp(g_last[..., None] - g)[..., None]
    xs = tuple(jnp.moveaxis(t, 2, 0) for t in (w, u, q_dec, k_dec, attn, g_last))

    def step(state, inp):
        w_c, u_c, q_c, k_c, a_c, gl_c = inp
        v_new = u_c - jnp.einsum("bhik,bhkv->bhiv", w_c, state)
        o_c = jnp.einsum("bhik,bhkv->bhiv", q_c, state) + jnp.einsum("bhij,bhjv->bhiv", a_c, v_new)
        state = state * jnp.exp(gl_c)[..., None, None] + jnp.einsum("bhik,bhiv->bhkv", k_c, v_new)
        return state, o_c

    state0 = jnp.zeros((B, H, dk, v.shape[-1]), q.dtype)
    _, o = lax.scan(step, state0, xs)
    return jnp.transpose(o, (1, 0, 3, 2, 4)).reshape(B, S, H, -1)


def gated_deltanet_mixer(h, w_in, conv_w, a_log, dt_bias, norm_g, w_out):
    B, S, _ = h.shape
    proj = h @ w_in
    qkv = jax.nn.silu(causal_depthwise_conv(proj[..., :GDN_QKV_W], conv_w))
    z = proj[..., GDN_QKV_W:GDN_QKV_W + GDN_V_W]
    b = proj[..., GDN_QKV_W + GDN_V_W:GDN_QKV_W + GDN_V_W + GDN_HEADS]
    a = proj[..., GDN_QKV_W + GDN_V_W + GDN_HEADS:]
    q = qkv[..., :GDN_QK_W].reshape(B, S, GDN_HEADS, GDN_KEY_DIM)
    k = qkv[..., GDN_QK_W:2 * GDN_QK_W].reshape(B, S, GDN_HEADS, GDN_KEY_DIM)
    v = qkv[..., 2 * GDN_QK_W:].reshape(B, S, GDN_HEADS, GDN_VALUE_DIM)
    beta = jax.nn.sigmoid(b.astype(jnp.float32))
    g = -jnp.exp(a_log.astype(jnp.float32)) * jax.nn.softplus(
        a.astype(jnp.float32) + dt_bias.astype(jnp.float32))
    o = gated_delta_rule(l2_normalize(q), l2_normalize(k), v.astype(jnp.float32), g, beta)
    o = rms_norm(o, norm_g) * jax.nn.silu(
        z.reshape(B, S, GDN_HEADS, GDN_VALUE_DIM).astype(jnp.float32))
    return o.reshape(B, S, GDN_V_W).astype(h.dtype) @ w_out


def _fwd_setup_inputs(seed: int = 0) -> dict:
    key = jax.random.key(seed)
    ks = jax.random.split(key, 24)
    f32 = jnp.float32

    def dense(kk, shape, fan_in):
        return jax.random.normal(kk, shape, f32) * (fan_in ** -0.5)

    def gain(kk, shape):
        return 1.0 + 0.02 * jax.random.normal(kk, shape, f32)

    x = jax.random.normal(ks[0], (BATCH, SEQ, D_MODEL), f32)
    offsets = jax.random.randint(ks[1], (BATCH, 1), 0, 1024, dtype=jnp.int32)
    positions = offsets + jnp.arange(SEQ, dtype=jnp.int32)[None, :]
    dt = jnp.exp(jax.random.uniform(ks[18], (N_ODD, GDN_HEADS), f32,
                                    float(np.log(1e-3)), float(np.log(1e-1))))
    return {
        "x": x,
        "positions": positions,
        "ffn1_norm": gain(ks[2], (DEPTH, D_MODEL)),
        "ffn1_w_in": dense(ks[3], (DEPTH, D_MODEL, 2 * FFN_DIM), D_MODEL),
        "ffn1_w_out": dense(ks[4], (DEPTH, FFN_DIM, D_MODEL), FFN_DIM),
        "mix_norm": gain(ks[5], (DEPTH, D_MODEL)),
        "ffn2_norm": gain(ks[6], (DEPTH, D_MODEL)),
        "ffn2_w_in": dense(ks[7], (DEPTH, D_MODEL, 2 * FFN_DIM), D_MODEL),
        "ffn2_w_out": dense(ks[8], (DEPTH, FFN_DIM, D_MODEL), FFN_DIM),
        "hyb_w_in": dense(ks[9], (N_EVEN, D_MODEL, HYB_IN), D_MODEL),
        "hyb_dw_w": dense(ks[10], (N_EVEN, CONV_WIDTH, CONV_CHANNELS), CONV_WIDTH),
        "hyb_dw_b": 0.02 * jax.random.normal(ks[11], (N_EVEN, CONV_CHANNELS), f32),
        "hyb_ln_g": gain(ks[12], (N_EVEN, CONV_CHANNELS)),
        "hyb_ln_b": 0.02 * jax.random.normal(ks[13], (N_EVEN, CONV_CHANNELS), f32),
        "hyb_w_out": dense(ks[14], (N_EVEN, A_WIDTH + CONV_CHANNELS, D_MODEL), A_WIDTH + CONV_CHANNELS),
        "gdn_w_in": dense(ks[15], (N_ODD, D_MODEL, GDN_IN), D_MODEL),
        "gdn_conv_w": dense(ks[16], (N_ODD, GDN_SHORT_CONV, GDN_QKV_W), GDN_SHORT_CONV),
        "gdn_A_log": jnp.log(jax.random.uniform(ks[17], (N_ODD, GDN_HEADS), f32, 1.0, 16.0)),
        "gdn_dt_bias": dt + jnp.log(-jnp.expm1(-dt)),
        "gdn_norm_g": gain(ks[19], (N_ODD, GDN_VALUE_DIM)),
        "gdn_w_out": dense(ks[20], (N_ODD, GDN_V_W, D_MODEL), GDN_V_W),
        "final_norm": gain(ks[21], (D_MODEL,)),
    }


def _fwd_reference(x, positions, ffn1_norm, ffn1_w_in, ffn1_w_out, mix_norm, ffn2_norm, ffn2_w_in,
              ffn2_w_out, hyb_w_in, hyb_dw_w, hyb_dw_b, hyb_ln_g, hyb_ln_b, hyb_w_out,
              gdn_w_in, gdn_conv_w, gdn_A_log, gdn_dt_bias, gdn_norm_g, gdn_w_out, final_norm):
    cos, sin = rotary_angles(positions)
    h = x
    for layer in range(DEPTH):
        h = h + 0.5 * swiglu_ffn(rms_norm(h, ffn1_norm[layer]), ffn1_w_in[layer], ffn1_w_out[layer])
        hn = rms_norm(h, mix_norm[layer])
        i = layer // 2
        if layer % 2 == 0:
            mix = hybrid_mixer(hn, cos, sin, hyb_w_in[i], hyb_dw_w[i], hyb_dw_b[i],
                               hyb_ln_g[i], hyb_ln_b[i], hyb_w_out[i])
        else:
            mix = gated_deltanet_mixer(hn, gdn_w_in[i], gdn_conv_w[i], gdn_A_log[i],
                                       gdn_dt_bias[i], gdn_norm_g[i], gdn_w_out[i])
        h = h + mix
        h = h + 0.5 * swiglu_ffn(rms_norm(h, ffn2_norm[layer]), ffn2_w_in[layer], ffn2_w_out[layer])
    return rms_norm(h, final_norm)


import jax as _jax
import jax.numpy as _jnp

TWIN_FORMAT = 'train_step'
FWD_PARAMS = ['x', 'positions', 'ffn1_norm', 'ffn1_w_in', 'ffn1_w_out', 'mix_norm', 'ffn2_norm', 'ffn2_w_in', 'ffn2_w_out', 'hyb_w_in', 'hyb_dw_w', 'hyb_dw_b', 'hyb_ln_g', 'hyb_ln_b', 'hyb_w_out', 'gdn_w_in', 'gdn_conv_w', 'gdn_A_log', 'gdn_dt_bias', 'gdn_norm_g', 'gdn_w_out', 'final_norm']
TWIN_WEIGHTS = ['ffn1_norm', 'ffn1_w_in', 'ffn1_w_out', 'mix_norm', 'ffn2_norm', 'ffn2_w_in', 'ffn2_w_out', 'hyb_w_in', 'hyb_dw_w', 'hyb_dw_b', 'hyb_ln_g', 'hyb_ln_b', 'hyb_w_out', 'gdn_w_in', 'gdn_conv_w', 'gdn_A_log', 'gdn_dt_bias', 'gdn_norm_g', 'gdn_w_out', 'final_norm']
TWIN_DIFF_INPUT = 'x'
TWIN_INPUTS = ['x', 'positions', 'ffn1_norm', 'ffn1_w_in', 'ffn1_w_out', 'mix_norm', 'ffn2_norm', 'ffn2_w_in', 'ffn2_w_out', 'hyb_w_in', 'hyb_dw_w', 'hyb_dw_b', 'hyb_ln_g', 'hyb_ln_b', 'hyb_w_out', 'gdn_w_in', 'gdn_conv_w', 'gdn_A_log', 'gdn_dt_bias', 'gdn_norm_g', 'gdn_w_out', 'final_norm', 'loss_target', 'm_ffn1_norm', 'm_ffn1_w_in', 'm_ffn1_w_out', 'm_mix_norm', 'm_ffn2_norm', 'm_ffn2_w_in', 'm_ffn2_w_out', 'm_hyb_w_in', 'm_hyb_dw_w', 'm_hyb_dw_b', 'm_hyb_ln_g', 'm_hyb_ln_b', 'm_hyb_w_out', 'm_gdn_w_in', 'm_gdn_conv_w', 'm_gdn_A_log', 'm_gdn_dt_bias', 'm_gdn_norm_g', 'm_gdn_w_out', 'm_final_norm', 'v_ffn1_norm', 'v_ffn1_w_in', 'v_ffn1_w_out', 'v_mix_norm', 'v_ffn2_norm', 'v_ffn2_w_in', 'v_ffn2_w_out', 'v_hyb_w_in', 'v_hyb_dw_w', 'v_hyb_dw_b', 'v_hyb_ln_g', 'v_hyb_ln_b', 'v_hyb_w_out', 'v_gdn_w_in', 'v_gdn_conv_w', 'v_gdn_A_log', 'v_gdn_dt_bias', 'v_gdn_norm_g', 'v_gdn_w_out', 'v_final_norm']
TWIN_OUTPUTS = ['loss', 'grad_x', 'grad_ffn1_norm', 'grad_ffn1_w_in', 'grad_ffn1_w_out', 'grad_mix_norm', 'grad_ffn2_norm', 'grad_ffn2_w_in', 'grad_ffn2_w_out', 'grad_hyb_w_in', 'grad_hyb_dw_w', 'grad_hyb_dw_b', 'grad_hyb_ln_g', 'grad_hyb_ln_b', 'grad_hyb_w_out', 'grad_gdn_w_in', 'grad_gdn_conv_w', 'grad_gdn_A_log', 'grad_gdn_dt_bias', 'grad_gdn_norm_g', 'grad_gdn_w_out', 'grad_final_norm', 'delta_ffn1_norm', 'delta_ffn1_w_in', 'delta_ffn1_w_out', 'delta_mix_norm', 'delta_ffn2_norm', 'delta_ffn2_w_in', 'delta_ffn2_w_out', 'delta_hyb_w_in', 'delta_hyb_dw_w', 'delta_hyb_dw_b', 'delta_hyb_ln_g', 'delta_hyb_ln_b', 'delta_hyb_w_out', 'delta_gdn_w_in', 'delta_gdn_conv_w', 'delta_gdn_A_log', 'delta_gdn_dt_bias', 'delta_gdn_norm_g', 'delta_gdn_w_out', 'delta_final_norm', 'new_m_ffn1_norm', 'new_m_ffn1_w_in', 'new_m_ffn1_w_out', 'new_m_mix_norm', 'new_m_ffn2_norm', 'new_m_ffn2_w_in', 'new_m_ffn2_w_out', 'new_m_hyb_w_in', 'new_m_hyb_dw_w', 'new_m_hyb_dw_b', 'new_m_hyb_ln_g', 'new_m_hyb_ln_b', 'new_m_hyb_w_out', 'new_m_gdn_w_in', 'new_m_gdn_conv_w', 'new_m_gdn_A_log', 'new_m_gdn_dt_bias', 'new_m_gdn_norm_g', 'new_m_gdn_w_out', 'new_m_final_norm', 'new_v_ffn1_norm', 'new_v_ffn1_w_in', 'new_v_ffn1_w_out', 'new_v_mix_norm', 'new_v_ffn2_norm', 'new_v_ffn2_w_in', 'new_v_ffn2_w_out', 'new_v_hyb_w_in', 'new_v_hyb_dw_w', 'new_v_hyb_dw_b', 'new_v_hyb_ln_g', 'new_v_hyb_ln_b', 'new_v_hyb_w_out', 'new_v_gdn_w_in', 'new_v_gdn_conv_w', 'new_v_gdn_A_log', 'new_v_gdn_dt_bias', 'new_v_gdn_norm_g', 'new_v_gdn_w_out', 'new_v_final_norm']
TWIN_LEAF_KINDS = {'loss': 'loss', 'grad_x': 'grad_x', 'grad_ffn1_norm': 'grad_w', 'grad_ffn1_w_in': 'grad_w', 'grad_ffn1_w_out': 'grad_w', 'grad_mix_norm': 'grad_w', 'grad_ffn2_norm': 'grad_w', 'grad_ffn2_w_in': 'grad_w', 'grad_ffn2_w_out': 'grad_w', 'grad_hyb_w_in': 'grad_w', 'grad_hyb_dw_w': 'grad_w', 'grad_hyb_dw_b': 'grad_w', 'grad_hyb_ln_g': 'grad_w', 'grad_hyb_ln_b': 'grad_w', 'grad_hyb_w_out': 'grad_w', 'grad_gdn_w_in': 'grad_w', 'grad_gdn_conv_w': 'grad_w', 'grad_gdn_A_log': 'grad_w', 'grad_gdn_dt_bias': 'grad_w', 'grad_gdn_norm_g': 'grad_w', 'grad_gdn_w_out': 'grad_w', 'grad_final_norm': 'grad_w', 'delta_ffn1_norm': 'delta_w', 'delta_ffn1_w_in': 'delta_w', 'delta_ffn1_w_out': 'delta_w', 'delta_mix_norm': 'delta_w', 'delta_ffn2_norm': 'delta_w', 'delta_ffn2_w_in': 'delta_w', 'delta_ffn2_w_out': 'delta_w', 'delta_hyb_w_in': 'delta_w', 'delta_hyb_dw_w': 'delta_w', 'delta_hyb_dw_b': 'delta_w', 'delta_hyb_ln_g': 'delta_w', 'delta_hyb_ln_b': 'delta_w', 'delta_hyb_w_out': 'delta_w', 'delta_gdn_w_in': 'delta_w', 'delta_gdn_conv_w': 'delta_w', 'delta_gdn_A_log': 'delta_w', 'delta_gdn_dt_bias': 'delta_w', 'delta_gdn_norm_g': 'delta_w', 'delta_gdn_w_out': 'delta_w', 'delta_final_norm': 'delta_w', 'new_m_ffn1_norm': 'new_m', 'new_m_ffn1_w_in': 'new_m', 'new_m_ffn1_w_out': 'new_m', 'new_m_mix_norm': 'new_m', 'new_m_ffn2_norm': 'new_m', 'new_m_ffn2_w_in': 'new_m', 'new_m_ffn2_w_out': 'new_m', 'new_m_hyb_w_in': 'new_m', 'new_m_hyb_dw_w': 'new_m', 'new_m_hyb_dw_b': 'new_m', 'new_m_hyb_ln_g': 'new_m', 'new_m_hyb_ln_b': 'new_m', 'new_m_hyb_w_out': 'new_m', 'new_m_gdn_w_in': 'new_m', 'new_m_gdn_conv_w': 'new_m', 'new_m_gdn_A_log': 'new_m', 'new_m_gdn_dt_bias': 'new_m', 'new_m_gdn_norm_g': 'new_m', 'new_m_gdn_w_out': 'new_m', 'new_m_final_norm': 'new_m', 'new_v_ffn1_norm': 'new_v', 'new_v_ffn1_w_in': 'new_v', 'new_v_ffn1_w_out': 'new_v', 'new_v_mix_norm': 'new_v', 'new_v_ffn2_norm': 'new_v', 'new_v_ffn2_w_in': 'new_v', 'new_v_ffn2_w_out': 'new_v', 'new_v_hyb_w_in': 'new_v', 'new_v_hyb_dw_w': 'new_v', 'new_v_hyb_dw_b': 'new_v', 'new_v_hyb_ln_g': 'new_v', 'new_v_hyb_ln_b': 'new_v', 'new_v_hyb_w_out': 'new_v', 'new_v_gdn_w_in': 'new_v', 'new_v_gdn_conv_w': 'new_v', 'new_v_gdn_A_log': 'new_v', 'new_v_gdn_dt_bias': 'new_v', 'new_v_gdn_norm_g': 'new_v', 'new_v_gdn_w_out': 'new_v', 'new_v_final_norm': 'new_v'}


def _forward(args):
    return _fwd_reference(*[args[k] for k in FWD_PARAMS])


def _output_shape():
    out = _jax.eval_shape(lambda: _forward(_fwd_setup_inputs(0)))
    return out.shape, out.dtype

N_MICROBATCH = 1
ADAM_LR = 0.001
ADAM_B1 = 0.9
ADAM_B2 = 0.999
ADAM_EPS = 1e-08
ADAM_WD = 0.01
ADAM_STEP = 10
PER_EXAMPLE_BATCH_AXIS = {'x': 0, 'positions': 0, 'loss_target': 0}
SHARED_INPUTS = []
_WEIGHT_DTYPES = {'ffn1_norm': _jnp.float32, 'ffn1_w_in': _jnp.float32, 'ffn1_w_out': _jnp.float32, 'mix_norm': _jnp.float32, 'ffn2_norm': _jnp.float32, 'ffn2_w_in': _jnp.float32, 'ffn2_w_out': _jnp.float32, 'hyb_w_in': _jnp.float32, 'hyb_dw_w': _jnp.float32, 'hyb_dw_b': _jnp.float32, 'hyb_ln_g': _jnp.float32, 'hyb_ln_b': _jnp.float32, 'hyb_w_out': _jnp.float32, 'gdn_w_in': _jnp.float32, 'gdn_conv_w': _jnp.float32, 'gdn_A_log': _jnp.float32, 'gdn_dt_bias': _jnp.float32, 'gdn_norm_g': _jnp.float32, 'gdn_w_out': _jnp.float32, 'final_norm': _jnp.float32}
MOMENT_SCALE = {'ffn1_norm': 8.621850e-02, 'ffn1_w_in': 3.641812e-02, 'ffn1_w_out': 5.940221e-02, 'mix_norm': 1.224861e-01, 'ffn2_norm': 7.111497e-02, 'ffn2_w_in': 3.007440e-02, 'ffn2_w_out': 4.912221e-02, 'hyb_w_in': 6.763805e-02, 'hyb_dw_w': 1.258988e-01, 'hyb_dw_b': 2.794607e-01, 'hyb_ln_g': 1.471103e-01, 'hyb_ln_b': 1.395907e-01, 'hyb_w_out': 9.209158e-02, 'gdn_w_in': 7.044816e-02, 'gdn_conv_w': 6.638546e-02, 'gdn_A_log': 5.560438e-01, 'gdn_dt_bias': 5.518161e-01, 'gdn_norm_g': 2.299333e-01, 'gdn_w_out': 8.376647e-02, 'final_norm': 3.205469e+01}


def _to_microbatches(a, axis):
    t = _jnp.moveaxis(a, axis, 0)
    t = t.reshape((N_MICROBATCH, t.shape[0] // N_MICROBATCH) + t.shape[1:])
    return _jnp.moveaxis(t, 1, axis + 1)


def setup_inputs(seed: int = 0) -> dict:
    inp = _fwd_setup_inputs(seed)
    key = _jax.random.fold_in(_jax.random.key(seed), 7919)
    shape, _ = _output_shape()
    out = dict(inp)
    out["loss_target"] = _jax.random.normal(_jax.random.fold_in(key, 0), shape, _jnp.float32)
    for i, name in enumerate(TWIN_WEIGHTS):
        w = inp[name].astype(_jnp.float32)
        if MOMENT_SCALE is None:
            s = _jnp.sqrt(_jnp.mean(_jnp.square(w)) + 1e-30)
        else:
            s = MOMENT_SCALE[name]
        km, kv = _jax.random.split(_jax.random.fold_in(key, i + 1))
        out[name] = w
        out["m_" + name] = s * _jax.random.normal(km, w.shape, _jnp.float32)
        out["v_" + name] = (s * s) * _jax.random.uniform(kv, w.shape, _jnp.float32, 0.5, 1.5)
    if N_MICROBATCH > 1:
        for name, axis in PER_EXAMPLE_BATCH_AXIS.items():
            out[name] = _to_microbatches(out[name], axis)
    return {'x': out['x'], 'positions': out['positions'], 'ffn1_norm': out['ffn1_norm'], 'ffn1_w_in': out['ffn1_w_in'], 'ffn1_w_out': out['ffn1_w_out'], 'mix_norm': out['mix_norm'], 'ffn2_norm': out['ffn2_norm'], 'ffn2_w_in': out['ffn2_w_in'], 'ffn2_w_out': out['ffn2_w_out'], 'hyb_w_in': out['hyb_w_in'], 'hyb_dw_w': out['hyb_dw_w'], 'hyb_dw_b': out['hyb_dw_b'], 'hyb_ln_g': out['hyb_ln_g'], 'hyb_ln_b': out['hyb_ln_b'], 'hyb_w_out': out['hyb_w_out'], 'gdn_w_in': out['gdn_w_in'], 'gdn_conv_w': out['gdn_conv_w'], 'gdn_A_log': out['gdn_A_log'], 'gdn_dt_bias': out['gdn_dt_bias'], 'gdn_norm_g': out['gdn_norm_g'], 'gdn_w_out': out['gdn_w_out'], 'final_norm': out['final_norm'], 'loss_target': out['loss_target'], 'm_ffn1_norm': out['m_ffn1_norm'], 'm_ffn1_w_in': out['m_ffn1_w_in'], 'm_ffn1_w_out': out['m_ffn1_w_out'], 'm_mix_norm': out['m_mix_norm'], 'm_ffn2_norm': out['m_ffn2_norm'], 'm_ffn2_w_in': out['m_ffn2_w_in'], 'm_ffn2_w_out': out['m_ffn2_w_out'], 'm_hyb_w_in': out['m_hyb_w_in'], 'm_hyb_dw_w': out['m_hyb_dw_w'], 'm_hyb_dw_b': out['m_hyb_dw_b'], 'm_hyb_ln_g': out['m_hyb_ln_g'], 'm_hyb_ln_b': out['m_hyb_ln_b'], 'm_hyb_w_out': out['m_hyb_w_out'], 'm_gdn_w_in': out['m_gdn_w_in'], 'm_gdn_conv_w': out['m_gdn_conv_w'], 'm_gdn_A_log': out['m_gdn_A_log'], 'm_gdn_dt_bias': out['m_gdn_dt_bias'], 'm_gdn_norm_g': out['m_gdn_norm_g'], 'm_gdn_w_out': out['m_gdn_w_out'], 'm_final_norm': out['m_final_norm'], 'v_ffn1_norm': out['v_ffn1_norm'], 'v_ffn1_w_in': out['v_ffn1_w_in'], 'v_ffn1_w_out': out['v_ffn1_w_out'], 'v_mix_norm': out['v_mix_norm'], 'v_ffn2_norm': out['v_ffn2_norm'], 'v_ffn2_w_in': out['v_ffn2_w_in'], 'v_ffn2_w_out': out['v_ffn2_w_out'], 'v_hyb_w_in': out['v_hyb_w_in'], 'v_hyb_dw_w': out['v_hyb_dw_w'], 'v_hyb_dw_b': out['v_hyb_dw_b'], 'v_hyb_ln_g': out['v_hyb_ln_g'], 'v_hyb_ln_b': out['v_hyb_ln_b'], 'v_hyb_w_out': out['v_hyb_w_out'], 'v_gdn_w_in': out['v_gdn_w_in'], 'v_gdn_conv_w': out['v_gdn_conv_w'], 'v_gdn_A_log': out['v_gdn_A_log'], 'v_gdn_dt_bias': out['v_gdn_dt_bias'], 'v_gdn_norm_g': out['v_gdn_norm_g'], 'v_gdn_w_out': out['v_gdn_w_out'], 'v_final_norm': out['v_final_norm']}


def _loss(weights, diff, rest, loss_target):
    with _jax.named_scope("forward"):
        args = {**rest, TWIN_DIFF_INPUT: diff, **{k: w.astype(_WEIGHT_DTYPES[k]) for k, w in weights.items()}}
        y = _forward(args)
    with _jax.named_scope("loss_head"):
        err = _jnp.square(y.astype(_jnp.float32) - loss_target)
        return 0.5 * _jnp.sum(_jnp.mean(err, axis=-1)) if err.ndim else 0.5 * err


def _adamw(w, g, m, v):
    m = ADAM_B1 * m + (1.0 - ADAM_B1) * g
    v = ADAM_B2 * v + (1.0 - ADAM_B2) * _jnp.square(g)
    m_hat = m / (1.0 - ADAM_B1 ** ADAM_STEP)
    v_hat = v / (1.0 - ADAM_B2 ** ADAM_STEP)
    delta = -ADAM_LR * (m_hat / (_jnp.sqrt(v_hat) + ADAM_EPS) + ADAM_WD * w)
    return delta, m, v


def reference(x, positions, ffn1_norm, ffn1_w_in, ffn1_w_out, mix_norm, ffn2_norm, ffn2_w_in, ffn2_w_out, hyb_w_in, hyb_dw_w, hyb_dw_b, hyb_ln_g, hyb_ln_b, hyb_w_out, gdn_w_in, gdn_conv_w, gdn_A_log, gdn_dt_bias, gdn_norm_g, gdn_w_out, final_norm, loss_target, m_ffn1_norm, m_ffn1_w_in, m_ffn1_w_out, m_mix_norm, m_ffn2_norm, m_ffn2_w_in, m_ffn2_w_out, m_hyb_w_in, m_hyb_dw_w, m_hyb_dw_b, m_hyb_ln_g, m_hyb_ln_b, m_hyb_w_out, m_gdn_w_in, m_gdn_conv_w, m_gdn_A_log, m_gdn_dt_bias, m_gdn_norm_g, m_gdn_w_out, m_final_norm, v_ffn1_norm, v_ffn1_w_in, v_ffn1_w_out, v_mix_norm, v_ffn2_norm, v_ffn2_w_in, v_ffn2_w_out, v_hyb_w_in, v_hyb_dw_w, v_hyb_dw_b, v_hyb_ln_g, v_hyb_ln_b, v_hyb_w_out, v_gdn_w_in, v_gdn_conv_w, v_gdn_A_log, v_gdn_dt_bias, v_gdn_norm_g, v_gdn_w_out, v_final_norm):
    given = dict(x=x, positions=positions, ffn1_norm=ffn1_norm, ffn1_w_in=ffn1_w_in, ffn1_w_out=ffn1_w_out, mix_norm=mix_norm, ffn2_norm=ffn2_norm, ffn2_w_in=ffn2_w_in, ffn2_w_out=ffn2_w_out, hyb_w_in=hyb_w_in, hyb_dw_w=hyb_dw_w, hyb_dw_b=hyb_dw_b, hyb_ln_g=hyb_ln_g, hyb_ln_b=hyb_ln_b, hyb_w_out=hyb_w_out, gdn_w_in=gdn_w_in, gdn_conv_w=gdn_conv_w, gdn_A_log=gdn_A_log, gdn_dt_bias=gdn_dt_bias, gdn_norm_g=gdn_norm_g, gdn_w_out=gdn_w_out, final_norm=final_norm, loss_target=loss_target, m_ffn1_norm=m_ffn1_norm, m_ffn1_w_in=m_ffn1_w_in, m_ffn1_w_out=m_ffn1_w_out, m_mix_norm=m_mix_norm, m_ffn2_norm=m_ffn2_norm, m_ffn2_w_in=m_ffn2_w_in, m_ffn2_w_out=m_ffn2_w_out, m_hyb_w_in=m_hyb_w_in, m_hyb_dw_w=m_hyb_dw_w, m_hyb_dw_b=m_hyb_dw_b, m_hyb_ln_g=m_hyb_ln_g, m_hyb_ln_b=m_hyb_ln_b, m_hyb_w_out=m_hyb_w_out, m_gdn_w_in=m_gdn_w_in, m_gdn_conv_w=m_gdn_conv_w, m_gdn_A_log=m_gdn_A_log, m_gdn_dt_bias=m_gdn_dt_bias, m_gdn_norm_g=m_gdn_norm_g, m_gdn_w_out=m_gdn_w_out, m_final_norm=m_final_norm, v_ffn1_norm=v_ffn1_norm, v_ffn1_w_in=v_ffn1_w_in, v_ffn1_w_out=v_ffn1_w_out, v_mix_norm=v_mix_norm, v_ffn2_norm=v_ffn2_norm, v_ffn2_w_in=v_ffn2_w_in, v_ffn2_w_out=v_ffn2_w_out, v_hyb_w_in=v_hyb_w_in, v_hyb_dw_w=v_hyb_dw_w, v_hyb_dw_b=v_hyb_dw_b, v_hyb_ln_g=v_hyb_ln_g, v_hyb_ln_b=v_hyb_ln_b, v_hyb_w_out=v_hyb_w_out, v_gdn_w_in=v_gdn_w_in, v_gdn_conv_w=v_gdn_conv_w, v_gdn_A_log=v_gdn_A_log, v_gdn_dt_bias=v_gdn_dt_bias, v_gdn_norm_g=v_gdn_norm_g, v_gdn_w_out=v_gdn_w_out, v_final_norm=v_final_norm)
    weights = {n: given[n] for n in TWIN_WEIGHTS}
    shared = {n: given[n] for n in SHARED_INPUTS}
    per_example = {n: given[n] for n in ['x', 'positions']}
    grad_fn = _jax.value_and_grad(_loss, argnums=(0, 1))

    def one_microbatch(ex, loss_target):
        ex = dict(ex)
        diff = ex.pop(TWIN_DIFF_INPUT)
        return grad_fn(weights, diff, {**shared, **ex}, loss_target)

    if N_MICROBATCH == 1:
        loss, (grad_w, grad_x) = one_microbatch(per_example, given["loss_target"])
    else:
        def body(carry, xs):
            loss_sum, grad_sum = carry
            l_k, (gw_k, gx_k) = one_microbatch(xs[0], xs[1])
            with _jax.named_scope("update"):
                return (loss_sum + l_k, _jax.tree.map(_jnp.add, grad_sum, gw_k)), gx_k

        init = (_jnp.zeros((), _jnp.float32), _jax.tree.map(_jnp.zeros_like, weights))
        (loss, grad_w), grad_x = _jax.lax.scan(body, init, (per_example, given["loss_target"]))
    with _jax.named_scope("update"):
        delta_w, new_m, new_v = {}, {}, {}
        for n in TWIN_WEIGHTS:
            delta_w[n], new_m[n], new_v[n] = _adamw(weights[n], grad_w[n], given["m_" + n], given["v_" + n])
    return (loss, grad_x, *[grad_w[n] for n in TWIN_WEIGHTS], *[delta_w[n] for n in TWIN_WEIGHTS],
            *[new_m[n] for n in TWIN_WEIGHTS], *[new_v[n] for n in TWIN_WEIGHTS])
```

```python
import functools

import jax
import jax.numpy as jnp
from jax import lax
from jax.experimental import pallas as pl
from jax.experimental.pallas import tpu as pltpu

_BF = jnp.bfloat16
_F32 = jnp.float32
_NORM_EPS = 1e-6
_VMEM_LIMIT_BYTES = 56 * 1024 * 1024
_HI = lax.Precision.HIGHEST

_A_HEADS = 8
_A_HEAD_DIM = 64
_WIN_BLOCK = 128
_DILATIONS = (1, 4, 16)
_CONV_WIDTH = 31
_GDN_HEADS = 8
_GDN_DIM = 128
_GDN_CHUNK = 64
_GDN_SHORT_CONV = 4

_ADAM_LR = 0.001
_ADAM_B1 = 0.9
_ADAM_B2 = 0.999
_ADAM_EPS = 1e-08
_ADAM_WD = 0.01
_ADAM_STEP = 10


def _params(sem):
    return pltpu.CompilerParams(dimension_semantics=sem, vmem_limit_bytes=_VMEM_LIMIT_BYTES)


def _dot(a, b, dn=(((1,), (0,)), ((), ()))):
    return lax.dot_general(a.astype(_BF), b.astype(_BF), dn, preferred_element_type=_F32)


def _dot_nt(a, b):
    return _dot(a, b, (((1,), (1,)), ((), ())))


def _dot_tn(a, b):
    return _dot(a, b, (((0,), (0,)), ((), ())))


def _dot_hi(a, b):
    return lax.dot_general(a, b, (((1,), (0,)), ((), ())), precision=_HI, preferred_element_type=_F32)


_NN = (((1,), (0,)), ((), ()))
_NT = (((1,), (1,)), ((), ()))
_TN = (((0,), (0,)), ((), ()))
_ANY = pl.BlockSpec(memory_space=pl.ANY)


def _mm(name, a, b, *, grid, a_spec, b_spec, dn, out_shape, out_spec, red=None, acc_shape=None,
        pre_a=None, pre_b=None, epi=None, extras=(), extra_specs=(), into=None):
    n_ex = len(extras)
    n_in = 2 + n_ex + (1 if into is not None else 0)
    nk = grid[red] if red is not None else 1

    def body(*refs):
        a_ref, b_ref = refs[0], refs[1]
        ex = refs[2:2 + n_ex]
        o_ref = refs[n_in]
        av, bv = a_ref[...], b_ref[...]
        if pre_a is not None:
            av = pre_a(av)
        if pre_b is not None:
            bv = pre_b(bv)
        part = lax.dot_general(av.astype(_BF), bv.astype(_BF), dn, preferred_element_type=_F32)

        def finish(acc):
            res = epi(acc, *[e[...] for e in ex]) if epi is not None else acc
            if isinstance(res, tuple):
                for idx, val in enumerate(res):
                    o_ref[idx] = val.astype(o_ref.dtype)
            else:
                o_ref[...] = res.astype(o_ref.dtype)

        if red is None:
            finish(part)
        else:
            acc_ref = refs[n_in + 1]
            k = pl.program_id(red)

            @pl.when(k == 0)
            def _():
                acc_ref[...] = part

            @pl.when(k > 0)
            def _():
                acc_ref[...] += part

            @pl.when(k == nk - 1)
            def _():
                finish(acc_ref[...])

    in_specs = [a_spec, b_spec, *extra_specs]
    operands = [a, b, *extras]
    aliases = {}
    if into is not None:
        in_specs.append(_ANY)
        operands.append(into)
        aliases = {n_in - 1: 0}
    return pl.pallas_call(
        body, name=name, grid=grid, in_specs=in_specs, out_specs=out_spec, out_shape=out_shape,
        scratch_shapes=[pltpu.VMEM(acc_shape, _F32)] if red is not None else [],
        input_output_aliases=aliases,
        compiler_params=_params(("arbitrary",) * len(grid)),
    )(*operands)


def _rowmap(name, body, tiled, params, tile_outs, param_outs=(), *, rows, tm, ncol=1, col_outer=False,
            into=None):
    nt, npar, nto, npo = len(tiled), len(params), len(tile_outs), len(param_outs)
    assert rows % tm == 0, (name, rows, tm)
    nrow = rows // tm
    grid = (ncol, nrow) if col_outer else (nrow, ncol)

    def ij(g):
        return (g[1], g[0]) if col_outer else (g[0], g[1])

    def cidx(col, j):
        return col(j) if callable(col) else col

    in_specs, operands = [], []
    for arr, width, col in tiled:
        operands.append(arr)
        if arr.ndim == 3:
            in_specs.append(pl.BlockSpec((arr.shape[0], tm, width),
                                         lambda *g, col=col: (0, ij(g)[0], cidx(col, ij(g)[1]))))
        else:
            in_specs.append(pl.BlockSpec((tm, width), lambda *g, col=col: (ij(g)[0], cidx(col, ij(g)[1]))))
    for p in params:
        if isinstance(p, tuple):
            arr, width, col = p
            operands.append(arr)
            in_specs.append(pl.BlockSpec((arr.shape[0], width), lambda *g, col=col: (0, cidx(col, ij(g)[1]))))
        else:
            operands.append(p)
            in_specs.append(pl.BlockSpec(p.shape, lambda *g, nd=p.ndim: (0,) * nd))
    out_shape, out_specs = [], []
    for total, width, col, dt in tile_outs:
        out_shape.append(jax.ShapeDtypeStruct((rows, total), dt))
        out_specs.append(pl.BlockSpec((tm, width), lambda *g, col=col: (ij(g)[0], cidx(col, ij(g)[1]))))
    for shp in param_outs:
        out_shape.append(jax.ShapeDtypeStruct(shp, _F32))
        out_specs.append(pl.BlockSpec(shp, lambda *g, nd=len(shp): (0,) * nd))
    n_in = nt + npar
    aliases = {}
    if into is not None:
        operands.append(into[0])
        in_specs.append(_ANY)
        aliases = {n_in: into[1]}
        n_in += 1

    def kern(*refs):
        vals = [r[...] for r in refs[:nt + npar]]
        touts = refs[n_in:n_in + nto]
        pouts = refs[n_in + nto:]
        res = body(*vals)
        if not isinstance(res, (tuple, list)):
            res = (res,)
        for r, val in zip(touts, res[:nto]):
            r[...] = val.astype(r.dtype)
        if npo:
            @pl.when((pl.program_id(0) == 0) & (pl.program_id(1) == 0))
            def _():
                for r in pouts:
                    r[...] = jnp.zeros_like(r)

            for r, val in zip(pouts, res[nto:]):
                r[...] += val

    return pl.pallas_call(
        kern, name=name, grid=grid, in_specs=in_specs, out_specs=out_specs, out_shape=out_shape,
        input_output_aliases=aliases, compiler_params=_params(("arbitrary", "arbitrary")),
    )(*operands)


def _vjp_body(fn, n_in, diff, n_ct):
    def body(*vals):
        ins, cts = vals[:n_in], vals[n_in:n_in + n_ct]
        outs, pull = jax.vjp(fn, *ins)
        single = not isinstance(outs, (tuple, list))
        if single:
            grads = pull(cts[0].astype(outs.dtype))
        else:
            grads = pull(tuple(c.astype(o.dtype) for c, o in zip(cts, outs)))
        return tuple(grads[i] for i in diff)
    return body


_TM = 512


def _rms(h, gamma):
    hf = h.astype(_F32)
    return hf * lax.rsqrt(jnp.mean(hf * hf, axis=-1, keepdims=True) + _NORM_EPS) * gamma


def _rmsnorm_fwd(h, gamma):
    s, d = h.shape
    return _rowmap("rmsnorm_fwd", lambda hv, gv: _rms(hv, gv).astype(_BF), [(h, d, 0)], [gamma],
                   [(d, d, 0, _BF)], rows=s, tm=_TM)[0]


def _rmsnorm_bwd(h, gamma, dxn, dh_out):
    s, d = h.shape

    def body(hv, dxv, dhv, gv):
        _, pull = jax.vjp(_rms, hv, gv)
        dh, dg = pull(dxv.astype(_F32))
        return dhv + dh, dg

    return _rowmap("rmsnorm_bwd", body, [(h, d, 0), (dxn, d, 0), (dh_out, d, 0)], [gamma],
                   [(d, d, 0, _F32)], [(1, d)], rows=s, tm=_TM)


def _silu(x):
    return x * jax.nn.sigmoid(x)


def _ffn_up(xn, w_in, layer):
    s, d = xn.shape
    ns = w_in.shape[-1]

    def body(x_ref, wg_ref, wu_ref, g_ref, u_ref, a_ref):
        xv = x_ref[...]
        g = jnp.dot(xv, wg_ref[...], preferred_element_type=_F32)
        u = jnp.dot(xv, wu_ref[...], preferred_element_type=_F32)
        g_ref[...] = g.astype(_BF)
        u_ref[...] = u.astype(_BF)
        a_ref[...] = (_silu(g) * u).astype(_BF)

    out = pl.BlockSpec((_TM, ns), lambda j, i: (i, j))
    sds = jax.ShapeDtypeStruct((s, 2 * ns), _BF)
    return pl.pallas_call(
        body, name="ffn_up", grid=(2, s // _TM),
        in_specs=[pl.BlockSpec((_TM, d), lambda j, i: (i, 0)),
                  pl.BlockSpec((None, None, d, ns), lambda j, i: (layer, j, 0, 0)),
                  pl.BlockSpec((None, None, d, ns), lambda j, i: (layer, j + 2, 0, 0))],
        out_specs=[out, out, out], out_shape=[sds, sds, sds],
        compiler_params=_params(("arbitrary", "arbitrary")),
    )(xn, w_in, w_in)


def _proj_residual(name, y, w, layer, h, scale):
    s, k = y.shape
    d = w.shape[-1]
    return _mm(name, y, w, grid=(s // _TM,), a_spec=pl.BlockSpec((_TM, k), lambda i: (i, 0)),
               b_spec=pl.BlockSpec((None, k, d), lambda i: (layer, 0, 0)), dn=_NN,
               out_shape=jax.ShapeDtypeStruct((s, d), _F32), out_spec=pl.BlockSpec((_TM, d), lambda i: (i, 0)),
               epi=lambda acc, hv: hv + scale * acc, extras=(h,),
               extra_specs=(pl.BlockSpec((_TM, d), lambda i: (i, 0)),))


def _ffn_dact(dh, w_out, layer, g, u):
    s, d = dh.shape
    f = g.shape[1]
    ns = f // 2

    def epi(da, gv, uv):
        gf, uf = gv.astype(_F32), uv.astype(_F32)
        sg = jax.nn.sigmoid(gf)
        return da * uf * (sg * (1.0 + gf * (1.0 - sg))), da * gf * sg

    gu = pl.BlockSpec((_TM, ns), lambda n, i: (i, n))
    return _mm("ffn_dact", dh, w_out, grid=(2, s // _TM), a_spec=pl.BlockSpec((_TM, d), lambda n, i: (i, 0)),
               b_spec=pl.BlockSpec((None, ns, d), lambda n, i: (layer, n, 0)), dn=_NT,
               out_shape=jax.ShapeDtypeStruct((2, s, f), _BF), out_spec=pl.BlockSpec((2, _TM, ns), lambda n, i: (0, i, n)),
               pre_a=lambda v: 0.5 * v, epi=epi, extras=(g, u), extra_specs=(gu, gu))


def _wgrad_rows(name, y, dh, layer, n_layers, buf, scale):
    s, k = y.shape
    d = dh.shape[1]
    tk = k // 2 if (k // 2) % 128 == 0 else k
    return _mm(name, y, dh, grid=(k // tk, s // _TM), a_spec=pl.BlockSpec((_TM, tk), lambda n, i: (i, n)),
               b_spec=pl.BlockSpec((_TM, d), lambda n, i: (i, 0)), dn=_TN, red=1, acc_shape=(tk, d),
               out_shape=jax.ShapeDtypeStruct((n_layers, k, d), _BF),
               out_spec=pl.BlockSpec((None, tk, d), lambda n, i: (layer, n, 0)),
               pre_b=(lambda v: scale * v) if scale != 1.0 else None, into=buf)


def _ffn_dxn(dz, w_in, layer):
    _, s, f = dz.shape
    d, ns = w_in.shape[2], w_in.shape[3]
    return _mm("ffn_dxn", dz, w_in, grid=(s // _TM, 4),
               a_spec=pl.BlockSpec((None, _TM, ns), lambda i, j: (j // 2, i, j % 2)),
               b_spec=pl.BlockSpec((None, None, d, ns), lambda i, j: (layer, j, 0, 0)), dn=_NT, red=1,
               acc_shape=(_TM, d), out_shape=jax.ShapeDtypeStruct((s, d), _F32),
               out_spec=pl.BlockSpec((_TM, d), lambda i, j: (i, 0)))


def _ffn_dwin(xn, dz, layer, n_layers, buf):
    s, d = xn.shape
    ns = dz.shape[2] // 2
    return _mm("ffn_dwin", xn, dz, grid=(4, s // _TM), a_spec=pl.BlockSpec((_TM, d), lambda j, i: (i, 0)),
               b_spec=pl.BlockSpec((None, _TM, ns), lambda j, i: (j // 2, i, j % 2)), dn=_TN, red=1,
               acc_shape=(d, ns), out_shape=jax.ShapeDtypeStruct((n_layers, 4, d, ns), _BF),
               out_spec=pl.BlockSpec((None, None, d, ns), lambda j, i: (layer, j, 0, 0)), into=buf)


def _ffn_fwd(h, gamma, w_in, w_out, layer):
    xn = _rmsnorm_fwd(h, gamma)
    g, u, act = _ffn_up(xn, w_in, layer)
    h_new = _proj_residual("ffn_down", act, w_out, layer, h, 0.5)
    return h_new, (h, xn, g, u, act)


def _ffn_bwd(dh, saved, gamma, w_in, w_out, layer, n_layers, dwin_buf, dwout_buf):
    h, xn, g, u, act = saved
    dz = _ffn_dact(dh, w_out, layer, g, u)
    dwout_buf = _wgrad_rows("ffn_dwout", act, dh, layer, n_layers, dwout_buf, 0.5)
    dxn = _ffn_dxn(dz, w_in, layer)
    dwin_buf = _ffn_dwin(xn, dz, layer, n_layers, dwin_buf)
    dh_in, dgamma = _rmsnorm_bwd(h, gamma, dxn, dh)
    return dh_in, dgamma, dwin_buf, dwout_buf


def _loss_head(h, gamma, target):
    s, d = h.shape

    def body(hv, tv, gv):
        y, pull = jax.vjp(_rms, hv, gv)
        err = y - tv
        loss = 0.5 * jnp.sum(jnp.mean(err * err, axis=-1, keepdims=True), axis=0, keepdims=True)
        dh, dg = pull(err * (1.0 / d))
        return dh, jnp.broadcast_to(loss, (1, 128)), dg

    return _rowmap("loss_head", body, [(h, d, 0), (target, d, 0)], [gamma], [(d, d, 0, _F32)],
                   [(1, 128), (1, d)], rows=s, tm=_TM)


_ROT = _A_HEAD_DIM // 4
_ROPE_THETA = 500000.0
_A_WIDTH = _A_HEADS * _A_HEAD_DIM
_NEG = -1e30


def _rotary_tables(positions):
    inv_freq = jnp.power(jnp.float32(_ROPE_THETA), -jnp.arange(0, _ROT, 2, dtype=_F32) / _ROT)
    ang = positions.reshape(-1, 1).astype(_F32) * inv_freq
    cos, sin = jnp.cos(ang), jnp.sin(ang)
    s = ang.shape[0]
    pad = jnp.zeros((s, _A_HEAD_DIM - _ROT), _F32)
    c_head = jnp.concatenate([cos, cos, pad + 1.0], axis=1)
    s_head = jnp.concatenate([-sin, sin, pad], axis=1)
    c2, s2 = jnp.tile(c_head, (1, 2)), jnp.tile(s_head, (1, 2))
    scale = _A_HEAD_DIM ** -0.5
    tab_c = jnp.stack([scale * c2, c2, jnp.ones_like(c2)])
    tab_s = jnp.stack([scale * s2, s2, jnp.zeros_like(s2)])
    return tab_c, tab_s


def _pair_swap(z):
    lane = lax.broadcasted_iota(jnp.int32, z.shape, 1) % _A_HEAD_DIM
    half = _ROT // 2
    up = jnp.where(lane < _ROT, pltpu.roll(z, half, axis=1), 0.0)
    return jnp.where(lane < half, pltpu.roll(z, 128 - half, axis=1), up)


def _rot_apply(x, c, s, transpose):
    outs = []
    for k in range(x.shape[1] // 128):
        xs = x[:, k * 128:(k + 1) * 128]
        outs.append(c * xs + (_pair_swap(s * xs) if transpose else s * _pair_swap(xs)))
    return jnp.concatenate(outs, axis=1)


def _qkv_prep(proj, tab_c, tab_s):
    s = proj.shape[0]
    w = _A_WIDTH

    def body(x_ref, c_ref, s_ref, o_ref):
        o_ref[...] = _rot_apply(x_ref[...], c_ref[...], s_ref[...], False).astype(_BF)

    tab = pl.BlockSpec((None, _TM, 128), lambda i, j: (j, i, 0))
    blk = pl.BlockSpec((_TM, w), lambda i, j: (i, j))
    return pl.pallas_call(
        body, name="qkv_prep", grid=(s // _TM, 3), in_specs=[blk, tab, tab], out_specs=blk,
        out_shape=jax.ShapeDtypeStruct((s, 3 * w), _BF), compiler_params=_params(("arbitrary", "arbitrary")),
    )(proj, tab_c, tab_s)


def _head_masks():
    lane = lax.broadcasted_iota(jnp.int32, (_WIN_BLOCK, 128), 1)
    return [lane // _A_HEAD_DIM == hh for hh in range(2)]


def _band_masks(first_block):
    qi = lax.broadcasted_iota(jnp.int32, (_WIN_BLOCK, _WIN_BLOCK), 0)
    kj = lax.broadcasted_iota(jnp.int32, (_WIN_BLOCK, _WIN_BLOCK), 1)
    return (kj >= qi) & jnp.logical_not(first_block), kj <= qi


def _attn_fwd(qkv, dil):
    s = qkv.shape[0]
    w = _A_WIDTH
    length = s // dil
    nb = length // _WIN_BLOCK
    view = qkv.reshape(length, dil * 3 * w)

    def body(q_ref, kp_ref, kc_ref, vp_ref, vc_ref, o_ref, l_ref):
        allow_p, allow_c = _band_masks(pl.program_id(1) == 0)
        hm = _head_masks()
        for sl in range(w // 128):
            cols = slice(sl * 128, (sl + 1) * 128)
            q, kp, kc, vp, vc = q_ref[:, cols], kp_ref[:, cols], kc_ref[:, cols], vp_ref[:, cols], vc_ref[:, cols]
            o_s = jnp.zeros((_WIN_BLOCK, 128), _F32)
            l_s = jnp.zeros((_WIN_BLOCK, 128), _F32)
            for hh in range(2):
                qh = jnp.where(hm[hh], q, jnp.zeros_like(q))
                sp = jnp.where(allow_p, _dot_nt(qh, kp), _NEG)
                sc = jnp.where(allow_c, _dot_nt(qh, kc), _NEG)
                m = jnp.maximum(jnp.max(sp, axis=1, keepdims=True), jnp.max(sc, axis=1, keepdims=True))
                pp, pc = jnp.exp(sp - m), jnp.exp(sc - m)
                den = jnp.sum(pp, axis=1, keepdims=True) + jnp.sum(pc, axis=1, keepdims=True)
                oh = (_dot(pp, vp) + _dot(pc, vc)) / den
                o_s = jnp.where(hm[hh], oh, o_s)
                l_s = jnp.where(hm[hh], m + jnp.log(den), l_s)
            o_ref[:, cols] = o_s
            l_ref[:, cols] = l_s

    def at(col, prev):
        if prev:
            return pl.BlockSpec((_WIN_BLOCK, w), lambda r, n: (jnp.maximum(n - 1, 0), 3 * r + col))
        return pl.BlockSpec((_WIN_BLOCK, w), lambda r, n: (n, 3 * r + col))

    out = pl.BlockSpec((_WIN_BLOCK, w), lambda r, n: (n, r))
    sds = jax.ShapeDtypeStruct((length, dil * w), _F32)
    o, lse = pl.pallas_call(
        body, name=f"attn_fwd_d{dil}", grid=(dil, nb),
        in_specs=[at(0, False), at(1, True), at(1, False), at(2, True), at(2, False)],
        out_specs=[out, out], out_shape=[sds, sds], compiler_params=_params(("arbitrary", "arbitrary")),
    )(view, view, view, view, view)
    return o.reshape(s, w), lse.reshape(s, w)


def _attn_bwd(qkv, o, lse, do, dlse, dil):
    s = qkv.shape[0]
    w = _A_WIDTH
    length = s // dil
    nb = length // _WIN_BLOCK
    view = qkv.reshape(length, dil * 3 * w)
    o, lse, do, dlse = (t.reshape(length, dil * w) for t in (o, lse, do, dlse))

    def body(q_ref, kp_ref, kc_ref, vp_ref, vc_ref, o_ref, l_ref, do_ref, dl_ref, out_ref,
             car_q, car_k, car_v, cur_q, cur_kc, cur_vc, cur_kp, cur_vp):
        n = pl.program_id(1)

        @pl.when(n == 0)
        def _():
            car_q[...] = jnp.zeros_like(car_q)
            car_k[...] = jnp.zeros_like(car_k)
            car_v[...] = jnp.zeros_like(car_v)

        @pl.when(n == nb)
        def _():
            cur_kp[...] = jnp.zeros_like(cur_kp)
            cur_vp[...] = jnp.zeros_like(cur_vp)

        @pl.when(n < nb)
        def _():
            allow_p, allow_c = _band_masks(n == 0)
            hm = _head_masks()
            for sl in range(w // 128):
                cols = slice(sl * 128, (sl + 1) * 128)
                q, kp, kc, vp, vc = q_ref[:, cols], kp_ref[:, cols], kc_ref[:, cols], vp_ref[:, cols], vc_ref[:, cols]
                ov, lv, dov, dlv = o_ref[:, cols], l_ref[:, cols], do_ref[:, cols], dl_ref[:, cols]
                zero = jnp.zeros((_WIN_BLOCK, 128), _F32)
                dq_s, dkp_s, dkc_s, dvp_s, dvc_s = zero, zero, zero, zero, zero
                for hh in range(2):
                    qh = jnp.where(hm[hh], q, jnp.zeros_like(q))
                    doh = jnp.where(hm[hh], dov, 0.0).astype(_BF)
                    lse_h = jnp.sum(jnp.where(hm[hh], lv, 0.0), axis=1, keepdims=True) * (1.0 / _A_HEAD_DIM)
                    delta = jnp.sum(jnp.where(hm[hh], dov * ov, 0.0), axis=1, keepdims=True)
                    dl = jnp.sum(jnp.where(hm[hh], dlv, 0.0), axis=1, keepdims=True)
                    pp = jnp.exp(jnp.where(allow_p, _dot_nt(qh, kp), _NEG) - lse_h)
                    pc = jnp.exp(jnp.where(allow_c, _dot_nt(qh, kc), _NEG) - lse_h)
                    dsp = (pp * (_dot_nt(doh, vp) - delta + dl)).astype(_BF)
                    dsc = (pc * (_dot_nt(doh, vc) - delta + dl)).astype(_BF)
                    dq_s = dq_s + jnp.where(hm[hh], _dot(dsp, kp) + _dot(dsc, kc), 0.0)
                    dkp_s = dkp_s + _dot_tn(dsp, qh)
                    dkc_s = dkc_s + _dot_tn(dsc, qh)
                    dvp_s = dvp_s + _dot_tn(pp, doh)
                    dvc_s = dvc_s + _dot_tn(pc, doh)
                cur_q[:, cols] = dq_s
                cur_kp[:, cols] = dkp_s
                cur_kc[:, cols] = dkc_s
                cur_vp[:, cols] = dvp_s
                cur_vc[:, cols] = dvc_s

        out_ref[:, 0:w] = car_q[...]
        out_ref[:, w:2 * w] = car_k[...] + cur_kp[...]
        out_ref[:, 2 * w:3 * w] = car_v[...] + cur_vp[...]

        @pl.when(n < nb)
        def _():
            car_q[...] = cur_q[...]
            car_k[...] = cur_kc[...]
            car_v[...] = cur_vc[...]

    def at(col, prev):
        if prev:
            return pl.BlockSpec((_WIN_BLOCK, w), lambda r, n: (jnp.maximum(jnp.minimum(n, nb - 1) - 1, 0), 3 * r + col))
        return pl.BlockSpec((_WIN_BLOCK, w), lambda r, n: (jnp.minimum(n, nb - 1), 3 * r + col))

    blk = pl.BlockSpec((_WIN_BLOCK, w), lambda r, n: (jnp.minimum(n, nb - 1), r))
    scratch = [pltpu.VMEM((_WIN_BLOCK, w), _F32) for _ in range(8)]
    out = pl.pallas_call(
        body, name=f"attn_bwd_d{dil}", grid=(dil, nb + 1),
        in_specs=[at(0, False), at(1, True), at(1, False), at(2, True), at(2, False), blk, blk, blk, blk],
        out_specs=pl.BlockSpec((_WIN_BLOCK, 3 * w), lambda r, n: (jnp.maximum(n - 1, 0), r)),
        out_shape=jax.ShapeDtypeStruct((length, dil * 3 * w), _F32), scratch_shapes=scratch,
        compiler_params=_params(("arbitrary", "arbitrary")),
    )(view, view, view, view, view, o, lse, do, dlse)
    return out.reshape(s, 3 * w)


def _mix(o1, o2, o3, l1, l2, l3):
    m = jnp.maximum(jnp.maximum(l1, l2), l3)
    e1, e2, e3 = jnp.exp(l1 - m), jnp.exp(l2 - m), jnp.exp(l3 - m)
    return (e1 * o1 + e2 * o2 + e3 * o3) / (e1 + e2 + e3)


def _attn_mix_fwd(outs, lses):
    s = outs[0].shape[0]
    col = lambda j: j
    tiles = [(t, 128, col) for t in (*outs, *lses)]
    return _rowmap("attn_mix_fwd", _mix, tiles, [], [(2 * _A_WIDTH, 128, col, _BF)], rows=s, tm=_TM,
                   ncol=_A_WIDTH // 128)[0]


def _attn_mix_bwd(outs, lses, dy):
    s = outs[0].shape[0]
    col = lambda j: j
    tiles = [(t, 128, col) for t in (*outs, *lses, dy)]
    body = _vjp_body(_mix, 6, range(6), 1)
    return _rowmap("attn_mix_bwd", body, tiles, [], [(_A_WIDTH, 128, col, _F32)] * 6, rows=s, tm=_TM,
                   ncol=_A_WIDTH // 128)


_CONV_TM = 256
_HALO = 32


def _ln_silu(pre, g, b):
    mu = jnp.mean(pre, axis=-1, keepdims=True)
    xc = pre - mu
    y = xc * lax.rsqrt(jnp.mean(xc * xc, axis=-1, keepdims=True) + _NORM_EPS)
    return _silu(y * g + b)


def _glu_into(buf, ua_ref, ub_ref, uah_ref, ubh_ref, first):
    halo = uah_ref[...] * jax.nn.sigmoid(ubh_ref[...])
    buf[0:_HALO, :] = jnp.where(first, 0.0, halo)
    buf[_HALO:, :] = ua_ref[...] * jax.nn.sigmoid(ub_ref[...])


def _causal_taps(buf, w_ref, rows):
    base = _HALO - (_CONV_WIDTH - 1)
    acc = jnp.zeros((rows, buf.shape[1]), _F32)
    for j in range(_CONV_WIDTH):
        acc = acc + w_ref[j:j + 1, :] * buf[base + j:base + j + rows, :]
    return acc


def _conv_specs(s):
    tm = _CONV_TM
    per = tm // _HALO
    cw = _A_WIDTH
    cur = lambda col: pl.BlockSpec((tm, cw), lambda i: (i, col))
    prev = lambda col: pl.BlockSpec((_HALO, cw), lambda i: (jnp.maximum(i * per - 1, 0), col))
    return tm, per, cw, cur, prev


def _full2(arr):
    return pl.BlockSpec(arr.shape, lambda i: (0, 0))


def _conv_fwd(proj, dw_w, dw_b, ln_g, ln_b, y):
    s = proj.shape[0]
    tm, per, cw, cur, prev = _conv_specs(s)

    def body(ua_ref, ub_ref, uah_ref, ubh_ref, w_ref, b_ref, g_ref, be_ref, y_in, o_ref, buf):
        del y_in
        _glu_into(buf, ua_ref, ub_ref, uah_ref, ubh_ref, pl.program_id(0) == 0)
        pre = _causal_taps(buf, w_ref, tm) + b_ref[...]
        o_ref[...] = _ln_silu(pre, g_ref[...], be_ref[...]).astype(_BF)

    return pl.pallas_call(
        body, name="conv_fwd", grid=(s // tm,),
        in_specs=[cur(3), cur(4), prev(3), prev(4), _full2(dw_w), _full2(dw_b), _full2(ln_g), _full2(ln_b), _ANY],
        out_specs=pl.BlockSpec((tm, cw), lambda i: (i, 1)), out_shape=jax.ShapeDtypeStruct(y.shape, y.dtype),
        scratch_shapes=[pltpu.VMEM((tm + _HALO, cw), _F32)], input_output_aliases={8: 0},
        compiler_params=_params(("arbitrary",)),
    )(proj, proj, proj, proj, dw_w, dw_b, ln_g, ln_b, y)


def _conv_bwd_pre(proj, dw_w, dw_b, ln_g, ln_b, dy):
    s = proj.shape[0]
    tm, per, cw, cur, prev = _conv_specs(s)

    def body(ua_ref, ub_ref, uah_ref, ubh_ref, w_ref, b_ref, g_ref, be_ref, dy_ref, dp_ref, dg_ref, dbe_ref, db_ref, buf):
        i = pl.program_id(0)
        _glu_into(buf, ua_ref, ub_ref, uah_ref, ubh_ref, i == 0)
        pre = _causal_taps(buf, w_ref, tm) + b_ref[...]
        _, pull = jax.vjp(_ln_silu, pre, g_ref[...], be_ref[...])
        dpre, dg, dbe = pull(dy_ref[...])
        dp_ref[...] = dpre

        @pl.when(i == 0)
        def _():
            dg_ref[...] = jnp.zeros_like(dg_ref)
            dbe_ref[...] = jnp.zeros_like(dbe_ref)
            db_ref[...] = jnp.zeros_like(db_ref)

        dg_ref[...] += dg
        dbe_ref[...] += dbe
        db_ref[...] += jnp.sum(dpre, axis=0, keepdims=True)

    vec = pl.BlockSpec((1, cw), lambda i: (0, 0))
    vsd = jax.ShapeDtypeStruct((1, cw), _F32)
    return pl.pallas_call(
        body, name="conv_bwd_pre", grid=(s // tm,),
        in_specs=[cur(3), cur(4), prev(3), prev(4), _full2(dw_w), _full2(dw_b), _full2(ln_g), _full2(ln_b), cur(1)],
        out_specs=[pl.BlockSpec((tm, cw), lambda i: (i, 0)), vec, vec, vec],
        out_shape=[jax.ShapeDtypeStruct((s, cw), _F32), vsd, vsd, vsd],
        scratch_shapes=[pltpu.VMEM((tm + _HALO, cw), _F32)], compiler_params=_params(("arbitrary",)),
    )(proj, proj, proj, proj, dw_w, dw_b, ln_g, ln_b, dy)


def _conv_bwd_taps(proj, dw_w, dpre):
    s = proj.shape[0]
    tm, per, cw, cur, prev = _conv_specs(s)
    nt = s // tm

    def body(ua_ref, ub_ref, uah_ref, ubh_ref, w_ref, dp_ref, dpn_ref, dglu_ref, dw_ref, buf, dbuf):
        i = pl.program_id(0)
        _glu_into(buf, ua_ref, ub_ref, uah_ref, ubh_ref, i == 0)
        dpre = dp_ref[...]
        dbuf[0:tm, :] = dpre
        dbuf[tm:, :] = jnp.where(i == nt - 1, 0.0, dpn_ref[...])
        acc = jnp.zeros((tm, cw), _F32)
        for j in range(_CONV_WIDTH):
            off = _CONV_WIDTH - 1 - j
            acc = acc + w_ref[j:j + 1, :] * dbuf[off:off + tm, :]
        dglu_ref[...] = acc

        @pl.when(i == 0)
        def _():
            dw_ref[...] = jnp.zeros_like(dw_ref)

        base = _HALO - (_CONV_WIDTH - 1)
        for j in range(_CONV_WIDTH):
            dw_ref[j:j + 1, :] += jnp.sum(dpre * buf[base + j:base + j + tm, :], axis=0, keepdims=True)

    nxt = pl.BlockSpec((_HALO, cw), lambda i: (jnp.minimum((i + 1) * per, s // _HALO - 1), 0))
    return pl.pallas_call(
        body, name="conv_bwd_taps", grid=(nt,),
        in_specs=[cur(3), cur(4), prev(3), prev(4), _full2(dw_w), pl.BlockSpec((tm, cw), lambda i: (i, 0)), nxt],
        out_specs=[pl.BlockSpec((tm, cw), lambda i: (i, 0)), pl.BlockSpec((_HALO, cw), lambda i: (0, 0))],
        out_shape=[jax.ShapeDtypeStruct((s, cw), _F32), jax.ShapeDtypeStruct((_HALO, cw), _F32)],
        scratch_shapes=[pltpu.VMEM((tm + _HALO, cw), _F32), pltpu.VMEM((tm + _HALO, cw), _F32)],
        compiler_params=_params(("arbitrary",)),
    )(proj, proj, proj, proj, dw_w, dpre, dpre)


def _hyb_dproj(dqkv, tab_c, tab_s, dglu, proj):
    s = proj.shape[0]
    w = _A_WIDTH

    def body(d1_ref, d2_ref, d3_ref, c_ref, s_ref, dg_ref, ua_ref, ub_ref, o_ref):
        j = pl.program_id(1)

        @pl.when(j < 3)
        def _():
            x = d1_ref[...] + d2_ref[...] + d3_ref[...]
            o_ref[...] = _rot_apply(x, c_ref[...], s_ref[...], True).astype(_BF)

        @pl.when(j == 3)
        def _():
            o_ref[...] = (dg_ref[...] * jax.nn.sigmoid(ub_ref[...])).astype(_BF)

        @pl.when(j == 4)
        def _():
            sg = jax.nn.sigmoid(ub_ref[...])
            o_ref[...] = (dg_ref[...] * ua_ref[...] * sg * (1.0 - sg)).astype(_BF)

    dq = pl.BlockSpec((_TM, w), lambda i, j: (i, jnp.minimum(j, 2)))
    tab = pl.BlockSpec((None, _TM, 128), lambda i, j: (jnp.minimum(j, 2), i, 0))
    col = lambda c: pl.BlockSpec((_TM, w), lambda i, j: (i, c))
    return pl.pallas_call(
        body, name="hyb_dproj", grid=(s // _TM, 5),
        in_specs=[dq, dq, dq, tab, tab, col(0), col(3), col(4)],
        out_specs=pl.BlockSpec((_TM, w), lambda i, j: (i, j)),
        out_shape=jax.ShapeDtypeStruct((s, 5 * w), _BF), compiler_params=_params(("arbitrary", "arbitrary")),
    )(*dqkv, tab_c, tab_s, dglu, proj, proj)


def _proj_in(name, xn, w, layer):
    s, d = xn.shape
    ns = w.shape[-1]
    return _mm(name, xn, w, grid=(4, s // _TM), a_spec=pl.BlockSpec((_TM, d), lambda j, i: (i, 0)),
               b_spec=pl.BlockSpec((None, None, d, ns), lambda j, i: (layer, j, 0, 0)), dn=_NN,
               out_shape=jax.ShapeDtypeStruct((s, 4 * ns), _F32), out_spec=pl.BlockSpec((_TM, ns), lambda j, i: (i, j)))


def _proj_in_dx(name, dz, w, layer):
    s = dz.shape[0]
    d, ns = w.shape[2], w.shape[3]
    return _mm(name, dz, w, grid=(s // _TM, 4), a_spec=pl.BlockSpec((_TM, ns), lambda i, j: (i, j)),
               b_spec=pl.BlockSpec((None, None, d, ns), lambda i, j: (layer, j, 0, 0)), dn=_NT, red=1,
               acc_shape=(_TM, d), out_shape=jax.ShapeDtypeStruct((s, d), _F32),
               out_spec=pl.BlockSpec((_TM, d), lambda i, j: (i, 0)))


def _proj_in_dw(name, xn, dz, layer, n_layers, buf):
    s, d = xn.shape
    ns = dz.shape[1] // 4
    return _mm(name, xn, dz, grid=(4, s // _TM), a_spec=pl.BlockSpec((_TM, d), lambda j, i: (i, 0)),
               b_spec=pl.BlockSpec((_TM, ns), lambda j, i: (i, j)), dn=_TN, red=1, acc_shape=(d, ns),
               out_shape=jax.ShapeDtypeStruct((n_layers, 4, d, ns), _BF),
               out_spec=pl.BlockSpec((None, None, d, ns), lambda j, i: (layer, j, 0, 0)), into=buf)


def _proj_out_dy(name, dh, w, layer):
    s, d = dh.shape
    k = w.shape[1]
    return _mm(name, dh, w, grid=(s // _TM,), a_spec=pl.BlockSpec((_TM, d), lambda i: (i, 0)),
               b_spec=pl.BlockSpec((None, k, d), lambda i: (layer, 0, 0)), dn=_NT,
               out_shape=jax.ShapeDtypeStruct((s, k), _F32), out_spec=pl.BlockSpec((_TM, k), lambda i: (i, 0)))


def _hyb_fwd(h, gamma, w_in, w_out, dw_w, dw_b, ln_g, ln_b, tabs, li):
    hn = _rmsnorm_fwd(h, gamma)
    proj = _proj_in("hyb_in", hn, w_in, li)
    qkv = _qkv_prep(proj, *tabs)
    branches = [_attn_fwd(qkv, dil) for dil in _DILATIONS]
    outs, lses = [b[0] for b in branches], [b[1] for b in branches]
    y = _attn_mix_fwd(outs, lses)
    y = _conv_fwd(proj, dw_w, dw_b, ln_g, ln_b, y)
    h_new = _proj_residual("hyb_out", y, w_out, li, h, 1.0)
    return h_new, (h, hn, proj, qkv, outs, lses, y)


def _hyb_bwd(dh, saved, gamma, w_in, w_out, dw_w, dw_b, ln_g, ln_b, tabs, li, n_layers, dwin_buf, dwout_buf):
    h, hn, proj, qkv, outs, lses, y = saved
    dy = _proj_out_dy("hyb_dy", dh, w_out, li)
    dwout_buf = _wgrad_rows("hyb_dwout", y, dh, li, n_layers, dwout_buf, 1.0)
    mixg = _attn_mix_bwd(outs, lses, dy)
    dqkv = [_attn_bwd(qkv, outs[b], lses[b], mixg[b], mixg[3 + b], dil) for b, dil in enumerate(_DILATIONS)]
    dpre, d_ln_g, d_ln_b, d_dw_b = _conv_bwd_pre(proj, dw_w, dw_b, ln_g, ln_b, dy)
    dglu, d_dw_w = _conv_bwd_taps(proj, dw_w, dpre)
    dproj = _hyb_dproj(dqkv, *tabs, dglu, proj)
    dhn = _proj_in_dx("hyb_dhn", dproj, w_in, li)
    dwin_buf = _proj_in_dw("hyb_dwin", hn, dproj, li, n_layers, dwin_buf)
    dh_in, dgamma = _rmsnorm_bwd(h, gamma, dhn, dh)
    return dh_in, dgamma, (d_dw_w, d_dw_b, d_ln_g, d_ln_b), dwin_buf, dwout_buf


_GDN_BLOCK = 256


def _gdn_chunk(q, k, v, gates, state, head):
    c = q.shape[0]
    lane = lax.broadcasted_iota(jnp.int32, gates.shape, 1)
    beta = jnp.sum(jnp.where(lane == head, gates, 0.0), axis=1, keepdims=True)
    g = jnp.sum(jnp.where(lane == head + _GDN_HEADS, gates, 0.0), axis=1, keepdims=True)
    row = lax.broadcasted_iota(jnp.int32, (c, c), 0)
    col = lax.broadcasted_iota(jnp.int32, (c, c), 1)
    gc_row = jnp.sum(jnp.where(row <= col, g, 0.0), axis=0, keepdims=True)
    gc_col = jnp.sum(jnp.where(row == col, gc_row, 0.0), axis=1, keepdims=True)
    causal = row >= col
    decay = jnp.where(causal, jnp.exp(jnp.where(causal, gc_col - gc_row, 0.0)), 0.0)
    kb = k * beta
    l_mat = jnp.where(row > col, _dot_nt(kb, k) * decay, 0.0)
    eye = jnp.where(row == col, 1.0, 0.0)
    t_inv = eye - l_mat
    x = _dot_hi(l_mat, l_mat)
    for it in range(5):
        t_inv = t_inv + _dot_hi(t_inv, x)
        if it < 4:
            x = _dot_hi(x, x)
    eg = jnp.exp(gc_col)
    u = _dot(t_inv, v * beta)
    w = _dot(t_inv, kb * eg)
    attn = jnp.where(causal, _dot_nt(q, k) * decay, 0.0)
    q_dec = q * eg
    rowc = lax.broadcasted_iota(jnp.int32, (c, 1), 0)
    g_last = jnp.sum(jnp.where(rowc == c - 1, gc_col, 0.0), axis=0, keepdims=True)
    k_dec = k * jnp.exp(g_last - gc_col)
    v_new = u - _dot(w, state)
    o = _dot(q_dec, state) + _dot(attn, v_new)
    new_state = state * jnp.exp(g_last) + _dot_tn(k_dec, v_new)
    return o, new_state


def _gdn_chunk_fwd(q, k, v, gates):
    s = q.shape[0]
    cb, c = _GDN_BLOCK, _GDN_CHUNK
    nblk, per = s // cb, cb // c

    def body(q_ref, k_ref, v_ref, g_ref, o_ref, st_ref, state):
        head = pl.program_id(0)

        @pl.when(pl.program_id(1) == 0)
        def _():
            state[...] = jnp.zeros_like(state)

        for ci in range(per):
            rows = slice(ci * c, (ci + 1) * c)
            st = state[...]
            st_ref[ci] = st
            o, new = _gdn_chunk(q_ref[rows, :], k_ref[rows, :], v_ref[rows, :], g_ref[rows, :], st, head)
            o_ref[rows, :] = o
            state[...] = new

    slab = pl.BlockSpec((cb, _GDN_DIM), lambda h, i: (i, h))
    return pl.pallas_call(
        body, name="gdn_chunk_fwd", grid=(_GDN_HEADS, nblk),
        in_specs=[slab, slab, slab, pl.BlockSpec((cb, 128), lambda h, i: (i, 0))],
        out_specs=[slab, pl.BlockSpec((None, per, _GDN_DIM, _GDN_DIM), lambda h, i: (h, i, 0, 0))],
        out_shape=[jax.ShapeDtypeStruct((s, _GDN_HEADS * _GDN_DIM), _F32),
                   jax.ShapeDtypeStruct((_GDN_HEADS, s // c, _GDN_DIM, _GDN_DIM), _F32)],
        scratch_shapes=[pltpu.VMEM((_GDN_DIM, _GDN_DIM), _F32)],
        compiler_params=_params(("arbitrary", "arbitrary")),
    )(q, k, v, gates)


def _gdn_chunk_bwd(q, k, v, gates, states, do):
    s = q.shape[0]
    cb, c = _GDN_BLOCK, _GDN_CHUNK
    nblk, per = s // cb, cb // c

    def body(q_ref, k_ref, v_ref, g_ref, st_ref, do_ref, dq_ref, dk_ref, dv_ref, dg_ref, dstate):
        head = pl.program_id(0)

        @pl.when(pl.program_id(1) == 0)
        def _():
            dstate[...] = jnp.zeros_like(dstate)

        for ci in reversed(range(per)):
            rows = slice(ci * c, (ci + 1) * c)
            fn = functools.partial(_gdn_chunk, head=head)
            _, pull = jax.vjp(fn, q_ref[rows, :], k_ref[rows, :], v_ref[rows, :], g_ref[rows, :], st_ref[ci])
            dq, dk, dv, dg, dst = pull((do_ref[rows, :], dstate[...]))
            dq_ref[rows, :] = dq
            dk_ref[rows, :] = dk
            dv_ref[rows, :] = dv
            dg_ref[rows, :] = dg
            dstate[...] = dst

    slab = pl.BlockSpec((cb, _GDN_DIM), lambda h, i: (nblk - 1 - i, h))
    sds = jax.ShapeDtypeStruct((s, _GDN_HEADS * _GDN_DIM), _F32)
    return pl.pallas_call(
        body, name="gdn_chunk_bwd", grid=(_GDN_HEADS, nblk),
        in_specs=[slab, slab, slab, pl.BlockSpec((cb, 128), lambda h, i: (nblk - 1 - i, 0)),
                  pl.BlockSpec((None, per, _GDN_DIM, _GDN_DIM), lambda h, i: (h, nblk - 1 - i, 0, 0)), slab],
        out_specs=[slab, slab, slab, pl.BlockSpec((None, cb, 128), lambda h, i: (h, nblk - 1 - i, 0))],
        out_shape=[sds, sds, sds, jax.ShapeDtypeStruct((_GDN_HEADS, s, 128), _F32)],
        scratch_shapes=[pltpu.VMEM((_GDN_DIM, _GDN_DIM), _F32)],
        compiler_params=_params(("arbitrary", "arbitrary")),
    )(q, k, v, gates, states, do)


_SHALO = 8


def _swish_l2(c, normalize, scale):
    y = _silu(c)
    if normalize:
        y = y * lax.rsqrt(jnp.sum(y * y, axis=-1, keepdims=True) + _NORM_EPS) * scale
    return y


def _short_taps(buf, w_ref, rows):
    base = _SHALO - (_GDN_SHORT_CONV - 1)
    acc = jnp.zeros((rows, buf.shape[1]), _F32)
    for t in range(_GDN_SHORT_CONV):
        acc = acc + w_ref[t:t + 1, :] * buf[base + t:base + t + rows, :]
    return acc


def _gdn_conv_fwd(proj, conv_w, off, normalize, scale):
    s = proj.shape[0]
    tm = _TM
    per = tm // _SHALO

    def body(x_ref, xp_ref, w_ref, o_ref, buf):
        buf[0:_SHALO, :] = jnp.where(pl.program_id(1) == 0, 0.0, xp_ref[...])
        buf[_SHALO:, :] = x_ref[...]
        o_ref[...] = _swish_l2(_short_taps(buf, w_ref, tm), normalize, scale)

    return pl.pallas_call(
        body, name=f"gdn_conv_fwd_{off}", grid=(_GDN_HEADS, s // tm),
        in_specs=[pl.BlockSpec((tm, 128), lambda j, i: (i, off + j)),
                  pl.BlockSpec((_SHALO, 128), lambda j, i: (jnp.maximum(i * per - 1, 0), off + j)),
                  pl.BlockSpec((_GDN_SHORT_CONV, 128), lambda j, i: (0, off + j))],
        out_specs=pl.BlockSpec((tm, 128), lambda j, i: (i, j)),
        out_shape=jax.ShapeDtypeStruct((s, _GDN_HEADS * 128), _F32),
        scratch_shapes=[pltpu.VMEM((tm + _SHALO, 128), _F32)], compiler_params=_params(("arbitrary", "arbitrary")),
    )(proj, proj, conv_w)


def _gdn_conv_bwd(proj, conv_w, off, normalize, scale, dy, dproj):
    s = proj.shape[0]
    tm = _TM
    per = tm // _SHALO
    nt = s // tm
    taps = _GDN_SHORT_CONV

    def body(x_ref, xp_ref, xn_ref, w_ref, dy_ref, dyn_ref, dp_in, dx_ref, dw_ref, buf, dbuf):
        del dp_in
        i = pl.program_id(1)
        last = i == nt - 1
        buf[0:_SHALO, :] = jnp.where(i == 0, 0.0, xp_ref[...])
        buf[_SHALO:_SHALO + tm, :] = x_ref[...]
        buf[_SHALO + tm:, :] = jnp.where(last, 0.0, xn_ref[...])
        conv = _short_taps(buf, w_ref, tm + _SHALO)
        dy_ext = jnp.concatenate([dy_ref[...], jnp.where(last, 0.0, dyn_ref[...])], axis=0)
        _, pull = jax.vjp(lambda c: _swish_l2(c, normalize, scale), conv)
        dconv = pull(dy_ext)[0]
        dbuf[...] = dconv
        acc = jnp.zeros((tm, 128), _F32)
        for t in range(taps):
            o = taps - 1 - t
            acc = acc + w_ref[t:t + 1, :] * dbuf[o:o + tm, :]
        dx_ref[...] = acc.astype(dx_ref.dtype)

        @pl.when(i == 0)
        def _():
            dw_ref[...] = jnp.zeros_like(dw_ref)

        base = _SHALO - (taps - 1)
        for t in range(taps):
            dw_ref[t:t + 1, :] += jnp.sum(dbuf[0:tm, :] * buf[base + t:base + t + tm, :], axis=0, keepdims=True)

    cur = lambda col0: pl.BlockSpec((tm, 128), lambda j, i: (i, col0 + j))
    nxt = lambda col0: pl.BlockSpec((_SHALO, 128), lambda j, i: (jnp.minimum((i + 1) * per, s // _SHALO - 1), col0 + j))
    return pl.pallas_call(
        body, name=f"gdn_conv_bwd_{off}", grid=(_GDN_HEADS, nt),
        in_specs=[cur(off), pl.BlockSpec((_SHALO, 128), lambda j, i: (jnp.maximum(i * per - 1, 0), off + j)), nxt(off),
                  pl.BlockSpec((taps, 128), lambda j, i: (0, off + j)), cur(0), nxt(0), _ANY],
        out_specs=[cur(off), pl.BlockSpec((_SHALO, 128), lambda j, i: (0, j))],
        out_shape=[jax.ShapeDtypeStruct(dproj.shape, dproj.dtype), jax.ShapeDtypeStruct((_SHALO, _GDN_HEADS * 128), _F32)],
        scratch_shapes=[pltpu.VMEM((tm + 2 * _SHALO, 128), _F32), pltpu.VMEM((tm + _SHALO, 128), _F32)],
        input_output_aliases={6: 0}, compiler_params=_params(("arbitrary", "arbitrary")),
    )(proj, proj, proj, conv_w, dy, dy, dproj)


def _softplus(z):
    return jnp.maximum(z, 0.0) + jnp.log(1.0 + jnp.exp(-jnp.abs(z)))


def _gates(ba, a_lane, dt_lane):
    lane = lax.broadcasted_iota(jnp.int32, ba.shape, 1)
    g = -jnp.exp(a_lane) * _softplus(ba + dt_lane)
    return jnp.where(lane < _GDN_HEADS, jax.nn.sigmoid(ba), jnp.where(lane < 2 * _GDN_HEADS, g, 0.0))


def _gates_fwd(ba, a_lane, dt_lane):
    return _rowmap("gdn_gates_fwd", _gates, [(ba, 128, 0)], [a_lane, dt_lane], [(128, 128, 0, _F32)],
                   rows=ba.shape[0], tm=_TM)[0]


def _gates_bwd(ba, a_lane, dt_lane, dgates):
    def body(bav, dg8, av, dv):
        _, pull = jax.vjp(_gates, bav, av, dv)
        return pull(jnp.sum(dg8, axis=0))

    return _rowmap("gdn_gates_bwd", body, [(ba, 128, 0), (dgates, 128, 0)], [a_lane, dt_lane], [(128, 128, 0, _BF)],
                   [(1, 128), (1, 128)], rows=ba.shape[0], tm=_TM)


def _out_gate(o, z, g):
    return (_rms(o, g) * _silu(z)).astype(_BF)


def _out_gate_fwd(o, proj, norm_g):
    s = o.shape[0]
    return _rowmap("gdn_out_gate_fwd", _out_gate, [(o, 128, lambda j: j), (proj, 128, lambda j: 3 * _GDN_HEADS + j)],
                   [norm_g], [(o.shape[1], 128, lambda j: j, _BF)], rows=s, tm=_TM, ncol=_GDN_HEADS)[0]


def _out_gate_bwd(o, proj, norm_g, dy):
    s = o.shape[0]
    body = _vjp_body(_out_gate, 3, (0, 1, 2), 1)
    tiles = [(o, 128, lambda j: j), (proj, 128, lambda j: 3 * _GDN_HEADS + j)]
    def ordered(ov, zv, dyv, gv):
        return body(ov, zv, gv, dyv)
    return _rowmap("gdn_out_gate_bwd", ordered, tiles + [(dy, 128, lambda j: j)], [norm_g],
                   [(o.shape[1], 128, lambda j: j, _F32), (proj.shape[1], 128, lambda j: 3 * _GDN_HEADS + j, _BF)],
                   [(1, 128)], rows=s, tm=_TM, ncol=_GDN_HEADS)


def _plain_in(name, xn, w, layer, tn):
    s, d = xn.shape
    n = w.shape[2]
    return _mm(name, xn, w, grid=(n // tn, s // _TM), a_spec=pl.BlockSpec((_TM, d), lambda j, i: (i, 0)),
               b_spec=pl.BlockSpec((None, d, tn), lambda j, i: (layer, 0, j)), dn=_NN,
               out_shape=jax.ShapeDtypeStruct((s, n), _F32), out_spec=pl.BlockSpec((_TM, tn), lambda j, i: (i, j)))


def _plain_dx(name, dz, w, layer, tn, extra=None):
    s, n = dz.shape
    d = w.shape[1]
    row = pl.BlockSpec((_TM, d), lambda i, j: (i, 0))
    return _mm(name, dz, w, grid=(s // _TM, n // tn), a_spec=pl.BlockSpec((_TM, tn), lambda i, j: (i, j)),
               b_spec=pl.BlockSpec((None, d, tn), lambda i, j: (layer, 0, j)), dn=_NT, red=1, acc_shape=(_TM, d),
               out_shape=jax.ShapeDtypeStruct((s, d), _F32), out_spec=row,
               epi=(lambda acc, e: acc + e) if extra is not None else None,
               extras=(extra,) if extra is not None else (), extra_specs=(row,) if extra is not None else ())


def _plain_dw(name, xn, dz, layer, n_layers, buf, tn):
    s, d = xn.shape
    n = dz.shape[1]
    return _mm(name, xn, dz, grid=(n // tn, s // _TM), a_spec=pl.BlockSpec((_TM, d), lambda j, i: (i, 0)),
               b_spec=pl.BlockSpec((_TM, tn), lambda j, i: (i, j)), dn=_TN, red=1, acc_shape=(d, tn),
               out_shape=jax.ShapeDtypeStruct((n_layers, d, n), _BF),
               out_spec=pl.BlockSpec((None, d, tn), lambda j, i: (layer, 0, j)), into=buf)


_GDN_QK_SCALE = _GDN_DIM ** -0.5


def _gdn_fwd(h, gamma, w_qkvz, w_ba, w_out, conv_w, a_lane, dt_lane, norm_g, li):
    hn = _rmsnorm_fwd(h, gamma)
    proj = _plain_in("gdn_in", hn, w_qkvz, li, 1024)
    ba = _plain_in("gdn_in_ba", hn, w_ba, li, 128)
    q = _gdn_conv_fwd(proj, conv_w, 0, True, _GDN_QK_SCALE)
    k = _gdn_conv_fwd(proj, conv_w, _GDN_HEADS, True, 1.0)
    v = _gdn_conv_fwd(proj, conv_w, 2 * _GDN_HEADS, False, 1.0)
    gates = _gates_fwd(ba, a_lane, dt_lane)
    o, states = _gdn_chunk_fwd(q, k, v, gates)
    y = _out_gate_fwd(o, proj, norm_g)
    h_new = _proj_residual("gdn_out", y, w_out, li, h, 1.0)
    return h_new, (h, hn, proj, ba, q, k, v, gates, states, o, y)


def _gdn_bwd(dh, saved, gamma, w_qkvz, w_ba, w_out, conv_w, a_lane, dt_lane, norm_g, li, n_layers, bufs):
    h, hn, proj, ba, q, k, v, gates, states, o, y = saved
    dw_qkvz, dw_ba, dw_out = bufs
    dy = _proj_out_dy("gdn_dy", dh, w_out, li)
    dw_out = _wgrad_rows("gdn_dwout", y, dh, li, n_layers, dw_out, 1.0)
    do, dproj, d_norm_g = _out_gate_bwd(o, proj, norm_g, dy)
    dq, dk, dv, dgates = _gdn_chunk_bwd(q, k, v, gates, states, do)
    dproj, dcw_q = _gdn_conv_bwd(proj, conv_w, 0, True, _GDN_QK_SCALE, dq, dproj)
    dproj, dcw_k = _gdn_conv_bwd(proj, conv_w, _GDN_HEADS, True, 1.0, dk, dproj)
    dproj, dcw_v = _gdn_conv_bwd(proj, conv_w, 2 * _GDN_HEADS, False, 1.0, dv, dproj)
    dba, d_a, d_dt = _gates_bwd(ba, a_lane, dt_lane, dgates)
    dhn_ba = _plain_dx("gdn_dhn_ba", dba, w_ba, li, 128)
    dhn = _plain_dx("gdn_dhn", dproj, w_qkvz, li, 1024, extra=dhn_ba)
    dw_qkvz = _plain_dw("gdn_dwin", hn, dproj, li, n_layers, dw_qkvz, 1024)
    dw_ba = _plain_dw("gdn_dwin_ba", hn, dba, li, n_layers, dw_ba, 128)
    dh_in, dgamma = _rmsnorm_bwd(h, gamma, dhn, dh)
    d_conv_w = jnp.concatenate([dcw_q, dcw_k, dcw_v], axis=1)[:_GDN_SHORT_CONV]
    return dh_in, dgamma, (d_conv_w, d_a, d_dt, d_norm_g), (dw_qkvz, dw_ba, dw_out)


_MESH = pl.DeviceIdType.MESH
_ROW_TM = 256


def _place():
    x, y, c = lax.axis_index("x"), lax.axis_index("y"), lax.axis_index("c")
    chips = [(1 - x, y), (x, 1 - y), (1 - x, 1 - y)]
    return x, y, c, chips, [2 * px + py for px, py in chips]


def _remote(src, dst, send_sem, recv_sem, device):
    return pltpu.make_async_remote_copy(src_ref=src, dst_ref=dst, send_sem=send_sem, recv_sem=recv_sem,
                                        device_id=device, device_id_type=_MESH)


def _comm_call(name, body, operands, out_shape, n_sems):
    return pl.pallas_call(
        body, name=name, in_specs=[_ANY] * len(operands),
        out_specs=[_ANY] * len(out_shape), out_shape=out_shape,
        scratch_shapes=[pltpu.SemaphoreType.DMA((n_sems,)), pltpu.SemaphoreType.DMA((n_sems,)),
                        pltpu.SemaphoreType.DMA],
    )(*operands)


def _gather(name, t):
    n_l = t.shape[0]
    half = n_l // 2

    def body(t_ref, out_ref, send, recv, local_sem):
        x, y, c, chips, idx = _place()
        me = 2 * x + y
        mine, other = pl.ds(c * half, half), pl.ds((1 - c) * half, half)
        local = pltpu.make_async_copy(t_ref, out_ref.at[:, me], local_sem)
        local.start()
        sends = [_remote(t_ref.at[mine], out_ref.at[mine, me], send.at[k], recv.at[k], (*chips[k], c))
                 for k in range(3)]
        for cp in sends:
            cp.start()
        passed = []
        for k in range(3):
            landed = out_ref.at[mine, idx[k]]
            _remote(landed, landed, send.at[k], recv.at[k], (*chips[k], c)).wait_recv()
            cp = _remote(landed, landed, send.at[3 + k], recv.at[3 + k], (x, y, 1 - c))
            cp.start()
            passed.append(cp)
        for k in range(3):
            dst = out_ref.at[other, idx[k]]
            _remote(dst, dst, send.at[3 + k], recv.at[3 + k], (x, y, 1 - c)).wait_recv()
        for cp in sends + passed:
            cp.wait_send()
        local.wait()

    out = jax.ShapeDtypeStruct((n_l, 4) + t.shape[1:], t.dtype)
    return _comm_call(name, body, [t], [out], 6)[0]


def _rs_sibling(name, g):
    half = g.shape[0] // 2

    def body(g_ref, got_ref, own_ref, send, recv, local_sem):
        x, y, c, _, _ = _place()
        mine, other = pl.ds(c * half, half), pl.ds((1 - c) * half, half)
        local = pltpu.make_async_copy(g_ref.at[mine], own_ref, local_sem)
        local.start()
        cp = _remote(g_ref.at[other], got_ref, send.at[0], recv.at[0], (x, y, 1 - c))
        cp.start()
        cp.wait()
        local.wait()

    out = jax.ShapeDtypeStruct((half,) + g.shape[1:], g.dtype)
    return _comm_call(name, body, [g], [out, out], 1)


def _rs_chips(name, hsum):
    half = hsum.shape[0]

    def body(h_ref, got_ref, own_ref, send, recv, local_sem):
        x, y, c, chips, idx = _place()
        local = pltpu.make_async_copy(h_ref.at[:, 2 * x + y], own_ref, local_sem)
        local.start()
        sends = [_remote(h_ref.at[:, idx[k]], got_ref.at[k], send.at[k], recv.at[k], (*chips[k], c)) for k in range(3)]
        for cp in sends:
            cp.start()
        for cp in sends:
            cp.wait()
        local.wait()

    rest = hsum.shape[2:]
    return _comm_call(name, body, [hsum],
                      [jax.ShapeDtypeStruct((3, half) + rest, hsum.dtype), jax.ShapeDtypeStruct((half,) + rest, hsum.dtype)], 3)


def _rs_join(name, part):
    half = part.shape[0]

    def body(p_ref, out_ref, send, recv, local_sem):
        x, y, c, _, _ = _place()
        mine, other = pl.ds(c * half, half), pl.ds((1 - c) * half, half)
        local = pltpu.make_async_copy(p_ref, out_ref.at[mine], local_sem)
        local.start()
        cp = _remote(p_ref, out_ref.at[mine], send.at[0], recv.at[0], (x, y, 1 - c))
        cp.start()
        landing = out_ref.at[other]
        _remote(landing, landing, send.at[0], recv.at[0], (x, y, 1 - c)).wait_recv()
        cp.wait_send()
        local.wait()

    return _comm_call(name, body, [part], [jax.ShapeDtypeStruct((2 * half,) + part.shape[1:], part.dtype)], 1)[0]


def _rows2d(t):
    return t.reshape(-1, t.shape[-1])


def _row_tile(rows):
    tm = _ROW_TM
    while tm > 8 and rows % tm:
        tm //= 2
    return tm


def _cast_bf16(t):
    t2 = _rows2d(t)
    rows, cols = t2.shape
    out = _rowmap("cast_bf16", lambda v: v.astype(_BF), [(t2, cols, 0)], [], [(cols, cols, 0, _BF)],
                  rows=rows, tm=_row_tile(rows))[0]
    return out.reshape(t.shape)


def _add_pair(a, b):
    a2, b2 = _rows2d(a), _rows2d(b)
    rows, cols = a2.shape
    out = _rowmap("rs_add_pair", lambda u, v: u.astype(_F32) + v.astype(_F32), [(a2, cols, 0), (b2, cols, 0)], [],
                  [(cols, cols, 0, _BF)], rows=rows, tm=_row_tile(rows))[0]
    return out.reshape(a.shape)


def _add_four(own, got):
    o2 = _rows2d(own)
    g3 = got.reshape(3, -1, got.shape[-1])
    rows, cols = o2.shape

    def body(o, g):
        return ((o.astype(_F32) + g[0].astype(_F32)) + g[1].astype(_F32)) + g[2].astype(_F32)

    out = _rowmap("rs_add_four", body, [(o2, cols, 0), (g3, cols, 0)], [], [(cols, cols, 0, _F32)],
                  rows=rows, tm=_row_tile(rows))[0]
    return out.reshape(own.shape)


def _reduce_scatter(name, g):
    got, own = _rs_sibling(name + "_sib", g)
    chip_sum = _add_pair(own, got)
    got3, own1 = _rs_chips(name + "_chips", chip_sum)
    return _rs_join(name + "_join", _add_four(own1, got3))


def _all_reduce_small(pack):
    rows, cols = pack.shape

    def body(p_ref, o_ref, land, send, recv):
        x, y, c = lax.axis_index("x"), lax.axis_index("y"), lax.axis_index("c")
        me = 4 * x + 2 * y + c
        land[me] = p_ref[...]
        copies = []
        for k in range(1, 8):
            bx, by, bc = (k >> 2) & 1, (k >> 1) & 1, k & 1
            peer = (1 - x if bx else x, 1 - y if by else y, 1 - c if bc else c)
            cp = _remote(p_ref, land.at[me], send.at[k - 1], recv.at[k - 1], peer)
            cp.start()
            copies.append((cp, 4 * peer[0] + 2 * peer[1] + peer[2]))
        for k, (cp, pid) in enumerate(copies):
            _remote(p_ref, land.at[pid], send.at[k], recv.at[k], (x, y, c)).wait_recv()
        for cp, _ in copies:
            cp.wait_send()
        acc = land[0]
        for d in range(1, 8):
            acc = acc + land[d]
        o_ref[...] = acc

    vm = pl.BlockSpec(memory_space=pltpu.VMEM)
    return pl.pallas_call(
        body, name="all_reduce_small", in_specs=[vm], out_specs=vm,
        out_shape=jax.ShapeDtypeStruct((rows, cols), _F32),
        scratch_shapes=[pltpu.VMEM((8, rows, cols), _F32), pltpu.SemaphoreType.DMA((7,)), pltpu.SemaphoreType.DMA((7,))],
    )(pack)


def _adamw_math(w, g, m, v):
    m2 = _ADAM_B1 * m + (1.0 - _ADAM_B1) * g
    v2 = _ADAM_B2 * v + (1.0 - _ADAM_B2) * (g * g)
    m_hat = m2 / (1.0 - _ADAM_B1 ** _ADAM_STEP)
    v_hat = v2 / (1.0 - _ADAM_B2 ** _ADAM_STEP)
    delta = -_ADAM_LR * (m_hat / (jnp.sqrt(v_hat) + _ADAM_EPS) + _ADAM_WD * w)
    return delta, m2, v2


def _adamw(w, g, m, v):
    shape = w.shape
    w2, g2, m2, v2 = (_rows2d(t) for t in (w, g, m, v))
    rows, cols = w2.shape
    outs = _rowmap("adamw", _adamw_math, [(t, cols, 0) for t in (w2, g2, m2, v2)], [], [(cols, cols, 0, _F32)] * 3,
                   rows=rows, tm=_row_tile(rows))
    return tuple(o.reshape(shape) for o in outs)


def kernel(x, positions, ffn1_norm, ffn1_w_in, ffn1_w_out, mix_norm, ffn2_norm, ffn2_w_in, ffn2_w_out, hyb_w_in, hyb_dw_w, hyb_dw_b, hyb_ln_g, hyb_ln_b, hyb_w_out, gdn_w_in, gdn_conv_w, gdn_A_log, gdn_dt_bias, gdn_norm_g, gdn_w_out, final_norm, loss_target, m_ffn1_norm, m_ffn1_w_in, m_ffn1_w_out, m_mix_norm, m_ffn2_norm, m_ffn2_w_in, m_ffn2_w_out, m_hyb_w_in, m_hyb_dw_w, m_hyb_dw_b, m_hyb_ln_g, m_hyb_ln_b, m_hyb_w_out, m_gdn_w_in, m_gdn_conv_w, m_gdn_A_log, m_gdn_dt_bias, m_gdn_norm_g, m_gdn_w_out, m_final_norm, v_ffn1_norm, v_ffn1_w_in, v_ffn1_w_out, v_mix_norm, v_ffn2_norm, v_ffn2_w_in, v_ffn2_w_out, v_hyb_w_in, v_hyb_dw_w, v_hyb_dw_b, v_hyb_ln_g, v_hyb_ln_b, v_hyb_w_out, v_gdn_w_in, v_gdn_conv_w, v_gdn_A_log, v_gdn_dt_bias, v_gdn_norm_g, v_gdn_w_out, v_final_norm):
    w = dict(ffn1_norm=ffn1_norm, ffn1_w_in=ffn1_w_in, ffn1_w_out=ffn1_w_out, mix_norm=mix_norm, ffn2_norm=ffn2_norm, ffn2_w_in=ffn2_w_in, ffn2_w_out=ffn2_w_out, hyb_w_in=hyb_w_in, hyb_dw_w=hyb_dw_w, hyb_dw_b=hyb_dw_b, hyb_ln_g=hyb_ln_g, hyb_ln_b=hyb_ln_b, hyb_w_out=hyb_w_out, gdn_w_in=gdn_w_in, gdn_conv_w=gdn_conv_w, gdn_A_log=gdn_A_log, gdn_dt_bias=gdn_dt_bias, gdn_norm_g=gdn_norm_g, gdn_w_out=gdn_w_out, final_norm=final_norm)
    m = dict(ffn1_norm=m_ffn1_norm, ffn1_w_in=m_ffn1_w_in, ffn1_w_out=m_ffn1_w_out, mix_norm=m_mix_norm, ffn2_norm=m_ffn2_norm, ffn2_w_in=m_ffn2_w_in, ffn2_w_out=m_ffn2_w_out, hyb_w_in=m_hyb_w_in, hyb_dw_w=m_hyb_dw_w, hyb_dw_b=m_hyb_dw_b, hyb_ln_g=m_hyb_ln_g, hyb_ln_b=m_hyb_ln_b, hyb_w_out=m_hyb_w_out, gdn_w_in=m_gdn_w_in, gdn_conv_w=m_gdn_conv_w, gdn_A_log=m_gdn_A_log, gdn_dt_bias=m_gdn_dt_bias, gdn_norm_g=m_gdn_norm_g, gdn_w_out=m_gdn_w_out, final_norm=m_final_norm)
    v = dict(ffn1_norm=v_ffn1_norm, ffn1_w_in=v_ffn1_w_in, ffn1_w_out=v_ffn1_w_out, mix_norm=v_mix_norm, ffn2_norm=v_ffn2_norm, ffn2_w_in=v_ffn2_w_in, ffn2_w_out=v_ffn2_w_out, hyb_w_in=v_hyb_w_in, hyb_dw_w=v_hyb_dw_w, hyb_dw_b=v_hyb_dw_b, hyb_ln_g=v_hyb_ln_g, hyb_ln_b=v_hyb_ln_b, hyb_w_out=v_hyb_w_out, gdn_w_in=v_gdn_w_in, gdn_conv_w=v_gdn_conv_w, gdn_A_log=v_gdn_A_log, gdn_dt_bias=v_gdn_dt_bias, gdn_norm_g=v_gdn_norm_g, gdn_w_out=v_gdn_w_out, final_norm=v_final_norm)
    return _train_step(x, positions, loss_target, w, m, v)


_WEIGHT_ORDER = ("ffn1_norm", "ffn1_w_in", "ffn1_w_out", "mix_norm", "ffn2_norm", "ffn2_w_in", "ffn2_w_out",
                 "hyb_w_in", "hyb_dw_w", "hyb_dw_b", "hyb_ln_g", "hyb_ln_b", "hyb_w_out", "gdn_w_in", "gdn_conv_w",
                 "gdn_A_log", "gdn_dt_bias", "gdn_norm_g", "gdn_w_out", "final_norm")
_BIG = ("ffn1_w_in", "ffn1_w_out", "ffn2_w_in", "ffn2_w_out", "hyb_w_in", "hyb_w_out", "gdn_w_in", "gdn_w_out")
_PACK_COLS = 1024


def _lane16(vec):
    return jnp.zeros((1, 128), _F32).at[0, _GDN_HEADS:2 * _GDN_HEADS].set(vec)


def _pack_rows(parts):
    rows, where, at = [], [], 0
    for p in parts:
        flat = p.reshape(-1).astype(_F32)
        n = -(-flat.shape[0] // _PACK_COLS)
        rows.append(jnp.pad(flat, (0, n * _PACK_COLS - flat.shape[0])).reshape(n, _PACK_COLS))
        where.append((at, n, flat.shape[0], p.shape))
        at += n
    pad = -at % 8
    if pad:
        rows.append(jnp.zeros((pad, _PACK_COLS), _F32))
    return jnp.concatenate(rows, axis=0), where


def _unpack_rows(pack, where):
    return [pack[at:at + n].reshape(-1)[:size].reshape(shape) for at, n, size, shape in where]


def _train_step(x, positions, loss_target, w, m, v):
    n_layers = w["ffn1_norm"].shape[0]
    n_hyb, n_gdn = w["hyb_w_in"].shape[0], w["gdn_w_in"].shape[0]
    d_model = x.shape[-1]
    xh, target = x[0], loss_target[0]
    chip = 2 * lax.axis_index("x") + lax.axis_index("y")

    full = {name: _gather("gather_" + name, _cast_bf16(w[name])) for name in _BIG}
    ffn_in = {1: full["ffn1_w_in"], 2: full["ffn2_w_in"]}
    ffn_out = {k: full[f"ffn{k}_w_out"].reshape(n_layers, -1, d_model) for k in (1, 2)}
    hyb_in, hyb_out = full["hyb_w_in"], full["hyb_w_out"].reshape(n_hyb, -1, d_model)
    gdn_out = full["gdn_w_out"].reshape(n_gdn, -1, d_model)
    gdn_cols = full["gdn_w_in"].shape[-1]
    gdn_in = jnp.transpose(full["gdn_w_in"], (0, 2, 1, 3)).reshape(n_gdn, d_model, 4 * gdn_cols)
    n_qkvz = 4 * _GDN_HEADS * _GDN_DIM
    n_ba = 4 * gdn_cols - n_qkvz
    gdn_qkvz = gdn_in[:, :, :n_qkvz]
    gdn_ba = jnp.pad(gdn_in[:, :, n_qkvz:], ((0, 0), (0, 0), (0, 128 - n_ba)))
    dw_w = _gather("gather_hyb_dw_w", jnp.pad(w["hyb_dw_w"], ((0, 0), (0, _HALO - _CONV_WIDTH), (0, 0))))
    dw_w = jnp.transpose(dw_w, (0, 2, 1, 3)).reshape(n_hyb, _HALO, -1)
    conv_w = _gather("gather_gdn_conv_w", jnp.pad(w["gdn_conv_w"], ((0, 0), (0, _SHALO - _GDN_SHORT_CONV), (0, 0))))
    conv_w = jnp.transpose(conv_w, (0, 2, 1, 3)).reshape(n_gdn, _SHALO, -1)[:, :_GDN_SHORT_CONV]
    tabs = _rotary_tables(positions)

    def mixer_args(layer):
        i = layer // 2
        if layer % 2 == 0:
            return (hyb_in, hyb_out, dw_w[i], w["hyb_dw_b"][i][None], w["hyb_ln_g"][i][None], w["hyb_ln_b"][i][None], tabs, i)
        return (gdn_qkvz, gdn_ba, gdn_out, conv_w[i], _lane16(w["gdn_A_log"][i]), _lane16(w["gdn_dt_bias"][i]),
                w["gdn_norm_g"][i][None], i)

    h = xh
    saved = []
    for layer in range(n_layers):
        h, s1 = _ffn_fwd(h, w["ffn1_norm"][layer][None], ffn_in[1], ffn_out[1], layer)
        gamma = w["mix_norm"][layer][None]
        if layer % 2 == 0:
            h, sm = _hyb_fwd(h, gamma, *mixer_args(layer))
        else:
            h, sm = _gdn_fwd(h, gamma, *mixer_args(layer))
        h, s2 = _ffn_fwd(h, w["ffn2_norm"][layer][None], ffn_in[2], ffn_out[2], layer)
        saved.append((s1, sm, s2))
    dh, loss_row, d_final = _loss_head(h, w["final_norm"][None], target)

    bufs = dict(ffn1_w_in=None, ffn1_w_out=None, ffn2_w_in=None, ffn2_w_out=None, hyb_w_in=None, hyb_w_out=None,
                gdn_qkvz=None, gdn_ba=None, gdn_w_out=None)
    d_norm = {k: [None] * n_layers for k in ("ffn1_norm", "mix_norm", "ffn2_norm")}
    d_hyb = [None] * n_hyb
    d_gdn = [None] * n_gdn
    for layer in reversed(range(n_layers)):
        s1, sm, s2 = saved[layer]
        dh, d_norm["ffn2_norm"][layer], bufs["ffn2_w_in"], bufs["ffn2_w_out"] = _ffn_bwd(
            dh, s2, w["ffn2_norm"][layer][None], ffn_in[2], ffn_out[2], layer, n_layers, bufs["ffn2_w_in"], bufs["ffn2_w_out"])
        gamma = w["mix_norm"][layer][None]
        if layer % 2 == 0:
            args = mixer_args(layer)
            dh, d_norm["mix_norm"][layer], d_hyb[layer // 2], bufs["hyb_w_in"], bufs["hyb_w_out"] = _hyb_bwd(
                dh, sm, gamma, *args, n_hyb, bufs["hyb_w_in"], bufs["hyb_w_out"])
        else:
            args = mixer_args(layer)
            dh, d_norm["mix_norm"][layer], d_gdn[layer // 2], (bufs["gdn_qkvz"], bufs["gdn_ba"], bufs["gdn_w_out"]) = _gdn_bwd(
                dh, sm, gamma, *args, n_gdn, (bufs["gdn_qkvz"], bufs["gdn_ba"], bufs["gdn_w_out"]))
        dh, d_norm["ffn1_norm"][layer], bufs["ffn1_w_in"], bufs["ffn1_w_out"] = _ffn_bwd(
            dh, s1, w["ffn1_norm"][layer][None], ffn_in[1], ffn_out[1], layer, n_layers, bufs["ffn1_w_in"], bufs["ffn1_w_out"])
    grad_x = dh[None]

    d_gdn_in = jnp.concatenate([bufs["gdn_qkvz"], bufs["gdn_ba"][:, :, :n_ba]], axis=2)
    d_gdn_in = jnp.transpose(d_gdn_in.reshape(n_gdn, d_model, 4, gdn_cols), (0, 2, 1, 3))
    full_grads = dict(
        ffn1_w_in=bufs["ffn1_w_in"], ffn2_w_in=bufs["ffn2_w_in"], hyb_w_in=bufs["hyb_w_in"], gdn_w_in=d_gdn_in,
        ffn1_w_out=bufs["ffn1_w_out"].reshape(n_layers, 4, -1, d_model),
        ffn2_w_out=bufs["ffn2_w_out"].reshape(n_layers, 4, -1, d_model),
        hyb_w_out=bufs["hyb_w_out"].reshape(n_hyb, 4, -1, d_model),
        gdn_w_out=bufs["gdn_w_out"].reshape(n_gdn, 4, -1, d_model))
    grads = {name: _reduce_scatter("rs_" + name, full_grads[name]) for name in _BIG}

    stack = lambda rows: jnp.concatenate(rows, axis=0)
    misc = jnp.concatenate([stack([g[1] for g in d_gdn])[:, _GDN_HEADS:2 * _GDN_HEADS].reshape(-1),
                            stack([g[2] for g in d_gdn])[:, _GDN_HEADS:2 * _GDN_HEADS].reshape(-1),
                            stack([g[3] for g in d_gdn]).reshape(-1), loss_row[0, :1]])
    n_a = n_gdn * _GDN_HEADS
    parts = [stack(d_norm["ffn1_norm"]), stack(d_norm["mix_norm"]), stack(d_norm["ffn2_norm"]), d_final,
             stack([g[1] for g in d_hyb]), stack([g[2] for g in d_hyb]), stack([g[3] for g in d_hyb]), misc,
             jnp.stack([g[0][:_CONV_WIDTH] for g in d_hyb]), jnp.stack([g[0] for g in d_gdn])]
    pack, where = _pack_rows(parts)
    red = _unpack_rows(_all_reduce_small(pack), where)
    misc = red[7]
    cw = w["hyb_dw_w"].shape[-1]
    gw = w["gdn_conv_w"].shape[-1]
    grads.update(
        ffn1_norm=red[0], mix_norm=red[1], ffn2_norm=red[2], final_norm=red[3].reshape(-1),
        hyb_dw_b=red[4], hyb_ln_g=red[5], hyb_ln_b=red[6],
        gdn_A_log=misc[:n_a].reshape(n_gdn, _GDN_HEADS), gdn_dt_bias=misc[n_a:2 * n_a].reshape(n_gdn, _GDN_HEADS),
        gdn_norm_g=misc[2 * n_a:2 * n_a + n_gdn * _GDN_DIM].reshape(n_gdn, _GDN_DIM),
        hyb_dw_w=lax.dynamic_slice_in_dim(red[8], chip * cw, cw, axis=2),
        gdn_conv_w=lax.dynamic_slice_in_dim(red[9], chip * gw, gw, axis=2))
    loss = misc[2 * n_a + n_gdn * _GDN_DIM]

    delta, new_m, new_v = {}, {}, {}
    for name in _BIG:
        delta[name], new_m[name], new_v[name] = _adamw(w[name], grads[name], m[name], v[name])
    small = [n for n in _WEIGHT_ORDER if n not in _BIG]
    packs = [_pack_rows([src[n] for n in small]) for src in (w, grads, m, v)]
    outs = _adamw(*[p[0] for p in packs])
    for res, o in zip((delta, new_m, new_v), outs):
        res.update(dict(zip(small, _unpack_rows(o, packs[0][1]))))
    return (loss, grad_x, *[grads[n] for n in _WEIGHT_ORDER], *[delta[n] for n in _WEIGHT_ORDER],
            *[new_m[n] for n in _WEIGHT_ORDER], *[new_v[n] for n in _WEIGHT_ORDER])
```

```python
import functools

import jax
import jax.numpy as jnp
from jax import lax
from jax.experimental import pallas as pl
from jax.experimental.pallas import tpu as pltpu

_BF = jnp.bfloat16
_F32 = jnp.float32
_NORM_EPS = 1e-6
_VMEM_LIMIT_BYTES = 56 * 1024 * 1024

_A_HEADS = 8
_A_HEAD_DIM = 64
_WIN_BLOCK = 128
_DILATIONS = (1, 4, 16)
_CONV_WIDTH = 31
_GDN_HEADS = 8
_GDN_DIM = 128
_GDN_CHUNK = 64
_GDN_SHORT_CONV = 4

_ADAM_LR = 0.001
_ADAM_B1 = 0.9
_ADAM_B2 = 0.999
_ADAM_EPS = 1e-08
_ADAM_WD = 0.01
_ADAM_STEP = 10


def _params(sem):
    return pltpu.CompilerParams(dimension_semantics=sem, vmem_limit_bytes=_VMEM_LIMIT_BYTES)


def _dot(a, b, dn=(((1,), (0,)), ((), ()))):
    return lax.dot_general(a.astype(_BF), b.astype(_BF), dn, preferred_element_type=_F32)


def _dot_nt(a, b):
    return _dot(a, b, (((1,), (1,)), ((), ())))


def _dot_tn(a, b):
    return _dot(a, b, (((0,), (0,)), ((), ())))


def _dot_hi(a, b):
    ah, bh = a.astype(_BF), b.astype(_BF)
    al, bl = (a - ah.astype(_F32)).astype(_BF), (b - bh.astype(_F32)).astype(_BF)
    dot = lambda u, w: jnp.dot(u, w, preferred_element_type=_F32)
    return dot(ah, bh) + (dot(ah, bl) + dot(al, bh))


_NN = (((1,), (0,)), ((), ()))
_NT = (((1,), (1,)), ((), ()))
_TN = (((0,), (0,)), ((), ()))
_ANY = pl.BlockSpec(memory_space=pl.ANY)


def _mm(name, a, b, *, grid, a_spec, b_spec, dn, out_shape, out_spec, red=None, acc_shape=None,
        pre_a=None, pre_b=None, epi=None, extras=(), extra_specs=(), into=None):
    n_ex = len(extras)
    n_in = 2 + n_ex + (1 if into is not None else 0)
    nk = grid[red] if red is not None else 1

    def body(*refs):
        a_ref, b_ref = refs[0], refs[1]
        ex = refs[2:2 + n_ex]
        o_ref = refs[n_in]
        av, bv = a_ref[...], b_ref[...]
        if pre_a is not None:
            av = pre_a(av)
        if pre_b is not None:
            bv = pre_b(bv)
        part = lax.dot_general(av.astype(_BF), bv.astype(_BF), dn, preferred_element_type=_F32)

        def finish(acc):
            res = epi(acc, *[e[...] for e in ex]) if epi is not None else acc
            if isinstance(res, tuple):
                for idx, val in enumerate(res):
                    o_ref[idx] = val.astype(o_ref.dtype)
            else:
                o_ref[...] = res.astype(o_ref.dtype)

        if red is None:
            finish(part)
        else:
            acc_ref = refs[n_in + 1]
            k = pl.program_id(red)

            @pl.when(k == 0)
            def _():
                acc_ref[...] = part

            @pl.when(k > 0)
            def _():
                acc_ref[...] += part

            @pl.when(k == nk - 1)
            def _():
                finish(acc_ref[...])

    in_specs = [a_spec, b_spec, *extra_specs]
    operands = [a, b, *extras]
    aliases = {}
    if into is not None:
        in_specs.append(_ANY)
        operands.append(into)
        aliases = {n_in - 1: 0}
    return pl.pallas_call(
        body, name=name, grid=grid, in_specs=in_specs, out_specs=out_spec, out_shape=out_shape,
        scratch_shapes=[pltpu.VMEM(acc_shape, _F32)] if red is not None else [],
        input_output_aliases=aliases,
        compiler_params=_params(("arbitrary",) * len(grid)),
    )(*operands)


def _rowmap(name, body, tiled, params, tile_outs, param_outs=(), *, rows, tm, ncol=1, col_outer=False,
            into=None):
    nt, npar, nto, npo = len(tiled), len(params), len(tile_outs), len(param_outs)
    assert rows % tm == 0, (name, rows, tm)
    nrow = rows // tm
    grid = (ncol, nrow) if col_outer else (nrow, ncol)

    def ij(g):
        return (g[1], g[0]) if col_outer else (g[0], g[1])

    def cidx(col, j):
        return col(j) if callable(col) else col

    in_specs, operands = [], []
    for arr, width, col in tiled:
        operands.append(arr)
        if arr.ndim == 3:
            in_specs.append(pl.BlockSpec((arr.shape[0], tm, width),
                                         lambda *g, col=col: (0, ij(g)[0], cidx(col, ij(g)[1]))))
        else:
            in_specs.append(pl.BlockSpec((tm, width), lambda *g, col=col: (ij(g)[0], cidx(col, ij(g)[1]))))
    for p in params:
        if isinstance(p, tuple):
            arr, width, col = p
            operands.append(arr)
            in_specs.append(pl.BlockSpec((arr.shape[0], width), lambda *g, col=col: (0, cidx(col, ij(g)[1]))))
        else:
            operands.append(p)
            in_specs.append(pl.BlockSpec(p.shape, lambda *g, nd=p.ndim: (0,) * nd))
    out_shape, out_specs = [], []
    for total, width, col, dt in tile_outs:
        out_shape.append(jax.ShapeDtypeStruct((rows, total), dt))
        out_specs.append(pl.BlockSpec((tm, width), lambda *g, col=col: (ij(g)[0], cidx(col, ij(g)[1]))))
    for shp in param_outs:
        out_shape.append(jax.ShapeDtypeStruct(shp, _F32))
        out_specs.append(pl.BlockSpec(shp, lambda *g, nd=len(shp): (0,) * nd))
    n_in = nt + npar
    aliases = {}
    if into is not None:
        operands.append(into[0])
        in_specs.append(_ANY)
        aliases = {n_in: into[1]}
        n_in += 1

    def kern(*refs):
        vals = [r[...] for r in refs[:nt + npar]]
        touts = refs[n_in:n_in + nto]
        pouts = refs[n_in + nto:]
        res = body(*vals)
        if not isinstance(res, (tuple, list)):
            res = (res,)
        for r, val in zip(touts, res[:nto]):
            r[...] = val.astype(r.dtype)
        if npo:
            @pl.when((pl.program_id(0) == 0) & (pl.program_id(1) == 0))
            def _():
                for r in pouts:
                    r[...] = jnp.zeros_like(r)

            for r, val in zip(pouts, res[nto:]):
                r[...] += val

    return pl.pallas_call(
        kern, name=name, grid=grid, in_specs=in_specs, out_specs=out_specs, out_shape=out_shape,
        input_output_aliases=aliases, compiler_params=_params(("arbitrary", "arbitrary")),
    )(*operands)


def _vjp_body(fn, n_in, diff, n_ct):
    def body(*vals):
        ins, cts = vals[:n_in], vals[n_in:n_in + n_ct]
        outs, pull = jax.vjp(fn, *ins)
        single = not isinstance(outs, (tuple, list))
        if single:
            grads = pull(cts[0].astype(outs.dtype))
        else:
            grads = pull(tuple(c.astype(o.dtype) for c, o in zip(cts, outs)))
        return tuple(grads[i] for i in diff)
    return body


_TM = 512


def _rms(h, gamma):
    hf = h.astype(_F32)
    return hf * lax.rsqrt(jnp.mean(hf * hf, axis=-1, keepdims=True) + _NORM_EPS) * gamma


def _rmsnorm_fwd(h, gamma):
    s, d = h.shape
    return _rowmap("rmsnorm_fwd", lambda hv, gv: _rms(hv, gv).astype(_BF), [(h, d, 0)], [gamma],
                   [(d, d, 0, _BF)], rows=s, tm=_TM)[0]


def _rmsnorm_bwd(h, gamma, dxn, dh_out):
    s, d = h.shape

    def body(hv, dxv, dhv, gv):
        _, pull = jax.vjp(_rms, hv, gv)
        dh, dg = pull(dxv.astype(_F32))
        return dhv + dh, dg

    return _rowmap("rmsnorm_bwd", body, [(h, d, 0), (dxn, d, 0), (dh_out, d, 0)], [gamma],
                   [(d, d, 0, _F32)], [(1, d)], rows=s, tm=_TM)


def _silu(x):
    return x * jax.nn.sigmoid(x)


def _ffn_up(xn, w_in, layer):
    s, d = xn.shape
    ns = w_in.shape[-1]

    def body(x_ref, wg_ref, wu_ref, g_ref, u_ref, a_ref):
        xv = x_ref[...]
        g = jnp.dot(xv, wg_ref[...], preferred_element_type=_F32)
        u = jnp.dot(xv, wu_ref[...], preferred_element_type=_F32)
        g_ref[...] = g.astype(_BF)
        u_ref[...] = u.astype(_BF)
        a_ref[...] = (_silu(g) * u).astype(_BF)

    out = pl.BlockSpec((_TM, ns), lambda j, i: (i, j))
    sds = jax.ShapeDtypeStruct((s, 2 * ns), _BF)
    return pl.pallas_call(
        body, name="ffn_up", grid=(2, s // _TM),
        in_specs=[pl.BlockSpec((_TM, d), lambda j, i: (i, 0)),
                  pl.BlockSpec((None, None, d, ns), lambda j, i: (layer, j, 0, 0)),
                  pl.BlockSpec((None, None, d, ns), lambda j, i: (layer, j + 2, 0, 0))],
        out_specs=[out, out, out], out_shape=[sds, sds, sds],
        compiler_params=_params(("arbitrary", "arbitrary")),
    )(xn, w_in, w_in)


def _proj_residual(name, y, w, layer, h, scale):
    s, k = y.shape
    d = w.shape[-1]
    return _mm(name, y, w, grid=(s // _TM,), a_spec=pl.BlockSpec((_TM, k), lambda i: (i, 0)),
               b_spec=pl.BlockSpec((None, k, d), lambda i: (layer, 0, 0)), dn=_NN,
               out_shape=jax.ShapeDtypeStruct((s, d), _F32), out_spec=pl.BlockSpec((_TM, d), lambda i: (i, 0)),
               epi=lambda acc, hv: hv + scale * acc, extras=(h,),
               extra_specs=(pl.BlockSpec((_TM, d), lambda i: (i, 0)),))


def _ffn_dact(dh, w_out, layer, g, u):
    s, d = dh.shape
    f = g.shape[1]
    ns = f // 2

    def epi(da, gv, uv):
        gf, uf = gv.astype(_F32), uv.astype(_F32)
        sg = jax.nn.sigmoid(gf)
        return da * uf * (sg * (1.0 + gf * (1.0 - sg))), da * gf * sg

    gu = pl.BlockSpec((_TM, ns), lambda n, i: (i, n))
    return _mm("ffn_dact", dh, w_out, grid=(2, s // _TM), a_spec=pl.BlockSpec((_TM, d), lambda n, i: (i, 0)),
               b_spec=pl.BlockSpec((None, ns, d), lambda n, i: (layer, n, 0)), dn=_NT,
               out_shape=jax.ShapeDtypeStruct((2, s, f), _BF), out_spec=pl.BlockSpec((2, _TM, ns), lambda n, i: (0, i, n)),
               pre_a=lambda v: 0.5 * v, epi=epi, extras=(g, u), extra_specs=(gu, gu))


def _wgrad_rows(name, y, dh, layer, n_layers, buf, scale):
    s, k = y.shape
    d = dh.shape[1]
    tk = k // 2 if (k // 2) % 128 == 0 else k
    return _mm(name, y, dh, grid=(k // tk, s // _TM), a_spec=pl.BlockSpec((_TM, tk), lambda n, i: (i, n)),
               b_spec=pl.BlockSpec((_TM, d), lambda n, i: (i, 0)), dn=_TN, red=1, acc_shape=(tk, d),
               out_shape=jax.ShapeDtypeStruct((n_layers, k, d), _BF),
               out_spec=pl.BlockSpec((None, tk, d), lambda n, i: (layer, n, 0)),
               pre_b=(lambda v: scale * v) if scale != 1.0 else None, into=buf)


def _ffn_dxn(dz, w_in, layer):
    _, s, f = dz.shape
    d, ns = w_in.shape[2], w_in.shape[3]
    return _mm("ffn_dxn", dz, w_in, grid=(s // _TM, 4),
               a_spec=pl.BlockSpec((None, _TM, ns), lambda i, j: (j // 2, i, j % 2)),
               b_spec=pl.BlockSpec((None, None, d, ns), lambda i, j: (layer, j, 0, 0)), dn=_NT, red=1,
               acc_shape=(_TM, d), out_shape=jax.ShapeDtypeStruct((s, d), _F32),
               out_spec=pl.BlockSpec((_TM, d), lambda i, j: (i, 0)))


def _ffn_dwin(xn, dz, layer, n_layers, buf):
    s, d = xn.shape
    ns = dz.shape[2] // 2
    return _mm("ffn_dwin", xn, dz, grid=(4, s // _TM), a_spec=pl.BlockSpec((_TM, d), lambda j, i: (i, 0)),
               b_spec=pl.BlockSpec((None, _TM, ns), lambda j, i: (j // 2, i, j % 2)), dn=_TN, red=1,
               acc_shape=(d, ns), out_shape=jax.ShapeDtypeStruct((n_layers, 4, d, ns), _BF),
               out_spec=pl.BlockSpec((None, None, d, ns), lambda j, i: (layer, j, 0, 0)), into=buf)


def _ffn_fwd(h, gamma, w_in, w_out, layer):
    xn = _rmsnorm_fwd(h, gamma)
    g, u, act = _ffn_up(xn, w_in, layer)
    h_new = _proj_residual("ffn_down", act, w_out, layer, h, 0.5)
    return h_new, (h, xn, g, u, act)


def _ffn_bwd(dh, saved, gamma, w_in, w_out, layer, n_layers, dwin_buf, dwout_buf):
    h, xn, g, u, act = saved
    dz = _ffn_dact(dh, w_out, layer, g, u)
    dwout_buf = _wgrad_rows("ffn_dwout", act, dh, layer, n_layers, dwout_buf, 0.5)
    dxn = _ffn_dxn(dz, w_in, layer)
    dwin_buf = _ffn_dwin(xn, dz, layer, n_layers, dwin_buf)
    dh_in, dgamma = _rmsnorm_bwd(h, gamma, dxn, dh)
    return dh_in, dgamma, dwin_buf, dwout_buf


def _loss_head(h, gamma, target):
    s, d = h.shape

    def body(hv, tv, gv):
        y, pull = jax.vjp(_rms, hv, gv)
        err = y - tv
        loss = 0.5 * jnp.sum(jnp.mean(err * err, axis=-1, keepdims=True), axis=0, keepdims=True)
        dh, dg = pull(err * (1.0 / d))
        return dh, jnp.broadcast_to(loss, (1, 128)), dg

    return _rowmap("loss_head", body, [(h, d, 0), (target, d, 0)], [gamma], [(d, d, 0, _F32)],
                   [(1, 128), (1, d)], rows=s, tm=_TM)


_ROT = _A_HEAD_DIM // 4
_ROPE_THETA = 500000.0
_A_WIDTH = _A_HEADS * _A_HEAD_DIM
_NEG = -1e30


def _rotary_tables(positions):
    inv_freq = jnp.power(jnp.float32(_ROPE_THETA), -jnp.arange(0, _ROT, 2, dtype=_F32) / _ROT)
    ang = positions.reshape(-1, 1).astype(_F32) * inv_freq
    cos, sin = jnp.cos(ang), jnp.sin(ang)
    s = ang.shape[0]
    pad = jnp.zeros((s, _A_HEAD_DIM - _ROT), _F32)
    c_head = jnp.concatenate([cos, cos, pad + 1.0], axis=1)
    s_head = jnp.concatenate([-sin, sin, pad], axis=1)
    c2, s2 = jnp.tile(c_head, (1, 2)), jnp.tile(s_head, (1, 2))
    scale = _A_HEAD_DIM ** -0.5
    tab_c = jnp.stack([scale * c2, c2, jnp.ones_like(c2)])
    tab_s = jnp.stack([scale * s2, s2, jnp.zeros_like(s2)])
    return tab_c, tab_s


def _pair_swap(z):
    lane = lax.broadcasted_iota(jnp.int32, z.shape, 1) % _A_HEAD_DIM
    half = _ROT // 2
    up = jnp.where(lane < _ROT, pltpu.roll(z, half, axis=1), 0.0)
    return jnp.where(lane < half, pltpu.roll(z, 128 - half, axis=1), up)


def _rot_apply(x, c, s, transpose):
    outs = []
    for k in range(x.shape[1] // 128):
        xs = x[:, k * 128:(k + 1) * 128]
        outs.append(c * xs + (_pair_swap(s * xs) if transpose else s * _pair_swap(xs)))
    return jnp.concatenate(outs, axis=1)


def _qkv_prep(proj, tab_c, tab_s):
    s = proj.shape[0]
    w = _A_WIDTH

    def body(x_ref, c_ref, s_ref, o_ref):
        o_ref[...] = _rot_apply(x_ref[...], c_ref[...], s_ref[...], False).astype(_BF)

    tab = pl.BlockSpec((None, _TM, 128), lambda i, j: (j, i, 0))
    blk = pl.BlockSpec((_TM, w), lambda i, j: (i, j))
    return pl.pallas_call(
        body, name="qkv_prep", grid=(s // _TM, 3), in_specs=[blk, tab, tab], out_specs=blk,
        out_shape=jax.ShapeDtypeStruct((s, 3 * w), _BF), compiler_params=_params(("arbitrary", "arbitrary")),
    )(proj, tab_c, tab_s)


def _head_masks():
    lane = lax.broadcasted_iota(jnp.int32, (_WIN_BLOCK, 128), 1)
    return [lane // _A_HEAD_DIM == hh for hh in range(2)]


def _band_masks(first_block):
    qi = lax.broadcasted_iota(jnp.int32, (_WIN_BLOCK, _WIN_BLOCK), 0)
    kj = lax.broadcasted_iota(jnp.int32, (_WIN_BLOCK, _WIN_BLOCK), 1)
    return (kj >= qi) & jnp.logical_not(first_block), kj <= qi


def _attn_fwd(qkv, dil):
    s = qkv.shape[0]
    w = _A_WIDTH
    length = s // dil
    nb = length // _WIN_BLOCK
    view = qkv.reshape(length, dil * 3 * w)

    def body(q_ref, kp_ref, kc_ref, vp_ref, vc_ref, o_ref, l_ref):
        allow_p, allow_c = _band_masks(pl.program_id(1) == 0)
        hm = _head_masks()
        for sl in range(w // 128):
            cols = slice(sl * 128, (sl + 1) * 128)
            q, kp, kc, vp, vc = q_ref[:, cols], kp_ref[:, cols], kc_ref[:, cols], vp_ref[:, cols], vc_ref[:, cols]
            o_s = jnp.zeros((_WIN_BLOCK, 128), _F32)
            l_s = jnp.zeros((_WIN_BLOCK, 128), _F32)
            for hh in range(2):
                qh = jnp.where(hm[hh], q, jnp.zeros_like(q))
                sp = jnp.where(allow_p, _dot_nt(qh, kp), _NEG)
                sc = jnp.where(allow_c, _dot_nt(qh, kc), _NEG)
                m = jnp.maximum(jnp.max(sp, axis=1, keepdims=True), jnp.max(sc, axis=1, keepdims=True))
                pp, pc = jnp.exp(sp - m), jnp.exp(sc - m)
                den = jnp.sum(pp, axis=1, keepdims=True) + jnp.sum(pc, axis=1, keepdims=True)
                oh = (_dot(pp, vp) + _dot(pc, vc)) / den
                o_s = jnp.where(hm[hh], oh, o_s)
                l_s = jnp.where(hm[hh], m + jnp.log(den), l_s)
            o_ref[:, cols] = o_s
            l_ref[:, cols] = l_s

    def at(col, prev):
        if prev:
            return pl.BlockSpec((_WIN_BLOCK, w), lambda r, n: (jnp.maximum(n - 1, 0), 3 * r + col))
        return pl.BlockSpec((_WIN_BLOCK, w), lambda r, n: (n, 3 * r + col))

    out = pl.BlockSpec((_WIN_BLOCK, w), lambda r, n: (n, r))
    sds = jax.ShapeDtypeStruct((length, dil * w), _F32)
    o, lse = pl.pallas_call(
        body, name=f"attn_fwd_d{dil}", grid=(dil, nb),
        in_specs=[at(0, False), at(1, True), at(1, False), at(2, True), at(2, False)],
        out_specs=[out, out], out_shape=[sds, sds], compiler_params=_params(("arbitrary", "arbitrary")),
    )(view, view, view, view, view)
    return o.reshape(s, w), lse.reshape(s, w)


def _attn_bwd(qkv, o, lse, do, dlse, dil):
    s = qkv.shape[0]
    w = _A_WIDTH
    length = s // dil
    nb = length // _WIN_BLOCK
    view = qkv.reshape(length, dil * 3 * w)
    o, lse, do, dlse = (t.reshape(length, dil * w) for t in (o, lse, do, dlse))

    def body(q_ref, kp_ref, kc_ref, vp_ref, vc_ref, o_ref, l_ref, do_ref, dl_ref, out_ref,
             car_q, car_k, car_v, cur_q, cur_kc, cur_vc, cur_kp, cur_vp):
        n = pl.program_id(1)

        @pl.when(n == 0)
        def _():
            car_q[...] = jnp.zeros_like(car_q)
            car_k[...] = jnp.zeros_like(car_k)
            car_v[...] = jnp.zeros_like(car_v)

        @pl.when(n == nb)
        def _():
            cur_kp[...] = jnp.zeros_like(cur_kp)
            cur_vp[...] = jnp.zeros_like(cur_vp)

        @pl.when(n < nb)
        def _():
            allow_p, allow_c = _band_masks(n == 0)
            hm = _head_masks()
            for sl in range(w // 128):
                cols = slice(sl * 128, (sl + 1) * 128)
                q, kp, kc, vp, vc = q_ref[:, cols], kp_ref[:, cols], kc_ref[:, cols], vp_ref[:, cols], vc_ref[:, cols]
                ov, lv, dov, dlv = o_ref[:, cols], l_ref[:, cols], do_ref[:, cols], dl_ref[:, cols]
                zero = jnp.zeros((_WIN_BLOCK, 128), _F32)
                dq_s, dkp_s, dkc_s, dvp_s, dvc_s = zero, zero, zero, zero, zero
                for hh in range(2):
                    qh = jnp.where(hm[hh], q, jnp.zeros_like(q))
                    doh = jnp.where(hm[hh], dov, 0.0).astype(_BF)
                    lse_h = jnp.sum(jnp.where(hm[hh], lv, 0.0), axis=1, keepdims=True) * (1.0 / _A_HEAD_DIM)
                    delta = jnp.sum(jnp.where(hm[hh], dov * ov, 0.0), axis=1, keepdims=True)
                    dl = jnp.sum(jnp.where(hm[hh], dlv, 0.0), axis=1, keepdims=True)
                    pp = jnp.exp(jnp.where(allow_p, _dot_nt(qh, kp), _NEG) - lse_h)
                    pc = jnp.exp(jnp.where(allow_c, _dot_nt(qh, kc), _NEG) - lse_h)
                    dsp = (pp * (_dot_nt(doh, vp) - delta + dl)).astype(_BF)
                    dsc = (pc * (_dot_nt(doh, vc) - delta + dl)).astype(_BF)
                    dq_s = dq_s + jnp.where(hm[hh], _dot(dsp, kp) + _dot(dsc, kc), 0.0)
                    dkp_s = dkp_s + _dot_tn(dsp, qh)
                    dkc_s = dkc_s + _dot_tn(dsc, qh)
                    dvp_s = dvp_s + _dot_tn(pp, doh)
                    dvc_s = dvc_s + _dot_tn(pc, doh)
                cur_q[:, cols] = dq_s
                cur_kp[:, cols] = dkp_s
                cur_kc[:, cols] = dkc_s
                cur_vp[:, cols] = dvp_s
                cur_vc[:, cols] = dvc_s

        out_ref[:, 0:w] = car_q[...]
        out_ref[:, w:2 * w] = car_k[...] + cur_kp[...]
        out_ref[:, 2 * w:3 * w] = car_v[...] + cur_vp[...]

        @pl.when(n < nb)
        def _():
            car_q[...] = cur_q[...]
            car_k[...] = cur_kc[...]
            car_v[...] = cur_vc[...]

    def at(col, prev):
        if prev:
            return pl.BlockSpec((_WIN_BLOCK, w), lambda r, n: (jnp.maximum(jnp.minimum(n, nb - 1) - 1, 0), 3 * r + col))
        return pl.BlockSpec((_WIN_BLOCK, w), lambda r, n: (jnp.minimum(n, nb - 1), 3 * r + col))

    blk = pl.BlockSpec((_WIN_BLOCK, w), lambda r, n: (jnp.minimum(n, nb - 1), r))
    scratch = [pltpu.VMEM((_WIN_BLOCK, w), _F32) for _ in range(8)]
    out = pl.pallas_call(
        body, name=f"attn_bwd_d{dil}", grid=(dil, nb + 1),
        in_specs=[at(0, False), at(1, True), at(1, False), at(2, True), at(2, False), blk, blk, blk, blk],
        out_specs=pl.BlockSpec((_WIN_BLOCK, 3 * w), lambda r, n: (jnp.maximum(n - 1, 0), r)),
        out_shape=jax.ShapeDtypeStruct((length, dil * 3 * w), _F32), scratch_shapes=scratch,
        compiler_params=_params(("arbitrary", "arbitrary")),
    )(view, view, view, view, view, o, lse, do, dlse)
    return out.reshape(s, 3 * w)


def _mix(o1, o2, o3, l1, l2, l3):
    m = jnp.maximum(jnp.maximum(l1, l2), l3)
    e1, e2, e3 = jnp.exp(l1 - m), jnp.exp(l2 - m), jnp.exp(l3 - m)
    return (e1 * o1 + e2 * o2 + e3 * o3) / (e1 + e2 + e3)


def _attn_mix_fwd(outs, lses):
    s = outs[0].shape[0]
    col = lambda j: j
    tiles = [(t, 128, col) for t in (*outs, *lses)]
    return _rowmap("attn_mix_fwd", _mix, tiles, [], [(2 * _A_WIDTH, 128, col, _BF)], rows=s, tm=_TM,
                   ncol=_A_WIDTH // 128)[0]


def _attn_mix_bwd(outs, lses, dy):
    s = outs[0].shape[0]
    col = lambda j: j
    tiles = [(t, 128, col) for t in (*outs, *lses, dy)]
    body = _vjp_body(_mix, 6, range(6), 1)
    return _rowmap("attn_mix_bwd", body, tiles, [], [(_A_WIDTH, 128, col, _F32)] * 6, rows=s, tm=_TM,
                   ncol=_A_WIDTH // 128)


_CONV_TM = 256
_HALO = 32


def _ln_silu(pre, g, b):
    mu = jnp.mean(pre, axis=-1, keepdims=True)
    xc = pre - mu
    y = xc * lax.rsqrt(jnp.mean(xc * xc, axis=-1, keepdims=True) + _NORM_EPS)
    return _silu(y * g + b)


def _glu_into(buf, ua_ref, ub_ref, uah_ref, ubh_ref, first):
    halo = uah_ref[...] * jax.nn.sigmoid(ubh_ref[...])
    buf[0:_HALO, :] = jnp.where(first, 0.0, halo)
    buf[_HALO:, :] = ua_ref[...] * jax.nn.sigmoid(ub_ref[...])


def _causal_taps(buf, w_ref, rows):
    base = _HALO - (_CONV_WIDTH - 1)
    acc = jnp.zeros((rows, buf.shape[1]), _F32)
    for j in range(_CONV_WIDTH):
        acc = acc + w_ref[j:j + 1, :] * buf[base + j:base + j + rows, :]
    return acc


def _conv_specs(s):
    tm = _CONV_TM
    per = tm // _HALO
    cw = _A_WIDTH
    cur = lambda col: pl.BlockSpec((tm, cw), lambda i: (i, col))
    prev = lambda col: pl.BlockSpec((_HALO, cw), lambda i: (jnp.maximum(i * per - 1, 0), col))
    return tm, per, cw, cur, prev


def _full2(arr):
    return pl.BlockSpec(arr.shape, lambda i: (0, 0))


def _conv_fwd(proj, dw_w, dw_b, ln_g, ln_b, y):
    s = proj.shape[0]
    tm, per, cw, cur, prev = _conv_specs(s)

    def body(ua_ref, ub_ref, uah_ref, ubh_ref, w_ref, b_ref, g_ref, be_ref, y_in, o_ref, buf):
        del y_in
        _glu_into(buf, ua_ref, ub_ref, uah_ref, ubh_ref, pl.program_id(0) == 0)
        pre = _causal_taps(buf, w_ref, tm) + b_ref[...]
        o_ref[...] = _ln_silu(pre, g_ref[...], be_ref[...]).astype(_BF)

    return pl.pallas_call(
        body, name="conv_fwd", grid=(s // tm,),
        in_specs=[cur(3), cur(4), prev(3), prev(4), _full2(dw_w), _full2(dw_b), _full2(ln_g), _full2(ln_b), _ANY],
        out_specs=pl.BlockSpec((tm, cw), lambda i: (i, 1)), out_shape=jax.ShapeDtypeStruct(y.shape, y.dtype),
        scratch_shapes=[pltpu.VMEM((tm + _HALO, cw), _F32)], input_output_aliases={8: 0},
        compiler_params=_params(("arbitrary",)),
    )(proj, proj, proj, proj, dw_w, dw_b, ln_g, ln_b, y)


def _conv_bwd_pre(proj, dw_w, dw_b, ln_g, ln_b, dy):
    s = proj.shape[0]
    tm, per, cw, cur, prev = _conv_specs(s)

    def body(ua_ref, ub_ref, uah_ref, ubh_ref, w_ref, b_ref, g_ref, be_ref, dy_ref, dp_ref, dg_ref, dbe_ref, db_ref, buf):
        i = pl.program_id(0)
        _glu_into(buf, ua_ref, ub_ref, uah_ref, ubh_ref, i == 0)
        pre = _causal_taps(buf, w_ref, tm) + b_ref[...]
        _, pull = jax.vjp(_ln_silu, pre, g_ref[...], be_ref[...])
        dpre, dg, dbe = pull(dy_ref[...])
        dp_ref[...] = dpre

        @pl.when(i == 0)
        def _():
            dg_ref[...] = jnp.zeros_like(dg_ref)
            dbe_ref[...] = jnp.zeros_like(dbe_ref)
            db_ref[...] = jnp.zeros_like(db_ref)

        dg_ref[...] += dg
        dbe_ref[...] += dbe
        db_ref[...] += jnp.sum(dpre, axis=0, keepdims=True)

    vec = pl.BlockSpec((1, cw), lambda i: (0, 0))
    vsd = jax.ShapeDtypeStruct((1, cw), _F32)
    return pl.pallas_call(
        body, name="conv_bwd_pre", grid=(s // tm,),
        in_specs=[cur(3), cur(4), prev(3), prev(4), _full2(dw_w), _full2(dw_b), _full2(ln_g), _full2(ln_b), cur(1)],
        out_specs=[pl.BlockSpec((tm, cw), lambda i: (i, 0)), vec, vec, vec],
        out_shape=[jax.ShapeDtypeStruct((s, cw), _F32), vsd, vsd, vsd],
        scratch_shapes=[pltpu.VMEM((tm + _HALO, cw), _F32)], compiler_params=_params(("arbitrary",)),
    )(proj, proj, proj, proj, dw_w, dw_b, ln_g, ln_b, dy)


def _conv_bwd_taps(proj, dw_w, dpre):
    s = proj.shape[0]
    tm, per, cw, cur, prev = _conv_specs(s)
    nt = s // tm

    def body(ua_ref, ub_ref, uah_ref, ubh_ref, w_ref, dp_ref, dpn_ref, dglu_ref, dw_ref, buf, dbuf):
        i = pl.program_id(0)
        _glu_into(buf, ua_ref, ub_ref, uah_ref, ubh_ref, i == 0)
        dpre = dp_ref[...]
        dbuf[0:tm, :] = dpre
        dbuf[tm:, :] = jnp.where(i == nt - 1, 0.0, dpn_ref[...])
        acc = jnp.zeros((tm, cw), _F32)
        for j in range(_CONV_WIDTH):
            off = _CONV_WIDTH - 1 - j
            acc = acc + w_ref[j:j + 1, :] * dbuf[off:off + tm, :]
        dglu_ref[...] = acc

        @pl.when(i == 0)
        def _():
            dw_ref[...] = jnp.zeros_like(dw_ref)

        base = _HALO - (_CONV_WIDTH - 1)
        for j in range(_CONV_WIDTH):
            dw_ref[j:j + 1, :] += jnp.sum(dpre * buf[base + j:base + j + tm, :], axis=0, keepdims=True)

    nxt = pl.BlockSpec((_HALO, cw), lambda i: (jnp.minimum((i + 1) * per, s // _HALO - 1), 0))
    return pl.pallas_call(
        body, name="conv_bwd_taps", grid=(nt,),
        in_specs=[cur(3), cur(4), prev(3), prev(4), _full2(dw_w), pl.BlockSpec((tm, cw), lambda i: (i, 0)), nxt],
        out_specs=[pl.BlockSpec((tm, cw), lambda i: (i, 0)), pl.BlockSpec((_HALO, cw), lambda i: (0, 0))],
        out_shape=[jax.ShapeDtypeStruct((s, cw), _F32), jax.ShapeDtypeStruct((_HALO, cw), _F32)],
        scratch_shapes=[pltpu.VMEM((tm + _HALO, cw), _F32), pltpu.VMEM((tm + _HALO, cw), _F32)],
        compiler_params=_params(("arbitrary",)),
    )(proj, proj, proj, proj, dw_w, dpre, dpre)


def _hyb_dproj(dqkv, tab_c, tab_s, dglu, proj):
    s = proj.shape[0]
    w = _A_WIDTH

    def body(d1_ref, d2_ref, d3_ref, c_ref, s_ref, dg_ref, ua_ref, ub_ref, o_ref):
        j = pl.program_id(1)

        @pl.when(j < 3)
        def _():
            x = d1_ref[...] + d2_ref[...] + d3_ref[...]
            o_ref[...] = _rot_apply(x, c_ref[...], s_ref[...], True).astype(_BF)

        @pl.when(j == 3)
        def _():
            o_ref[...] = (dg_ref[...] * jax.nn.sigmoid(ub_ref[...])).astype(_BF)

        @pl.when(j == 4)
        def _():
            sg = jax.nn.sigmoid(ub_ref[...])
            o_ref[...] = (dg_ref[...] * ua_ref[...] * sg * (1.0 - sg)).astype(_BF)

    dq = pl.BlockSpec((_TM, w), lambda i, j: (i, jnp.minimum(j, 2)))
    tab = pl.BlockSpec((None, _TM, 128), lambda i, j: (jnp.minimum(j, 2), i, 0))
    col = lambda c: pl.BlockSpec((_TM, w), lambda i, j: (i, c))
    return pl.pallas_call(
        body, name="hyb_dproj", grid=(s // _TM, 5),
        in_specs=[dq, dq, dq, tab, tab, col(0), col(3), col(4)],
        out_specs=pl.BlockSpec((_TM, w), lambda i, j: (i, j)),
        out_shape=jax.ShapeDtypeStruct((s, 5 * w), _BF), compiler_params=_params(("arbitrary", "arbitrary")),
    )(*dqkv, tab_c, tab_s, dglu, proj, proj)


def _proj_in(name, xn, w, layer):
    s, d = xn.shape
    ns = w.shape[-1]
    return _mm(name, xn, w, grid=(4, s // _TM), a_spec=pl.BlockSpec((_TM, d), lambda j, i: (i, 0)),
               b_spec=pl.BlockSpec((None, None, d, ns), lambda j, i: (layer, j, 0, 0)), dn=_NN,
               out_shape=jax.ShapeDtypeStruct((s, 4 * ns), _F32), out_spec=pl.BlockSpec((_TM, ns), lambda j, i: (i, j)))


def _proj_in_dx(name, dz, w, layer):
    s = dz.shape[0]
    d, ns = w.shape[2], w.shape[3]
    return _mm(name, dz, w, grid=(s // _TM, 4), a_spec=pl.BlockSpec((_TM, ns), lambda i, j: (i, j)),
               b_spec=pl.BlockSpec((None, None, d, ns), lambda i, j: (layer, j, 0, 0)), dn=_NT, red=1,
               acc_shape=(_TM, d), out_shape=jax.ShapeDtypeStruct((s, d), _F32),
               out_spec=pl.BlockSpec((_TM, d), lambda i, j: (i, 0)))


def _proj_in_dw(name, xn, dz, layer, n_layers, buf):
    s, d = xn.shape
    ns = dz.shape[1] // 4
    return _mm(name, xn, dz, grid=(4, s // _TM), a_spec=pl.BlockSpec((_TM, d), lambda j, i: (i, 0)),
               b_spec=pl.BlockSpec((_TM, ns), lambda j, i: (i, j)), dn=_TN, red=1, acc_shape=(d, ns),
               out_shape=jax.ShapeDtypeStruct((n_layers, 4, d, ns), _BF),
               out_spec=pl.BlockSpec((None, None, d, ns), lambda j, i: (layer, j, 0, 0)), into=buf)


def _proj_out_dy(name, dh, w, layer):
    s, d = dh.shape
    k = w.shape[1]
    return _mm(name, dh, w, grid=(s // _TM,), a_spec=pl.BlockSpec((_TM, d), lambda i: (i, 0)),
               b_spec=pl.BlockSpec((None, k, d), lambda i: (layer, 0, 0)), dn=_NT,
               out_shape=jax.ShapeDtypeStruct((s, k), _F32), out_spec=pl.BlockSpec((_TM, k), lambda i: (i, 0)))


def _hyb_fwd(h, gamma, w_in, w_out, dw_w, dw_b, ln_g, ln_b, tabs, li):
    hn = _rmsnorm_fwd(h, gamma)
    proj = _proj_in("hyb_in", hn, w_in, li)
    qkv = _qkv_prep(proj, *tabs)
    branches = [_attn_fwd(qkv, dil) for dil in _DILATIONS]
    outs, lses = [b[0] for b in branches], [b[1] for b in branches]
    y = _attn_mix_fwd(outs, lses)
    y = _conv_fwd(proj, dw_w, dw_b, ln_g, ln_b, y)
    h_new = _proj_residual("hyb_out", y, w_out, li, h, 1.0)
    return h_new, (h, hn, proj, qkv, outs, lses, y)


def _hyb_bwd(dh, saved, gamma, w_in, w_out, dw_w, dw_b, ln_g, ln_b, tabs, li, n_layers, dwin_buf, dwout_buf):
    h, hn, proj, qkv, outs, lses, y = saved
    dy = _proj_out_dy("hyb_dy", dh, w_out, li)
    dwout_buf = _wgrad_rows("hyb_dwout", y, dh, li, n_layers, dwout_buf, 1.0)
    mixg = _attn_mix_bwd(outs, lses, dy)
    dqkv = [_attn_bwd(qkv, outs[b], lses[b], mixg[b], mixg[3 + b], dil) for b, dil in enumerate(_DILATIONS)]
    dpre, d_ln_g, d_ln_b, d_dw_b = _conv_bwd_pre(proj, dw_w, dw_b, ln_g, ln_b, dy)
    dglu, d_dw_w = _conv_bwd_taps(proj, dw_w, dpre)
    dproj = _hyb_dproj(dqkv, *tabs, dglu, proj)
    dhn = _proj_in_dx("hyb_dhn", dproj, w_in, li)
    dwin_buf = _proj_in_dw("hyb_dwin", hn, dproj, li, n_layers, dwin_buf)
    dh_in, dgamma = _rmsnorm_bwd(h, gamma, dhn, dh)
    return dh_in, dgamma, (d_dw_w, d_dw_b, d_ln_g, d_ln_b), dwin_buf, dwout_buf


_GDN_BLOCK = 64
_GDN_HEADS_PER_STEP = 8


def _gdn_chunk(qs, ks, vs, gates, states, heads):
    n = len(qs)
    hs = range(n)
    c = qs[0].shape[0]
    lane = lax.broadcasted_iota(jnp.int32, gates.shape, 1)
    row = lax.broadcasted_iota(jnp.int32, (c, c), 0)
    col = lax.broadcasted_iota(jnp.int32, (c, c), 1)
    rowc = lax.broadcasted_iota(jnp.int32, (c, 1), 0)
    causal = row >= col
    eye = jnp.where(row == col, 1.0, 0.0)
    beta = [jnp.sum(jnp.where(lane == heads[h], gates, 0.0), axis=1, keepdims=True) for h in hs]
    g = [jnp.sum(jnp.where(lane == heads[h] + _GDN_HEADS, gates, 0.0), axis=1, keepdims=True) for h in hs]
    gc_row = [jnp.sum(jnp.where(row <= col, g[h], 0.0), axis=0, keepdims=True) for h in hs]
    gc_col = [jnp.sum(jnp.where(row == col, gc_row[h], 0.0), axis=1, keepdims=True) for h in hs]
    decay = [jnp.where(causal, jnp.exp(jnp.where(causal, gc_col[h] - gc_row[h], 0.0)), 0.0) for h in hs]
    kb = [ks[h] * beta[h] for h in hs]
    l_mat = [jnp.where(row > col, _dot_nt(kb[h], ks[h]) * decay[h], 0.0) for h in hs]
    t_inv = [eye - l_mat[h] for h in hs]
    x = [_dot_hi(l_mat[h], l_mat[h]) for h in hs]
    for it in range(5):
        t_inv = [t_inv[h] + _dot_hi(t_inv[h], x[h]) for h in hs]
        if it < 4:
            x = [_dot_hi(x[h], x[h]) for h in hs]
    eg = [jnp.exp(gc_col[h]) for h in hs]
    u = [_dot(t_inv[h], vs[h] * beta[h]) for h in hs]
    w = [_dot(t_inv[h], kb[h] * eg[h]) for h in hs]
    attn = [jnp.where(causal, _dot_nt(qs[h], ks[h]) * decay[h], 0.0) for h in hs]
    g_last = [jnp.sum(jnp.where(rowc == c - 1, gc_col[h], 0.0), axis=0, keepdims=True) for h in hs]
    k_dec = [ks[h] * jnp.exp(g_last[h] - gc_col[h]) for h in hs]
    v_new = [u[h] - _dot(w[h], states[h]) for h in hs]
    o = [_dot(qs[h] * eg[h], states[h]) + _dot(attn[h], v_new[h]) for h in hs]
    new_states = [states[h] * jnp.exp(g_last[h]) + _dot_tn(k_dec[h], v_new[h]) for h in hs]
    return o, new_states


def _gdn_chunk_fwd(q, k, v, gates):
    s = q.shape[0]
    cb, c = _GDN_BLOCK, _GDN_CHUNK
    nblk, per = s // cb, cb // c

    hps = _GDN_HEADS_PER_STEP

    def body(q_ref, k_ref, v_ref, g_ref, o_ref, st_ref, state):
        @pl.when(pl.program_id(1) == 0)
        def _():
            state[...] = jnp.zeros_like(state)

        heads = [pl.program_id(0) * hps + hh for hh in range(hps)]
        cols = [slice(hh * _GDN_DIM, (hh + 1) * _GDN_DIM) for hh in range(hps)]
        for ci in range(per):
            rows = slice(ci * c, (ci + 1) * c)
            sts = [state[hh] for hh in range(hps)]
            for hh in range(hps):
                st_ref[hh, ci] = sts[hh]
            outs, new = _gdn_chunk([q_ref[rows, cl] for cl in cols], [k_ref[rows, cl] for cl in cols],
                                   [v_ref[rows, cl] for cl in cols], g_ref[rows, :], sts, heads)
            for hh in range(hps):
                o_ref[rows, cols[hh]] = outs[hh]
                state[hh] = new[hh]

    slab = pl.BlockSpec((cb, hps * _GDN_DIM), lambda h, i: (i, h))
    return pl.pallas_call(
        body, name="gdn_chunk_fwd", grid=(_GDN_HEADS // hps, nblk),
        in_specs=[slab, slab, slab, pl.BlockSpec((cb, 128), lambda h, i: (i, 0))],
        out_specs=[slab, pl.BlockSpec((hps, per, _GDN_DIM, _GDN_DIM), lambda h, i: (h, i, 0, 0))],
        out_shape=[jax.ShapeDtypeStruct((s, _GDN_HEADS * _GDN_DIM), _F32),
                   jax.ShapeDtypeStruct((_GDN_HEADS, s // c, _GDN_DIM, _GDN_DIM), _F32)],
        scratch_shapes=[pltpu.VMEM((hps, _GDN_DIM, _GDN_DIM), _F32)],
        compiler_params=_params(("arbitrary", "arbitrary")),
    )(q, k, v, gates)


def _gdn_chunk_bwd(q, k, v, gates, states, do):
    s = q.shape[0]
    cb, c = _GDN_BLOCK, _GDN_CHUNK
    nblk, per = s // cb, cb // c

    hps = _GDN_HEADS_PER_STEP

    def body(q_ref, k_ref, v_ref, g_ref, st_ref, do_ref, dq_ref, dk_ref, dv_ref, dg_ref, dstate):
        @pl.when(pl.program_id(1) == 0)
        def _():
            dstate[...] = jnp.zeros_like(dstate)

        heads = [pl.program_id(0) * hps + hh for hh in range(hps)]
        cols = [slice(hh * _GDN_DIM, (hh + 1) * _GDN_DIM) for hh in range(hps)]
        for ci in reversed(range(per)):
            rows = slice(ci * c, (ci + 1) * c)
            fn = functools.partial(_gdn_chunk, heads=heads)
            _, pull = jax.vjp(fn, [q_ref[rows, cl] for cl in cols], [k_ref[rows, cl] for cl in cols],
                              [v_ref[rows, cl] for cl in cols], g_ref[rows, :], [st_ref[hh, ci] for hh in range(hps)])
            dq, dk, dv, dg, dst = pull(([do_ref[rows, cl] for cl in cols], [dstate[hh] for hh in range(hps)]))
            dg_ref[rows, :] = dg
            for hh in range(hps):
                dq_ref[rows, cols[hh]] = dq[hh]
                dk_ref[rows, cols[hh]] = dk[hh]
                dv_ref[rows, cols[hh]] = dv[hh]
                dstate[hh] = dst[hh]

    slab = pl.BlockSpec((cb, hps * _GDN_DIM), lambda h, i: (nblk - 1 - i, h))
    sds = jax.ShapeDtypeStruct((s, _GDN_HEADS * _GDN_DIM), _F32)
    return pl.pallas_call(
        body, name="gdn_chunk_bwd", grid=(_GDN_HEADS // hps, nblk),
        in_specs=[slab, slab, slab, pl.BlockSpec((cb, 128), lambda h, i: (nblk - 1 - i, 0)),
                  pl.BlockSpec((hps, per, _GDN_DIM, _GDN_DIM), lambda h, i: (h, nblk - 1 - i, 0, 0)), slab],
        out_specs=[slab, slab, slab, pl.BlockSpec((None, cb, 128), lambda h, i: (h, nblk - 1 - i, 0))],
        out_shape=[sds, sds, sds, jax.ShapeDtypeStruct((_GDN_HEADS // hps, s, 128), _F32)],
        scratch_shapes=[pltpu.VMEM((hps, _GDN_DIM, _GDN_DIM), _F32)],
        compiler_params=_params(("arbitrary", "arbitrary")),
    )(q, k, v, gates, states, do)


_SHALO = 8


def _swish_l2(c, normalize, scale):
    y = _silu(c)
    if normalize:
        y = y * lax.rsqrt(jnp.sum(y * y, axis=-1, keepdims=True) + _NORM_EPS) * scale
    return y


def _short_taps(buf, w_ref, rows):
    base = _SHALO - (_GDN_SHORT_CONV - 1)
    acc = jnp.zeros((rows, buf.shape[1]), _F32)
    for t in range(_GDN_SHORT_CONV):
        acc = acc + w_ref[t:t + 1, :] * buf[base + t:base + t + rows, :]
    return acc


def _gdn_conv_fwd(proj, conv_w, off, normalize, scale):
    s = proj.shape[0]
    tm = _TM
    per = tm // _SHALO

    def body(x_ref, xp_ref, w_ref, o_ref, buf):
        buf[0:_SHALO, :] = jnp.where(pl.program_id(1) == 0, 0.0, xp_ref[...])
        buf[_SHALO:, :] = x_ref[...]
        o_ref[...] = _swish_l2(_short_taps(buf, w_ref, tm), normalize, scale)

    return pl.pallas_call(
        body, name=f"gdn_conv_fwd_{off}", grid=(_GDN_HEADS, s // tm),
        in_specs=[pl.BlockSpec((tm, 128), lambda j, i: (i, off + j)),
                  pl.BlockSpec((_SHALO, 128), lambda j, i: (jnp.maximum(i * per - 1, 0), off + j)),
                  pl.BlockSpec((_GDN_SHORT_CONV, 128), lambda j, i: (0, off + j))],
        out_specs=pl.BlockSpec((tm, 128), lambda j, i: (i, j)),
        out_shape=jax.ShapeDtypeStruct((s, _GDN_HEADS * 128), _F32),
        scratch_shapes=[pltpu.VMEM((tm + _SHALO, 128), _F32)], compiler_params=_params(("arbitrary", "arbitrary")),
    )(proj, proj, conv_w)


def _gdn_conv_bwd(proj, conv_w, off, normalize, scale, dy, dproj):
    s = proj.shape[0]
    tm = _TM
    per = tm // _SHALO
    nt = s // tm
    taps = _GDN_SHORT_CONV

    def body(x_ref, xp_ref, xn_ref, w_ref, dy_ref, dyn_ref, dp_in, dx_ref, dw_ref, buf, dbuf):
        del dp_in
        i = pl.program_id(1)
        last = i == nt - 1
        buf[0:_SHALO, :] = jnp.where(i == 0, 0.0, xp_ref[...])
        buf[_SHALO:_SHALO + tm, :] = x_ref[...]
        buf[_SHALO + tm:, :] = jnp.where(last, 0.0, xn_ref[...])
        conv = _short_taps(buf, w_ref, tm + _SHALO)
        dy_ext = jnp.concatenate([dy_ref[...], jnp.where(last, 0.0, dyn_ref[...])], axis=0)
        _, pull = jax.vjp(lambda c: _swish_l2(c, normalize, scale), conv)
        dconv = pull(dy_ext)[0]
        dbuf[...] = dconv
        acc = jnp.zeros((tm, 128), _F32)
        for t in range(taps):
            o = taps - 1 - t
            acc = acc + w_ref[t:t + 1, :] * dbuf[o:o + tm, :]
        dx_ref[...] = acc.astype(dx_ref.dtype)

        @pl.when(i == 0)
        def _():
            dw_ref[...] = jnp.zeros_like(dw_ref)

        base = _SHALO - (taps - 1)
        for t in range(taps):
            dw_ref[t:t + 1, :] += jnp.sum(dbuf[0:tm, :] * buf[base + t:base + t + tm, :], axis=0, keepdims=True)

    cur = lambda col0: pl.BlockSpec((tm, 128), lambda j, i: (i, col0 + j))
    nxt = lambda col0: pl.BlockSpec((_SHALO, 128), lambda j, i: (jnp.minimum((i + 1) * per, s // _SHALO - 1), col0 + j))
    return pl.pallas_call(
        body, name=f"gdn_conv_bwd_{off}", grid=(_GDN_HEADS, nt),
        in_specs=[cur(off), pl.BlockSpec((_SHALO, 128), lambda j, i: (jnp.maximum(i * per - 1, 0), off + j)), nxt(off),
                  pl.BlockSpec((taps, 128), lambda j, i: (0, off + j)), cur(0), nxt(0), _ANY],
        out_specs=[cur(off), pl.BlockSpec((_SHALO, 128), lambda j, i: (0, j))],
        out_shape=[jax.ShapeDtypeStruct(dproj.shape, dproj.dtype), jax.ShapeDtypeStruct((_SHALO, _GDN_HEADS * 128), _F32)],
        scratch_shapes=[pltpu.VMEM((tm + 2 * _SHALO, 128), _F32), pltpu.VMEM((tm + _SHALO, 128), _F32)],
        input_output_aliases={6: 0}, compiler_params=_params(("arbitrary", "arbitrary")),
    )(proj, proj, proj, conv_w, dy, dy, dproj)


def _softplus(z):
    return jnp.maximum(z, 0.0) + jnp.log(1.0 + jnp.exp(-jnp.abs(z)))


def _gates(ba, a_lane, dt_lane):
    lane = lax.broadcasted_iota(jnp.int32, ba.shape, 1)
    g = -jnp.exp(a_lane) * _softplus(ba + dt_lane)
    return jnp.where(lane < _GDN_HEADS, jax.nn.sigmoid(ba), jnp.where(lane < 2 * _GDN_HEADS, g, 0.0))


def _gates_fwd(ba, a_lane, dt_lane):
    return _rowmap("gdn_gates_fwd", _gates, [(ba, 128, 0)], [a_lane, dt_lane], [(128, 128, 0, _F32)],
                   rows=ba.shape[0], tm=_TM)[0]


def _gates_bwd(ba, a_lane, dt_lane, dgates):
    def body(bav, dg8, av, dv):
        _, pull = jax.vjp(_gates, bav, av, dv)
        return pull(jnp.sum(dg8, axis=0))

    return _rowmap("gdn_gates_bwd", body, [(ba, 128, 0), (dgates, 128, 0)], [a_lane, dt_lane], [(128, 128, 0, _BF)],
                   [(1, 128), (1, 128)], rows=ba.shape[0], tm=_TM)


def _out_gate(o, z, g):
    return (_rms(o, g) * _silu(z)).astype(_BF)


def _out_gate_fwd(o, proj, norm_g):
    s = o.shape[0]
    return _rowmap("gdn_out_gate_fwd", _out_gate, [(o, 128, lambda j: j), (proj, 128, lambda j: 3 * _GDN_HEADS + j)],
                   [norm_g], [(o.shape[1], 128, lambda j: j, _BF)], rows=s, tm=_TM, ncol=_GDN_HEADS)[0]


def _out_gate_bwd(o, proj, norm_g, dy):
    s = o.shape[0]
    body = _vjp_body(_out_gate, 3, (0, 1, 2), 1)
    tiles = [(o, 128, lambda j: j), (proj, 128, lambda j: 3 * _GDN_HEADS + j)]
    def ordered(ov, zv, dyv, gv):
        return body(ov, zv, gv, dyv)
    return _rowmap("gdn_out_gate_bwd", ordered, tiles + [(dy, 128, lambda j: j)], [norm_g],
                   [(o.shape[1], 128, lambda j: j, _F32), (proj.shape[1], 128, lambda j: 3 * _GDN_HEADS + j, _BF)],
                   [(1, 128)], rows=s, tm=_TM, ncol=_GDN_HEADS)


def _plain_in(name, xn, w, layer, tn):
    s, d = xn.shape
    n = w.shape[2]
    return _mm(name, xn, w, grid=(n // tn, s // _TM), a_spec=pl.BlockSpec((_TM, d), lambda j, i: (i, 0)),
               b_spec=pl.BlockSpec((None, d, tn), lambda j, i: (layer, 0, j)), dn=_NN,
               out_shape=jax.ShapeDtypeStruct((s, n), _F32), out_spec=pl.BlockSpec((_TM, tn), lambda j, i: (i, j)))


def _plain_dx(name, dz, w, layer, tn, extra=None):
    s, n = dz.shape
    d = w.shape[1]
    row = pl.BlockSpec((_TM, d), lambda i, j: (i, 0))
    return _mm(name, dz, w, grid=(s // _TM, n // tn), a_spec=pl.BlockSpec((_TM, tn), lambda i, j: (i, j)),
               b_spec=pl.BlockSpec((None, d, tn), lambda i, j: (layer, 0, j)), dn=_NT, red=1, acc_shape=(_TM, d),
               out_shape=jax.ShapeDtypeStruct((s, d), _F32), out_spec=row,
               epi=(lambda acc, e: acc + e) if extra is not None else None,
               extras=(extra,) if extra is not None else (), extra_specs=(row,) if extra is not None else ())


def _plain_dw(name, xn, dz, layer, n_layers, buf, tn):
    s, d = xn.shape
    n = dz.shape[1]
    return _mm(name, xn, dz, grid=(n // tn, s // _TM), a_spec=pl.BlockSpec((_TM, d), lambda j, i: (i, 0)),
               b_spec=pl.BlockSpec((_TM, tn), lambda j, i: (i, j)), dn=_TN, red=1, acc_shape=(d, tn),
               out_shape=jax.ShapeDtypeStruct((n_layers, d, n), _BF),
               out_spec=pl.BlockSpec((None, d, tn), lambda j, i: (layer, 0, j)), into=buf)


_GDN_QK_SCALE = _GDN_DIM ** -0.5


def _gdn_fwd(h, gamma, w_qkvz, w_ba, w_out, conv_w, a_lane, dt_lane, norm_g, li):
    hn = _rmsnorm_fwd(h, gamma)
    proj = _plain_in("gdn_in", hn, w_qkvz, li, 1024)
    ba = _plain_in("gdn_in_ba", hn, w_ba, li, 128)
    q = _gdn_conv_fwd(proj, conv_w, 0, True, _GDN_QK_SCALE)
    k = _gdn_conv_fwd(proj, conv_w, _GDN_HEADS, True, 1.0)
    v = _gdn_conv_fwd(proj, conv_w, 2 * _GDN_HEADS, False, 1.0)
    gates = _gates_fwd(ba, a_lane, dt_lane)
    o, states = _gdn_chunk_fwd(q, k, v, gates)
    y = _out_gate_fwd(o, proj, norm_g)
    h_new = _proj_residual("gdn_out", y, w_out, li, h, 1.0)
    return h_new, (h, hn, proj, ba, q, k, v, gates, states, o, y)


def _gdn_bwd(dh, saved, gamma, w_qkvz, w_ba, w_out, conv_w, a_lane, dt_lane, norm_g, li, n_layers, bufs):
    h, hn, proj, ba, q, k, v, gates, states, o, y = saved
    dw_qkvz, dw_ba, dw_out = bufs
    dy = _proj_out_dy("gdn_dy", dh, w_out, li)
    dw_out = _wgrad_rows("gdn_dwout", y, dh, li, n_layers, dw_out, 1.0)
    do, dproj, d_norm_g = _out_gate_bwd(o, proj, norm_g, dy)
    dq, dk, dv, dgates = _gdn_chunk_bwd(q, k, v, gates, states, do)
    dproj, dcw_q = _gdn_conv_bwd(proj, conv_w, 0, True, _GDN_QK_SCALE, dq, dproj)
    dproj, dcw_k = _gdn_conv_bwd(proj, conv_w, _GDN_HEADS, True, 1.0, dk, dproj)
    dproj, dcw_v = _gdn_conv_bwd(proj, conv_w, 2 * _GDN_HEADS, False, 1.0, dv, dproj)
    dba, d_a, d_dt = _gates_bwd(ba, a_lane, dt_lane, dgates)
    dhn_ba = _plain_dx("gdn_dhn_ba", dba, w_ba, li, 128)
    dhn = _plain_dx("gdn_dhn", dproj, w_qkvz, li, 1024, extra=dhn_ba)
    dw_qkvz = _plain_dw("gdn_dwin", hn, dproj, li, n_layers, dw_qkvz, 1024)
    dw_ba = _plain_dw("gdn_dwin_ba", hn, dba, li, n_layers, dw_ba, 128)
    dh_in, dgamma = _rmsnorm_bwd(h, gamma, dhn, dh)
    d_conv_w = jnp.concatenate([dcw_q, dcw_k, dcw_v], axis=1)[:_GDN_SHORT_CONV]
    return dh_in, dgamma, (d_conv_w, d_a, d_dt, d_norm_g), (dw_qkvz, dw_ba, dw_out)


_MESH = pl.DeviceIdType.MESH
_ROW_TM = 256


def _place():
    x, y, c = lax.axis_index("x"), lax.axis_index("y"), lax.axis_index("c")
    chips = [(1 - x, y), (x, 1 - y), (1 - x, 1 - y)]
    return x, y, c, chips, [2 * px + py for px, py in chips]


def _remote(src, dst, send_sem, recv_sem, device):
    return pltpu.make_async_remote_copy(src_ref=src, dst_ref=dst, send_sem=send_sem, recv_sem=recv_sem,
                                        device_id=device, device_id_type=_MESH)


def _comm_call(name, body, operands, out_shape, n_sems):
    return pl.pallas_call(
        body, name=name, in_specs=[_ANY] * len(operands),
        out_specs=[_ANY] * len(out_shape), out_shape=out_shape,
        scratch_shapes=[pltpu.SemaphoreType.DMA((n_sems,)), pltpu.SemaphoreType.DMA((n_sems,)),
                        pltpu.SemaphoreType.DMA],
    )(*operands)


def _rows2d(t):
    return t.reshape(-1, t.shape[-1])


def _row_tile(rows):
    tm = _ROW_TM
    while tm > 8 and rows % tm:
        tm //= 2
    return tm


def _placed_map(name, fn, where, ins, out_rows, cols, out_dtype, out_row, steps, tm, into=None):
    n_in = len(ins)

    def kern(where_ref, *refs):
        del where_ref
        o_ref = refs[n_in + (1 if into is not None else 0)]
        o_ref[...] = fn(*[r[...] for r in refs[:n_in]]).astype(o_ref.dtype)

    in_specs, operands = [], [where]
    for arr, rowfn in ins:
        operands.append(arr)
        if arr.ndim == 3:
            in_specs.append(pl.BlockSpec((arr.shape[0], tm, cols), lambda i, wh, f=rowfn: (0, f(i, wh), 0)))
        else:
            in_specs.append(pl.BlockSpec((tm, cols), lambda i, wh, f=rowfn: (f(i, wh), 0)))
    aliases = {}
    if into is not None:
        operands.append(into)
        in_specs.append(_ANY)
        aliases = {n_in + 1: 0}
    spec = pltpu.PrefetchScalarGridSpec(
        num_scalar_prefetch=1, grid=(steps,), in_specs=in_specs,
        out_specs=pl.BlockSpec((tm, cols), lambda i, wh: (out_row(i, wh), 0)))
    return pl.pallas_call(
        kern, name=name, grid_spec=spec, out_shape=jax.ShapeDtypeStruct((out_rows, cols), out_dtype),
        input_output_aliases=aliases, compiler_params=_params(("arbitrary",)),
    )(*operands)


def _gather_small(name, t):
    n_l = t.shape[0]
    half = n_l // 2

    def body(t_ref, out_ref, send, recv, local_sem):
        x, y, c, chips, idx = _place()
        me = 2 * x + y
        mine, other = pl.ds(c * half, half), pl.ds((1 - c) * half, half)
        local = pltpu.make_async_copy(t_ref, out_ref.at[:, me], local_sem)
        local.start()
        sends = [_remote(t_ref.at[mine], out_ref.at[mine, me], send.at[k], recv.at[k], (*chips[k], c))
                 for k in range(3)]
        for cp in sends:
            cp.start()
        passed = []
        for k in range(3):
            landed = out_ref.at[mine, idx[k]]
            _remote(landed, landed, send.at[k], recv.at[k], (*chips[k], c)).wait_recv()
            cp = _remote(landed, landed, send.at[3 + k], recv.at[3 + k], (x, y, 1 - c))
            cp.start()
            passed.append(cp)
        for k in range(3):
            dst = out_ref.at[other, idx[k]]
            _remote(dst, dst, send.at[3 + k], recv.at[3 + k], (x, y, 1 - c)).wait_recv()
        for cp in sends + passed:
            cp.wait_send()
        local.wait()

    out = jax.ShapeDtypeStruct((n_l, 4) + t.shape[1:], t.dtype)
    return _comm_call(name, body, [t], [out], 6)[0]


def _gather(name, t, where):
    n_l, r, cols = t.shape
    half = n_l // 2
    tm = _row_tile(r)
    per = r // tm
    buf = _placed_map("cast_" + name, lambda val: val, where, [(_rows2d(t), lambda i, wh: i)], n_l * 4 * r, cols, _BF,
                      lambda i, wh: ((i // per) * 4 + wh[1]) * per + i % per, n_l * per, tm)

    def body(in_ref, out_ref, send, recv, local_sem):
        del in_ref, local_sem
        x, y, c, chips, idx = _place()
        me = 2 * x + y
        mine, other = pl.ds(c * half, half), pl.ds((1 - c) * half, half)
        own = out_ref.at[mine, me]
        sends = [_remote(own, own, send.at[k], recv.at[k], (*chips[k], c)) for k in range(3)]
        for cp in sends:
            cp.start()
        passed = []
        for k in range(3):
            landed = out_ref.at[mine, idx[k]]
            _remote(landed, landed, send.at[k], recv.at[k], (*chips[k], c)).wait_recv()
            cp = _remote(landed, landed, send.at[3 + k], recv.at[3 + k], (x, y, 1 - c))
            cp.start()
            passed.append(cp)
        for k in range(3):
            dst = out_ref.at[other, idx[k]]
            _remote(dst, dst, send.at[3 + k], recv.at[3 + k], (x, y, 1 - c)).wait_recv()
        for cp in sends + passed:
            cp.wait_send()

    shape = (n_l, 4, r, cols)
    return pl.pallas_call(
        body, name="gather_" + name, in_specs=[_ANY], out_specs=_ANY, out_shape=jax.ShapeDtypeStruct(shape, _BF),
        scratch_shapes=[pltpu.SemaphoreType.DMA((6,)), pltpu.SemaphoreType.DMA((6,)), pltpu.SemaphoreType.DMA],
        input_output_aliases={0: 0},
    )(buf.reshape(shape))


def _rs_sibling(name, g):
    half = g.shape[0] // 2

    def body(g_ref, got_ref, send, recv, local_sem):
        del local_sem
        x, y, c, _, _ = _place()
        cp = _remote(g_ref.at[pl.ds((1 - c) * half, half)], got_ref, send.at[0], recv.at[0], (x, y, 1 - c))
        cp.start()
        cp.wait()

    return _comm_call(name, body, [g], [jax.ShapeDtypeStruct((half,) + g.shape[1:], g.dtype)], 1)[0]


def _rs_chips(name, hsum):
    half = hsum.shape[0]

    def body(h_ref, got_ref, send, recv, local_sem):
        del local_sem
        x, y, c, chips, idx = _place()
        sends = [_remote(h_ref.at[:, idx[k]], got_ref.at[k], send.at[k], recv.at[k], (*chips[k], c)) for k in range(3)]
        for cp in sends:
            cp.start()
        for cp in sends:
            cp.wait()

    return _comm_call(name, body, [hsum], [jax.ShapeDtypeStruct((3, half) + hsum.shape[2:], hsum.dtype)], 3)[0]


def _rs_join(name, total):
    half = total.shape[0] // 2

    def body(in_ref, out_ref, send, recv, local_sem):
        del in_ref, local_sem
        x, y, c, _, _ = _place()
        mine = out_ref.at[pl.ds(c * half, half)]
        cp = _remote(mine, mine, send.at[0], recv.at[0], (x, y, 1 - c))
        cp.start()
        landing = out_ref.at[pl.ds((1 - c) * half, half)]
        _remote(landing, landing, send.at[0], recv.at[0], (x, y, 1 - c)).wait_recv()
        cp.wait_send()

    return pl.pallas_call(
        body, name=name, in_specs=[_ANY], out_specs=_ANY, out_shape=jax.ShapeDtypeStruct(total.shape, total.dtype),
        scratch_shapes=[pltpu.SemaphoreType.DMA((1,)), pltpu.SemaphoreType.DMA((1,)), pltpu.SemaphoreType.DMA],
        input_output_aliases={0: 0},
    )(total)


def _reduce_scatter(name, g, where):
    n_l, _, r, cols = g.shape
    half = n_l // 2
    tm = _row_tile(r)
    per = r // tm
    got = _rs_sibling(name + "_sib", g)
    n_pair = half * 4 * per
    chip_sum = _placed_map("rs_add_pair", lambda u, w: u.astype(_F32) + w.astype(_F32), where,
                           [(_rows2d(g), lambda i, wh: wh[0] * n_pair + i), (_rows2d(got), lambda i, wh: i)],
                           half * 4 * r, cols, _BF, lambda i, wh: i, n_pair, tm).reshape(half, 4, r, cols)
    got3 = _rs_chips(name + "_chips", chip_sum)

    def add_four(own, others):
        return ((own.astype(_F32) + others[0].astype(_F32)) + others[1].astype(_F32)) + others[2].astype(_F32)

    n_four = half * per
    total = _placed_map("rs_add_four", add_four, where,
                        [(_rows2d(chip_sum), lambda i, wh: ((i // per) * 4 + wh[1]) * per + i % per),
                         (got3.reshape(3, half * r, cols), lambda i, wh: i)],
                        n_l * r, cols, _F32, lambda i, wh: wh[0] * n_four + i, n_four, tm)
    return _rs_join(name + "_join", total.reshape(n_l, r, cols))


def _all_reduce_small(pack):
    rows, cols = pack.shape

    def body(p_ref, o_ref, land, send, recv):
        x, y, c = lax.axis_index("x"), lax.axis_index("y"), lax.axis_index("c")
        me = 4 * x + 2 * y + c
        land[me] = p_ref[...]
        copies = []
        for k in range(1, 8):
            bx, by, bc = (k >> 2) & 1, (k >> 1) & 1, k & 1
            peer = (1 - x if bx else x, 1 - y if by else y, 1 - c if bc else c)
            cp = _remote(p_ref, land.at[me], send.at[k - 1], recv.at[k - 1], peer)
            cp.start()
            copies.append((cp, 4 * peer[0] + 2 * peer[1] + peer[2]))
        for k, (cp, pid) in enumerate(copies):
            _remote(p_ref, land.at[pid], send.at[k], recv.at[k], (x, y, c)).wait_recv()
        for cp, _ in copies:
            cp.wait_send()
        acc = land[0]
        for d in range(1, 8):
            acc = acc + land[d]
        o_ref[...] = acc

    vm = pl.BlockSpec(memory_space=pltpu.VMEM)
    return pl.pallas_call(
        body, name="all_reduce_small", in_specs=[vm], out_specs=vm,
        out_shape=jax.ShapeDtypeStruct((rows, cols), _F32),
        scratch_shapes=[pltpu.VMEM((8, rows, cols), _F32), pltpu.SemaphoreType.DMA((7,)), pltpu.SemaphoreType.DMA((7,))],
    )(pack)


def _adamw_math(w, g, m, v):
    m2 = _ADAM_B1 * m + (1.0 - _ADAM_B1) * g
    v2 = _ADAM_B2 * v + (1.0 - _ADAM_B2) * (g * g)
    m_hat = m2 / (1.0 - _ADAM_B1 ** _ADAM_STEP)
    v_hat = v2 / (1.0 - _ADAM_B2 ** _ADAM_STEP)
    delta = -_ADAM_LR * (m_hat / (jnp.sqrt(v_hat) + _ADAM_EPS) + _ADAM_WD * w)
    return delta, m2, v2


def _adamw(w, g, m, v):
    shape = w.shape
    w2, g2, m2, v2 = (_rows2d(t) for t in (w, g, m, v))
    rows, cols = w2.shape
    outs = _rowmap("adamw", _adamw_math, [(t, cols, 0) for t in (w2, g2, m2, v2)], [], [(cols, cols, 0, _F32)] * 3,
                   rows=rows, tm=_row_tile(rows))
    return tuple(o.reshape(shape) for o in outs)


def kernel(x, positions, ffn1_norm, ffn1_w_in, ffn1_w_out, mix_norm, ffn2_norm, ffn2_w_in, ffn2_w_out, hyb_w_in, hyb_dw_w, hyb_dw_b, hyb_ln_g, hyb_ln_b, hyb_w_out, gdn_w_in, gdn_conv_w, gdn_A_log, gdn_dt_bias, gdn_norm_g, gdn_w_out, final_norm, loss_target, m_ffn1_norm, m_ffn1_w_in, m_ffn1_w_out, m_mix_norm, m_ffn2_norm, m_ffn2_w_in, m_ffn2_w_out, m_hyb_w_in, m_hyb_dw_w, m_hyb_dw_b, m_hyb_ln_g, m_hyb_ln_b, m_hyb_w_out, m_gdn_w_in, m_gdn_conv_w, m_gdn_A_log, m_gdn_dt_bias, m_gdn_norm_g, m_gdn_w_out, m_final_norm, v_ffn1_norm, v_ffn1_w_in, v_ffn1_w_out, v_mix_norm, v_ffn2_norm, v_ffn2_w_in, v_ffn2_w_out, v_hyb_w_in, v_hyb_dw_w, v_hyb_dw_b, v_hyb_ln_g, v_hyb_ln_b, v_hyb_w_out, v_gdn_w_in, v_gdn_conv_w, v_gdn_A_log, v_gdn_dt_bias, v_gdn_norm_g, v_gdn_w_out, v_final_norm):
    w = dict(ffn1_norm=ffn1_norm, ffn1_w_in=ffn1_w_in, ffn1_w_out=ffn1_w_out, mix_norm=mix_norm, ffn2_norm=ffn2_norm, ffn2_w_in=ffn2_w_in, ffn2_w_out=ffn2_w_out, hyb_w_in=hyb_w_in, hyb_dw_w=hyb_dw_w, hyb_dw_b=hyb_dw_b, hyb_ln_g=hyb_ln_g, hyb_ln_b=hyb_ln_b, hyb_w_out=hyb_w_out, gdn_w_in=gdn_w_in, gdn_conv_w=gdn_conv_w, gdn_A_log=gdn_A_log, gdn_dt_bias=gdn_dt_bias, gdn_norm_g=gdn_norm_g, gdn_w_out=gdn_w_out, final_norm=final_norm)
    m = dict(ffn1_norm=m_ffn1_norm, ffn1_w_in=m_ffn1_w_in, ffn1_w_out=m_ffn1_w_out, mix_norm=m_mix_norm, ffn2_norm=m_ffn2_norm, ffn2_w_in=m_ffn2_w_in, ffn2_w_out=m_ffn2_w_out, hyb_w_in=m_hyb_w_in, hyb_dw_w=m_hyb_dw_w, hyb_dw_b=m_hyb_dw_b, hyb_ln_g=m_hyb_ln_g, hyb_ln_b=m_hyb_ln_b, hyb_w_out=m_hyb_w_out, gdn_w_in=m_gdn_w_in, gdn_conv_w=m_gdn_conv_w, gdn_A_log=m_gdn_A_log, gdn_dt_bias=m_gdn_dt_bias, gdn_norm_g=m_gdn_norm_g, gdn_w_out=m_gdn_w_out, final_norm=m_final_norm)
    v = dict(ffn1_norm=v_ffn1_norm, ffn1_w_in=v_ffn1_w_in, ffn1_w_out=v_ffn1_w_out, mix_norm=v_mix_norm, ffn2_norm=v_ffn2_norm, ffn2_w_in=v_ffn2_w_in, ffn2_w_out=v_ffn2_w_out, hyb_w_in=v_hyb_w_in, hyb_dw_w=v_hyb_dw_w, hyb_dw_b=v_hyb_dw_b, hyb_ln_g=v_hyb_ln_g, hyb_ln_b=v_hyb_ln_b, hyb_w_out=v_hyb_w_out, gdn_w_in=v_gdn_w_in, gdn_conv_w=v_gdn_conv_w, gdn_A_log=v_gdn_A_log, gdn_dt_bias=v_gdn_dt_bias, gdn_norm_g=v_gdn_norm_g, gdn_w_out=v_gdn_w_out, final_norm=v_final_norm)
    return _train_step(x, positions, loss_target, w, m, v)


_WEIGHT_ORDER = ("ffn1_norm", "ffn1_w_in", "ffn1_w_out", "mix_norm", "ffn2_norm", "ffn2_w_in", "ffn2_w_out",
                 "hyb_w_in", "hyb_dw_w", "hyb_dw_b", "hyb_ln_g", "hyb_ln_b", "hyb_w_out", "gdn_w_in", "gdn_conv_w",
                 "gdn_A_log", "gdn_dt_bias", "gdn_norm_g", "gdn_w_out", "final_norm")
_BIG = ("ffn1_w_in", "ffn1_w_out", "ffn2_w_in", "ffn2_w_out", "hyb_w_in", "hyb_w_out", "gdn_w_in", "gdn_w_out")
_PACK_COLS = 1024


def _lane16(vec):
    return jnp.zeros((1, 128), _F32).at[0, _GDN_HEADS:2 * _GDN_HEADS].set(vec)


def _pack_rows(parts):
    rows, where, at = [], [], 0
    for p in parts:
        flat = p.reshape(-1).astype(_F32)
        n = -(-flat.shape[0] // (8 * _PACK_COLS)) * 8
        rows.append(jnp.pad(flat, (0, n * _PACK_COLS - flat.shape[0])).reshape(n, _PACK_COLS))
        where.append((at, n, flat.shape[0], p.shape))
        at += n
    return jnp.concatenate(rows, axis=0), where


def _unpack_rows(pack, where):
    return [pack[at:at + n].reshape(-1)[:size].reshape(shape) for at, n, size, shape in where]


def _train_step(x, positions, loss_target, w, m, v):
    n_layers = w["ffn1_norm"].shape[0]
    n_hyb, n_gdn = w["hyb_w_in"].shape[0], w["gdn_w_in"].shape[0]
    d_model = x.shape[-1]
    xh, target = x[0], loss_target[0]
    chip = 2 * lax.axis_index("x") + lax.axis_index("y")
    where = jnp.stack([lax.axis_index("c"), chip]).astype(jnp.int32)

    full = {name: _gather(name, w[name], where) for name in _BIG}
    ffn_in = {1: full["ffn1_w_in"], 2: full["ffn2_w_in"]}
    ffn_out = {k: full[f"ffn{k}_w_out"].reshape(n_layers, -1, d_model) for k in (1, 2)}
    hyb_in, hyb_out = full["hyb_w_in"], full["hyb_w_out"].reshape(n_hyb, -1, d_model)
    gdn_out = full["gdn_w_out"].reshape(n_gdn, -1, d_model)
    gdn_cols = full["gdn_w_in"].shape[-1]
    gdn_in = jnp.transpose(full["gdn_w_in"], (0, 2, 1, 3)).reshape(n_gdn, d_model, 4 * gdn_cols)
    n_qkvz = 4 * _GDN_HEADS * _GDN_DIM
    n_ba = 4 * gdn_cols - n_qkvz
    gdn_qkvz = gdn_in[:, :, :n_qkvz]
    gdn_ba = jnp.pad(gdn_in[:, :, n_qkvz:], ((0, 0), (0, 0), (0, 128 - n_ba)))
    dw_w = _gather_small("gather_hyb_dw_w", jnp.pad(w["hyb_dw_w"], ((0, 0), (0, _HALO - _CONV_WIDTH), (0, 0))))
    dw_w = jnp.transpose(dw_w, (0, 2, 1, 3)).reshape(n_hyb, _HALO, -1)
    conv_w = _gather_small("gather_gdn_conv_w", jnp.pad(w["gdn_conv_w"], ((0, 0), (0, _SHALO - _GDN_SHORT_CONV), (0, 0))))
    conv_w = jnp.transpose(conv_w, (0, 2, 1, 3)).reshape(n_gdn, _SHALO, -1)[:, :_GDN_SHORT_CONV]
    tabs = _rotary_tables(positions)

    def mixer_args(layer):
        i = layer // 2
        if layer % 2 == 0:
            return (hyb_in, hyb_out, dw_w[i], w["hyb_dw_b"][i][None], w["hyb_ln_g"][i][None], w["hyb_ln_b"][i][None], tabs, i)
        return (gdn_qkvz, gdn_ba, gdn_out, conv_w[i], _lane16(w["gdn_A_log"][i]), _lane16(w["gdn_dt_bias"][i]),
                w["gdn_norm_g"][i][None], i)

    h = xh
    saved = []
    for layer in range(n_layers):
        h, s1 = _ffn_fwd(h, w["ffn1_norm"][layer][None], ffn_in[1], ffn_out[1], layer)
        gamma = w["mix_norm"][layer][None]
        if layer % 2 == 0:
            h, sm = _hyb_fwd(h, gamma, *mixer_args(layer))
        else:
            h, sm = _gdn_fwd(h, gamma, *mixer_args(layer))
        h, s2 = _ffn_fwd(h, w["ffn2_norm"][layer][None], ffn_in[2], ffn_out[2], layer)
        saved.append((s1, sm, s2))
    dh, loss_row, d_final = _loss_head(h, w["final_norm"][None], target)

    bufs = dict(ffn1_w_in=None, ffn1_w_out=None, ffn2_w_in=None, ffn2_w_out=None, hyb_w_in=None, hyb_w_out=None,
                gdn_qkvz=None, gdn_ba=None, gdn_w_out=None)
    d_norm = {k: [None] * n_layers for k in ("ffn1_norm", "mix_norm", "ffn2_norm")}
    d_hyb = [None] * n_hyb
    d_gdn = [None] * n_gdn
    for layer in reversed(range(n_layers)):
        s1, sm, s2 = saved[layer]
        dh, d_norm["ffn2_norm"][layer], bufs["ffn2_w_in"], bufs["ffn2_w_out"] = _ffn_bwd(
            dh, s2, w["ffn2_norm"][layer][None], ffn_in[2], ffn_out[2], layer, n_layers, bufs["ffn2_w_in"], bufs["ffn2_w_out"])
        gamma = w["mix_norm"][layer][None]
        if layer % 2 == 0:
            args = mixer_args(layer)
            dh, d_norm["mix_norm"][layer], d_hyb[layer // 2], bufs["hyb_w_in"], bufs["hyb_w_out"] = _hyb_bwd(
                dh, sm, gamma, *args, n_hyb, bufs["hyb_w_in"], bufs["hyb_w_out"])
        else:
            args = mixer_args(layer)
            dh, d_norm["mix_norm"][layer], d_gdn[layer // 2], (bufs["gdn_qkvz"], bufs["gdn_ba"], bufs["gdn_w_out"]) = _gdn_bwd(
                dh, sm, gamma, *args, n_gdn, (bufs["gdn_qkvz"], bufs["gdn_ba"], bufs["gdn_w_out"]))
        dh, d_norm["ffn1_norm"][layer], bufs["ffn1_w_in"], bufs["ffn1_w_out"] = _ffn_bwd(
            dh, s1, w["ffn1_norm"][layer][None], ffn_in[1], ffn_out[1], layer, n_layers, bufs["ffn1_w_in"], bufs["ffn1_w_out"])
    grad_x = dh[None]

    d_gdn_in = jnp.concatenate([bufs["gdn_qkvz"], bufs["gdn_ba"][:, :, :n_ba]], axis=2)
    d_gdn_in = jnp.transpose(d_gdn_in.reshape(n_gdn, d_model, 4, gdn_cols), (0, 2, 1, 3))
    full_grads = dict(
        ffn1_w_in=bufs["ffn1_w_in"], ffn2_w_in=bufs["ffn2_w_in"], hyb_w_in=bufs["hyb_w_in"], gdn_w_in=d_gdn_in,
        ffn1_w_out=bufs["ffn1_w_out"].reshape(n_layers, 4, -1, d_model),
        ffn2_w_out=bufs["ffn2_w_out"].reshape(n_layers, 4, -1, d_model),
        hyb_w_out=bufs["hyb_w_out"].reshape(n_hyb, 4, -1, d_model),
        gdn_w_out=bufs["gdn_w_out"].reshape(n_gdn, 4, -1, d_model))
    grads = {name: _reduce_scatter("rs_" + name, full_grads[name], where) for name in _BIG}

    stack = lambda rows: jnp.concatenate(rows, axis=0)
    misc = jnp.concatenate([stack([g[1] for g in d_gdn])[:, _GDN_HEADS:2 * _GDN_HEADS].reshape(-1),
                            stack([g[2] for g in d_gdn])[:, _GDN_HEADS:2 * _GDN_HEADS].reshape(-1),
                            stack([g[3] for g in d_gdn]).reshape(-1), loss_row[0, :1]])
    n_a = n_gdn * _GDN_HEADS
    parts = [stack(d_norm["ffn1_norm"]), stack(d_norm["mix_norm"]), stack(d_norm["ffn2_norm"]), d_final,
             stack([g[1] for g in d_hyb]), stack([g[2] for g in d_hyb]), stack([g[3] for g in d_hyb]), misc,
             jnp.stack([g[0][:_CONV_WIDTH] for g in d_hyb]), jnp.stack([g[0] for g in d_gdn])]
    pack, where = _pack_rows(parts)
    red = _unpack_rows(_all_reduce_small(pack), where)
    misc = red[7]
    cw = w["hyb_dw_w"].shape[-1]
    gw = w["gdn_conv_w"].shape[-1]
    grads.update(
        ffn1_norm=red[0], mix_norm=red[1], ffn2_norm=red[2], final_norm=red[3].reshape(-1),
        hyb_dw_b=red[4], hyb_ln_g=red[5], hyb_ln_b=red[6],
        gdn_A_log=misc[:n_a].reshape(n_gdn, _GDN_HEADS), gdn_dt_bias=misc[n_a:2 * n_a].reshape(n_gdn, _GDN_HEADS),
        gdn_norm_g=misc[2 * n_a:2 * n_a + n_gdn * _GDN_DIM].reshape(n_gdn, _GDN_DIM),
        hyb_dw_w=lax.dynamic_slice_in_dim(red[8], chip * cw, cw, axis=2),
        gdn_conv_w=lax.dynamic_slice_in_dim(red[9], chip * gw, gw, axis=2))
    loss = misc[2 * n_a + n_gdn * _GDN_DIM]

    delta, new_m, new_v = {}, {}, {}
    for name in _BIG:
        delta[name], new_m[name], new_v[name] = _adamw(w[name], grads[name], m[name], v[name])
    small = [n for n in _WEIGHT_ORDER if n not in _BIG]
    packs = [_pack_rows([src[n] for n in small]) for src in (w, grads, m, v)]
    outs = _adamw(*[p[0] for p in packs])
    for res, o in zip((delta, new_m, new_v), outs):
        res.update(dict(zip(small, _unpack_rows(o, packs[0][1]))))
    return (loss, grad_x, *[grads[n] for n in _WEIGHT_ORDER], *[delta[n] for n in _WEIGHT_ORDER],
            *[new_m[n] for n in _WEIGHT_ORDER], *[new_v[n] for n in _WEIGHT_ORDER])
```

```python
import functools

import jax
import jax.numpy as jnp
from jax import lax
from jax.experimental import pallas as pl
from jax.experimental.pallas import tpu as pltpu

_BF = jnp.bfloat16
_F32 = jnp.float32
_NORM_EPS = 1e-6
_VMEM_LIMIT_BYTES = 56 * 1024 * 1024

_A_HEADS = 8
_A_HEAD_DIM = 64
_WIN_BLOCK = 128
_DILATIONS = (1, 4, 16)
_CONV_WIDTH = 31
_GDN_HEADS = 8
_GDN_DIM = 128
_GDN_CHUNK = 64
_GDN_SHORT_CONV = 4

_ADAM_LR = 0.001
_ADAM_B1 = 0.9
_ADAM_B2 = 0.999
_ADAM_EPS = 1e-08
_ADAM_WD = 0.01
_ADAM_STEP = 10


def _params(sem):
    return pltpu.CompilerParams(dimension_semantics=sem, vmem_limit_bytes=_VMEM_LIMIT_BYTES)


def _dot(a, b, dn=(((1,), (0,)), ((), ()))):
    return lax.dot_general(a.astype(_BF), b.astype(_BF), dn, preferred_element_type=_F32)


def _dot_nt(a, b):
    return _dot(a, b, (((1,), (1,)), ((), ())))


def _dot_tn(a, b):
    return _dot(a, b, (((0,), (0,)), ((), ())))


def _dot_hi(a, b):
    ah, bh = a.astype(_BF), b.astype(_BF)
    al, bl = (a - ah.astype(_F32)).astype(_BF), (b - bh.astype(_F32)).astype(_BF)
    dot = lambda u, w: jnp.dot(u, w, preferred_element_type=_F32)
    return dot(ah, bh) + (dot(ah, bl) + dot(al, bh))


_NN = (((1,), (0,)), ((), ()))
_NT = (((1,), (1,)), ((), ()))
_TN = (((0,), (0,)), ((), ()))
_ANY = pl.BlockSpec(memory_space=pl.ANY)


def _mm(name, a, b, *, grid, a_spec, b_spec, dn, out_shape, out_spec, red=None, acc_shape=None,
        pre_a=None, pre_b=None, epi=None, extras=(), extra_specs=(), into=None):
    n_ex = len(extras)
    n_in = 2 + n_ex + (1 if into is not None else 0)
    nk = grid[red] if red is not None else 1

    def body(*refs):
        a_ref, b_ref = refs[0], refs[1]
        ex = refs[2:2 + n_ex]
        o_ref = refs[n_in]
        av, bv = a_ref[...], b_ref[...]
        if pre_a is not None:
            av = pre_a(av)
        if pre_b is not None:
            bv = pre_b(bv)
        part = lax.dot_general(av.astype(_BF), bv.astype(_BF), dn, preferred_element_type=_F32)

        def finish(acc):
            res = epi(acc, *[e[...] for e in ex]) if epi is not None else acc
            if isinstance(res, tuple):
                for idx, val in enumerate(res):
                    o_ref[idx] = val.astype(o_ref.dtype)
            else:
                o_ref[...] = res.astype(o_ref.dtype)

        if red is None:
            finish(part)
        else:
            acc_ref = refs[n_in + 1]
            k = pl.program_id(red)

            @pl.when(k == 0)
            def _():
                acc_ref[...] = part

            @pl.when(k > 0)
            def _():
                acc_ref[...] += part

            @pl.when(k == nk - 1)
            def _():
                finish(acc_ref[...])

    in_specs = [a_spec, b_spec, *extra_specs]
    operands = [a, b, *extras]
    aliases = {}
    if into is not None:
        in_specs.append(_ANY)
        operands.append(into)
        aliases = {n_in - 1: 0}
    return pl.pallas_call(
        body, name=name, grid=grid, in_specs=in_specs, out_specs=out_spec, out_shape=out_shape,
        scratch_shapes=[pltpu.VMEM(acc_shape, _F32)] if red is not None else [],
        input_output_aliases=aliases,
        compiler_params=_params(("arbitrary",) * len(grid)),
    )(*operands)


def _rowmap(name, body, tiled, params, tile_outs, param_outs=(), *, rows, tm, ncol=1, col_outer=False,
            into=None):
    nt, npar, nto, npo = len(tiled), len(params), len(tile_outs), len(param_outs)
    assert rows % tm == 0, (name, rows, tm)
    nrow = rows // tm
    grid = (ncol, nrow) if col_outer else (nrow, ncol)

    def ij(g):
        return (g[1], g[0]) if col_outer else (g[0], g[1])

    def cidx(col, j):
        return col(j) if callable(col) else col

    in_specs, operands = [], []
    for arr, width, col in tiled:
        operands.append(arr)
        if arr.ndim == 3:
            in_specs.append(pl.BlockSpec((arr.shape[0], tm, width),
                                         lambda *g, col=col: (0, ij(g)[0], cidx(col, ij(g)[1]))))
        else:
            in_specs.append(pl.BlockSpec((tm, width), lambda *g, col=col: (ij(g)[0], cidx(col, ij(g)[1]))))
    for p in params:
        if isinstance(p, tuple):
            arr, width, col = p
            operands.append(arr)
            in_specs.append(pl.BlockSpec((arr.shape[0], width), lambda *g, col=col: (0, cidx(col, ij(g)[1]))))
        else:
            operands.append(p)
            in_specs.append(pl.BlockSpec(p.shape, lambda *g, nd=p.ndim: (0,) * nd))
    out_shape, out_specs = [], []
    for total, width, col, dt in tile_outs:
        out_shape.append(jax.ShapeDtypeStruct((rows, total), dt))
        out_specs.append(pl.BlockSpec((tm, width), lambda *g, col=col: (ij(g)[0], cidx(col, ij(g)[1]))))
    for shp in param_outs:
        out_shape.append(jax.ShapeDtypeStruct(shp, _F32))
        out_specs.append(pl.BlockSpec(shp, lambda *g, nd=len(shp): (0,) * nd))
    n_in = nt + npar
    aliases = {}
    if into is not None:
        operands.append(into[0])
        in_specs.append(_ANY)
        aliases = {n_in: into[1]}
        n_in += 1

    def kern(*refs):
        vals = [r[...] for r in refs[:nt + npar]]
        touts = refs[n_in:n_in + nto]
        pouts = refs[n_in + nto:]
        res = body(*vals)
        if not isinstance(res, (tuple, list)):
            res = (res,)
        for r, val in zip(touts, res[:nto]):
            r[...] = val.astype(r.dtype)
        if npo:
            @pl.when((pl.program_id(0) == 0) & (pl.program_id(1) == 0))
            def _():
                for r in pouts:
                    r[...] = jnp.zeros_like(r)

            for r, val in zip(pouts, res[nto:]):
                r[...] += val

    return pl.pallas_call(
        kern, name=name, grid=grid, in_specs=in_specs, out_specs=out_specs, out_shape=out_shape,
        input_output_aliases=aliases, compiler_params=_params(("arbitrary", "arbitrary")),
    )(*operands)


def _vjp_body(fn, n_in, diff, n_ct):
    def body(*vals):
        ins, cts = vals[:n_in], vals[n_in:n_in + n_ct]
        outs, pull = jax.vjp(fn, *ins)
        single = not isinstance(outs, (tuple, list))
        if single:
            grads = pull(cts[0].astype(outs.dtype))
        else:
            grads = pull(tuple(c.astype(o.dtype) for c, o in zip(cts, outs)))
        return tuple(grads[i] for i in diff)
    return body


_TM = 512


def _rms(h, gamma):
    hf = h.astype(_F32)
    return hf * lax.rsqrt(jnp.mean(hf * hf, axis=-1, keepdims=True) + _NORM_EPS) * gamma


def _rmsnorm_fwd(h, gamma):
    s, d = h.shape
    return _rowmap("rmsnorm_fwd", lambda hv, gv: _rms(hv, gv).astype(_BF), [(h, d, 0)], [gamma],
                   [(d, d, 0, _BF)], rows=s, tm=_TM)[0]


def _rmsnorm_bwd(h, gamma, dxn, dh_out):
    s, d = h.shape

    def body(hv, dxv, dhv, gv):
        _, pull = jax.vjp(_rms, hv, gv)
        dh, dg = pull(dxv.astype(_F32))
        return dhv + dh, dg

    return _rowmap("rmsnorm_bwd", body, [(h, d, 0), (dxn, d, 0), (dh_out, d, 0)], [gamma],
                   [(d, d, 0, _F32)], [(1, d)], rows=s, tm=_TM)


def _silu(x):
    return x * jax.nn.sigmoid(x)


def _ffn_up(xn, w_in, layer):
    s, d = xn.shape
    ns = w_in.shape[-1]

    def body(x_ref, wg_ref, wu_ref, g_ref, u_ref, a_ref):
        xv = x_ref[...]
        g = jnp.dot(xv, wg_ref[...], preferred_element_type=_F32)
        u = jnp.dot(xv, wu_ref[...], preferred_element_type=_F32)
        g_ref[...] = g.astype(_BF)
        u_ref[...] = u.astype(_BF)
        a_ref[...] = (_silu(g) * u).astype(_BF)

    out = pl.BlockSpec((_TM, ns), lambda j, i: (i, j))
    sds = jax.ShapeDtypeStruct((s, 2 * ns), _BF)
    return pl.pallas_call(
        body, name="ffn_up", grid=(2, s // _TM),
        in_specs=[pl.BlockSpec((_TM, d), lambda j, i: (i, 0)),
                  pl.BlockSpec((None, None, d, ns), lambda j, i: (layer, j, 0, 0)),
                  pl.BlockSpec((None, None, d, ns), lambda j, i: (layer, j + 2, 0, 0))],
        out_specs=[out, out, out], out_shape=[sds, sds, sds],
        compiler_params=_params(("arbitrary", "arbitrary")),
    )(xn, w_in, w_in)


def _proj_residual(name, y, w, layer, h, scale):
    s, k = y.shape
    d = w.shape[-1]
    return _mm(name, y, w, grid=(s // _TM,), a_spec=pl.BlockSpec((_TM, k), lambda i: (i, 0)),
               b_spec=pl.BlockSpec((None, k, d), lambda i: (layer, 0, 0)), dn=_NN,
               out_shape=jax.ShapeDtypeStruct((s, d), _F32), out_spec=pl.BlockSpec((_TM, d), lambda i: (i, 0)),
               epi=lambda acc, hv: hv + scale * acc, extras=(h,),
               extra_specs=(pl.BlockSpec((_TM, d), lambda i: (i, 0)),))


def _ffn_dact(dh, w_out, layer, g, u):
    s, d = dh.shape
    f = g.shape[1]
    ns = f // 2

    def epi(da, gv, uv):
        gf, uf = gv.astype(_F32), uv.astype(_F32)
        sg = jax.nn.sigmoid(gf)
        return da * uf * (sg * (1.0 + gf * (1.0 - sg))), da * gf * sg

    gu = pl.BlockSpec((_TM, ns), lambda n, i: (i, n))
    return _mm("ffn_dact", dh, w_out, grid=(2, s // _TM), a_spec=pl.BlockSpec((_TM, d), lambda n, i: (i, 0)),
               b_spec=pl.BlockSpec((None, ns, d), lambda n, i: (layer, n, 0)), dn=_NT,
               out_shape=jax.ShapeDtypeStruct((2, s, f), _BF), out_spec=pl.BlockSpec((2, _TM, ns), lambda n, i: (0, i, n)),
               pre_a=lambda v: 0.5 * v, epi=epi, extras=(g, u), extra_specs=(gu, gu))


def _wgrad_rows(name, y, dh, layer, n_layers, buf, scale):
    s, k = y.shape
    d = dh.shape[1]
    tk = k // 2 if (k // 2) % 128 == 0 else k
    return _mm(name, y, dh, grid=(k // tk, s // _TM), a_spec=pl.BlockSpec((_TM, tk), lambda n, i: (i, n)),
               b_spec=pl.BlockSpec((_TM, d), lambda n, i: (i, 0)), dn=_TN, red=1, acc_shape=(tk, d),
               out_shape=jax.ShapeDtypeStruct((n_layers, k, d), _BF),
               out_spec=pl.BlockSpec((None, tk, d), lambda n, i: (layer, n, 0)),
               pre_b=(lambda v: scale * v) if scale != 1.0 else None, into=buf)


def _ffn_dxn(dz, w_in, layer):
    _, s, f = dz.shape
    d, ns = w_in.shape[2], w_in.shape[3]
    return _mm("ffn_dxn", dz, w_in, grid=(s // _TM, 4),
               a_spec=pl.BlockSpec((None, _TM, ns), lambda i, j: (j // 2, i, j % 2)),
               b_spec=pl.BlockSpec((None, None, d, ns), lambda i, j: (layer, j, 0, 0)), dn=_NT, red=1,
               acc_shape=(_TM, d), out_shape=jax.ShapeDtypeStruct((s, d), _F32),
               out_spec=pl.BlockSpec((_TM, d), lambda i, j: (i, 0)))


def _ffn_dwin(xn, dz, layer, n_layers, buf):
    s, d = xn.shape
    ns = dz.shape[2] // 2
    return _mm("ffn_dwin", xn, dz, grid=(4, s // _TM), a_spec=pl.BlockSpec((_TM, d), lambda j, i: (i, 0)),
               b_spec=pl.BlockSpec((None, _TM, ns), lambda j, i: (j // 2, i, j % 2)), dn=_TN, red=1,
               acc_shape=(d, ns), out_shape=jax.ShapeDtypeStruct((n_layers, 4, d, ns), _BF),
               out_spec=pl.BlockSpec((None, None, d, ns), lambda j, i: (layer, j, 0, 0)), into=buf)


def _ffn_fwd(h, gamma, w_in, w_out, layer):
    xn = _rmsnorm_fwd(h, gamma)
    g, u, act = _ffn_up(xn, w_in, layer)
    h_new = _proj_residual("ffn_down", act, w_out, layer, h, 0.5)
    return h_new, (h, xn, g, u, act)


def _ffn_bwd(dh, saved, gamma, w_in, w_out, layer, n_layers, dwin_buf, dwout_buf):
    h, xn, g, u, act = saved
    dz = _ffn_dact(dh, w_out, layer, g, u)
    dwout_buf = _wgrad_rows("ffn_dwout", act, dh, layer, n_layers, dwout_buf, 0.5)
    dxn = _ffn_dxn(dz, w_in, layer)
    dwin_buf = _ffn_dwin(xn, dz, layer, n_layers, dwin_buf)
    dh_in, dgamma = _rmsnorm_bwd(h, gamma, dxn, dh)
    return dh_in, dgamma, dwin_buf, dwout_buf


def _loss_head(h, gamma, target):
    s, d = h.shape

    def body(hv, tv, gv):
        y, pull = jax.vjp(_rms, hv, gv)
        err = y - tv
        loss = 0.5 * jnp.sum(jnp.mean(err * err, axis=-1, keepdims=True), axis=0, keepdims=True)
        dh, dg = pull(err * (1.0 / d))
        return dh, jnp.broadcast_to(loss, (1, 128)), dg

    return _rowmap("loss_head", body, [(h, d, 0), (target, d, 0)], [gamma], [(d, d, 0, _F32)],
                   [(1, 128), (1, d)], rows=s, tm=_TM)


_ROT = _A_HEAD_DIM // 4
_ROPE_THETA = 500000.0
_A_WIDTH = _A_HEADS * _A_HEAD_DIM
_NEG = -1e30


def _rotary_tables(positions):
    inv_freq = jnp.power(jnp.float32(_ROPE_THETA), -jnp.arange(0, _ROT, 2, dtype=_F32) / _ROT)
    ang = positions.reshape(-1, 1).astype(_F32) * inv_freq
    cos, sin = jnp.cos(ang), jnp.sin(ang)
    s = ang.shape[0]
    pad = jnp.zeros((s, _A_HEAD_DIM - _ROT), _F32)
    c_head = jnp.concatenate([cos, cos, pad + 1.0], axis=1)
    s_head = jnp.concatenate([-sin, sin, pad], axis=1)
    c2, s2 = jnp.tile(c_head, (1, 2)), jnp.tile(s_head, (1, 2))
    scale = _A_HEAD_DIM ** -0.5
    tab_c = jnp.stack([scale * c2, c2, jnp.ones_like(c2)])
    tab_s = jnp.stack([scale * s2, s2, jnp.zeros_like(s2)])
    return tab_c, tab_s


def _pair_swap(z):
    lane = lax.broadcasted_iota(jnp.int32, z.shape, 1) % _A_HEAD_DIM
    half = _ROT // 2
    up = jnp.where(lane < _ROT, pltpu.roll(z, half, axis=1), 0.0)
    return jnp.where(lane < half, pltpu.roll(z, 128 - half, axis=1), up)


def _rot_apply(x, c, s, transpose):
    outs = []
    for k in range(x.shape[1] // 128):
        xs = x[:, k * 128:(k + 1) * 128]
        outs.append(c * xs + (_pair_swap(s * xs) if transpose else s * _pair_swap(xs)))
    return jnp.concatenate(outs, axis=1)


def _qkv_prep(proj, tab_c, tab_s):
    s = proj.shape[0]
    w = _A_WIDTH

    def body(x_ref, c_ref, s_ref, o_ref):
        o_ref[...] = _rot_apply(x_ref[...], c_ref[...], s_ref[...], False).astype(_BF)

    tab = pl.BlockSpec((None, _TM, 128), lambda i, j: (j, i, 0))
    blk = pl.BlockSpec((_TM, w), lambda i, j: (i, j))
    return pl.pallas_call(
        body, name="qkv_prep", grid=(s // _TM, 3), in_specs=[blk, tab, tab], out_specs=blk,
        out_shape=jax.ShapeDtypeStruct((s, 3 * w), _BF), compiler_params=_params(("arbitrary", "arbitrary")),
    )(proj, tab_c, tab_s)


def _head_masks():
    lane = lax.broadcasted_iota(jnp.int32, (_WIN_BLOCK, 128), 1)
    return [lane // _A_HEAD_DIM == hh for hh in range(2)]


def _band_masks(first_block):
    qi = lax.broadcasted_iota(jnp.int32, (_WIN_BLOCK, _WIN_BLOCK), 0)
    kj = lax.broadcasted_iota(jnp.int32, (_WIN_BLOCK, _WIN_BLOCK), 1)
    return (kj >= qi) & jnp.logical_not(first_block), kj <= qi


def _attn_fwd(qkv, dil):
    s = qkv.shape[0]
    w = _A_WIDTH
    length = s // dil
    nb = length // _WIN_BLOCK
    view = qkv.reshape(length, dil * 3 * w)

    def body(q_ref, kp_ref, kc_ref, vp_ref, vc_ref, o_ref, l_ref):
        allow_p, allow_c = _band_masks(pl.program_id(1) == 0)
        hm = _head_masks()
        for sl in range(w // 128):
            cols = slice(sl * 128, (sl + 1) * 128)
            q, kp, kc, vp, vc = q_ref[:, cols], kp_ref[:, cols], kc_ref[:, cols], vp_ref[:, cols], vc_ref[:, cols]
            o_s = jnp.zeros((_WIN_BLOCK, 128), _F32)
            l_s = jnp.zeros((_WIN_BLOCK, 128), _F32)
            for hh in range(2):
                qh = jnp.where(hm[hh], q, jnp.zeros_like(q))
                sp = jnp.where(allow_p, _dot_nt(qh, kp), _NEG)
                sc = jnp.where(allow_c, _dot_nt(qh, kc), _NEG)
                m = jnp.maximum(jnp.max(sp, axis=1, keepdims=True), jnp.max(sc, axis=1, keepdims=True))
                pp, pc = jnp.exp(sp - m), jnp.exp(sc - m)
                den = jnp.sum(pp, axis=1, keepdims=True) + jnp.sum(pc, axis=1, keepdims=True)
                oh = (_dot(pp, vp) + _dot(pc, vc)) / den
                o_s = jnp.where(hm[hh], oh, o_s)
                l_s = jnp.where(hm[hh], m + jnp.log(den), l_s)
            o_ref[:, cols] = o_s
            l_ref[:, cols] = l_s

    def at(col, prev):
        if prev:
            return pl.BlockSpec((_WIN_BLOCK, w), lambda r, n: (jnp.maximum(n - 1, 0), 3 * r + col))
        return pl.BlockSpec((_WIN_BLOCK, w), lambda r, n: (n, 3 * r + col))

    out = pl.BlockSpec((_WIN_BLOCK, w), lambda r, n: (n, r))
    sds = jax.ShapeDtypeStruct((length, dil * w), _F32)
    o, lse = pl.pallas_call(
        body, name=f"attn_fwd_d{dil}", grid=(dil, nb),
        in_specs=[at(0, False), at(1, True), at(1, False), at(2, True), at(2, False)],
        out_specs=[out, out], out_shape=[sds, sds], compiler_params=_params(("arbitrary", "arbitrary")),
    )(view, view, view, view, view)
    return o.reshape(s, w), lse.reshape(s, w)


def _attn_bwd(qkv, o, lse, do, dlse, dil):
    s = qkv.shape[0]
    w = _A_WIDTH
    length = s // dil
    nb = length // _WIN_BLOCK
    view = qkv.reshape(length, dil * 3 * w)
    o, lse, do, dlse = (t.reshape(length, dil * w) for t in (o, lse, do, dlse))

    def body(q_ref, kp_ref, kc_ref, vp_ref, vc_ref, o_ref, l_ref, do_ref, dl_ref, out_ref,
             car_q, car_k, car_v, cur_q, cur_kc, cur_vc, cur_kp, cur_vp):
        n = pl.program_id(1)

        @pl.when(n == 0)
        def _():
            car_q[...] = jnp.zeros_like(car_q)
            car_k[...] = jnp.zeros_like(car_k)
            car_v[...] = jnp.zeros_like(car_v)

        @pl.when(n == nb)
        def _():
            cur_kp[...] = jnp.zeros_like(cur_kp)
            cur_vp[...] = jnp.zeros_like(cur_vp)

        @pl.when(n < nb)
        def _():
            allow_p, allow_c = _band_masks(n == 0)
            hm = _head_masks()
            for sl in range(w // 128):
                cols = slice(sl * 128, (sl + 1) * 128)
                q, kp, kc, vp, vc = q_ref[:, cols], kp_ref[:, cols], kc_ref[:, cols], vp_ref[:, cols], vc_ref[:, cols]
                ov, lv, dov, dlv = o_ref[:, cols], l_ref[:, cols], do_ref[:, cols], dl_ref[:, cols]
                zero = jnp.zeros((_WIN_BLOCK, 128), _F32)
                dq_s, dkp_s, dkc_s, dvp_s, dvc_s = zero, zero, zero, zero, zero
                for hh in range(2):
                    qh = jnp.where(hm[hh], q, jnp.zeros_like(q))
                    doh = jnp.where(hm[hh], dov, 0.0).astype(_BF)
                    lse_h = jnp.sum(jnp.where(hm[hh], lv, 0.0), axis=1, keepdims=True) * (1.0 / _A_HEAD_DIM)
                    delta = jnp.sum(jnp.where(hm[hh], dov * ov, 0.0), axis=1, keepdims=True)
                    dl = jnp.sum(jnp.where(hm[hh], dlv, 0.0), axis=1, keepdims=True)
                    pp = jnp.exp(jnp.where(allow_p, _dot_nt(qh, kp), _NEG) - lse_h)
                    pc = jnp.exp(jnp.where(allow_c, _dot_nt(qh, kc), _NEG) - lse_h)
                    dsp = (pp * (_dot_nt(doh, vp) - delta + dl)).astype(_BF)
                    dsc = (pc * (_dot_nt(doh, vc) - delta + dl)).astype(_BF)
                    dq_s = dq_s + jnp.where(hm[hh], _dot(dsp, kp) + _dot(dsc, kc), 0.0)
                    dkp_s = dkp_s + _dot_tn(dsp, qh)
                    dkc_s = dkc_s + _dot_tn(dsc, qh)
                    dvp_s = dvp_s + _dot_tn(pp, doh)
                    dvc_s = dvc_s + _dot_tn(pc, doh)
                cur_q[:, cols] = dq_s
                cur_kp[:, cols] = dkp_s
                cur_kc[:, cols] = dkc_s
                cur_vp[:, cols] = dvp_s
                cur_vc[:, cols] = dvc_s

        out_ref[:, 0:w] = car_q[...]
        out_ref[:, w:2 * w] = car_k[...] + cur_kp[...]
        out_ref[:, 2 * w:3 * w] = car_v[...] + cur_vp[...]

        @pl.when(n < nb)
        def _():
            car_q[...] = cur_q[...]
            car_k[...] = cur_kc[...]
            car_v[...] = cur_vc[...]

    def at(col, prev):
        if prev:
            return pl.BlockSpec((_WIN_BLOCK, w), lambda r, n: (jnp.maximum(jnp.minimum(n, nb - 1) - 1, 0), 3 * r + col))
        return pl.BlockSpec((_WIN_BLOCK, w), lambda r, n: (jnp.minimum(n, nb - 1), 3 * r + col))

    blk = pl.BlockSpec((_WIN_BLOCK, w), lambda r, n: (jnp.minimum(n, nb - 1), r))
    scratch = [pltpu.VMEM((_WIN_BLOCK, w), _F32) for _ in range(8)]
    out = pl.pallas_call(
        body, name=f"attn_bwd_d{dil}", grid=(dil, nb + 1),
        in_specs=[at(0, False), at(1, True), at(1, False), at(2, True), at(2, False), blk, blk, blk, blk],
        out_specs=pl.BlockSpec((_WIN_BLOCK, 3 * w), lambda r, n: (jnp.maximum(n - 1, 0), r)),
        out_shape=jax.ShapeDtypeStruct((length, dil * 3 * w), _F32), scratch_shapes=scratch,
        compiler_params=_params(("arbitrary", "arbitrary")),
    )(view, view, view, view, view, o, lse, do, dlse)
    return out.reshape(s, 3 * w)


def _mix(o1, o2, o3, l1, l2, l3):
    m = jnp.maximum(jnp.maximum(l1, l2), l3)
    e1, e2, e3 = jnp.exp(l1 - m), jnp.exp(l2 - m), jnp.exp(l3 - m)
    return (e1 * o1 + e2 * o2 + e3 * o3) / (e1 + e2 + e3)


def _attn_mix_fwd(outs, lses):
    s = outs[0].shape[0]
    tiles = [(t, _A_WIDTH, 0) for t in (*outs, *lses)]
    return _rowmap("attn_mix_fwd", _mix, tiles, [], [(2 * _A_WIDTH, _A_WIDTH, 0, _BF)], rows=s, tm=_CONV_TM)[0]


def _attn_mix_bwd(outs, lses, dy):
    s = outs[0].shape[0]
    tiles = [(t, _A_WIDTH, 0) for t in (*outs, *lses, dy)]
    body = _vjp_body(_mix, 6, range(6), 1)
    return _rowmap("attn_mix_bwd", body, tiles, [], [(_A_WIDTH, _A_WIDTH, 0, _F32)] * 6, rows=s, tm=_CONV_TM)


_CONV_TM = 256
_HALO = 32


def _ln_silu(pre, g, b):
    mu = jnp.mean(pre, axis=-1, keepdims=True)
    xc = pre - mu
    y = xc * lax.rsqrt(jnp.mean(xc * xc, axis=-1, keepdims=True) + _NORM_EPS)
    return _silu(y * g + b)


def _glu_into(buf, ua_ref, ub_ref, uah_ref, ubh_ref, first):
    halo = uah_ref[...] * jax.nn.sigmoid(ubh_ref[...])
    buf[0:_HALO, :] = jnp.where(first, 0.0, halo)
    buf[_HALO:, :] = ua_ref[...] * jax.nn.sigmoid(ub_ref[...])


def _causal_taps(buf, w_ref, rows):
    base = _HALO - (_CONV_WIDTH - 1)
    acc = jnp.zeros((rows, buf.shape[1]), _F32)
    for j in range(_CONV_WIDTH):
        acc = acc + w_ref[j:j + 1, :] * buf[base + j:base + j + rows, :]
    return acc


def _conv_specs(s):
    tm = _CONV_TM
    per = tm // _HALO
    cw = _A_WIDTH
    cur = lambda col: pl.BlockSpec((tm, cw), lambda i: (i, col))
    prev = lambda col: pl.BlockSpec((_HALO, cw), lambda i: (jnp.maximum(i * per - 1, 0), col))
    return tm, per, cw, cur, prev


def _full2(arr):
    return pl.BlockSpec(arr.shape, lambda i: (0, 0))


def _conv_fwd(proj, dw_w, dw_b, ln_g, ln_b, y):
    s = proj.shape[0]
    tm, per, cw, cur, prev = _conv_specs(s)

    def body(ua_ref, ub_ref, uah_ref, ubh_ref, w_ref, b_ref, g_ref, be_ref, y_in, o_ref, buf):
        del y_in
        _glu_into(buf, ua_ref, ub_ref, uah_ref, ubh_ref, pl.program_id(0) == 0)
        pre = _causal_taps(buf, w_ref, tm) + b_ref[...]
        o_ref[...] = _ln_silu(pre, g_ref[...], be_ref[...]).astype(_BF)

    return pl.pallas_call(
        body, name="conv_fwd", grid=(s // tm,),
        in_specs=[cur(3), cur(4), prev(3), prev(4), _full2(dw_w), _full2(dw_b), _full2(ln_g), _full2(ln_b), _ANY],
        out_specs=pl.BlockSpec((tm, cw), lambda i: (i, 1)), out_shape=jax.ShapeDtypeStruct(y.shape, y.dtype),
        scratch_shapes=[pltpu.VMEM((tm + _HALO, cw), _F32)], input_output_aliases={8: 0},
        compiler_params=_params(("arbitrary",)),
    )(proj, proj, proj, proj, dw_w, dw_b, ln_g, ln_b, y)


def _conv_bwd_pre(proj, dw_w, dw_b, ln_g, ln_b, dy):
    s = proj.shape[0]
    tm, per, cw, cur, prev = _conv_specs(s)

    def body(ua_ref, ub_ref, uah_ref, ubh_ref, w_ref, b_ref, g_ref, be_ref, dy_ref, dp_ref, dg_ref, dbe_ref, db_ref, buf):
        i = pl.program_id(0)
        _glu_into(buf, ua_ref, ub_ref, uah_ref, ubh_ref, i == 0)
        pre = _causal_taps(buf, w_ref, tm) + b_ref[...]
        _, pull = jax.vjp(_ln_silu, pre, g_ref[...], be_ref[...])
        dpre, dg, dbe = pull(dy_ref[...])
        dp_ref[...] = dpre

        @pl.when(i == 0)
        def _():
            dg_ref[...] = jnp.zeros_like(dg_ref)
            dbe_ref[...] = jnp.zeros_like(dbe_ref)
            db_ref[...] = jnp.zeros_like(db_ref)

        dg_ref[...] += dg
        dbe_ref[...] += dbe
        db_ref[...] += jnp.sum(dpre, axis=0, keepdims=True)

    vec = pl.BlockSpec((1, cw), lambda i: (0, 0))
    vsd = jax.ShapeDtypeStruct((1, cw), _F32)
    return pl.pallas_call(
        body, name="conv_bwd_pre", grid=(s // tm,),
        in_specs=[cur(3), cur(4), prev(3), prev(4), _full2(dw_w), _full2(dw_b), _full2(ln_g), _full2(ln_b), cur(1)],
        out_specs=[pl.BlockSpec((tm, cw), lambda i: (i, 0)), vec, vec, vec],
        out_shape=[jax.ShapeDtypeStruct((s, cw), _F32), vsd, vsd, vsd],
        scratch_shapes=[pltpu.VMEM((tm + _HALO, cw), _F32)], compiler_params=_params(("arbitrary",)),
    )(proj, proj, proj, proj, dw_w, dw_b, ln_g, ln_b, dy)


def _conv_bwd_taps(proj, dw_w, dpre):
    s = proj.shape[0]
    tm, per, cw, cur, prev = _conv_specs(s)
    nt = s // tm

    def body(ua_ref, ub_ref, uah_ref, ubh_ref, w_ref, dp_ref, dpn_ref, dglu_ref, dw_ref, buf, dbuf):
        i = pl.program_id(0)
        _glu_into(buf, ua_ref, ub_ref, uah_ref, ubh_ref, i == 0)
        dpre = dp_ref[...]
        dbuf[0:tm, :] = dpre
        dbuf[tm:, :] = jnp.where(i == nt - 1, 0.0, dpn_ref[...])
        acc = jnp.zeros((tm, cw), _F32)
        for j in range(_CONV_WIDTH):
            off = _CONV_WIDTH - 1 - j
            acc = acc + w_ref[j:j + 1, :] * dbuf[off:off + tm, :]
        dglu_ref[...] = acc

        @pl.when(i == 0)
        def _():
            dw_ref[...] = jnp.zeros_like(dw_ref)

        base = _HALO - (_CONV_WIDTH - 1)
        for j in range(_CONV_WIDTH):
            dw_ref[j:j + 1, :] += jnp.sum(dpre * buf[base + j:base + j + tm, :], axis=0, keepdims=True)

    nxt = pl.BlockSpec((_HALO, cw), lambda i: (jnp.minimum((i + 1) * per, s // _HALO - 1), 0))
    return pl.pallas_call(
        body, name="conv_bwd_taps", grid=(nt,),
        in_specs=[cur(3), cur(4), prev(3), prev(4), _full2(dw_w), pl.BlockSpec((tm, cw), lambda i: (i, 0)), nxt],
        out_specs=[pl.BlockSpec((tm, cw), lambda i: (i, 0)), pl.BlockSpec((_HALO, cw), lambda i: (0, 0))],
        out_shape=[jax.ShapeDtypeStruct((s, cw), _F32), jax.ShapeDtypeStruct((_HALO, cw), _F32)],
        scratch_shapes=[pltpu.VMEM((tm + _HALO, cw), _F32), pltpu.VMEM((tm + _HALO, cw), _F32)],
        compiler_params=_params(("arbitrary",)),
    )(proj, proj, proj, proj, dw_w, dpre, dpre)


def _hyb_dproj(dqkv, tab_c, tab_s, dglu, proj):
    s = proj.shape[0]
    w = _A_WIDTH

    def body(d1_ref, d2_ref, d3_ref, c_ref, s_ref, dg_ref, ua_ref, ub_ref, o_ref):
        j = pl.program_id(1)

        @pl.when(j < 3)
        def _():
            x = d1_ref[...] + d2_ref[...] + d3_ref[...]
            o_ref[...] = _rot_apply(x, c_ref[...], s_ref[...], True).astype(_BF)

        @pl.when(j == 3)
        def _():
            o_ref[...] = (dg_ref[...] * jax.nn.sigmoid(ub_ref[...])).astype(_BF)

        @pl.when(j == 4)
        def _():
            sg = jax.nn.sigmoid(ub_ref[...])
            o_ref[...] = (dg_ref[...] * ua_ref[...] * sg * (1.0 - sg)).astype(_BF)

    dq = pl.BlockSpec((_TM, w), lambda i, j: (i, jnp.minimum(j, 2)))
    tab = pl.BlockSpec((None, _TM, 128), lambda i, j: (jnp.minimum(j, 2), i, 0))
    col = lambda c: pl.BlockSpec((_TM, w), lambda i, j: (i, c))
    return pl.pallas_call(
        body, name="hyb_dproj", grid=(s // _TM, 5),
        in_specs=[dq, dq, dq, tab, tab, col(0), col(3), col(4)],
        out_specs=pl.BlockSpec((_TM, w), lambda i, j: (i, j)),
        out_shape=jax.ShapeDtypeStruct((s, 5 * w), _BF), compiler_params=_params(("arbitrary", "arbitrary")),
    )(*dqkv, tab_c, tab_s, dglu, proj, proj)


def _proj_in(name, xn, w, layer):
    s, d = xn.shape
    ns = w.shape[-1]
    return _mm(name, xn, w, grid=(4, s // _TM), a_spec=pl.BlockSpec((_TM, d), lambda j, i: (i, 0)),
               b_spec=pl.BlockSpec((None, None, d, ns), lambda j, i: (layer, j, 0, 0)), dn=_NN,
               out_shape=jax.ShapeDtypeStruct((s, 4 * ns), _F32), out_spec=pl.BlockSpec((_TM, ns), lambda j, i: (i, j)))


def _proj_in_dx(name, dz, w, layer):
    s = dz.shape[0]
    d, ns = w.shape[2], w.shape[3]
    return _mm(name, dz, w, grid=(s // _TM, 4), a_spec=pl.BlockSpec((_TM, ns), lambda i, j: (i, j)),
               b_spec=pl.BlockSpec((None, None, d, ns), lambda i, j: (layer, j, 0, 0)), dn=_NT, red=1,
               acc_shape=(_TM, d), out_shape=jax.ShapeDtypeStruct((s, d), _F32),
               out_spec=pl.BlockSpec((_TM, d), lambda i, j: (i, 0)))


def _proj_in_dw(name, xn, dz, layer, n_layers, buf):
    s, d = xn.shape
    ns = dz.shape[1] // 4
    return _mm(name, xn, dz, grid=(4, s // _TM), a_spec=pl.BlockSpec((_TM, d), lambda j, i: (i, 0)),
               b_spec=pl.BlockSpec((_TM, ns), lambda j, i: (i, j)), dn=_TN, red=1, acc_shape=(d, ns),
               out_shape=jax.ShapeDtypeStruct((n_layers, 4, d, ns), _BF),
               out_spec=pl.BlockSpec((None, None, d, ns), lambda j, i: (layer, j, 0, 0)), into=buf)


def _proj_out_dy(name, dh, w, layer):
    s, d = dh.shape
    k = w.shape[1]
    return _mm(name, dh, w, grid=(s // _TM,), a_spec=pl.BlockSpec((_TM, d), lambda i: (i, 0)),
               b_spec=pl.BlockSpec((None, k, d), lambda i: (layer, 0, 0)), dn=_NT,
               out_shape=jax.ShapeDtypeStruct((s, k), _F32), out_spec=pl.BlockSpec((_TM, k), lambda i: (i, 0)))


def _hyb_fwd(h, gamma, w_in, w_out, dw_w, dw_b, ln_g, ln_b, tabs, li):
    hn = _rmsnorm_fwd(h, gamma)
    proj = _proj_in("hyb_in", hn, w_in, li)
    qkv = _qkv_prep(proj, *tabs)
    branches = [_attn_fwd(qkv, dil) for dil in _DILATIONS]
    outs, lses = [b[0] for b in branches], [b[1] for b in branches]
    y = _attn_mix_fwd(outs, lses)
    y = _conv_fwd(proj, dw_w, dw_b, ln_g, ln_b, y)
    h_new = _proj_residual("hyb_out", y, w_out, li, h, 1.0)
    return h_new, (h, hn, proj, qkv, outs, lses, y)


def _hyb_bwd(dh, saved, gamma, w_in, w_out, dw_w, dw_b, ln_g, ln_b, tabs, li, n_layers, dwin_buf, dwout_buf):
    h, hn, proj, qkv, outs, lses, y = saved
    dy = _proj_out_dy("hyb_dy", dh, w_out, li)
    dwout_buf = _wgrad_rows("hyb_dwout", y, dh, li, n_layers, dwout_buf, 1.0)
    mixg = _attn_mix_bwd(outs, lses, dy)
    dqkv = [_attn_bwd(qkv, outs[b], lses[b], mixg[b], mixg[3 + b], dil) for b, dil in enumerate(_DILATIONS)]
    dpre, d_ln_g, d_ln_b, d_dw_b = _conv_bwd_pre(proj, dw_w, dw_b, ln_g, ln_b, dy)
    dglu, d_dw_w = _conv_bwd_taps(proj, dw_w, dpre)
    dproj = _hyb_dproj(dqkv, *tabs, dglu, proj)
    dhn = _proj_in_dx("hyb_dhn", dproj, w_in, li)
    dwin_buf = _proj_in_dw("hyb_dwin", hn, dproj, li, n_layers, dwin_buf)
    dh_in, dgamma = _rmsnorm_bwd(h, gamma, dhn, dh)
    return dh_in, dgamma, (d_dw_w, d_dw_b, d_ln_g, d_ln_b), dwin_buf, dwout_buf


_GDN_BLOCK = 64
_GDN_HEADS_PER_STEP = 8


def _gdn_chunk(qs, ks, vs, gates, states, heads):
    n = len(qs)
    hs = range(n)
    c = qs[0].shape[0]
    lane = lax.broadcasted_iota(jnp.int32, gates.shape, 1)
    row = lax.broadcasted_iota(jnp.int32, (c, c), 0)
    col = lax.broadcasted_iota(jnp.int32, (c, c), 1)
    rowc = lax.broadcasted_iota(jnp.int32, (c, 1), 0)
    causal = row >= col
    eye = jnp.where(row == col, 1.0, 0.0)
    beta = [jnp.sum(jnp.where(lane == heads[h], gates, 0.0), axis=1, keepdims=True) for h in hs]
    g = [jnp.sum(jnp.where(lane == heads[h] + _GDN_HEADS, gates, 0.0), axis=1, keepdims=True) for h in hs]
    gc_row = [jnp.sum(jnp.where(row <= col, g[h], 0.0), axis=0, keepdims=True) for h in hs]
    gc_col = [jnp.sum(jnp.where(row == col, gc_row[h], 0.0), axis=1, keepdims=True) for h in hs]
    decay = [jnp.where(causal, jnp.exp(jnp.where(causal, gc_col[h] - gc_row[h], 0.0)), 0.0) for h in hs]
    kb = [ks[h] * beta[h] for h in hs]
    l_mat = [jnp.where(row > col, _dot_nt(kb[h], ks[h]) * decay[h], 0.0) for h in hs]
    t_inv = [eye - l_mat[h] for h in hs]
    x = [_dot_hi(l_mat[h], l_mat[h]) for h in hs]
    for it in range(5):
        t_inv = [t_inv[h] + _dot_hi(t_inv[h], x[h]) for h in hs]
        if it < 4:
            x = [_dot_hi(x[h], x[h]) for h in hs]
    eg = [jnp.exp(gc_col[h]) for h in hs]
    u = [_dot(t_inv[h], vs[h] * beta[h]) for h in hs]
    w = [_dot(t_inv[h], kb[h] * eg[h]) for h in hs]
    attn = [jnp.where(causal, _dot_nt(qs[h], ks[h]) * decay[h], 0.0) for h in hs]
    g_last = [jnp.sum(jnp.where(rowc == c - 1, gc_col[h], 0.0), axis=0, keepdims=True) for h in hs]
    k_dec = [ks[h] * jnp.exp(g_last[h] - gc_col[h]) for h in hs]
    v_new = [u[h] - _dot(w[h], states[h]) for h in hs]
    o = [_dot(qs[h] * eg[h], states[h]) + _dot(attn[h], v_new[h]) for h in hs]
    new_states = [states[h] * jnp.exp(g_last[h]) + _dot_tn(k_dec[h], v_new[h]) for h in hs]
    return o, new_states


def _gdn_chunk_fwd(q, k, v, gates):
    s = q.shape[0]
    cb, c = _GDN_BLOCK, _GDN_CHUNK
    nblk, per = s // cb, cb // c

    hps = _GDN_HEADS_PER_STEP

    def body(q_ref, k_ref, v_ref, g_ref, o_ref, st_ref, state):
        @pl.when(pl.program_id(1) == 0)
        def _():
            state[...] = jnp.zeros_like(state)

        heads = [pl.program_id(0) * hps + hh for hh in range(hps)]
        cols = [slice(hh * _GDN_DIM, (hh + 1) * _GDN_DIM) for hh in range(hps)]
        for ci in range(per):
            rows = slice(ci * c, (ci + 1) * c)
            sts = [state[hh] for hh in range(hps)]
            for hh in range(hps):
                st_ref[hh, ci] = sts[hh]
            outs, new = _gdn_chunk([q_ref[rows, cl] for cl in cols], [k_ref[rows, cl] for cl in cols],
                                   [v_ref[rows, cl] for cl in cols], g_ref[rows, :], sts, heads)
            for hh in range(hps):
                o_ref[rows, cols[hh]] = outs[hh]
                state[hh] = new[hh]

    slab = pl.BlockSpec((cb, hps * _GDN_DIM), lambda h, i: (i, h))
    return pl.pallas_call(
        body, name="gdn_chunk_fwd", grid=(_GDN_HEADS // hps, nblk),
        in_specs=[slab, slab, slab, pl.BlockSpec((cb, 128), lambda h, i: (i, 0))],
        out_specs=[slab, pl.BlockSpec((hps, per, _GDN_DIM, _GDN_DIM), lambda h, i: (h, i, 0, 0))],
        out_shape=[jax.ShapeDtypeStruct((s, _GDN_HEADS * _GDN_DIM), _F32),
                   jax.ShapeDtypeStruct((_GDN_HEADS, s // c, _GDN_DIM, _GDN_DIM), _F32)],
        scratch_shapes=[pltpu.VMEM((hps, _GDN_DIM, _GDN_DIM), _F32)],
        compiler_params=_params(("arbitrary", "arbitrary")),
    )(q, k, v, gates)


def _gdn_chunk_bwd(q, k, v, gates, states, do):
    s = q.shape[0]
    cb, c = _GDN_BLOCK, _GDN_CHUNK
    nblk, per = s // cb, cb // c

    hps = _GDN_HEADS_PER_STEP

    def body(q_ref, k_ref, v_ref, g_ref, st_ref, do_ref, dq_ref, dk_ref, dv_ref, dg_ref, dstate):
        @pl.when(pl.program_id(1) == 0)
        def _():
            dstate[...] = jnp.zeros_like(dstate)

        heads = [pl.program_id(0) * hps + hh for hh in range(hps)]
        cols = [slice(hh * _GDN_DIM, (hh + 1) * _GDN_DIM) for hh in range(hps)]
        for ci in reversed(range(per)):
            rows = slice(ci * c, (ci + 1) * c)
            fn = functools.partial(_gdn_chunk, heads=heads)
            _, pull = jax.vjp(fn, [q_ref[rows, cl] for cl in cols], [k_ref[rows, cl] for cl in cols],
                              [v_ref[rows, cl] for cl in cols], g_ref[rows, :], [st_ref[hh, ci] for hh in range(hps)])
            dq, dk, dv, dg, dst = pull(([do_ref[rows, cl] for cl in cols], [dstate[hh] for hh in range(hps)]))
            dg_ref[rows, :] = dg
            for hh in range(hps):
                dq_ref[rows, cols[hh]] = dq[hh]
                dk_ref[rows, cols[hh]] = dk[hh]
                dv_ref[rows, cols[hh]] = dv[hh]
                dstate[hh] = dst[hh]

    slab = pl.BlockSpec((cb, hps * _GDN_DIM), lambda h, i: (nblk - 1 - i, h))
    sds = jax.ShapeDtypeStruct((s, _GDN_HEADS * _GDN_DIM), _F32)
    return pl.pallas_call(
        body, name="gdn_chunk_bwd", grid=(_GDN_HEADS // hps, nblk),
        in_specs=[slab, slab, slab, pl.BlockSpec((cb, 128), lambda h, i: (nblk - 1 - i, 0)),
                  pl.BlockSpec((hps, per, _GDN_DIM, _GDN_DIM), lambda h, i: (h, nblk - 1 - i, 0, 0)), slab],
        out_specs=[slab, slab, slab, pl.BlockSpec((None, cb, 128), lambda h, i: (h, nblk - 1 - i, 0))],
        out_shape=[sds, sds, sds, jax.ShapeDtypeStruct((_GDN_HEADS // hps, s, 128), _F32)],
        scratch_shapes=[pltpu.VMEM((hps, _GDN_DIM, _GDN_DIM), _F32)],
        compiler_params=_params(("arbitrary", "arbitrary")),
    )(q, k, v, gates, states, do)


_SHALO = 8


def _swish_l2(c, normalize, scale):
    y = _silu(c)
    if normalize:
        y = y * lax.rsqrt(jnp.sum(y * y, axis=-1, keepdims=True) + _NORM_EPS) * scale
    return y


def _short_taps(buf, w_ref, rows):
    base = _SHALO - (_GDN_SHORT_CONV - 1)
    acc = jnp.zeros((rows, buf.shape[1]), _F32)
    for t in range(_GDN_SHORT_CONV):
        acc = acc + w_ref[t:t + 1, :] * buf[base + t:base + t + rows, :]
    return acc


_GDN_W = _GDN_HEADS * _GDN_DIM


def _gdn_conv_fwd(proj, conv_w, grp, normalize, scale):
    s = proj.shape[0]
    tm = _CONV_TM
    per = tm // _SHALO

    def body(x_ref, xp_ref, w_ref, o_ref, buf):
        buf[0:_SHALO, :] = jnp.where(pl.program_id(0) == 0, 0.0, xp_ref[...])
        buf[_SHALO:, :] = x_ref[...]
        conv = _short_taps(buf, w_ref, tm)
        for h in range(_GDN_HEADS):
            cols = slice(h * _GDN_DIM, (h + 1) * _GDN_DIM)
            o_ref[:, cols] = _swish_l2(conv[:, cols], normalize, scale)

    return pl.pallas_call(
        body, name=f"gdn_conv_fwd_{grp}", grid=(s // tm,),
        in_specs=[pl.BlockSpec((tm, _GDN_W), lambda i: (i, grp)),
                  pl.BlockSpec((_SHALO, _GDN_W), lambda i: (jnp.maximum(i * per - 1, 0), grp)),
                  pl.BlockSpec((_GDN_SHORT_CONV, _GDN_W), lambda i: (0, grp))],
        out_specs=pl.BlockSpec((tm, _GDN_W), lambda i: (i, 0)),
        out_shape=jax.ShapeDtypeStruct((s, _GDN_W), _F32),
        scratch_shapes=[pltpu.VMEM((tm + _SHALO, _GDN_W), _F32)], compiler_params=_params(("arbitrary",)),
    )(proj, proj, conv_w)


def _gdn_conv_bwd(proj, conv_w, grp, normalize, scale, dy, dproj):
    s = proj.shape[0]
    tm = _CONV_TM
    per = tm // _SHALO
    nt = s // tm
    taps = _GDN_SHORT_CONV

    def body(x_ref, xp_ref, xn_ref, w_ref, dy_ref, dyn_ref, dp_in, dx_ref, dw_ref, buf, dbuf):
        del dp_in
        i = pl.program_id(0)
        last = i == nt - 1
        buf[0:_SHALO, :] = jnp.where(i == 0, 0.0, xp_ref[...])
        buf[_SHALO:_SHALO + tm, :] = x_ref[...]
        buf[_SHALO + tm:, :] = jnp.where(last, 0.0, xn_ref[...])
        conv = _short_taps(buf, w_ref, tm + _SHALO)
        dy_ext = jnp.concatenate([dy_ref[...], jnp.where(last, 0.0, dyn_ref[...])], axis=0)
        for h in range(_GDN_HEADS):
            cols = slice(h * _GDN_DIM, (h + 1) * _GDN_DIM)
            _, pull = jax.vjp(lambda c: _swish_l2(c, normalize, scale), conv[:, cols])
            dbuf[:, cols] = pull(dy_ext[:, cols])[0]
        acc = jnp.zeros((tm, _GDN_W), _F32)
        for t in range(taps):
            o = taps - 1 - t
            acc = acc + w_ref[t:t + 1, :] * dbuf[o:o + tm, :]
        dx_ref[...] = acc.astype(dx_ref.dtype)

        @pl.when(i == 0)
        def _():
            dw_ref[...] = jnp.zeros_like(dw_ref)

        base = _SHALO - (taps - 1)
        for t in range(taps):
            dw_ref[t:t + 1, :] += jnp.sum(dbuf[0:tm, :] * buf[base + t:base + t + tm, :], axis=0, keepdims=True)

    cur = lambda col: pl.BlockSpec((tm, _GDN_W), lambda i: (i, col))
    nxt = lambda col: pl.BlockSpec((_SHALO, _GDN_W), lambda i: (jnp.minimum((i + 1) * per, s // _SHALO - 1), col))
    return pl.pallas_call(
        body, name=f"gdn_conv_bwd_{grp}", grid=(nt,),
        in_specs=[cur(grp), pl.BlockSpec((_SHALO, _GDN_W), lambda i: (jnp.maximum(i * per - 1, 0), grp)), nxt(grp),
                  pl.BlockSpec((taps, _GDN_W), lambda i: (0, grp)), cur(0), nxt(0), _ANY],
        out_specs=[cur(grp), pl.BlockSpec((_SHALO, _GDN_W), lambda i: (0, 0))],
        out_shape=[jax.ShapeDtypeStruct(dproj.shape, dproj.dtype), jax.ShapeDtypeStruct((_SHALO, _GDN_W), _F32)],
        scratch_shapes=[pltpu.VMEM((tm + 2 * _SHALO, _GDN_W), _F32), pltpu.VMEM((tm + _SHALO, _GDN_W), _F32)],
        input_output_aliases={6: 0}, compiler_params=_params(("arbitrary",)),
    )(proj, proj, proj, conv_w, dy, dy, dproj)


def _softplus(z):
    return jnp.maximum(z, 0.0) + jnp.log(1.0 + jnp.exp(-jnp.abs(z)))


def _gates(ba, a_lane, dt_lane):
    lane = lax.broadcasted_iota(jnp.int32, ba.shape, 1)
    g = -jnp.exp(a_lane) * _softplus(ba + dt_lane)
    return jnp.where(lane < _GDN_HEADS, jax.nn.sigmoid(ba), jnp.where(lane < 2 * _GDN_HEADS, g, 0.0))


def _gates_fwd(ba, a_lane, dt_lane):
    return _rowmap("gdn_gates_fwd", _gates, [(ba, 128, 0)], [a_lane, dt_lane], [(128, 128, 0, _F32)],
                   rows=ba.shape[0], tm=_TM)[0]


def _gates_bwd(ba, a_lane, dt_lane, dgates):
    def body(bav, dg8, av, dv):
        _, pull = jax.vjp(_gates, bav, av, dv)
        return pull(jnp.sum(dg8, axis=0))

    return _rowmap("gdn_gates_bwd", body, [(ba, 128, 0), (dgates, 128, 0)], [a_lane, dt_lane], [(128, 128, 0, _BF)],
                   [(1, 128), (1, 128)], rows=ba.shape[0], tm=_TM)


def _out_gate(o, z, g):
    return (_rms(o, g) * _silu(z)).astype(_BF)


def _head_cols():
    return [slice(h * _GDN_DIM, (h + 1) * _GDN_DIM) for h in range(_GDN_HEADS)]


def _out_gate_fwd(o, proj, norm_g):
    def body(ov, zv, gv):
        return jnp.concatenate([_out_gate(ov[:, cl], zv[:, cl], gv) for cl in _head_cols()], axis=1)

    return _rowmap("gdn_out_gate_fwd", body, [(o, _GDN_W, 0), (proj, _GDN_W, 3)], [norm_g], [(_GDN_W, _GDN_W, 0, _BF)],
                   rows=o.shape[0], tm=_CONV_TM)[0]


def _out_gate_bwd(o, proj, norm_g, dy):
    def body(ov, zv, dyv, gv):
        d_o, d_z, d_g = [], [], jnp.zeros_like(gv)
        for cl in _head_cols():
            _, pull = jax.vjp(_out_gate, ov[:, cl], zv[:, cl], gv)
            g_o, g_z, g_g = pull(dyv[:, cl].astype(_BF))
            d_o.append(g_o)
            d_z.append(g_z)
            d_g = d_g + g_g
        return jnp.concatenate(d_o, axis=1), jnp.concatenate(d_z, axis=1), d_g

    return _rowmap("gdn_out_gate_bwd", body, [(o, _GDN_W, 0), (proj, _GDN_W, 3), (dy, _GDN_W, 0)], [norm_g],
                   [(_GDN_W, _GDN_W, 0, _F32), (proj.shape[1], _GDN_W, 3, _BF)], [(1, _GDN_DIM)],
                   rows=o.shape[0], tm=_CONV_TM)


def _plain_in(name, xn, w, layer, tn):
    s, d = xn.shape
    n = w.shape[2]
    return _mm(name, xn, w, grid=(n // tn, s // _TM), a_spec=pl.BlockSpec((_TM, d), lambda j, i: (i, 0)),
               b_spec=pl.BlockSpec((None, d, tn), lambda j, i: (layer, 0, j)), dn=_NN,
               out_shape=jax.ShapeDtypeStruct((s, n), _F32), out_spec=pl.BlockSpec((_TM, tn), lambda j, i: (i, j)))


def _plain_dx(name, dz, w, layer, tn, extra=None):
    s, n = dz.shape
    d = w.shape[1]
    row = pl.BlockSpec((_TM, d), lambda i, j: (i, 0))
    return _mm(name, dz, w, grid=(s // _TM, n // tn), a_spec=pl.BlockSpec((_TM, tn), lambda i, j: (i, j)),
               b_spec=pl.BlockSpec((None, d, tn), lambda i, j: (layer, 0, j)), dn=_NT, red=1, acc_shape=(_TM, d),
               out_shape=jax.ShapeDtypeStruct((s, d), _F32), out_spec=row,
               epi=(lambda acc, e: acc + e) if extra is not None else None,
               extras=(extra,) if extra is not None else (), extra_specs=(row,) if extra is not None else ())


def _plain_dw(name, xn, dz, layer, n_layers, buf, tn):
    s, d = xn.shape
    n = dz.shape[1]
    return _mm(name, xn, dz, grid=(n // tn, s // _TM), a_spec=pl.BlockSpec((_TM, d), lambda j, i: (i, 0)),
               b_spec=pl.BlockSpec((_TM, tn), lambda j, i: (i, j)), dn=_TN, red=1, acc_shape=(d, tn),
               out_shape=jax.ShapeDtypeStruct((n_layers, d, n), _BF),
               out_spec=pl.BlockSpec((None, d, tn), lambda j, i: (layer, 0, j)), into=buf)


_GDN_QK_SCALE = _GDN_DIM ** -0.5


def _gdn_fwd(h, gamma, w_qkvz, w_ba, w_out, conv_w, a_lane, dt_lane, norm_g, li):
    hn = _rmsnorm_fwd(h, gamma)
    proj = _plain_in("gdn_in", hn, w_qkvz, li, 1024)
    ba = _plain_in("gdn_in_ba", hn, w_ba, li, 128)
    q = _gdn_conv_fwd(proj, conv_w, 0, True, _GDN_QK_SCALE)
    k = _gdn_conv_fwd(proj, conv_w, 1, True, 1.0)
    v = _gdn_conv_fwd(proj, conv_w, 2, False, 1.0)
    gates = _gates_fwd(ba, a_lane, dt_lane)
    o, states = _gdn_chunk_fwd(q, k, v, gates)
    y = _out_gate_fwd(o, proj, norm_g)
    h_new = _proj_residual("gdn_out", y, w_out, li, h, 1.0)
    return h_new, (h, hn, proj, ba, q, k, v, gates, states, o, y)


def _gdn_bwd(dh, saved, gamma, w_qkvz, w_ba, w_out, conv_w, a_lane, dt_lane, norm_g, li, n_layers, bufs):
    h, hn, proj, ba, q, k, v, gates, states, o, y = saved
    dw_qkvz, dw_ba, dw_out = bufs
    dy = _proj_out_dy("gdn_dy", dh, w_out, li)
    dw_out = _wgrad_rows("gdn_dwout", y, dh, li, n_layers, dw_out, 1.0)
    do, dproj, d_norm_g = _out_gate_bwd(o, proj, norm_g, dy)
    dq, dk, dv, dgates = _gdn_chunk_bwd(q, k, v, gates, states, do)
    dproj, dcw_q = _gdn_conv_bwd(proj, conv_w, 0, True, _GDN_QK_SCALE, dq, dproj)
    dproj, dcw_k = _gdn_conv_bwd(proj, conv_w, 1, True, 1.0, dk, dproj)
    dproj, dcw_v = _gdn_conv_bwd(proj, conv_w, 2, False, 1.0, dv, dproj)
    dba, d_a, d_dt = _gates_bwd(ba, a_lane, dt_lane, dgates)
    dhn_ba = _plain_dx("gdn_dhn_ba", dba, w_ba, li, 128)
    dhn = _plain_dx("gdn_dhn", dproj, w_qkvz, li, 1024, extra=dhn_ba)
    dw_qkvz = _plain_dw("gdn_dwin", hn, dproj, li, n_layers, dw_qkvz, 1024)
    dw_ba = _plain_dw("gdn_dwin_ba", hn, dba, li, n_layers, dw_ba, 128)
    dh_in, dgamma = _rmsnorm_bwd(h, gamma, dhn, dh)
    d_conv_w = jnp.concatenate([dcw_q, dcw_k, dcw_v], axis=1)[:_GDN_SHORT_CONV]
    return dh_in, dgamma, (d_conv_w, d_a, d_dt, d_norm_g), (dw_qkvz, dw_ba, dw_out)


_MESH = pl.DeviceIdType.MESH
_ROW_TM = 256


def _place():
    x, y, c = lax.axis_index("x"), lax.axis_index("y"), lax.axis_index("c")
    chips = [(1 - x, y), (x, 1 - y), (1 - x, 1 - y)]
    return x, y, c, chips, [2 * px + py for px, py in chips]


def _remote(src, dst, send_sem, recv_sem, device):
    return pltpu.make_async_remote_copy(src_ref=src, dst_ref=dst, send_sem=send_sem, recv_sem=recv_sem,
                                        device_id=device, device_id_type=_MESH)


def _comm_call(name, body, operands, out_shape, n_sems):
    return pl.pallas_call(
        body, name=name, in_specs=[_ANY] * len(operands),
        out_specs=[_ANY] * len(out_shape), out_shape=out_shape,
        scratch_shapes=[pltpu.SemaphoreType.DMA((n_sems,)), pltpu.SemaphoreType.DMA((n_sems,)),
                        pltpu.SemaphoreType.DMA],
    )(*operands)


def _rows2d(t):
    return t.reshape(-1, t.shape[-1])


def _row_tile(rows):
    tm = _ROW_TM
    while tm > 8 and rows % tm:
        tm //= 2
    return tm


def _placed_map(name, fn, where, ins, out_rows, cols, out_dtype, out_row, steps, tm, into=None):
    n_in = len(ins)

    def kern(where_ref, *refs):
        del where_ref
        o_ref = refs[n_in + (1 if into is not None else 0)]
        o_ref[...] = fn(*[r[...] for r in refs[:n_in]]).astype(o_ref.dtype)

    in_specs, operands = [], [where]
    for arr, rowfn in ins:
        operands.append(arr)
        if arr.ndim == 3:
            in_specs.append(pl.BlockSpec((arr.shape[0], tm, cols), lambda i, wh, f=rowfn: (0, f(i, wh), 0)))
        else:
            in_specs.append(pl.BlockSpec((tm, cols), lambda i, wh, f=rowfn: (f(i, wh), 0)))
    aliases = {}
    if into is not None:
        operands.append(into)
        in_specs.append(_ANY)
        aliases = {n_in + 1: 0}
    spec = pltpu.PrefetchScalarGridSpec(
        num_scalar_prefetch=1, grid=(steps,), in_specs=in_specs,
        out_specs=pl.BlockSpec((tm, cols), lambda i, wh: (out_row(i, wh), 0)))
    return pl.pallas_call(
        kern, name=name, grid_spec=spec, out_shape=jax.ShapeDtypeStruct((out_rows, cols), out_dtype),
        input_output_aliases=aliases, compiler_params=_params(("arbitrary",)),
    )(*operands)


def _gather_small(name, t):
    n_l = t.shape[0]
    half = n_l // 2

    def body(t_ref, out_ref, send, recv, local_sem):
        x, y, c, chips, idx = _place()
        me = 2 * x + y
        mine, other = pl.ds(c * half, half), pl.ds((1 - c) * half, half)
        local = pltpu.make_async_copy(t_ref, out_ref.at[:, me], local_sem)
        local.start()
        sends = [_remote(t_ref.at[mine], out_ref.at[mine, me], send.at[k], recv.at[k], (*chips[k], c))
                 for k in range(3)]
        for cp in sends:
            cp.start()
        passed = []
        for k in range(3):
            landed = out_ref.at[mine, idx[k]]
            _remote(landed, landed, send.at[k], recv.at[k], (*chips[k], c)).wait_recv()
            cp = _remote(landed, landed, send.at[3 + k], recv.at[3 + k], (x, y, 1 - c))
            cp.start()
            passed.append(cp)
        for k in range(3):
            dst = out_ref.at[other, idx[k]]
            _remote(dst, dst, send.at[3 + k], recv.at[3 + k], (x, y, 1 - c)).wait_recv()
        for cp in sends + passed:
            cp.wait_send()
        local.wait()

    out = jax.ShapeDtypeStruct((n_l, 4) + t.shape[1:], t.dtype)
    return _comm_call(name, body, [t], [out], 6)[0]


def _cast_slab(name, t, where):
    n_l, r, cols = t.shape
    tm = _row_tile(r)
    per = r // tm
    buf = _placed_map("cast_" + name, lambda val: val, where, [(_rows2d(t), lambda i, wh: i)], n_l * 4 * r, cols, _BF,
                      lambda i, wh: ((i // per) * 4 + wh[1]) * per + i % per, n_l * per, tm)
    return buf.reshape(n_l, 4, r, cols)


_HBM = pl.BlockSpec(memory_space=pltpu.HBM)
_SEM = pl.BlockSpec(memory_space=pltpu.SEMAPHORE)
_EFFECT = pltpu.SideEffectType.DATAFLOW_SIDE_EFFECTING


def _ici_copies(refs, send, recv, arrivals):
    x, y, c, chips, idx = _place()
    me = 2 * x + y
    res = []
    for t, ref in enumerate(refs):
        half = ref.shape[0] // 2
        mine = pl.ds(c * half, half)
        for k in range(3):
            sem = 3 * t + k
            block = ref.at[mine, idx[k] if arrivals else me]
            res.append(_remote(block, block, send.at[sem], recv.at[sem], (*chips[k], c)))
    return res


def _gather_start(name, bufs, after):
    n = len(bufs)

    def body(*refs):
        send, recv = refs[n + 1], refs[n + 2]
        for cp in _ici_copies(refs[:n], send, recv, False):
            cp.start()
        refs[2 * n + 3][...] = jnp.zeros((8, 128), _F32)

    res = pl.pallas_call(
        body, name=name,
        out_shape=(pltpu.SemaphoreType.DMA((3 * n,)), pltpu.SemaphoreType.DMA((3 * n,)),
                   *[pltpu.HBM(b.shape, b.dtype) for b in bufs], jax.ShapeDtypeStruct((8, 128), _F32)),
        in_specs=[*[_HBM] * n, _ANY], out_specs=(_SEM, _SEM, *[_HBM] * n, pl.BlockSpec(memory_space=pltpu.VMEM)),
        input_output_aliases={t: 2 + t for t in range(n)},
        compiler_params=pltpu.CompilerParams(has_side_effects=_EFFECT),
    )(*[pltpu.with_memory_space_constraint(b, pltpu.HBM) for b in bufs], after)
    return res[0], res[1], list(res[2:2 + n]), res[2 + n]


def _gather_wait(name, send_sem, recv_sem, bufs, after):
    n = len(bufs)

    def body(*refs):
        send, recv = refs[n], refs[n + 1]
        for cp in _ici_copies(refs[:n], send, recv, False):
            cp.wait_send()
        for cp in _ici_copies(refs[:n], send, recv, True):
            cp.wait_recv()

    res = pl.pallas_call(
        body, name=name, out_shape=tuple(pltpu.HBM(b.shape, b.dtype) for b in bufs),
        in_specs=[*[_HBM] * n, _SEM, _SEM, _ANY], out_specs=tuple([_HBM] * n),
        input_output_aliases={t: t for t in range(n)},
        compiler_params=pltpu.CompilerParams(has_side_effects=_EFFECT),
    )(*bufs, send_sem, recv_sem, after)
    return list(res)


def _gather_forward(name, bufs):
    n = len(bufs)

    def body(*refs):
        outs, send, recv = refs[n:2 * n], refs[2 * n], refs[2 * n + 1]
        x, y, c, _, idx = _place()
        copies, arrive = [], []
        for t, ref in enumerate(outs):
            half = ref.shape[0] // 2
            mine, other = pl.ds(c * half, half), pl.ds((1 - c) * half, half)
            for k in range(3):
                sem = 3 * t + k
                landed, dst = ref.at[mine, idx[k]], ref.at[other, idx[k]]
                copies.append(_remote(landed, landed, send.at[sem], recv.at[sem], (x, y, 1 - c)))
                arrive.append(_remote(dst, dst, send.at[sem], recv.at[sem], (x, y, 1 - c)))
        for cp in copies:
            cp.start()
        for cp in arrive:
            cp.wait_recv()
        for cp in copies:
            cp.wait_send()

    res = pl.pallas_call(
        body, name=name, in_specs=[_ANY] * n, out_specs=[_ANY] * n,
        out_shape=[jax.ShapeDtypeStruct(b.shape, b.dtype) for b in bufs],
        scratch_shapes=[pltpu.SemaphoreType.DMA((3 * n,)), pltpu.SemaphoreType.DMA((3 * n,))],
        input_output_aliases={t: t for t in range(n)},
    )(*bufs)
    return list(res)


def _rs_sibling(name, g):
    half = g.shape[0] // 2

    def body(g_ref, got_ref, send, recv, local_sem):
        del local_sem
        x, y, c, _, _ = _place()
        cp = _remote(g_ref.at[pl.ds((1 - c) * half, half)], got_ref, send.at[0], recv.at[0], (x, y, 1 - c))
        cp.start()
        cp.wait()

    return _comm_call(name, body, [g], [jax.ShapeDtypeStruct((half,) + g.shape[1:], g.dtype)], 1)[0]


def _rs_chips(name, hsum):
    half = hsum.shape[0]

    def body(h_ref, got_ref, send, recv, local_sem):
        del local_sem
        x, y, c, chips, idx = _place()
        sends = [_remote(h_ref.at[:, idx[k]], got_ref.at[k], send.at[k], recv.at[k], (*chips[k], c)) for k in range(3)]
        for cp in sends:
            cp.start()
        for cp in sends:
            cp.wait()

    return _comm_call(name, body, [hsum], [jax.ShapeDtypeStruct((3, half) + hsum.shape[2:], hsum.dtype)], 3)[0]


def _rs_join(name, total):
    half = total.shape[0] // 2

    def body(in_ref, out_ref, send, recv, local_sem):
        del in_ref, local_sem
        x, y, c, _, _ = _place()
        mine = out_ref.at[pl.ds(c * half, half)]
        cp = _remote(mine, mine, send.at[0], recv.at[0], (x, y, 1 - c))
        cp.start()
        landing = out_ref.at[pl.ds((1 - c) * half, half)]
        _remote(landing, landing, send.at[0], recv.at[0], (x, y, 1 - c)).wait_recv()
        cp.wait_send()

    return pl.pallas_call(
        body, name=name, in_specs=[_ANY], out_specs=_ANY, out_shape=jax.ShapeDtypeStruct(total.shape, total.dtype),
        scratch_shapes=[pltpu.SemaphoreType.DMA((1,)), pltpu.SemaphoreType.DMA((1,)), pltpu.SemaphoreType.DMA],
        input_output_aliases={0: 0},
    )(total)


def _reduce_scatter(name, g, where):
    n_l, _, r, cols = g.shape
    half = n_l // 2
    tm = _row_tile(r)
    per = r // tm
    got = _rs_sibling(name + "_sib", g)
    n_pair = half * 4 * per
    chip_sum = _placed_map("rs_add_pair", lambda u, w: u.astype(_F32) + w.astype(_F32), where,
                           [(_rows2d(g), lambda i, wh: wh[0] * n_pair + i), (_rows2d(got), lambda i, wh: i)],
                           half * 4 * r, cols, _BF, lambda i, wh: i, n_pair, tm).reshape(half, 4, r, cols)
    got3 = _rs_chips(name + "_chips", chip_sum)

    def add_four(own, others):
        return ((own.astype(_F32) + others[0].astype(_F32)) + others[1].astype(_F32)) + others[2].astype(_F32)

    n_four = half * per
    total = _placed_map("rs_add_four", add_four, where,
                        [(_rows2d(chip_sum), lambda i, wh: ((i // per) * 4 + wh[1]) * per + i % per),
                         (got3.reshape(3, half * r, cols), lambda i, wh: i)],
                        n_l * r, cols, _F32, lambda i, wh: wh[0] * n_four + i, n_four, tm)
    return _rs_join(name + "_join", total.reshape(n_l, r, cols))


def _all_reduce_small(pack):
    rows, cols = pack.shape

    def body(p_ref, o_ref, land, send, recv):
        x, y, c = lax.axis_index("x"), lax.axis_index("y"), lax.axis_index("c")
        me = 4 * x + 2 * y + c
        land[me] = p_ref[...]
        copies = []
        for k in range(1, 8):
            bx, by, bc = (k >> 2) & 1, (k >> 1) & 1, k & 1
            peer = (1 - x if bx else x, 1 - y if by else y, 1 - c if bc else c)
            cp = _remote(p_ref, land.at[me], send.at[k - 1], recv.at[k - 1], peer)
            cp.start()
            copies.append((cp, 4 * peer[0] + 2 * peer[1] + peer[2]))
        for k, (cp, pid) in enumerate(copies):
            _remote(p_ref, land.at[pid], send.at[k], recv.at[k], (x, y, c)).wait_recv()
        for cp, _ in copies:
            cp.wait_send()
        acc = land[0]
        for d in range(1, 8):
            acc = acc + land[d]
        o_ref[...] = acc

    vm = pl.BlockSpec(memory_space=pltpu.VMEM)
    return pl.pallas_call(
        body, name="all_reduce_small", in_specs=[vm], out_specs=vm,
        out_shape=jax.ShapeDtypeStruct((rows, cols), _F32),
        scratch_shapes=[pltpu.VMEM((8, rows, cols), _F32), pltpu.SemaphoreType.DMA((7,)), pltpu.SemaphoreType.DMA((7,))],
    )(pack)


def _adamw_math(w, g, m, v):
    m2 = _ADAM_B1 * m + (1.0 - _ADAM_B1) * g
    v2 = _ADAM_B2 * v + (1.0 - _ADAM_B2) * (g * g)
    m_hat = m2 / (1.0 - _ADAM_B1 ** _ADAM_STEP)
    v_hat = v2 / (1.0 - _ADAM_B2 ** _ADAM_STEP)
    delta = -_ADAM_LR * (m_hat / (jnp.sqrt(v_hat) + _ADAM_EPS) + _ADAM_WD * w)
    return delta, m2, v2


def _adamw(w, g, m, v):
    shape = w.shape
    w2, g2, m2, v2 = (_rows2d(t) for t in (w, g, m, v))
    rows, cols = w2.shape
    outs = _rowmap("adamw", _adamw_math, [(t, cols, 0) for t in (w2, g2, m2, v2)], [], [(cols, cols, 0, _F32)] * 3,
                   rows=rows, tm=_row_tile(rows))
    return tuple(o.reshape(shape) for o in outs)


def kernel(x, positions, ffn1_norm, ffn1_w_in, ffn1_w_out, mix_norm, ffn2_norm, ffn2_w_in, ffn2_w_out, hyb_w_in, hyb_dw_w, hyb_dw_b, hyb_ln_g, hyb_ln_b, hyb_w_out, gdn_w_in, gdn_conv_w, gdn_A_log, gdn_dt_bias, gdn_norm_g, gdn_w_out, final_norm, loss_target, m_ffn1_norm, m_ffn1_w_in, m_ffn1_w_out, m_mix_norm, m_ffn2_norm, m_ffn2_w_in, m_ffn2_w_out, m_hyb_w_in, m_hyb_dw_w, m_hyb_dw_b, m_hyb_ln_g, m_hyb_ln_b, m_hyb_w_out, m_gdn_w_in, m_gdn_conv_w, m_gdn_A_log, m_gdn_dt_bias, m_gdn_norm_g, m_gdn_w_out, m_final_norm, v_ffn1_norm, v_ffn1_w_in, v_ffn1_w_out, v_mix_norm, v_ffn2_norm, v_ffn2_w_in, v_ffn2_w_out, v_hyb_w_in, v_hyb_dw_w, v_hyb_dw_b, v_hyb_ln_g, v_hyb_ln_b, v_hyb_w_out, v_gdn_w_in, v_gdn_conv_w, v_gdn_A_log, v_gdn_dt_bias, v_gdn_norm_g, v_gdn_w_out, v_final_norm):
    w = dict(ffn1_norm=ffn1_norm, ffn1_w_in=ffn1_w_in, ffn1_w_out=ffn1_w_out, mix_norm=mix_norm, ffn2_norm=ffn2_norm, ffn2_w_in=ffn2_w_in, ffn2_w_out=ffn2_w_out, hyb_w_in=hyb_w_in, hyb_dw_w=hyb_dw_w, hyb_dw_b=hyb_dw_b, hyb_ln_g=hyb_ln_g, hyb_ln_b=hyb_ln_b, hyb_w_out=hyb_w_out, gdn_w_in=gdn_w_in, gdn_conv_w=gdn_conv_w, gdn_A_log=gdn_A_log, gdn_dt_bias=gdn_dt_bias, gdn_norm_g=gdn_norm_g, gdn_w_out=gdn_w_out, final_norm=final_norm)
    m = dict(ffn1_norm=m_ffn1_norm, ffn1_w_in=m_ffn1_w_in, ffn1_w_out=m_ffn1_w_out, mix_norm=m_mix_norm, ffn2_norm=m_ffn2_norm, ffn2_w_in=m_ffn2_w_in, ffn2_w_out=m_ffn2_w_out, hyb_w_in=m_hyb_w_in, hyb_dw_w=m_hyb_dw_w, hyb_dw_b=m_hyb_dw_b, hyb_ln_g=m_hyb_ln_g, hyb_ln_b=m_hyb_ln_b, hyb_w_out=m_hyb_w_out, gdn_w_in=m_gdn_w_in, gdn_conv_w=m_gdn_conv_w, gdn_A_log=m_gdn_A_log, gdn_dt_bias=m_gdn_dt_bias, gdn_norm_g=m_gdn_norm_g, gdn_w_out=m_gdn_w_out, final_norm=m_final_norm)
    v = dict(ffn1_norm=v_ffn1_norm, ffn1_w_in=v_ffn1_w_in, ffn1_w_out=v_ffn1_w_out, mix_norm=v_mix_norm, ffn2_norm=v_ffn2_norm, ffn2_w_in=v_ffn2_w_in, ffn2_w_out=v_ffn2_w_out, hyb_w_in=v_hyb_w_in, hyb_dw_w=v_hyb_dw_w, hyb_dw_b=v_hyb_dw_b, hyb_ln_g=v_hyb_ln_g, hyb_ln_b=v_hyb_ln_b, hyb_w_out=v_hyb_w_out, gdn_w_in=v_gdn_w_in, gdn_conv_w=v_gdn_conv_w, gdn_A_log=v_gdn_A_log, gdn_dt_bias=v_gdn_dt_bias, gdn_norm_g=v_gdn_norm_g, gdn_w_out=v_gdn_w_out, final_norm=v_final_norm)
    return _train_step(x, positions, loss_target, w, m, v)


_WEIGHT_ORDER = ("ffn1_norm", "ffn1_w_in", "ffn1_w_out", "mix_norm", "ffn2_norm", "ffn2_w_in", "ffn2_w_out",
                 "hyb_w_in", "hyb_dw_w", "hyb_dw_b", "hyb_ln_g", "hyb_ln_b", "hyb_w_out", "gdn_w_in", "gdn_conv_w",
                 "gdn_A_log", "gdn_dt_bias", "gdn_norm_g", "gdn_w_out", "final_norm")
_BIG = ("ffn1_w_in", "ffn1_w_out", "ffn2_w_in", "ffn2_w_out", "hyb_w_in", "hyb_w_out", "gdn_w_in", "gdn_w_out")
_PACK_COLS = 1024


def _lane16(vec):
    return jnp.zeros((1, 128), _F32).at[0, _GDN_HEADS:2 * _GDN_HEADS].set(vec)


def _pack_rows(parts):
    rows, where, at = [], [], 0
    for p in parts:
        flat = p.reshape(-1).astype(_F32)
        n = -(-flat.shape[0] // (8 * _PACK_COLS)) * 8
        rows.append(jnp.pad(flat, (0, n * _PACK_COLS - flat.shape[0])).reshape(n, _PACK_COLS))
        where.append((at, n, flat.shape[0], p.shape))
        at += n
    return jnp.concatenate(rows, axis=0), where


def _unpack_rows(pack, where):
    return [pack[at:at + n].reshape(-1)[:size].reshape(shape) for at, n, size, shape in where]


def _train_step(x, positions, loss_target, w, m, v):
    n_layers = w["ffn1_norm"].shape[0]
    n_hyb, n_gdn = w["hyb_w_in"].shape[0], w["gdn_w_in"].shape[0]
    d_model = x.shape[-1]
    xh, target = x[0], loss_target[0]
    chip = 2 * lax.axis_index("x") + lax.axis_index("y")
    where = jnp.stack([lax.axis_index("c"), chip]).astype(jnp.int32)

    groups = (("ffn1_w_in", "ffn1_w_out"), ("hyb_w_in", "hyb_w_out"), ("ffn2_w_in", "ffn2_w_out"),
              ("gdn_w_in", "gdn_w_out"))
    pending, token = [], where
    for gi, names in enumerate(groups):
        send, recv, thru, token = _gather_start(f"gather_start_{gi}", [_cast_slab(n, w[n], where) for n in names], token)
        pending.append((send, recv, thru))
    gdn_cols = w["gdn_w_in"].shape[-1]
    n_qkvz = 4 * _GDN_HEADS * _GDN_DIM
    n_ba = 4 * gdn_cols - n_qkvz
    wv = {}

    def arrive(gi, after):
        send, recv, thru = pending[gi]
        a, b = _gather_forward(f"gather_pass_{gi}", _gather_wait(f"gather_wait_{gi}", send, recv, thru, after))
        kind = groups[gi][0]
        if kind == "gdn_w_in":
            gdn_in = jnp.transpose(a, (0, 2, 1, 3)).reshape(n_gdn, d_model, 4 * gdn_cols)
            wv["gdn_qkvz"] = gdn_in[:, :, :n_qkvz]
            wv["gdn_ba"] = jnp.pad(gdn_in[:, :, n_qkvz:], ((0, 0), (0, 0), (0, 128 - n_ba)))
        else:
            wv[kind] = a
        wv[groups[gi][1]] = b.reshape(b.shape[0], -1, d_model)

    dw_w = _gather_small("gather_hyb_dw_w", jnp.pad(w["hyb_dw_w"], ((0, 0), (0, _HALO - _CONV_WIDTH), (0, 0))))
    dw_w = jnp.transpose(dw_w, (0, 2, 1, 3)).reshape(n_hyb, _HALO, -1)
    conv_w = _gather_small("gather_gdn_conv_w", jnp.pad(w["gdn_conv_w"], ((0, 0), (0, _SHALO - _GDN_SHORT_CONV), (0, 0))))
    conv_w = jnp.transpose(conv_w, (0, 2, 1, 3)).reshape(n_gdn, _SHALO, -1)[:, :_GDN_SHORT_CONV]
    tabs = _rotary_tables(positions)

    def mixer_args(layer):
        i = layer // 2
        if layer % 2 == 0:
            return (wv["hyb_w_in"], wv["hyb_w_out"], dw_w[i], w["hyb_dw_b"][i][None], w["hyb_ln_g"][i][None],
                    w["hyb_ln_b"][i][None], tabs, i)
        return (wv["gdn_qkvz"], wv["gdn_ba"], wv["gdn_w_out"], conv_w[i], _lane16(w["gdn_A_log"][i]),
                _lane16(w["gdn_dt_bias"][i]), w["gdn_norm_g"][i][None], i)

    h = xh
    saved = []
    arrive(0, token)
    for layer in range(n_layers):
        h, s1 = _ffn_fwd(h, w["ffn1_norm"][layer][None], wv["ffn1_w_in"], wv["ffn1_w_out"], layer)
        gamma = w["mix_norm"][layer][None]
        if layer % 2 == 0:
            if layer == 0:
                arrive(1, h)
            h, sm = _hyb_fwd(h, gamma, *mixer_args(layer))
        else:
            if layer == 1:
                arrive(3, h)
            h, sm = _gdn_fwd(h, gamma, *mixer_args(layer))
        if layer == 0:
            arrive(2, h)
        h, s2 = _ffn_fwd(h, w["ffn2_norm"][layer][None], wv["ffn2_w_in"], wv["ffn2_w_out"], layer)
        saved.append((s1, sm, s2))
    dh, loss_row, d_final = _loss_head(h, w["final_norm"][None], target)

    bufs = dict(ffn1_w_in=None, ffn1_w_out=None, ffn2_w_in=None, ffn2_w_out=None, hyb_w_in=None, hyb_w_out=None,
                gdn_qkvz=None, gdn_ba=None, gdn_w_out=None)
    d_norm = {k: [None] * n_layers for k in ("ffn1_norm", "mix_norm", "ffn2_norm")}
    d_hyb = [None] * n_hyb
    d_gdn = [None] * n_gdn
    for layer in reversed(range(n_layers)):
        s1, sm, s2 = saved[layer]
        dh, d_norm["ffn2_norm"][layer], bufs["ffn2_w_in"], bufs["ffn2_w_out"] = _ffn_bwd(
            dh, s2, w["ffn2_norm"][layer][None], wv["ffn2_w_in"], wv["ffn2_w_out"], layer, n_layers, bufs["ffn2_w_in"],
            bufs["ffn2_w_out"])
        gamma = w["mix_norm"][layer][None]
        if layer % 2 == 0:
            args = mixer_args(layer)
            dh, d_norm["mix_norm"][layer], d_hyb[layer // 2], bufs["hyb_w_in"], bufs["hyb_w_out"] = _hyb_bwd(
                dh, sm, gamma, *args, n_hyb, bufs["hyb_w_in"], bufs["hyb_w_out"])
        else:
            args = mixer_args(layer)
            dh, d_norm["mix_norm"][layer], d_gdn[layer // 2], (bufs["gdn_qkvz"], bufs["gdn_ba"], bufs["gdn_w_out"]) = _gdn_bwd(
                dh, sm, gamma, *args, n_gdn, (bufs["gdn_qkvz"], bufs["gdn_ba"], bufs["gdn_w_out"]))
        dh, d_norm["ffn1_norm"][layer], bufs["ffn1_w_in"], bufs["ffn1_w_out"] = _ffn_bwd(
            dh, s1, w["ffn1_norm"][layer][None], wv["ffn1_w_in"], wv["ffn1_w_out"], layer, n_layers, bufs["ffn1_w_in"],
            bufs["ffn1_w_out"])
    grad_x = dh[None]

    d_gdn_in = jnp.concatenate([bufs["gdn_qkvz"], bufs["gdn_ba"][:, :, :n_ba]], axis=2)
    d_gdn_in = jnp.transpose(d_gdn_in.reshape(n_gdn, d_model, 4, gdn_cols), (0, 2, 1, 3))
    full_grads = dict(
        ffn1_w_in=bufs["ffn1_w_in"], ffn2_w_in=bufs["ffn2_w_in"], hyb_w_in=bufs["hyb_w_in"], gdn_w_in=d_gdn_in,
        ffn1_w_out=bufs["ffn1_w_out"].reshape(n_layers, 4, -1, d_model),
        ffn2_w_out=bufs["ffn2_w_out"].reshape(n_layers, 4, -1, d_model),
        hyb_w_out=bufs["hyb_w_out"].reshape(n_hyb, 4, -1, d_model),
        gdn_w_out=bufs["gdn_w_out"].reshape(n_gdn, 4, -1, d_model))
    grads = {name: _reduce_scatter("rs_" + name, full_grads[name], where) for name in _BIG}

    stack = lambda rows: jnp.concatenate(rows, axis=0)
    misc = jnp.concatenate([stack([g[1] for g in d_gdn])[:, _GDN_HEADS:2 * _GDN_HEADS].reshape(-1),
                            stack([g[2] for g in d_gdn])[:, _GDN_HEADS:2 * _GDN_HEADS].reshape(-1),
                            stack([g[3] for g in d_gdn]).reshape(-1), loss_row[0, :1]])
    n_a = n_gdn * _GDN_HEADS
    parts = [stack(d_norm["ffn1_norm"]), stack(d_norm["mix_norm"]), stack(d_norm["ffn2_norm"]), d_final,
             stack([g[1] for g in d_hyb]), stack([g[2] for g in d_hyb]), stack([g[3] for g in d_hyb]), misc,
             jnp.stack([g[0][:_CONV_WIDTH] for g in d_hyb]), jnp.stack([g[0] for g in d_gdn])]
    pack, where = _pack_rows(parts)
    red = _unpack_rows(_all_reduce_small(pack), where)
    misc = red[7]
    cw = w["hyb_dw_w"].shape[-1]
    gw = w["gdn_conv_w"].shape[-1]
    grads.update(
        ffn1_norm=red[0], mix_norm=red[1], ffn2_norm=red[2], final_norm=red[3].reshape(-1),
        hyb_dw_b=red[4], hyb_ln_g=red[5], hyb_ln_b=red[6],
        gdn_A_log=misc[:n_a].reshape(n_gdn, _GDN_HEADS), gdn_dt_bias=misc[n_a:2 * n_a].reshape(n_gdn, _GDN_HEADS),
        gdn_norm_g=misc[2 * n_a:2 * n_a + n_gdn * _GDN_DIM].reshape(n_gdn, _GDN_DIM),
        hyb_dw_w=lax.dynamic_slice_in_dim(red[8], chip * cw, cw, axis=2),
        gdn_conv_w=lax.dynamic_slice_in_dim(red[9], chip * gw, gw, axis=2))
    loss = misc[2 * n_a + n_gdn * _GDN_DIM]

    delta, new_m, new_v = {}, {}, {}
    for name in _BIG:
        delta[name], new_m[name], new_v[name] = _adamw(w[name], grads[name], m[name], v[name])
    small = [n for n in _WEIGHT_ORDER if n not in _BIG]
    packs = [_pack_rows([src[n] for n in small]) for src in (w, grads, m, v)]
    outs = _adamw(*[p[0] for p in packs])
    for res, o in zip((delta, new_m, new_v), outs):
        res.update(dict(zip(small, _unpack_rows(o, packs[0][1]))))
    return (loss, grad_x, *[grads[n] for n in _WEIGHT_ORDER], *[delta[n] for n in _WEIGHT_ORDER],
            *[new_m[n] for n in _WEIGHT_ORDER], *[new_v[n] for n in _WEIGHT_ORDER])
```

```python
import functools

import jax
import jax.numpy as jnp
from jax import lax
from jax.experimental import pallas as pl
from jax.experimental.pallas import tpu as pltpu

_BF = jnp.bfloat16
_F32 = jnp.float32
_NORM_EPS = 1e-6
_VMEM_LIMIT_BYTES = 56 * 1024 * 1024

_A_HEADS = 8
_A_HEAD_DIM = 64
_WIN_BLOCK = 128
_DILATIONS = (1, 4, 16)
_CONV_WIDTH = 31
_GDN_HEADS = 8
_GDN_DIM = 128
_GDN_CHUNK = 64
_GDN_SHORT_CONV = 4

_ADAM_LR = 0.001
_ADAM_B1 = 0.9
_ADAM_B2 = 0.999
_ADAM_EPS = 1e-08
_ADAM_WD = 0.01
_ADAM_STEP = 10


def _params(sem):
    return pltpu.CompilerParams(dimension_semantics=sem, vmem_limit_bytes=_VMEM_LIMIT_BYTES)


def _dot(a, b, dn=(((1,), (0,)), ((), ()))):
    return lax.dot_general(a.astype(_BF), b.astype(_BF), dn, preferred_element_type=_F32)


def _dot_nt(a, b):
    return _dot(a, b, (((1,), (1,)), ((), ())))


def _dot_tn(a, b):
    return _dot(a, b, (((0,), (0,)), ((), ())))


def _dot_hi(a, b):
    ah, bh = a.astype(_BF), b.astype(_BF)
    al, bl = (a - ah.astype(_F32)).astype(_BF), (b - bh.astype(_F32)).astype(_BF)
    dot = lambda u, w: jnp.dot(u, w, preferred_element_type=_F32)
    return dot(ah, bh) + (dot(ah, bl) + dot(al, bh))


_NN = (((1,), (0,)), ((), ()))
_NT = (((1,), (1,)), ((), ()))
_TN = (((0,), (0,)), ((), ()))
_ANY = pl.BlockSpec(memory_space=pl.ANY)


def _mm(name, a, b, *, grid, a_spec, b_spec, dn, out_shape, out_spec, red=None, acc_shape=None,
        pre_a=None, pre_b=None, epi=None, extras=(), extra_specs=()):
    n_ex = len(extras)
    n_in = 2 + n_ex
    nk = grid[red] if red is not None else 1

    def body(*refs):
        a_ref, b_ref = refs[0], refs[1]
        ex = refs[2:2 + n_ex]
        o_ref = refs[n_in]
        av, bv = a_ref[...], b_ref[...]
        if pre_a is not None:
            av = pre_a(av)
        if pre_b is not None:
            bv = pre_b(bv)
        part = lax.dot_general(av.astype(_BF), bv.astype(_BF), dn, preferred_element_type=_F32)

        def finish(acc):
            res = epi(acc, *[e[...] for e in ex]) if epi is not None else acc
            if isinstance(res, tuple):
                for idx, val in enumerate(res):
                    o_ref[idx] = val.astype(o_ref.dtype)
            else:
                o_ref[...] = res.astype(o_ref.dtype)

        if red is None:
            finish(part)
        else:
            acc_ref = refs[n_in + 1]
            k = pl.program_id(red)

            @pl.when(k == 0)
            def _():
                acc_ref[...] = part

            @pl.when(k > 0)
            def _():
                acc_ref[...] += part

            @pl.when(k == nk - 1)
            def _():
                finish(acc_ref[...])

    return pl.pallas_call(
        body, name=name, grid=grid, in_specs=[a_spec, b_spec, *extra_specs], out_specs=out_spec, out_shape=out_shape,
        scratch_shapes=[pltpu.VMEM(acc_shape, _F32)] if red is not None else [],
        compiler_params=_params(("arbitrary",) * len(grid)),
    )(a, b, *extras)


def _rowmap(name, body, tiled, params, tile_outs, param_outs=(), *, rows, tm, ncol=1, col_outer=False,
            into=None):
    nt, npar, nto, npo = len(tiled), len(params), len(tile_outs), len(param_outs)
    assert rows % tm == 0, (name, rows, tm)
    nrow = rows // tm
    grid = (ncol, nrow) if col_outer else (nrow, ncol)

    def ij(g):
        return (g[1], g[0]) if col_outer else (g[0], g[1])

    def cidx(col, j):
        return col(j) if callable(col) else col

    in_specs, operands = [], []
    for arr, width, col in tiled:
        operands.append(arr)
        if arr.ndim == 3:
            in_specs.append(pl.BlockSpec((arr.shape[0], tm, width),
                                         lambda *g, col=col: (0, ij(g)[0], cidx(col, ij(g)[1]))))
        else:
            in_specs.append(pl.BlockSpec((tm, width), lambda *g, col=col: (ij(g)[0], cidx(col, ij(g)[1]))))
    for p in params:
        if isinstance(p, tuple):
            arr, width, col = p
            operands.append(arr)
            in_specs.append(pl.BlockSpec((arr.shape[0], width), lambda *g, col=col: (0, cidx(col, ij(g)[1]))))
        else:
            operands.append(p)
            in_specs.append(pl.BlockSpec(p.shape, lambda *g, nd=p.ndim: (0,) * nd))
    out_shape, out_specs = [], []
    for total, width, col, dt in tile_outs:
        out_shape.append(jax.ShapeDtypeStruct((rows, total), dt))
        out_specs.append(pl.BlockSpec((tm, width), lambda *g, col=col: (ij(g)[0], cidx(col, ij(g)[1]))))
    for shp in param_outs:
        out_shape.append(jax.ShapeDtypeStruct(shp, _F32))
        out_specs.append(pl.BlockSpec(shp, lambda *g, nd=len(shp): (0,) * nd))
    n_in = nt + npar
    aliases = {}
    if into is not None:
        operands.append(into[0])
        in_specs.append(_ANY)
        aliases = {n_in: into[1]}
        n_in += 1

    def kern(*refs):
        vals = [r[...] for r in refs[:nt + npar]]
        touts = refs[n_in:n_in + nto]
        pouts = refs[n_in + nto:]
        res = body(*vals)
        if not isinstance(res, (tuple, list)):
            res = (res,)
        for r, val in zip(touts, res[:nto]):
            r[...] = val.astype(r.dtype)
        if npo:
            @pl.when((pl.program_id(0) == 0) & (pl.program_id(1) == 0))
            def _():
                for r in pouts:
                    r[...] = jnp.zeros_like(r)

            for r, val in zip(pouts, res[nto:]):
                r[...] += val

    return pl.pallas_call(
        kern, name=name, grid=grid, in_specs=in_specs, out_specs=out_specs, out_shape=out_shape,
        input_output_aliases=aliases, compiler_params=_params(("arbitrary", "arbitrary")),
    )(*operands)


def _vjp_body(fn, n_in, diff, n_ct):
    def body(*vals):
        ins, cts = vals[:n_in], vals[n_in:n_in + n_ct]
        outs, pull = jax.vjp(fn, *ins)
        single = not isinstance(outs, (tuple, list))
        if single:
            grads = pull(cts[0].astype(outs.dtype))
        else:
            grads = pull(tuple(c.astype(o.dtype) for c, o in zip(cts, outs)))
        return tuple(grads[i] for i in diff)
    return body


_TM = 512


def _rms(h, gamma):
    hf = h.astype(_F32)
    return hf * lax.rsqrt(jnp.mean(hf * hf, axis=-1, keepdims=True) + _NORM_EPS) * gamma


def _rmsnorm_fwd(h, gamma):
    s, d = h.shape
    return _rowmap("rmsnorm_fwd", lambda hv, gv: _rms(hv, gv).astype(_BF), [(h, d, 0)], [gamma],
                   [(d, d, 0, _BF)], rows=s, tm=_TM)[0]


def _rmsnorm_bwd(h, gamma, dxn, dh_out):
    s, d = h.shape

    def body(hv, dxv, dhv, gv):
        _, pull = jax.vjp(_rms, hv, gv)
        dh, dg = pull(dxv.astype(_F32))
        return dhv + dh, dg

    return _rowmap("rmsnorm_bwd", body, [(h, d, 0), (dxn, d, 0), (dh_out, d, 0)], [gamma],
                   [(d, d, 0, _F32)], [(1, d)], rows=s, tm=_TM)


def _silu(x):
    return x * jax.nn.sigmoid(x)


def _ffn_up(xn, w_in, layer):
    s, d = xn.shape
    ns = w_in.shape[-1]

    def body(x_ref, wg_ref, wu_ref, g_ref, u_ref, a_ref):
        xv = x_ref[...]
        g = jnp.dot(xv, wg_ref[...], preferred_element_type=_F32)
        u = jnp.dot(xv, wu_ref[...], preferred_element_type=_F32)
        g_ref[...] = g.astype(_BF)
        u_ref[...] = u.astype(_BF)
        a_ref[...] = (_silu(g) * u).astype(_BF)

    out = pl.BlockSpec((_TM, ns), lambda j, i: (i, j))
    sds = jax.ShapeDtypeStruct((s, 2 * ns), _BF)
    return pl.pallas_call(
        body, name="ffn_up", grid=(2, s // _TM),
        in_specs=[pl.BlockSpec((_TM, d), lambda j, i: (i, 0)),
                  pl.BlockSpec((None, None, d, ns), lambda j, i: (layer, j, 0, 0)),
                  pl.BlockSpec((None, None, d, ns), lambda j, i: (layer, j + 2, 0, 0))],
        out_specs=[out, out, out], out_shape=[sds, sds, sds],
        compiler_params=_params(("arbitrary", "arbitrary")),
    )(xn, w_in, w_in)


def _proj_residual(name, y, w, layer, h, scale):
    s, k = y.shape
    d = w.shape[-1]
    return _mm(name, y, w, grid=(s // _TM,), a_spec=pl.BlockSpec((_TM, k), lambda i: (i, 0)),
               b_spec=pl.BlockSpec((None, k, d), lambda i: (layer, 0, 0)), dn=_NN,
               out_shape=jax.ShapeDtypeStruct((s, d), _F32), out_spec=pl.BlockSpec((_TM, d), lambda i: (i, 0)),
               epi=lambda acc, hv: hv + scale * acc, extras=(h,),
               extra_specs=(pl.BlockSpec((_TM, d), lambda i: (i, 0)),))


def _ffn_dact(dh, w_out, layer, g, u):
    s, d = dh.shape
    f = g.shape[1]
    ns = f // 2

    def epi(da, gv, uv):
        gf, uf = gv.astype(_F32), uv.astype(_F32)
        sg = jax.nn.sigmoid(gf)
        return da * uf * (sg * (1.0 + gf * (1.0 - sg))), da * gf * sg

    gu = pl.BlockSpec((_TM, ns), lambda n, i: (i, n))
    return _mm("ffn_dact", dh, w_out, grid=(2, s // _TM), a_spec=pl.BlockSpec((_TM, d), lambda n, i: (i, 0)),
               b_spec=pl.BlockSpec((None, ns, d), lambda n, i: (layer, n, 0)), dn=_NT,
               out_shape=jax.ShapeDtypeStruct((2, s, f), _BF), out_spec=pl.BlockSpec((2, _TM, ns), lambda n, i: (0, i, n)),
               pre_a=lambda v: 0.5 * v, epi=epi, extras=(g, u), extra_specs=(gu, gu))


def _wgrad_rows(name, y, dh, scale):
    s, k = y.shape
    d = dh.shape[1]
    tk = k // 2 if (k // 2) % 128 == 0 else k
    out = _mm(name, y, dh, grid=(k // tk, s // _TM), a_spec=pl.BlockSpec((_TM, tk), lambda n, i: (i, n)),
              b_spec=pl.BlockSpec((_TM, d), lambda n, i: (i, 0)), dn=_TN, red=1, acc_shape=(tk, d),
              out_shape=jax.ShapeDtypeStruct((k, d), _BF), out_spec=pl.BlockSpec((tk, d), lambda n, i: (n, 0)),
              pre_b=(lambda v: scale * v) if scale != 1.0 else None)
    return out.reshape(4, k // 4, d)


def _ffn_dxn(dz, w_in, layer):
    _, s, f = dz.shape
    d, ns = w_in.shape[2], w_in.shape[3]
    return _mm("ffn_dxn", dz, w_in, grid=(s // _TM, 4),
               a_spec=pl.BlockSpec((None, _TM, ns), lambda i, j: (j // 2, i, j % 2)),
               b_spec=pl.BlockSpec((None, None, d, ns), lambda i, j: (layer, j, 0, 0)), dn=_NT, red=1,
               acc_shape=(_TM, d), out_shape=jax.ShapeDtypeStruct((s, d), _F32),
               out_spec=pl.BlockSpec((_TM, d), lambda i, j: (i, 0)))


def _ffn_dwin(xn, dz):
    s, d = xn.shape
    ns = dz.shape[2] // 2
    return _mm("ffn_dwin", xn, dz, grid=(4, s // _TM), a_spec=pl.BlockSpec((_TM, d), lambda j, i: (i, 0)),
               b_spec=pl.BlockSpec((None, _TM, ns), lambda j, i: (j // 2, i, j % 2)), dn=_TN, red=1,
               acc_shape=(d, ns), out_shape=jax.ShapeDtypeStruct((4, d, ns), _BF),
               out_spec=pl.BlockSpec((None, d, ns), lambda j, i: (j, 0, 0)))


def _ffn_fwd(h, gamma, w_in, w_out, layer):
    xn = _rmsnorm_fwd(h, gamma)
    g, u, act = _ffn_up(xn, w_in, layer)
    h_new = _proj_residual("ffn_down", act, w_out, layer, h, 0.5)
    return h_new, (h, xn, g, u, act)


def _ffn_bwd(dh, saved, gamma, w_in, w_out, layer):
    h, xn, g, u, act = saved
    dz = _ffn_dact(dh, w_out, layer, g, u)
    d_w_out = _wgrad_rows("ffn_dwout", act, dh, 0.5)
    dxn = _ffn_dxn(dz, w_in, layer)
    d_w_in = _ffn_dwin(xn, dz)
    dh_in, dgamma = _rmsnorm_bwd(h, gamma, dxn, dh)
    return dh_in, dgamma, d_w_in, d_w_out


def _loss_head(h, gamma, target):
    s, d = h.shape

    def body(hv, tv, gv):
        y, pull = jax.vjp(_rms, hv, gv)
        err = y - tv
        loss = 0.5 * jnp.sum(jnp.mean(err * err, axis=-1, keepdims=True), axis=0, keepdims=True)
        dh, dg = pull(err * (1.0 / d))
        return dh, jnp.broadcast_to(loss, (1, 128)), dg

    return _rowmap("loss_head", body, [(h, d, 0), (target, d, 0)], [gamma], [(d, d, 0, _F32)],
                   [(1, 128), (1, d)], rows=s, tm=_TM)


_ROT = _A_HEAD_DIM // 4
_ROPE_THETA = 500000.0
_A_WIDTH = _A_HEADS * _A_HEAD_DIM
_NEG = -1e30


def _rotary_tables(positions):
    inv_freq = jnp.power(jnp.float32(_ROPE_THETA), -jnp.arange(0, _ROT, 2, dtype=_F32) / _ROT)
    ang = positions.reshape(-1, 1).astype(_F32) * inv_freq
    cos, sin = jnp.cos(ang), jnp.sin(ang)
    s = ang.shape[0]
    pad = jnp.zeros((s, _A_HEAD_DIM - _ROT), _F32)
    c_head = jnp.concatenate([cos, cos, pad + 1.0], axis=1)
    s_head = jnp.concatenate([-sin, sin, pad], axis=1)
    c2, s2 = jnp.tile(c_head, (1, 2)), jnp.tile(s_head, (1, 2))
    scale = _A_HEAD_DIM ** -0.5
    tab_c = jnp.stack([scale * c2, c2, jnp.ones_like(c2)])
    tab_s = jnp.stack([scale * s2, s2, jnp.zeros_like(s2)])
    return tab_c, tab_s


def _pair_swap(z):
    lane = lax.broadcasted_iota(jnp.int32, z.shape, 1) % _A_HEAD_DIM
    half = _ROT // 2
    up = jnp.where(lane < _ROT, pltpu.roll(z, half, axis=1), 0.0)
    return jnp.where(lane < half, pltpu.roll(z, 128 - half, axis=1), up)


def _rot_apply(x, c, s, transpose):
    outs = []
    for k in range(x.shape[1] // 128):
        xs = x[:, k * 128:(k + 1) * 128]
        outs.append(c * xs + (_pair_swap(s * xs) if transpose else s * _pair_swap(xs)))
    return jnp.concatenate(outs, axis=1)


def _qkv_prep(proj, tab_c, tab_s):
    s = proj.shape[0]
    w = _A_WIDTH

    def body(x_ref, c_ref, s_ref, o_ref):
        o_ref[...] = _rot_apply(x_ref[...], c_ref[...], s_ref[...], False).astype(_BF)

    tab = pl.BlockSpec((None, _TM, 128), lambda i, j: (j, i, 0))
    blk = pl.BlockSpec((_TM, w), lambda i, j: (i, j))
    return pl.pallas_call(
        body, name="qkv_prep", grid=(s // _TM, 3), in_specs=[blk, tab, tab], out_specs=blk,
        out_shape=jax.ShapeDtypeStruct((s, 3 * w), _BF), compiler_params=_params(("arbitrary", "arbitrary")),
    )(proj, tab_c, tab_s)


def _head_masks():
    lane = lax.broadcasted_iota(jnp.int32, (_WIN_BLOCK, 128), 1)
    return [lane // _A_HEAD_DIM == hh for hh in range(2)]


def _band_masks(first_block):
    qi = lax.broadcasted_iota(jnp.int32, (_WIN_BLOCK, _WIN_BLOCK), 0)
    kj = lax.broadcasted_iota(jnp.int32, (_WIN_BLOCK, _WIN_BLOCK), 1)
    return (kj >= qi) & jnp.logical_not(first_block), kj <= qi


def _attn_fwd(qkv, dil):
    s = qkv.shape[0]
    w = _A_WIDTH
    length = s // dil
    nb = length // _WIN_BLOCK
    view = qkv.reshape(length, dil * 3 * w)

    def body(q_ref, kp_ref, kc_ref, vp_ref, vc_ref, o_ref, l_ref):
        allow_p, allow_c = _band_masks(pl.program_id(1) == 0)
        hm = _head_masks()
        for sl in range(w // 128):
            cols = slice(sl * 128, (sl + 1) * 128)
            q, kp, kc, vp, vc = q_ref[:, cols], kp_ref[:, cols], kc_ref[:, cols], vp_ref[:, cols], vc_ref[:, cols]
            o_s = jnp.zeros((_WIN_BLOCK, 128), _F32)
            l_s = jnp.zeros((_WIN_BLOCK, 128), _F32)
            for hh in range(2):
                qh = jnp.where(hm[hh], q, jnp.zeros_like(q))
                sp = jnp.where(allow_p, _dot_nt(qh, kp), _NEG)
                sc = jnp.where(allow_c, _dot_nt(qh, kc), _NEG)
                m = jnp.maximum(jnp.max(sp, axis=1, keepdims=True), jnp.max(sc, axis=1, keepdims=True))
                pp, pc = jnp.exp(sp - m), jnp.exp(sc - m)
                den = jnp.sum(pp, axis=1, keepdims=True) + jnp.sum(pc, axis=1, keepdims=True)
                oh = (_dot(pp, vp) + _dot(pc, vc)) / den
                o_s = jnp.where(hm[hh], oh, o_s)
                l_s = jnp.where(hm[hh], m + jnp.log(den), l_s)
            o_ref[:, cols] = o_s
            l_ref[:, cols] = l_s

    def at(col, prev):
        if prev:
            return pl.BlockSpec((_WIN_BLOCK, w), lambda r, n: (jnp.maximum(n - 1, 0), 3 * r + col))
        return pl.BlockSpec((_WIN_BLOCK, w), lambda r, n: (n, 3 * r + col))

    out = pl.BlockSpec((_WIN_BLOCK, w), lambda r, n: (n, r))
    sds = jax.ShapeDtypeStruct((length, dil * w), _F32)
    o, lse = pl.pallas_call(
        body, name=f"attn_fwd_d{dil}", grid=(dil, nb),
        in_specs=[at(0, False), at(1, True), at(1, False), at(2, True), at(2, False)],
        out_specs=[out, out], out_shape=[sds, sds], compiler_params=_params(("arbitrary", "arbitrary")),
    )(view, view, view, view, view)
    return o.reshape(s, w), lse.reshape(s, w)


def _attn_bwd(qkv, o, lse, do, dlse, dil):
    s = qkv.shape[0]
    w = _A_WIDTH
    length = s // dil
    nb = length // _WIN_BLOCK
    view = qkv.reshape(length, dil * 3 * w)
    o, lse, do, dlse = (t.reshape(length, dil * w) for t in (o, lse, do, dlse))

    def body(q_ref, kp_ref, kc_ref, vp_ref, vc_ref, o_ref, l_ref, do_ref, dl_ref, out_ref,
             car_q, car_k, car_v, cur_q, cur_kc, cur_vc, cur_kp, cur_vp):
        n = pl.program_id(1)

        @pl.when(n == 0)
        def _():
            car_q[...] = jnp.zeros_like(car_q)
            car_k[...] = jnp.zeros_like(car_k)
            car_v[...] = jnp.zeros_like(car_v)

        @pl.when(n == nb)
        def _():
            cur_kp[...] = jnp.zeros_like(cur_kp)
            cur_vp[...] = jnp.zeros_like(cur_vp)

        @pl.when(n < nb)
        def _():
            allow_p, allow_c = _band_masks(n == 0)
            hm = _head_masks()
            for sl in range(w // 128):
                cols = slice(sl * 128, (sl + 1) * 128)
                q, kp, kc, vp, vc = q_ref[:, cols], kp_ref[:, cols], kc_ref[:, cols], vp_ref[:, cols], vc_ref[:, cols]
                ov, lv, dov, dlv = o_ref[:, cols], l_ref[:, cols], do_ref[:, cols], dl_ref[:, cols]
                zero = jnp.zeros((_WIN_BLOCK, 128), _F32)
                dq_s, dkp_s, dkc_s, dvp_s, dvc_s = zero, zero, zero, zero, zero
                for hh in range(2):
                    qh = jnp.where(hm[hh], q, jnp.zeros_like(q))
                    doh = jnp.where(hm[hh], dov, 0.0).astype(_BF)
                    lse_h = jnp.sum(jnp.where(hm[hh], lv, 0.0), axis=1, keepdims=True) * (1.0 / _A_HEAD_DIM)
                    delta = jnp.sum(jnp.where(hm[hh], dov * ov, 0.0), axis=1, keepdims=True)
                    dl = jnp.sum(jnp.where(hm[hh], dlv, 0.0), axis=1, keepdims=True)
                    pp = jnp.exp(jnp.where(allow_p, _dot_nt(qh, kp), _NEG) - lse_h)
                    pc = jnp.exp(jnp.where(allow_c, _dot_nt(qh, kc), _NEG) - lse_h)
                    dsp = (pp * (_dot_nt(doh, vp) - delta + dl)).astype(_BF)
                    dsc = (pc * (_dot_nt(doh, vc) - delta + dl)).astype(_BF)
                    dq_s = dq_s + jnp.where(hm[hh], _dot(dsp, kp) + _dot(dsc, kc), 0.0)
                    dkp_s = dkp_s + _dot_tn(dsp, qh)
                    dkc_s = dkc_s + _dot_tn(dsc, qh)
                    dvp_s = dvp_s + _dot_tn(pp, doh)
                    dvc_s = dvc_s + _dot_tn(pc, doh)
                cur_q[:, cols] = dq_s
                cur_kp[:, cols] = dkp_s
                cur_kc[:, cols] = dkc_s
                cur_vp[:, cols] = dvp_s
                cur_vc[:, cols] = dvc_s

        out_ref[:, 0:w] = car_q[...]
        out_ref[:, w:2 * w] = car_k[...] + cur_kp[...]
        out_ref[:, 2 * w:3 * w] = car_v[...] + cur_vp[...]

        @pl.when(n < nb)
        def _():
            car_q[...] = cur_q[...]
            car_k[...] = cur_kc[...]
            car_v[...] = cur_vc[...]

    def at(col, prev):
        if prev:
            return pl.BlockSpec((_WIN_BLOCK, w), lambda r, n: (jnp.maximum(jnp.minimum(n, nb - 1) - 1, 0), 3 * r + col))
        return pl.BlockSpec((_WIN_BLOCK, w), lambda r, n: (jnp.minimum(n, nb - 1), 3 * r + col))

    blk = pl.BlockSpec((_WIN_BLOCK, w), lambda r, n: (jnp.minimum(n, nb - 1), r))
    scratch = [pltpu.VMEM((_WIN_BLOCK, w), _F32) for _ in range(8)]
    out = pl.pallas_call(
        body, name=f"attn_bwd_d{dil}", grid=(dil, nb + 1),
        in_specs=[at(0, False), at(1, True), at(1, False), at(2, True), at(2, False), blk, blk, blk, blk],
        out_specs=pl.BlockSpec((_WIN_BLOCK, 3 * w), lambda r, n: (jnp.maximum(n - 1, 0), r)),
        out_shape=jax.ShapeDtypeStruct((length, dil * 3 * w), _F32), scratch_shapes=scratch,
        compiler_params=_params(("arbitrary", "arbitrary")),
    )(view, view, view, view, view, o, lse, do, dlse)
    return out.reshape(s, 3 * w)


def _mix(o1, o2, o3, l1, l2, l3):
    m = jnp.maximum(jnp.maximum(l1, l2), l3)
    e1, e2, e3 = jnp.exp(l1 - m), jnp.exp(l2 - m), jnp.exp(l3 - m)
    return (e1 * o1 + e2 * o2 + e3 * o3) / (e1 + e2 + e3)


def _attn_mix_fwd(outs, lses):
    s = outs[0].shape[0]
    tiles = [(t, _A_WIDTH, 0) for t in (*outs, *lses)]
    return _rowmap("attn_mix_fwd", _mix, tiles, [], [(2 * _A_WIDTH, _A_WIDTH, 0, _BF)], rows=s, tm=_CONV_TM)[0]


def _attn_mix_bwd(outs, lses, dy):
    s = outs[0].shape[0]
    tiles = [(t, _A_WIDTH, 0) for t in (*outs, *lses, dy)]
    body = _vjp_body(_mix, 6, range(6), 1)
    return _rowmap("attn_mix_bwd", body, tiles, [], [(_A_WIDTH, _A_WIDTH, 0, _F32)] * 6, rows=s, tm=_CONV_TM)


_CONV_TM = 256
_HALO = 32


def _ln_silu(pre, g, b):
    mu = jnp.mean(pre, axis=-1, keepdims=True)
    xc = pre - mu
    y = xc * lax.rsqrt(jnp.mean(xc * xc, axis=-1, keepdims=True) + _NORM_EPS)
    return _silu(y * g + b)


def _glu_into(buf, ua_ref, ub_ref, uah_ref, ubh_ref, first):
    halo = uah_ref[...] * jax.nn.sigmoid(ubh_ref[...])
    buf[0:_HALO, :] = jnp.where(first, 0.0, halo)
    buf[_HALO:, :] = ua_ref[...] * jax.nn.sigmoid(ub_ref[...])


def _causal_taps(buf, w_ref, rows):
    base = _HALO - (_CONV_WIDTH - 1)
    acc = jnp.zeros((rows, buf.shape[1]), _F32)
    for j in range(_CONV_WIDTH):
        acc = acc + w_ref[j:j + 1, :] * buf[base + j:base + j + rows, :]
    return acc


def _conv_specs(s):
    tm = _CONV_TM
    per = tm // _HALO
    cw = _A_WIDTH
    cur = lambda col: pl.BlockSpec((tm, cw), lambda i: (i, col))
    prev = lambda col: pl.BlockSpec((_HALO, cw), lambda i: (jnp.maximum(i * per - 1, 0), col))
    return tm, per, cw, cur, prev


def _full2(arr):
    return pl.BlockSpec(arr.shape, lambda i: (0, 0))


def _conv_fwd(proj, dw_w, dw_b, ln_g, ln_b, y):
    s = proj.shape[0]
    tm, per, cw, cur, prev = _conv_specs(s)

    def body(ua_ref, ub_ref, uah_ref, ubh_ref, w_ref, b_ref, g_ref, be_ref, y_in, o_ref, buf):
        del y_in
        _glu_into(buf, ua_ref, ub_ref, uah_ref, ubh_ref, pl.program_id(0) == 0)
        pre = _causal_taps(buf, w_ref, tm) + b_ref[...]
        o_ref[...] = _ln_silu(pre, g_ref[...], be_ref[...]).astype(_BF)

    return pl.pallas_call(
        body, name="conv_fwd", grid=(s // tm,),
        in_specs=[cur(3), cur(4), prev(3), prev(4), _full2(dw_w), _full2(dw_b), _full2(ln_g), _full2(ln_b), _ANY],
        out_specs=pl.BlockSpec((tm, cw), lambda i: (i, 1)), out_shape=jax.ShapeDtypeStruct(y.shape, y.dtype),
        scratch_shapes=[pltpu.VMEM((tm + _HALO, cw), _F32)], input_output_aliases={8: 0},
        compiler_params=_params(("arbitrary",)),
    )(proj, proj, proj, proj, dw_w, dw_b, ln_g, ln_b, y)


def _conv_bwd_pre(proj, dw_w, dw_b, ln_g, ln_b, dy):
    s = proj.shape[0]
    tm, per, cw, cur, prev = _conv_specs(s)

    def body(ua_ref, ub_ref, uah_ref, ubh_ref, w_ref, b_ref, g_ref, be_ref, dy_ref, dp_ref, dg_ref, dbe_ref, db_ref, buf):
        i = pl.program_id(0)
        _glu_into(buf, ua_ref, ub_ref, uah_ref, ubh_ref, i == 0)
        pre = _causal_taps(buf, w_ref, tm) + b_ref[...]
        _, pull = jax.vjp(_ln_silu, pre, g_ref[...], be_ref[...])
        dpre, dg, dbe = pull(dy_ref[...])
        dp_ref[...] = dpre

        @pl.when(i == 0)
        def _():
            dg_ref[...] = jnp.zeros_like(dg_ref)
            dbe_ref[...] = jnp.zeros_like(dbe_ref)
            db_ref[...] = jnp.zeros_like(db_ref)

        dg_ref[...] += dg
        dbe_ref[...] += dbe
        db_ref[...] += jnp.sum(dpre, axis=0, keepdims=True)

    vec = pl.BlockSpec((1, cw), lambda i: (0, 0))
    vsd = jax.ShapeDtypeStruct((1, cw), _F32)
    return pl.pallas_call(
        body, name="conv_bwd_pre", grid=(s // tm,),
        in_specs=[cur(3), cur(4), prev(3), prev(4), _full2(dw_w), _full2(dw_b), _full2(ln_g), _full2(ln_b), cur(1)],
        out_specs=[pl.BlockSpec((tm, cw), lambda i: (i, 0)), vec, vec, vec],
        out_shape=[jax.ShapeDtypeStruct((s, cw), _F32), vsd, vsd, vsd],
        scratch_shapes=[pltpu.VMEM((tm + _HALO, cw), _F32)], compiler_params=_params(("arbitrary",)),
    )(proj, proj, proj, proj, dw_w, dw_b, ln_g, ln_b, dy)


def _conv_bwd_taps(proj, dw_w, dpre):
    s = proj.shape[0]
    tm, per, cw, cur, prev = _conv_specs(s)
    nt = s // tm

    def body(ua_ref, ub_ref, uah_ref, ubh_ref, w_ref, dp_ref, dpn_ref, dglu_ref, dw_ref, buf, dbuf):
        i = pl.program_id(0)
        _glu_into(buf, ua_ref, ub_ref, uah_ref, ubh_ref, i == 0)
        dpre = dp_ref[...]
        dbuf[0:tm, :] = dpre
        dbuf[tm:, :] = jnp.where(i == nt - 1, 0.0, dpn_ref[...])
        acc = jnp.zeros((tm, cw), _F32)
        for j in range(_CONV_WIDTH):
            off = _CONV_WIDTH - 1 - j
            acc = acc + w_ref[j:j + 1, :] * dbuf[off:off + tm, :]
        dglu_ref[...] = acc

        @pl.when(i == 0)
        def _():
            dw_ref[...] = jnp.zeros_like(dw_ref)

        base = _HALO - (_CONV_WIDTH - 1)
        for j in range(_CONV_WIDTH):
            dw_ref[j:j + 1, :] += jnp.sum(dpre * buf[base + j:base + j + tm, :], axis=0, keepdims=True)

    nxt = pl.BlockSpec((_HALO, cw), lambda i: (jnp.minimum((i + 1) * per, s // _HALO - 1), 0))
    return pl.pallas_call(
        body, name="conv_bwd_taps", grid=(nt,),
        in_specs=[cur(3), cur(4), prev(3), prev(4), _full2(dw_w), pl.BlockSpec((tm, cw), lambda i: (i, 0)), nxt],
        out_specs=[pl.BlockSpec((tm, cw), lambda i: (i, 0)), pl.BlockSpec((_HALO, cw), lambda i: (0, 0))],
        out_shape=[jax.ShapeDtypeStruct((s, cw), _F32), jax.ShapeDtypeStruct((_HALO, cw), _F32)],
        scratch_shapes=[pltpu.VMEM((tm + _HALO, cw), _F32), pltpu.VMEM((tm + _HALO, cw), _F32)],
        compiler_params=_params(("arbitrary",)),
    )(proj, proj, proj, proj, dw_w, dpre, dpre)


def _hyb_dproj(dqkv, tab_c, tab_s, dglu, proj):
    s = proj.shape[0]
    w = _A_WIDTH

    def body(d1_ref, d2_ref, d3_ref, c_ref, s_ref, dg_ref, ua_ref, ub_ref, o_ref):
        j = pl.program_id(1)

        @pl.when(j < 3)
        def _():
            x = d1_ref[...] + d2_ref[...] + d3_ref[...]
            o_ref[...] = _rot_apply(x, c_ref[...], s_ref[...], True).astype(_BF)

        @pl.when(j == 3)
        def _():
            o_ref[...] = (dg_ref[...] * jax.nn.sigmoid(ub_ref[...])).astype(_BF)

        @pl.when(j == 4)
        def _():
            sg = jax.nn.sigmoid(ub_ref[...])
            o_ref[...] = (dg_ref[...] * ua_ref[...] * sg * (1.0 - sg)).astype(_BF)

    dq = pl.BlockSpec((_TM, w), lambda i, j: (i, jnp.minimum(j, 2)))
    tab = pl.BlockSpec((None, _TM, 128), lambda i, j: (jnp.minimum(j, 2), i, 0))
    col = lambda c: pl.BlockSpec((_TM, w), lambda i, j: (i, c))
    return pl.pallas_call(
        body, name="hyb_dproj", grid=(s // _TM, 5),
        in_specs=[dq, dq, dq, tab, tab, col(0), col(3), col(4)],
        out_specs=pl.BlockSpec((_TM, w), lambda i, j: (i, j)),
        out_shape=jax.ShapeDtypeStruct((s, 5 * w), _BF), compiler_params=_params(("arbitrary", "arbitrary")),
    )(*dqkv, tab_c, tab_s, dglu, proj, proj)


def _proj_in(name, xn, w, layer):
    s, d = xn.shape
    ns = w.shape[-1]
    return _mm(name, xn, w, grid=(4, s // _TM), a_spec=pl.BlockSpec((_TM, d), lambda j, i: (i, 0)),
               b_spec=pl.BlockSpec((None, None, d, ns), lambda j, i: (layer, j, 0, 0)), dn=_NN,
               out_shape=jax.ShapeDtypeStruct((s, 4 * ns), _F32), out_spec=pl.BlockSpec((_TM, ns), lambda j, i: (i, j)))


def _proj_in_dx(name, dz, w, layer):
    s = dz.shape[0]
    d, ns = w.shape[2], w.shape[3]
    return _mm(name, dz, w, grid=(s // _TM, 4), a_spec=pl.BlockSpec((_TM, ns), lambda i, j: (i, j)),
               b_spec=pl.BlockSpec((None, None, d, ns), lambda i, j: (layer, j, 0, 0)), dn=_NT, red=1,
               acc_shape=(_TM, d), out_shape=jax.ShapeDtypeStruct((s, d), _F32),
               out_spec=pl.BlockSpec((_TM, d), lambda i, j: (i, 0)))


def _proj_in_dw(name, xn, dz):
    s, d = xn.shape
    ns = dz.shape[1] // 4
    return _mm(name, xn, dz, grid=(4, s // _TM), a_spec=pl.BlockSpec((_TM, d), lambda j, i: (i, 0)),
               b_spec=pl.BlockSpec((_TM, ns), lambda j, i: (i, j)), dn=_TN, red=1, acc_shape=(d, ns),
               out_shape=jax.ShapeDtypeStruct((4, d, ns), _BF),
               out_spec=pl.BlockSpec((None, d, ns), lambda j, i: (j, 0, 0)))


def _proj_out_dy(name, dh, w, layer):
    s, d = dh.shape
    k = w.shape[1]
    return _mm(name, dh, w, grid=(s // _TM,), a_spec=pl.BlockSpec((_TM, d), lambda i: (i, 0)),
               b_spec=pl.BlockSpec((None, k, d), lambda i: (layer, 0, 0)), dn=_NT,
               out_shape=jax.ShapeDtypeStruct((s, k), _F32), out_spec=pl.BlockSpec((_TM, k), lambda i: (i, 0)))


def _hyb_fwd(h, gamma, w_in, w_out, dw_w, dw_b, ln_g, ln_b, tabs, li):
    hn = _rmsnorm_fwd(h, gamma)
    proj = _proj_in("hyb_in", hn, w_in, li)
    qkv = _qkv_prep(proj, *tabs)
    branches = [_attn_fwd(qkv, dil) for dil in _DILATIONS]
    outs, lses = [b[0] for b in branches], [b[1] for b in branches]
    y = _attn_mix_fwd(outs, lses)
    y = _conv_fwd(proj, dw_w, dw_b, ln_g, ln_b, y)
    h_new = _proj_residual("hyb_out", y, w_out, li, h, 1.0)
    return h_new, (h, hn, proj, qkv, outs, lses, y)


def _hyb_bwd(dh, saved, gamma, w_in, w_out, dw_w, dw_b, ln_g, ln_b, tabs, li):
    h, hn, proj, qkv, outs, lses, y = saved
    dy = _proj_out_dy("hyb_dy", dh, w_out, li)
    d_w_out = _wgrad_rows("hyb_dwout", y, dh, 1.0)
    mixg = _attn_mix_bwd(outs, lses, dy)
    dqkv = [_attn_bwd(qkv, outs[b], lses[b], mixg[b], mixg[3 + b], dil) for b, dil in enumerate(_DILATIONS)]
    dpre, d_ln_g, d_ln_b, d_dw_b = _conv_bwd_pre(proj, dw_w, dw_b, ln_g, ln_b, dy)
    dglu, d_dw_w = _conv_bwd_taps(proj, dw_w, dpre)
    dproj = _hyb_dproj(dqkv, *tabs, dglu, proj)
    dhn = _proj_in_dx("hyb_dhn", dproj, w_in, li)
    d_w_in = _proj_in_dw("hyb_dwin", hn, dproj)
    dh_in, dgamma = _rmsnorm_bwd(h, gamma, dhn, dh)
    return dh_in, dgamma, (d_dw_w, d_dw_b, d_ln_g, d_ln_b), d_w_in, d_w_out


_GDN_BLOCK = 64
_GDN_HEADS_PER_STEP = 8


def _gdn_chunk(qs, ks, vs, gates, states, heads):
    n = len(qs)
    hs = range(n)
    c = qs[0].shape[0]
    lane = lax.broadcasted_iota(jnp.int32, gates.shape, 1)
    row = lax.broadcasted_iota(jnp.int32, (c, c), 0)
    col = lax.broadcasted_iota(jnp.int32, (c, c), 1)
    rowc = lax.broadcasted_iota(jnp.int32, (c, 1), 0)
    causal = row >= col
    eye = jnp.where(row == col, 1.0, 0.0)
    beta = [jnp.sum(jnp.where(lane == heads[h], gates, 0.0), axis=1, keepdims=True) for h in hs]
    g = [jnp.sum(jnp.where(lane == heads[h] + _GDN_HEADS, gates, 0.0), axis=1, keepdims=True) for h in hs]
    gc_row = [jnp.sum(jnp.where(row <= col, g[h], 0.0), axis=0, keepdims=True) for h in hs]
    gc_col = [jnp.sum(jnp.where(row == col, gc_row[h], 0.0), axis=1, keepdims=True) for h in hs]
    decay = [jnp.where(causal, jnp.exp(jnp.where(causal, gc_col[h] - gc_row[h], 0.0)), 0.0) for h in hs]
    kb = [ks[h] * beta[h] for h in hs]
    l_mat = [jnp.where(row > col, _dot_nt(kb[h], ks[h]) * decay[h], 0.0) for h in hs]
    t_inv = [eye - l_mat[h] for h in hs]
    x = [_dot_hi(l_mat[h], l_mat[h]) for h in hs]
    for it in range(5):
        t_inv = [t_inv[h] + _dot_hi(t_inv[h], x[h]) for h in hs]
        if it < 4:
            x = [_dot_hi(x[h], x[h]) for h in hs]
    eg = [jnp.exp(gc_col[h]) for h in hs]
    u = [_dot(t_inv[h], vs[h] * beta[h]) for h in hs]
    w = [_dot(t_inv[h], kb[h] * eg[h]) for h in hs]
    attn = [jnp.where(causal, _dot_nt(qs[h], ks[h]) * decay[h], 0.0) for h in hs]
    g_last = [jnp.sum(jnp.where(rowc == c - 1, gc_col[h], 0.0), axis=0, keepdims=True) for h in hs]
    k_dec = [ks[h] * jnp.exp(g_last[h] - gc_col[h]) for h in hs]
    v_new = [u[h] - _dot(w[h], states[h]) for h in hs]
    o = [_dot(qs[h] * eg[h], states[h]) + _dot(attn[h], v_new[h]) for h in hs]
    new_states = [states[h] * jnp.exp(g_last[h]) + _dot_tn(k_dec[h], v_new[h]) for h in hs]
    return o, new_states


def _gdn_chunk_fwd(q, k, v, gates):
    s = q.shape[0]
    cb, c = _GDN_BLOCK, _GDN_CHUNK
    nblk, per = s // cb, cb // c

    hps = _GDN_HEADS_PER_STEP

    def body(q_ref, k_ref, v_ref, g_ref, o_ref, st_ref, state):
        @pl.when(pl.program_id(1) == 0)
        def _():
            state[...] = jnp.zeros_like(state)

        heads = [pl.program_id(0) * hps + hh for hh in range(hps)]
        cols = [slice(hh * _GDN_DIM, (hh + 1) * _GDN_DIM) for hh in range(hps)]
        for ci in range(per):
            rows = slice(ci * c, (ci + 1) * c)
            sts = [state[hh] for hh in range(hps)]
            for hh in range(hps):
                st_ref[hh, ci] = sts[hh]
            outs, new = _gdn_chunk([q_ref[rows, cl] for cl in cols], [k_ref[rows, cl] for cl in cols],
                                   [v_ref[rows, cl] for cl in cols], g_ref[rows, :], sts, heads)
            for hh in range(hps):
                o_ref[rows, cols[hh]] = outs[hh]
                state[hh] = new[hh]

    slab = pl.BlockSpec((cb, hps * _GDN_DIM), lambda h, i: (i, h))
    return pl.pallas_call(
        body, name="gdn_chunk_fwd", grid=(_GDN_HEADS // hps, nblk),
        in_specs=[slab, slab, slab, pl.BlockSpec((cb, 128), lambda h, i: (i, 0))],
        out_specs=[slab, pl.BlockSpec((hps, per, _GDN_DIM, _GDN_DIM), lambda h, i: (h, i, 0, 0))],
        out_shape=[jax.ShapeDtypeStruct((s, _GDN_HEADS * _GDN_DIM), _F32),
                   jax.ShapeDtypeStruct((_GDN_HEADS, s // c, _GDN_DIM, _GDN_DIM), _F32)],
        scratch_shapes=[pltpu.VMEM((hps, _GDN_DIM, _GDN_DIM), _F32)],
        compiler_params=_params(("arbitrary", "arbitrary")),
    )(q, k, v, gates)


def _gdn_chunk_bwd(q, k, v, gates, states, do):
    s = q.shape[0]
    cb, c = _GDN_BLOCK, _GDN_CHUNK
    nblk, per = s // cb, cb // c

    hps = _GDN_HEADS_PER_STEP

    def body(q_ref, k_ref, v_ref, g_ref, st_ref, do_ref, dq_ref, dk_ref, dv_ref, dg_ref, dstate):
        @pl.when(pl.program_id(1) == 0)
        def _():
            dstate[...] = jnp.zeros_like(dstate)

        heads = [pl.program_id(0) * hps + hh for hh in range(hps)]
        cols = [slice(hh * _GDN_DIM, (hh + 1) * _GDN_DIM) for hh in range(hps)]
        for ci in reversed(range(per)):
            rows = slice(ci * c, (ci + 1) * c)
            fn = functools.partial(_gdn_chunk, heads=heads)
            _, pull = jax.vjp(fn, [q_ref[rows, cl] for cl in cols], [k_ref[rows, cl] for cl in cols],
                              [v_ref[rows, cl] for cl in cols], g_ref[rows, :], [st_ref[hh, ci] for hh in range(hps)])
            dq, dk, dv, dg, dst = pull(([do_ref[rows, cl] for cl in cols], [dstate[hh] for hh in range(hps)]))
            dg_ref[rows, :] = dg
            for hh in range(hps):
                dq_ref[rows, cols[hh]] = dq[hh]
                dk_ref[rows, cols[hh]] = dk[hh]
                dv_ref[rows, cols[hh]] = dv[hh]
                dstate[hh] = dst[hh]

    slab = pl.BlockSpec((cb, hps * _GDN_DIM), lambda h, i: (nblk - 1 - i, h))
    sds = jax.ShapeDtypeStruct((s, _GDN_HEADS * _GDN_DIM), _F32)
    return pl.pallas_call(
        body, name="gdn_chunk_bwd", grid=(_GDN_HEADS // hps, nblk),
        in_specs=[slab, slab, slab, pl.BlockSpec((cb, 128), lambda h, i: (nblk - 1 - i, 0)),
                  pl.BlockSpec((hps, per, _GDN_DIM, _GDN_DIM), lambda h, i: (h, nblk - 1 - i, 0, 0)), slab],
        out_specs=[slab, slab, slab, pl.BlockSpec((None, cb, 128), lambda h, i: (h, nblk - 1 - i, 0))],
        out_shape=[sds, sds, sds, jax.ShapeDtypeStruct((_GDN_HEADS // hps, s, 128), _F32)],
        scratch_shapes=[pltpu.VMEM((hps, _GDN_DIM, _GDN_DIM), _F32)],
        compiler_params=_params(("arbitrary", "arbitrary")),
    )(q, k, v, gates, states, do)


_SHALO = 8


def _swish_l2(c, normalize, scale):
    y = _silu(c)
    if normalize:
        y = y * lax.rsqrt(jnp.sum(y * y, axis=-1, keepdims=True) + _NORM_EPS) * scale
    return y


def _short_taps(buf, w_ref, rows):
    base = _SHALO - (_GDN_SHORT_CONV - 1)
    acc = jnp.zeros((rows, buf.shape[1]), _F32)
    for t in range(_GDN_SHORT_CONV):
        acc = acc + w_ref[t:t + 1, :] * buf[base + t:base + t + rows, :]
    return acc


_GDN_W = _GDN_HEADS * _GDN_DIM


def _gdn_conv_fwd(proj, conv_w, grp, normalize, scale):
    s = proj.shape[0]
    tm = _CONV_TM
    per = tm // _SHALO

    def body(x_ref, xp_ref, w_ref, o_ref, buf):
        buf[0:_SHALO, :] = jnp.where(pl.program_id(0) == 0, 0.0, xp_ref[...])
        buf[_SHALO:, :] = x_ref[...]
        conv = _short_taps(buf, w_ref, tm)
        for h in range(_GDN_HEADS):
            cols = slice(h * _GDN_DIM, (h + 1) * _GDN_DIM)
            o_ref[:, cols] = _swish_l2(conv[:, cols], normalize, scale)

    return pl.pallas_call(
        body, name=f"gdn_conv_fwd_{grp}", grid=(s // tm,),
        in_specs=[pl.BlockSpec((tm, _GDN_W), lambda i: (i, grp)),
                  pl.BlockSpec((_SHALO, _GDN_W), lambda i: (jnp.maximum(i * per - 1, 0), grp)),
                  pl.BlockSpec((_GDN_SHORT_CONV, _GDN_W), lambda i: (0, grp))],
        out_specs=pl.BlockSpec((tm, _GDN_W), lambda i: (i, 0)),
        out_shape=jax.ShapeDtypeStruct((s, _GDN_W), _F32),
        scratch_shapes=[pltpu.VMEM((tm + _SHALO, _GDN_W), _F32)], compiler_params=_params(("arbitrary",)),
    )(proj, proj, conv_w)


def _gdn_conv_bwd(proj, conv_w, grp, normalize, scale, dy, dproj):
    s = proj.shape[0]
    tm = _CONV_TM
    per = tm // _SHALO
    nt = s // tm
    taps = _GDN_SHORT_CONV

    def body(x_ref, xp_ref, xn_ref, w_ref, dy_ref, dyn_ref, dp_in, dx_ref, dw_ref, buf, dbuf):
        del dp_in
        i = pl.program_id(0)
        last = i == nt - 1
        buf[0:_SHALO, :] = jnp.where(i == 0, 0.0, xp_ref[...])
        buf[_SHALO:_SHALO + tm, :] = x_ref[...]
        buf[_SHALO + tm:, :] = jnp.where(last, 0.0, xn_ref[...])
        conv = _short_taps(buf, w_ref, tm + _SHALO)
        dy_ext = jnp.concatenate([dy_ref[...], jnp.where(last, 0.0, dyn_ref[...])], axis=0)
        for h in range(_GDN_HEADS):
            cols = slice(h * _GDN_DIM, (h + 1) * _GDN_DIM)
            _, pull = jax.vjp(lambda c: _swish_l2(c, normalize, scale), conv[:, cols])
            dbuf[:, cols] = pull(dy_ext[:, cols])[0]
        acc = jnp.zeros((tm, _GDN_W), _F32)
        for t in range(taps):
            o = taps - 1 - t
            acc = acc + w_ref[t:t + 1, :] * dbuf[o:o + tm, :]
        dx_ref[...] = acc.astype(dx_ref.dtype)

        @pl.when(i == 0)
        def _():
            dw_ref[...] = jnp.zeros_like(dw_ref)

        base = _SHALO - (taps - 1)
        for t in range(taps):
            dw_ref[t:t + 1, :] += jnp.sum(dbuf[0:tm, :] * buf[base + t:base + t + tm, :], axis=0, keepdims=True)

    cur = lambda col: pl.BlockSpec((tm, _GDN_W), lambda i: (i, col))
    nxt = lambda col: pl.BlockSpec((_SHALO, _GDN_W), lambda i: (jnp.minimum((i + 1) * per, s // _SHALO - 1), col))
    return pl.pallas_call(
        body, name=f"gdn_conv_bwd_{grp}", grid=(nt,),
        in_specs=[cur(grp), pl.BlockSpec((_SHALO, _GDN_W), lambda i: (jnp.maximum(i * per - 1, 0), grp)), nxt(grp),
                  pl.BlockSpec((taps, _GDN_W), lambda i: (0, grp)), cur(0), nxt(0), _ANY],
        out_specs=[cur(grp), pl.BlockSpec((_SHALO, _GDN_W), lambda i: (0, 0))],
        out_shape=[jax.ShapeDtypeStruct(dproj.shape, dproj.dtype), jax.ShapeDtypeStruct((_SHALO, _GDN_W), _F32)],
        scratch_shapes=[pltpu.VMEM((tm + 2 * _SHALO, _GDN_W), _F32), pltpu.VMEM((tm + _SHALO, _GDN_W), _F32)],
        input_output_aliases={6: 0}, compiler_params=_params(("arbitrary",)),
    )(proj, proj, proj, conv_w, dy, dy, dproj)


def _softplus(z):
    return jnp.maximum(z, 0.0) + jnp.log(1.0 + jnp.exp(-jnp.abs(z)))


def _gates(ba, a_lane, dt_lane):
    lane = lax.broadcasted_iota(jnp.int32, ba.shape, 1)
    g = -jnp.exp(a_lane) * _softplus(ba + dt_lane)
    return jnp.where(lane < _GDN_HEADS, jax.nn.sigmoid(ba), jnp.where(lane < 2 * _GDN_HEADS, g, 0.0))


def _gates_fwd(ba, a_lane, dt_lane):
    return _rowmap("gdn_gates_fwd", _gates, [(ba, 128, 0)], [a_lane, dt_lane], [(128, 128, 0, _F32)],
                   rows=ba.shape[0], tm=_TM)[0]


def _gates_bwd(ba, a_lane, dt_lane, dgates):
    def body(bav, dg8, av, dv):
        _, pull = jax.vjp(_gates, bav, av, dv)
        return pull(jnp.sum(dg8, axis=0))

    return _rowmap("gdn_gates_bwd", body, [(ba, 128, 0), (dgates, 128, 0)], [a_lane, dt_lane], [(128, 128, 0, _BF)],
                   [(1, 128), (1, 128)], rows=ba.shape[0], tm=_TM)


def _out_gate(o, z, g):
    return (_rms(o, g) * _silu(z)).astype(_BF)


def _head_cols():
    return [slice(h * _GDN_DIM, (h + 1) * _GDN_DIM) for h in range(_GDN_HEADS)]


def _out_gate_fwd(o, proj, norm_g):
    def body(ov, zv, gv):
        return jnp.concatenate([_out_gate(ov[:, cl], zv[:, cl], gv) for cl in _head_cols()], axis=1)

    return _rowmap("gdn_out_gate_fwd", body, [(o, _GDN_W, 0), (proj, _GDN_W, 3)], [norm_g], [(_GDN_W, _GDN_W, 0, _BF)],
                   rows=o.shape[0], tm=_CONV_TM)[0]


def _out_gate_bwd(o, proj, norm_g, dy):
    def body(ov, zv, dyv, gv):
        d_o, d_z, d_g = [], [], jnp.zeros_like(gv)
        for cl in _head_cols():
            _, pull = jax.vjp(_out_gate, ov[:, cl], zv[:, cl], gv)
            g_o, g_z, g_g = pull(dyv[:, cl].astype(_BF))
            d_o.append(g_o)
            d_z.append(g_z)
            d_g = d_g + g_g
        return jnp.concatenate(d_o, axis=1), jnp.concatenate(d_z, axis=1), d_g

    return _rowmap("gdn_out_gate_bwd", body, [(o, _GDN_W, 0), (proj, _GDN_W, 3), (dy, _GDN_W, 0)], [norm_g],
                   [(_GDN_W, _GDN_W, 0, _F32), (proj.shape[1], _GDN_W, 3, _BF)], [(1, _GDN_DIM)],
                   rows=o.shape[0], tm=_CONV_TM)


def _plain_in(name, xn, w, layer, tn):
    s, d = xn.shape
    n = w.shape[2]
    return _mm(name, xn, w, grid=(n // tn, s // _TM), a_spec=pl.BlockSpec((_TM, d), lambda j, i: (i, 0)),
               b_spec=pl.BlockSpec((None, d, tn), lambda j, i: (layer, 0, j)), dn=_NN,
               out_shape=jax.ShapeDtypeStruct((s, n), _F32), out_spec=pl.BlockSpec((_TM, tn), lambda j, i: (i, j)))


def _plain_dx(name, dz, w, layer, tn, extra=None):
    s, n = dz.shape
    d = w.shape[1]
    row = pl.BlockSpec((_TM, d), lambda i, j: (i, 0))
    return _mm(name, dz, w, grid=(s // _TM, n // tn), a_spec=pl.BlockSpec((_TM, tn), lambda i, j: (i, j)),
               b_spec=pl.BlockSpec((None, d, tn), lambda i, j: (layer, 0, j)), dn=_NT, red=1, acc_shape=(_TM, d),
               out_shape=jax.ShapeDtypeStruct((s, d), _F32), out_spec=row,
               epi=(lambda acc, e: acc + e) if extra is not None else None,
               extras=(extra,) if extra is not None else (), extra_specs=(row,) if extra is not None else ())


def _plain_dw(name, xn, dz, tn):
    s, d = xn.shape
    n = dz.shape[1]
    return _mm(name, xn, dz, grid=(n // tn, s // _TM), a_spec=pl.BlockSpec((_TM, d), lambda j, i: (i, 0)),
               b_spec=pl.BlockSpec((_TM, tn), lambda j, i: (i, j)), dn=_TN, red=1, acc_shape=(d, tn),
               out_shape=jax.ShapeDtypeStruct((d, n), _BF), out_spec=pl.BlockSpec((d, tn), lambda j, i: (0, j)))


_GDN_QK_SCALE = _GDN_DIM ** -0.5


def _gdn_fwd(h, gamma, w_qkvz, w_ba, w_out, conv_w, a_lane, dt_lane, norm_g, li):
    hn = _rmsnorm_fwd(h, gamma)
    proj = _plain_in("gdn_in", hn, w_qkvz, li, 1024)
    ba = _plain_in("gdn_in_ba", hn, w_ba, li, 128)
    q = _gdn_conv_fwd(proj, conv_w, 0, True, _GDN_QK_SCALE)
    k = _gdn_conv_fwd(proj, conv_w, 1, True, 1.0)
    v = _gdn_conv_fwd(proj, conv_w, 2, False, 1.0)
    gates = _gates_fwd(ba, a_lane, dt_lane)
    o, states = _gdn_chunk_fwd(q, k, v, gates)
    y = _out_gate_fwd(o, proj, norm_g)
    h_new = _proj_residual("gdn_out", y, w_out, li, h, 1.0)
    return h_new, (h, hn, proj, ba, q, k, v, gates, states, o, y)


def _gdn_bwd(dh, saved, gamma, w_qkvz, w_ba, w_out, conv_w, a_lane, dt_lane, norm_g, li, n_ba):
    h, hn, proj, ba, q, k, v, gates, states, o, y = saved
    dy = _proj_out_dy("gdn_dy", dh, w_out, li)
    d_w_out = _wgrad_rows("gdn_dwout", y, dh, 1.0)
    do, dproj, d_norm_g = _out_gate_bwd(o, proj, norm_g, dy)
    dq, dk, dv, dgates = _gdn_chunk_bwd(q, k, v, gates, states, do)
    dproj, dcw_q = _gdn_conv_bwd(proj, conv_w, 0, True, _GDN_QK_SCALE, dq, dproj)
    dproj, dcw_k = _gdn_conv_bwd(proj, conv_w, 1, True, 1.0, dk, dproj)
    dproj, dcw_v = _gdn_conv_bwd(proj, conv_w, 2, False, 1.0, dv, dproj)
    dba, d_a, d_dt = _gates_bwd(ba, a_lane, dt_lane, dgates)
    dhn_ba = _plain_dx("gdn_dhn_ba", dba, w_ba, li, 128)
    dhn = _plain_dx("gdn_dhn", dproj, w_qkvz, li, 1024, extra=dhn_ba)
    dw_qkvz = _plain_dw("gdn_dwin", hn, dproj, 1024)
    dw_ba = _plain_dw("gdn_dwin_ba", hn, dba, 128)
    dh_in, dgamma = _rmsnorm_bwd(h, gamma, dhn, dh)
    d_conv_w = jnp.concatenate([dcw_q, dcw_k, dcw_v], axis=1)[:_GDN_SHORT_CONV]
    d_w_in = jnp.concatenate([dw_qkvz, dw_ba[:, :n_ba]], axis=1)
    d_w_in = jnp.transpose(d_w_in.reshape(d_w_in.shape[0], 4, -1), (1, 0, 2))
    return dh_in, dgamma, (d_conv_w, d_a, d_dt, d_norm_g), d_w_in, d_w_out


_MESH = pl.DeviceIdType.MESH
_ROW_TM = 256


def _place():
    x, y, c = lax.axis_index("x"), lax.axis_index("y"), lax.axis_index("c")
    chips = [(1 - x, y), (x, 1 - y), (1 - x, 1 - y)]
    return x, y, c, chips, [2 * px + py for px, py in chips]


def _remote(src, dst, send_sem, recv_sem, device):
    return pltpu.make_async_remote_copy(src_ref=src, dst_ref=dst, send_sem=send_sem, recv_sem=recv_sem,
                                        device_id=device, device_id_type=_MESH)


def _rows2d(t):
    return t.reshape(-1, t.shape[-1])


def _row_tile(rows):
    tm = _ROW_TM
    while tm > 8 and rows % tm:
        tm //= 2
    return tm


def _placed_map(name, fn, where, ins, out_rows, cols, out_dtype, out_row, steps, tm, into=None):
    n_in = len(ins)

    def kern(where_ref, *refs):
        del where_ref
        o_ref = refs[n_in + (1 if into is not None else 0)]
        o_ref[...] = fn(*[r[...] for r in refs[:n_in]]).astype(o_ref.dtype)

    in_specs, operands = [], [where]
    for arr, rowfn in ins:
        operands.append(arr)
        if arr.ndim == 3:
            in_specs.append(pl.BlockSpec((arr.shape[0], tm, cols), lambda i, wh, f=rowfn: (0, f(i, wh), 0)))
        else:
            in_specs.append(pl.BlockSpec((tm, cols), lambda i, wh, f=rowfn: (f(i, wh), 0)))
    aliases = {}
    if into is not None:
        operands.append(into)
        in_specs.append(_ANY)
        aliases = {n_in + 1: 0}
    spec = pltpu.PrefetchScalarGridSpec(
        num_scalar_prefetch=1, grid=(steps,), in_specs=in_specs,
        out_specs=pl.BlockSpec((tm, cols), lambda i, wh: (out_row(i, wh), 0)))
    return pl.pallas_call(
        kern, name=name, grid_spec=spec, out_shape=jax.ShapeDtypeStruct((out_rows, cols), out_dtype),
        input_output_aliases=aliases, compiler_params=_params(("arbitrary",)),
    )(*operands)


def _cast_slab(name, t, where):
    n_l, r, cols = t.shape
    tm = _row_tile(r)
    per = r // tm
    buf = _placed_map("cast_" + name, lambda val: val, where, [(_rows2d(t), lambda i, wh: i)], n_l * 4 * r, cols, _BF,
                      lambda i, wh: ((i // per) * 4 + wh[1]) * per + i % per, n_l * per, tm)
    return buf.reshape(n_l, 4, r, cols)


_HBM = pl.BlockSpec(memory_space=pltpu.HBM)
_SEM = pl.BlockSpec(memory_space=pltpu.SEMAPHORE)
_EFFECT = pltpu.SideEffectType.DATAFLOW_SIDE_EFFECTING


def _ici_copies(refs, send, recv, arrivals):
    x, y, c, chips, idx = _place()
    me = 2 * x + y
    res = []
    for t, ref in enumerate(refs):
        half = ref.shape[0] // 2
        mine = pl.ds(c * half, half)
        for k in range(3):
            sem = 3 * t + k
            block = ref.at[mine, idx[k] if arrivals else me]
            res.append(_remote(block, block, send.at[sem], recv.at[sem], (*chips[k], c)))
    return res


def _gather_start(name, bufs, after):
    n = len(bufs)

    def body(*refs):
        send, recv = refs[n + 1], refs[n + 2]
        for cp in _ici_copies(refs[:n], send, recv, False):
            cp.start()
        refs[2 * n + 3][...] = jnp.zeros((8, 128), _F32)

    res = pl.pallas_call(
        body, name=name,
        out_shape=(pltpu.SemaphoreType.DMA((3 * n,)), pltpu.SemaphoreType.DMA((3 * n,)),
                   *[pltpu.HBM(b.shape, b.dtype) for b in bufs], jax.ShapeDtypeStruct((8, 128), _F32)),
        in_specs=[*[_HBM] * n, _ANY], out_specs=(_SEM, _SEM, *[_HBM] * n, pl.BlockSpec(memory_space=pltpu.VMEM)),
        input_output_aliases={t: 2 + t for t in range(n)},
        compiler_params=pltpu.CompilerParams(has_side_effects=_EFFECT),
    )(*[pltpu.with_memory_space_constraint(b, pltpu.HBM) for b in bufs], after)
    return res[0], res[1], list(res[2:2 + n]), res[2 + n]


def _gather_wait(name, send_sem, recv_sem, bufs, after):
    n = len(bufs)

    def body(*refs):
        send, recv = refs[n], refs[n + 1]
        for cp in _ici_copies(refs[:n], send, recv, False):
            cp.wait_send()
        for cp in _ici_copies(refs[:n], send, recv, True):
            cp.wait_recv()

    res = pl.pallas_call(
        body, name=name, out_shape=tuple(pltpu.HBM(b.shape, b.dtype) for b in bufs),
        in_specs=[*[_HBM] * n, _SEM, _SEM, _ANY], out_specs=tuple([_HBM] * n),
        input_output_aliases={t: t for t in range(n)},
        compiler_params=pltpu.CompilerParams(has_side_effects=_EFFECT),
    )(*bufs, send_sem, recv_sem, after)
    return list(res)


def _gather_forward(name, bufs):
    n = len(bufs)

    def body(*refs):
        outs, send, recv = refs[n:2 * n], refs[2 * n], refs[2 * n + 1]
        x, y, c, _, idx = _place()
        copies, arrive = [], []
        for t, ref in enumerate(outs):
            half = ref.shape[0] // 2
            mine, other = pl.ds(c * half, half), pl.ds((1 - c) * half, half)
            for k in range(3):
                sem = 3 * t + k
                landed, dst = ref.at[mine, idx[k]], ref.at[other, idx[k]]
                copies.append(_remote(landed, landed, send.at[sem], recv.at[sem], (x, y, 1 - c)))
                arrive.append(_remote(dst, dst, send.at[sem], recv.at[sem], (x, y, 1 - c)))
        for cp in copies:
            cp.start()
        for cp in arrive:
            cp.wait_recv()
        for cp in copies:
            cp.wait_send()

    res = pl.pallas_call(
        body, name=name, in_specs=[_ANY] * n, out_specs=[_ANY] * n,
        out_shape=[jax.ShapeDtypeStruct(b.shape, b.dtype) for b in bufs],
        scratch_shapes=[pltpu.SemaphoreType.DMA((3 * n,)), pltpu.SemaphoreType.DMA((3 * n,))],
        input_output_aliases={t: t for t in range(n)},
    )(*bufs)
    return list(res)


def _rs_sibling(name, gs):
    n = len(gs)

    def body(*refs):
        send, recv = refs[2 * n], refs[2 * n + 1]
        x, y, c, _, _ = _place()
        copies = []
        for t in range(n):
            rh = refs[t].shape[1] // 2
            cp = _remote(refs[t].at[:, pl.ds((1 - c) * rh, rh)], refs[n + t], send.at[t], recv.at[t], (x, y, 1 - c))
            cp.start()
            copies.append(cp)
        for cp in copies:
            cp.wait()

    return pl.pallas_call(
        body, name=name, in_specs=[_ANY] * n, out_specs=[_ANY] * n,
        out_shape=[jax.ShapeDtypeStruct((4, g.shape[1] // 2, g.shape[2]), g.dtype) for g in gs],
        scratch_shapes=[pltpu.SemaphoreType.DMA((n,)), pltpu.SemaphoreType.DMA((n,))],
    )(*gs)


def _chip_copies(sums, lands, send, recv):
    x, y, c, chips, idx = _place()
    return [_remote(sums[t].at[idx[k]], lands[t].at[k], send.at[3 * t + k], recv.at[3 * t + k], (*chips[k], c))
            for t in range(len(sums)) for k in range(3)]


def _rs_chips_start(name, sums, after):
    n = len(sums)
    lands = [lax.empty((3,) + s.shape[1:], s.dtype) for s in sums]

    def body(*refs):
        send, recv = refs[2 * n + 1], refs[2 * n + 2]
        for cp in _chip_copies(refs[:n], refs[n:2 * n], send, recv):
            cp.start()
        refs[4 * n + 3][...] = jnp.zeros((8, 128), _F32)

    both = [*sums, *lands]
    res = pl.pallas_call(
        body, name=name,
        out_shape=(pltpu.SemaphoreType.DMA((3 * n,)), pltpu.SemaphoreType.DMA((3 * n,)),
                   *[pltpu.HBM(b.shape, b.dtype) for b in both], jax.ShapeDtypeStruct((8, 128), _F32)),
        in_specs=[*[_HBM] * (2 * n), _ANY],
        out_specs=(_SEM, _SEM, *[_HBM] * (2 * n), pl.BlockSpec(memory_space=pltpu.VMEM)),
        input_output_aliases={t: 2 + t for t in range(2 * n)},
        compiler_params=pltpu.CompilerParams(has_side_effects=_EFFECT),
    )(*[pltpu.with_memory_space_constraint(b, pltpu.HBM) for b in both], after)
    return res[0], res[1], list(res[2:2 + n]), list(res[2 + n:2 + 2 * n]), res[2 + 2 * n]


def _rs_chips_wait(name, send_sem, recv_sem, sums, lands, after):
    n = len(sums)

    def body(*refs):
        send, recv = refs[2 * n], refs[2 * n + 1]
        copies = _chip_copies(refs[:n], refs[n:2 * n], send, recv)
        for cp in copies:
            cp.wait_send()
        for cp in copies:
            cp.wait_recv()

    both = [*sums, *lands]
    res = pl.pallas_call(
        body, name=name, out_shape=tuple(pltpu.HBM(b.shape, b.dtype) for b in both),
        in_specs=[*[_HBM] * (2 * n), _SEM, _SEM, _ANY], out_specs=tuple([_HBM] * (2 * n)),
        input_output_aliases={t: t for t in range(2 * n)},
        compiler_params=pltpu.CompilerParams(has_side_effects=_EFFECT),
    )(*both, send_sem, recv_sem, after)
    return list(res[:n]), list(res[n:])


def _rs_join(name, totals, layers):
    n = len(totals)

    def body(*refs):
        outs, send, recv = refs[n:2 * n], refs[2 * n], refs[2 * n + 1]
        x, y, c, _, _ = _place()
        copies, arrive = [], []
        for t in range(n):
            rh = outs[t].shape[1] // 2
            mine = outs[t].at[layers[t], pl.ds(c * rh, rh)]
            landing = outs[t].at[layers[t], pl.ds((1 - c) * rh, rh)]
            copies.append(_remote(mine, mine, send.at[t], recv.at[t], (x, y, 1 - c)))
            arrive.append(_remote(landing, landing, send.at[t], recv.at[t], (x, y, 1 - c)))
        for cp in copies:
            cp.start()
        for cp in arrive:
            cp.wait_recv()
        for cp in copies:
            cp.wait_send()

    return pl.pallas_call(
        body, name=name, in_specs=[_ANY] * n, out_specs=[_ANY] * n,
        out_shape=[jax.ShapeDtypeStruct(b.shape, b.dtype) for b in totals],
        scratch_shapes=[pltpu.SemaphoreType.DMA((n,)), pltpu.SemaphoreType.DMA((n,))],
        input_output_aliases={t: t for t in range(n)},
    )(*totals)


def _pair_sum(g, got, where):
    _, r, cols = g.shape
    rh = r // 2
    tm = _row_tile(rh)
    per, perh = r // tm, rh // tm
    out = _placed_map("rs_add_pair", lambda u, w: u.astype(_F32) + w.astype(_F32), where,
                      [(_rows2d(g), lambda i, wh: (i // perh) * per + wh[0] * perh + i % perh),
                       (_rows2d(got), lambda i, wh: i)],
                      4 * rh, cols, _BF, lambda i, wh: i, 4 * perh, tm)
    return out.reshape(4, rh, cols)


def _sum_of_four(pair, land, where, total, layer, n_layers):
    _, rh, cols = pair.shape
    r = 2 * rh
    tm = _row_tile(rh)
    per, perh = r // tm, rh // tm

    def add_four(own, others):
        return ((own.astype(_F32) + others[0].astype(_F32)) + others[1].astype(_F32)) + others[2].astype(_F32)

    out = _placed_map("rs_add_four", add_four, where,
                      [(_rows2d(pair), lambda i, wh: wh[1] * perh + i), (land, lambda i, wh: i)],
                      n_layers * r, cols, _F32, lambda i, wh: layer * per + wh[0] * perh + i, perh, tm,
                      into=None if total is None else _rows2d(total))
    return out.reshape(n_layers, r, cols)


def _all_reduce_small(pack):
    rows, cols = pack.shape

    def body(p_ref, o_ref, land, send, recv):
        x, y, c = lax.axis_index("x"), lax.axis_index("y"), lax.axis_index("c")
        me = 4 * x + 2 * y + c
        land[me] = p_ref[...]
        copies = []
        for k in range(1, 8):
            bx, by, bc = (k >> 2) & 1, (k >> 1) & 1, k & 1
            peer = (1 - x if bx else x, 1 - y if by else y, 1 - c if bc else c)
            cp = _remote(p_ref, land.at[me], send.at[k - 1], recv.at[k - 1], peer)
            cp.start()
            copies.append((cp, 4 * peer[0] + 2 * peer[1] + peer[2]))
        for k, (cp, pid) in enumerate(copies):
            _remote(p_ref, land.at[pid], send.at[k], recv.at[k], (x, y, c)).wait_recv()
        for cp, _ in copies:
            cp.wait_send()
        acc = land[0]
        for d in range(1, 8):
            acc = acc + land[d]
        o_ref[...] = acc

    vm = pl.BlockSpec(memory_space=pltpu.VMEM)
    return pl.pallas_call(
        body, name="all_reduce_small", in_specs=[vm], out_specs=vm,
        out_shape=jax.ShapeDtypeStruct((rows, cols), _F32),
        scratch_shapes=[pltpu.VMEM((8, rows, cols), _F32), pltpu.SemaphoreType.DMA((7,)), pltpu.SemaphoreType.DMA((7,))],
    )(pack)


def _adamw_math(w, g, m, v):
    m2 = _ADAM_B1 * m + (1.0 - _ADAM_B1) * g
    v2 = _ADAM_B2 * v + (1.0 - _ADAM_B2) * (g * g)
    m_hat = m2 / (1.0 - _ADAM_B1 ** _ADAM_STEP)
    v_hat = v2 / (1.0 - _ADAM_B2 ** _ADAM_STEP)
    delta = -_ADAM_LR * (m_hat / (jnp.sqrt(v_hat) + _ADAM_EPS) + _ADAM_WD * w)
    return delta, m2, v2


def _adamw(w, g, m, v):
    shape = w.shape
    w2, g2, m2, v2 = (_rows2d(t) for t in (w, g, m, v))
    rows, cols = w2.shape
    outs = _rowmap("adamw", _adamw_math, [(t, cols, 0) for t in (w2, g2, m2, v2)], [], [(cols, cols, 0, _F32)] * 3,
                   rows=rows, tm=_row_tile(rows))
    return tuple(o.reshape(shape) for o in outs)


def kernel(x, positions, ffn1_norm, ffn1_w_in, ffn1_w_out, mix_norm, ffn2_norm, ffn2_w_in, ffn2_w_out, hyb_w_in, hyb_dw_w, hyb_dw_b, hyb_ln_g, hyb_ln_b, hyb_w_out, gdn_w_in, gdn_conv_w, gdn_A_log, gdn_dt_bias, gdn_norm_g, gdn_w_out, final_norm, loss_target, m_ffn1_norm, m_ffn1_w_in, m_ffn1_w_out, m_mix_norm, m_ffn2_norm, m_ffn2_w_in, m_ffn2_w_out, m_hyb_w_in, m_hyb_dw_w, m_hyb_dw_b, m_hyb_ln_g, m_hyb_ln_b, m_hyb_w_out, m_gdn_w_in, m_gdn_conv_w, m_gdn_A_log, m_gdn_dt_bias, m_gdn_norm_g, m_gdn_w_out, m_final_norm, v_ffn1_norm, v_ffn1_w_in, v_ffn1_w_out, v_mix_norm, v_ffn2_norm, v_ffn2_w_in, v_ffn2_w_out, v_hyb_w_in, v_hyb_dw_w, v_hyb_dw_b, v_hyb_ln_g, v_hyb_ln_b, v_hyb_w_out, v_gdn_w_in, v_gdn_conv_w, v_gdn_A_log, v_gdn_dt_bias, v_gdn_norm_g, v_gdn_w_out, v_final_norm):
    w = dict(ffn1_norm=ffn1_norm, ffn1_w_in=ffn1_w_in, ffn1_w_out=ffn1_w_out, mix_norm=mix_norm, ffn2_norm=ffn2_norm, ffn2_w_in=ffn2_w_in, ffn2_w_out=ffn2_w_out, hyb_w_in=hyb_w_in, hyb_dw_w=hyb_dw_w, hyb_dw_b=hyb_dw_b, hyb_ln_g=hyb_ln_g, hyb_ln_b=hyb_ln_b, hyb_w_out=hyb_w_out, gdn_w_in=gdn_w_in, gdn_conv_w=gdn_conv_w, gdn_A_log=gdn_A_log, gdn_dt_bias=gdn_dt_bias, gdn_norm_g=gdn_norm_g, gdn_w_out=gdn_w_out, final_norm=final_norm)
    m = dict(ffn1_norm=m_ffn1_norm, ffn1_w_in=m_ffn1_w_in, ffn1_w_out=m_ffn1_w_out, mix_norm=m_mix_norm, ffn2_norm=m_ffn2_norm, ffn2_w_in=m_ffn2_w_in, ffn2_w_out=m_ffn2_w_out, hyb_w_in=m_hyb_w_in, hyb_dw_w=m_hyb_dw_w, hyb_dw_b=m_hyb_dw_b, hyb_ln_g=m_hyb_ln_g, hyb_ln_b=m_hyb_ln_b, hyb_w_out=m_hyb_w_out, gdn_w_in=m_gdn_w_in, gdn_conv_w=m_gdn_conv_w, gdn_A_log=m_gdn_A_log, gdn_dt_bias=m_gdn_dt_bias, gdn_norm_g=m_gdn_norm_g, gdn_w_out=m_gdn_w_out, final_norm=m_final_norm)
    v = dict(ffn1_norm=v_ffn1_norm, ffn1_w_in=v_ffn1_w_in, ffn1_w_out=v_ffn1_w_out, mix_norm=v_mix_norm, ffn2_norm=v_ffn2_norm, ffn2_w_in=v_ffn2_w_in, ffn2_w_out=v_ffn2_w_out, hyb_w_in=v_hyb_w_in, hyb_dw_w=v_hyb_dw_w, hyb_dw_b=v_hyb_dw_b, hyb_ln_g=v_hyb_ln_g, hyb_ln_b=v_hyb_ln_b, hyb_w_out=v_hyb_w_out, gdn_w_in=v_gdn_w_in, gdn_conv_w=v_gdn_conv_w, gdn_A_log=v_gdn_A_log, gdn_dt_bias=v_gdn_dt_bias, gdn_norm_g=v_gdn_norm_g, gdn_w_out=v_gdn_w_out, final_norm=v_final_norm)
    return _train_step(x, positions, loss_target, w, m, v)


_WEIGHT_ORDER = ("ffn1_norm", "ffn1_w_in", "ffn1_w_out", "mix_norm", "ffn2_norm", "ffn2_w_in", "ffn2_w_out",
                 "hyb_w_in", "hyb_dw_w", "hyb_dw_b", "hyb_ln_g", "hyb_ln_b", "hyb_w_out", "gdn_w_in", "gdn_conv_w",
                 "gdn_A_log", "gdn_dt_bias", "gdn_norm_g", "gdn_w_out", "final_norm")
_BIG = ("ffn1_w_in", "ffn1_w_out", "ffn2_w_in", "ffn2_w_out", "hyb_w_in", "hyb_w_out", "gdn_w_in", "gdn_w_out")
_PACK_COLS = 1024


def _lane16(vec):
    return jnp.zeros((1, 128), _F32).at[0, _GDN_HEADS:2 * _GDN_HEADS].set(vec)


def _pack_rows(parts):
    rows, where, at = [], [], 0
    for p in parts:
        flat = p.reshape(-1).astype(_F32)
        n = -(-flat.shape[0] // (8 * _PACK_COLS)) * 8
        rows.append(jnp.pad(flat, (0, n * _PACK_COLS - flat.shape[0])).reshape(n, _PACK_COLS))
        where.append((at, n, flat.shape[0], p.shape))
        at += n
    return jnp.concatenate(rows, axis=0), where


def _unpack_rows(pack, where):
    return [pack[at:at + n].reshape(-1)[:size].reshape(shape) for at, n, size, shape in where]


def _train_step(x, positions, loss_target, w, m, v):
    n_layers = w["ffn1_norm"].shape[0]
    n_hyb, n_gdn = w["hyb_w_in"].shape[0], w["gdn_w_in"].shape[0]
    d_model = x.shape[-1]
    xh, target = x[0], loss_target[0]
    chip = 2 * lax.axis_index("x") + lax.axis_index("y")
    where = jnp.stack([lax.axis_index("c"), chip]).astype(jnp.int32)

    groups = (("ffn1_w_in", "ffn1_w_out"), ("hyb_w_in", "hyb_w_out"), ("ffn2_w_in", "ffn2_w_out"),
              ("gdn_w_in", "gdn_w_out"))
    pending, token = [], where
    for gi, names in enumerate(groups):
        send, recv, thru, token = _gather_start(f"gather_start_{gi}", [_cast_slab(n, w[n], where) for n in names], token)
        pending.append((send, recv, thru))
    gdn_cols = w["gdn_w_in"].shape[-1]
    n_qkvz = 4 * _GDN_HEADS * _GDN_DIM
    n_ba = 4 * gdn_cols - n_qkvz
    wv = {}

    def arrive(gi, after):
        send, recv, thru = pending[gi]
        a, b = _gather_forward(f"gather_pass_{gi}", _gather_wait(f"gather_wait_{gi}", send, recv, thru, after))
        kind = groups[gi][0]
        if kind == "gdn_w_in":
            gdn_in = jnp.transpose(a, (0, 2, 1, 3)).reshape(n_gdn, d_model, 4 * gdn_cols)
            wv["gdn_qkvz"] = gdn_in[:, :, :n_qkvz]
            wv["gdn_ba"] = jnp.pad(gdn_in[:, :, n_qkvz:], ((0, 0), (0, 0), (0, 128 - n_ba)))
        else:
            wv[kind] = a
        wv[groups[gi][1]] = b.reshape(b.shape[0], -1, d_model)

    def own_columns(shard):
        cols = shard.shape[-1]
        zeros = jnp.zeros(shard.shape[:-1] + (4 * cols,), _F32)
        return lax.dynamic_update_slice_in_dim(zeros, shard, chip * cols, axis=2)

    small_pack, small_where = _pack_rows([own_columns(w["hyb_dw_w"]), own_columns(w["gdn_conv_w"])])
    dw_w, conv_w = _unpack_rows(0.5 * _all_reduce_small(small_pack), small_where)
    dw_w = jnp.pad(dw_w, ((0, 0), (0, _HALO - _CONV_WIDTH), (0, 0)))
    tabs = _rotary_tables(positions)

    def mixer_args(layer):
        i = layer // 2
        if layer % 2 == 0:
            return (wv["hyb_w_in"], wv["hyb_w_out"], dw_w[i], w["hyb_dw_b"][i][None], w["hyb_ln_g"][i][None],
                    w["hyb_ln_b"][i][None], tabs, i)
        return (wv["gdn_qkvz"], wv["gdn_ba"], wv["gdn_w_out"], conv_w[i], _lane16(w["gdn_A_log"][i]),
                _lane16(w["gdn_dt_bias"][i]), w["gdn_norm_g"][i][None], i)

    h = xh
    saved = []
    arrive(0, token)
    for layer in range(n_layers):
        h, s1 = _ffn_fwd(h, w["ffn1_norm"][layer][None], wv["ffn1_w_in"], wv["ffn1_w_out"], layer)
        gamma = w["mix_norm"][layer][None]
        if layer % 2 == 0:
            if layer == 0:
                arrive(1, h)
            h, sm = _hyb_fwd(h, gamma, *mixer_args(layer))
        else:
            if layer == 1:
                arrive(3, h)
            h, sm = _gdn_fwd(h, gamma, *mixer_args(layer))
        if layer == 0:
            arrive(2, h)
        h, s2 = _ffn_fwd(h, w["ffn2_norm"][layer][None], wv["ffn2_w_in"], wv["ffn2_w_out"], layer)
        saved.append((s1, sm, s2))
    dh, loss_row, d_final = _loss_head(h, w["final_norm"][None], target)

    d_norm = {k: [None] * n_layers for k in ("ffn1_norm", "mix_norm", "ffn2_norm")}
    d_hyb = [None] * n_hyb
    d_gdn = [None] * n_gdn
    grads = {name: None for name in _BIG}
    n_of = {name: w[name].shape[0] for name in _BIG}

    def rs_begin(layer, full, token):
        names = list(full)
        got = _rs_sibling(f"rs_sib_{layer}", [full[n] for n in names])
        sums = [_pair_sum(full[n], g, where) for n, g in zip(names, got)]
        send, recv, sums, lands, token = _rs_chips_start(f"rs_chips_start_{layer}", sums, token)
        return (layer, names, send, recv, sums, lands), token

    def rs_end(state, after):
        layer, names, send, recv, sums, lands = state
        sums, lands = _rs_chips_wait(f"rs_chips_wait_{layer}", send, recv, sums, lands, after)
        index = [layer if n.startswith("ffn") else layer // 2 for n in names]
        for n, li, pair, land in zip(names, index, sums, lands):
            grads[n] = _sum_of_four(pair, land, where, grads[n], li, n_of[n])
        joined = _rs_join(f"rs_join_{layer}", [grads[n] for n in names], index)
        grads.update(zip(names, joined))

    in_flight = None
    for layer in reversed(range(n_layers)):
        s1, sm, s2 = saved[layer]
        full = {}
        dh, d_norm["ffn2_norm"][layer], full["ffn2_w_in"], full["ffn2_w_out"] = _ffn_bwd(
            dh, s2, w["ffn2_norm"][layer][None], wv["ffn2_w_in"], wv["ffn2_w_out"], layer)
        gamma = w["mix_norm"][layer][None]
        if layer % 2 == 0:
            dh, d_norm["mix_norm"][layer], d_hyb[layer // 2], full["hyb_w_in"], full["hyb_w_out"] = _hyb_bwd(
                dh, sm, gamma, *mixer_args(layer))
        else:
            dh, d_norm["mix_norm"][layer], d_gdn[layer // 2], full["gdn_w_in"], full["gdn_w_out"] = _gdn_bwd(
                dh, sm, gamma, *mixer_args(layer), n_ba)
        dh, d_norm["ffn1_norm"][layer], full["ffn1_w_in"], full["ffn1_w_out"] = _ffn_bwd(
            dh, s1, w["ffn1_norm"][layer][None], wv["ffn1_w_in"], wv["ffn1_w_out"], layer)
        if in_flight is not None:
            rs_end(in_flight, dh)
        in_flight, token = rs_begin(layer, full, token)
        dh, token = lax.optimization_barrier((dh, token))
    rs_end(in_flight, token)
    grad_x = dh[None]

    stack = lambda rows: jnp.concatenate(rows, axis=0)
    misc = jnp.concatenate([stack([g[1] for g in d_gdn])[:, _GDN_HEADS:2 * _GDN_HEADS].reshape(-1),
                            stack([g[2] for g in d_gdn])[:, _GDN_HEADS:2 * _GDN_HEADS].reshape(-1),
                            stack([g[3] for g in d_gdn]).reshape(-1), loss_row[0, :1]])
    n_a = n_gdn * _GDN_HEADS
    parts = [stack(d_norm["ffn1_norm"]), stack(d_norm["mix_norm"]), stack(d_norm["ffn2_norm"]), d_final,
             stack([g[1] for g in d_hyb]), stack([g[2] for g in d_hyb]), stack([g[3] for g in d_hyb]), misc,
             jnp.stack([g[0][:_CONV_WIDTH] for g in d_hyb]), jnp.stack([g[0] for g in d_gdn])]
    pack, where = _pack_rows(parts)
    red = _unpack_rows(_all_reduce_small(pack), where)
    misc = red[7]
    cw = w["hyb_dw_w"].shape[-1]
    gw = w["gdn_conv_w"].shape[-1]
    grads.update(
        ffn1_norm=red[0], mix_norm=red[1], ffn2_norm=red[2], final_norm=red[3].reshape(-1),
        hyb_dw_b=red[4], hyb_ln_g=red[5], hyb_ln_b=red[6],
        gdn_A_log=misc[:n_a].reshape(n_gdn, _GDN_HEADS), gdn_dt_bias=misc[n_a:2 * n_a].reshape(n_gdn, _GDN_HEADS),
        gdn_norm_g=misc[2 * n_a:2 * n_a + n_gdn * _GDN_DIM].reshape(n_gdn, _GDN_DIM),
        hyb_dw_w=lax.dynamic_slice_in_dim(red[8], chip * cw, cw, axis=2),
        gdn_conv_w=lax.dynamic_slice_in_dim(red[9], chip * gw, gw, axis=2))
    loss = misc[2 * n_a + n_gdn * _GDN_DIM]

    delta, new_m, new_v = {}, {}, {}
    for name in _BIG:
        delta[name], new_m[name], new_v[name] = _adamw(w[name], grads[name], m[name], v[name])
    small = [n for n in _WEIGHT_ORDER if n not in _BIG]
    packs = [_pack_rows([src[n] for n in small]) for src in (w, grads, m, v)]
    outs = _adamw(*[p[0] for p in packs])
    for res, o in zip((delta, new_m, new_v), outs):
        res.update(dict(zip(small, _unpack_rows(o, packs[0][1]))))
    return (loss, grad_x, *[grads[n] for n in _WEIGHT_ORDER], *[delta[n] for n in _WEIGHT_ORDER],
            *[new_m[n] for n in _WEIGHT_ORDER], *[new_v[n] for n in _WEIGHT_ORDER])
```

```python
import functools

import jax
import jax.numpy as jnp
from jax import lax
from jax.experimental import pallas as pl
from jax.experimental.pallas import tpu as pltpu

_BF = jnp.bfloat16
_F32 = jnp.float32
_NORM_EPS = 1e-6
_VMEM_LIMIT_BYTES = 56 * 1024 * 1024

_A_HEADS = 8
_A_HEAD_DIM = 64
_WIN_BLOCK = 128
_DILATIONS = (1, 4, 16)
_CONV_WIDTH = 31
_GDN_HEADS = 8
_GDN_DIM = 128
_GDN_CHUNK = 64
_GDN_SHORT_CONV = 4

_ADAM_LR = 0.001
_ADAM_B1 = 0.9
_ADAM_B2 = 0.999
_ADAM_EPS = 1e-08
_ADAM_WD = 0.01
_ADAM_STEP = 10


def _params(sem):
    return pltpu.CompilerParams(dimension_semantics=sem, vmem_limit_bytes=_VMEM_LIMIT_BYTES)


def _dot(a, b, dn=(((1,), (0,)), ((), ()))):
    return lax.dot_general(a.astype(_BF), b.astype(_BF), dn, preferred_element_type=_F32)


def _dot_nt(a, b):
    return _dot(a, b, (((1,), (1,)), ((), ())))


def _dot_tn(a, b):
    return _dot(a, b, (((0,), (0,)), ((), ())))


_NN = (((1,), (0,)), ((), ()))
_NT = (((1,), (1,)), ((), ()))
_TN = (((0,), (0,)), ((), ()))
_ANY = pl.BlockSpec(memory_space=pl.ANY)


def _mm(name, a, b, *, grid, a_spec, b_spec, dn, out_shape, out_spec, red=None, acc_shape=None,
        pre_a=None, pre_b=None, epi=None, extras=(), extra_specs=()):
    n_ex = len(extras)
    n_in = 2 + n_ex
    nk = grid[red] if red is not None else 1

    def body(*refs):
        a_ref, b_ref = refs[0], refs[1]
        ex = refs[2:2 + n_ex]
        o_ref = refs[n_in]
        av, bv = a_ref[...], b_ref[...]
        if pre_a is not None:
            av = pre_a(av)
        if pre_b is not None:
            bv = pre_b(bv)
        part = lax.dot_general(av.astype(_BF), bv.astype(_BF), dn, preferred_element_type=_F32)

        def finish(acc):
            res = epi(acc, *[e[...] for e in ex]) if epi is not None else acc
            if isinstance(res, tuple):
                for idx, val in enumerate(res):
                    o_ref[idx] = val.astype(o_ref.dtype)
            else:
                o_ref[...] = res.astype(o_ref.dtype)

        if red is None:
            finish(part)
        else:
            acc_ref = refs[n_in + 1]
            k = pl.program_id(red)

            @pl.when(k == 0)
            def _():
                acc_ref[...] = part

            @pl.when(k > 0)
            def _():
                acc_ref[...] += part

            @pl.when(k == nk - 1)
            def _():
                finish(acc_ref[...])

    return pl.pallas_call(
        body, name=name, grid=grid, in_specs=[a_spec, b_spec, *extra_specs], out_specs=out_spec, out_shape=out_shape,
        scratch_shapes=[pltpu.VMEM(acc_shape, _F32)] if red is not None else [],
        compiler_params=_params(("arbitrary",) * len(grid)),
    )(a, b, *extras)


def _rowmap(name, body, tiled, params, tile_outs, param_outs=(), *, rows, tm, ncol=1, col_outer=False,
            into=None):
    nt, npar, nto, npo = len(tiled), len(params), len(tile_outs), len(param_outs)
    assert rows % tm == 0, (name, rows, tm)
    nrow = rows // tm
    grid = (ncol, nrow) if col_outer else (nrow, ncol)

    def ij(g):
        return (g[1], g[0]) if col_outer else (g[0], g[1])

    def cidx(col, j):
        return col(j) if callable(col) else col

    in_specs, operands = [], []
    for arr, width, col in tiled:
        operands.append(arr)
        if arr.ndim == 3:
            in_specs.append(pl.BlockSpec((arr.shape[0], tm, width),
                                         lambda *g, col=col: (0, ij(g)[0], cidx(col, ij(g)[1]))))
        else:
            in_specs.append(pl.BlockSpec((tm, width), lambda *g, col=col: (ij(g)[0], cidx(col, ij(g)[1]))))
    for p in params:
        if isinstance(p, tuple):
            arr, width, col = p
            operands.append(arr)
            in_specs.append(pl.BlockSpec((arr.shape[0], width), lambda *g, col=col: (0, cidx(col, ij(g)[1]))))
        else:
            operands.append(p)
            in_specs.append(pl.BlockSpec(p.shape, lambda *g, nd=p.ndim: (0,) * nd))
    out_shape, out_specs = [], []
    for total, width, col, dt in tile_outs:
        out_shape.append(jax.ShapeDtypeStruct((rows, total), dt))
        out_specs.append(pl.BlockSpec((tm, width), lambda *g, col=col: (ij(g)[0], cidx(col, ij(g)[1]))))
    for shp in param_outs:
        out_shape.append(jax.ShapeDtypeStruct(shp, _F32))
        out_specs.append(pl.BlockSpec(shp, lambda *g, nd=len(shp): (0,) * nd))
    n_in = nt + npar
    aliases = {}
    if into is not None:
        operands.append(into[0])
        in_specs.append(_ANY)
        aliases = {n_in: into[1]}
        n_in += 1

    def kern(*refs):
        vals = [r[...] for r in refs[:nt + npar]]
        touts = refs[n_in:n_in + nto]
        pouts = refs[n_in + nto:]
        res = body(*vals)
        if not isinstance(res, (tuple, list)):
            res = (res,)
        for r, val in zip(touts, res[:nto]):
            r[...] = val.astype(r.dtype)
        if npo:
            @pl.when((pl.program_id(0) == 0) & (pl.program_id(1) == 0))
            def _():
                for r in pouts:
                    r[...] = jnp.zeros_like(r)

            for r, val in zip(pouts, res[nto:]):
                r[...] += val

    return pl.pallas_call(
        kern, name=name, grid=grid, in_specs=in_specs, out_specs=out_specs, out_shape=out_shape,
        input_output_aliases=aliases, compiler_params=_params(("arbitrary", "arbitrary")),
    )(*operands)


def _vjp_body(fn, n_in, diff, n_ct):
    def body(*vals):
        ins, cts = vals[:n_in], vals[n_in:n_in + n_ct]
        outs, pull = jax.vjp(fn, *ins)
        single = not isinstance(outs, (tuple, list))
        if single:
            grads = pull(cts[0].astype(outs.dtype))
        else:
            grads = pull(tuple(c.astype(o.dtype) for c, o in zip(cts, outs)))
        return tuple(grads[i] for i in diff)
    return body


_TM = 512
_TM_WIDE = 1024


def _rms(h, gamma):
    hf = h.astype(_F32)
    return hf * lax.rsqrt(jnp.mean(hf * hf, axis=-1, keepdims=True) + _NORM_EPS) * gamma


def _rmsnorm_fwd(h, gamma):
    s, d = h.shape
    return _rowmap("rmsnorm_fwd", lambda hv, gv: _rms(hv, gv).astype(_BF), [(h, d, 0)], [gamma],
                   [(d, d, 0, _BF)], rows=s, tm=_TM)[0]


def _rmsnorm_bwd(h, gamma, dxn, dh_out):
    s, d = h.shape

    def body(hv, dxv, dhv, gv):
        _, pull = jax.vjp(_rms, hv, gv)
        dh, dg = pull(dxv.astype(_F32))
        return dhv + dh, dg

    return _rowmap("rmsnorm_bwd", body, [(h, d, 0), (dxn, d, 0), (dh_out, d, 0)], [gamma],
                   [(d, d, 0, _F32)], [(1, d)], rows=s, tm=_TM)


def _silu(x):
    return x * jax.nn.sigmoid(x)


def _ffn_up(xn, w_in, layer):
    s, d = xn.shape
    ns = w_in.shape[-1]

    def body(x_ref, wg_ref, wu_ref, g_ref, u_ref, a_ref):
        xv = x_ref[...]
        g = jnp.dot(xv, wg_ref[...], preferred_element_type=_F32)
        u = jnp.dot(xv, wu_ref[...], preferred_element_type=_F32)
        g_ref[...] = g.astype(_BF)
        u_ref[...] = u.astype(_BF)
        a_ref[...] = (_silu(g) * u).astype(_BF)

    out = pl.BlockSpec((_TM, ns), lambda j, i: (i, j))
    sds = jax.ShapeDtypeStruct((s, 2 * ns), _BF)
    return pl.pallas_call(
        body, name="ffn_up", grid=(2, s // _TM),
        in_specs=[pl.BlockSpec((_TM, d), lambda j, i: (i, 0)),
                  pl.BlockSpec((None, None, d, ns), lambda j, i: (layer, j, 0, 0)),
                  pl.BlockSpec((None, None, d, ns), lambda j, i: (layer, j + 2, 0, 0))],
        out_specs=[out, out, out], out_shape=[sds, sds, sds],
        compiler_params=_params(("arbitrary", "arbitrary")),
    )(xn, w_in, w_in)


def _proj_residual(name, y, w, layer, h, scale):
    s, k = y.shape
    d = w.shape[-1]
    return _mm(name, y, w, grid=(s // _TM,), a_spec=pl.BlockSpec((_TM, k), lambda i: (i, 0)),
               b_spec=pl.BlockSpec((None, k, d), lambda i: (layer, 0, 0)), dn=_NN,
               out_shape=jax.ShapeDtypeStruct((s, d), _F32), out_spec=pl.BlockSpec((_TM, d), lambda i: (i, 0)),
               epi=lambda acc, hv: hv + scale * acc, extras=(h,),
               extra_specs=(pl.BlockSpec((_TM, d), lambda i: (i, 0)),))


def _ffn_dact(dh, w_out, layer, g, u):
    s, d = dh.shape
    f = g.shape[1]
    ns = f // 2

    def epi(da, gv, uv):
        gf, uf = gv.astype(_F32), uv.astype(_F32)
        sg = jax.nn.sigmoid(gf)
        return da * uf * (sg * (1.0 + gf * (1.0 - sg))), da * gf * sg

    gu = pl.BlockSpec((_TM, ns), lambda n, i: (i, n))
    return _mm("ffn_dact", dh, w_out, grid=(2, s // _TM), a_spec=pl.BlockSpec((_TM, d), lambda n, i: (i, 0)),
               b_spec=pl.BlockSpec((None, ns, d), lambda n, i: (layer, n, 0)), dn=_NT,
               out_shape=jax.ShapeDtypeStruct((2, s, f), _BF), out_spec=pl.BlockSpec((2, _TM, ns), lambda n, i: (0, i, n)),
               pre_a=lambda v: 0.5 * v, epi=epi, extras=(g, u), extra_specs=(gu, gu))


def _wgrad_rows(name, y, dh, scale):
    s, k = y.shape
    d = dh.shape[1]
    tk = k // 2 if (k // 2) % 128 == 0 else k
    out = _mm(name, y, dh, grid=(k // tk, s // _TM), a_spec=pl.BlockSpec((_TM, tk), lambda n, i: (i, n)),
              b_spec=pl.BlockSpec((_TM, d), lambda n, i: (i, 0)), dn=_TN, red=1, acc_shape=(tk, d),
              out_shape=jax.ShapeDtypeStruct((k, d), _BF), out_spec=pl.BlockSpec((tk, d), lambda n, i: (n, 0)),
              pre_b=(lambda v: scale * v) if scale != 1.0 else None)
    return out.reshape(4, k // 4, d)


def _ffn_dxn(dz, w_in, layer):
    _, s, f = dz.shape
    d, ns = w_in.shape[2], w_in.shape[3]
    tm = _TM_WIDE
    return _mm("ffn_dxn", dz, w_in, grid=(s // tm, 4),
               a_spec=pl.BlockSpec((None, tm, ns), lambda i, j: (j // 2, i, j % 2)),
               b_spec=pl.BlockSpec((None, None, d, ns), lambda i, j: (layer, j, 0, 0)), dn=_NT, red=1,
               acc_shape=(tm, d), out_shape=jax.ShapeDtypeStruct((s, d), _F32),
               out_spec=pl.BlockSpec((tm, d), lambda i, j: (i, 0)))


def _ffn_dwin(xn, dz):
    s, d = xn.shape
    ns = dz.shape[2] // 2
    tm = _TM_WIDE
    return _mm("ffn_dwin", xn, dz, grid=(4, s // tm), a_spec=pl.BlockSpec((tm, d), lambda j, i: (i, 0)),
               b_spec=pl.BlockSpec((None, tm, ns), lambda j, i: (j // 2, i, j % 2)), dn=_TN, red=1,
               acc_shape=(d, ns), out_shape=jax.ShapeDtypeStruct((4, d, ns), _BF),
               out_spec=pl.BlockSpec((None, d, ns), lambda j, i: (j, 0, 0)))


def _ffn_fwd(h, gamma, w_in, w_out, layer):
    xn = _rmsnorm_fwd(h, gamma)
    g, u, act = _ffn_up(xn, w_in, layer)
    h_new = _proj_residual("ffn_down", act, w_out, layer, h, 0.5)
    return h_new, (h, xn, g, u, act)


def _ffn_bwd(dh, saved, gamma, w_in, w_out, layer):
    h, xn, g, u, act = saved
    dz = _ffn_dact(dh, w_out, layer, g, u)
    d_w_out = _wgrad_rows("ffn_dwout", act, dh, 0.5)
    dxn = _ffn_dxn(dz, w_in, layer)
    d_w_in = _ffn_dwin(xn, dz)
    dh_in, dgamma = _rmsnorm_bwd(h, gamma, dxn, dh)
    return dh_in, dgamma, d_w_in, d_w_out


def _loss_head(h, gamma, target):
    s, d = h.shape

    def body(hv, tv, gv):
        y, pull = jax.vjp(_rms, hv, gv)
        err = y - tv
        loss = 0.5 * jnp.sum(jnp.mean(err * err, axis=-1, keepdims=True), axis=0, keepdims=True)
        dh, dg = pull(err * (1.0 / d))
        return dh, jnp.broadcast_to(loss, (1, 128)), dg

    return _rowmap("loss_head", body, [(h, d, 0), (target, d, 0)], [gamma], [(d, d, 0, _F32)],
                   [(1, 128), (1, d)], rows=s, tm=_TM)


_ROT = _A_HEAD_DIM // 4
_ROPE_THETA = 500000.0
_A_WIDTH = _A_HEADS * _A_HEAD_DIM
_NEG = -1e30


def _rotary_tables(positions):
    inv_freq = jnp.power(jnp.float32(_ROPE_THETA), -jnp.arange(0, _ROT, 2, dtype=_F32) / _ROT)
    ang = positions.reshape(-1, 1).astype(_F32) * inv_freq
    cos, sin = jnp.cos(ang), jnp.sin(ang)
    s = ang.shape[0]
    pad = jnp.zeros((s, _A_HEAD_DIM - _ROT), _F32)
    c_head = jnp.concatenate([cos, cos, pad + 1.0], axis=1)
    s_head = jnp.concatenate([-sin, sin, pad], axis=1)
    c2, s2 = jnp.tile(c_head, (1, 2)), jnp.tile(s_head, (1, 2))
    scale = _A_HEAD_DIM ** -0.5
    tab_c = jnp.stack([scale * c2, c2, jnp.ones_like(c2)])
    tab_s = jnp.stack([scale * s2, s2, jnp.zeros_like(s2)])
    return tab_c, tab_s


def _pair_swap(z):
    lane = lax.broadcasted_iota(jnp.int32, z.shape, 1) % _A_HEAD_DIM
    half = _ROT // 2
    up = jnp.where(lane < _ROT, pltpu.roll(z, half, axis=1), 0.0)
    return jnp.where(lane < half, pltpu.roll(z, 128 - half, axis=1), up)


def _rot_apply(x, c, s, transpose):
    outs = []
    for k in range(x.shape[1] // 128):
        xs = x[:, k * 128:(k + 1) * 128]
        outs.append(c * xs + (_pair_swap(s * xs) if transpose else s * _pair_swap(xs)))
    return jnp.concatenate(outs, axis=1)


def _qkv_prep(proj, tab_c, tab_s):
    s = proj.shape[0]
    w = _A_WIDTH

    def body(x_ref, c_ref, s_ref, o_ref):
        o_ref[...] = _rot_apply(x_ref[...], c_ref[...], s_ref[...], False).astype(_BF)

    tab = pl.BlockSpec((None, _TM, 128), lambda i, j: (j, i, 0))
    blk = pl.BlockSpec((_TM, w), lambda i, j: (i, j))
    return pl.pallas_call(
        body, name="qkv_prep", grid=(s // _TM, 3), in_specs=[blk, tab, tab], out_specs=blk,
        out_shape=jax.ShapeDtypeStruct((s, 3 * w), _BF), compiler_params=_params(("arbitrary", "arbitrary")),
    )(proj, tab_c, tab_s)


def _head_masks():
    lane = lax.broadcasted_iota(jnp.int32, (_WIN_BLOCK, 128), 1)
    return [lane // _A_HEAD_DIM == hh for hh in range(2)]


def _band_masks(first_block):
    qi = lax.broadcasted_iota(jnp.int32, (_WIN_BLOCK, _WIN_BLOCK), 0)
    kj = lax.broadcasted_iota(jnp.int32, (_WIN_BLOCK, _WIN_BLOCK), 1)
    return (kj >= qi) & jnp.logical_not(first_block), kj <= qi


def _attn_fwd(qkv, dil):
    s = qkv.shape[0]
    w = _A_WIDTH
    length = s // dil
    nb = length // _WIN_BLOCK
    view = qkv.reshape(length, dil * 3 * w)

    def body(q_ref, kp_ref, kc_ref, vp_ref, vc_ref, o_ref, l_ref):
        allow_p, allow_c = _band_masks(pl.program_id(1) == 0)
        hm = _head_masks()
        for sl in range(w // 128):
            cols = slice(sl * 128, (sl + 1) * 128)
            q, kp, kc, vp, vc = q_ref[:, cols], kp_ref[:, cols], kc_ref[:, cols], vp_ref[:, cols], vc_ref[:, cols]
            o_s = jnp.zeros((_WIN_BLOCK, 128), _F32)
            l_s = jnp.zeros((_WIN_BLOCK, 128), _F32)
            for hh in range(2):
                qh = jnp.where(hm[hh], q, jnp.zeros_like(q))
                sp = jnp.where(allow_p, _dot_nt(qh, kp), _NEG)
                sc = jnp.where(allow_c, _dot_nt(qh, kc), _NEG)
                m = jnp.maximum(jnp.max(sp, axis=1, keepdims=True), jnp.max(sc, axis=1, keepdims=True))
                pp, pc = jnp.exp(sp - m), jnp.exp(sc - m)
                den = jnp.sum(pp, axis=1, keepdims=True) + jnp.sum(pc, axis=1, keepdims=True)
                oh = (_dot(pp, vp) + _dot(pc, vc)) / den
                o_s = jnp.where(hm[hh], oh, o_s)
                l_s = jnp.where(hm[hh], m + jnp.log(den), l_s)
            o_ref[:, cols] = o_s
            l_ref[:, cols] = l_s

    def at(col, prev):
        if prev:
            return pl.BlockSpec((_WIN_BLOCK, w), lambda r, n: (jnp.maximum(n - 1, 0), 3 * r + col))
        return pl.BlockSpec((_WIN_BLOCK, w), lambda r, n: (n, 3 * r + col))

    out = pl.BlockSpec((_WIN_BLOCK, w), lambda r, n: (n, r))
    sds = jax.ShapeDtypeStruct((length, dil * w), _F32)
    o, lse = pl.pallas_call(
        body, name=f"attn_fwd_d{dil}", grid=(dil, nb),
        in_specs=[at(0, False), at(1, True), at(1, False), at(2, True), at(2, False)],
        out_specs=[out, out], out_shape=[sds, sds], compiler_params=_params(("arbitrary", "arbitrary")),
    )(view, view, view, view, view)
    return o.reshape(s, w), lse.reshape(s, w)


def _attn_bwd(qkv, o, lse, do, dlse, dil):
    s = qkv.shape[0]
    w = _A_WIDTH
    length = s // dil
    nb = length // _WIN_BLOCK
    view = qkv.reshape(length, dil * 3 * w)
    o, lse, do, dlse = (t.reshape(length, dil * w) for t in (o, lse, do, dlse))

    def body(q_ref, kp_ref, kc_ref, vp_ref, vc_ref, o_ref, l_ref, do_ref, dl_ref, out_ref,
             car_q, car_k, car_v, cur_q, cur_kc, cur_vc, cur_kp, cur_vp):
        n = pl.program_id(1)

        @pl.when(n == 0)
        def _():
            car_q[...] = jnp.zeros_like(car_q)
            car_k[...] = jnp.zeros_like(car_k)
            car_v[...] = jnp.zeros_like(car_v)

        @pl.when(n == nb)
        def _():
            cur_kp[...] = jnp.zeros_like(cur_kp)
            cur_vp[...] = jnp.zeros_like(cur_vp)

        @pl.when(n < nb)
        def _():
            allow_p, allow_c = _band_masks(n == 0)
            hm = _head_masks()
            for sl in range(w // 128):
                cols = slice(sl * 128, (sl + 1) * 128)
                q, kp, kc, vp, vc = q_ref[:, cols], kp_ref[:, cols], kc_ref[:, cols], vp_ref[:, cols], vc_ref[:, cols]
                ov, lv, dov, dlv = o_ref[:, cols], l_ref[:, cols], do_ref[:, cols], dl_ref[:, cols]
                zero = jnp.zeros((_WIN_BLOCK, 128), _F32)
                dq_s, dkp_s, dkc_s, dvp_s, dvc_s = zero, zero, zero, zero, zero
                for hh in range(2):
                    qh = jnp.where(hm[hh], q, jnp.zeros_like(q))
                    doh = jnp.where(hm[hh], dov, 0.0).astype(_BF)
                    lse_h = jnp.sum(jnp.where(hm[hh], lv, 0.0), axis=1, keepdims=True) * (1.0 / _A_HEAD_DIM)
                    delta = jnp.sum(jnp.where(hm[hh], dov * ov, 0.0), axis=1, keepdims=True)
                    dl = jnp.sum(jnp.where(hm[hh], dlv, 0.0), axis=1, keepdims=True)
                    pp = jnp.exp(jnp.where(allow_p, _dot_nt(qh, kp), _NEG) - lse_h)
                    pc = jnp.exp(jnp.where(allow_c, _dot_nt(qh, kc), _NEG) - lse_h)
                    dsp = (pp * (_dot_nt(doh, vp) - delta + dl)).astype(_BF)
                    dsc = (pc * (_dot_nt(doh, vc) - delta + dl)).astype(_BF)
                    dq_s = dq_s + jnp.where(hm[hh], _dot(dsp, kp) + _dot(dsc, kc), 0.0)
                    dkp_s = dkp_s + _dot_tn(dsp, qh)
                    dkc_s = dkc_s + _dot_tn(dsc, qh)
                    dvp_s = dvp_s + _dot_tn(pp, doh)
                    dvc_s = dvc_s + _dot_tn(pc, doh)
                cur_q[:, cols] = dq_s
                cur_kp[:, cols] = dkp_s
                cur_kc[:, cols] = dkc_s
                cur_vp[:, cols] = dvp_s
                cur_vc[:, cols] = dvc_s

        out_ref[:, 0:w] = car_q[...]
        out_ref[:, w:2 * w] = car_k[...] + cur_kp[...]
        out_ref[:, 2 * w:3 * w] = car_v[...] + cur_vp[...]

        @pl.when(n < nb)
        def _():
            car_q[...] = cur_q[...]
            car_k[...] = cur_kc[...]
            car_v[...] = cur_vc[...]

    def at(col, prev):
        if prev:
            return pl.BlockSpec((_WIN_BLOCK, w), lambda r, n: (jnp.maximum(jnp.minimum(n, nb - 1) - 1, 0), 3 * r + col))
        return pl.BlockSpec((_WIN_BLOCK, w), lambda r, n: (jnp.minimum(n, nb - 1), 3 * r + col))

    blk = pl.BlockSpec((_WIN_BLOCK, w), lambda r, n: (jnp.minimum(n, nb - 1), r))
    scratch = [pltpu.VMEM((_WIN_BLOCK, w), _F32) for _ in range(8)]
    out = pl.pallas_call(
        body, name=f"attn_bwd_d{dil}", grid=(dil, nb + 1),
        in_specs=[at(0, False), at(1, True), at(1, False), at(2, True), at(2, False), blk, blk, blk, blk],
        out_specs=pl.BlockSpec((_WIN_BLOCK, 3 * w), lambda r, n: (jnp.maximum(n - 1, 0), r)),
        out_shape=jax.ShapeDtypeStruct((length, dil * 3 * w), _F32), scratch_shapes=scratch,
        compiler_params=_params(("arbitrary", "arbitrary")),
    )(view, view, view, view, view, o, lse, do, dlse)
    return out.reshape(s, 3 * w)


def _mix(o1, o2, o3, l1, l2, l3):
    m = jnp.maximum(jnp.maximum(l1, l2), l3)
    e1, e2, e3 = jnp.exp(l1 - m), jnp.exp(l2 - m), jnp.exp(l3 - m)
    return (e1 * o1 + e2 * o2 + e3 * o3) / (e1 + e2 + e3)


def _attn_mix_fwd(outs, lses):
    s = outs[0].shape[0]
    tiles = [(t, _A_WIDTH, 0) for t in (*outs, *lses)]
    return _rowmap("attn_mix_fwd", _mix, tiles, [], [(2 * _A_WIDTH, _A_WIDTH, 0, _BF)], rows=s, tm=_CONV_TM)[0]


def _attn_mix_bwd(outs, lses, dy):
    s = outs[0].shape[0]
    tiles = [(t, _A_WIDTH, 0) for t in (*outs, *lses, dy)]
    body = _vjp_body(_mix, 6, range(6), 1)
    return _rowmap("attn_mix_bwd", body, tiles, [], [(_A_WIDTH, _A_WIDTH, 0, _F32)] * 6, rows=s, tm=_CONV_TM)


_CONV_TM = 256
_HALO = 32


def _ln_silu(pre, g, b):
    mu = jnp.mean(pre, axis=-1, keepdims=True)
    xc = pre - mu
    y = xc * lax.rsqrt(jnp.mean(xc * xc, axis=-1, keepdims=True) + _NORM_EPS)
    return _silu(y * g + b)


def _glu_into(buf, ua_ref, ub_ref, uah_ref, ubh_ref, first):
    halo = uah_ref[...] * jax.nn.sigmoid(ubh_ref[...])
    buf[0:_HALO, :] = jnp.where(first, 0.0, halo)
    buf[_HALO:, :] = ua_ref[...] * jax.nn.sigmoid(ub_ref[...])


def _causal_taps(buf, w_ref, rows):
    base = _HALO - (_CONV_WIDTH - 1)
    acc = jnp.zeros((rows, buf.shape[1]), _F32)
    for j in range(_CONV_WIDTH):
        acc = acc + w_ref[j:j + 1, :] * buf[base + j:base + j + rows, :]
    return acc


def _conv_specs(s):
    tm = _CONV_TM
    per = tm // _HALO
    cw = _A_WIDTH
    cur = lambda col: pl.BlockSpec((tm, cw), lambda i: (i, col))
    prev = lambda col: pl.BlockSpec((_HALO, cw), lambda i: (jnp.maximum(i * per - 1, 0), col))
    return tm, per, cw, cur, prev


def _full2(arr):
    return pl.BlockSpec(arr.shape, lambda i: (0, 0))


def _conv_fwd(proj, dw_w, dw_b, ln_g, ln_b, y):
    s = proj.shape[0]
    tm, per, cw, cur, prev = _conv_specs(s)

    def body(ua_ref, ub_ref, uah_ref, ubh_ref, w_ref, b_ref, g_ref, be_ref, y_in, o_ref, buf):
        del y_in
        _glu_into(buf, ua_ref, ub_ref, uah_ref, ubh_ref, pl.program_id(0) == 0)
        pre = _causal_taps(buf, w_ref, tm) + b_ref[...]
        o_ref[...] = _ln_silu(pre, g_ref[...], be_ref[...]).astype(_BF)

    return pl.pallas_call(
        body, name="conv_fwd", grid=(s // tm,),
        in_specs=[cur(3), cur(4), prev(3), prev(4), _full2(dw_w), _full2(dw_b), _full2(ln_g), _full2(ln_b), _ANY],
        out_specs=pl.BlockSpec((tm, cw), lambda i: (i, 1)), out_shape=jax.ShapeDtypeStruct(y.shape, y.dtype),
        scratch_shapes=[pltpu.VMEM((tm + _HALO, cw), _F32)], input_output_aliases={8: 0},
        compiler_params=_params(("arbitrary",)),
    )(proj, proj, proj, proj, dw_w, dw_b, ln_g, ln_b, y)


def _conv_bwd_pre(proj, dw_w, dw_b, ln_g, ln_b, dy):
    s = proj.shape[0]
    tm, per, cw, cur, prev = _conv_specs(s)

    def body(ua_ref, ub_ref, uah_ref, ubh_ref, w_ref, b_ref, g_ref, be_ref, dy_ref, dp_ref, dg_ref, dbe_ref, db_ref, buf):
        i = pl.program_id(0)
        _glu_into(buf, ua_ref, ub_ref, uah_ref, ubh_ref, i == 0)
        pre = _causal_taps(buf, w_ref, tm) + b_ref[...]
        _, pull = jax.vjp(_ln_silu, pre, g_ref[...], be_ref[...])
        dpre, dg, dbe = pull(dy_ref[...])
        dp_ref[...] = dpre

        @pl.when(i == 0)
        def _():
            dg_ref[...] = jnp.zeros_like(dg_ref)
            dbe_ref[...] = jnp.zeros_like(dbe_ref)
            db_ref[...] = jnp.zeros_like(db_ref)

        dg_ref[...] += dg
        dbe_ref[...] += dbe
        db_ref[...] += jnp.sum(dpre, axis=0, keepdims=True)

    vec = pl.BlockSpec((1, cw), lambda i: (0, 0))
    vsd = jax.ShapeDtypeStruct((1, cw), _F32)
    return pl.pallas_call(
        body, name="conv_bwd_pre", grid=(s // tm,),
        in_specs=[cur(3), cur(4), prev(3), prev(4), _full2(dw_w), _full2(dw_b), _full2(ln_g), _full2(ln_b), cur(1)],
        out_specs=[pl.BlockSpec((tm, cw), lambda i: (i, 0)), vec, vec, vec],
        out_shape=[jax.ShapeDtypeStruct((s, cw), _F32), vsd, vsd, vsd],
        scratch_shapes=[pltpu.VMEM((tm + _HALO, cw), _F32)], compiler_params=_params(("arbitrary",)),
    )(proj, proj, proj, proj, dw_w, dw_b, ln_g, ln_b, dy)


def _conv_bwd_taps(proj, dw_w, dpre):
    s = proj.shape[0]
    tm, per, cw, cur, prev = _conv_specs(s)
    nt = s // tm

    def body(ua_ref, ub_ref, uah_ref, ubh_ref, w_ref, dp_ref, dpn_ref, dglu_ref, dw_ref, buf, dbuf):
        i = pl.program_id(0)
        _glu_into(buf, ua_ref, ub_ref, uah_ref, ubh_ref, i == 0)
        dpre = dp_ref[...]
        dbuf[0:tm, :] = dpre
        dbuf[tm:, :] = jnp.where(i == nt - 1, 0.0, dpn_ref[...])
        acc = jnp.zeros((tm, cw), _F32)
        for j in range(_CONV_WIDTH):
            off = _CONV_WIDTH - 1 - j
            acc = acc + w_ref[j:j + 1, :] * dbuf[off:off + tm, :]
        dglu_ref[...] = acc

        @pl.when(i == 0)
        def _():
            dw_ref[...] = jnp.zeros_like(dw_ref)

        base = _HALO - (_CONV_WIDTH - 1)
        for j in range(_CONV_WIDTH):
            dw_ref[j:j + 1, :] += jnp.sum(dpre * buf[base + j:base + j + tm, :], axis=0, keepdims=True)

    nxt = pl.BlockSpec((_HALO, cw), lambda i: (jnp.minimum((i + 1) * per, s // _HALO - 1), 0))
    return pl.pallas_call(
        body, name="conv_bwd_taps", grid=(nt,),
        in_specs=[cur(3), cur(4), prev(3), prev(4), _full2(dw_w), pl.BlockSpec((tm, cw), lambda i: (i, 0)), nxt],
        out_specs=[pl.BlockSpec((tm, cw), lambda i: (i, 0)), pl.BlockSpec((_HALO, cw), lambda i: (0, 0))],
        out_shape=[jax.ShapeDtypeStruct((s, cw), _F32), jax.ShapeDtypeStruct((_HALO, cw), _F32)],
        scratch_shapes=[pltpu.VMEM((tm + _HALO, cw), _F32), pltpu.VMEM((tm + _HALO, cw), _F32)],
        compiler_params=_params(("arbitrary",)),
    )(proj, proj, proj, proj, dw_w, dpre, dpre)


def _hyb_dproj(dqkv, tab_c, tab_s, dglu, proj):
    s = proj.shape[0]
    w = _A_WIDTH

    def body(d1_ref, d2_ref, d3_ref, c_ref, s_ref, dg_ref, ua_ref, ub_ref, o_ref):
        j = pl.program_id(1)

        @pl.when(j < 3)
        def _():
            x = d1_ref[...] + d2_ref[...] + d3_ref[...]
            o_ref[...] = _rot_apply(x, c_ref[...], s_ref[...], True).astype(_BF)

        @pl.when(j == 3)
        def _():
            o_ref[...] = (dg_ref[...] * jax.nn.sigmoid(ub_ref[...])).astype(_BF)

        @pl.when(j == 4)
        def _():
            sg = jax.nn.sigmoid(ub_ref[...])
            o_ref[...] = (dg_ref[...] * ua_ref[...] * sg * (1.0 - sg)).astype(_BF)

    dq = pl.BlockSpec((_TM, w), lambda i, j: (i, jnp.minimum(j, 2)))
    tab = pl.BlockSpec((None, _TM, 128), lambda i, j: (jnp.minimum(j, 2), i, 0))
    col = lambda c: pl.BlockSpec((_TM, w), lambda i, j: (i, c))
    return pl.pallas_call(
        body, name="hyb_dproj", grid=(s // _TM, 5),
        in_specs=[dq, dq, dq, tab, tab, col(0), col(3), col(4)],
        out_specs=pl.BlockSpec((_TM, w), lambda i, j: (i, j)),
        out_shape=jax.ShapeDtypeStruct((s, 5 * w), _BF), compiler_params=_params(("arbitrary", "arbitrary")),
    )(*dqkv, tab_c, tab_s, dglu, proj, proj)


def _proj_in(name, xn, w, layer):
    s, d = xn.shape
    ns = w.shape[-1]
    return _mm(name, xn, w, grid=(4, s // _TM), a_spec=pl.BlockSpec((_TM, d), lambda j, i: (i, 0)),
               b_spec=pl.BlockSpec((None, None, d, ns), lambda j, i: (layer, j, 0, 0)), dn=_NN,
               out_shape=jax.ShapeDtypeStruct((s, 4 * ns), _F32), out_spec=pl.BlockSpec((_TM, ns), lambda j, i: (i, j)))


def _proj_in_dx(name, dz, w, layer):
    s = dz.shape[0]
    d, ns = w.shape[2], w.shape[3]
    return _mm(name, dz, w, grid=(s // _TM, 4), a_spec=pl.BlockSpec((_TM, ns), lambda i, j: (i, j)),
               b_spec=pl.BlockSpec((None, None, d, ns), lambda i, j: (layer, j, 0, 0)), dn=_NT, red=1,
               acc_shape=(_TM, d), out_shape=jax.ShapeDtypeStruct((s, d), _F32),
               out_spec=pl.BlockSpec((_TM, d), lambda i, j: (i, 0)))


def _proj_in_dw(name, xn, dz):
    s, d = xn.shape
    ns = dz.shape[1] // 4
    return _mm(name, xn, dz, grid=(4, s // _TM), a_spec=pl.BlockSpec((_TM, d), lambda j, i: (i, 0)),
               b_spec=pl.BlockSpec((_TM, ns), lambda j, i: (i, j)), dn=_TN, red=1, acc_shape=(d, ns),
               out_shape=jax.ShapeDtypeStruct((4, d, ns), _BF),
               out_spec=pl.BlockSpec((None, d, ns), lambda j, i: (j, 0, 0)))


def _proj_out_dy(name, dh, w, layer):
    s, d = dh.shape
    k = w.shape[1]
    return _mm(name, dh, w, grid=(s // _TM,), a_spec=pl.BlockSpec((_TM, d), lambda i: (i, 0)),
               b_spec=pl.BlockSpec((None, k, d), lambda i: (layer, 0, 0)), dn=_NT,
               out_shape=jax.ShapeDtypeStruct((s, k), _F32), out_spec=pl.BlockSpec((_TM, k), lambda i: (i, 0)))


def _hyb_fwd(h, gamma, w_in, w_out, dw_w, dw_b, ln_g, ln_b, tabs, li):
    hn = _rmsnorm_fwd(h, gamma)
    proj = _proj_in("hyb_in", hn, w_in, li)
    qkv = _qkv_prep(proj, *tabs)
    branches = [_attn_fwd(qkv, dil) for dil in _DILATIONS]
    outs, lses = [b[0] for b in branches], [b[1] for b in branches]
    y = _attn_mix_fwd(outs, lses)
    y = _conv_fwd(proj, dw_w, dw_b, ln_g, ln_b, y)
    h_new = _proj_residual("hyb_out", y, w_out, li, h, 1.0)
    return h_new, (h, hn, proj, qkv, outs, lses, y)


def _hyb_bwd(dh, saved, gamma, w_in, w_out, dw_w, dw_b, ln_g, ln_b, tabs, li):
    h, hn, proj, qkv, outs, lses, y = saved
    dy = _proj_out_dy("hyb_dy", dh, w_out, li)
    d_w_out = _wgrad_rows("hyb_dwout", y, dh, 1.0)
    mixg = _attn_mix_bwd(outs, lses, dy)
    dqkv = [_attn_bwd(qkv, outs[b], lses[b], mixg[b], mixg[3 + b], dil) for b, dil in enumerate(_DILATIONS)]
    dpre, d_ln_g, d_ln_b, d_dw_b = _conv_bwd_pre(proj, dw_w, dw_b, ln_g, ln_b, dy)
    dglu, d_dw_w = _conv_bwd_taps(proj, dw_w, dpre)
    dproj = _hyb_dproj(dqkv, *tabs, dglu, proj)
    dhn = _proj_in_dx("hyb_dhn", dproj, w_in, li)
    d_w_in = _proj_in_dw("hyb_dwin", hn, dproj)
    dh_in, dgamma = _rmsnorm_bwd(h, gamma, dhn, dh)
    return dh_in, dgamma, (d_dw_w, d_dw_b, d_ln_g, d_ln_b), d_w_in, d_w_out


_GDN_BLOCK = 64
_GDN_HEADS_PER_STEP = 8


def _gdn_chunk(qs, ks, vs, gates, states, heads):
    n = len(qs)
    hs = range(n)
    c = qs[0].shape[0]
    lane = lax.broadcasted_iota(jnp.int32, gates.shape, 1)
    row = lax.broadcasted_iota(jnp.int32, (c, c), 0)
    col = lax.broadcasted_iota(jnp.int32, (c, c), 1)
    rowc = lax.broadcasted_iota(jnp.int32, (c, 1), 0)
    causal = row >= col
    eye = jnp.where(row == col, 1.0, 0.0)
    beta = [jnp.sum(jnp.where(lane == heads[h], gates, 0.0), axis=1, keepdims=True) for h in hs]
    g = [jnp.sum(jnp.where(lane == heads[h] + _GDN_HEADS, gates, 0.0), axis=1, keepdims=True) for h in hs]
    gc_row = [jnp.sum(jnp.where(row <= col, g[h], 0.0), axis=0, keepdims=True) for h in hs]
    gc_col = [jnp.sum(jnp.where(row == col, gc_row[h], 0.0), axis=1, keepdims=True) for h in hs]
    decay = [jnp.where(causal, jnp.exp(jnp.where(causal, gc_col[h] - gc_row[h], 0.0)), 0.0) for h in hs]
    kb = [ks[h] * beta[h] for h in hs]
    l_mat = [jnp.where(row > col, _dot_nt(kb[h], ks[h]) * decay[h], 0.0) for h in hs]
    t_inv = [eye - l_mat[h] for h in hs]
    x = [_dot(l_mat[h], l_mat[h]) for h in hs]
    for it in range(5):
        t_inv = [t_inv[h] + _dot(t_inv[h], x[h]) for h in hs]
        if it < 4:
            x = [_dot(x[h], x[h]) for h in hs]
    eg = [jnp.exp(gc_col[h]) for h in hs]
    u = [_dot(t_inv[h], vs[h] * beta[h]) for h in hs]
    w = [_dot(t_inv[h], kb[h] * eg[h]) for h in hs]
    attn = [jnp.where(causal, _dot_nt(qs[h], ks[h]) * decay[h], 0.0) for h in hs]
    g_last = [jnp.sum(jnp.where(rowc == c - 1, gc_col[h], 0.0), axis=0, keepdims=True) for h in hs]
    k_dec = [ks[h] * jnp.exp(g_last[h] - gc_col[h]) for h in hs]
    v_new = [u[h] - _dot(w[h], states[h]) for h in hs]
    o = [_dot(qs[h] * eg[h], states[h]) + _dot(attn[h], v_new[h]) for h in hs]
    new_states = [states[h] * jnp.exp(g_last[h]) + _dot_tn(k_dec[h], v_new[h]) for h in hs]
    return o, new_states


def _gdn_chunk_fwd(q, k, v, gates):
    s = q.shape[0]
    cb, c = _GDN_BLOCK, _GDN_CHUNK
    nblk, per = s // cb, cb // c

    hps = _GDN_HEADS_PER_STEP

    def body(q_ref, k_ref, v_ref, g_ref, o_ref, st_ref, state):
        @pl.when(pl.program_id(1) == 0)
        def _():
            state[...] = jnp.zeros_like(state)

        heads = [pl.program_id(0) * hps + hh for hh in range(hps)]
        cols = [slice(hh * _GDN_DIM, (hh + 1) * _GDN_DIM) for hh in range(hps)]
        for ci in range(per):
            rows = slice(ci * c, (ci + 1) * c)
            sts = [state[hh] for hh in range(hps)]
            for hh in range(hps):
                st_ref[hh, ci] = sts[hh]
            outs, new = _gdn_chunk([q_ref[rows, cl] for cl in cols], [k_ref[rows, cl] for cl in cols],
                                   [v_ref[rows, cl] for cl in cols], g_ref[rows, :], sts, heads)
            for hh in range(hps):
                o_ref[rows, cols[hh]] = outs[hh]
                state[hh] = new[hh]

    slab = pl.BlockSpec((cb, hps * _GDN_DIM), lambda h, i: (i, h))
    return pl.pallas_call(
        body, name="gdn_chunk_fwd", grid=(_GDN_HEADS // hps, nblk),
        in_specs=[slab, slab, slab, pl.BlockSpec((cb, 128), lambda h, i: (i, 0))],
        out_specs=[slab, pl.BlockSpec((hps, per, _GDN_DIM, _GDN_DIM), lambda h, i: (h, i, 0, 0))],
        out_shape=[jax.ShapeDtypeStruct((s, _GDN_HEADS * _GDN_DIM), _F32),
                   jax.ShapeDtypeStruct((_GDN_HEADS, s // c, _GDN_DIM, _GDN_DIM), _F32)],
        scratch_shapes=[pltpu.VMEM((hps, _GDN_DIM, _GDN_DIM), _F32)],
        compiler_params=_params(("arbitrary", "arbitrary")),
    )(q, k, v, gates)


def _gdn_chunk_bwd(q, k, v, gates, states, do):
    s = q.shape[0]
    cb, c = _GDN_BLOCK, _GDN_CHUNK
    nblk, per = s // cb, cb // c

    hps = _GDN_HEADS_PER_STEP

    def body(q_ref, k_ref, v_ref, g_ref, st_ref, do_ref, dq_ref, dk_ref, dv_ref, dg_ref, dstate):
        @pl.when(pl.program_id(1) == 0)
        def _():
            dstate[...] = jnp.zeros_like(dstate)

        heads = [pl.program_id(0) * hps + hh for hh in range(hps)]
        cols = [slice(hh * _GDN_DIM, (hh + 1) * _GDN_DIM) for hh in range(hps)]
        for ci in reversed(range(per)):
            rows = slice(ci * c, (ci + 1) * c)
            fn = functools.partial(_gdn_chunk, heads=heads)
            _, pull = jax.vjp(fn, [q_ref[rows, cl] for cl in cols], [k_ref[rows, cl] for cl in cols],
                              [v_ref[rows, cl] for cl in cols], g_ref[rows, :], [st_ref[hh, ci] for hh in range(hps)])
            dq, dk, dv, dg, dst = pull(([do_ref[rows, cl] for cl in cols], [dstate[hh] for hh in range(hps)]))
            dg_ref[rows, :] = dg
            for hh in range(hps):
                dq_ref[rows, cols[hh]] = dq[hh]
                dk_ref[rows, cols[hh]] = dk[hh]
                dv_ref[rows, cols[hh]] = dv[hh]
                dstate[hh] = dst[hh]

    slab = pl.BlockSpec((cb, hps * _GDN_DIM), lambda h, i: (nblk - 1 - i, h))
    sds = jax.ShapeDtypeStruct((s, _GDN_HEADS * _GDN_DIM), _F32)
    return pl.pallas_call(
        body, name="gdn_chunk_bwd", grid=(_GDN_HEADS // hps, nblk),
        in_specs=[slab, slab, slab, pl.BlockSpec((cb, 128), lambda h, i: (nblk - 1 - i, 0)),
                  pl.BlockSpec((hps, per, _GDN_DIM, _GDN_DIM), lambda h, i: (h, nblk - 1 - i, 0, 0)), slab],
        out_specs=[slab, slab, slab, pl.BlockSpec((None, cb, 128), lambda h, i: (h, nblk - 1 - i, 0))],
        out_shape=[sds, sds, sds, jax.ShapeDtypeStruct((_GDN_HEADS // hps, s, 128), _F32)],
        scratch_shapes=[pltpu.VMEM((hps, _GDN_DIM, _GDN_DIM), _F32)],
        compiler_params=_params(("arbitrary", "arbitrary")),
    )(q, k, v, gates, states, do)


_SHALO = 8


def _swish_l2(c, normalize, scale):
    y = _silu(c)
    if normalize:
        y = y * lax.rsqrt(jnp.sum(y * y, axis=-1, keepdims=True) + _NORM_EPS) * scale
    return y


def _short_taps(buf, w_ref, rows):
    base = _SHALO - (_GDN_SHORT_CONV - 1)
    acc = jnp.zeros((rows, buf.shape[1]), _F32)
    for t in range(_GDN_SHORT_CONV):
        acc = acc + w_ref[t:t + 1, :] * buf[base + t:base + t + rows, :]
    return acc


_GDN_W = _GDN_HEADS * _GDN_DIM


def _gdn_conv_fwd(proj, conv_w, grp, normalize, scale):
    s = proj.shape[0]
    tm = _CONV_TM
    per = tm // _SHALO

    def body(x_ref, xp_ref, w_ref, o_ref, buf):
        buf[0:_SHALO, :] = jnp.where(pl.program_id(0) == 0, 0.0, xp_ref[...])
        buf[_SHALO:, :] = x_ref[...]
        conv = _short_taps(buf, w_ref, tm)
        for h in range(_GDN_HEADS):
            cols = slice(h * _GDN_DIM, (h + 1) * _GDN_DIM)
            o_ref[:, cols] = _swish_l2(conv[:, cols], normalize, scale)

    return pl.pallas_call(
        body, name=f"gdn_conv_fwd_{grp}", grid=(s // tm,),
        in_specs=[pl.BlockSpec((tm, _GDN_W), lambda i: (i, grp)),
                  pl.BlockSpec((_SHALO, _GDN_W), lambda i: (jnp.maximum(i * per - 1, 0), grp)),
                  pl.BlockSpec((_GDN_SHORT_CONV, _GDN_W), lambda i: (0, grp))],
        out_specs=pl.BlockSpec((tm, _GDN_W), lambda i: (i, 0)),
        out_shape=jax.ShapeDtypeStruct((s, _GDN_W), _F32),
        scratch_shapes=[pltpu.VMEM((tm + _SHALO, _GDN_W), _F32)], compiler_params=_params(("arbitrary",)),
    )(proj, proj, conv_w)


def _gdn_conv_bwd(proj, conv_w, grp, normalize, scale, dy, dproj):
    s = proj.shape[0]
    tm = _CONV_TM
    per = tm // _SHALO
    nt = s // tm
    taps = _GDN_SHORT_CONV

    def body(x_ref, xp_ref, xn_ref, w_ref, dy_ref, dyn_ref, dp_in, dx_ref, dw_ref, buf, dbuf):
        del dp_in
        i = pl.program_id(0)
        last = i == nt - 1
        buf[0:_SHALO, :] = jnp.where(i == 0, 0.0, xp_ref[...])
        buf[_SHALO:_SHALO + tm, :] = x_ref[...]
        buf[_SHALO + tm:, :] = jnp.where(last, 0.0, xn_ref[...])
        conv = _short_taps(buf, w_ref, tm + _SHALO)
        dy_ext = jnp.concatenate([dy_ref[...], jnp.where(last, 0.0, dyn_ref[...])], axis=0)
        for h in range(_GDN_HEADS):
            cols = slice(h * _GDN_DIM, (h + 1) * _GDN_DIM)
            _, pull = jax.vjp(lambda c: _swish_l2(c, normalize, scale), conv[:, cols])
            dbuf[:, cols] = pull(dy_ext[:, cols])[0]
        acc = jnp.zeros((tm, _GDN_W), _F32)
        for t in range(taps):
            o = taps - 1 - t
            acc = acc + w_ref[t:t + 1, :] * dbuf[o:o + tm, :]
        dx_ref[...] = acc.astype(dx_ref.dtype)

        @pl.when(i == 0)
        def _():
            dw_ref[...] = jnp.zeros_like(dw_ref)

        base = _SHALO - (taps - 1)
        for t in range(taps):
            dw_ref[t:t + 1, :] += jnp.sum(dbuf[0:tm, :] * buf[base + t:base + t + tm, :], axis=0, keepdims=True)

    cur = lambda col: pl.BlockSpec((tm, _GDN_W), lambda i: (i, col))
    nxt = lambda col: pl.BlockSpec((_SHALO, _GDN_W), lambda i: (jnp.minimum((i + 1) * per, s // _SHALO - 1), col))
    return pl.pallas_call(
        body, name=f"gdn_conv_bwd_{grp}", grid=(nt,),
        in_specs=[cur(grp), pl.BlockSpec((_SHALO, _GDN_W), lambda i: (jnp.maximum(i * per - 1, 0), grp)), nxt(grp),
                  pl.BlockSpec((taps, _GDN_W), lambda i: (0, grp)), cur(0), nxt(0), _ANY],
        out_specs=[cur(grp), pl.BlockSpec((_SHALO, _GDN_W), lambda i: (0, 0))],
        out_shape=[jax.ShapeDtypeStruct(dproj.shape, dproj.dtype), jax.ShapeDtypeStruct((_SHALO, _GDN_W), _F32)],
        scratch_shapes=[pltpu.VMEM((tm + 2 * _SHALO, _GDN_W), _F32), pltpu.VMEM((tm + _SHALO, _GDN_W), _F32)],
        input_output_aliases={6: 0}, compiler_params=_params(("arbitrary",)),
    )(proj, proj, proj, conv_w, dy, dy, dproj)


def _softplus(z):
    return jnp.maximum(z, 0.0) + jnp.log(1.0 + jnp.exp(-jnp.abs(z)))


def _gates(ba, a_lane, dt_lane):
    lane = lax.broadcasted_iota(jnp.int32, ba.shape, 1)
    g = -jnp.exp(a_lane) * _softplus(ba + dt_lane)
    return jnp.where(lane < _GDN_HEADS, jax.nn.sigmoid(ba), jnp.where(lane < 2 * _GDN_HEADS, g, 0.0))


def _gates_fwd(ba, a_lane, dt_lane):
    return _rowmap("gdn_gates_fwd", _gates, [(ba, 128, 0)], [a_lane, dt_lane], [(128, 128, 0, _F32)],
                   rows=ba.shape[0], tm=_TM)[0]


def _gates_bwd(ba, a_lane, dt_lane, dgates):
    def body(bav, dg8, av, dv):
        _, pull = jax.vjp(_gates, bav, av, dv)
        return pull(jnp.sum(dg8, axis=0))

    return _rowmap("gdn_gates_bwd", body, [(ba, 128, 0), (dgates, 128, 0)], [a_lane, dt_lane], [(128, 128, 0, _BF)],
                   [(1, 128), (1, 128)], rows=ba.shape[0], tm=_TM)


def _out_gate(o, z, g):
    return (_rms(o, g) * _silu(z)).astype(_BF)


def _head_cols():
    return [slice(h * _GDN_DIM, (h + 1) * _GDN_DIM) for h in range(_GDN_HEADS)]


def _out_gate_fwd(o, proj, norm_g):
    def body(ov, zv, gv):
        return jnp.concatenate([_out_gate(ov[:, cl], zv[:, cl], gv) for cl in _head_cols()], axis=1)

    return _rowmap("gdn_out_gate_fwd", body, [(o, _GDN_W, 0), (proj, _GDN_W, 3)], [norm_g], [(_GDN_W, _GDN_W, 0, _BF)],
                   rows=o.shape[0], tm=_CONV_TM)[0]


def _out_gate_bwd(o, proj, norm_g, dy):
    def body(ov, zv, dyv, gv):
        d_o, d_z, d_g = [], [], jnp.zeros_like(gv)
        for cl in _head_cols():
            _, pull = jax.vjp(_out_gate, ov[:, cl], zv[:, cl], gv)
            g_o, g_z, g_g = pull(dyv[:, cl].astype(_BF))
            d_o.append(g_o)
            d_z.append(g_z)
            d_g = d_g + g_g
        return jnp.concatenate(d_o, axis=1), jnp.concatenate(d_z, axis=1), d_g

    return _rowmap("gdn_out_gate_bwd", body, [(o, _GDN_W, 0), (proj, _GDN_W, 3), (dy, _GDN_W, 0)], [norm_g],
                   [(_GDN_W, _GDN_W, 0, _F32), (proj.shape[1], _GDN_W, 3, _BF)], [(1, _GDN_DIM)],
                   rows=o.shape[0], tm=_CONV_TM)


def _plain_in(name, xn, w, layer, tn):
    s, d = xn.shape
    n = w.shape[2]
    return _mm(name, xn, w, grid=(n // tn, s // _TM), a_spec=pl.BlockSpec((_TM, d), lambda j, i: (i, 0)),
               b_spec=pl.BlockSpec((None, d, tn), lambda j, i: (layer, 0, j)), dn=_NN,
               out_shape=jax.ShapeDtypeStruct((s, n), _F32), out_spec=pl.BlockSpec((_TM, tn), lambda j, i: (i, j)))


def _plain_dx(name, dz, w, layer, tn, extra=None):
    s, n = dz.shape
    d = w.shape[1]
    row = pl.BlockSpec((_TM, d), lambda i, j: (i, 0))
    return _mm(name, dz, w, grid=(s // _TM, n // tn), a_spec=pl.BlockSpec((_TM, tn), lambda i, j: (i, j)),
               b_spec=pl.BlockSpec((None, d, tn), lambda i, j: (layer, 0, j)), dn=_NT, red=1, acc_shape=(_TM, d),
               out_shape=jax.ShapeDtypeStruct((s, d), _F32), out_spec=row,
               epi=(lambda acc, e: acc + e) if extra is not None else None,
               extras=(extra,) if extra is not None else (), extra_specs=(row,) if extra is not None else ())


def _plain_dw(name, xn, dz, tn):
    s, d = xn.shape
    n = dz.shape[1]
    return _mm(name, xn, dz, grid=(n // tn, s // _TM), a_spec=pl.BlockSpec((_TM, d), lambda j, i: (i, 0)),
               b_spec=pl.BlockSpec((_TM, tn), lambda j, i: (i, j)), dn=_TN, red=1, acc_shape=(d, tn),
               out_shape=jax.ShapeDtypeStruct((d, n), _BF), out_spec=pl.BlockSpec((d, tn), lambda j, i: (0, j)))


_GDN_QK_SCALE = _GDN_DIM ** -0.5


def _gdn_fwd(h, gamma, w_qkvz, w_ba, w_out, conv_w, a_lane, dt_lane, norm_g, li):
    hn = _rmsnorm_fwd(h, gamma)
    proj = _plain_in("gdn_in", hn, w_qkvz, li, 1024)
    ba = _plain_in("gdn_in_ba", hn, w_ba, li, 128)
    q = _gdn_conv_fwd(proj, conv_w, 0, True, _GDN_QK_SCALE)
    k = _gdn_conv_fwd(proj, conv_w, 1, True, 1.0)
    v = _gdn_conv_fwd(proj, conv_w, 2, False, 1.0)
    gates = _gates_fwd(ba, a_lane, dt_lane)
    o, states = _gdn_chunk_fwd(q, k, v, gates)
    y = _out_gate_fwd(o, proj, norm_g)
    h_new = _proj_residual("gdn_out", y, w_out, li, h, 1.0)
    return h_new, (h, hn, proj, ba, q, k, v, gates, states, o, y)


def _gdn_bwd(dh, saved, gamma, w_qkvz, w_ba, w_out, conv_w, a_lane, dt_lane, norm_g, li, n_ba):
    h, hn, proj, ba, q, k, v, gates, states, o, y = saved
    dy = _proj_out_dy("gdn_dy", dh, w_out, li)
    d_w_out = _wgrad_rows("gdn_dwout", y, dh, 1.0)
    do, dproj, d_norm_g = _out_gate_bwd(o, proj, norm_g, dy)
    dq, dk, dv, dgates = _gdn_chunk_bwd(q, k, v, gates, states, do)
    dproj, dcw_q = _gdn_conv_bwd(proj, conv_w, 0, True, _GDN_QK_SCALE, dq, dproj)
    dproj, dcw_k = _gdn_conv_bwd(proj, conv_w, 1, True, 1.0, dk, dproj)
    dproj, dcw_v = _gdn_conv_bwd(proj, conv_w, 2, False, 1.0, dv, dproj)
    dba, d_a, d_dt = _gates_bwd(ba, a_lane, dt_lane, dgates)
    dhn_ba = _plain_dx("gdn_dhn_ba", dba, w_ba, li, 128)
    dhn = _plain_dx("gdn_dhn", dproj, w_qkvz, li, 1024, extra=dhn_ba)
    dw_qkvz = _plain_dw("gdn_dwin", hn, dproj, 1024)
    dw_ba = _plain_dw("gdn_dwin_ba", hn, dba, 128)
    dh_in, dgamma = _rmsnorm_bwd(h, gamma, dhn, dh)
    d_conv_w = jnp.concatenate([dcw_q, dcw_k, dcw_v], axis=1)[:_GDN_SHORT_CONV]
    d_w_in = jnp.concatenate([dw_qkvz, dw_ba[:, :n_ba]], axis=1)
    d_w_in = jnp.transpose(d_w_in.reshape(d_w_in.shape[0], 4, -1), (1, 0, 2))
    return dh_in, dgamma, (d_conv_w, d_a, d_dt, d_norm_g), d_w_in, d_w_out


_MESH = pl.DeviceIdType.MESH
_ROW_TM = 384


def _place():
    x, y, c = lax.axis_index("x"), lax.axis_index("y"), lax.axis_index("c")
    chips = [(1 - x, y), (x, 1 - y), (1 - x, 1 - y)]
    return x, y, c, chips, [2 * px + py for px, py in chips]


def _remote(src, dst, send_sem, recv_sem, device):
    return pltpu.make_async_remote_copy(src_ref=src, dst_ref=dst, send_sem=send_sem, recv_sem=recv_sem,
                                        device_id=device, device_id_type=_MESH)


def _rows2d(t):
    return t.reshape(-1, t.shape[-1])


def _row_tile(rows):
    for tm in range(min(rows, _ROW_TM) // 8 * 8, 7, -8):
        if rows % tm == 0:
            return tm
    raise ValueError(f"no row tile for {rows} rows")


def _placed_map(name, fn, where, ins, out_rows, cols, out_dtype, out_row, steps, tm, into=None):
    n_in = len(ins)

    def kern(where_ref, *refs):
        del where_ref
        o_ref = refs[n_in + (1 if into is not None else 0)]
        o_ref[...] = fn(*[r[...] for r in refs[:n_in]]).astype(o_ref.dtype)

    in_specs, operands = [], [where]
    for arr, rowfn in ins:
        operands.append(arr)
        if arr.ndim == 3:
            in_specs.append(pl.BlockSpec((arr.shape[0], tm, cols), lambda i, wh, f=rowfn: (0, f(i, wh), 0)))
        else:
            in_specs.append(pl.BlockSpec((tm, cols), lambda i, wh, f=rowfn: (f(i, wh), 0)))
    aliases = {}
    if into is not None:
        operands.append(into)
        in_specs.append(_ANY)
        aliases = {n_in + 1: 0}
    spec = pltpu.PrefetchScalarGridSpec(
        num_scalar_prefetch=1, grid=(steps,), in_specs=in_specs,
        out_specs=pl.BlockSpec((tm, cols), lambda i, wh: (out_row(i, wh), 0)))
    return pl.pallas_call(
        kern, name=name, grid_spec=spec, out_shape=jax.ShapeDtypeStruct((out_rows, cols), out_dtype),
        input_output_aliases=aliases, compiler_params=_params(("arbitrary",)),
    )(*operands)


def _cast_slab(name, t, where):
    n_l, r, cols = t.shape
    tm = _row_tile(r)
    per = r // tm
    buf = _placed_map("cast_" + name, lambda val: val, where, [(_rows2d(t), lambda i, wh: i)], n_l * 4 * r, cols, _BF,
                      lambda i, wh: ((i // per) * 4 + wh[1]) * per + i % per, n_l * per, tm)
    return buf.reshape(n_l, 4, r, cols)


_HBM = pl.BlockSpec(memory_space=pltpu.HBM)
_SEM = pl.BlockSpec(memory_space=pltpu.SEMAPHORE)
_EFFECT = pltpu.SideEffectType.DATAFLOW_SIDE_EFFECTING


def _ici_copies(refs, send, recv, arrivals):
    x, y, c, chips, idx = _place()
    me = 2 * x + y
    res = []
    for t, ref in enumerate(refs):
        half = ref.shape[0] // 2
        mine = pl.ds(c * half, half)
        for k in range(3):
            sem = 3 * t + k
            block = ref.at[mine, idx[k] if arrivals else me]
            res.append(_remote(block, block, send.at[sem], recv.at[sem], (*chips[k], c)))
    return res


def _gather_start(name, bufs, after):
    n = len(bufs)

    def body(*refs):
        send, recv = refs[n + 1], refs[n + 2]
        for cp in _ici_copies(refs[:n], send, recv, False):
            cp.start()
        refs[2 * n + 3][...] = jnp.zeros((8, 128), _F32)

    res = pl.pallas_call(
        body, name=name,
        out_shape=(pltpu.SemaphoreType.DMA((3 * n,)), pltpu.SemaphoreType.DMA((3 * n,)),
                   *[pltpu.HBM(b.shape, b.dtype) for b in bufs], jax.ShapeDtypeStruct((8, 128), _F32)),
        in_specs=[*[_HBM] * n, _ANY], out_specs=(_SEM, _SEM, *[_HBM] * n, pl.BlockSpec(memory_space=pltpu.VMEM)),
        input_output_aliases={t: 2 + t for t in range(n)},
        compiler_params=pltpu.CompilerParams(has_side_effects=_EFFECT),
    )(*[pltpu.with_memory_space_constraint(b, pltpu.HBM) for b in bufs], after)
    return res[0], res[1], list(res[2:2 + n]), res[2 + n]


def _gather_wait(name, send_sem, recv_sem, bufs, after):
    n = len(bufs)

    def body(*refs):
        send, recv = refs[n], refs[n + 1]
        for cp in _ici_copies(refs[:n], send, recv, False):
            cp.wait_send()
        for cp in _ici_copies(refs[:n], send, recv, True):
            cp.wait_recv()

    res = pl.pallas_call(
        body, name=name, out_shape=tuple(pltpu.HBM(b.shape, b.dtype) for b in bufs),
        in_specs=[*[_HBM] * n, _SEM, _SEM, _ANY], out_specs=tuple([_HBM] * n),
        input_output_aliases={t: t for t in range(n)},
        compiler_params=pltpu.CompilerParams(has_side_effects=_EFFECT),
    )(*bufs, send_sem, recv_sem, after)
    return list(res)


def _gather_forward(name, bufs):
    n = len(bufs)

    def body(*refs):
        outs, send, recv = refs[n:2 * n], refs[2 * n], refs[2 * n + 1]
        x, y, c, _, idx = _place()
        copies, arrive = [], []
        for t, ref in enumerate(outs):
            half = ref.shape[0] // 2
            mine, other = pl.ds(c * half, half), pl.ds((1 - c) * half, half)
            for k in range(3):
                sem = 3 * t + k
                landed, dst = ref.at[mine, idx[k]], ref.at[other, idx[k]]
                copies.append(_remote(landed, landed, send.at[sem], recv.at[sem], (x, y, 1 - c)))
                arrive.append(_remote(dst, dst, send.at[sem], recv.at[sem], (x, y, 1 - c)))
        for cp in copies:
            cp.start()
        for cp in arrive:
            cp.wait_recv()
        for cp in copies:
            cp.wait_send()

    res = pl.pallas_call(
        body, name=name, in_specs=[_ANY] * n, out_specs=[_ANY] * n,
        out_shape=[jax.ShapeDtypeStruct(b.shape, b.dtype) for b in bufs],
        scratch_shapes=[pltpu.SemaphoreType.DMA((3 * n,)), pltpu.SemaphoreType.DMA((3 * n,))],
        input_output_aliases={t: t for t in range(n)},
    )(*bufs)
    return list(res)


def _rs_sibling(name, gs):
    n = len(gs)

    def body(*refs):
        send, recv = refs[2 * n], refs[2 * n + 1]
        x, y, c, _, _ = _place()
        copies = []
        for t in range(n):
            rh = refs[t].shape[1] // 2
            cp = _remote(refs[t].at[:, pl.ds((1 - c) * rh, rh)], refs[n + t], send.at[t], recv.at[t], (x, y, 1 - c))
            cp.start()
            copies.append(cp)
        for cp in copies:
            cp.wait()

    return pl.pallas_call(
        body, name=name, in_specs=[_ANY] * n, out_specs=[_ANY] * n,
        out_shape=[jax.ShapeDtypeStruct((4, g.shape[1] // 2, g.shape[2]), g.dtype) for g in gs],
        scratch_shapes=[pltpu.SemaphoreType.DMA((n,)), pltpu.SemaphoreType.DMA((n,))],
    )(*gs)


def _chip_copies(sums, lands, send, recv):
    x, y, c, chips, idx = _place()
    return [_remote(sums[t].at[idx[k]], lands[t].at[k], send.at[3 * t + k], recv.at[3 * t + k], (*chips[k], c))
            for t in range(len(sums)) for k in range(3)]


def _rs_chips_start(name, sums, after):
    n = len(sums)
    lands = [lax.empty((3,) + s.shape[1:], s.dtype) for s in sums]

    def body(*refs):
        send, recv = refs[2 * n + 1], refs[2 * n + 2]
        for cp in _chip_copies(refs[:n], refs[n:2 * n], send, recv):
            cp.start()
        refs[4 * n + 3][...] = jnp.zeros((8, 128), _F32)

    both = [*sums, *lands]
    res = pl.pallas_call(
        body, name=name,
        out_shape=(pltpu.SemaphoreType.DMA((3 * n,)), pltpu.SemaphoreType.DMA((3 * n,)),
                   *[pltpu.HBM(b.shape, b.dtype) for b in both], jax.ShapeDtypeStruct((8, 128), _F32)),
        in_specs=[*[_HBM] * (2 * n), _ANY],
        out_specs=(_SEM, _SEM, *[_HBM] * (2 * n), pl.BlockSpec(memory_space=pltpu.VMEM)),
        input_output_aliases={t: 2 + t for t in range(2 * n)},
        compiler_params=pltpu.CompilerParams(has_side_effects=_EFFECT),
    )(*[pltpu.with_memory_space_constraint(b, pltpu.HBM) for b in both], after)
    return res[0], res[1], list(res[2:2 + n]), list(res[2 + n:2 + 2 * n]), res[2 + 2 * n]


def _rs_chips_wait(name, send_sem, recv_sem, sums, lands, after):
    n = len(sums)

    def body(*refs):
        send, recv = refs[2 * n], refs[2 * n + 1]
        copies = _chip_copies(refs[:n], refs[n:2 * n], send, recv)
        for cp in copies:
            cp.wait_send()
        for cp in copies:
            cp.wait_recv()

    both = [*sums, *lands]
    res = pl.pallas_call(
        body, name=name, out_shape=tuple(pltpu.HBM(b.shape, b.dtype) for b in both),
        in_specs=[*[_HBM] * (2 * n), _SEM, _SEM, _ANY], out_specs=tuple([_HBM] * (2 * n)),
        input_output_aliases={t: t for t in range(2 * n)},
        compiler_params=pltpu.CompilerParams(has_side_effects=_EFFECT),
    )(*both, send_sem, recv_sem, after)
    return list(res[:n]), list(res[n:])


def _rs_join(name, totals, layers):
    n = len(totals)

    def body(*refs):
        outs, send, recv = refs[n:2 * n], refs[2 * n], refs[2 * n + 1]
        x, y, c, _, _ = _place()
        copies, arrive = [], []
        for t in range(n):
            rh = outs[t].shape[1] // 2
            mine = outs[t].at[layers[t], pl.ds(c * rh, rh)]
            landing = outs[t].at[layers[t], pl.ds((1 - c) * rh, rh)]
            copies.append(_remote(mine, mine, send.at[t], recv.at[t], (x, y, 1 - c)))
            arrive.append(_remote(landing, landing, send.at[t], recv.at[t], (x, y, 1 - c)))
        for cp in copies:
            cp.start()
        for cp in arrive:
            cp.wait_recv()
        for cp in copies:
            cp.wait_send()

    return pl.pallas_call(
        body, name=name, in_specs=[_ANY] * n, out_specs=[_ANY] * n,
        out_shape=[jax.ShapeDtypeStruct(b.shape, b.dtype) for b in totals],
        scratch_shapes=[pltpu.SemaphoreType.DMA((n,)), pltpu.SemaphoreType.DMA((n,))],
        input_output_aliases={t: t for t in range(n)},
    )(*totals)


def _pair_sum(g, got, where):
    _, r, cols = g.shape
    rh = r // 2
    tm = _row_tile(rh)
    per, perh = r // tm, rh // tm
    out = _placed_map("rs_add_pair", lambda u, w: u.astype(_F32) + w.astype(_F32), where,
                      [(_rows2d(g), lambda i, wh: (i // perh) * per + wh[0] * perh + i % perh),
                       (_rows2d(got), lambda i, wh: i)],
                      4 * rh, cols, _BF, lambda i, wh: i, 4 * perh, tm)
    return out.reshape(4, rh, cols)


def _sum_of_four(pair, land, where, total, layer, n_layers):
    _, rh, cols = pair.shape
    r = 2 * rh
    tm = _row_tile(rh)
    per, perh = r // tm, rh // tm

    def add_four(own, others):
        return ((own.astype(_F32) + others[0].astype(_F32)) + others[1].astype(_F32)) + others[2].astype(_F32)

    out = _placed_map("rs_add_four", add_four, where,
                      [(_rows2d(pair), lambda i, wh: wh[1] * perh + i), (land, lambda i, wh: i)],
                      n_layers * r, cols, _F32, lambda i, wh: layer * per + wh[0] * perh + i, perh, tm,
                      into=None if total is None else _rows2d(total))
    return out.reshape(n_layers, r, cols)


def _all_reduce_small(pack):
    rows, cols = pack.shape

    def body(p_ref, o_ref, land, send, recv):
        x, y, c = lax.axis_index("x"), lax.axis_index("y"), lax.axis_index("c")
        me = 4 * x + 2 * y + c
        land[me] = p_ref[...]
        copies = []
        for k in range(1, 8):
            bx, by, bc = (k >> 2) & 1, (k >> 1) & 1, k & 1
            peer = (1 - x if bx else x, 1 - y if by else y, 1 - c if bc else c)
            cp = _remote(p_ref, land.at[me], send.at[k - 1], recv.at[k - 1], peer)
            cp.start()
            copies.append((cp, 4 * peer[0] + 2 * peer[1] + peer[2]))
        for k, (cp, pid) in enumerate(copies):
            _remote(p_ref, land.at[pid], send.at[k], recv.at[k], (x, y, c)).wait_recv()
        for cp, _ in copies:
            cp.wait_send()
        acc = land[0]
        for d in range(1, 8):
            acc = acc + land[d]
        o_ref[...] = acc

    vm = pl.BlockSpec(memory_space=pltpu.VMEM)
    return pl.pallas_call(
        body, name="all_reduce_small", in_specs=[vm], out_specs=vm,
        out_shape=jax.ShapeDtypeStruct((rows, cols), _F32),
        scratch_shapes=[pltpu.VMEM((8, rows, cols), _F32), pltpu.SemaphoreType.DMA((7,)), pltpu.SemaphoreType.DMA((7,))],
    )(pack)


def _adamw_math(w, g, m, v):
    m2 = _ADAM_B1 * m + (1.0 - _ADAM_B1) * g
    v2 = _ADAM_B2 * v + (1.0 - _ADAM_B2) * (g * g)
    m_hat = m2 / (1.0 - _ADAM_B1 ** _ADAM_STEP)
    v_hat = v2 / (1.0 - _ADAM_B2 ** _ADAM_STEP)
    delta = -_ADAM_LR * (m_hat / (jnp.sqrt(v_hat) + _ADAM_EPS) + _ADAM_WD * w)
    return delta, m2, v2


def _adamw(w, g, m, v):
    shape = w.shape
    w2, g2, m2, v2 = (_rows2d(t) for t in (w, g, m, v))
    rows, cols = w2.shape
    outs = _rowmap("adamw", _adamw_math, [(t, cols, 0) for t in (w2, g2, m2, v2)], [], [(cols, cols, 0, _F32)] * 3,
                   rows=rows, tm=_row_tile(rows))
    return tuple(o.reshape(shape) for o in outs)


def kernel(x, positions, ffn1_norm, ffn1_w_in, ffn1_w_out, mix_norm, ffn2_norm, ffn2_w_in, ffn2_w_out, hyb_w_in, hyb_dw_w, hyb_dw_b, hyb_ln_g, hyb_ln_b, hyb_w_out, gdn_w_in, gdn_conv_w, gdn_A_log, gdn_dt_bias, gdn_norm_g, gdn_w_out, final_norm, loss_target, m_ffn1_norm, m_ffn1_w_in, m_ffn1_w_out, m_mix_norm, m_ffn2_norm, m_ffn2_w_in, m_ffn2_w_out, m_hyb_w_in, m_hyb_dw_w, m_hyb_dw_b, m_hyb_ln_g, m_hyb_ln_b, m_hyb_w_out, m_gdn_w_in, m_gdn_conv_w, m_gdn_A_log, m_gdn_dt_bias, m_gdn_norm_g, m_gdn_w_out, m_final_norm, v_ffn1_norm, v_ffn1_w_in, v_ffn1_w_out, v_mix_norm, v_ffn2_norm, v_ffn2_w_in, v_ffn2_w_out, v_hyb_w_in, v_hyb_dw_w, v_hyb_dw_b, v_hyb_ln_g, v_hyb_ln_b, v_hyb_w_out, v_gdn_w_in, v_gdn_conv_w, v_gdn_A_log, v_gdn_dt_bias, v_gdn_norm_g, v_gdn_w_out, v_final_norm):
    w = dict(ffn1_norm=ffn1_norm, ffn1_w_in=ffn1_w_in, ffn1_w_out=ffn1_w_out, mix_norm=mix_norm, ffn2_norm=ffn2_norm, ffn2_w_in=ffn2_w_in, ffn2_w_out=ffn2_w_out, hyb_w_in=hyb_w_in, hyb_dw_w=hyb_dw_w, hyb_dw_b=hyb_dw_b, hyb_ln_g=hyb_ln_g, hyb_ln_b=hyb_ln_b, hyb_w_out=hyb_w_out, gdn_w_in=gdn_w_in, gdn_conv_w=gdn_conv_w, gdn_A_log=gdn_A_log, gdn_dt_bias=gdn_dt_bias, gdn_norm_g=gdn_norm_g, gdn_w_out=gdn_w_out, final_norm=final_norm)
    m = dict(ffn1_norm=m_ffn1_norm, ffn1_w_in=m_ffn1_w_in, ffn1_w_out=m_ffn1_w_out, mix_norm=m_mix_norm, ffn2_norm=m_ffn2_norm, ffn2_w_in=m_ffn2_w_in, ffn2_w_out=m_ffn2_w_out, hyb_w_in=m_hyb_w_in, hyb_dw_w=m_hyb_dw_w, hyb_dw_b=m_hyb_dw_b, hyb_ln_g=m_hyb_ln_g, hyb_ln_b=m_hyb_ln_b, hyb_w_out=m_hyb_w_out, gdn_w_in=m_gdn_w_in, gdn_conv_w=m_gdn_conv_w, gdn_A_log=m_gdn_A_log, gdn_dt_bias=m_gdn_dt_bias, gdn_norm_g=m_gdn_norm_g, gdn_w_out=m_gdn_w_out, final_norm=m_final_norm)
    v = dict(ffn1_norm=v_ffn1_norm, ffn1_w_in=v_ffn1_w_in, ffn1_w_out=v_ffn1_w_out, mix_norm=v_mix_norm, ffn2_norm=v_ffn2_norm, ffn2_w_in=v_ffn2_w_in, ffn2_w_out=v_ffn2_w_out, hyb_w_in=v_hyb_w_in, hyb_dw_w=v_hyb_dw_w, hyb_dw_b=v_hyb_dw_b, hyb_ln_g=v_hyb_ln_g, hyb_ln_b=v_hyb_ln_b, hyb_w_out=v_hyb_w_out, gdn_w_in=v_gdn_w_in, gdn_conv_w=v_gdn_conv_w, gdn_A_log=v_gdn_A_log, gdn_dt_bias=v_gdn_dt_bias, gdn_norm_g=v_gdn_norm_g, gdn_w_out=v_gdn_w_out, final_norm=v_final_norm)
    return _train_step(x, positions, loss_target, w, m, v)


_WEIGHT_ORDER = ("ffn1_norm", "ffn1_w_in", "ffn1_w_out", "mix_norm", "ffn2_norm", "ffn2_w_in", "ffn2_w_out",
                 "hyb_w_in", "hyb_dw_w", "hyb_dw_b", "hyb_ln_g", "hyb_ln_b", "hyb_w_out", "gdn_w_in", "gdn_conv_w",
                 "gdn_A_log", "gdn_dt_bias", "gdn_norm_g", "gdn_w_out", "final_norm")
_BIG = ("ffn1_w_in", "ffn1_w_out", "ffn2_w_in", "ffn2_w_out", "hyb_w_in", "hyb_w_out", "gdn_w_in", "gdn_w_out")
_PACK_COLS = 1024


def _lane16(vec):
    return jnp.zeros((1, 128), _F32).at[0, _GDN_HEADS:2 * _GDN_HEADS].set(vec)


def _pack_rows(parts):
    rows, where, at = [], [], 0
    for p in parts:
        flat = p.reshape(-1).astype(_F32)
        n = -(-flat.shape[0] // (8 * _PACK_COLS)) * 8
        rows.append(jnp.pad(flat, (0, n * _PACK_COLS - flat.shape[0])).reshape(n, _PACK_COLS))
        where.append((at, n, flat.shape[0], p.shape))
        at += n
    return jnp.concatenate(rows, axis=0), where


def _unpack_rows(pack, where):
    return [pack[at:at + n].reshape(-1)[:size].reshape(shape) for at, n, size, shape in where]


def _train_step(x, positions, loss_target, w, m, v):
    n_layers = w["ffn1_norm"].shape[0]
    n_hyb, n_gdn = w["hyb_w_in"].shape[0], w["gdn_w_in"].shape[0]
    d_model = x.shape[-1]
    xh, target = x[0], loss_target[0]
    chip = 2 * lax.axis_index("x") + lax.axis_index("y")
    where = jnp.stack([lax.axis_index("c"), chip]).astype(jnp.int32)

    groups = (("ffn1_w_in", "ffn1_w_out"), ("hyb_w_in", "hyb_w_out"), ("ffn2_w_in", "ffn2_w_out"),
              ("gdn_w_in", "gdn_w_out"))
    pending, token = [], where
    for gi, names in enumerate(groups):
        send, recv, thru, token = _gather_start(f"gather_start_{gi}", [_cast_slab(n, w[n], where) for n in names], token)
        pending.append((send, recv, thru))
    gdn_cols = w["gdn_w_in"].shape[-1]
    n_qkvz = 4 * _GDN_HEADS * _GDN_DIM
    n_ba = 4 * gdn_cols - n_qkvz
    wv = {}

    def arrive(gi, after):
        send, recv, thru = pending[gi]
        a, b = _gather_forward(f"gather_pass_{gi}", _gather_wait(f"gather_wait_{gi}", send, recv, thru, after))
        kind = groups[gi][0]
        if kind == "gdn_w_in":
            gdn_in = jnp.transpose(a, (0, 2, 1, 3)).reshape(n_gdn, d_model, 4 * gdn_cols)
            wv["gdn_qkvz"] = gdn_in[:, :, :n_qkvz]
            wv["gdn_ba"] = jnp.pad(gdn_in[:, :, n_qkvz:], ((0, 0), (0, 0), (0, 128 - n_ba)))
        else:
            wv[kind] = a
        wv[groups[gi][1]] = b.reshape(b.shape[0], -1, d_model)

    def own_columns(shard):
        cols = shard.shape[-1]
        zeros = jnp.zeros(shard.shape[:-1] + (4 * cols,), _F32)
        return lax.dynamic_update_slice_in_dim(zeros, shard, chip * cols, axis=2)

    small_pack, small_where = _pack_rows([own_columns(w["hyb_dw_w"]), own_columns(w["gdn_conv_w"])])
    dw_w, conv_w = _unpack_rows(0.5 * _all_reduce_small(small_pack), small_where)
    dw_w = jnp.pad(dw_w, ((0, 0), (0, _HALO - _CONV_WIDTH), (0, 0)))
    tabs = _rotary_tables(positions)

    def mixer_args(layer):
        i = layer // 2
        if layer % 2 == 0:
            return (wv["hyb_w_in"], wv["hyb_w_out"], dw_w[i], w["hyb_dw_b"][i][None], w["hyb_ln_g"][i][None],
                    w["hyb_ln_b"][i][None], tabs, i)
        return (wv["gdn_qkvz"], wv["gdn_ba"], wv["gdn_w_out"], conv_w[i], _lane16(w["gdn_A_log"][i]),
                _lane16(w["gdn_dt_bias"][i]), w["gdn_norm_g"][i][None], i)

    h = xh
    saved = []
    arrive(0, token)
    for layer in range(n_layers):
        h, s1 = _ffn_fwd(h, w["ffn1_norm"][layer][None], wv["ffn1_w_in"], wv["ffn1_w_out"], layer)
        gamma = w["mix_norm"][layer][None]
        if layer % 2 == 0:
            if layer == 0:
                arrive(1, h)
            h, sm = _hyb_fwd(h, gamma, *mixer_args(layer))
        else:
            if layer == 1:
                arrive(3, h)
            h, sm = _gdn_fwd(h, gamma, *mixer_args(layer))
        if layer == 0:
            arrive(2, h)
        h, s2 = _ffn_fwd(h, w["ffn2_norm"][layer][None], wv["ffn2_w_in"], wv["ffn2_w_out"], layer)
        saved.append((s1, sm, s2))
    dh, loss_row, d_final = _loss_head(h, w["final_norm"][None], target)

    d_norm = {k: [None] * n_layers for k in ("ffn1_norm", "mix_norm", "ffn2_norm")}
    d_hyb = [None] * n_hyb
    d_gdn = [None] * n_gdn
    grads = {name: None for name in _BIG}
    n_of = {name: w[name].shape[0] for name in _BIG}

    def rs_begin(layer, full, token):
        names = list(full)
        got = _rs_sibling(f"rs_sib_{layer}", [full[n] for n in names])
        sums = [_pair_sum(full[n], g, where) for n, g in zip(names, got)]
        send, recv, sums, lands, token = _rs_chips_start(f"rs_chips_start_{layer}", sums, token)
        return (layer, names, send, recv, sums, lands), token

    def rs_end(state, after):
        layer, names, send, recv, sums, lands = state
        sums, lands = _rs_chips_wait(f"rs_chips_wait_{layer}", send, recv, sums, lands, after)
        index = [layer if n.startswith("ffn") else layer // 2 for n in names]
        for n, li, pair, land in zip(names, index, sums, lands):
            grads[n] = _sum_of_four(pair, land, where, grads[n], li, n_of[n])
        joined = _rs_join(f"rs_join_{layer}", [grads[n] for n in names], index)
        grads.update(zip(names, joined))

    in_flight = None
    for layer in reversed(range(n_layers)):
        s1, sm, s2 = saved[layer]
        full = {}
        dh, d_norm["ffn2_norm"][layer], full["ffn2_w_in"], full["ffn2_w_out"] = _ffn_bwd(
            dh, s2, w["ffn2_norm"][layer][None], wv["ffn2_w_in"], wv["ffn2_w_out"], layer)
        gamma = w["mix_norm"][layer][None]
        if layer % 2 == 0:
            dh, d_norm["mix_norm"][layer], d_hyb[layer // 2], full["hyb_w_in"], full["hyb_w_out"] = _hyb_bwd(
                dh, sm, gamma, *mixer_args(layer))
        else:
            dh, d_norm["mix_norm"][layer], d_gdn[layer // 2], full["gdn_w_in"], full["gdn_w_out"] = _gdn_bwd(
                dh, sm, gamma, *mixer_args(layer), n_ba)
        dh, d_norm["ffn1_norm"][layer], full["ffn1_w_in"], full["ffn1_w_out"] = _ffn_bwd(
            dh, s1, w["ffn1_norm"][layer][None], wv["ffn1_w_in"], wv["ffn1_w_out"], layer)
        if in_flight is not None:
            rs_end(in_flight, dh)
        in_flight, token = rs_begin(layer, full, token)
        dh, token = lax.optimization_barrier((dh, token))
    rs_end(in_flight, token)
    grad_x = dh[None]

    stack = lambda rows: jnp.concatenate(rows, axis=0)
    misc = jnp.concatenate([stack([g[1] for g in d_gdn])[:, _GDN_HEADS:2 * _GDN_HEADS].reshape(-1),
                            stack([g[2] for g in d_gdn])[:, _GDN_HEADS:2 * _GDN_HEADS].reshape(-1),
                            stack([g[3] for g in d_gdn]).reshape(-1), loss_row[0, :1]])
    n_a = n_gdn * _GDN_HEADS
    parts = [stack(d_norm["ffn1_norm"]), stack(d_norm["mix_norm"]), stack(d_norm["ffn2_norm"]), d_final,
             stack([g[1] for g in d_hyb]), stack([g[2] for g in d_hyb]), stack([g[3] for g in d_hyb]), misc,
             jnp.stack([g[0][:_CONV_WIDTH] for g in d_hyb]), jnp.stack([g[0] for g in d_gdn])]
    pack, where = _pack_rows(parts)
    red = _unpack_rows(_all_reduce_small(pack), where)
    misc = red[7]
    cw = w["hyb_dw_w"].shape[-1]
    gw = w["gdn_conv_w"].shape[-1]
    grads.update(
        ffn1_norm=red[0], mix_norm=red[1], ffn2_norm=red[2], final_norm=red[3].reshape(-1),
        hyb_dw_b=red[4], hyb_ln_g=red[5], hyb_ln_b=red[6],
        gdn_A_log=misc[:n_a].reshape(n_gdn, _GDN_HEADS), gdn_dt_bias=misc[n_a:2 * n_a].reshape(n_gdn, _GDN_HEADS),
        gdn_norm_g=misc[2 * n_a:2 * n_a + n_gdn * _GDN_DIM].reshape(n_gdn, _GDN_DIM),
        hyb_dw_w=lax.dynamic_slice_in_dim(red[8], chip * cw, cw, axis=2),
        gdn_conv_w=lax.dynamic_slice_in_dim(red[9], chip * gw, gw, axis=2))
    loss = misc[2 * n_a + n_gdn * _GDN_DIM]

    delta, new_m, new_v = {}, {}, {}
    for name in _BIG:
        delta[name], new_m[name], new_v[name] = _adamw(w[name], grads[name], m[name], v[name])
    small = [n for n in _WEIGHT_ORDER if n not in _BIG]
    packs = [_pack_rows([src[n] for n in small]) for src in (w, grads, m, v)]
    outs = _adamw(*[p[0] for p in packs])
    for res, o in zip((delta, new_m, new_v), outs):
        res.update(dict(zip(small, _unpack_rows(o, packs[0][1]))))
    return (loss, grad_x, *[grads[n] for n in _WEIGHT_ORDER], *[delta[n] for n in _WEIGHT_ORDER],
            *[new_m[n] for n in _WEIGHT_ORDER], *[new_v[n] for n in _WEIGHT_ORDER])
```

```python
import functools

import jax
import jax.numpy as jnp
from jax import lax
from jax.experimental import pallas as pl
from jax.experimental.pallas import tpu as pltpu

_BF = jnp.bfloat16
_F32 = jnp.float32
_NORM_EPS = 1e-6
_VMEM_LIMIT_BYTES = 56 * 1024 * 1024

_A_HEADS = 8
_A_HEAD_DIM = 64
_WIN_BLOCK = 128
_DILATIONS = (1, 4, 16)
_CONV_WIDTH = 31
_GDN_HEADS = 8
_GDN_DIM = 128
_GDN_CHUNK = 64
_GDN_SHORT_CONV = 4

_ADAM_LR = 0.001
_ADAM_B1 = 0.9
_ADAM_B2 = 0.999
_ADAM_EPS = 1e-08
_ADAM_WD = 0.01
_ADAM_STEP = 10


def _params(sem):
    return pltpu.CompilerParams(dimension_semantics=sem, vmem_limit_bytes=_VMEM_LIMIT_BYTES)


def _dot(a, b, dn=(((1,), (0,)), ((), ()))):
    return lax.dot_general(a.astype(_BF), b.astype(_BF), dn, preferred_element_type=_F32)


def _dot_nt(a, b):
    return _dot(a, b, (((1,), (1,)), ((), ())))


def _dot_tn(a, b):
    return _dot(a, b, (((0,), (0,)), ((), ())))


_NN = (((1,), (0,)), ((), ()))
_NT = (((1,), (1,)), ((), ()))
_TN = (((0,), (0,)), ((), ()))
_ANY = pl.BlockSpec(memory_space=pl.ANY)


def _mm(name, a, b, *, grid, a_spec, b_spec, dn, out_shape, out_spec, red=None, acc_shape=None,
        pre_a=None, pre_b=None, epi=None, extras=(), extra_specs=()):
    n_ex = len(extras)
    n_in = 2 + n_ex
    nk = grid[red] if red is not None else 1

    def body(*refs):
        a_ref, b_ref = refs[0], refs[1]
        ex = refs[2:2 + n_ex]
        o_ref = refs[n_in]
        av, bv = a_ref[...], b_ref[...]
        if pre_a is not None:
            av = pre_a(av)
        if pre_b is not None:
            bv = pre_b(bv)
        part = lax.dot_general(av.astype(_BF), bv.astype(_BF), dn, preferred_element_type=_F32)

        def finish(acc):
            res = epi(acc, *[e[...] for e in ex]) if epi is not None else acc
            if isinstance(res, tuple):
                for idx, val in enumerate(res):
                    o_ref[idx] = val.astype(o_ref.dtype)
            else:
                o_ref[...] = res.astype(o_ref.dtype)

        if red is None:
            finish(part)
        else:
            acc_ref = refs[n_in + 1]
            k = pl.program_id(red)

            @pl.when(k == 0)
            def _():
                acc_ref[...] = part

            @pl.when(k > 0)
            def _():
                acc_ref[...] += part

            @pl.when(k == nk - 1)
            def _():
                finish(acc_ref[...])

    return pl.pallas_call(
        body, name=name, grid=grid, in_specs=[a_spec, b_spec, *extra_specs], out_specs=out_spec, out_shape=out_shape,
        scratch_shapes=[pltpu.VMEM(acc_shape, _F32)] if red is not None else [],
        compiler_params=_params(("arbitrary",) * len(grid)),
    )(a, b, *extras)


def _rowmap(name, body, tiled, params, tile_outs, param_outs=(), *, rows, tm, ncol=1, col_outer=False,
            into=None):
    nt, npar, nto, npo = len(tiled), len(params), len(tile_outs), len(param_outs)
    assert rows % tm == 0, (name, rows, tm)
    nrow = rows // tm
    grid = (ncol, nrow) if col_outer else (nrow, ncol)

    def ij(g):
        return (g[1], g[0]) if col_outer else (g[0], g[1])

    def cidx(col, j):
        return col(j) if callable(col) else col

    in_specs, operands = [], []
    for arr, width, col in tiled:
        operands.append(arr)
        if arr.ndim == 3:
            in_specs.append(pl.BlockSpec((arr.shape[0], tm, width),
                                         lambda *g, col=col: (0, ij(g)[0], cidx(col, ij(g)[1]))))
        else:
            in_specs.append(pl.BlockSpec((tm, width), lambda *g, col=col: (ij(g)[0], cidx(col, ij(g)[1]))))
    for p in params:
        if isinstance(p, tuple):
            arr, width, col = p
            operands.append(arr)
            in_specs.append(pl.BlockSpec((arr.shape[0], width), lambda *g, col=col: (0, cidx(col, ij(g)[1]))))
        else:
            operands.append(p)
            in_specs.append(pl.BlockSpec(p.shape, lambda *g, nd=p.ndim: (0,) * nd))
    out_shape, out_specs = [], []
    for total, width, col, dt in tile_outs:
        out_shape.append(jax.ShapeDtypeStruct((rows, total), dt))
        out_specs.append(pl.BlockSpec((tm, width), lambda *g, col=col: (ij(g)[0], cidx(col, ij(g)[1]))))
    for shp in param_outs:
        out_shape.append(jax.ShapeDtypeStruct(shp, _F32))
        out_specs.append(pl.BlockSpec(shp, lambda *g, nd=len(shp): (0,) * nd))
    n_in = nt + npar
    aliases = {}
    if into is not None:
        operands.append(into[0])
        in_specs.append(_ANY)
        aliases = {n_in: into[1]}
        n_in += 1

    def kern(*refs):
        vals = [r[...] for r in refs[:nt + npar]]
        touts = refs[n_in:n_in + nto]
        pouts = refs[n_in + nto:]
        res = body(*vals)
        if not isinstance(res, (tuple, list)):
            res = (res,)
        for r, val in zip(touts, res[:nto]):
            r[...] = val.astype(r.dtype)
        if npo:
            @pl.when((pl.program_id(0) == 0) & (pl.program_id(1) == 0))
            def _():
                for r in pouts:
                    r[...] = jnp.zeros_like(r)

            for r, val in zip(pouts, res[nto:]):
                r[...] += val

    return pl.pallas_call(
        kern, name=name, grid=grid, in_specs=in_specs, out_specs=out_specs, out_shape=out_shape,
        input_output_aliases=aliases, compiler_params=_params(("arbitrary", "arbitrary")),
    )(*operands)


def _vjp_body(fn, n_in, diff, n_ct):
    def body(*vals):
        ins, cts = vals[:n_in], vals[n_in:n_in + n_ct]
        outs, pull = jax.vjp(fn, *ins)
        single = not isinstance(outs, (tuple, list))
        if single:
            grads = pull(cts[0].astype(outs.dtype))
        else:
            grads = pull(tuple(c.astype(o.dtype) for c, o in zip(cts, outs)))
        return tuple(grads[i] for i in diff)
    return body


_TM = 512
_TM_WIDE = 1024


def _rms(h, gamma):
    hf = h.astype(_F32)
    return hf * lax.rsqrt(jnp.mean(hf * hf, axis=-1, keepdims=True) + _NORM_EPS) * gamma


def _rmsnorm_fwd(h, gamma):
    s, d = h.shape
    return _rowmap("rmsnorm_fwd", lambda hv, gv: _rms(hv, gv).astype(_BF), [(h, d, 0)], [gamma],
                   [(d, d, 0, _BF)], rows=s, tm=_TM)[0]


def _rmsnorm_bwd(h, gamma, dxn, dh_out):
    s, d = h.shape

    def body(hv, dxv, dhv, gv):
        _, pull = jax.vjp(_rms, hv, gv)
        dh, dg = pull(dxv.astype(_F32))
        return dhv + dh, dg

    return _rowmap("rmsnorm_bwd", body, [(h, d, 0), (dxn, d, 0), (dh_out, d, 0)], [gamma],
                   [(d, d, 0, _F32)], [(1, d)], rows=s, tm=_TM)


def _silu(x):
    return x * jax.nn.sigmoid(x)


def _ffn_up(xn, w_in, layer):
    s, d = xn.shape
    ns = w_in.shape[-1]

    def body(x_ref, wg_ref, wu_ref, g_ref, u_ref, a_ref):
        xv = x_ref[...]
        g = jnp.dot(xv, wg_ref[...], preferred_element_type=_F32)
        u = jnp.dot(xv, wu_ref[...], preferred_element_type=_F32)
        g_ref[...] = g.astype(_BF)
        u_ref[...] = u.astype(_BF)
        a_ref[...] = (_silu(g) * u).astype(_BF)

    out = pl.BlockSpec((_TM, ns), lambda j, i: (i, j))
    sds = jax.ShapeDtypeStruct((s, 2 * ns), _BF)
    return pl.pallas_call(
        body, name="ffn_up", grid=(2, s // _TM),
        in_specs=[pl.BlockSpec((_TM, d), lambda j, i: (i, 0)),
                  pl.BlockSpec((None, None, d, ns), lambda j, i: (layer, j, 0, 0)),
                  pl.BlockSpec((None, None, d, ns), lambda j, i: (layer, j + 2, 0, 0))],
        out_specs=[out, out, out], out_shape=[sds, sds, sds],
        compiler_params=_params(("arbitrary", "arbitrary")),
    )(xn, w_in, w_in)


def _proj_residual(name, y, w, layer, h, scale):
    s, k = y.shape
    d = w.shape[-1]
    return _mm(name, y, w, grid=(s // _TM,), a_spec=pl.BlockSpec((_TM, k), lambda i: (i, 0)),
               b_spec=pl.BlockSpec((None, k, d), lambda i: (layer, 0, 0)), dn=_NN,
               out_shape=jax.ShapeDtypeStruct((s, d), _F32), out_spec=pl.BlockSpec((_TM, d), lambda i: (i, 0)),
               epi=lambda acc, hv: hv + scale * acc, extras=(h,),
               extra_specs=(pl.BlockSpec((_TM, d), lambda i: (i, 0)),))


def _ffn_dact(dh, w_out, layer, g, u):
    s, d = dh.shape
    f = g.shape[1]
    ns = f // 2

    def epi(da, gv, uv):
        gf, uf = gv.astype(_F32), uv.astype(_F32)
        sg = jax.nn.sigmoid(gf)
        return da * uf * (sg * (1.0 + gf * (1.0 - sg))), da * gf * sg

    gu = pl.BlockSpec((_TM, ns), lambda n, i: (i, n))
    return _mm("ffn_dact", dh, w_out, grid=(2, s // _TM), a_spec=pl.BlockSpec((_TM, d), lambda n, i: (i, 0)),
               b_spec=pl.BlockSpec((None, ns, d), lambda n, i: (layer, n, 0)), dn=_NT,
               out_shape=jax.ShapeDtypeStruct((2, s, f), _BF), out_spec=pl.BlockSpec((2, _TM, ns), lambda n, i: (0, i, n)),
               pre_a=lambda v: 0.5 * v, epi=epi, extras=(g, u), extra_specs=(gu, gu))


def _wgrad_rows(name, y, dh, scale):
    s, k = y.shape
    d = dh.shape[1]
    out = _mm(name, y, dh, grid=(s // _TM,), a_spec=pl.BlockSpec((_TM, k), lambda i: (i, 0)),
              b_spec=pl.BlockSpec((_TM, d), lambda i: (i, 0)), dn=_TN, red=0, acc_shape=(k, d),
              out_shape=jax.ShapeDtypeStruct((k, d), _BF), out_spec=pl.BlockSpec((k, d), lambda i: (0, 0)),
              pre_b=(lambda v: scale * v) if scale != 1.0 else None)
    return out.reshape(4, k // 4, d)


def _ffn_dxn(dz, w_in, layer):
    _, s, f = dz.shape
    d, ns = w_in.shape[2], w_in.shape[3]
    tm = _TM_WIDE
    return _mm("ffn_dxn", dz, w_in, grid=(s // tm, 4),
               a_spec=pl.BlockSpec((None, tm, ns), lambda i, j: (j // 2, i, j % 2)),
               b_spec=pl.BlockSpec((None, None, d, ns), lambda i, j: (layer, j, 0, 0)), dn=_NT, red=1,
               acc_shape=(tm, d), out_shape=jax.ShapeDtypeStruct((s, d), _F32),
               out_spec=pl.BlockSpec((tm, d), lambda i, j: (i, 0)))


def _ffn_dwin(xn, dz):
    s, d = xn.shape
    ns = dz.shape[2] // 2
    tm = _TM_WIDE
    return _mm("ffn_dwin", xn, dz, grid=(4, s // tm), a_spec=pl.BlockSpec((tm, d), lambda j, i: (i, 0)),
               b_spec=pl.BlockSpec((None, tm, ns), lambda j, i: (j // 2, i, j % 2)), dn=_TN, red=1,
               acc_shape=(d, ns), out_shape=jax.ShapeDtypeStruct((4, d, ns), _BF),
               out_spec=pl.BlockSpec((None, d, ns), lambda j, i: (j, 0, 0)))


def _ffn_fwd(h, gamma, w_in, w_out, layer):
    xn = _rmsnorm_fwd(h, gamma)
    g, u, act = _ffn_up(xn, w_in, layer)
    h_new = _proj_residual("ffn_down", act, w_out, layer, h, 0.5)
    return h_new, (h, xn, g, u, act)


def _ffn_bwd(dh, saved, gamma, w_in, w_out, layer):
    h, xn, g, u, act = saved
    dz = _ffn_dact(dh, w_out, layer, g, u)
    d_w_out = _wgrad_rows("ffn_dwout", act, dh, 0.5)
    dxn = _ffn_dxn(dz, w_in, layer)
    d_w_in = _ffn_dwin(xn, dz)
    dh_in, dgamma = _rmsnorm_bwd(h, gamma, dxn, dh)
    return dh_in, dgamma, d_w_in, d_w_out


def _loss_head(h, gamma, target):
    s, d = h.shape

    def body(hv, tv, gv):
        y, pull = jax.vjp(_rms, hv, gv)
        err = y - tv
        loss = 0.5 * jnp.sum(jnp.mean(err * err, axis=-1, keepdims=True), axis=0, keepdims=True)
        dh, dg = pull(err * (1.0 / d))
        return dh, jnp.broadcast_to(loss, (1, 128)), dg

    return _rowmap("loss_head", body, [(h, d, 0), (target, d, 0)], [gamma], [(d, d, 0, _F32)],
                   [(1, 128), (1, d)], rows=s, tm=_TM)


_ROT = _A_HEAD_DIM // 4
_ROPE_THETA = 500000.0
_A_WIDTH = _A_HEADS * _A_HEAD_DIM
_NEG = -1e30


def _rotary_tables(positions):
    inv_freq = jnp.power(jnp.float32(_ROPE_THETA), -jnp.arange(0, _ROT, 2, dtype=_F32) / _ROT)
    ang = positions.reshape(-1, 1).astype(_F32) * inv_freq
    cos, sin = jnp.cos(ang), jnp.sin(ang)
    s = ang.shape[0]
    pad = jnp.zeros((s, _A_HEAD_DIM - _ROT), _F32)
    c_head = jnp.concatenate([cos, cos, pad + 1.0], axis=1)
    s_head = jnp.concatenate([-sin, sin, pad], axis=1)
    c2, s2 = jnp.tile(c_head, (1, 2)), jnp.tile(s_head, (1, 2))
    scale = _A_HEAD_DIM ** -0.5
    tab_c = jnp.stack([scale * c2, c2, jnp.ones_like(c2)])
    tab_s = jnp.stack([scale * s2, s2, jnp.zeros_like(s2)])
    return tab_c, tab_s


def _pair_swap(z):
    lane = lax.broadcasted_iota(jnp.int32, z.shape, 1) % _A_HEAD_DIM
    half = _ROT // 2
    up = jnp.where(lane < _ROT, pltpu.roll(z, half, axis=1), 0.0)
    return jnp.where(lane < half, pltpu.roll(z, 128 - half, axis=1), up)


def _rot_apply(x, c, s, transpose):
    outs = []
    for k in range(x.shape[1] // 128):
        xs = x[:, k * 128:(k + 1) * 128]
        outs.append(c * xs + (_pair_swap(s * xs) if transpose else s * _pair_swap(xs)))
    return jnp.concatenate(outs, axis=1)


def _qkv_prep(proj, tab_c, tab_s):
    s = proj.shape[0]
    w = _A_WIDTH

    def body(x_ref, c_ref, s_ref, o_ref):
        o_ref[...] = _rot_apply(x_ref[...], c_ref[...], s_ref[...], False).astype(_BF)

    tab = pl.BlockSpec((None, _TM, 128), lambda i, j: (j, i, 0))
    blk = pl.BlockSpec((_TM, w), lambda i, j: (i, j))
    return pl.pallas_call(
        body, name="qkv_prep", grid=(s // _TM, 3), in_specs=[blk, tab, tab], out_specs=blk,
        out_shape=jax.ShapeDtypeStruct((s, 3 * w), _BF), compiler_params=_params(("arbitrary", "arbitrary")),
    )(proj, tab_c, tab_s)


def _head_masks():
    lane = lax.broadcasted_iota(jnp.int32, (_WIN_BLOCK, 128), 1)
    return [lane // _A_HEAD_DIM == hh for hh in range(2)]


def _band_masks(first_block):
    qi = lax.broadcasted_iota(jnp.int32, (_WIN_BLOCK, _WIN_BLOCK), 0)
    kj = lax.broadcasted_iota(jnp.int32, (_WIN_BLOCK, _WIN_BLOCK), 1)
    return (kj >= qi) & jnp.logical_not(first_block), kj <= qi


def _attn_fwd(qkv, dil):
    s = qkv.shape[0]
    w = _A_WIDTH
    length = s // dil
    nb = length // _WIN_BLOCK
    view = qkv.reshape(length, dil * 3 * w)

    def body(q_ref, kp_ref, kc_ref, vp_ref, vc_ref, o_ref, l_ref):
        allow_p, allow_c = _band_masks(pl.program_id(1) == 0)
        hm = _head_masks()
        for sl in range(w // 128):
            cols = slice(sl * 128, (sl + 1) * 128)
            q, kp, kc, vp, vc = q_ref[:, cols], kp_ref[:, cols], kc_ref[:, cols], vp_ref[:, cols], vc_ref[:, cols]
            o_s = jnp.zeros((_WIN_BLOCK, 128), _F32)
            l_s = jnp.zeros((_WIN_BLOCK, 128), _F32)
            for hh in range(2):
                qh = jnp.where(hm[hh], q, jnp.zeros_like(q))
                sp = jnp.where(allow_p, _dot_nt(qh, kp), _NEG)
                sc = jnp.where(allow_c, _dot_nt(qh, kc), _NEG)
                m = jnp.maximum(jnp.max(sp, axis=1, keepdims=True), jnp.max(sc, axis=1, keepdims=True))
                pp, pc = jnp.exp(sp - m), jnp.exp(sc - m)
                den = jnp.sum(pp, axis=1, keepdims=True) + jnp.sum(pc, axis=1, keepdims=True)
                oh = (_dot(pp, vp) + _dot(pc, vc)) / den
                o_s = jnp.where(hm[hh], oh, o_s)
                l_s = jnp.where(hm[hh], m + jnp.log(den), l_s)
            o_ref[:, cols] = o_s
            l_ref[:, cols] = l_s

    def at(col, prev):
        if prev:
            return pl.BlockSpec((_WIN_BLOCK, w), lambda r, n: (jnp.maximum(n - 1, 0), 3 * r + col))
        return pl.BlockSpec((_WIN_BLOCK, w), lambda r, n: (n, 3 * r + col))

    out = pl.BlockSpec((_WIN_BLOCK, w), lambda r, n: (n, r))
    sds = jax.ShapeDtypeStruct((length, dil * w), _F32)
    o, lse = pl.pallas_call(
        body, name=f"attn_fwd_d{dil}", grid=(dil, nb),
        in_specs=[at(0, False), at(1, True), at(1, False), at(2, True), at(2, False)],
        out_specs=[out, out], out_shape=[sds, sds], compiler_params=_params(("arbitrary", "arbitrary")),
    )(view, view, view, view, view)
    return o.reshape(s, w), lse.reshape(s, w)


def _attn_bwd(qkv, o, lse, do, dlse, dil):
    s = qkv.shape[0]
    w = _A_WIDTH
    length = s // dil
    nb = length // _WIN_BLOCK
    view = qkv.reshape(length, dil * 3 * w)
    o, lse, do, dlse = (t.reshape(length, dil * w) for t in (o, lse, do, dlse))

    def body(q_ref, kp_ref, kc_ref, vp_ref, vc_ref, o_ref, l_ref, do_ref, dl_ref, out_ref,
             car_q, car_k, car_v, cur_q, cur_kc, cur_vc, cur_kp, cur_vp):
        n = pl.program_id(1)

        @pl.when(n == 0)
        def _():
            car_q[...] = jnp.zeros_like(car_q)
            car_k[...] = jnp.zeros_like(car_k)
            car_v[...] = jnp.zeros_like(car_v)

        @pl.when(n == nb)
        def _():
            cur_kp[...] = jnp.zeros_like(cur_kp)
            cur_vp[...] = jnp.zeros_like(cur_vp)

        @pl.when(n < nb)
        def _():
            allow_p, allow_c = _band_masks(n == 0)
            hm = _head_masks()
            for sl in range(w // 128):
                cols = slice(sl * 128, (sl + 1) * 128)
                q, kp, kc, vp, vc = q_ref[:, cols], kp_ref[:, cols], kc_ref[:, cols], vp_ref[:, cols], vc_ref[:, cols]
                ov, lv, dov, dlv = o_ref[:, cols], l_ref[:, cols], do_ref[:, cols], dl_ref[:, cols]
                zero = jnp.zeros((_WIN_BLOCK, 128), _F32)
                dq_s, dkp_s, dkc_s, dvp_s, dvc_s = zero, zero, zero, zero, zero
                for hh in range(2):
                    qh = jnp.where(hm[hh], q, jnp.zeros_like(q))
                    doh = jnp.where(hm[hh], dov, 0.0).astype(_BF)
                    lse_h = jnp.sum(jnp.where(hm[hh], lv, 0.0), axis=1, keepdims=True) * (1.0 / _A_HEAD_DIM)
                    delta = jnp.sum(jnp.where(hm[hh], dov * ov, 0.0), axis=1, keepdims=True)
                    dl = jnp.sum(jnp.where(hm[hh], dlv, 0.0), axis=1, keepdims=True)
                    pp = jnp.exp(jnp.where(allow_p, _dot_nt(qh, kp), _NEG) - lse_h)
                    pc = jnp.exp(jnp.where(allow_c, _dot_nt(qh, kc), _NEG) - lse_h)
                    dsp = (pp * (_dot_nt(doh, vp) - delta + dl)).astype(_BF)
                    dsc = (pc * (_dot_nt(doh, vc) - delta + dl)).astype(_BF)
                    dq_s = dq_s + jnp.where(hm[hh], _dot(dsp, kp) + _dot(dsc, kc), 0.0)
                    dkp_s = dkp_s + _dot_tn(dsp, qh)
                    dkc_s = dkc_s + _dot_tn(dsc, qh)
                    dvp_s = dvp_s + _dot_tn(pp, doh)
                    dvc_s = dvc_s + _dot_tn(pc, doh)
                cur_q[:, cols] = dq_s
                cur_kp[:, cols] = dkp_s
                cur_kc[:, cols] = dkc_s
                cur_vp[:, cols] = dvp_s
                cur_vc[:, cols] = dvc_s

        out_ref[:, 0:w] = car_q[...]
        out_ref[:, w:2 * w] = car_k[...] + cur_kp[...]
        out_ref[:, 2 * w:3 * w] = car_v[...] + cur_vp[...]

        @pl.when(n < nb)
        def _():
            car_q[...] = cur_q[...]
            car_k[...] = cur_kc[...]
            car_v[...] = cur_vc[...]

    def at(col, prev):
        if prev:
            return pl.BlockSpec((_WIN_BLOCK, w), lambda r, n: (jnp.maximum(jnp.minimum(n, nb - 1) - 1, 0), 3 * r + col))
        return pl.BlockSpec((_WIN_BLOCK, w), lambda r, n: (jnp.minimum(n, nb - 1), 3 * r + col))

    blk = pl.BlockSpec((_WIN_BLOCK, w), lambda r, n: (jnp.minimum(n, nb - 1), r))
    scratch = [pltpu.VMEM((_WIN_BLOCK, w), _F32) for _ in range(8)]
    out = pl.pallas_call(
        body, name=f"attn_bwd_d{dil}", grid=(dil, nb + 1),
        in_specs=[at(0, False), at(1, True), at(1, False), at(2, True), at(2, False), blk, blk, blk, blk],
        out_specs=pl.BlockSpec((_WIN_BLOCK, 3 * w), lambda r, n: (jnp.maximum(n - 1, 0), r)),
        out_shape=jax.ShapeDtypeStruct((length, dil * 3 * w), _F32), scratch_shapes=scratch,
        compiler_params=_params(("arbitrary", "arbitrary")),
    )(view, view, view, view, view, o, lse, do, dlse)
    return out.reshape(s, 3 * w)


def _mix(o1, o2, o3, l1, l2, l3):
    m = jnp.maximum(jnp.maximum(l1, l2), l3)
    e1, e2, e3 = jnp.exp(l1 - m), jnp.exp(l2 - m), jnp.exp(l3 - m)
    return (e1 * o1 + e2 * o2 + e3 * o3) / (e1 + e2 + e3)


def _attn_mix_fwd(outs, lses):
    s = outs[0].shape[0]
    tiles = [(t, _A_WIDTH, 0) for t in (*outs, *lses)]
    return _rowmap("attn_mix_fwd", _mix, tiles, [], [(2 * _A_WIDTH, _A_WIDTH, 0, _BF)], rows=s, tm=_CONV_TM)[0]


def _attn_mix_bwd(outs, lses, dy):
    s = outs[0].shape[0]
    tiles = [(t, _A_WIDTH, 0) for t in (*outs, *lses, dy)]
    body = _vjp_body(_mix, 6, range(6), 1)
    return _rowmap("attn_mix_bwd", body, tiles, [], [(_A_WIDTH, _A_WIDTH, 0, _F32)] * 6, rows=s, tm=_CONV_TM)


_CONV_TM = 256
_HALO = 32


def _ln_silu(pre, g, b):
    mu = jnp.mean(pre, axis=-1, keepdims=True)
    xc = pre - mu
    y = xc * lax.rsqrt(jnp.mean(xc * xc, axis=-1, keepdims=True) + _NORM_EPS)
    return _silu(y * g + b)


def _glu_into(buf, ua_ref, ub_ref, uah_ref, ubh_ref, first):
    halo = uah_ref[...] * jax.nn.sigmoid(ubh_ref[...])
    buf[0:_HALO, :] = jnp.where(first, 0.0, halo)
    buf[_HALO:, :] = ua_ref[...] * jax.nn.sigmoid(ub_ref[...])


def _causal_taps(buf, w_ref, rows):
    base = _HALO - (_CONV_WIDTH - 1)
    acc = jnp.zeros((rows, buf.shape[1]), _F32)
    for j in range(_CONV_WIDTH):
        acc = acc + w_ref[j:j + 1, :] * buf[base + j:base + j + rows, :]
    return acc


def _conv_specs(s):
    tm = _CONV_TM
    per = tm // _HALO
    cw = _A_WIDTH
    cur = lambda col: pl.BlockSpec((tm, cw), lambda i: (i, col))
    prev = lambda col: pl.BlockSpec((_HALO, cw), lambda i: (jnp.maximum(i * per - 1, 0), col))
    return tm, per, cw, cur, prev


def _full2(arr):
    return pl.BlockSpec(arr.shape, lambda i: (0, 0))


def _conv_fwd(proj, dw_w, dw_b, ln_g, ln_b, y):
    s = proj.shape[0]
    tm, per, cw, cur, prev = _conv_specs(s)

    def body(ua_ref, ub_ref, uah_ref, ubh_ref, w_ref, b_ref, g_ref, be_ref, y_in, o_ref, buf):
        del y_in
        _glu_into(buf, ua_ref, ub_ref, uah_ref, ubh_ref, pl.program_id(0) == 0)
        pre = _causal_taps(buf, w_ref, tm) + b_ref[...]
        o_ref[...] = _ln_silu(pre, g_ref[...], be_ref[...]).astype(_BF)

    return pl.pallas_call(
        body, name="conv_fwd", grid=(s // tm,),
        in_specs=[cur(3), cur(4), prev(3), prev(4), _full2(dw_w), _full2(dw_b), _full2(ln_g), _full2(ln_b), _ANY],
        out_specs=pl.BlockSpec((tm, cw), lambda i: (i, 1)), out_shape=jax.ShapeDtypeStruct(y.shape, y.dtype),
        scratch_shapes=[pltpu.VMEM((tm + _HALO, cw), _F32)], input_output_aliases={8: 0},
        compiler_params=_params(("arbitrary",)),
    )(proj, proj, proj, proj, dw_w, dw_b, ln_g, ln_b, y)


def _conv_bwd_pre(proj, dw_w, dw_b, ln_g, ln_b, dy):
    s = proj.shape[0]
    tm, per, cw, cur, prev = _conv_specs(s)

    def body(ua_ref, ub_ref, uah_ref, ubh_ref, w_ref, b_ref, g_ref, be_ref, dy_ref, dp_ref, dg_ref, dbe_ref, db_ref, buf):
        i = pl.program_id(0)
        _glu_into(buf, ua_ref, ub_ref, uah_ref, ubh_ref, i == 0)
        pre = _causal_taps(buf, w_ref, tm) + b_ref[...]
        _, pull = jax.vjp(_ln_silu, pre, g_ref[...], be_ref[...])
        dpre, dg, dbe = pull(dy_ref[...])
        dp_ref[...] = dpre

        @pl.when(i == 0)
        def _():
            dg_ref[...] = jnp.zeros_like(dg_ref)
            dbe_ref[...] = jnp.zeros_like(dbe_ref)
            db_ref[...] = jnp.zeros_like(db_ref)

        dg_ref[...] += dg
        dbe_ref[...] += dbe
        db_ref[...] += jnp.sum(dpre, axis=0, keepdims=True)

    vec = pl.BlockSpec((1, cw), lambda i: (0, 0))
    vsd = jax.ShapeDtypeStruct((1, cw), _F32)
    return pl.pallas_call(
        body, name="conv_bwd_pre", grid=(s // tm,),
        in_specs=[cur(3), cur(4), prev(3), prev(4), _full2(dw_w), _full2(dw_b), _full2(ln_g), _full2(ln_b), cur(1)],
        out_specs=[pl.BlockSpec((tm, cw), lambda i: (i, 0)), vec, vec, vec],
        out_shape=[jax.ShapeDtypeStruct((s, cw), _F32), vsd, vsd, vsd],
        scratch_shapes=[pltpu.VMEM((tm + _HALO, cw), _F32)], compiler_params=_params(("arbitrary",)),
    )(proj, proj, proj, proj, dw_w, dw_b, ln_g, ln_b, dy)


def _conv_bwd_taps(proj, dw_w, dpre):
    s = proj.shape[0]
    tm, per, cw, cur, prev = _conv_specs(s)
    nt = s // tm

    def body(ua_ref, ub_ref, uah_ref, ubh_ref, w_ref, dp_ref, dpn_ref, dglu_ref, dw_ref, buf, dbuf):
        i = pl.program_id(0)
        _glu_into(buf, ua_ref, ub_ref, uah_ref, ubh_ref, i == 0)
        dpre = dp_ref[...]
        dbuf[0:tm, :] = dpre
        dbuf[tm:, :] = jnp.where(i == nt - 1, 0.0, dpn_ref[...])
        acc = jnp.zeros((tm, cw), _F32)
        for j in range(_CONV_WIDTH):
            off = _CONV_WIDTH - 1 - j
            acc = acc + w_ref[j:j + 1, :] * dbuf[off:off + tm, :]
        dglu_ref[...] = acc

        @pl.when(i == 0)
        def _():
            dw_ref[...] = jnp.zeros_like(dw_ref)

        base = _HALO - (_CONV_WIDTH - 1)
        for j in range(_CONV_WIDTH):
            dw_ref[j:j + 1, :] += jnp.sum(dpre * buf[base + j:base + j + tm, :], axis=0, keepdims=True)

    nxt = pl.BlockSpec((_HALO, cw), lambda i: (jnp.minimum((i + 1) * per, s // _HALO - 1), 0))
    return pl.pallas_call(
        body, name="conv_bwd_taps", grid=(nt,),
        in_specs=[cur(3), cur(4), prev(3), prev(4), _full2(dw_w), pl.BlockSpec((tm, cw), lambda i: (i, 0)), nxt],
        out_specs=[pl.BlockSpec((tm, cw), lambda i: (i, 0)), pl.BlockSpec((_HALO, cw), lambda i: (0, 0))],
        out_shape=[jax.ShapeDtypeStruct((s, cw), _F32), jax.ShapeDtypeStruct((_HALO, cw), _F32)],
        scratch_shapes=[pltpu.VMEM((tm + _HALO, cw), _F32), pltpu.VMEM((tm + _HALO, cw), _F32)],
        compiler_params=_params(("arbitrary",)),
    )(proj, proj, proj, proj, dw_w, dpre, dpre)


def _hyb_dproj(dqkv, tab_c, tab_s, dglu, proj):
    s = proj.shape[0]
    w = _A_WIDTH

    def body(d1_ref, d2_ref, d3_ref, c_ref, s_ref, dg_ref, ua_ref, ub_ref, o_ref):
        j = pl.program_id(1)

        @pl.when(j < 3)
        def _():
            x = d1_ref[...] + d2_ref[...] + d3_ref[...]
            o_ref[...] = _rot_apply(x, c_ref[...], s_ref[...], True).astype(_BF)

        @pl.when(j == 3)
        def _():
            o_ref[...] = (dg_ref[...] * jax.nn.sigmoid(ub_ref[...])).astype(_BF)

        @pl.when(j == 4)
        def _():
            sg = jax.nn.sigmoid(ub_ref[...])
            o_ref[...] = (dg_ref[...] * ua_ref[...] * sg * (1.0 - sg)).astype(_BF)

    dq = pl.BlockSpec((_TM, w), lambda i, j: (i, jnp.minimum(j, 2)))
    tab = pl.BlockSpec((None, _TM, 128), lambda i, j: (jnp.minimum(j, 2), i, 0))
    col = lambda c: pl.BlockSpec((_TM, w), lambda i, j: (i, c))
    return pl.pallas_call(
        body, name="hyb_dproj", grid=(s // _TM, 5),
        in_specs=[dq, dq, dq, tab, tab, col(0), col(3), col(4)],
        out_specs=pl.BlockSpec((_TM, w), lambda i, j: (i, j)),
        out_shape=jax.ShapeDtypeStruct((s, 5 * w), _BF), compiler_params=_params(("arbitrary", "arbitrary")),
    )(*dqkv, tab_c, tab_s, dglu, proj, proj)


def _proj_in(name, xn, w, layer):
    s, d = xn.shape
    ns = w.shape[-1]
    return _mm(name, xn, w, grid=(4, s // _TM), a_spec=pl.BlockSpec((_TM, d), lambda j, i: (i, 0)),
               b_spec=pl.BlockSpec((None, None, d, ns), lambda j, i: (layer, j, 0, 0)), dn=_NN,
               out_shape=jax.ShapeDtypeStruct((s, 4 * ns), _F32), out_spec=pl.BlockSpec((_TM, ns), lambda j, i: (i, j)))


def _proj_in_dx(name, dz, w, layer):
    s = dz.shape[0]
    d, ns = w.shape[2], w.shape[3]
    return _mm(name, dz, w, grid=(s // _TM, 4), a_spec=pl.BlockSpec((_TM, ns), lambda i, j: (i, j)),
               b_spec=pl.BlockSpec((None, None, d, ns), lambda i, j: (layer, j, 0, 0)), dn=_NT, red=1,
               acc_shape=(_TM, d), out_shape=jax.ShapeDtypeStruct((s, d), _F32),
               out_spec=pl.BlockSpec((_TM, d), lambda i, j: (i, 0)))


def _proj_in_dw(name, xn, dz):
    s, d = xn.shape
    ns = dz.shape[1] // 4
    return _mm(name, xn, dz, grid=(4, s // _TM), a_spec=pl.BlockSpec((_TM, d), lambda j, i: (i, 0)),
               b_spec=pl.BlockSpec((_TM, ns), lambda j, i: (i, j)), dn=_TN, red=1, acc_shape=(d, ns),
               out_shape=jax.ShapeDtypeStruct((4, d, ns), _BF),
               out_spec=pl.BlockSpec((None, d, ns), lambda j, i: (j, 0, 0)))


def _proj_out_dy(name, dh, w, layer):
    s, d = dh.shape
    k = w.shape[1]
    return _mm(name, dh, w, grid=(s // _TM,), a_spec=pl.BlockSpec((_TM, d), lambda i: (i, 0)),
               b_spec=pl.BlockSpec((None, k, d), lambda i: (layer, 0, 0)), dn=_NT,
               out_shape=jax.ShapeDtypeStruct((s, k), _F32), out_spec=pl.BlockSpec((_TM, k), lambda i: (i, 0)))


def _hyb_fwd(h, gamma, w_in, w_out, dw_w, dw_b, ln_g, ln_b, tabs, li):
    hn = _rmsnorm_fwd(h, gamma)
    proj = _proj_in("hyb_in", hn, w_in, li)
    qkv = _qkv_prep(proj, *tabs)
    branches = [_attn_fwd(qkv, dil) for dil in _DILATIONS]
    outs, lses = [b[0] for b in branches], [b[1] for b in branches]
    y = _attn_mix_fwd(outs, lses)
    y = _conv_fwd(proj, dw_w, dw_b, ln_g, ln_b, y)
    h_new = _proj_residual("hyb_out", y, w_out, li, h, 1.0)
    return h_new, (h, hn, proj, qkv, outs, lses, y)


def _hyb_bwd(dh, saved, gamma, w_in, w_out, dw_w, dw_b, ln_g, ln_b, tabs, li):
    h, hn, proj, qkv, outs, lses, y = saved
    dy = _proj_out_dy("hyb_dy", dh, w_out, li)
    d_w_out = _wgrad_rows("hyb_dwout", y, dh, 1.0)
    mixg = _attn_mix_bwd(outs, lses, dy)
    dqkv = [_attn_bwd(qkv, outs[b], lses[b], mixg[b], mixg[3 + b], dil) for b, dil in enumerate(_DILATIONS)]
    dpre, d_ln_g, d_ln_b, d_dw_b = _conv_bwd_pre(proj, dw_w, dw_b, ln_g, ln_b, dy)
    dglu, d_dw_w = _conv_bwd_taps(proj, dw_w, dpre)
    dproj = _hyb_dproj(dqkv, *tabs, dglu, proj)
    dhn = _proj_in_dx("hyb_dhn", dproj, w_in, li)
    d_w_in = _proj_in_dw("hyb_dwin", hn, dproj)
    dh_in, dgamma = _rmsnorm_bwd(h, gamma, dhn, dh)
    return dh_in, dgamma, (d_dw_w, d_dw_b, d_ln_g, d_ln_b), d_w_in, d_w_out


_GDN_BLOCK = 128
_GDN_HEADS_PER_STEP = 8


def _gdn_chunk(qs, ks, vs, gates, states, heads):
    n = len(qs)
    hs = range(n)
    c = qs[0].shape[0]
    lane = lax.broadcasted_iota(jnp.int32, gates.shape, 1)
    row = lax.broadcasted_iota(jnp.int32, (c, c), 0)
    col = lax.broadcasted_iota(jnp.int32, (c, c), 1)
    rowc = lax.broadcasted_iota(jnp.int32, (c, 1), 0)
    causal = row >= col
    eye = jnp.where(row == col, 1.0, 0.0)
    beta = [jnp.sum(jnp.where(lane == heads[h], gates, 0.0), axis=1, keepdims=True) for h in hs]
    g = [jnp.sum(jnp.where(lane == heads[h] + _GDN_HEADS, gates, 0.0), axis=1, keepdims=True) for h in hs]
    gc_row = [jnp.sum(jnp.where(row <= col, g[h], 0.0), axis=0, keepdims=True) for h in hs]
    gc_col = [jnp.sum(jnp.where(row == col, gc_row[h], 0.0), axis=1, keepdims=True) for h in hs]
    decay = [jnp.where(causal, jnp.exp(jnp.where(causal, gc_col[h] - gc_row[h], 0.0)), 0.0) for h in hs]
    kb = [ks[h] * beta[h] for h in hs]
    l_mat = [jnp.where(row > col, _dot_nt(kb[h], ks[h]) * decay[h], 0.0) for h in hs]
    t_inv = [eye - l_mat[h] for h in hs]
    x = [_dot(l_mat[h], l_mat[h]) for h in hs]
    for it in range(5):
        t_inv = [t_inv[h] + _dot(t_inv[h], x[h]) for h in hs]
        if it < 4:
            x = [_dot(x[h], x[h]) for h in hs]
    eg = [jnp.exp(gc_col[h]) for h in hs]
    u = [_dot(t_inv[h], vs[h] * beta[h]) for h in hs]
    w = [_dot(t_inv[h], kb[h] * eg[h]) for h in hs]
    attn = [jnp.where(causal, _dot_nt(qs[h], ks[h]) * decay[h], 0.0) for h in hs]
    g_last = [jnp.sum(jnp.where(rowc == c - 1, gc_col[h], 0.0), axis=0, keepdims=True) for h in hs]
    k_dec = [ks[h] * jnp.exp(g_last[h] - gc_col[h]) for h in hs]
    v_new = [u[h] - _dot(w[h], states[h]) for h in hs]
    o = [_dot(qs[h] * eg[h], states[h]) + _dot(attn[h], v_new[h]) for h in hs]
    new_states = [states[h] * jnp.exp(g_last[h]) + _dot_tn(k_dec[h], v_new[h]) for h in hs]
    return o, new_states


def _gdn_chunk_fwd(q, k, v, gates):
    s = q.shape[0]
    cb, c = _GDN_BLOCK, _GDN_CHUNK
    nblk, per = s // cb, cb // c

    hps = _GDN_HEADS_PER_STEP

    def body(q_ref, k_ref, v_ref, g_ref, o_ref, st_ref, state):
        @pl.when(pl.program_id(1) == 0)
        def _():
            state[...] = jnp.zeros_like(state)

        heads = [pl.program_id(0) * hps + hh for hh in range(hps)]
        cols = [slice(hh * _GDN_DIM, (hh + 1) * _GDN_DIM) for hh in range(hps)]
        for ci in range(per):
            rows = slice(ci * c, (ci + 1) * c)
            sts = [state[hh] for hh in range(hps)]
            for hh in range(hps):
                st_ref[hh, ci] = sts[hh]
            outs, new = _gdn_chunk([q_ref[rows, cl] for cl in cols], [k_ref[rows, cl] for cl in cols],
                                   [v_ref[rows, cl] for cl in cols], g_ref[rows, :], sts, heads)
            for hh in range(hps):
                o_ref[rows, cols[hh]] = outs[hh]
                state[hh] = new[hh]

    slab = pl.BlockSpec((cb, hps * _GDN_DIM), lambda h, i: (i, h))
    return pl.pallas_call(
        body, name="gdn_chunk_fwd", grid=(_GDN_HEADS // hps, nblk),
        in_specs=[slab, slab, slab, pl.BlockSpec((cb, 128), lambda h, i: (i, 0))],
        out_specs=[slab, pl.BlockSpec((hps, per, _GDN_DIM, _GDN_DIM), lambda h, i: (h, i, 0, 0))],
        out_shape=[jax.ShapeDtypeStruct((s, _GDN_HEADS * _GDN_DIM), _F32),
                   jax.ShapeDtypeStruct((_GDN_HEADS, s // c, _GDN_DIM, _GDN_DIM), _F32)],
        scratch_shapes=[pltpu.VMEM((hps, _GDN_DIM, _GDN_DIM), _F32)],
        compiler_params=_params(("arbitrary", "arbitrary")),
    )(q, k, v, gates)


def _gdn_chunk_bwd(q, k, v, gates, states, do):
    s = q.shape[0]
    cb, c = _GDN_BLOCK, _GDN_CHUNK
    nblk, per = s // cb, cb // c

    hps = _GDN_HEADS_PER_STEP

    def body(q_ref, k_ref, v_ref, g_ref, st_ref, do_ref, dq_ref, dk_ref, dv_ref, dg_ref, dstate):
        @pl.when(pl.program_id(1) == 0)
        def _():
            dstate[...] = jnp.zeros_like(dstate)

        heads = [pl.program_id(0) * hps + hh for hh in range(hps)]
        cols = [slice(hh * _GDN_DIM, (hh + 1) * _GDN_DIM) for hh in range(hps)]
        for ci in reversed(range(per)):
            rows = slice(ci * c, (ci + 1) * c)
            fn = functools.partial(_gdn_chunk, heads=heads)
            _, pull = jax.vjp(fn, [q_ref[rows, cl] for cl in cols], [k_ref[rows, cl] for cl in cols],
                              [v_ref[rows, cl] for cl in cols], g_ref[rows, :], [st_ref[hh, ci] for hh in range(hps)])
            dq, dk, dv, dg, dst = pull(([do_ref[rows, cl] for cl in cols], [dstate[hh] for hh in range(hps)]))
            dg_ref[rows, :] = dg
            for hh in range(hps):
                dq_ref[rows, cols[hh]] = dq[hh]
                dk_ref[rows, cols[hh]] = dk[hh]
                dv_ref[rows, cols[hh]] = dv[hh]
                dstate[hh] = dst[hh]

    slab = pl.BlockSpec((cb, hps * _GDN_DIM), lambda h, i: (nblk - 1 - i, h))
    sds = jax.ShapeDtypeStruct((s, _GDN_HEADS * _GDN_DIM), _F32)
    return pl.pallas_call(
        body, name="gdn_chunk_bwd", grid=(_GDN_HEADS // hps, nblk),
        in_specs=[slab, slab, slab, pl.BlockSpec((cb, 128), lambda h, i: (nblk - 1 - i, 0)),
                  pl.BlockSpec((hps, per, _GDN_DIM, _GDN_DIM), lambda h, i: (h, nblk - 1 - i, 0, 0)), slab],
        out_specs=[slab, slab, slab, pl.BlockSpec((None, cb, 128), lambda h, i: (h, nblk - 1 - i, 0))],
        out_shape=[sds, sds, sds, jax.ShapeDtypeStruct((_GDN_HEADS // hps, s, 128), _F32)],
        scratch_shapes=[pltpu.VMEM((hps, _GDN_DIM, _GDN_DIM), _F32)],
        compiler_params=_params(("arbitrary", "arbitrary")),
    )(q, k, v, gates, states, do)


_SHALO = 8


def _swish_l2(c, normalize, scale):
    y = _silu(c)
    if normalize:
        y = y * lax.rsqrt(jnp.sum(y * y, axis=-1, keepdims=True) + _NORM_EPS) * scale
    return y


def _short_taps(buf, w_ref, rows):
    base = _SHALO - (_GDN_SHORT_CONV - 1)
    acc = jnp.zeros((rows, buf.shape[1]), _F32)
    for t in range(_GDN_SHORT_CONV):
        acc = acc + w_ref[t:t + 1, :] * buf[base + t:base + t + rows, :]
    return acc


_GDN_W = _GDN_HEADS * _GDN_DIM


def _gdn_conv_fwd(proj, conv_w, grp, normalize, scale):
    s = proj.shape[0]
    tm = _CONV_TM
    per = tm // _SHALO

    def body(x_ref, xp_ref, w_ref, o_ref, buf):
        buf[0:_SHALO, :] = jnp.where(pl.program_id(0) == 0, 0.0, xp_ref[...])
        buf[_SHALO:, :] = x_ref[...]
        conv = _short_taps(buf, w_ref, tm)
        for h in range(_GDN_HEADS):
            cols = slice(h * _GDN_DIM, (h + 1) * _GDN_DIM)
            o_ref[:, cols] = _swish_l2(conv[:, cols], normalize, scale)

    return pl.pallas_call(
        body, name=f"gdn_conv_fwd_{grp}", grid=(s // tm,),
        in_specs=[pl.BlockSpec((tm, _GDN_W), lambda i: (i, grp)),
                  pl.BlockSpec((_SHALO, _GDN_W), lambda i: (jnp.maximum(i * per - 1, 0), grp)),
                  pl.BlockSpec((_GDN_SHORT_CONV, _GDN_W), lambda i: (0, grp))],
        out_specs=pl.BlockSpec((tm, _GDN_W), lambda i: (i, 0)),
        out_shape=jax.ShapeDtypeStruct((s, _GDN_W), _F32),
        scratch_shapes=[pltpu.VMEM((tm + _SHALO, _GDN_W), _F32)], compiler_params=_params(("arbitrary",)),
    )(proj, proj, conv_w)


def _gdn_conv_bwd(proj, conv_w, grp, normalize, scale, dy, dproj):
    s = proj.shape[0]
    tm = _CONV_TM
    per = tm // _SHALO
    nt = s // tm
    taps = _GDN_SHORT_CONV

    def body(x_ref, xp_ref, xn_ref, w_ref, dy_ref, dyn_ref, dp_in, dx_ref, dw_ref, buf, dbuf):
        del dp_in
        i = pl.program_id(0)
        last = i == nt - 1
        buf[0:_SHALO, :] = jnp.where(i == 0, 0.0, xp_ref[...])
        buf[_SHALO:_SHALO + tm, :] = x_ref[...]
        buf[_SHALO + tm:, :] = jnp.where(last, 0.0, xn_ref[...])
        conv = _short_taps(buf, w_ref, tm + _SHALO)
        dy_ext = jnp.concatenate([dy_ref[...], jnp.where(last, 0.0, dyn_ref[...])], axis=0)
        for h in range(_GDN_HEADS):
            cols = slice(h * _GDN_DIM, (h + 1) * _GDN_DIM)
            _, pull = jax.vjp(lambda c: _swish_l2(c, normalize, scale), conv[:, cols])
            dbuf[:, cols] = pull(dy_ext[:, cols])[0]
        acc = jnp.zeros((tm, _GDN_W), _F32)
        for t in range(taps):
            o = taps - 1 - t
            acc = acc + w_ref[t:t + 1, :] * dbuf[o:o + tm, :]
        dx_ref[...] = acc.astype(dx_ref.dtype)

        @pl.when(i == 0)
        def _():
            dw_ref[...] = jnp.zeros_like(dw_ref)

        base = _SHALO - (taps - 1)
        for t in range(taps):
            dw_ref[t:t + 1, :] += jnp.sum(dbuf[0:tm, :] * buf[base + t:base + t + tm, :], axis=0, keepdims=True)

    cur = lambda col: pl.BlockSpec((tm, _GDN_W), lambda i: (i, col))
    nxt = lambda col: pl.BlockSpec((_SHALO, _GDN_W), lambda i: (jnp.minimum((i + 1) * per, s // _SHALO - 1), col))
    return pl.pallas_call(
        body, name=f"gdn_conv_bwd_{grp}", grid=(nt,),
        in_specs=[cur(grp), pl.BlockSpec((_SHALO, _GDN_W), lambda i: (jnp.maximum(i * per - 1, 0), grp)), nxt(grp),
                  pl.BlockSpec((taps, _GDN_W), lambda i: (0, grp)), cur(0), nxt(0), _ANY],
        out_specs=[cur(grp), pl.BlockSpec((_SHALO, _GDN_W), lambda i: (0, 0))],
        out_shape=[jax.ShapeDtypeStruct(dproj.shape, dproj.dtype), jax.ShapeDtypeStruct((_SHALO, _GDN_W), _F32)],
        scratch_shapes=[pltpu.VMEM((tm + 2 * _SHALO, _GDN_W), _F32), pltpu.VMEM((tm + _SHALO, _GDN_W), _F32)],
        input_output_aliases={6: 0}, compiler_params=_params(("arbitrary",)),
    )(proj, proj, proj, conv_w, dy, dy, dproj)


def _softplus(z):
    return jnp.maximum(z, 0.0) + jnp.log(1.0 + jnp.exp(-jnp.abs(z)))


def _gates(ba, a_lane, dt_lane):
    lane = lax.broadcasted_iota(jnp.int32, ba.shape, 1)
    g = -jnp.exp(a_lane) * _softplus(ba + dt_lane)
    return jnp.where(lane < _GDN_HEADS, jax.nn.sigmoid(ba), jnp.where(lane < 2 * _GDN_HEADS, g, 0.0))


def _gates_fwd(ba, a_lane, dt_lane):
    return _rowmap("gdn_gates_fwd", _gates, [(ba, 128, 0)], [a_lane, dt_lane], [(128, 128, 0, _F32)],
                   rows=ba.shape[0], tm=_TM)[0]


def _gates_bwd(ba, a_lane, dt_lane, dgates):
    def body(bav, dg8, av, dv):
        _, pull = jax.vjp(_gates, bav, av, dv)
        return pull(jnp.sum(dg8, axis=0))

    return _rowmap("gdn_gates_bwd", body, [(ba, 128, 0), (dgates, 128, 0)], [a_lane, dt_lane], [(128, 128, 0, _BF)],
                   [(1, 128), (1, 128)], rows=ba.shape[0], tm=_TM)


def _out_gate(o, z, g):
    return (_rms(o, g) * _silu(z)).astype(_BF)


def _head_cols():
    return [slice(h * _GDN_DIM, (h + 1) * _GDN_DIM) for h in range(_GDN_HEADS)]


def _out_gate_fwd(o, proj, norm_g):
    def body(ov, zv, gv):
        return jnp.concatenate([_out_gate(ov[:, cl], zv[:, cl], gv) for cl in _head_cols()], axis=1)

    return _rowmap("gdn_out_gate_fwd", body, [(o, _GDN_W, 0), (proj, _GDN_W, 3)], [norm_g], [(_GDN_W, _GDN_W, 0, _BF)],
                   rows=o.shape[0], tm=_CONV_TM)[0]


def _out_gate_bwd(o, proj, norm_g, dy):
    def body(ov, zv, dyv, gv):
        d_o, d_z, d_g = [], [], jnp.zeros_like(gv)
        for cl in _head_cols():
            _, pull = jax.vjp(_out_gate, ov[:, cl], zv[:, cl], gv)
            g_o, g_z, g_g = pull(dyv[:, cl].astype(_BF))
            d_o.append(g_o)
            d_z.append(g_z)
            d_g = d_g + g_g
        return jnp.concatenate(d_o, axis=1), jnp.concatenate(d_z, axis=1), d_g

    return _rowmap("gdn_out_gate_bwd", body, [(o, _GDN_W, 0), (proj, _GDN_W, 3), (dy, _GDN_W, 0)], [norm_g],
                   [(_GDN_W, _GDN_W, 0, _F32), (proj.shape[1], _GDN_W, 3, _BF)], [(1, _GDN_DIM)],
                   rows=o.shape[0], tm=_CONV_TM)


def _plain_in(name, xn, w, layer, tn):
    s, d = xn.shape
    n = w.shape[2]
    return _mm(name, xn, w, grid=(n // tn, s // _TM), a_spec=pl.BlockSpec((_TM, d), lambda j, i: (i, 0)),
               b_spec=pl.BlockSpec((None, d, tn), lambda j, i: (layer, 0, j)), dn=_NN,
               out_shape=jax.ShapeDtypeStruct((s, n), _F32), out_spec=pl.BlockSpec((_TM, tn), lambda j, i: (i, j)))


def _plain_dx(name, dz, w, layer, tn, extra=None):
    s, n = dz.shape
    d = w.shape[1]
    row = pl.BlockSpec((_TM, d), lambda i, j: (i, 0))
    return _mm(name, dz, w, grid=(s // _TM, n // tn), a_spec=pl.BlockSpec((_TM, tn), lambda i, j: (i, j)),
               b_spec=pl.BlockSpec((None, d, tn), lambda i, j: (layer, 0, j)), dn=_NT, red=1, acc_shape=(_TM, d),
               out_shape=jax.ShapeDtypeStruct((s, d), _F32), out_spec=row,
               epi=(lambda acc, e: acc + e) if extra is not None else None,
               extras=(extra,) if extra is not None else (), extra_specs=(row,) if extra is not None else ())


def _plain_dw(name, xn, dz, tn):
    s, d = xn.shape
    n = dz.shape[1]
    return _mm(name, xn, dz, grid=(n // tn, s // _TM), a_spec=pl.BlockSpec((_TM, d), lambda j, i: (i, 0)),
               b_spec=pl.BlockSpec((_TM, tn), lambda j, i: (i, j)), dn=_TN, red=1, acc_shape=(d, tn),
               out_shape=jax.ShapeDtypeStruct((d, n), _BF), out_spec=pl.BlockSpec((d, tn), lambda j, i: (0, j)))


_GDN_QK_SCALE = _GDN_DIM ** -0.5


def _gdn_fwd(h, gamma, w_qkvz, w_ba, w_out, conv_w, a_lane, dt_lane, norm_g, li):
    hn = _rmsnorm_fwd(h, gamma)
    proj = _plain_in("gdn_in", hn, w_qkvz, li, 1024)
    ba = _plain_in("gdn_in_ba", hn, w_ba, li, 128)
    q = _gdn_conv_fwd(proj, conv_w, 0, True, _GDN_QK_SCALE)
    k = _gdn_conv_fwd(proj, conv_w, 1, True, 1.0)
    v = _gdn_conv_fwd(proj, conv_w, 2, False, 1.0)
    gates = _gates_fwd(ba, a_lane, dt_lane)
    o, states = _gdn_chunk_fwd(q, k, v, gates)
    y = _out_gate_fwd(o, proj, norm_g)
    h_new = _proj_residual("gdn_out", y, w_out, li, h, 1.0)
    return h_new, (h, hn, proj, ba, q, k, v, gates, states, o, y)


def _gdn_bwd(dh, saved, gamma, w_qkvz, w_ba, w_out, conv_w, a_lane, dt_lane, norm_g, li, n_ba):
    h, hn, proj, ba, q, k, v, gates, states, o, y = saved
    dy = _proj_out_dy("gdn_dy", dh, w_out, li)
    d_w_out = _wgrad_rows("gdn_dwout", y, dh, 1.0)
    do, dproj, d_norm_g = _out_gate_bwd(o, proj, norm_g, dy)
    dq, dk, dv, dgates = _gdn_chunk_bwd(q, k, v, gates, states, do)
    dproj, dcw_q = _gdn_conv_bwd(proj, conv_w, 0, True, _GDN_QK_SCALE, dq, dproj)
    dproj, dcw_k = _gdn_conv_bwd(proj, conv_w, 1, True, 1.0, dk, dproj)
    dproj, dcw_v = _gdn_conv_bwd(proj, conv_w, 2, False, 1.0, dv, dproj)
    dba, d_a, d_dt = _gates_bwd(ba, a_lane, dt_lane, dgates)
    dhn_ba = _plain_dx("gdn_dhn_ba", dba, w_ba, li, 128)
    dhn = _plain_dx("gdn_dhn", dproj, w_qkvz, li, 1024, extra=dhn_ba)
    dw_qkvz = _plain_dw("gdn_dwin", hn, dproj, 1024)
    dw_ba = _plain_dw("gdn_dwin_ba", hn, dba, 128)
    dh_in, dgamma = _rmsnorm_bwd(h, gamma, dhn, dh)
    d_conv_w = jnp.concatenate([dcw_q, dcw_k, dcw_v], axis=1)[:_GDN_SHORT_CONV]
    d_w_in = jnp.concatenate([dw_qkvz, dw_ba[:, :n_ba]], axis=1)
    d_w_in = jnp.transpose(d_w_in.reshape(d_w_in.shape[0], 4, -1), (1, 0, 2))
    return dh_in, dgamma, (d_conv_w, d_a, d_dt, d_norm_g), d_w_in, d_w_out


_MESH = pl.DeviceIdType.MESH
_ROW_TM = 384


def _place():
    x, y, c = lax.axis_index("x"), lax.axis_index("y"), lax.axis_index("c")
    chips = [(1 - x, y), (x, 1 - y), (1 - x, 1 - y)]
    return x, y, c, chips, [2 * px + py for px, py in chips]


def _remote(src, dst, send_sem, recv_sem, device):
    return pltpu.make_async_remote_copy(src_ref=src, dst_ref=dst, send_sem=send_sem, recv_sem=recv_sem,
                                        device_id=device, device_id_type=_MESH)


def _rows2d(t):
    return t.reshape(-1, t.shape[-1])


def _row_tile(rows):
    for tm in range(min(rows, _ROW_TM) // 8 * 8, 7, -8):
        if rows % tm == 0:
            return tm
    raise ValueError(f"no row tile for {rows} rows")


def _placed_map(name, fn, where, ins, out_rows, cols, out_dtype, out_row, steps, tm, into=None):
    n_in = len(ins)

    def kern(where_ref, *refs):
        del where_ref
        o_ref = refs[n_in + (1 if into is not None else 0)]
        o_ref[...] = fn(*[r[...] for r in refs[:n_in]]).astype(o_ref.dtype)

    in_specs, operands = [], [where]
    for arr, rowfn in ins:
        operands.append(arr)
        if arr.ndim == 3:
            in_specs.append(pl.BlockSpec((arr.shape[0], tm, cols), lambda i, wh, f=rowfn: (0, f(i, wh), 0)))
        else:
            in_specs.append(pl.BlockSpec((tm, cols), lambda i, wh, f=rowfn: (f(i, wh), 0)))
    aliases = {}
    if into is not None:
        operands.append(into)
        in_specs.append(_ANY)
        aliases = {n_in + 1: 0}
    spec = pltpu.PrefetchScalarGridSpec(
        num_scalar_prefetch=1, grid=(steps,), in_specs=in_specs,
        out_specs=pl.BlockSpec((tm, cols), lambda i, wh: (out_row(i, wh), 0)))
    return pl.pallas_call(
        kern, name=name, grid_spec=spec, out_shape=jax.ShapeDtypeStruct((out_rows, cols), out_dtype),
        input_output_aliases=aliases, compiler_params=_params(("arbitrary",)),
    )(*operands)


def _cast_slab(name, t, where):
    n_l, r, cols = t.shape
    tm = _row_tile(r)
    per = r // tm
    buf = _placed_map("cast_" + name, lambda val: val, where, [(_rows2d(t), lambda i, wh: i)], n_l * 4 * r, cols, _BF,
                      lambda i, wh: ((i // per) * 4 + wh[1]) * per + i % per, n_l * per, tm)
    return buf.reshape(n_l, 4, r, cols)


_HBM = pl.BlockSpec(memory_space=pltpu.HBM)
_SEM = pl.BlockSpec(memory_space=pltpu.SEMAPHORE)
_EFFECT = pltpu.SideEffectType.DATAFLOW_SIDE_EFFECTING


def _core_layers(n_layers, c, part):
    half = n_layers // 2
    sub = half // part[1]
    return pl.ds(c * half + part[0] * sub, sub), pl.ds((1 - c) * half + part[0] * sub, sub)


def _ici_copies(refs, send, recv, arrivals, part):
    x, y, c, chips, idx = _place()
    me = 2 * x + y
    res = []
    for t, ref in enumerate(refs):
        mine, _ = _core_layers(ref.shape[0], c, part)
        for k in range(3):
            sem = 3 * t + k
            block = ref.at[mine, idx[k] if arrivals else me]
            res.append(_remote(block, block, send.at[sem], recv.at[sem], (*chips[k], c)))
    return res


def _gather_start(name, bufs, after, part):
    n = len(bufs)

    def body(*refs):
        send, recv = refs[n + 1], refs[n + 2]
        for cp in _ici_copies(refs[:n], send, recv, False, part):
            cp.start()
        refs[2 * n + 3][...] = jnp.zeros((8, 128), _F32)

    res = pl.pallas_call(
        body, name=name,
        out_shape=(pltpu.SemaphoreType.DMA((3 * n,)), pltpu.SemaphoreType.DMA((3 * n,)),
                   *[pltpu.HBM(b.shape, b.dtype) for b in bufs], jax.ShapeDtypeStruct((8, 128), _F32)),
        in_specs=[*[_HBM] * n, _ANY], out_specs=(_SEM, _SEM, *[_HBM] * n, pl.BlockSpec(memory_space=pltpu.VMEM)),
        input_output_aliases={t: 2 + t for t in range(n)},
        compiler_params=pltpu.CompilerParams(has_side_effects=_EFFECT),
    )(*[pltpu.with_memory_space_constraint(b, pltpu.HBM) for b in bufs], after)
    return res[0], res[1], list(res[2:2 + n]), res[2 + n]


def _gather_wait(name, send_sem, recv_sem, bufs, after, part):
    n = len(bufs)

    def body(*refs):
        send, recv = refs[n], refs[n + 1]
        for cp in _ici_copies(refs[:n], send, recv, False, part):
            cp.wait_send()
        for cp in _ici_copies(refs[:n], send, recv, True, part):
            cp.wait_recv()

    res = pl.pallas_call(
        body, name=name, out_shape=tuple(pltpu.HBM(b.shape, b.dtype) for b in bufs),
        in_specs=[*[_HBM] * n, _SEM, _SEM, _ANY], out_specs=tuple([_HBM] * n),
        input_output_aliases={t: t for t in range(n)},
        compiler_params=pltpu.CompilerParams(has_side_effects=_EFFECT),
    )(*bufs, send_sem, recv_sem, after)
    return list(res)


def _gather_forward(name, bufs, part):
    n = len(bufs)

    def body(*refs):
        outs, send, recv = refs[n:2 * n], refs[2 * n], refs[2 * n + 1]
        x, y, c, _, idx = _place()
        copies, arrive = [], []
        for t, ref in enumerate(outs):
            mine, other = _core_layers(ref.shape[0], c, part)
            for k in range(3):
                sem = 3 * t + k
                landed, dst = ref.at[mine, idx[k]], ref.at[other, idx[k]]
                copies.append(_remote(landed, landed, send.at[sem], recv.at[sem], (x, y, 1 - c)))
                arrive.append(_remote(dst, dst, send.at[sem], recv.at[sem], (x, y, 1 - c)))
        for cp in copies:
            cp.start()
        for cp in arrive:
            cp.wait_recv()
        for cp in copies:
            cp.wait_send()

    res = pl.pallas_call(
        body, name=name, in_specs=[_ANY] * n, out_specs=[_ANY] * n,
        out_shape=[jax.ShapeDtypeStruct(b.shape, b.dtype) for b in bufs],
        scratch_shapes=[pltpu.SemaphoreType.DMA((3 * n,)), pltpu.SemaphoreType.DMA((3 * n,))],
        input_output_aliases={t: t for t in range(n)},
    )(*bufs)
    return list(res)


def _rs_sibling(name, gs):
    n = len(gs)

    def body(*refs):
        send, recv = refs[2 * n], refs[2 * n + 1]
        x, y, c, _, _ = _place()
        copies = []
        for t in range(n):
            rh = refs[t].shape[1] // 2
            cp = _remote(refs[t].at[:, pl.ds((1 - c) * rh, rh)], refs[n + t], send.at[t], recv.at[t], (x, y, 1 - c))
            cp.start()
            copies.append(cp)
        for cp in copies:
            cp.wait()

    return pl.pallas_call(
        body, name=name, in_specs=[_ANY] * n, out_specs=[_ANY] * n,
        out_shape=[jax.ShapeDtypeStruct((4, g.shape[1] // 2, g.shape[2]), g.dtype) for g in gs],
        scratch_shapes=[pltpu.SemaphoreType.DMA((n,)), pltpu.SemaphoreType.DMA((n,))],
    )(*gs)


def _chip_copies(sums, lands, send, recv):
    x, y, c, chips, idx = _place()
    return [_remote(sums[t].at[idx[k]], lands[t].at[k], send.at[3 * t + k], recv.at[3 * t + k], (*chips[k], c))
            for t in range(len(sums)) for k in range(3)]


def _rs_chips_start(name, sums, after):
    n = len(sums)
    lands = [lax.empty((3,) + s.shape[1:], s.dtype) for s in sums]

    def body(*refs):
        send, recv = refs[2 * n + 1], refs[2 * n + 2]
        for cp in _chip_copies(refs[:n], refs[n:2 * n], send, recv):
            cp.start()
        refs[4 * n + 3][...] = jnp.zeros((8, 128), _F32)

    both = [*sums, *lands]
    res = pl.pallas_call(
        body, name=name,
        out_shape=(pltpu.SemaphoreType.DMA((3 * n,)), pltpu.SemaphoreType.DMA((3 * n,)),
                   *[pltpu.HBM(b.shape, b.dtype) for b in both], jax.ShapeDtypeStruct((8, 128), _F32)),
        in_specs=[*[_HBM] * (2 * n), _ANY],
        out_specs=(_SEM, _SEM, *[_HBM] * (2 * n), pl.BlockSpec(memory_space=pltpu.VMEM)),
        input_output_aliases={t: 2 + t for t in range(2 * n)},
        compiler_params=pltpu.CompilerParams(has_side_effects=_EFFECT),
    )(*[pltpu.with_memory_space_constraint(b, pltpu.HBM) for b in both], after)
    return res[0], res[1], list(res[2:2 + n]), list(res[2 + n:2 + 2 * n]), res[2 + 2 * n]


def _rs_chips_wait(name, send_sem, recv_sem, sums, lands, after):
    n = len(sums)

    def body(*refs):
        send, recv = refs[2 * n], refs[2 * n + 1]
        copies = _chip_copies(refs[:n], refs[n:2 * n], send, recv)
        for cp in copies:
            cp.wait_send()
        for cp in copies:
            cp.wait_recv()

    both = [*sums, *lands]
    res = pl.pallas_call(
        body, name=name, out_shape=tuple(pltpu.HBM(b.shape, b.dtype) for b in both),
        in_specs=[*[_HBM] * (2 * n), _SEM, _SEM, _ANY], out_specs=tuple([_HBM] * (2 * n)),
        input_output_aliases={t: t for t in range(2 * n)},
        compiler_params=pltpu.CompilerParams(has_side_effects=_EFFECT),
    )(*both, send_sem, recv_sem, after)
    return list(res[:n]), list(res[n:])


def _rs_join(name, totals, layers):
    n = len(totals)

    def body(*refs):
        outs, send, recv = refs[n:2 * n], refs[2 * n], refs[2 * n + 1]
        x, y, c, _, _ = _place()
        copies, arrive = [], []
        for t in range(n):
            rh = outs[t].shape[1] // 2
            mine = outs[t].at[layers[t], pl.ds(c * rh, rh)]
            landing = outs[t].at[layers[t], pl.ds((1 - c) * rh, rh)]
            copies.append(_remote(mine, mine, send.at[t], recv.at[t], (x, y, 1 - c)))
            arrive.append(_remote(landing, landing, send.at[t], recv.at[t], (x, y, 1 - c)))
        for cp in copies:
            cp.start()
        for cp in arrive:
            cp.wait_recv()
        for cp in copies:
            cp.wait_send()

    return pl.pallas_call(
        body, name=name, in_specs=[_ANY] * n, out_specs=[_ANY] * n,
        out_shape=[jax.ShapeDtypeStruct(b.shape, b.dtype) for b in totals],
        scratch_shapes=[pltpu.SemaphoreType.DMA((n,)), pltpu.SemaphoreType.DMA((n,))],
        input_output_aliases={t: t for t in range(n)},
    )(*totals)


def _pair_sum(g, got, where):
    _, r, cols = g.shape
    rh = r // 2
    tm = _row_tile(rh)
    per, perh = r // tm, rh // tm
    out = _placed_map("rs_add_pair", lambda u, w: u.astype(_F32) + w.astype(_F32), where,
                      [(_rows2d(g), lambda i, wh: (i // perh) * per + wh[0] * perh + i % perh),
                       (_rows2d(got), lambda i, wh: i)],
                      4 * rh, cols, _BF, lambda i, wh: i, 4 * perh, tm)
    return out.reshape(4, rh, cols)


def _sum_of_four(pair, land, where, total, layer, n_layers):
    _, rh, cols = pair.shape
    r = 2 * rh
    tm = _row_tile(rh)
    per, perh = r // tm, rh // tm

    def add_four(own, others):
        return ((own.astype(_F32) + others[0].astype(_F32)) + others[1].astype(_F32)) + others[2].astype(_F32)

    out = _placed_map("rs_add_four", add_four, where,
                      [(_rows2d(pair), lambda i, wh: wh[1] * perh + i), (land, lambda i, wh: i)],
                      n_layers * r, cols, _F32, lambda i, wh: layer * per + wh[0] * perh + i, perh, tm,
                      into=None if total is None else _rows2d(total))
    return out.reshape(n_layers, r, cols)


def _all_reduce_small(pack):
    rows, cols = pack.shape

    def body(p_ref, o_ref, land, send, recv):
        x, y, c = lax.axis_index("x"), lax.axis_index("y"), lax.axis_index("c")
        me = 4 * x + 2 * y + c
        land[me] = p_ref[...]
        copies = []
        for k in range(1, 8):
            bx, by, bc = (k >> 2) & 1, (k >> 1) & 1, k & 1
            peer = (1 - x if bx else x, 1 - y if by else y, 1 - c if bc else c)
            cp = _remote(p_ref, land.at[me], send.at[k - 1], recv.at[k - 1], peer)
            cp.start()
            copies.append((cp, 4 * peer[0] + 2 * peer[1] + peer[2]))
        for k, (cp, pid) in enumerate(copies):
            _remote(p_ref, land.at[pid], send.at[k], recv.at[k], (x, y, c)).wait_recv()
        for cp, _ in copies:
            cp.wait_send()
        acc = land[0]
        for d in range(1, 8):
            acc = acc + land[d]
        o_ref[...] = acc

    vm = pl.BlockSpec(memory_space=pltpu.VMEM)
    return pl.pallas_call(
        body, name="all_reduce_small", in_specs=[vm], out_specs=vm,
        out_shape=jax.ShapeDtypeStruct((rows, cols), _F32),
        scratch_shapes=[pltpu.VMEM((8, rows, cols), _F32), pltpu.SemaphoreType.DMA((7,)), pltpu.SemaphoreType.DMA((7,))],
    )(pack)


def _adamw_math(w, g, m, v):
    m2 = _ADAM_B1 * m + (1.0 - _ADAM_B1) * g
    v2 = _ADAM_B2 * v + (1.0 - _ADAM_B2) * (g * g)
    m_hat = m2 / (1.0 - _ADAM_B1 ** _ADAM_STEP)
    v_hat = v2 / (1.0 - _ADAM_B2 ** _ADAM_STEP)
    delta = -_ADAM_LR * (m_hat / (jnp.sqrt(v_hat) + _ADAM_EPS) + _ADAM_WD * w)
    return delta, m2, v2


def _adamw(w, g, m, v, lo=0, hi=None, into=None):
    shape = w.shape
    n_l = shape[0] if w.ndim == 3 else 1
    hi = n_l if hi is None else hi
    w2, g2, m2, v2 = (_rows2d(t) for t in (w, g, m, v))
    rows, cols = w2.shape
    per_layer = rows // n_l
    tm = _row_tile(per_layer)
    first, steps = lo * per_layer // tm, (hi - lo) * per_layer // tm
    n_into = 0 if into is None else 3

    def body(*refs):
        outs = refs[4 + n_into:]
        for ref, val in zip(outs, _adamw_math(*[r[...] for r in refs[:4]])):
            ref[...] = val

    spec = pl.BlockSpec((tm, cols), lambda i: (first + i, 0))
    sds = jax.ShapeDtypeStruct((rows, cols), _F32)
    outs = pl.pallas_call(
        body, name="adamw", grid=(steps,), in_specs=[spec] * 4 + [_ANY] * n_into, out_specs=[spec] * 3,
        out_shape=[sds] * 3, input_output_aliases={4 + k: k for k in range(n_into)},
        compiler_params=_params(("arbitrary",)),
    )(w2, g2, m2, v2, *([] if into is None else [_rows2d(t) for t in into]))
    return tuple(o.reshape(shape) for o in outs)


def kernel(x, positions, ffn1_norm, ffn1_w_in, ffn1_w_out, mix_norm, ffn2_norm, ffn2_w_in, ffn2_w_out, hyb_w_in, hyb_dw_w, hyb_dw_b, hyb_ln_g, hyb_ln_b, hyb_w_out, gdn_w_in, gdn_conv_w, gdn_A_log, gdn_dt_bias, gdn_norm_g, gdn_w_out, final_norm, loss_target, m_ffn1_norm, m_ffn1_w_in, m_ffn1_w_out, m_mix_norm, m_ffn2_norm, m_ffn2_w_in, m_ffn2_w_out, m_hyb_w_in, m_hyb_dw_w, m_hyb_dw_b, m_hyb_ln_g, m_hyb_ln_b, m_hyb_w_out, m_gdn_w_in, m_gdn_conv_w, m_gdn_A_log, m_gdn_dt_bias, m_gdn_norm_g, m_gdn_w_out, m_final_norm, v_ffn1_norm, v_ffn1_w_in, v_ffn1_w_out, v_mix_norm, v_ffn2_norm, v_ffn2_w_in, v_ffn2_w_out, v_hyb_w_in, v_hyb_dw_w, v_hyb_dw_b, v_hyb_ln_g, v_hyb_ln_b, v_hyb_w_out, v_gdn_w_in, v_gdn_conv_w, v_gdn_A_log, v_gdn_dt_bias, v_gdn_norm_g, v_gdn_w_out, v_final_norm):
    w = dict(ffn1_norm=ffn1_norm, ffn1_w_in=ffn1_w_in, ffn1_w_out=ffn1_w_out, mix_norm=mix_norm, ffn2_norm=ffn2_norm, ffn2_w_in=ffn2_w_in, ffn2_w_out=ffn2_w_out, hyb_w_in=hyb_w_in, hyb_dw_w=hyb_dw_w, hyb_dw_b=hyb_dw_b, hyb_ln_g=hyb_ln_g, hyb_ln_b=hyb_ln_b, hyb_w_out=hyb_w_out, gdn_w_in=gdn_w_in, gdn_conv_w=gdn_conv_w, gdn_A_log=gdn_A_log, gdn_dt_bias=gdn_dt_bias, gdn_norm_g=gdn_norm_g, gdn_w_out=gdn_w_out, final_norm=final_norm)
    m = dict(ffn1_norm=m_ffn1_norm, ffn1_w_in=m_ffn1_w_in, ffn1_w_out=m_ffn1_w_out, mix_norm=m_mix_norm, ffn2_norm=m_ffn2_norm, ffn2_w_in=m_ffn2_w_in, ffn2_w_out=m_ffn2_w_out, hyb_w_in=m_hyb_w_in, hyb_dw_w=m_hyb_dw_w, hyb_dw_b=m_hyb_dw_b, hyb_ln_g=m_hyb_ln_g, hyb_ln_b=m_hyb_ln_b, hyb_w_out=m_hyb_w_out, gdn_w_in=m_gdn_w_in, gdn_conv_w=m_gdn_conv_w, gdn_A_log=m_gdn_A_log, gdn_dt_bias=m_gdn_dt_bias, gdn_norm_g=m_gdn_norm_g, gdn_w_out=m_gdn_w_out, final_norm=m_final_norm)
    v = dict(ffn1_norm=v_ffn1_norm, ffn1_w_in=v_ffn1_w_in, ffn1_w_out=v_ffn1_w_out, mix_norm=v_mix_norm, ffn2_norm=v_ffn2_norm, ffn2_w_in=v_ffn2_w_in, ffn2_w_out=v_ffn2_w_out, hyb_w_in=v_hyb_w_in, hyb_dw_w=v_hyb_dw_w, hyb_dw_b=v_hyb_dw_b, hyb_ln_g=v_hyb_ln_g, hyb_ln_b=v_hyb_ln_b, hyb_w_out=v_hyb_w_out, gdn_w_in=v_gdn_w_in, gdn_conv_w=v_gdn_conv_w, gdn_A_log=v_gdn_A_log, gdn_dt_bias=v_gdn_dt_bias, gdn_norm_g=v_gdn_norm_g, gdn_w_out=v_gdn_w_out, final_norm=v_final_norm)
    return _train_step(x, positions, loss_target, w, m, v)


_WEIGHT_ORDER = ("ffn1_norm", "ffn1_w_in", "ffn1_w_out", "mix_norm", "ffn2_norm", "ffn2_w_in", "ffn2_w_out",
                 "hyb_w_in", "hyb_dw_w", "hyb_dw_b", "hyb_ln_g", "hyb_ln_b", "hyb_w_out", "gdn_w_in", "gdn_conv_w",
                 "gdn_A_log", "gdn_dt_bias", "gdn_norm_g", "gdn_w_out", "final_norm")
_BIG = ("ffn1_w_in", "ffn1_w_out", "ffn2_w_in", "ffn2_w_out", "hyb_w_in", "hyb_w_out", "gdn_w_in", "gdn_w_out")
_PACK_COLS = 1024


def _lane16(vec):
    return jnp.zeros((1, 128), _F32).at[0, _GDN_HEADS:2 * _GDN_HEADS].set(vec)


def _pack_rows(parts):
    rows, where, at = [], [], 0
    for p in parts:
        flat = p.reshape(-1).astype(_F32)
        n = -(-flat.shape[0] // (8 * _PACK_COLS)) * 8
        rows.append(jnp.pad(flat, (0, n * _PACK_COLS - flat.shape[0])).reshape(n, _PACK_COLS))
        where.append((at, n, flat.shape[0], p.shape))
        at += n
    return jnp.concatenate(rows, axis=0), where


def _unpack_rows(pack, where):
    return [pack[at:at + n].reshape(-1)[:size].reshape(shape) for at, n, size, shape in where]


def _train_step(x, positions, loss_target, w, m, v):
    n_layers = w["ffn1_norm"].shape[0]
    n_hyb, n_gdn = w["hyb_w_in"].shape[0], w["gdn_w_in"].shape[0]
    d_model = x.shape[-1]
    xh, target = x[0], loss_target[0]
    chip = 2 * lax.axis_index("x") + lax.axis_index("y")
    where = jnp.stack([lax.axis_index("c"), chip]).astype(jnp.int32)

    groups = (("ffn1_w_in", "ffn1_w_out"), ("hyb_w_in", "hyb_w_out"), ("ffn2_w_in", "ffn2_w_out"),
              ("gdn_w_in", "gdn_w_out"))
    steps = [(0, (0, 2)), (0, (1, 2)), (1, (0, 1)), (2, (0, 1)), (3, (0, 1))]
    bufs = [[_cast_slab(n, w[n], where) for n in names] for names in groups]
    pending, token = [], where
    for si, (gi, part) in enumerate(steps):
        send, recv, bufs[gi], token = _gather_start(f"gather_start_{si}", bufs[gi], token, part)
        pending.append((send, recv))
    gdn_cols = w["gdn_w_in"].shape[-1]
    n_qkvz = 4 * _GDN_HEADS * _GDN_DIM
    n_ba = 4 * gdn_cols - n_qkvz
    wv = {}

    def arrive(si, after):
        gi, part = steps[si]
        send, recv = pending[si]
        bufs[gi] = _gather_wait(f"gather_wait_{si}", send, recv, bufs[gi], after, part)
        bufs[gi] = _gather_forward(f"gather_pass_{si}", bufs[gi], part)
        a, b = bufs[gi]
        kind = groups[gi][0]
        if kind == "gdn_w_in":
            gdn_in = jnp.transpose(a, (0, 2, 1, 3)).reshape(n_gdn, d_model, 4 * gdn_cols)
            wv["gdn_qkvz"] = gdn_in[:, :, :n_qkvz]
            wv["gdn_ba"] = jnp.pad(gdn_in[:, :, n_qkvz:], ((0, 0), (0, 0), (0, 128 - n_ba)))
        else:
            wv[kind] = a
        wv[groups[gi][1]] = b.reshape(b.shape[0], -1, d_model)

    def own_columns(shard):
        cols = shard.shape[-1]
        zeros = jnp.zeros(shard.shape[:-1] + (4 * cols,), _F32)
        return lax.dynamic_update_slice_in_dim(zeros, shard, chip * cols, axis=2)

    small_pack, small_where = _pack_rows([own_columns(w["hyb_dw_w"]), own_columns(w["gdn_conv_w"])])
    dw_w, conv_w = _unpack_rows(0.5 * _all_reduce_small(small_pack), small_where)
    dw_w = jnp.pad(dw_w, ((0, 0), (0, _HALO - _CONV_WIDTH), (0, 0)))
    tabs = _rotary_tables(positions)

    def mixer_args(layer):
        i = layer // 2
        if layer % 2 == 0:
            return (wv["hyb_w_in"], wv["hyb_w_out"], dw_w[i], w["hyb_dw_b"][i][None], w["hyb_ln_g"][i][None],
                    w["hyb_ln_b"][i][None], tabs, i)
        return (wv["gdn_qkvz"], wv["gdn_ba"], wv["gdn_w_out"], conv_w[i], _lane16(w["gdn_A_log"][i]),
                _lane16(w["gdn_dt_bias"][i]), w["gdn_norm_g"][i][None], i)

    h = xh
    saved = []
    arrive(0, token)
    for layer in range(n_layers):
        if layer == 1:
            arrive(1, h)
        h, s1 = _ffn_fwd(h, w["ffn1_norm"][layer][None], wv["ffn1_w_in"], wv["ffn1_w_out"], layer)
        gamma = w["mix_norm"][layer][None]
        if layer % 2 == 0:
            if layer == 0:
                arrive(2, h)
            h, sm = _hyb_fwd(h, gamma, *mixer_args(layer))
        else:
            if layer == 1:
                arrive(4, h)
            h, sm = _gdn_fwd(h, gamma, *mixer_args(layer))
        if layer == 0:
            arrive(3, h)
        h, s2 = _ffn_fwd(h, w["ffn2_norm"][layer][None], wv["ffn2_w_in"], wv["ffn2_w_out"], layer)
        saved.append((s1, sm, s2))
    dh, loss_row, d_final = _loss_head(h, w["final_norm"][None], target)

    d_norm = {k: [None] * n_layers for k in ("ffn1_norm", "mix_norm", "ffn2_norm")}
    d_hyb = [None] * n_hyb
    d_gdn = [None] * n_gdn
    grads = {name: None for name in _BIG}
    n_of = {name: w[name].shape[0] for name in _BIG}

    def rs_begin(layer, full, token):
        names = list(full)
        got = _rs_sibling(f"rs_sib_{layer}", [full[n] for n in names])
        sums = [_pair_sum(full[n], g, where) for n, g in zip(names, got)]
        send, recv, sums, lands, token = _rs_chips_start(f"rs_chips_start_{layer}", sums, token)
        return (layer, names, send, recv, sums, lands), token

    def rs_end(state, after):
        layer, names, send, recv, sums, lands = state
        sums, lands = _rs_chips_wait(f"rs_chips_wait_{layer}", send, recv, sums, lands, after)
        index = [layer if n.startswith("ffn") else layer // 2 for n in names]
        for n, li, pair, land in zip(names, index, sums, lands):
            grads[n] = _sum_of_four(pair, land, where, grads[n], li, n_of[n])
        joined = _rs_join(f"rs_join_{layer}", [grads[n] for n in names], index)
        grads.update(zip(names, joined))

    in_flight = None
    for layer in reversed(range(n_layers)):
        s1, sm, s2 = saved[layer]
        full = {}
        dh, d_norm["ffn2_norm"][layer], full["ffn2_w_in"], full["ffn2_w_out"] = _ffn_bwd(
            dh, s2, w["ffn2_norm"][layer][None], wv["ffn2_w_in"], wv["ffn2_w_out"], layer)
        gamma = w["mix_norm"][layer][None]
        if layer % 2 == 0:
            dh, d_norm["mix_norm"][layer], d_hyb[layer // 2], full["hyb_w_in"], full["hyb_w_out"] = _hyb_bwd(
                dh, sm, gamma, *mixer_args(layer))
        else:
            dh, d_norm["mix_norm"][layer], d_gdn[layer // 2], full["gdn_w_in"], full["gdn_w_out"] = _gdn_bwd(
                dh, sm, gamma, *mixer_args(layer), n_ba)
        dh, d_norm["ffn1_norm"][layer], full["ffn1_w_in"], full["ffn1_w_out"] = _ffn_bwd(
            dh, s1, w["ffn1_norm"][layer][None], wv["ffn1_w_in"], wv["ffn1_w_out"], layer)
        if in_flight is not None:
            rs_end(in_flight, dh)
        in_flight, token = rs_begin(layer, full, token)
        dh, token = lax.optimization_barrier((dh, token))
    grad_x = dh[None]

    stack = lambda rows: jnp.concatenate(rows, axis=0)
    misc = jnp.concatenate([stack([g[1] for g in d_gdn])[:, _GDN_HEADS:2 * _GDN_HEADS].reshape(-1),
                            stack([g[2] for g in d_gdn])[:, _GDN_HEADS:2 * _GDN_HEADS].reshape(-1),
                            stack([g[3] for g in d_gdn]).reshape(-1), loss_row[0, :1]])
    n_a = n_gdn * _GDN_HEADS
    parts = [stack(d_norm["ffn1_norm"]), stack(d_norm["mix_norm"]), stack(d_norm["ffn2_norm"]), d_final,
             stack([g[1] for g in d_hyb]), stack([g[2] for g in d_hyb]), stack([g[3] for g in d_hyb]), misc,
             jnp.stack([g[0][:_CONV_WIDTH] for g in d_hyb]), jnp.stack([g[0] for g in d_gdn])]
    pack, pack_where = _pack_rows(parts)
    red = _unpack_rows(_all_reduce_small(pack), pack_where)
    misc = red[7]
    cw = w["hyb_dw_w"].shape[-1]
    gw = w["gdn_conv_w"].shape[-1]
    grads.update(
        ffn1_norm=red[0], mix_norm=red[1], ffn2_norm=red[2], final_norm=red[3].reshape(-1),
        hyb_dw_b=red[4], hyb_ln_g=red[5], hyb_ln_b=red[6],
        gdn_A_log=misc[:n_a].reshape(n_gdn, _GDN_HEADS), gdn_dt_bias=misc[n_a:2 * n_a].reshape(n_gdn, _GDN_HEADS),
        gdn_norm_g=misc[2 * n_a:2 * n_a + n_gdn * _GDN_DIM].reshape(n_gdn, _GDN_DIM),
        hyb_dw_w=lax.dynamic_slice_in_dim(red[8], chip * cw, cw, axis=2),
        gdn_conv_w=lax.dynamic_slice_in_dim(red[9], chip * gw, gw, axis=2))
    loss = misc[2 * n_a + n_gdn * _GDN_DIM]

    small = [n for n in _WEIGHT_ORDER if n not in _BIG]
    packs = [_pack_rows([src[n] for n in small]) for src in (w, grads, m, v)]
    outs = _adamw(*[p[0] for p in packs])
    delta, new_m, new_v = {}, {}, {}
    for res, o in zip((delta, new_m, new_v), outs):
        res.update(dict(zip(small, _unpack_rows(o, packs[0][1]))))
    late = in_flight[1]
    early = {name: _adamw(w[name], grads[name], m[name], v[name], lo=1 if name in late else 0) for name in _BIG}
    early, outs, token = lax.optimization_barrier((early, outs, token))
    rs_end(in_flight, token)
    for name in _BIG:
        res = early[name]
        if name in late:
            res = _adamw(w[name], grads[name], m[name], v[name], lo=0, hi=1, into=res)
        delta[name], new_m[name], new_v[name] = res
    return (loss, grad_x, *[grads[n] for n in _WEIGHT_ORDER], *[delta[n] for n in _WEIGHT_ORDER],
            *[new_m[n] for n in _WEIGHT_ORDER], *[new_v[n] for n in _WEIGHT_ORDER])
```

```python
import functools

import jax
import jax.numpy as jnp
from jax import lax
from jax.experimental import pallas as pl
from jax.experimental.pallas import tpu as pltpu

_BF = jnp.bfloat16
_F32 = jnp.float32
_NORM_EPS = 1e-6
_VMEM_LIMIT_BYTES = 56 * 1024 * 1024

_A_HEADS = 8
_A_HEAD_DIM = 64
_WIN_BLOCK = 128
_DILATIONS = (1, 4, 16)
_CONV_WIDTH = 31
_GDN_HEADS = 8
_GDN_DIM = 128
_GDN_CHUNK = 64
_GDN_SHORT_CONV = 4

_ADAM_LR = 0.001
_ADAM_B1 = 0.9
_ADAM_B2 = 0.999
_ADAM_EPS = 1e-08
_ADAM_WD = 0.01
_ADAM_STEP = 10


def _params(sem):
    return pltpu.CompilerParams(dimension_semantics=sem, vmem_limit_bytes=_VMEM_LIMIT_BYTES)


def _dot(a, b, dn=(((1,), (0,)), ((), ()))):
    return lax.dot_general(a.astype(_BF), b.astype(_BF), dn, preferred_element_type=_F32)


def _dot_nt(a, b):
    return _dot(a, b, (((1,), (1,)), ((), ())))


def _dot_tn(a, b):
    return _dot(a, b, (((0,), (0,)), ((), ())))


_NN = (((1,), (0,)), ((), ()))
_NT = (((1,), (1,)), ((), ()))
_TN = (((0,), (0,)), ((), ()))
_ANY = pl.BlockSpec(memory_space=pl.ANY)


def _mm(name, a, b, *, grid, a_spec, b_spec, dn, out_shape, out_spec, red=None, acc_shape=None,
        pre_a=None, pre_b=None, epi=None, extras=(), extra_specs=(), sum_shape=None):
    n_ex = len(extras)
    n_in = 2 + n_ex
    nk = grid[red] if red is not None else 1
    n_out = 1 if sum_shape is None else 2

    def body(*refs):
        a_ref, b_ref = refs[0], refs[1]
        ex = refs[2:2 + n_ex]
        o_ref = refs[n_in]
        av, bv = a_ref[...], b_ref[...]
        if pre_a is not None:
            av = pre_a(av)
        if pre_b is not None:
            bv = pre_b(bv)
        part = lax.dot_general(av.astype(_BF), bv.astype(_BF), dn, preferred_element_type=_F32)

        def finish(acc):
            res = epi(acc, *[e[...] for e in ex]) if epi is not None else acc
            if sum_shape is not None:
                res, term = res
                s_ref = refs[n_in + 1]
                first = functools.reduce(jnp.logical_and, [pl.program_id(ax) == 0 for ax in range(len(grid)) if ax != red])

                @pl.when(first)
                def _():
                    s_ref[...] = jnp.zeros_like(s_ref)

                s_ref[...] += term
            if isinstance(res, tuple):
                for idx, val in enumerate(res):
                    o_ref[idx] = val.astype(o_ref.dtype)
            else:
                o_ref[...] = res.astype(o_ref.dtype)

        if red is None:
            finish(part)
        else:
            acc_ref = refs[n_in + n_out]
            k = pl.program_id(red)

            @pl.when(k == 0)
            def _():
                acc_ref[...] = part

            @pl.when(k > 0)
            def _():
                acc_ref[...] += part

            @pl.when(k == nk - 1)
            def _():
                finish(acc_ref[...])

    if sum_shape is not None:
        out_shape = [out_shape, jax.ShapeDtypeStruct(sum_shape, _F32)]
        out_spec = [out_spec, pl.BlockSpec(sum_shape, lambda *g: (0,) * len(sum_shape))]
    return pl.pallas_call(
        body, name=name, grid=grid, in_specs=[a_spec, b_spec, *extra_specs], out_specs=out_spec, out_shape=out_shape,
        scratch_shapes=[pltpu.VMEM(acc_shape, _F32)] if red is not None else [],
        compiler_params=_params(("arbitrary",) * len(grid)),
    )(a, b, *extras)


def _rowmap(name, body, tiled, params, tile_outs, param_outs=(), *, rows, tm, ncol=1, col_outer=False,
            into=None):
    nt, npar, nto, npo = len(tiled), len(params), len(tile_outs), len(param_outs)
    assert rows % tm == 0, (name, rows, tm)
    nrow = rows // tm
    grid = (ncol, nrow) if col_outer else (nrow, ncol)

    def ij(g):
        return (g[1], g[0]) if col_outer else (g[0], g[1])

    def cidx(col, j):
        return col(j) if callable(col) else col

    in_specs, operands = [], []
    for arr, width, col in tiled:
        operands.append(arr)
        if arr.ndim == 3:
            in_specs.append(pl.BlockSpec((arr.shape[0], tm, width),
                                         lambda *g, col=col: (0, ij(g)[0], cidx(col, ij(g)[1]))))
        else:
            in_specs.append(pl.BlockSpec((tm, width), lambda *g, col=col: (ij(g)[0], cidx(col, ij(g)[1]))))
    for p in params:
        if isinstance(p, tuple):
            arr, width, col = p
            operands.append(arr)
            in_specs.append(pl.BlockSpec((arr.shape[0], width), lambda *g, col=col: (0, cidx(col, ij(g)[1]))))
        else:
            operands.append(p)
            in_specs.append(pl.BlockSpec(p.shape, lambda *g, nd=p.ndim: (0,) * nd))
    out_shape, out_specs = [], []
    for total, width, col, dt in tile_outs:
        out_shape.append(jax.ShapeDtypeStruct((rows, total), dt))
        out_specs.append(pl.BlockSpec((tm, width), lambda *g, col=col: (ij(g)[0], cidx(col, ij(g)[1]))))
    for shp in param_outs:
        out_shape.append(jax.ShapeDtypeStruct(shp, _F32))
        out_specs.append(pl.BlockSpec(shp, lambda *g, nd=len(shp): (0,) * nd))
    n_in = nt + npar
    aliases = {}
    if into is not None:
        operands.append(into[0])
        in_specs.append(_ANY)
        aliases = {n_in: into[1]}
        n_in += 1

    def kern(*refs):
        vals = [r[...] for r in refs[:nt + npar]]
        touts = refs[n_in:n_in + nto]
        pouts = refs[n_in + nto:]
        res = body(*vals)
        if not isinstance(res, (tuple, list)):
            res = (res,)
        for r, val in zip(touts, res[:nto]):
            r[...] = val.astype(r.dtype)
        if npo:
            @pl.when((pl.program_id(0) == 0) & (pl.program_id(1) == 0))
            def _():
                for r in pouts:
                    r[...] = jnp.zeros_like(r)

            for r, val in zip(pouts, res[nto:]):
                r[...] += val

    return pl.pallas_call(
        kern, name=name, grid=grid, in_specs=in_specs, out_specs=out_specs, out_shape=out_shape,
        input_output_aliases=aliases, compiler_params=_params(("arbitrary", "arbitrary")),
    )(*operands)


def _vjp_body(fn, n_in, diff, n_ct):
    def body(*vals):
        ins, cts = vals[:n_in], vals[n_in:n_in + n_ct]
        outs, pull = jax.vjp(fn, *ins)
        single = not isinstance(outs, (tuple, list))
        if single:
            grads = pull(cts[0].astype(outs.dtype))
        else:
            grads = pull(tuple(c.astype(o.dtype) for c, o in zip(cts, outs)))
        return tuple(grads[i] for i in diff)
    return body


_TM = 512
_TM_WIDE = 1024


def _rms(h, gamma):
    hf = h.astype(_F32)
    return hf * lax.rsqrt(jnp.mean(hf * hf, axis=-1, keepdims=True) + _NORM_EPS) * gamma


def _rmsnorm_fwd(h, gamma):
    s, d = h.shape
    return _rowmap("rmsnorm_fwd", lambda hv, gv: _rms(hv, gv).astype(_BF), [(h, d, 0)], [gamma],
                   [(d, d, 0, _BF)], rows=s, tm=_TM)[0]


def _norm_bwd_epi(dxn, hv, dhv, gv):
    _, pull = jax.vjp(_rms, hv, gv)
    dh, dg = pull(dxn)
    return dhv + dh, dg


def _norm_bwd_args(h, dh_out, gamma, tm):
    d = h.shape[1]
    row = pl.BlockSpec((tm, d), lambda i, j: (i, 0))
    return dict(extras=(h, dh_out, gamma), extra_specs=(row, row, pl.BlockSpec((1, d), lambda i, j: (0, 0))),
                sum_shape=(1, d))


def _silu(x):
    return x * jax.nn.sigmoid(x)


def _ffn_up(xn, w_in, layer):
    s, d = xn.shape
    ns = w_in.shape[-1]

    def body(x_ref, wg_ref, wu_ref, g_ref, u_ref, a_ref):
        xv = x_ref[...]
        g = jnp.dot(xv, wg_ref[...], preferred_element_type=_F32)
        u = jnp.dot(xv, wu_ref[...], preferred_element_type=_F32)
        g_ref[...] = g.astype(_BF)
        u_ref[...] = u.astype(_BF)
        a_ref[...] = (_silu(g) * u).astype(_BF)

    out = pl.BlockSpec((_TM, ns), lambda j, i: (i, j))
    sds = jax.ShapeDtypeStruct((s, 2 * ns), _BF)
    return pl.pallas_call(
        body, name="ffn_up", grid=(2, s // _TM),
        in_specs=[pl.BlockSpec((_TM, d), lambda j, i: (i, 0)),
                  pl.BlockSpec((None, None, d, ns), lambda j, i: (layer, j, 0, 0)),
                  pl.BlockSpec((None, None, d, ns), lambda j, i: (layer, j + 2, 0, 0))],
        out_specs=[out, out, out], out_shape=[sds, sds, sds],
        compiler_params=_params(("arbitrary", "arbitrary")),
    )(xn, w_in, w_in)


def _proj_residual(name, y, w, layer, h, scale):
    s, k = y.shape
    d = w.shape[-1]
    return _mm(name, y, w, grid=(s // _TM,), a_spec=pl.BlockSpec((_TM, k), lambda i: (i, 0)),
               b_spec=pl.BlockSpec((None, k, d), lambda i: (layer, 0, 0)), dn=_NN,
               out_shape=jax.ShapeDtypeStruct((s, d), _F32), out_spec=pl.BlockSpec((_TM, d), lambda i: (i, 0)),
               epi=lambda acc, hv: hv + scale * acc, extras=(h,),
               extra_specs=(pl.BlockSpec((_TM, d), lambda i: (i, 0)),))


def _ffn_dact(dh, w_out, layer, g, u):
    s, d = dh.shape
    f = g.shape[1]
    ns = f // 2

    def epi(da, gv, uv):
        gf, uf = gv.astype(_F32), uv.astype(_F32)
        sg = jax.nn.sigmoid(gf)
        return da * uf * (sg * (1.0 + gf * (1.0 - sg))), da * gf * sg

    gu = pl.BlockSpec((_TM, ns), lambda n, i: (i, n))
    return _mm("ffn_dact", dh, w_out, grid=(2, s // _TM), a_spec=pl.BlockSpec((_TM, d), lambda n, i: (i, 0)),
               b_spec=pl.BlockSpec((None, ns, d), lambda n, i: (layer, n, 0)), dn=_NT,
               out_shape=jax.ShapeDtypeStruct((2, s, f), _BF), out_spec=pl.BlockSpec((2, _TM, ns), lambda n, i: (0, i, n)),
               pre_a=lambda v: 0.5 * v, epi=epi, extras=(g, u), extra_specs=(gu, gu))


def _wgrad_rows(name, y, dh, scale):
    s, k = y.shape
    d = dh.shape[1]
    out = _mm(name, y, dh, grid=(s // _TM,), a_spec=pl.BlockSpec((_TM, k), lambda i: (i, 0)),
              b_spec=pl.BlockSpec((_TM, d), lambda i: (i, 0)), dn=_TN, red=0, acc_shape=(k, d),
              out_shape=jax.ShapeDtypeStruct((k, d), _BF), out_spec=pl.BlockSpec((k, d), lambda i: (0, 0)),
              pre_b=(lambda v: scale * v) if scale != 1.0 else None)
    return out.reshape(4, k // 4, d)


def _ffn_dxn(dz, w_in, layer, h, gamma, dh_out):
    _, s, f = dz.shape
    d, ns = w_in.shape[2], w_in.shape[3]
    tm = _TM_WIDE
    return _mm("ffn_dxn", dz, w_in, grid=(s // tm, 4),
               a_spec=pl.BlockSpec((None, tm, ns), lambda i, j: (j // 2, i, j % 2)),
               b_spec=pl.BlockSpec((None, None, d, ns), lambda i, j: (layer, j, 0, 0)), dn=_NT, red=1,
               acc_shape=(tm, d), out_shape=jax.ShapeDtypeStruct((s, d), _F32),
               out_spec=pl.BlockSpec((tm, d), lambda i, j: (i, 0)), epi=_norm_bwd_epi,
               **_norm_bwd_args(h, dh_out, gamma, tm))


def _ffn_dwin(xn, dz):
    s, d = xn.shape
    ns = dz.shape[2] // 2
    tm = _TM_WIDE
    return _mm("ffn_dwin", xn, dz, grid=(4, s // tm), a_spec=pl.BlockSpec((tm, d), lambda j, i: (i, 0)),
               b_spec=pl.BlockSpec((None, tm, ns), lambda j, i: (j // 2, i, j % 2)), dn=_TN, red=1,
               acc_shape=(d, ns), out_shape=jax.ShapeDtypeStruct((4, d, ns), _BF),
               out_spec=pl.BlockSpec((None, d, ns), lambda j, i: (j, 0, 0)))


def _ffn_fwd(h, gamma, w_in, w_out, layer):
    xn = _rmsnorm_fwd(h, gamma)
    g, u, act = _ffn_up(xn, w_in, layer)
    h_new = _proj_residual("ffn_down", act, w_out, layer, h, 0.5)
    return h_new, (h, xn, g, u, act)


def _ffn_bwd(dh, saved, gamma, w_in, w_out, layer):
    h, xn, g, u, act = saved
    dz = _ffn_dact(dh, w_out, layer, g, u)
    d_w_out = _wgrad_rows("ffn_dwout", act, dh, 0.5)
    dh_in, dgamma = _ffn_dxn(dz, w_in, layer, h, gamma, dh)
    d_w_in = _ffn_dwin(xn, dz)
    return dh_in, dgamma, d_w_in, d_w_out


def _loss_head(h, gamma, target):
    s, d = h.shape

    def body(hv, tv, gv):
        y, pull = jax.vjp(_rms, hv, gv)
        err = y - tv
        loss = 0.5 * jnp.sum(jnp.mean(err * err, axis=-1, keepdims=True), axis=0, keepdims=True)
        dh, dg = pull(err * (1.0 / d))
        return dh, jnp.broadcast_to(loss, (1, 128)), dg

    return _rowmap("loss_head", body, [(h, d, 0), (target, d, 0)], [gamma], [(d, d, 0, _F32)],
                   [(1, 128), (1, d)], rows=s, tm=_TM)


_ROT = _A_HEAD_DIM // 4
_ROPE_THETA = 500000.0
_A_WIDTH = _A_HEADS * _A_HEAD_DIM
_NEG = -1e30


def _rotary_tables(positions):
    inv_freq = jnp.power(jnp.float32(_ROPE_THETA), -jnp.arange(0, _ROT, 2, dtype=_F32) / _ROT)
    ang = positions.reshape(-1, 1).astype(_F32) * inv_freq
    cos, sin = jnp.cos(ang), jnp.sin(ang)
    s = ang.shape[0]
    pad = jnp.zeros((s, _A_HEAD_DIM - _ROT), _F32)
    c_head = jnp.concatenate([cos, cos, pad + 1.0], axis=1)
    s_head = jnp.concatenate([-sin, sin, pad], axis=1)
    c2, s2 = jnp.tile(c_head, (1, 2)), jnp.tile(s_head, (1, 2))
    scale = _A_HEAD_DIM ** -0.5
    tab_c = jnp.stack([scale * c2, c2, jnp.ones_like(c2)])
    tab_s = jnp.stack([scale * s2, s2, jnp.zeros_like(s2)])
    return tab_c, tab_s


def _pair_swap(z):
    lane = lax.broadcasted_iota(jnp.int32, z.shape, 1) % _A_HEAD_DIM
    half = _ROT // 2
    up = jnp.where(lane < _ROT, pltpu.roll(z, half, axis=1), 0.0)
    return jnp.where(lane < half, pltpu.roll(z, 128 - half, axis=1), up)


def _rot_apply(x, c, s, transpose):
    outs = []
    for k in range(x.shape[1] // 128):
        xs = x[:, k * 128:(k + 1) * 128]
        outs.append(c * xs + (_pair_swap(s * xs) if transpose else s * _pair_swap(xs)))
    return jnp.concatenate(outs, axis=1)


def _qkv_prep(proj, tab_c, tab_s):
    s = proj.shape[0]
    w = _A_WIDTH

    def body(x_ref, c_ref, s_ref, o_ref):
        o_ref[...] = _rot_apply(x_ref[...], c_ref[...], s_ref[...], False).astype(_BF)

    tab = pl.BlockSpec((None, _TM, 128), lambda i, j: (j, i, 0))
    blk = pl.BlockSpec((_TM, w), lambda i, j: (i, j))
    return pl.pallas_call(
        body, name="qkv_prep", grid=(s // _TM, 3), in_specs=[blk, tab, tab], out_specs=blk,
        out_shape=jax.ShapeDtypeStruct((s, 3 * w), _BF), compiler_params=_params(("arbitrary", "arbitrary")),
    )(proj, tab_c, tab_s)


def _head_masks():
    lane = lax.broadcasted_iota(jnp.int32, (_WIN_BLOCK, 128), 1)
    return [lane // _A_HEAD_DIM == hh for hh in range(2)]


def _band_masks(first_block):
    qi = lax.broadcasted_iota(jnp.int32, (_WIN_BLOCK, _WIN_BLOCK), 0)
    kj = lax.broadcasted_iota(jnp.int32, (_WIN_BLOCK, _WIN_BLOCK), 1)
    return (kj >= qi) & jnp.logical_not(first_block), kj <= qi


def _attn_fwd(qkv, dil):
    s = qkv.shape[0]
    w = _A_WIDTH
    length = s // dil
    nb = length // _WIN_BLOCK
    view = qkv.reshape(length, dil * 3 * w)

    def body(q_ref, kp_ref, kc_ref, vp_ref, vc_ref, o_ref, l_ref):
        allow_p, allow_c = _band_masks(pl.program_id(1) == 0)
        hm = _head_masks()
        for sl in range(w // 128):
            cols = slice(sl * 128, (sl + 1) * 128)
            q, kp, kc, vp, vc = q_ref[:, cols], kp_ref[:, cols], kc_ref[:, cols], vp_ref[:, cols], vc_ref[:, cols]
            o_s = jnp.zeros((_WIN_BLOCK, 128), _F32)
            l_s = jnp.zeros((_WIN_BLOCK, 128), _F32)
            for hh in range(2):
                qh = jnp.where(hm[hh], q, jnp.zeros_like(q))
                sp = jnp.where(allow_p, _dot_nt(qh, kp), _NEG)
                sc = jnp.where(allow_c, _dot_nt(qh, kc), _NEG)
                m = jnp.maximum(jnp.max(sp, axis=1, keepdims=True), jnp.max(sc, axis=1, keepdims=True))
                pp, pc = jnp.exp(sp - m), jnp.exp(sc - m)
                den = jnp.sum(pp, axis=1, keepdims=True) + jnp.sum(pc, axis=1, keepdims=True)
                oh = (_dot(pp, vp) + _dot(pc, vc)) / den
                o_s = jnp.where(hm[hh], oh, o_s)
                l_s = jnp.where(hm[hh], m + jnp.log(den), l_s)
            o_ref[:, cols] = o_s
            l_ref[:, cols] = l_s

    def at(col, prev):
        if prev:
            return pl.BlockSpec((_WIN_BLOCK, w), lambda r, n: (jnp.maximum(n - 1, 0), 3 * r + col))
        return pl.BlockSpec((_WIN_BLOCK, w), lambda r, n: (n, 3 * r + col))

    out = pl.BlockSpec((_WIN_BLOCK, w), lambda r, n: (n, r))
    sds = jax.ShapeDtypeStruct((length, dil * w), _F32)
    o, lse = pl.pallas_call(
        body, name=f"attn_fwd_d{dil}", grid=(dil, nb),
        in_specs=[at(0, False), at(1, True), at(1, False), at(2, True), at(2, False)],
        out_specs=[out, out], out_shape=[sds, sds], compiler_params=_params(("arbitrary", "arbitrary")),
    )(view, view, view, view, view)
    return o.reshape(s, w), lse.reshape(s, w)


def _attn_bwd(qkv, o, lse, do, dlse, dil):
    s = qkv.shape[0]
    w = _A_WIDTH
    length = s // dil
    nb = length // _WIN_BLOCK
    view = qkv.reshape(length, dil * 3 * w)
    o, lse, do, dlse = (t.reshape(length, dil * w) for t in (o, lse, do, dlse))

    def body(q_ref, kp_ref, kc_ref, vp_ref, vc_ref, o_ref, l_ref, do_ref, dl_ref, out_ref,
             car_q, car_k, car_v, cur_q, cur_kc, cur_vc, cur_kp, cur_vp):
        n = pl.program_id(1)

        @pl.when(n == 0)
        def _():
            car_q[...] = jnp.zeros_like(car_q)
            car_k[...] = jnp.zeros_like(car_k)
            car_v[...] = jnp.zeros_like(car_v)

        @pl.when(n == nb)
        def _():
            cur_kp[...] = jnp.zeros_like(cur_kp)
            cur_vp[...] = jnp.zeros_like(cur_vp)

        @pl.when(n < nb)
        def _():
            allow_p, allow_c = _band_masks(n == 0)
            hm = _head_masks()
            for sl in range(w // 128):
                cols = slice(sl * 128, (sl + 1) * 128)
                q, kp, kc, vp, vc = q_ref[:, cols], kp_ref[:, cols], kc_ref[:, cols], vp_ref[:, cols], vc_ref[:, cols]
                ov, lv, dov, dlv = o_ref[:, cols], l_ref[:, cols], do_ref[:, cols], dl_ref[:, cols]
                zero = jnp.zeros((_WIN_BLOCK, 128), _F32)
                dq_s, dkp_s, dkc_s, dvp_s, dvc_s = zero, zero, zero, zero, zero
                for hh in range(2):
                    qh = jnp.where(hm[hh], q, jnp.zeros_like(q))
                    doh = jnp.where(hm[hh], dov, 0.0).astype(_BF)
                    lse_h = jnp.sum(jnp.where(hm[hh], lv, 0.0), axis=1, keepdims=True) * (1.0 / _A_HEAD_DIM)
                    delta = jnp.sum(jnp.where(hm[hh], dov * ov, 0.0), axis=1, keepdims=True)
                    dl = jnp.sum(jnp.where(hm[hh], dlv, 0.0), axis=1, keepdims=True)
                    pp = jnp.exp(jnp.where(allow_p, _dot_nt(qh, kp), _NEG) - lse_h)
                    pc = jnp.exp(jnp.where(allow_c, _dot_nt(qh, kc), _NEG) - lse_h)
                    dsp = (pp * (_dot_nt(doh, vp) - delta + dl)).astype(_BF)
                    dsc = (pc * (_dot_nt(doh, vc) - delta + dl)).astype(_BF)
                    dq_s = dq_s + jnp.where(hm[hh], _dot(dsp, kp) + _dot(dsc, kc), 0.0)
                    dkp_s = dkp_s + _dot_tn(dsp, qh)
                    dkc_s = dkc_s + _dot_tn(dsc, qh)
                    dvp_s = dvp_s + _dot_tn(pp, doh)
                    dvc_s = dvc_s + _dot_tn(pc, doh)
                cur_q[:, cols] = dq_s
                cur_kp[:, cols] = dkp_s
                cur_kc[:, cols] = dkc_s
                cur_vp[:, cols] = dvp_s
                cur_vc[:, cols] = dvc_s

        out_ref[:, 0:w] = car_q[...]
        out_ref[:, w:2 * w] = car_k[...] + cur_kp[...]
        out_ref[:, 2 * w:3 * w] = car_v[...] + cur_vp[...]

        @pl.when(n < nb)
        def _():
            car_q[...] = cur_q[...]
            car_k[...] = cur_kc[...]
            car_v[...] = cur_vc[...]

    def at(col, prev):
        if prev:
            return pl.BlockSpec((_WIN_BLOCK, w), lambda r, n: (jnp.maximum(jnp.minimum(n, nb - 1) - 1, 0), 3 * r + col))
        return pl.BlockSpec((_WIN_BLOCK, w), lambda r, n: (jnp.minimum(n, nb - 1), 3 * r + col))

    blk = pl.BlockSpec((_WIN_BLOCK, w), lambda r, n: (jnp.minimum(n, nb - 1), r))
    scratch = [pltpu.VMEM((_WIN_BLOCK, w), _F32) for _ in range(8)]
    out = pl.pallas_call(
        body, name=f"attn_bwd_d{dil}", grid=(dil, nb + 1),
        in_specs=[at(0, False), at(1, True), at(1, False), at(2, True), at(2, False), blk, blk, blk, blk],
        out_specs=pl.BlockSpec((_WIN_BLOCK, 3 * w), lambda r, n: (jnp.maximum(n - 1, 0), r)),
        out_shape=jax.ShapeDtypeStruct((length, dil * 3 * w), _F32), scratch_shapes=scratch,
        compiler_params=_params(("arbitrary", "arbitrary")),
    )(view, view, view, view, view, o, lse, do, dlse)
    return out.reshape(s, 3 * w)


def _mix(o1, o2, o3, l1, l2, l3):
    m = jnp.maximum(jnp.maximum(l1, l2), l3)
    e1, e2, e3 = jnp.exp(l1 - m), jnp.exp(l2 - m), jnp.exp(l3 - m)
    return (e1 * o1 + e2 * o2 + e3 * o3) / (e1 + e2 + e3)


def _attn_mix_fwd(outs, lses):
    s = outs[0].shape[0]
    tiles = [(t, _A_WIDTH, 0) for t in (*outs, *lses)]
    return _rowmap("attn_mix_fwd", _mix, tiles, [], [(2 * _A_WIDTH, _A_WIDTH, 0, _BF)], rows=s, tm=_CONV_TM)[0]


def _attn_mix_bwd(outs, lses, dy):
    s = outs[0].shape[0]
    tiles = [(t, _A_WIDTH, 0) for t in (*outs, *lses, dy)]
    body = _vjp_body(_mix, 6, range(6), 1)
    return _rowmap("attn_mix_bwd", body, tiles, [], [(_A_WIDTH, _A_WIDTH, 0, _F32)] * 6, rows=s, tm=_CONV_TM)


_CONV_TM = 256
_HALO = 32


def _ln_silu(pre, g, b):
    mu = jnp.mean(pre, axis=-1, keepdims=True)
    xc = pre - mu
    y = xc * lax.rsqrt(jnp.mean(xc * xc, axis=-1, keepdims=True) + _NORM_EPS)
    return _silu(y * g + b)


def _glu_into(buf, ua_ref, ub_ref, uah_ref, ubh_ref, first):
    halo = uah_ref[...] * jax.nn.sigmoid(ubh_ref[...])
    buf[0:_HALO, :] = jnp.where(first, 0.0, halo)
    buf[_HALO:, :] = ua_ref[...] * jax.nn.sigmoid(ub_ref[...])


_SHIFT_ROWS = _CONV_TM + _HALO - 8


def _shift_copies(sh, buf):
    for b in range(1, 8):
        sh[b - 1] = buf[b:b + _SHIFT_ROWS, :]


def _rows_at(buf, sh, off, rows):
    a, b = divmod(off, 8)
    return buf[8 * a:8 * a + rows, :] if b == 0 else sh[b - 1, 8 * a:8 * a + rows, :]


def _causal_taps(buf, sh, w_ref, rows):
    base = _HALO - (_CONV_WIDTH - 1)
    acc = jnp.zeros((rows, buf.shape[1]), _F32)
    for j in range(_CONV_WIDTH):
        acc = acc + w_ref[j:j + 1, :] * _rows_at(buf, sh, base + j, rows)
    return acc


def _conv_specs(s):
    tm = _CONV_TM
    per = tm // _HALO
    cw = _A_WIDTH
    cur = lambda col: pl.BlockSpec((tm, cw), lambda i: (i, col))
    prev = lambda col: pl.BlockSpec((_HALO, cw), lambda i: (jnp.maximum(i * per - 1, 0), col))
    return tm, per, cw, cur, prev


def _full2(arr):
    return pl.BlockSpec(arr.shape, lambda i: (0, 0))


def _conv_fwd(proj, dw_w, dw_b, ln_g, ln_b, y):
    s = proj.shape[0]
    tm, per, cw, cur, prev = _conv_specs(s)

    def body(ua_ref, ub_ref, uah_ref, ubh_ref, w_ref, b_ref, g_ref, be_ref, y_in, o_ref, buf, sh):
        del y_in
        _glu_into(buf, ua_ref, ub_ref, uah_ref, ubh_ref, pl.program_id(0) == 0)
        _shift_copies(sh, buf)
        pre = _causal_taps(buf, sh, w_ref, tm) + b_ref[...]
        o_ref[...] = _ln_silu(pre, g_ref[...], be_ref[...]).astype(_BF)

    return pl.pallas_call(
        body, name="conv_fwd", grid=(s // tm,),
        in_specs=[cur(3), cur(4), prev(3), prev(4), _full2(dw_w), _full2(dw_b), _full2(ln_g), _full2(ln_b), _ANY],
        out_specs=pl.BlockSpec((tm, cw), lambda i: (i, 1)), out_shape=jax.ShapeDtypeStruct(y.shape, y.dtype),
        scratch_shapes=[pltpu.VMEM((tm + _HALO, cw), _F32), pltpu.VMEM((7, _SHIFT_ROWS, cw), _F32)],
        input_output_aliases={8: 0}, compiler_params=_params(("arbitrary",)),
    )(proj, proj, proj, proj, dw_w, dw_b, ln_g, ln_b, y)


def _conv_bwd_pre(proj, dw_w, dw_b, ln_g, ln_b, dy):
    s = proj.shape[0]
    tm, per, cw, cur, prev = _conv_specs(s)

    def body(ua_ref, ub_ref, uah_ref, ubh_ref, w_ref, b_ref, g_ref, be_ref, dy_ref, dp_ref, dg_ref, dbe_ref, db_ref, buf, sh):
        i = pl.program_id(0)
        _glu_into(buf, ua_ref, ub_ref, uah_ref, ubh_ref, i == 0)
        _shift_copies(sh, buf)
        pre = _causal_taps(buf, sh, w_ref, tm) + b_ref[...]
        _, pull = jax.vjp(_ln_silu, pre, g_ref[...], be_ref[...])
        dpre, dg, dbe = pull(dy_ref[...])
        dp_ref[...] = dpre

        @pl.when(i == 0)
        def _():
            dg_ref[...] = jnp.zeros_like(dg_ref)
            dbe_ref[...] = jnp.zeros_like(dbe_ref)
            db_ref[...] = jnp.zeros_like(db_ref)

        dg_ref[...] += dg
        dbe_ref[...] += dbe
        db_ref[...] += jnp.sum(dpre, axis=0, keepdims=True)

    vec = pl.BlockSpec((1, cw), lambda i: (0, 0))
    vsd = jax.ShapeDtypeStruct((1, cw), _F32)
    return pl.pallas_call(
        body, name="conv_bwd_pre", grid=(s // tm,),
        in_specs=[cur(3), cur(4), prev(3), prev(4), _full2(dw_w), _full2(dw_b), _full2(ln_g), _full2(ln_b), cur(1)],
        out_specs=[pl.BlockSpec((tm, cw), lambda i: (i, 0)), vec, vec, vec],
        out_shape=[jax.ShapeDtypeStruct((s, cw), _F32), vsd, vsd, vsd],
        scratch_shapes=[pltpu.VMEM((tm + _HALO, cw), _F32), pltpu.VMEM((7, _SHIFT_ROWS, cw), _F32)],
        compiler_params=_params(("arbitrary",)),
    )(proj, proj, proj, proj, dw_w, dw_b, ln_g, ln_b, dy)


def _conv_bwd_taps(proj, dw_w, dpre):
    s = proj.shape[0]
    tm, per, cw, cur, prev = _conv_specs(s)
    nt = s // tm

    def body(ua_ref, ub_ref, uah_ref, ubh_ref, w_ref, dp_ref, dpn_ref, dglu_ref, dw_ref, buf, dbuf, sh, dsh):
        i = pl.program_id(0)
        _glu_into(buf, ua_ref, ub_ref, uah_ref, ubh_ref, i == 0)
        dpre = dp_ref[...]
        dbuf[0:tm, :] = dpre
        dbuf[tm:, :] = jnp.where(i == nt - 1, 0.0, dpn_ref[...])
        _shift_copies(sh, buf)
        _shift_copies(dsh, dbuf)
        acc = jnp.zeros((tm, cw), _F32)
        for j in range(_CONV_WIDTH):
            acc = acc + w_ref[j:j + 1, :] * _rows_at(dbuf, dsh, _CONV_WIDTH - 1 - j, tm)
        dglu_ref[...] = acc

        @pl.when(i == 0)
        def _():
            dw_ref[...] = jnp.zeros_like(dw_ref)

        base = _HALO - (_CONV_WIDTH - 1)
        for j in range(_CONV_WIDTH):
            dw_ref[j:j + 1, :] += jnp.sum(dpre * _rows_at(buf, sh, base + j, tm), axis=0, keepdims=True)

    nxt = pl.BlockSpec((_HALO, cw), lambda i: (jnp.minimum((i + 1) * per, s // _HALO - 1), 0))
    return pl.pallas_call(
        body, name="conv_bwd_taps", grid=(nt,),
        in_specs=[cur(3), cur(4), prev(3), prev(4), _full2(dw_w), pl.BlockSpec((tm, cw), lambda i: (i, 0)), nxt],
        out_specs=[pl.BlockSpec((tm, cw), lambda i: (i, 0)), pl.BlockSpec((_HALO, cw), lambda i: (0, 0))],
        out_shape=[jax.ShapeDtypeStruct((s, cw), _F32), jax.ShapeDtypeStruct((_HALO, cw), _F32)],
        scratch_shapes=[pltpu.VMEM((tm + _HALO, cw), _F32), pltpu.VMEM((tm + _HALO, cw), _F32),
                        pltpu.VMEM((7, _SHIFT_ROWS, cw), _F32), pltpu.VMEM((7, _SHIFT_ROWS, cw), _F32)],
        compiler_params=_params(("arbitrary",)),
    )(proj, proj, proj, proj, dw_w, dpre, dpre)


def _hyb_dproj(dqkv, tab_c, tab_s, dglu, proj):
    s = proj.shape[0]
    w = _A_WIDTH

    def body(d1_ref, d2_ref, d3_ref, c_ref, s_ref, dg_ref, ua_ref, ub_ref, o_ref):
        j = pl.program_id(1)

        @pl.when(j < 3)
        def _():
            x = d1_ref[...] + d2_ref[...] + d3_ref[...]
            o_ref[...] = _rot_apply(x, c_ref[...], s_ref[...], True).astype(_BF)

        @pl.when(j == 3)
        def _():
            o_ref[...] = (dg_ref[...] * jax.nn.sigmoid(ub_ref[...])).astype(_BF)

        @pl.when(j == 4)
        def _():
            sg = jax.nn.sigmoid(ub_ref[...])
            o_ref[...] = (dg_ref[...] * ua_ref[...] * sg * (1.0 - sg)).astype(_BF)

    dq = pl.BlockSpec((_TM, w), lambda i, j: (i, jnp.minimum(j, 2)))
    tab = pl.BlockSpec((None, _TM, 128), lambda i, j: (jnp.minimum(j, 2), i, 0))
    col = lambda c: pl.BlockSpec((_TM, w), lambda i, j: (i, c))
    return pl.pallas_call(
        body, name="hyb_dproj", grid=(s // _TM, 5),
        in_specs=[dq, dq, dq, tab, tab, col(0), col(3), col(4)],
        out_specs=pl.BlockSpec((_TM, w), lambda i, j: (i, j)),
        out_shape=jax.ShapeDtypeStruct((s, 5 * w), _BF), compiler_params=_params(("arbitrary", "arbitrary")),
    )(*dqkv, tab_c, tab_s, dglu, proj, proj)


def _proj_in(name, xn, w, layer):
    s, d = xn.shape
    ns = w.shape[-1]
    return _mm(name, xn, w, grid=(4, s // _TM), a_spec=pl.BlockSpec((_TM, d), lambda j, i: (i, 0)),
               b_spec=pl.BlockSpec((None, None, d, ns), lambda j, i: (layer, j, 0, 0)), dn=_NN,
               out_shape=jax.ShapeDtypeStruct((s, 4 * ns), _F32), out_spec=pl.BlockSpec((_TM, ns), lambda j, i: (i, j)))


def _proj_in_dx(name, dz, w, layer, h, gamma, dh_out):
    s = dz.shape[0]
    d, ns = w.shape[2], w.shape[3]
    return _mm(name, dz, w, grid=(s // _TM, 4), a_spec=pl.BlockSpec((_TM, ns), lambda i, j: (i, j)),
               b_spec=pl.BlockSpec((None, None, d, ns), lambda i, j: (layer, j, 0, 0)), dn=_NT, red=1,
               acc_shape=(_TM, d), out_shape=jax.ShapeDtypeStruct((s, d), _F32),
               out_spec=pl.BlockSpec((_TM, d), lambda i, j: (i, 0)), epi=_norm_bwd_epi,
               **_norm_bwd_args(h, dh_out, gamma, _TM))


def _proj_in_dw(name, xn, dz):
    s, d = xn.shape
    ns = dz.shape[1] // 4
    return _mm(name, xn, dz, grid=(4, s // _TM), a_spec=pl.BlockSpec((_TM, d), lambda j, i: (i, 0)),
               b_spec=pl.BlockSpec((_TM, ns), lambda j, i: (i, j)), dn=_TN, red=1, acc_shape=(d, ns),
               out_shape=jax.ShapeDtypeStruct((4, d, ns), _BF),
               out_spec=pl.BlockSpec((None, d, ns), lambda j, i: (j, 0, 0)))


def _proj_out_dy(name, dh, w, layer):
    s, d = dh.shape
    k = w.shape[1]
    return _mm(name, dh, w, grid=(s // _TM,), a_spec=pl.BlockSpec((_TM, d), lambda i: (i, 0)),
               b_spec=pl.BlockSpec((None, k, d), lambda i: (layer, 0, 0)), dn=_NT,
               out_shape=jax.ShapeDtypeStruct((s, k), _F32), out_spec=pl.BlockSpec((_TM, k), lambda i: (i, 0)))


def _hyb_fwd(h, gamma, w_in, w_out, dw_w, dw_b, ln_g, ln_b, tabs, li):
    hn = _rmsnorm_fwd(h, gamma)
    proj = _proj_in("hyb_in", hn, w_in, li)
    qkv = _qkv_prep(proj, *tabs)
    branches = [_attn_fwd(qkv, dil) for dil in _DILATIONS]
    outs, lses = [b[0] for b in branches], [b[1] for b in branches]
    y = _attn_mix_fwd(outs, lses)
    y = _conv_fwd(proj, dw_w, dw_b, ln_g, ln_b, y)
    h_new = _proj_residual("hyb_out", y, w_out, li, h, 1.0)
    return h_new, (h, hn, proj, qkv, outs, lses, y)


def _hyb_bwd(dh, saved, gamma, w_in, w_out, dw_w, dw_b, ln_g, ln_b, tabs, li):
    h, hn, proj, qkv, outs, lses, y = saved
    dy = _proj_out_dy("hyb_dy", dh, w_out, li)
    d_w_out = _wgrad_rows("hyb_dwout", y, dh, 1.0)
    mixg = _attn_mix_bwd(outs, lses, dy)
    dqkv = [_attn_bwd(qkv, outs[b], lses[b], mixg[b], mixg[3 + b], dil) for b, dil in enumerate(_DILATIONS)]
    dpre, d_ln_g, d_ln_b, d_dw_b = _conv_bwd_pre(proj, dw_w, dw_b, ln_g, ln_b, dy)
    dglu, d_dw_w = _conv_bwd_taps(proj, dw_w, dpre)
    dproj = _hyb_dproj(dqkv, *tabs, dglu, proj)
    dh_in, dgamma = _proj_in_dx("hyb_dhn", dproj, w_in, li, h, gamma, dh)
    d_w_in = _proj_in_dw("hyb_dwin", hn, dproj)
    return dh_in, dgamma, (d_dw_w, d_dw_b, d_ln_g, d_ln_b), d_w_in, d_w_out


_GDN_BLOCK = 128
_GDN_HEADS_PER_STEP = 8


def _gdn_chunk(qs, ks, vs, gates, states, heads):
    n = len(qs)
    hs = range(n)
    c = qs[0].shape[0]
    lane = lax.broadcasted_iota(jnp.int32, gates.shape, 1)
    row = lax.broadcasted_iota(jnp.int32, (c, c), 0)
    col = lax.broadcasted_iota(jnp.int32, (c, c), 1)
    rowc = lax.broadcasted_iota(jnp.int32, (c, 1), 0)
    causal = row >= col
    eye = jnp.where(row == col, 1.0, 0.0)
    beta = [jnp.sum(jnp.where(lane == heads[h], gates, 0.0), axis=1, keepdims=True) for h in hs]
    g = [jnp.sum(jnp.where(lane == heads[h] + _GDN_HEADS, gates, 0.0), axis=1, keepdims=True) for h in hs]
    gc_row = [jnp.sum(jnp.where(row <= col, g[h], 0.0), axis=0, keepdims=True) for h in hs]
    gc_col = [jnp.sum(jnp.where(row == col, gc_row[h], 0.0), axis=1, keepdims=True) for h in hs]
    decay = [jnp.where(causal, jnp.exp(jnp.where(causal, gc_col[h] - gc_row[h], 0.0)), 0.0) for h in hs]
    kb = [ks[h] * beta[h] for h in hs]
    l_mat = [jnp.where(row > col, _dot_nt(kb[h], ks[h]) * decay[h], 0.0) for h in hs]
    t_inv = [eye - l_mat[h] for h in hs]
    x = [_dot(l_mat[h], l_mat[h]) for h in hs]
    for it in range(5):
        t_inv = [t_inv[h] + _dot(t_inv[h], x[h]) for h in hs]
        if it < 4:
            x = [_dot(x[h], x[h]) for h in hs]
    eg = [jnp.exp(gc_col[h]) for h in hs]
    u = [_dot(t_inv[h], vs[h] * beta[h]) for h in hs]
    w = [_dot(t_inv[h], kb[h] * eg[h]) for h in hs]
    attn = [jnp.where(causal, _dot_nt(qs[h], ks[h]) * decay[h], 0.0) for h in hs]
    g_last = [jnp.sum(jnp.where(rowc == c - 1, gc_col[h], 0.0), axis=0, keepdims=True) for h in hs]
    k_dec = [ks[h] * jnp.exp(g_last[h] - gc_col[h]) for h in hs]
    v_new = [u[h] - _dot(w[h], states[h]) for h in hs]
    o = [_dot(qs[h] * eg[h], states[h]) + _dot(attn[h], v_new[h]) for h in hs]
    new_states = [states[h] * jnp.exp(g_last[h]) + _dot_tn(k_dec[h], v_new[h]) for h in hs]
    return o, new_states


def _gdn_chunk_fwd(q, k, v, gates):
    s = q.shape[0]
    cb, c = _GDN_BLOCK, _GDN_CHUNK
    nblk, per = s // cb, cb // c

    hps = _GDN_HEADS_PER_STEP

    def body(q_ref, k_ref, v_ref, g_ref, o_ref, st_ref, state):
        @pl.when(pl.program_id(1) == 0)
        def _():
            state[...] = jnp.zeros_like(state)

        heads = [pl.program_id(0) * hps + hh for hh in range(hps)]
        cols = [slice(hh * _GDN_DIM, (hh + 1) * _GDN_DIM) for hh in range(hps)]
        for ci in range(per):
            rows = slice(ci * c, (ci + 1) * c)
            sts = [state[hh] for hh in range(hps)]
            for hh in range(hps):
                st_ref[hh, ci] = sts[hh]
            outs, new = _gdn_chunk([q_ref[rows, cl] for cl in cols], [k_ref[rows, cl] for cl in cols],
                                   [v_ref[rows, cl] for cl in cols], g_ref[rows, :], sts, heads)
            for hh in range(hps):
                o_ref[rows, cols[hh]] = outs[hh]
                state[hh] = new[hh]

    slab = pl.BlockSpec((cb, hps * _GDN_DIM), lambda h, i: (i, h))
    return pl.pallas_call(
        body, name="gdn_chunk_fwd", grid=(_GDN_HEADS // hps, nblk),
        in_specs=[slab, slab, slab, pl.BlockSpec((cb, 128), lambda h, i: (i, 0))],
        out_specs=[slab, pl.BlockSpec((hps, per, _GDN_DIM, _GDN_DIM), lambda h, i: (h, i, 0, 0))],
        out_shape=[jax.ShapeDtypeStruct((s, _GDN_HEADS * _GDN_DIM), _F32),
                   jax.ShapeDtypeStruct((_GDN_HEADS, s // c, _GDN_DIM, _GDN_DIM), _F32)],
        scratch_shapes=[pltpu.VMEM((hps, _GDN_DIM, _GDN_DIM), _F32)],
        compiler_params=_params(("arbitrary", "arbitrary")),
    )(q, k, v, gates)


def _gdn_chunk_bwd(q, k, v, gates, states, do):
    s = q.shape[0]
    cb, c = _GDN_BLOCK, _GDN_CHUNK
    nblk, per = s // cb, cb // c

    hps = _GDN_HEADS_PER_STEP

    def body(q_ref, k_ref, v_ref, g_ref, st_ref, do_ref, dq_ref, dk_ref, dv_ref, dg_ref, dstate):
        @pl.when(pl.program_id(1) == 0)
        def _():
            dstate[...] = jnp.zeros_like(dstate)

        heads = [pl.program_id(0) * hps + hh for hh in range(hps)]
        cols = [slice(hh * _GDN_DIM, (hh + 1) * _GDN_DIM) for hh in range(hps)]
        for ci in reversed(range(per)):
            rows = slice(ci * c, (ci + 1) * c)
            fn = functools.partial(_gdn_chunk, heads=heads)
            _, pull = jax.vjp(fn, [q_ref[rows, cl] for cl in cols], [k_ref[rows, cl] for cl in cols],
                              [v_ref[rows, cl] for cl in cols], g_ref[rows, :], [st_ref[hh, ci] for hh in range(hps)])
            dq, dk, dv, dg, dst = pull(([do_ref[rows, cl] for cl in cols], [dstate[hh] for hh in range(hps)]))
            dg_ref[rows, :] = dg
            for hh in range(hps):
                dq_ref[rows, cols[hh]] = dq[hh]
                dk_ref[rows, cols[hh]] = dk[hh]
                dv_ref[rows, cols[hh]] = dv[hh]
                dstate[hh] = dst[hh]

    slab = pl.BlockSpec((cb, hps * _GDN_DIM), lambda h, i: (nblk - 1 - i, h))
    sds = jax.ShapeDtypeStruct((s, _GDN_HEADS * _GDN_DIM), _F32)
    return pl.pallas_call(
        body, name="gdn_chunk_bwd", grid=(_GDN_HEADS // hps, nblk),
        in_specs=[slab, slab, slab, pl.BlockSpec((cb, 128), lambda h, i: (nblk - 1 - i, 0)),
                  pl.BlockSpec((hps, per, _GDN_DIM, _GDN_DIM), lambda h, i: (h, nblk - 1 - i, 0, 0)), slab],
        out_specs=[slab, slab, slab, pl.BlockSpec((None, cb, 128), lambda h, i: (h, nblk - 1 - i, 0))],
        out_shape=[sds, sds, sds, jax.ShapeDtypeStruct((_GDN_HEADS // hps, s, 128), _F32)],
        scratch_shapes=[pltpu.VMEM((hps, _GDN_DIM, _GDN_DIM), _F32)],
        compiler_params=_params(("arbitrary", "arbitrary")),
    )(q, k, v, gates, states, do)


_SHALO = 8


def _swish_l2(c, normalize, scale):
    y = _silu(c)
    if normalize:
        y = y * lax.rsqrt(jnp.sum(y * y, axis=-1, keepdims=True) + _NORM_EPS) * scale
    return y


def _short_taps(buf, w_ref, rows):
    base = _SHALO - (_GDN_SHORT_CONV - 1)
    acc = jnp.zeros((rows, buf.shape[1]), _F32)
    for t in range(_GDN_SHORT_CONV):
        acc = acc + w_ref[t:t + 1, :] * buf[base + t:base + t + rows, :]
    return acc


_GDN_W = _GDN_HEADS * _GDN_DIM


def _gdn_conv_fwd(proj, conv_w, grp, normalize, scale):
    s = proj.shape[0]
    tm = _CONV_TM
    per = tm // _SHALO

    def body(x_ref, xp_ref, w_ref, o_ref, buf):
        buf[0:_SHALO, :] = jnp.where(pl.program_id(0) == 0, 0.0, xp_ref[...])
        buf[_SHALO:, :] = x_ref[...]
        conv = _short_taps(buf, w_ref, tm)
        for h in range(_GDN_HEADS):
            cols = slice(h * _GDN_DIM, (h + 1) * _GDN_DIM)
            o_ref[:, cols] = _swish_l2(conv[:, cols], normalize, scale)

    return pl.pallas_call(
        body, name=f"gdn_conv_fwd_{grp}", grid=(s // tm,),
        in_specs=[pl.BlockSpec((tm, _GDN_W), lambda i: (i, grp)),
                  pl.BlockSpec((_SHALO, _GDN_W), lambda i: (jnp.maximum(i * per - 1, 0), grp)),
                  pl.BlockSpec((_GDN_SHORT_CONV, _GDN_W), lambda i: (0, grp))],
        out_specs=pl.BlockSpec((tm, _GDN_W), lambda i: (i, 0)),
        out_shape=jax.ShapeDtypeStruct((s, _GDN_W), _F32),
        scratch_shapes=[pltpu.VMEM((tm + _SHALO, _GDN_W), _F32)], compiler_params=_params(("arbitrary",)),
    )(proj, proj, conv_w)


def _gdn_conv_bwd(proj, conv_w, grp, normalize, scale, dy, dproj):
    s = proj.shape[0]
    tm = _CONV_TM
    per = tm // _SHALO
    nt = s // tm
    taps = _GDN_SHORT_CONV

    def body(x_ref, xp_ref, xn_ref, w_ref, dy_ref, dyn_ref, dp_in, dx_ref, dw_ref, buf, dbuf):
        del dp_in
        i = pl.program_id(0)
        last = i == nt - 1
        buf[0:_SHALO, :] = jnp.where(i == 0, 0.0, xp_ref[...])
        buf[_SHALO:_SHALO + tm, :] = x_ref[...]
        buf[_SHALO + tm:, :] = jnp.where(last, 0.0, xn_ref[...])
        conv = _short_taps(buf, w_ref, tm + _SHALO)
        dy_ext = jnp.concatenate([dy_ref[...], jnp.where(last, 0.0, dyn_ref[...])], axis=0)
        for h in range(_GDN_HEADS):
            cols = slice(h * _GDN_DIM, (h + 1) * _GDN_DIM)
            _, pull = jax.vjp(lambda c: _swish_l2(c, normalize, scale), conv[:, cols])
            dbuf[:, cols] = pull(dy_ext[:, cols])[0]
        acc = jnp.zeros((tm, _GDN_W), _F32)
        for t in range(taps):
            o = taps - 1 - t
            acc = acc + w_ref[t:t + 1, :] * dbuf[o:o + tm, :]
        dx_ref[...] = acc.astype(dx_ref.dtype)

        @pl.when(i == 0)
        def _():
            dw_ref[...] = jnp.zeros_like(dw_ref)

        base = _SHALO - (taps - 1)
        for t in range(taps):
            dw_ref[t:t + 1, :] += jnp.sum(dbuf[0:tm, :] * buf[base + t:base + t + tm, :], axis=0, keepdims=True)

    cur = lambda col: pl.BlockSpec((tm, _GDN_W), lambda i: (i, col))
    nxt = lambda col: pl.BlockSpec((_SHALO, _GDN_W), lambda i: (jnp.minimum((i + 1) * per, s // _SHALO - 1), col))
    return pl.pallas_call(
        body, name=f"gdn_conv_bwd_{grp}", grid=(nt,),
        in_specs=[cur(grp), pl.BlockSpec((_SHALO, _GDN_W), lambda i: (jnp.maximum(i * per - 1, 0), grp)), nxt(grp),
                  pl.BlockSpec((taps, _GDN_W), lambda i: (0, grp)), cur(0), nxt(0), _ANY],
        out_specs=[cur(grp), pl.BlockSpec((_SHALO, _GDN_W), lambda i: (0, 0))],
        out_shape=[jax.ShapeDtypeStruct(dproj.shape, dproj.dtype), jax.ShapeDtypeStruct((_SHALO, _GDN_W), _F32)],
        scratch_shapes=[pltpu.VMEM((tm + 2 * _SHALO, _GDN_W), _F32), pltpu.VMEM((tm + _SHALO, _GDN_W), _F32)],
        input_output_aliases={6: 0}, compiler_params=_params(("arbitrary",)),
    )(proj, proj, proj, conv_w, dy, dy, dproj)


def _softplus(z):
    return jnp.maximum(z, 0.0) + jnp.log(1.0 + jnp.exp(-jnp.abs(z)))


def _gates(ba, a_lane, dt_lane):
    lane = lax.broadcasted_iota(jnp.int32, ba.shape, 1)
    g = -jnp.exp(a_lane) * _softplus(ba + dt_lane)
    return jnp.where(lane < _GDN_HEADS, jax.nn.sigmoid(ba), jnp.where(lane < 2 * _GDN_HEADS, g, 0.0))


def _gates_fwd(ba, a_lane, dt_lane):
    return _rowmap("gdn_gates_fwd", _gates, [(ba, 128, 0)], [a_lane, dt_lane], [(128, 128, 0, _F32)],
                   rows=ba.shape[0], tm=_TM)[0]


def _gates_bwd(ba, a_lane, dt_lane, dgates):
    def body(bav, dg8, av, dv):
        _, pull = jax.vjp(_gates, bav, av, dv)
        return pull(jnp.sum(dg8, axis=0))

    return _rowmap("gdn_gates_bwd", body, [(ba, 128, 0), (dgates, 128, 0)], [a_lane, dt_lane], [(128, 128, 0, _BF)],
                   [(1, 128), (1, 128)], rows=ba.shape[0], tm=_TM)


def _out_gate(o, z, g):
    return (_rms(o, g) * _silu(z)).astype(_BF)


def _head_cols():
    return [slice(h * _GDN_DIM, (h + 1) * _GDN_DIM) for h in range(_GDN_HEADS)]


def _out_gate_fwd(o, proj, norm_g):
    def body(ov, zv, gv):
        return jnp.concatenate([_out_gate(ov[:, cl], zv[:, cl], gv) for cl in _head_cols()], axis=1)

    return _rowmap("gdn_out_gate_fwd", body, [(o, _GDN_W, 0), (proj, _GDN_W, 3)], [norm_g], [(_GDN_W, _GDN_W, 0, _BF)],
                   rows=o.shape[0], tm=_CONV_TM)[0]


def _out_gate_bwd(o, proj, norm_g, dy):
    def body(ov, zv, dyv, gv):
        d_o, d_z, d_g = [], [], jnp.zeros_like(gv)
        for cl in _head_cols():
            _, pull = jax.vjp(_out_gate, ov[:, cl], zv[:, cl], gv)
            g_o, g_z, g_g = pull(dyv[:, cl].astype(_BF))
            d_o.append(g_o)
            d_z.append(g_z)
            d_g = d_g + g_g
        return jnp.concatenate(d_o, axis=1), jnp.concatenate(d_z, axis=1), d_g

    return _rowmap("gdn_out_gate_bwd", body, [(o, _GDN_W, 0), (proj, _GDN_W, 3), (dy, _GDN_W, 0)], [norm_g],
                   [(_GDN_W, _GDN_W, 0, _F32), (proj.shape[1], _GDN_W, 3, _BF)], [(1, _GDN_DIM)],
                   rows=o.shape[0], tm=_CONV_TM)


def _plain_in(name, xn, w, layer, tn):
    s, d = xn.shape
    n = w.shape[2]
    return _mm(name, xn, w, grid=(n // tn, s // _TM), a_spec=pl.BlockSpec((_TM, d), lambda j, i: (i, 0)),
               b_spec=pl.BlockSpec((None, d, tn), lambda j, i: (layer, 0, j)), dn=_NN,
               out_shape=jax.ShapeDtypeStruct((s, n), _F32), out_spec=pl.BlockSpec((_TM, tn), lambda j, i: (i, j)))


def _plain_dx(name, dz, w, layer, tn, norm=None):
    s, n = dz.shape
    d = w.shape[1]
    row = pl.BlockSpec((_TM, d), lambda i, j: (i, 0))
    kw = {}
    if norm is not None:
        extra, h, gamma, dh_out = norm
        kw = _norm_bwd_args(h, dh_out, gamma, _TM)
        kw.update(extras=(extra, *kw["extras"]), extra_specs=(row, *kw["extra_specs"]),
                  epi=lambda acc, e, hv, dhv, gv: _norm_bwd_epi(acc + e, hv, dhv, gv))
    return _mm(name, dz, w, grid=(s // _TM, n // tn), a_spec=pl.BlockSpec((_TM, tn), lambda i, j: (i, j)),
               b_spec=pl.BlockSpec((None, d, tn), lambda i, j: (layer, 0, j)), dn=_NT, red=1, acc_shape=(_TM, d),
               out_shape=jax.ShapeDtypeStruct((s, d), _F32), out_spec=row, **kw)


def _plain_dw(name, xn, dz, tn):
    s, d = xn.shape
    n = dz.shape[1]
    return _mm(name, xn, dz, grid=(n // tn, s // _TM), a_spec=pl.BlockSpec((_TM, d), lambda j, i: (i, 0)),
               b_spec=pl.BlockSpec((_TM, tn), lambda j, i: (i, j)), dn=_TN, red=1, acc_shape=(d, tn),
               out_shape=jax.ShapeDtypeStruct((d, n), _BF), out_spec=pl.BlockSpec((d, tn), lambda j, i: (0, j)))


_GDN_QK_SCALE = _GDN_DIM ** -0.5


def _gdn_fwd(h, gamma, w_qkvz, w_ba, w_out, conv_w, a_lane, dt_lane, norm_g, li):
    hn = _rmsnorm_fwd(h, gamma)
    proj = _plain_in("gdn_in", hn, w_qkvz, li, 1024)
    ba = _plain_in("gdn_in_ba", hn, w_ba, li, 128)
    q = _gdn_conv_fwd(proj, conv_w, 0, True, _GDN_QK_SCALE)
    k = _gdn_conv_fwd(proj, conv_w, 1, True, 1.0)
    v = _gdn_conv_fwd(proj, conv_w, 2, False, 1.0)
    gates = _gates_fwd(ba, a_lane, dt_lane)
    o, states = _gdn_chunk_fwd(q, k, v, gates)
    y = _out_gate_fwd(o, proj, norm_g)
    h_new = _proj_residual("gdn_out", y, w_out, li, h, 1.0)
    return h_new, (h, hn, proj, ba, q, k, v, gates, states, o, y)


def _gdn_bwd(dh, saved, gamma, w_qkvz, w_ba, w_out, conv_w, a_lane, dt_lane, norm_g, li, n_ba):
    h, hn, proj, ba, q, k, v, gates, states, o, y = saved
    dy = _proj_out_dy("gdn_dy", dh, w_out, li)
    d_w_out = _wgrad_rows("gdn_dwout", y, dh, 1.0)
    do, dproj, d_norm_g = _out_gate_bwd(o, proj, norm_g, dy)
    dq, dk, dv, dgates = _gdn_chunk_bwd(q, k, v, gates, states, do)
    dproj, dcw_q = _gdn_conv_bwd(proj, conv_w, 0, True, _GDN_QK_SCALE, dq, dproj)
    dproj, dcw_k = _gdn_conv_bwd(proj, conv_w, 1, True, 1.0, dk, dproj)
    dproj, dcw_v = _gdn_conv_bwd(proj, conv_w, 2, False, 1.0, dv, dproj)
    dba, d_a, d_dt = _gates_bwd(ba, a_lane, dt_lane, dgates)
    dhn_ba = _plain_dx("gdn_dhn_ba", dba, w_ba, li, 128)
    dh_in, dgamma = _plain_dx("gdn_dhn", dproj, w_qkvz, li, 1024, norm=(dhn_ba, h, gamma, dh))
    dw_qkvz = _plain_dw("gdn_dwin", hn, dproj, 1024)
    dw_ba = _plain_dw("gdn_dwin_ba", hn, dba, 128)
    d_conv_w = jnp.concatenate([dcw_q, dcw_k, dcw_v], axis=1)[:_GDN_SHORT_CONV]
    d_w_in = jnp.concatenate([dw_qkvz, dw_ba[:, :n_ba]], axis=1)
    d_w_in = jnp.transpose(d_w_in.reshape(d_w_in.shape[0], 4, -1), (1, 0, 2))
    return dh_in, dgamma, (d_conv_w, d_a, d_dt, d_norm_g), d_w_in, d_w_out


_MESH = pl.DeviceIdType.MESH
_ROW_TM = 384


def _place():
    x, y, c = lax.axis_index("x"), lax.axis_index("y"), lax.axis_index("c")
    chips = [(1 - x, y), (x, 1 - y), (1 - x, 1 - y)]
    return x, y, c, chips, [2 * px + py for px, py in chips]


def _remote(src, dst, send_sem, recv_sem, device):
    return pltpu.make_async_remote_copy(src_ref=src, dst_ref=dst, send_sem=send_sem, recv_sem=recv_sem,
                                        device_id=device, device_id_type=_MESH)


def _rows2d(t):
    return t.reshape(-1, t.shape[-1])


def _row_tile(rows):
    for tm in range(min(rows, _ROW_TM) // 8 * 8, 7, -8):
        if rows % tm == 0:
            return tm
    raise ValueError(f"no row tile for {rows} rows")


def _placed_map(name, fn, where, ins, out_rows, cols, out_dtype, out_row, steps, tm, into=None):
    n_in = len(ins)

    def kern(where_ref, *refs):
        del where_ref
        o_ref = refs[n_in + (1 if into is not None else 0)]
        o_ref[...] = fn(*[r[...] for r in refs[:n_in]]).astype(o_ref.dtype)

    in_specs, operands = [], [where]
    for arr, rowfn in ins:
        operands.append(arr)
        if arr.ndim == 3:
            in_specs.append(pl.BlockSpec((arr.shape[0], tm, cols), lambda i, wh, f=rowfn: (0, f(i, wh), 0)))
        else:
            in_specs.append(pl.BlockSpec((tm, cols), lambda i, wh, f=rowfn: (f(i, wh), 0)))
    aliases = {}
    if into is not None:
        operands.append(into)
        in_specs.append(_ANY)
        aliases = {n_in + 1: 0}
    spec = pltpu.PrefetchScalarGridSpec(
        num_scalar_prefetch=1, grid=(steps,), in_specs=in_specs,
        out_specs=pl.BlockSpec((tm, cols), lambda i, wh: (out_row(i, wh), 0)))
    return pl.pallas_call(
        kern, name=name, grid_spec=spec, out_shape=jax.ShapeDtypeStruct((out_rows, cols), out_dtype),
        input_output_aliases=aliases, compiler_params=_params(("arbitrary",)),
    )(*operands)


def _cast_slab(name, t, where):
    n_l, r, cols = t.shape
    tm = _row_tile(r)
    per = r // tm
    buf = _placed_map("cast_" + name, lambda val: val, where, [(_rows2d(t), lambda i, wh: i)], n_l * 4 * r, cols, _BF,
                      lambda i, wh: ((i // per) * 4 + wh[1]) * per + i % per, n_l * per, tm)
    return buf.reshape(n_l, 4, r, cols)


_HBM = pl.BlockSpec(memory_space=pltpu.HBM)
_SEM = pl.BlockSpec(memory_space=pltpu.SEMAPHORE)
_EFFECT = pltpu.SideEffectType.DATAFLOW_SIDE_EFFECTING


def _core_layers(n_layers, c, part):
    half = n_layers // 2
    sub = half // part[1]
    return pl.ds(c * half + part[0] * sub, sub), pl.ds((1 - c) * half + part[0] * sub, sub)


def _ici_copies(refs, send, recv, arrivals, part):
    x, y, c, chips, idx = _place()
    me = 2 * x + y
    res = []
    for t, ref in enumerate(refs):
        mine, _ = _core_layers(ref.shape[0], c, part)
        for k in range(3):
            sem = 3 * t + k
            block = ref.at[mine, idx[k] if arrivals else me]
            res.append(_remote(block, block, send.at[sem], recv.at[sem], (*chips[k], c)))
    return res


def _gather_start(name, bufs, after, part):
    n = len(bufs)

    def body(*refs):
        send, recv = refs[n + 1], refs[n + 2]
        for cp in _ici_copies(refs[:n], send, recv, False, part):
            cp.start()
        refs[2 * n + 3][...] = jnp.zeros((8, 128), _F32)

    res = pl.pallas_call(
        body, name=name,
        out_shape=(pltpu.SemaphoreType.DMA((3 * n,)), pltpu.SemaphoreType.DMA((3 * n,)),
                   *[pltpu.HBM(b.shape, b.dtype) for b in bufs], jax.ShapeDtypeStruct((8, 128), _F32)),
        in_specs=[*[_HBM] * n, _ANY], out_specs=(_SEM, _SEM, *[_HBM] * n, pl.BlockSpec(memory_space=pltpu.VMEM)),
        input_output_aliases={t: 2 + t for t in range(n)},
        compiler_params=pltpu.CompilerParams(has_side_effects=_EFFECT),
    )(*[pltpu.with_memory_space_constraint(b, pltpu.HBM) for b in bufs], after)
    return res[0], res[1], list(res[2:2 + n]), res[2 + n]


def _gather_wait(name, send_sem, recv_sem, bufs, after, part):
    n = len(bufs)

    def body(*refs):
        send, recv = refs[n], refs[n + 1]
        for cp in _ici_copies(refs[:n], send, recv, False, part):
            cp.wait_send()
        for cp in _ici_copies(refs[:n], send, recv, True, part):
            cp.wait_recv()

    res = pl.pallas_call(
        body, name=name, out_shape=tuple(pltpu.HBM(b.shape, b.dtype) for b in bufs),
        in_specs=[*[_HBM] * n, _SEM, _SEM, _ANY], out_specs=tuple([_HBM] * n),
        input_output_aliases={t: t for t in range(n)},
        compiler_params=pltpu.CompilerParams(has_side_effects=_EFFECT),
    )(*bufs, send_sem, recv_sem, after)
    return list(res)


def _gather_forward(name, bufs, part):
    n = len(bufs)

    def body(*refs):
        outs, send, recv = refs[n:2 * n], refs[2 * n], refs[2 * n + 1]
        x, y, c, _, idx = _place()
        copies, arrive = [], []
        for t, ref in enumerate(outs):
            mine, other = _core_layers(ref.shape[0], c, part)
            for k in range(3):
                sem = 3 * t + k
                landed, dst = ref.at[mine, idx[k]], ref.at[other, idx[k]]
                copies.append(_remote(landed, landed, send.at[sem], recv.at[sem], (x, y, 1 - c)))
                arrive.append(_remote(dst, dst, send.at[sem], recv.at[sem], (x, y, 1 - c)))
        for cp in copies:
            cp.start()
        for cp in arrive:
            cp.wait_recv()
        for cp in copies:
            cp.wait_send()

    res = pl.pallas_call(
        body, name=name, in_specs=[_ANY] * n, out_specs=[_ANY] * n,
        out_shape=[jax.ShapeDtypeStruct(b.shape, b.dtype) for b in bufs],
        scratch_shapes=[pltpu.SemaphoreType.DMA((3 * n,)), pltpu.SemaphoreType.DMA((3 * n,))],
        input_output_aliases={t: t for t in range(n)},
    )(*bufs)
    return list(res)


def _rs_sibling(name, gs):
    n = len(gs)

    def body(*refs):
        send, recv = refs[2 * n], refs[2 * n + 1]
        x, y, c, _, _ = _place()
        copies = []
        for t in range(n):
            rh = refs[t].shape[1] // 2
            cp = _remote(refs[t].at[:, pl.ds((1 - c) * rh, rh)], refs[n + t], send.at[t], recv.at[t], (x, y, 1 - c))
            cp.start()
            copies.append(cp)
        for cp in copies:
            cp.wait()

    return pl.pallas_call(
        body, name=name, in_specs=[_ANY] * n, out_specs=[_ANY] * n,
        out_shape=[jax.ShapeDtypeStruct((4, g.shape[1] // 2, g.shape[2]), g.dtype) for g in gs],
        scratch_shapes=[pltpu.SemaphoreType.DMA((n,)), pltpu.SemaphoreType.DMA((n,))],
    )(*gs)


def _chip_copies(sums, lands, send, recv):
    x, y, c, chips, idx = _place()
    return [_remote(sums[t].at[idx[k]], lands[t].at[k], send.at[3 * t + k], recv.at[3 * t + k], (*chips[k], c))
            for t in range(len(sums)) for k in range(3)]


def _rs_chips_start(name, sums, after):
    n = len(sums)
    lands = [lax.empty((3,) + s.shape[1:], s.dtype) for s in sums]

    def body(*refs):
        send, recv = refs[2 * n + 1], refs[2 * n + 2]
        for cp in _chip_copies(refs[:n], refs[n:2 * n], send, recv):
            cp.start()
        refs[4 * n + 3][...] = jnp.zeros((8, 128), _F32)

    both = [*sums, *lands]
    res = pl.pallas_call(
        body, name=name,
        out_shape=(pltpu.SemaphoreType.DMA((3 * n,)), pltpu.SemaphoreType.DMA((3 * n,)),
                   *[pltpu.HBM(b.shape, b.dtype) for b in both], jax.ShapeDtypeStruct((8, 128), _F32)),
        in_specs=[*[_HBM] * (2 * n), _ANY],
        out_specs=(_SEM, _SEM, *[_HBM] * (2 * n), pl.BlockSpec(memory_space=pltpu.VMEM)),
        input_output_aliases={t: 2 + t for t in range(2 * n)},
        compiler_params=pltpu.CompilerParams(has_side_effects=_EFFECT),
    )(*[pltpu.with_memory_space_constraint(b, pltpu.HBM) for b in both], after)
    return res[0], res[1], list(res[2:2 + n]), list(res[2 + n:2 + 2 * n]), res[2 + 2 * n]


def _rs_chips_wait(name, send_sem, recv_sem, sums, lands, after):
    n = len(sums)

    def body(*refs):
        send, recv = refs[2 * n], refs[2 * n + 1]
        copies = _chip_copies(refs[:n], refs[n:2 * n], send, recv)
        for cp in copies:
            cp.wait_send()
        for cp in copies:
            cp.wait_recv()

    both = [*sums, *lands]
    res = pl.pallas_call(
        body, name=name, out_shape=tuple(pltpu.HBM(b.shape, b.dtype) for b in both),
        in_specs=[*[_HBM] * (2 * n), _SEM, _SEM, _ANY], out_specs=tuple([_HBM] * (2 * n)),
        input_output_aliases={t: t for t in range(2 * n)},
        compiler_params=pltpu.CompilerParams(has_side_effects=_EFFECT),
    )(*both, send_sem, recv_sem, after)
    return list(res[:n]), list(res[n:])


def _rs_join(name, totals, layers):
    n = len(totals)

    def body(*refs):
        outs, send, recv = refs[n:2 * n], refs[2 * n], refs[2 * n + 1]
        x, y, c, _, _ = _place()
        copies, arrive = [], []
        for t in range(n):
            rh = outs[t].shape[1] // 2
            mine = outs[t].at[layers[t], pl.ds(c * rh, rh)]
            landing = outs[t].at[layers[t], pl.ds((1 - c) * rh, rh)]
            copies.append(_remote(mine, mine, send.at[t], recv.at[t], (x, y, 1 - c)))
            arrive.append(_remote(landing, landing, send.at[t], recv.at[t], (x, y, 1 - c)))
        for cp in copies:
            cp.start()
        for cp in arrive:
            cp.wait_recv()
        for cp in copies:
            cp.wait_send()

    return pl.pallas_call(
        body, name=name, in_specs=[_ANY] * n, out_specs=[_ANY] * n,
        out_shape=[jax.ShapeDtypeStruct(b.shape, b.dtype) for b in totals],
        scratch_shapes=[pltpu.SemaphoreType.DMA((n,)), pltpu.SemaphoreType.DMA((n,))],
        input_output_aliases={t: t for t in range(n)},
    )(*totals)


def _pair_sum(g, got, where):
    _, r, cols = g.shape
    rh = r // 2
    tm = _row_tile(rh)
    per, perh = r // tm, rh // tm
    out = _placed_map("rs_add_pair", lambda u, w: u.astype(_F32) + w.astype(_F32), where,
                      [(_rows2d(g), lambda i, wh: (i // perh) * per + wh[0] * perh + i % perh),
                       (_rows2d(got), lambda i, wh: i)],
                      4 * rh, cols, _BF, lambda i, wh: i, 4 * perh, tm)
    return out.reshape(4, rh, cols)


def _sum_of_four(pair, land, where, total, layer, n_layers):
    _, rh, cols = pair.shape
    r = 2 * rh
    tm = _row_tile(rh)
    per, perh = r // tm, rh // tm

    def add_four(own, others):
        return ((own.astype(_F32) + others[0].astype(_F32)) + others[1].astype(_F32)) + others[2].astype(_F32)

    out = _placed_map("rs_add_four", add_four, where,
                      [(_rows2d(pair), lambda i, wh: wh[1] * perh + i), (land, lambda i, wh: i)],
                      n_layers * r, cols, _F32, lambda i, wh: layer * per + wh[0] * perh + i, perh, tm,
                      into=None if total is None else _rows2d(total))
    return out.reshape(n_layers, r, cols)


def _all_reduce_small(pack):
    rows, cols = pack.shape

    def body(p_ref, o_ref, land, send, recv):
        x, y, c = lax.axis_index("x"), lax.axis_index("y"), lax.axis_index("c")
        me = 4 * x + 2 * y + c
        land[me] = p_ref[...]
        copies = []
        for k in range(1, 8):
            bx, by, bc = (k >> 2) & 1, (k >> 1) & 1, k & 1
            peer = (1 - x if bx else x, 1 - y if by else y, 1 - c if bc else c)
            cp = _remote(p_ref, land.at[me], send.at[k - 1], recv.at[k - 1], peer)
            cp.start()
            copies.append((cp, 4 * peer[0] + 2 * peer[1] + peer[2]))
        for k, (cp, pid) in enumerate(copies):
            _remote(p_ref, land.at[pid], send.at[k], recv.at[k], (x, y, c)).wait_recv()
        for cp, _ in copies:
            cp.wait_send()
        acc = land[0]
        for d in range(1, 8):
            acc = acc + land[d]
        o_ref[...] = acc

    vm = pl.BlockSpec(memory_space=pltpu.VMEM)
    return pl.pallas_call(
        body, name="all_reduce_small", in_specs=[vm], out_specs=vm,
        out_shape=jax.ShapeDtypeStruct((rows, cols), _F32),
        scratch_shapes=[pltpu.VMEM((8, rows, cols), _F32), pltpu.SemaphoreType.DMA((7,)), pltpu.SemaphoreType.DMA((7,))],
    )(pack)


def _adamw_math(w, g, m, v):
    m2 = _ADAM_B1 * m + (1.0 - _ADAM_B1) * g
    v2 = _ADAM_B2 * v + (1.0 - _ADAM_B2) * (g * g)
    m_hat = m2 / (1.0 - _ADAM_B1 ** _ADAM_STEP)
    v_hat = v2 / (1.0 - _ADAM_B2 ** _ADAM_STEP)
    delta = -_ADAM_LR * (m_hat / (jnp.sqrt(v_hat) + _ADAM_EPS) + _ADAM_WD * w)
    return delta, m2, v2


def _adamw(w, g, m, v, lo=0, hi=None, into=None):
    shape = w.shape
    n_l = shape[0] if w.ndim == 3 else 1
    hi = n_l if hi is None else hi
    w2, g2, m2, v2 = (_rows2d(t) for t in (w, g, m, v))
    rows, cols = w2.shape
    per_layer = rows // n_l
    tm = _row_tile(per_layer)
    first, steps = lo * per_layer // tm, (hi - lo) * per_layer // tm
    n_into = 0 if into is None else 3

    def body(*refs):
        outs = refs[4 + n_into:]
        for ref, val in zip(outs, _adamw_math(*[r[...] for r in refs[:4]])):
            ref[...] = val

    spec = pl.BlockSpec((tm, cols), lambda i: (first + i, 0))
    sds = jax.ShapeDtypeStruct((rows, cols), _F32)
    outs = pl.pallas_call(
        body, name="adamw", grid=(steps,), in_specs=[spec] * 4 + [_ANY] * n_into, out_specs=[spec] * 3,
        out_shape=[sds] * 3, input_output_aliases={4 + k: k for k in range(n_into)},
        compiler_params=_params(("arbitrary",)),
    )(w2, g2, m2, v2, *([] if into is None else [_rows2d(t) for t in into]))
    return tuple(o.reshape(shape) for o in outs)


def kernel(x, positions, ffn1_norm, ffn1_w_in, ffn1_w_out, mix_norm, ffn2_norm, ffn2_w_in, ffn2_w_out, hyb_w_in, hyb_dw_w, hyb_dw_b, hyb_ln_g, hyb_ln_b, hyb_w_out, gdn_w_in, gdn_conv_w, gdn_A_log, gdn_dt_bias, gdn_norm_g, gdn_w_out, final_norm, loss_target, m_ffn1_norm, m_ffn1_w_in, m_ffn1_w_out, m_mix_norm, m_ffn2_norm, m_ffn2_w_in, m_ffn2_w_out, m_hyb_w_in, m_hyb_dw_w, m_hyb_dw_b, m_hyb_ln_g, m_hyb_ln_b, m_hyb_w_out, m_gdn_w_in, m_gdn_conv_w, m_gdn_A_log, m_gdn_dt_bias, m_gdn_norm_g, m_gdn_w_out, m_final_norm, v_ffn1_norm, v_ffn1_w_in, v_ffn1_w_out, v_mix_norm, v_ffn2_norm, v_ffn2_w_in, v_ffn2_w_out, v_hyb_w_in, v_hyb_dw_w, v_hyb_dw_b, v_hyb_ln_g, v_hyb_ln_b, v_hyb_w_out, v_gdn_w_in, v_gdn_conv_w, v_gdn_A_log, v_gdn_dt_bias, v_gdn_norm_g, v_gdn_w_out, v_final_norm):
    w = dict(ffn1_norm=ffn1_norm, ffn1_w_in=ffn1_w_in, ffn1_w_out=ffn1_w_out, mix_norm=mix_norm, ffn2_norm=ffn2_norm, ffn2_w_in=ffn2_w_in, ffn2_w_out=ffn2_w_out, hyb_w_in=hyb_w_in, hyb_dw_w=hyb_dw_w, hyb_dw_b=hyb_dw_b, hyb_ln_g=hyb_ln_g, hyb_ln_b=hyb_ln_b, hyb_w_out=hyb_w_out, gdn_w_in=gdn_w_in, gdn_conv_w=gdn_conv_w, gdn_A_log=gdn_A_log, gdn_dt_bias=gdn_dt_bias, gdn_norm_g=gdn_norm_g, gdn_w_out=gdn_w_out, final_norm=final_norm)
    m = dict(ffn1_norm=m_ffn1_norm, ffn1_w_in=m_ffn1_w_in, ffn1_w_out=m_ffn1_w_out, mix_norm=m_mix_norm, ffn2_norm=m_ffn2_norm, ffn2_w_in=m_ffn2_w_in, ffn2_w_out=m_ffn2_w_out, hyb_w_in=m_hyb_w_in, hyb_dw_w=m_hyb_dw_w, hyb_dw_b=m_hyb_dw_b, hyb_ln_g=m_hyb_ln_g, hyb_ln_b=m_hyb_ln_b, hyb_w_out=m_hyb_w_out, gdn_w_in=m_gdn_w_in, gdn_conv_w=m_gdn_conv_w, gdn_A_log=m_gdn_A_log, gdn_dt_bias=m_gdn_dt_bias, gdn_norm_g=m_gdn_norm_g, gdn_w_out=m_gdn_w_out, final_norm=m_final_norm)
    v = dict(ffn1_norm=v_ffn1_norm, ffn1_w_in=v_ffn1_w_in, ffn1_w_out=v_ffn1_w_out, mix_norm=v_mix_norm, ffn2_norm=v_ffn2_norm, ffn2_w_in=v_ffn2_w_in, ffn2_w_out=v_ffn2_w_out, hyb_w_in=v_hyb_w_in, hyb_dw_w=v_hyb_dw_w, hyb_dw_b=v_hyb_dw_b, hyb_ln_g=v_hyb_ln_g, hyb_ln_b=v_hyb_ln_b, hyb_w_out=v_hyb_w_out, gdn_w_in=v_gdn_w_in, gdn_conv_w=v_gdn_conv_w, gdn_A_log=v_gdn_A_log, gdn_dt_bias=v_gdn_dt_bias, gdn_norm_g=v_gdn_norm_g, gdn_w_out=v_gdn_w_out, final_norm=v_final_norm)
    return _train_step(x, positions, loss_target, w, m, v)


_WEIGHT_ORDER = ("ffn1_norm", "ffn1_w_in", "ffn1_w_out", "mix_norm", "ffn2_norm", "ffn2_w_in", "ffn2_w_out",
                 "hyb_w_in", "hyb_dw_w", "hyb_dw_b", "hyb_ln_g", "hyb_ln_b", "hyb_w_out", "gdn_w_in", "gdn_conv_w",
                 "gdn_A_log", "gdn_dt_bias", "gdn_norm_g", "gdn_w_out", "final_norm")
_BIG = ("ffn1_w_in", "ffn1_w_out", "ffn2_w_in", "ffn2_w_out", "hyb_w_in", "hyb_w_out", "gdn_w_in", "gdn_w_out")
_PACK_COLS = 1024


def _lane16(vec):
    return jnp.zeros((1, 128), _F32).at[0, _GDN_HEADS:2 * _GDN_HEADS].set(vec)


def _pack_rows(parts):
    rows, where, at = [], [], 0
    for p in parts:
        flat = p.reshape(-1).astype(_F32)
        n = -(-flat.shape[0] // (8 * _PACK_COLS)) * 8
        rows.append(jnp.pad(flat, (0, n * _PACK_COLS - flat.shape[0])).reshape(n, _PACK_COLS))
        where.append((at, n, flat.shape[0], p.shape))
        at += n
    return jnp.concatenate(rows, axis=0), where


def _unpack_rows(pack, where):
    return [pack[at:at + n].reshape(-1)[:size].reshape(shape) for at, n, size, shape in where]


def _train_step(x, positions, loss_target, w, m, v):
    n_layers = w["ffn1_norm"].shape[0]
    n_hyb, n_gdn = w["hyb_w_in"].shape[0], w["gdn_w_in"].shape[0]
    d_model = x.shape[-1]
    xh, target = x[0], loss_target[0]
    chip = 2 * lax.axis_index("x") + lax.axis_index("y")
    where = jnp.stack([lax.axis_index("c"), chip]).astype(jnp.int32)

    groups = (("ffn1_w_in", "ffn1_w_out"), ("hyb_w_in", "hyb_w_out"), ("ffn2_w_in", "ffn2_w_out"),
              ("gdn_w_in", "gdn_w_out"))
    steps = [(0, (0, 2)), (0, (1, 2)), (1, (0, 1)), (2, (0, 1)), (3, (0, 1))]
    bufs = [[_cast_slab(n, w[n], where) for n in names] for names in groups]
    pending, token = [], where
    for si, (gi, part) in enumerate(steps):
        send, recv, bufs[gi], token = _gather_start(f"gather_start_{si}", bufs[gi], token, part)
        pending.append((send, recv))
    gdn_cols = w["gdn_w_in"].shape[-1]
    n_qkvz = 4 * _GDN_HEADS * _GDN_DIM
    n_ba = 4 * gdn_cols - n_qkvz
    wv = {}

    def arrive(si, after):
        gi, part = steps[si]
        send, recv = pending[si]
        bufs[gi] = _gather_wait(f"gather_wait_{si}", send, recv, bufs[gi], after, part)
        bufs[gi] = _gather_forward(f"gather_pass_{si}", bufs[gi], part)
        a, b = bufs[gi]
        kind = groups[gi][0]
        if kind == "gdn_w_in":
            gdn_in = jnp.transpose(a, (0, 2, 1, 3)).reshape(n_gdn, d_model, 4 * gdn_cols)
            wv["gdn_qkvz"] = gdn_in[:, :, :n_qkvz]
            wv["gdn_ba"] = jnp.pad(gdn_in[:, :, n_qkvz:], ((0, 0), (0, 0), (0, 128 - n_ba)))
        else:
            wv[kind] = a
        wv[groups[gi][1]] = b.reshape(b.shape[0], -1, d_model)

    def own_columns(shard):
        cols = shard.shape[-1]
        zeros = jnp.zeros(shard.shape[:-1] + (4 * cols,), _F32)
        return lax.dynamic_update_slice_in_dim(zeros, shard, chip * cols, axis=2)

    small_pack, small_where = _pack_rows([own_columns(w["hyb_dw_w"]), own_columns(w["gdn_conv_w"])])
    dw_w, conv_w = _unpack_rows(0.5 * _all_reduce_small(small_pack), small_where)
    dw_w = jnp.pad(dw_w, ((0, 0), (0, _HALO - _CONV_WIDTH), (0, 0)))
    tabs = _rotary_tables(positions)

    def mixer_args(layer):
        i = layer // 2
        if layer % 2 == 0:
            return (wv["hyb_w_in"], wv["hyb_w_out"], dw_w[i], w["hyb_dw_b"][i][None], w["hyb_ln_g"][i][None],
                    w["hyb_ln_b"][i][None], tabs, i)
        return (wv["gdn_qkvz"], wv["gdn_ba"], wv["gdn_w_out"], conv_w[i], _lane16(w["gdn_A_log"][i]),
                _lane16(w["gdn_dt_bias"][i]), w["gdn_norm_g"][i][None], i)

    h = xh
    saved = []
    arrive(0, token)
    for layer in range(n_layers):
        if layer == 1:
            arrive(1, h)
        h, s1 = _ffn_fwd(h, w["ffn1_norm"][layer][None], wv["ffn1_w_in"], wv["ffn1_w_out"], layer)
        gamma = w["mix_norm"][layer][None]
        if layer % 2 == 0:
            if layer == 0:
                arrive(2, h)
            h, sm = _hyb_fwd(h, gamma, *mixer_args(layer))
        else:
            if layer == 1:
                arrive(4, h)
            h, sm = _gdn_fwd(h, gamma, *mixer_args(layer))
        if layer == 0:
            arrive(3, h)
        h, s2 = _ffn_fwd(h, w["ffn2_norm"][layer][None], wv["ffn2_w_in"], wv["ffn2_w_out"], layer)
        saved.append((s1, sm, s2))
    dh, loss_row, d_final = _loss_head(h, w["final_norm"][None], target)

    d_norm = {k: [None] * n_layers for k in ("ffn1_norm", "mix_norm", "ffn2_norm")}
    d_hyb = [None] * n_hyb
    d_gdn = [None] * n_gdn
    grads = {name: None for name in _BIG}
    n_of = {name: w[name].shape[0] for name in _BIG}

    def rs_begin(layer, full, token):
        names = list(full)
        got = _rs_sibling(f"rs_sib_{layer}", [full[n] for n in names])
        sums = [_pair_sum(full[n], g, where) for n, g in zip(names, got)]
        send, recv, sums, lands, token = _rs_chips_start(f"rs_chips_start_{layer}", sums, token)
        return (layer, names, send, recv, sums, lands), token

    def rs_end(state, after):
        layer, names, send, recv, sums, lands = state
        sums, lands = _rs_chips_wait(f"rs_chips_wait_{layer}", send, recv, sums, lands, after)
        index = [layer if n.startswith("ffn") else layer // 2 for n in names]
        for n, li, pair, land in zip(names, index, sums, lands):
            grads[n] = _sum_of_four(pair, land, where, grads[n], li, n_of[n])
        joined = _rs_join(f"rs_join_{layer}", [grads[n] for n in names], index)
        grads.update(zip(names, joined))

    in_flight = None
    for layer in reversed(range(n_layers)):
        s1, sm, s2 = saved[layer]
        full = {}
        dh, d_norm["ffn2_norm"][layer], full["ffn2_w_in"], full["ffn2_w_out"] = _ffn_bwd(
            dh, s2, w["ffn2_norm"][layer][None], wv["ffn2_w_in"], wv["ffn2_w_out"], layer)
        gamma = w["mix_norm"][layer][None]
        if layer % 2 == 0:
            dh, d_norm["mix_norm"][layer], d_hyb[layer // 2], full["hyb_w_in"], full["hyb_w_out"] = _hyb_bwd(
                dh, sm, gamma, *mixer_args(layer))
        else:
            dh, d_norm["mix_norm"][layer], d_gdn[layer // 2], full["gdn_w_in"], full["gdn_w_out"] = _gdn_bwd(
                dh, sm, gamma, *mixer_args(layer), n_ba)
        dh, d_norm["ffn1_norm"][layer], full["ffn1_w_in"], full["ffn1_w_out"] = _ffn_bwd(
            dh, s1, w["ffn1_norm"][layer][None], wv["ffn1_w_in"], wv["ffn1_w_out"], layer)
        if in_flight is not None:
            rs_end(in_flight, dh)
        in_flight, token = rs_begin(layer, full, token)
        dh, token = lax.optimization_barrier((dh, token))
    grad_x = dh[None]

    stack = lambda rows: jnp.concatenate(rows, axis=0)
    misc = jnp.concatenate([stack([g[1] for g in d_gdn])[:, _GDN_HEADS:2 * _GDN_HEADS].reshape(-1),
                            stack([g[2] for g in d_gdn])[:, _GDN_HEADS:2 * _GDN_HEADS].reshape(-1),
                            stack([g[3] for g in d_gdn]).reshape(-1), loss_row[0, :1]])
    n_a = n_gdn * _GDN_HEADS
    parts = [stack(d_norm["ffn1_norm"]), stack(d_norm["mix_norm"]), stack(d_norm["ffn2_norm"]), d_final,
             stack([g[1] for g in d_hyb]), stack([g[2] for g in d_hyb]), stack([g[3] for g in d_hyb]), misc,
             jnp.stack([g[0][:_CONV_WIDTH] for g in d_hyb]), jnp.stack([g[0] for g in d_gdn])]
    pack, pack_where = _pack_rows(parts)
    red = _unpack_rows(_all_reduce_small(pack), pack_where)
    misc = red[7]
    cw = w["hyb_dw_w"].shape[-1]
    gw = w["gdn_conv_w"].shape[-1]
    grads.update(
        ffn1_norm=red[0], mix_norm=red[1], ffn2_norm=red[2], final_norm=red[3].reshape(-1),
        hyb_dw_b=red[4], hyb_ln_g=red[5], hyb_ln_b=red[6],
        gdn_A_log=misc[:n_a].reshape(n_gdn, _GDN_HEADS), gdn_dt_bias=misc[n_a:2 * n_a].reshape(n_gdn, _GDN_HEADS),
        gdn_norm_g=misc[2 * n_a:2 * n_a + n_gdn * _GDN_DIM].reshape(n_gdn, _GDN_DIM),
        hyb_dw_w=lax.dynamic_slice_in_dim(red[8], chip * cw, cw, axis=2),
        gdn_conv_w=lax.dynamic_slice_in_dim(red[9], chip * gw, gw, axis=2))
    loss = misc[2 * n_a + n_gdn * _GDN_DIM]

    small = [n for n in _WEIGHT_ORDER if n not in _BIG]
    packs = [_pack_rows([src[n] for n in small]) for src in (w, grads, m, v)]
    outs = _adamw(*[p[0] for p in packs])
    delta, new_m, new_v = {}, {}, {}
    for res, o in zip((delta, new_m, new_v), outs):
        res.update(dict(zip(small, _unpack_rows(o, packs[0][1]))))
    late = in_flight[1]
    early = {name: _adamw(w[name], grads[name], m[name], v[name], lo=1 if name in late else 0) for name in _BIG}
    early, outs, token = lax.optimization_barrier((early, outs, token))
    rs_end(in_flight, token)
    for name in _BIG:
        res = early[name]
        if name in late:
            res = _adamw(w[name], grads[name], m[name], v[name], lo=0, hi=1, into=res)
        delta[name], new_m[name], new_v[name] = res
    return (loss, grad_x, *[grads[n] for n in _WEIGHT_ORDER], *[delta[n] for n in _WEIGHT_ORDER],
            *[new_m[n] for n in _WEIGHT_ORDER], *[new_v[n] for n in _WEIGHT_ORDER])
```

```python
import functools

import jax
import jax.numpy as jnp
from jax import lax
from jax.experimental import pallas as pl
from jax.experimental.pallas import tpu as pltpu

_BF = jnp.bfloat16
_F32 = jnp.float32
_NORM_EPS = 1e-6
_VMEM_LIMIT_BYTES = 56 * 1024 * 1024

_A_HEADS = 8
_A_HEAD_DIM = 64
_WIN_BLOCK = 128
_DILATIONS = (1, 4, 16)
_CONV_WIDTH = 31
_GDN_HEADS = 8
_GDN_DIM = 128
_GDN_CHUNK = 64
_GDN_SHORT_CONV = 4

_ADAM_LR = 0.001
_ADAM_B1 = 0.9
_ADAM_B2 = 0.999
_ADAM_EPS = 1e-08
_ADAM_WD = 0.01
_ADAM_STEP = 10


def _params(sem):
    return pltpu.CompilerParams(dimension_semantics=sem, vmem_limit_bytes=_VMEM_LIMIT_BYTES)


def _dot(a, b, dn=(((1,), (0,)), ((), ()))):
    return lax.dot_general(a.astype(_BF), b.astype(_BF), dn, preferred_element_type=_F32)


def _dot_nt(a, b):
    return _dot(a, b, (((1,), (1,)), ((), ())))


def _dot_tn(a, b):
    return _dot(a, b, (((0,), (0,)), ((), ())))


_NN = (((1,), (0,)), ((), ()))
_NT = (((1,), (1,)), ((), ()))
_TN = (((0,), (0,)), ((), ()))
_ANY = pl.BlockSpec(memory_space=pl.ANY)


def _mm(name, a, b, *, grid, a_spec, b_spec, dn, out_shape, out_spec, red=None, acc_shape=None,
        pre_a=None, pre_b=None, epi=None, extras=(), extra_specs=(), sum_shape=None):
    n_ex = len(extras)
    n_in = 2 + n_ex
    nk = grid[red] if red is not None else 1
    n_out = 1 if sum_shape is None else 2

    def body(*refs):
        a_ref, b_ref = refs[0], refs[1]
        ex = refs[2:2 + n_ex]
        o_ref = refs[n_in]
        av, bv = a_ref[...], b_ref[...]
        if pre_a is not None:
            av = pre_a(av)
        if pre_b is not None:
            bv = pre_b(bv)
        part = lax.dot_general(av.astype(_BF), bv.astype(_BF), dn, preferred_element_type=_F32)

        def finish(acc):
            res = epi(acc, *[e[...] for e in ex]) if epi is not None else acc
            if sum_shape is not None:
                res, term = res
                s_ref = refs[n_in + 1]
                first = functools.reduce(jnp.logical_and, [pl.program_id(ax) == 0 for ax in range(len(grid)) if ax != red])

                @pl.when(first)
                def _():
                    s_ref[...] = jnp.zeros_like(s_ref)

                s_ref[...] += term
            if isinstance(res, tuple):
                for idx, val in enumerate(res):
                    o_ref[idx] = val.astype(o_ref.dtype)
            else:
                o_ref[...] = res.astype(o_ref.dtype)

        if red is None:
            finish(part)
        else:
            acc_ref = refs[n_in + n_out]
            k = pl.program_id(red)

            @pl.when(k == 0)
            def _():
                acc_ref[...] = part

            @pl.when(k > 0)
            def _():
                acc_ref[...] += part

            @pl.when(k == nk - 1)
            def _():
                finish(acc_ref[...])

    if sum_shape is not None:
        out_shape = [out_shape, jax.ShapeDtypeStruct(sum_shape, _F32)]
        out_spec = [out_spec, pl.BlockSpec(sum_shape, lambda *g: (0,) * len(sum_shape))]
    return pl.pallas_call(
        body, name=name, grid=grid, in_specs=[a_spec, b_spec, *extra_specs], out_specs=out_spec, out_shape=out_shape,
        scratch_shapes=[pltpu.VMEM(acc_shape, _F32)] if red is not None else [],
        compiler_params=_params(("arbitrary",) * len(grid)),
    )(a, b, *extras)


def _rowmap(name, body, tiled, params, tile_outs, param_outs=(), *, rows, tm, ncol=1, col_outer=False,
            into=None):
    nt, npar, nto, npo = len(tiled), len(params), len(tile_outs), len(param_outs)
    assert rows % tm == 0, (name, rows, tm)
    nrow = rows // tm
    grid = (ncol, nrow) if col_outer else (nrow, ncol)

    def ij(g):
        return (g[1], g[0]) if col_outer else (g[0], g[1])

    def cidx(col, j):
        return col(j) if callable(col) else col

    in_specs, operands = [], []
    for arr, width, col in tiled:
        operands.append(arr)
        if arr.ndim == 3:
            in_specs.append(pl.BlockSpec((arr.shape[0], tm, width),
                                         lambda *g, col=col: (0, ij(g)[0], cidx(col, ij(g)[1]))))
        else:
            in_specs.append(pl.BlockSpec((tm, width), lambda *g, col=col: (ij(g)[0], cidx(col, ij(g)[1]))))
    for p in params:
        if isinstance(p, tuple):
            arr, width, col = p
            operands.append(arr)
            in_specs.append(pl.BlockSpec((arr.shape[0], width), lambda *g, col=col: (0, cidx(col, ij(g)[1]))))
        else:
            operands.append(p)
            in_specs.append(pl.BlockSpec(p.shape, lambda *g, nd=p.ndim: (0,) * nd))
    out_shape, out_specs = [], []
    for total, width, col, dt in tile_outs:
        out_shape.append(jax.ShapeDtypeStruct((rows, total), dt))
        out_specs.append(pl.BlockSpec((tm, width), lambda *g, col=col: (ij(g)[0], cidx(col, ij(g)[1]))))
    for shp in param_outs:
        out_shape.append(jax.ShapeDtypeStruct(shp, _F32))
        out_specs.append(pl.BlockSpec(shp, lambda *g, nd=len(shp): (0,) * nd))
    n_in = nt + npar
    aliases = {}
    if into is not None:
        operands.append(into[0])
        in_specs.append(_ANY)
        aliases = {n_in: into[1]}
        n_in += 1

    def kern(*refs):
        vals = [r[...] for r in refs[:nt + npar]]
        touts = refs[n_in:n_in + nto]
        pouts = refs[n_in + nto:]
        res = body(*vals)
        if not isinstance(res, (tuple, list)):
            res = (res,)
        for r, val in zip(touts, res[:nto]):
            r[...] = val.astype(r.dtype)
        if npo:
            @pl.when((pl.program_id(0) == 0) & (pl.program_id(1) == 0))
            def _():
                for r in pouts:
                    r[...] = jnp.zeros_like(r)

            for r, val in zip(pouts, res[nto:]):
                r[...] += val

    return pl.pallas_call(
        kern, name=name, grid=grid, in_specs=in_specs, out_specs=out_specs, out_shape=out_shape,
        input_output_aliases=aliases, compiler_params=_params(("arbitrary", "arbitrary")),
    )(*operands)


def _vjp_body(fn, n_in, diff, n_ct):
    def body(*vals):
        ins, cts = vals[:n_in], vals[n_in:n_in + n_ct]
        outs, pull = jax.vjp(fn, *ins)
        single = not isinstance(outs, (tuple, list))
        if single:
            grads = pull(cts[0].astype(outs.dtype))
        else:
            grads = pull(tuple(c.astype(o.dtype) for c, o in zip(cts, outs)))
        return tuple(grads[i] for i in diff)
    return body


_TM = 512
_TM_WIDE = 1024


def _rms(h, gamma):
    hf = h.astype(_F32)
    return hf * lax.rsqrt(jnp.mean(hf * hf, axis=-1, keepdims=True) + _NORM_EPS) * gamma


def _rmsnorm_fwd(h, gamma):
    s, d = h.shape
    return _rowmap("rmsnorm_fwd", lambda hv, gv: _rms(hv, gv).astype(_BF), [(h, d, 0)], [gamma],
                   [(d, d, 0, _BF)], rows=s, tm=_TM)[0]


def _norm_bwd_epi(dxn, hv, dhv, gv):
    _, pull = jax.vjp(_rms, hv, gv)
    dh, dg = pull(dxn)
    return dhv + dh, dg


def _norm_bwd_args(h, dh_out, gamma, tm):
    d = h.shape[1]
    row = pl.BlockSpec((tm, d), lambda i, j: (i, 0))
    return dict(extras=(h, dh_out, gamma), extra_specs=(row, row, pl.BlockSpec((1, d), lambda i, j: (0, 0))),
                sum_shape=(1, d))


def _silu(x):
    return x * jax.nn.sigmoid(x)


def _ffn_up(h, gamma, w_in, layer):
    s, d = h.shape
    ns = w_in.shape[-1]
    tm = _CONV_TM

    def body(h_ref, gam_ref, w_ref, x_ref, g_ref, u_ref, a_ref):
        xv = _rms(h_ref[...], gam_ref[...]).astype(_BF)
        x_ref[...] = xv
        for half in range(2):
            cols = slice(half * ns, (half + 1) * ns)
            g = jnp.dot(xv, w_ref[half], preferred_element_type=_F32)
            u = jnp.dot(xv, w_ref[half + 2], preferred_element_type=_F32)
            g_ref[:, cols] = g.astype(_BF)
            u_ref[:, cols] = u.astype(_BF)
            a_ref[:, cols] = (_silu(g) * u).astype(_BF)

    row = pl.BlockSpec((tm, d), lambda i: (i, 0))
    out = pl.BlockSpec((tm, 2 * ns), lambda i: (i, 0))
    sds = jax.ShapeDtypeStruct((s, 2 * ns), _BF)
    return pl.pallas_call(
        body, name="ffn_up", grid=(s // tm,),
        in_specs=[row, pl.BlockSpec((1, d), lambda i: (0, 0)), pl.BlockSpec((None, 4, d, ns), lambda i: (layer, 0, 0, 0))],
        out_specs=[row, out, out, out], out_shape=[jax.ShapeDtypeStruct((s, d), _BF), sds, sds, sds],
        compiler_params=_params(("arbitrary",)),
    )(h, gamma, w_in)


def _proj_residual(name, y, w, layer, h, scale):
    s, k = y.shape
    d = w.shape[-1]
    return _mm(name, y, w, grid=(s // _TM,), a_spec=pl.BlockSpec((_TM, k), lambda i: (i, 0)),
               b_spec=pl.BlockSpec((None, k, d), lambda i: (layer, 0, 0)), dn=_NN,
               out_shape=jax.ShapeDtypeStruct((s, d), _F32), out_spec=pl.BlockSpec((_TM, d), lambda i: (i, 0)),
               epi=lambda acc, hv: hv + scale * acc, extras=(h,),
               extra_specs=(pl.BlockSpec((_TM, d), lambda i: (i, 0)),))


def _ffn_dact(dh, w_out, layer, g, u):
    s, d = dh.shape
    f = g.shape[1]
    ns = f // 2

    def epi(da, gv, uv):
        gf, uf = gv.astype(_F32), uv.astype(_F32)
        sg = jax.nn.sigmoid(gf)
        return da * uf * (sg * (1.0 + gf * (1.0 - sg))), da * gf * sg

    tm = _CONV_TM
    gu = pl.BlockSpec((tm, f), lambda i: (i, 0))
    return _mm("ffn_dact", dh, w_out, grid=(s // tm,), a_spec=pl.BlockSpec((tm, d), lambda i: (i, 0)),
               b_spec=pl.BlockSpec((None, f, d), lambda i: (layer, 0, 0)), dn=_NT,
               out_shape=jax.ShapeDtypeStruct((2, s, f), _BF), out_spec=pl.BlockSpec((2, tm, f), lambda i: (0, i, 0)),
               pre_a=lambda v: 0.5 * v, epi=epi, extras=(g, u), extra_specs=(gu, gu))


def _wgrad_rows(name, y, dh, scale):
    s, k = y.shape
    d = dh.shape[1]
    out = _mm(name, y, dh, grid=(s // _TM,), a_spec=pl.BlockSpec((_TM, k), lambda i: (i, 0)),
              b_spec=pl.BlockSpec((_TM, d), lambda i: (i, 0)), dn=_TN, red=0, acc_shape=(k, d),
              out_shape=jax.ShapeDtypeStruct((k, d), _BF), out_spec=pl.BlockSpec((k, d), lambda i: (0, 0)),
              pre_b=(lambda v: scale * v) if scale != 1.0 else None)
    return out.reshape(4, k // 4, d)


def _ffn_dxn(dz, w_in, layer, h, gamma, dh_out):
    _, s, f = dz.shape
    d, ns = w_in.shape[2], w_in.shape[3]
    tm = _TM_WIDE
    return _mm("ffn_dxn", dz, w_in, grid=(s // tm, 4),
               a_spec=pl.BlockSpec((None, tm, ns), lambda i, j: (j // 2, i, j % 2)),
               b_spec=pl.BlockSpec((None, None, d, ns), lambda i, j: (layer, j, 0, 0)), dn=_NT, red=1,
               acc_shape=(tm, d), out_shape=jax.ShapeDtypeStruct((s, d), _F32),
               out_spec=pl.BlockSpec((tm, d), lambda i, j: (i, 0)), epi=_norm_bwd_epi,
               **_norm_bwd_args(h, dh_out, gamma, tm))


def _ffn_dwin(xn, dz):
    s, d = xn.shape
    ns = dz.shape[2] // 2
    tm = _TM_WIDE
    return _mm("ffn_dwin", xn, dz, grid=(4, s // tm), a_spec=pl.BlockSpec((tm, d), lambda j, i: (i, 0)),
               b_spec=pl.BlockSpec((None, tm, ns), lambda j, i: (j // 2, i, j % 2)), dn=_TN, red=1,
               acc_shape=(d, ns), out_shape=jax.ShapeDtypeStruct((4, d, ns), _BF),
               out_spec=pl.BlockSpec((None, d, ns), lambda j, i: (j, 0, 0)))


def _ffn_fwd(h, gamma, w_in, w_out, layer):
    xn, g, u, act = _ffn_up(h, gamma, w_in, layer)
    h_new = _proj_residual("ffn_down", act, w_out, layer, h, 0.5)
    return h_new, (h, xn, g, u, act)


def _ffn_bwd(dh, saved, gamma, w_in, w_out, layer):
    h, xn, g, u, act = saved
    dz = _ffn_dact(dh, w_out, layer, g, u)
    d_w_out = _wgrad_rows("ffn_dwout", act, dh, 0.5)
    dh_in, dgamma = _ffn_dxn(dz, w_in, layer, h, gamma, dh)
    d_w_in = _ffn_dwin(xn, dz)
    return dh_in, dgamma, d_w_in, d_w_out


def _loss_head(h, gamma, target):
    s, d = h.shape

    def body(hv, tv, gv):
        y, pull = jax.vjp(_rms, hv, gv)
        err = y - tv
        loss = 0.5 * jnp.sum(jnp.mean(err * err, axis=-1, keepdims=True), axis=0, keepdims=True)
        dh, dg = pull(err * (1.0 / d))
        return dh, jnp.broadcast_to(loss, (1, 128)), dg

    return _rowmap("loss_head", body, [(h, d, 0), (target, d, 0)], [gamma], [(d, d, 0, _F32)],
                   [(1, 128), (1, d)], rows=s, tm=_TM)


_ROT = _A_HEAD_DIM // 4
_ROPE_THETA = 500000.0
_A_WIDTH = _A_HEADS * _A_HEAD_DIM
_NEG = -1e30


def _rotary_tables(positions):
    inv_freq = jnp.power(jnp.float32(_ROPE_THETA), -jnp.arange(0, _ROT, 2, dtype=_F32) / _ROT)
    ang = positions.reshape(-1, 1).astype(_F32) * inv_freq
    cos, sin = jnp.cos(ang), jnp.sin(ang)
    s = ang.shape[0]
    pad = jnp.zeros((s, _A_HEAD_DIM - _ROT), _F32)
    c_head = jnp.concatenate([cos, cos, pad + 1.0], axis=1)
    s_head = jnp.concatenate([-sin, sin, pad], axis=1)
    c2, s2 = jnp.tile(c_head, (1, 2)), jnp.tile(s_head, (1, 2))
    scale = _A_HEAD_DIM ** -0.5
    tab_c = jnp.stack([scale * c2, c2, jnp.ones_like(c2)])
    tab_s = jnp.stack([scale * s2, s2, jnp.zeros_like(s2)])
    return tab_c, tab_s


def _pair_swap(z):
    lane = lax.broadcasted_iota(jnp.int32, z.shape, 1) % _A_HEAD_DIM
    half = _ROT // 2
    up = jnp.where(lane < _ROT, pltpu.roll(z, half, axis=1), 0.0)
    return jnp.where(lane < half, pltpu.roll(z, 128 - half, axis=1), up)


def _rot_apply(x, c, s, transpose):
    outs = []
    for k in range(x.shape[1] // 128):
        xs = x[:, k * 128:(k + 1) * 128]
        outs.append(c * xs + (_pair_swap(s * xs) if transpose else s * _pair_swap(xs)))
    return jnp.concatenate(outs, axis=1)


def _qkv_prep(proj, tab_c, tab_s):
    s = proj.shape[0]
    w = _A_WIDTH

    def body(x_ref, c_ref, s_ref, o_ref):
        o_ref[...] = _rot_apply(x_ref[...], c_ref[...], s_ref[...], False).astype(_BF)

    tab = pl.BlockSpec((None, _TM, 128), lambda i, j: (j, i, 0))
    blk = pl.BlockSpec((_TM, w), lambda i, j: (i, j))
    return pl.pallas_call(
        body, name="qkv_prep", grid=(s // _TM, 3), in_specs=[blk, tab, tab], out_specs=blk,
        out_shape=jax.ShapeDtypeStruct((s, 3 * w), _BF), compiler_params=_params(("arbitrary", "arbitrary")),
    )(proj, tab_c, tab_s)


def _head_masks():
    lane = lax.broadcasted_iota(jnp.int32, (_WIN_BLOCK, 128), 1)
    return [lane // _A_HEAD_DIM == hh for hh in range(2)]


def _band_masks(first_block):
    qi = lax.broadcasted_iota(jnp.int32, (_WIN_BLOCK, _WIN_BLOCK), 0)
    kj = lax.broadcasted_iota(jnp.int32, (_WIN_BLOCK, _WIN_BLOCK), 1)
    return (kj >= qi) & jnp.logical_not(first_block), kj <= qi


def _attn_fwd(qkv, dil):
    s = qkv.shape[0]
    w = _A_WIDTH
    length = s // dil
    nb = length // _WIN_BLOCK
    view = qkv.reshape(length, dil * 3 * w)

    def body(q_ref, kp_ref, kc_ref, vp_ref, vc_ref, o_ref, l_ref):
        allow_p, allow_c = _band_masks(pl.program_id(1) == 0)
        hm = _head_masks()
        for sl in range(w // 128):
            cols = slice(sl * 128, (sl + 1) * 128)
            q, kp, kc, vp, vc = q_ref[:, cols], kp_ref[:, cols], kc_ref[:, cols], vp_ref[:, cols], vc_ref[:, cols]
            o_s = jnp.zeros((_WIN_BLOCK, 128), _F32)
            l_s = jnp.zeros((_WIN_BLOCK, 128), _F32)
            for hh in range(2):
                qh = jnp.where(hm[hh], q, jnp.zeros_like(q))
                sp = jnp.where(allow_p, _dot_nt(qh, kp), _NEG)
                sc = jnp.where(allow_c, _dot_nt(qh, kc), _NEG)
                m = jnp.maximum(jnp.max(sp, axis=1, keepdims=True), jnp.max(sc, axis=1, keepdims=True))
                pp, pc = jnp.exp(sp - m), jnp.exp(sc - m)
                den = jnp.sum(pp, axis=1, keepdims=True) + jnp.sum(pc, axis=1, keepdims=True)
                oh = (_dot(pp, vp) + _dot(pc, vc)) / den
                o_s = jnp.where(hm[hh], oh, o_s)
                l_s = jnp.where(hm[hh], m + jnp.log(den), l_s)
            o_ref[:, cols] = o_s
            l_ref[:, cols] = l_s

    def at(col, prev):
        if prev:
            return pl.BlockSpec((_WIN_BLOCK, w), lambda r, n: (jnp.maximum(n - 1, 0), 3 * r + col))
        return pl.BlockSpec((_WIN_BLOCK, w), lambda r, n: (n, 3 * r + col))

    out = pl.BlockSpec((_WIN_BLOCK, w), lambda r, n: (n, r))
    sds = jax.ShapeDtypeStruct((length, dil * w), _F32)
    o, lse = pl.pallas_call(
        body, name=f"attn_fwd_d{dil}", grid=(dil, nb),
        in_specs=[at(0, False), at(1, True), at(1, False), at(2, True), at(2, False)],
        out_specs=[out, out], out_shape=[sds, sds], compiler_params=_params(("arbitrary", "arbitrary")),
    )(view, view, view, view, view)
    return o.reshape(s, w), lse.reshape(s, w)


def _attn_bwd(qkv, o, lse, do, dlse, dil):
    s = qkv.shape[0]
    w = _A_WIDTH
    length = s // dil
    nb = length // _WIN_BLOCK
    view = qkv.reshape(length, dil * 3 * w)
    o, lse, do, dlse = (t.reshape(length, dil * w) for t in (o, lse, do, dlse))

    def body(q_ref, kp_ref, kc_ref, vp_ref, vc_ref, o_ref, l_ref, do_ref, dl_ref, out_ref,
             car_q, car_k, car_v, cur_q, cur_kc, cur_vc, cur_kp, cur_vp):
        n = pl.program_id(1)

        @pl.when(n == 0)
        def _():
            car_q[...] = jnp.zeros_like(car_q)
            car_k[...] = jnp.zeros_like(car_k)
            car_v[...] = jnp.zeros_like(car_v)

        @pl.when(n == nb)
        def _():
            cur_kp[...] = jnp.zeros_like(cur_kp)
            cur_vp[...] = jnp.zeros_like(cur_vp)

        @pl.when(n < nb)
        def _():
            allow_p, allow_c = _band_masks(n == 0)
            hm = _head_masks()
            for sl in range(w // 128):
                cols = slice(sl * 128, (sl + 1) * 128)
                q, kp, kc, vp, vc = q_ref[:, cols], kp_ref[:, cols], kc_ref[:, cols], vp_ref[:, cols], vc_ref[:, cols]
                ov, lv, dov, dlv = o_ref[:, cols], l_ref[:, cols], do_ref[:, cols], dl_ref[:, cols]
                zero = jnp.zeros((_WIN_BLOCK, 128), _F32)
                dq_s, dkp_s, dkc_s, dvp_s, dvc_s = zero, zero, zero, zero, zero
                for hh in range(2):
                    qh = jnp.where(hm[hh], q, jnp.zeros_like(q))
                    doh = jnp.where(hm[hh], dov, 0.0).astype(_BF)
                    lse_h = jnp.sum(jnp.where(hm[hh], lv, 0.0), axis=1, keepdims=True) * (1.0 / _A_HEAD_DIM)
                    delta = jnp.sum(jnp.where(hm[hh], dov * ov, 0.0), axis=1, keepdims=True)
                    dl = jnp.sum(jnp.where(hm[hh], dlv, 0.0), axis=1, keepdims=True)
                    pp = jnp.exp(jnp.where(allow_p, _dot_nt(qh, kp), _NEG) - lse_h)
                    pc = jnp.exp(jnp.where(allow_c, _dot_nt(qh, kc), _NEG) - lse_h)
                    dsp = (pp * (_dot_nt(doh, vp) - delta + dl)).astype(_BF)
                    dsc = (pc * (_dot_nt(doh, vc) - delta + dl)).astype(_BF)
                    dq_s = dq_s + jnp.where(hm[hh], _dot(dsp, kp) + _dot(dsc, kc), 0.0)
                    dkp_s = dkp_s + _dot_tn(dsp, qh)
                    dkc_s = dkc_s + _dot_tn(dsc, qh)
                    dvp_s = dvp_s + _dot_tn(pp, doh)
                    dvc_s = dvc_s + _dot_tn(pc, doh)
                cur_q[:, cols] = dq_s
                cur_kp[:, cols] = dkp_s
                cur_kc[:, cols] = dkc_s
                cur_vp[:, cols] = dvp_s
                cur_vc[:, cols] = dvc_s

        out_ref[:, 0:w] = car_q[...]
        out_ref[:, w:2 * w] = car_k[...] + cur_kp[...]
        out_ref[:, 2 * w:3 * w] = car_v[...] + cur_vp[...]

        @pl.when(n < nb)
        def _():
            car_q[...] = cur_q[...]
            car_k[...] = cur_kc[...]
            car_v[...] = cur_vc[...]

    def at(col, prev):
        if prev:
            return pl.BlockSpec((_WIN_BLOCK, w), lambda r, n: (jnp.maximum(jnp.minimum(n, nb - 1) - 1, 0), 3 * r + col))
        return pl.BlockSpec((_WIN_BLOCK, w), lambda r, n: (jnp.minimum(n, nb - 1), 3 * r + col))

    blk = pl.BlockSpec((_WIN_BLOCK, w), lambda r, n: (jnp.minimum(n, nb - 1), r))
    scratch = [pltpu.VMEM((_WIN_BLOCK, w), _F32) for _ in range(8)]
    out = pl.pallas_call(
        body, name=f"attn_bwd_d{dil}", grid=(dil, nb + 1),
        in_specs=[at(0, False), at(1, True), at(1, False), at(2, True), at(2, False), blk, blk, blk, blk],
        out_specs=pl.BlockSpec((_WIN_BLOCK, 3 * w), lambda r, n: (jnp.maximum(n - 1, 0), r)),
        out_shape=jax.ShapeDtypeStruct((length, dil * 3 * w), _F32), scratch_shapes=scratch,
        compiler_params=_params(("arbitrary", "arbitrary")),
    )(view, view, view, view, view, o, lse, do, dlse)
    return out.reshape(s, 3 * w)


def _mix(o1, o2, o3, l1, l2, l3):
    m = jnp.maximum(jnp.maximum(l1, l2), l3)
    e1, e2, e3 = jnp.exp(l1 - m), jnp.exp(l2 - m), jnp.exp(l3 - m)
    return (e1 * o1 + e2 * o2 + e3 * o3) / (e1 + e2 + e3)


def _attn_mix_fwd(outs, lses):
    s = outs[0].shape[0]
    tiles = [(t, _A_WIDTH, 0) for t in (*outs, *lses)]
    return _rowmap("attn_mix_fwd", _mix, tiles, [], [(2 * _A_WIDTH, _A_WIDTH, 0, _BF)], rows=s, tm=_CONV_TM)[0]


def _attn_mix_bwd(outs, lses, dy):
    s = outs[0].shape[0]
    tiles = [(t, _A_WIDTH, 0) for t in (*outs, *lses, dy)]
    body = _vjp_body(_mix, 6, range(6), 1)
    return _rowmap("attn_mix_bwd", body, tiles, [], [(_A_WIDTH, _A_WIDTH, 0, _F32)] * 6, rows=s, tm=_CONV_TM)


_CONV_TM = 256
_HALO = 32


def _ln_silu(pre, g, b):
    mu = jnp.mean(pre, axis=-1, keepdims=True)
    xc = pre - mu
    y = xc * lax.rsqrt(jnp.mean(xc * xc, axis=-1, keepdims=True) + _NORM_EPS)
    return _silu(y * g + b)


def _glu_into(buf, ua_ref, ub_ref, uah_ref, ubh_ref, first):
    halo = uah_ref[...] * jax.nn.sigmoid(ubh_ref[...])
    buf[0:_HALO, :] = jnp.where(first, 0.0, halo)
    buf[_HALO:, :] = ua_ref[...] * jax.nn.sigmoid(ub_ref[...])


_SHIFT_ROWS = _CONV_TM + _HALO - 8


def _shift_copies(sh, buf):
    for b in range(1, 8):
        sh[b - 1] = buf[b:b + _SHIFT_ROWS, :]


def _rows_at(buf, sh, off, rows):
    a, b = divmod(off, 8)
    return buf[8 * a:8 * a + rows, :] if b == 0 else sh[b - 1, 8 * a:8 * a + rows, :]


def _causal_taps(buf, sh, w_ref, rows):
    base = _HALO - (_CONV_WIDTH - 1)
    acc = jnp.zeros((rows, buf.shape[1]), _F32)
    for j in range(_CONV_WIDTH):
        acc = acc + w_ref[j:j + 1, :] * _rows_at(buf, sh, base + j, rows)
    return acc


def _conv_specs(s):
    tm = _CONV_TM
    per = tm // _HALO
    cw = _A_WIDTH
    cur = lambda col: pl.BlockSpec((tm, cw), lambda i: (i, col))
    prev = lambda col: pl.BlockSpec((_HALO, cw), lambda i: (jnp.maximum(i * per - 1, 0), col))
    return tm, per, cw, cur, prev


def _full2(arr):
    return pl.BlockSpec(arr.shape, lambda i: (0, 0))


def _conv_fwd(proj, dw_w, dw_b, ln_g, ln_b, y):
    s = proj.shape[0]
    tm, per, cw, cur, prev = _conv_specs(s)

    def body(ua_ref, ub_ref, uah_ref, ubh_ref, w_ref, b_ref, g_ref, be_ref, y_in, o_ref, buf, sh):
        del y_in
        _glu_into(buf, ua_ref, ub_ref, uah_ref, ubh_ref, pl.program_id(0) == 0)
        _shift_copies(sh, buf)
        pre = _causal_taps(buf, sh, w_ref, tm) + b_ref[...]
        o_ref[...] = _ln_silu(pre, g_ref[...], be_ref[...]).astype(_BF)

    return pl.pallas_call(
        body, name="conv_fwd", grid=(s // tm,),
        in_specs=[cur(3), cur(4), prev(3), prev(4), _full2(dw_w), _full2(dw_b), _full2(ln_g), _full2(ln_b), _ANY],
        out_specs=pl.BlockSpec((tm, cw), lambda i: (i, 1)), out_shape=jax.ShapeDtypeStruct(y.shape, y.dtype),
        scratch_shapes=[pltpu.VMEM((tm + _HALO, cw), _F32), pltpu.VMEM((7, _SHIFT_ROWS, cw), _F32)],
        input_output_aliases={8: 0}, compiler_params=_params(("arbitrary",)),
    )(proj, proj, proj, proj, dw_w, dw_b, ln_g, ln_b, y)


def _conv_bwd_pre(proj, dw_w, dw_b, ln_g, ln_b, dy):
    s = proj.shape[0]
    tm, per, cw, cur, prev = _conv_specs(s)

    def body(ua_ref, ub_ref, uah_ref, ubh_ref, w_ref, b_ref, g_ref, be_ref, dy_ref, dp_ref, dg_ref, dbe_ref, db_ref, buf, sh):
        i = pl.program_id(0)
        _glu_into(buf, ua_ref, ub_ref, uah_ref, ubh_ref, i == 0)
        _shift_copies(sh, buf)
        pre = _causal_taps(buf, sh, w_ref, tm) + b_ref[...]
        _, pull = jax.vjp(_ln_silu, pre, g_ref[...], be_ref[...])
        dpre, dg, dbe = pull(dy_ref[...])
        dp_ref[...] = dpre

        @pl.when(i == 0)
        def _():
            dg_ref[...] = jnp.zeros_like(dg_ref)
            dbe_ref[...] = jnp.zeros_like(dbe_ref)
            db_ref[...] = jnp.zeros_like(db_ref)

        dg_ref[...] += dg
        dbe_ref[...] += dbe
        db_ref[...] += jnp.sum(dpre, axis=0, keepdims=True)

    vec = pl.BlockSpec((1, cw), lambda i: (0, 0))
    vsd = jax.ShapeDtypeStruct((1, cw), _F32)
    return pl.pallas_call(
        body, name="conv_bwd_pre", grid=(s // tm,),
        in_specs=[cur(3), cur(4), prev(3), prev(4), _full2(dw_w), _full2(dw_b), _full2(ln_g), _full2(ln_b), cur(1)],
        out_specs=[pl.BlockSpec((tm, cw), lambda i: (i, 0)), vec, vec, vec],
        out_shape=[jax.ShapeDtypeStruct((s, cw), _F32), vsd, vsd, vsd],
        scratch_shapes=[pltpu.VMEM((tm + _HALO, cw), _F32), pltpu.VMEM((7, _SHIFT_ROWS, cw), _F32)],
        compiler_params=_params(("arbitrary",)),
    )(proj, proj, proj, proj, dw_w, dw_b, ln_g, ln_b, dy)


def _conv_bwd_taps(proj, dw_w, dpre):
    s = proj.shape[0]
    tm, per, cw, cur, prev = _conv_specs(s)
    nt = s // tm

    def body(ua_ref, ub_ref, uah_ref, ubh_ref, w_ref, dp_ref, dpn_ref, dglu_ref, dw_ref, buf, dbuf, sh, dsh):
        i = pl.program_id(0)
        _glu_into(buf, ua_ref, ub_ref, uah_ref, ubh_ref, i == 0)
        dpre = dp_ref[...]
        dbuf[0:tm, :] = dpre
        dbuf[tm:, :] = jnp.where(i == nt - 1, 0.0, dpn_ref[...])
        _shift_copies(sh, buf)
        _shift_copies(dsh, dbuf)
        acc = jnp.zeros((tm, cw), _F32)
        for j in range(_CONV_WIDTH):
            acc = acc + w_ref[j:j + 1, :] * _rows_at(dbuf, dsh, _CONV_WIDTH - 1 - j, tm)
        dglu_ref[...] = acc

        @pl.when(i == 0)
        def _():
            dw_ref[...] = jnp.zeros_like(dw_ref)

        base = _HALO - (_CONV_WIDTH - 1)
        for j in range(_CONV_WIDTH):
            dw_ref[j:j + 1, :] += jnp.sum(dpre * _rows_at(buf, sh, base + j, tm), axis=0, keepdims=True)

    nxt = pl.BlockSpec((_HALO, cw), lambda i: (jnp.minimum((i + 1) * per, s // _HALO - 1), 0))
    return pl.pallas_call(
        body, name="conv_bwd_taps", grid=(nt,),
        in_specs=[cur(3), cur(4), prev(3), prev(4), _full2(dw_w), pl.BlockSpec((tm, cw), lambda i: (i, 0)), nxt],
        out_specs=[pl.BlockSpec((tm, cw), lambda i: (i, 0)), pl.BlockSpec((_HALO, cw), lambda i: (0, 0))],
        out_shape=[jax.ShapeDtypeStruct((s, cw), _F32), jax.ShapeDtypeStruct((_HALO, cw), _F32)],
        scratch_shapes=[pltpu.VMEM((tm + _HALO, cw), _F32), pltpu.VMEM((tm + _HALO, cw), _F32),
                        pltpu.VMEM((7, _SHIFT_ROWS, cw), _F32), pltpu.VMEM((7, _SHIFT_ROWS, cw), _F32)],
        compiler_params=_params(("arbitrary",)),
    )(proj, proj, proj, proj, dw_w, dpre, dpre)


def _hyb_dproj(dqkv, tab_c, tab_s, dglu, proj):
    s = proj.shape[0]
    w = _A_WIDTH

    def body(d1_ref, d2_ref, d3_ref, c_ref, s_ref, dg_ref, ua_ref, ub_ref, o_ref):
        j = pl.program_id(1)

        @pl.when(j < 3)
        def _():
            x = d1_ref[...] + d2_ref[...] + d3_ref[...]
            o_ref[...] = _rot_apply(x, c_ref[...], s_ref[...], True).astype(_BF)

        @pl.when(j == 3)
        def _():
            o_ref[...] = (dg_ref[...] * jax.nn.sigmoid(ub_ref[...])).astype(_BF)

        @pl.when(j == 4)
        def _():
            sg = jax.nn.sigmoid(ub_ref[...])
            o_ref[...] = (dg_ref[...] * ua_ref[...] * sg * (1.0 - sg)).astype(_BF)

    dq = pl.BlockSpec((_TM, w), lambda i, j: (i, jnp.minimum(j, 2)))
    tab = pl.BlockSpec((None, _TM, 128), lambda i, j: (jnp.minimum(j, 2), i, 0))
    col = lambda c: pl.BlockSpec((_TM, w), lambda i, j: (i, c))
    return pl.pallas_call(
        body, name="hyb_dproj", grid=(s // _TM, 5),
        in_specs=[dq, dq, dq, tab, tab, col(0), col(3), col(4)],
        out_specs=pl.BlockSpec((_TM, w), lambda i, j: (i, j)),
        out_shape=jax.ShapeDtypeStruct((s, 5 * w), _BF), compiler_params=_params(("arbitrary", "arbitrary")),
    )(*dqkv, tab_c, tab_s, dglu, proj, proj)


def _proj_in(name, xn, w, layer):
    s, d = xn.shape
    ns = w.shape[-1]
    return _mm(name, xn, w, grid=(4, s // _TM), a_spec=pl.BlockSpec((_TM, d), lambda j, i: (i, 0)),
               b_spec=pl.BlockSpec((None, None, d, ns), lambda j, i: (layer, j, 0, 0)), dn=_NN,
               out_shape=jax.ShapeDtypeStruct((s, 4 * ns), _F32), out_spec=pl.BlockSpec((_TM, ns), lambda j, i: (i, j)))


def _proj_in_dx(name, dz, w, layer, h, gamma, dh_out):
    s = dz.shape[0]
    d, ns = w.shape[2], w.shape[3]
    return _mm(name, dz, w, grid=(s // _TM, 4), a_spec=pl.BlockSpec((_TM, ns), lambda i, j: (i, j)),
               b_spec=pl.BlockSpec((None, None, d, ns), lambda i, j: (layer, j, 0, 0)), dn=_NT, red=1,
               acc_shape=(_TM, d), out_shape=jax.ShapeDtypeStruct((s, d), _F32),
               out_spec=pl.BlockSpec((_TM, d), lambda i, j: (i, 0)), epi=_norm_bwd_epi,
               **_norm_bwd_args(h, dh_out, gamma, _TM))


def _proj_in_dw(name, xn, dz):
    s, d = xn.shape
    ns = dz.shape[1] // 4
    return _mm(name, xn, dz, grid=(4, s // _TM), a_spec=pl.BlockSpec((_TM, d), lambda j, i: (i, 0)),
               b_spec=pl.BlockSpec((_TM, ns), lambda j, i: (i, j)), dn=_TN, red=1, acc_shape=(d, ns),
               out_shape=jax.ShapeDtypeStruct((4, d, ns), _BF),
               out_spec=pl.BlockSpec((None, d, ns), lambda j, i: (j, 0, 0)))


def _proj_out_dy(name, dh, w, layer):
    s, d = dh.shape
    k = w.shape[1]
    return _mm(name, dh, w, grid=(s // _TM,), a_spec=pl.BlockSpec((_TM, d), lambda i: (i, 0)),
               b_spec=pl.BlockSpec((None, k, d), lambda i: (layer, 0, 0)), dn=_NT,
               out_shape=jax.ShapeDtypeStruct((s, k), _F32), out_spec=pl.BlockSpec((_TM, k), lambda i: (i, 0)))


def _hyb_fwd(h, gamma, w_in, w_out, dw_w, dw_b, ln_g, ln_b, tabs, li):
    hn = _rmsnorm_fwd(h, gamma)
    proj = _proj_in("hyb_in", hn, w_in, li)
    qkv = _qkv_prep(proj, *tabs)
    branches = [_attn_fwd(qkv, dil) for dil in _DILATIONS]
    outs, lses = [b[0] for b in branches], [b[1] for b in branches]
    y = _attn_mix_fwd(outs, lses)
    y = _conv_fwd(proj, dw_w, dw_b, ln_g, ln_b, y)
    h_new = _proj_residual("hyb_out", y, w_out, li, h, 1.0)
    return h_new, (h, hn, proj, qkv, outs, lses, y)


def _hyb_bwd(dh, saved, gamma, w_in, w_out, dw_w, dw_b, ln_g, ln_b, tabs, li):
    h, hn, proj, qkv, outs, lses, y = saved
    dy = _proj_out_dy("hyb_dy", dh, w_out, li)
    d_w_out = _wgrad_rows("hyb_dwout", y, dh, 1.0)
    mixg = _attn_mix_bwd(outs, lses, dy)
    dqkv = [_attn_bwd(qkv, outs[b], lses[b], mixg[b], mixg[3 + b], dil) for b, dil in enumerate(_DILATIONS)]
    dpre, d_ln_g, d_ln_b, d_dw_b = _conv_bwd_pre(proj, dw_w, dw_b, ln_g, ln_b, dy)
    dglu, d_dw_w = _conv_bwd_taps(proj, dw_w, dpre)
    dproj = _hyb_dproj(dqkv, *tabs, dglu, proj)
    dh_in, dgamma = _proj_in_dx("hyb_dhn", dproj, w_in, li, h, gamma, dh)
    d_w_in = _proj_in_dw("hyb_dwin", hn, dproj)
    return dh_in, dgamma, (d_dw_w, d_dw_b, d_ln_g, d_ln_b), d_w_in, d_w_out


_GDN_BLOCK = 128
_GDN_HEADS_PER_STEP = 8


def _gdn_chunk(qs, ks, vs, gates, states, heads):
    n = len(qs)
    hs = range(n)
    c = qs[0].shape[0]
    lane = lax.broadcasted_iota(jnp.int32, gates.shape, 1)
    row = lax.broadcasted_iota(jnp.int32, (c, c), 0)
    col = lax.broadcasted_iota(jnp.int32, (c, c), 1)
    rowc = lax.broadcasted_iota(jnp.int32, (c, 1), 0)
    causal = row >= col
    eye = jnp.where(row == col, 1.0, 0.0)
    beta = [jnp.sum(jnp.where(lane == heads[h], gates, 0.0), axis=1, keepdims=True) for h in hs]
    g = [jnp.sum(jnp.where(lane == heads[h] + _GDN_HEADS, gates, 0.0), axis=1, keepdims=True) for h in hs]
    gc_row = [jnp.sum(jnp.where(row <= col, g[h], 0.0), axis=0, keepdims=True) for h in hs]
    gc_col = [jnp.sum(jnp.where(row == col, gc_row[h], 0.0), axis=1, keepdims=True) for h in hs]
    decay = [jnp.where(causal, jnp.exp(jnp.where(causal, gc_col[h] - gc_row[h], 0.0)), 0.0) for h in hs]
    kb = [ks[h] * beta[h] for h in hs]
    l_mat = [jnp.where(row > col, _dot_nt(kb[h], ks[h]) * decay[h], 0.0) for h in hs]
    t_inv = [eye - l_mat[h] for h in hs]
    x = [_dot(l_mat[h], l_mat[h]) for h in hs]
    for it in range(5):
        t_inv = [t_inv[h] + _dot(t_inv[h], x[h]) for h in hs]
        if it < 4:
            x = [_dot(x[h], x[h]) for h in hs]
    eg = [jnp.exp(gc_col[h]) for h in hs]
    u = [_dot(t_inv[h], vs[h] * beta[h]) for h in hs]
    w = [_dot(t_inv[h], kb[h] * eg[h]) for h in hs]
    attn = [jnp.where(causal, _dot_nt(qs[h], ks[h]) * decay[h], 0.0) for h in hs]
    g_last = [jnp.sum(jnp.where(rowc == c - 1, gc_col[h], 0.0), axis=0, keepdims=True) for h in hs]
    k_dec = [ks[h] * jnp.exp(g_last[h] - gc_col[h]) for h in hs]
    v_new = [u[h] - _dot(w[h], states[h]) for h in hs]
    o = [_dot(qs[h] * eg[h], states[h]) + _dot(attn[h], v_new[h]) for h in hs]
    new_states = [states[h] * jnp.exp(g_last[h]) + _dot_tn(k_dec[h], v_new[h]) for h in hs]
    return o, new_states


def _gdn_chunk_fwd(q, k, v, gates):
    s = q.shape[0]
    cb, c = _GDN_BLOCK, _GDN_CHUNK
    nblk, per = s // cb, cb // c

    hps = _GDN_HEADS_PER_STEP

    def body(q_ref, k_ref, v_ref, g_ref, o_ref, st_ref, state):
        @pl.when(pl.program_id(1) == 0)
        def _():
            state[...] = jnp.zeros_like(state)

        heads = [pl.program_id(0) * hps + hh for hh in range(hps)]
        cols = [slice(hh * _GDN_DIM, (hh + 1) * _GDN_DIM) for hh in range(hps)]
        for ci in range(per):
            rows = slice(ci * c, (ci + 1) * c)
            sts = [state[hh] for hh in range(hps)]
            for hh in range(hps):
                st_ref[hh, ci] = sts[hh]
            outs, new = _gdn_chunk([q_ref[rows, cl] for cl in cols], [k_ref[rows, cl] for cl in cols],
                                   [v_ref[rows, cl] for cl in cols], g_ref[rows, :], sts, heads)
            for hh in range(hps):
                o_ref[rows, cols[hh]] = outs[hh]
                state[hh] = new[hh]

    slab = pl.BlockSpec((cb, hps * _GDN_DIM), lambda h, i: (i, h))
    return pl.pallas_call(
        body, name="gdn_chunk_fwd", grid=(_GDN_HEADS // hps, nblk),
        in_specs=[slab, slab, slab, pl.BlockSpec((cb, 128), lambda h, i: (i, 0))],
        out_specs=[slab, pl.BlockSpec((hps, per, _GDN_DIM, _GDN_DIM), lambda h, i: (h, i, 0, 0))],
        out_shape=[jax.ShapeDtypeStruct((s, _GDN_HEADS * _GDN_DIM), _F32),
                   jax.ShapeDtypeStruct((_GDN_HEADS, s // c, _GDN_DIM, _GDN_DIM), _F32)],
        scratch_shapes=[pltpu.VMEM((hps, _GDN_DIM, _GDN_DIM), _F32)],
        compiler_params=_params(("arbitrary", "arbitrary")),
    )(q, k, v, gates)


def _gdn_chunk_bwd(q, k, v, gates, states, do):
    s = q.shape[0]
    cb, c = _GDN_BLOCK, _GDN_CHUNK
    nblk, per = s // cb, cb // c

    hps = _GDN_HEADS_PER_STEP

    def body(q_ref, k_ref, v_ref, g_ref, st_ref, do_ref, dq_ref, dk_ref, dv_ref, dg_ref, dstate):
        @pl.when(pl.program_id(1) == 0)
        def _():
            dstate[...] = jnp.zeros_like(dstate)

        heads = [pl.program_id(0) * hps + hh for hh in range(hps)]
        cols = [slice(hh * _GDN_DIM, (hh + 1) * _GDN_DIM) for hh in range(hps)]
        for ci in reversed(range(per)):
            rows = slice(ci * c, (ci + 1) * c)
            fn = functools.partial(_gdn_chunk, heads=heads)
            _, pull = jax.vjp(fn, [q_ref[rows, cl] for cl in cols], [k_ref[rows, cl] for cl in cols],
                              [v_ref[rows, cl] for cl in cols], g_ref[rows, :], [st_ref[hh, ci] for hh in range(hps)])
            dq, dk, dv, dg, dst = pull(([do_ref[rows, cl] for cl in cols], [dstate[hh] for hh in range(hps)]))
            dg_ref[rows, :] = dg
            for hh in range(hps):
                dq_ref[rows, cols[hh]] = dq[hh]
                dk_ref[rows, cols[hh]] = dk[hh]
                dv_ref[rows, cols[hh]] = dv[hh]
                dstate[hh] = dst[hh]

    slab = pl.BlockSpec((cb, hps * _GDN_DIM), lambda h, i: (nblk - 1 - i, h))
    sds = jax.ShapeDtypeStruct((s, _GDN_HEADS * _GDN_DIM), _F32)
    return pl.pallas_call(
        body, name="gdn_chunk_bwd", grid=(_GDN_HEADS // hps, nblk),
        in_specs=[slab, slab, slab, pl.BlockSpec((cb, 128), lambda h, i: (nblk - 1 - i, 0)),
                  pl.BlockSpec((hps, per, _GDN_DIM, _GDN_DIM), lambda h, i: (h, nblk - 1 - i, 0, 0)), slab],
        out_specs=[slab, slab, slab, pl.BlockSpec((None, cb, 128), lambda h, i: (h, nblk - 1 - i, 0))],
        out_shape=[sds, sds, sds, jax.ShapeDtypeStruct((_GDN_HEADS // hps, s, 128), _F32)],
        scratch_shapes=[pltpu.VMEM((hps, _GDN_DIM, _GDN_DIM), _F32)],
        compiler_params=_params(("arbitrary", "arbitrary")),
    )(q, k, v, gates, states, do)


_SHALO = 8


def _swish_l2(c, normalize, scale):
    y = _silu(c)
    if normalize:
        y = y * lax.rsqrt(jnp.sum(y * y, axis=-1, keepdims=True) + _NORM_EPS) * scale
    return y


def _short_taps(buf, w_ref, rows):
    base = _SHALO - (_GDN_SHORT_CONV - 1)
    acc = jnp.zeros((rows, buf.shape[1]), _F32)
    for t in range(_GDN_SHORT_CONV):
        acc = acc + w_ref[t:t + 1, :] * buf[base + t:base + t + rows, :]
    return acc


_GDN_W = _GDN_HEADS * _GDN_DIM


def _gdn_conv_fwd(proj, conv_w, grp, normalize, scale):
    s = proj.shape[0]
    tm = _CONV_TM
    per = tm // _SHALO

    def body(x_ref, xp_ref, w_ref, o_ref, buf):
        buf[0:_SHALO, :] = jnp.where(pl.program_id(0) == 0, 0.0, xp_ref[...])
        buf[_SHALO:, :] = x_ref[...]
        conv = _short_taps(buf, w_ref, tm)
        for h in range(_GDN_HEADS):
            cols = slice(h * _GDN_DIM, (h + 1) * _GDN_DIM)
            o_ref[:, cols] = _swish_l2(conv[:, cols], normalize, scale)

    return pl.pallas_call(
        body, name=f"gdn_conv_fwd_{grp}", grid=(s // tm,),
        in_specs=[pl.BlockSpec((tm, _GDN_W), lambda i: (i, grp)),
                  pl.BlockSpec((_SHALO, _GDN_W), lambda i: (jnp.maximum(i * per - 1, 0), grp)),
                  pl.BlockSpec((_GDN_SHORT_CONV, _GDN_W), lambda i: (0, grp))],
        out_specs=pl.BlockSpec((tm, _GDN_W), lambda i: (i, 0)),
        out_shape=jax.ShapeDtypeStruct((s, _GDN_W), _F32),
        scratch_shapes=[pltpu.VMEM((tm + _SHALO, _GDN_W), _F32)], compiler_params=_params(("arbitrary",)),
    )(proj, proj, conv_w)


def _gdn_conv_bwd(proj, conv_w, grp, normalize, scale, dy, dproj):
    s = proj.shape[0]
    tm = _CONV_TM
    per = tm // _SHALO
    nt = s // tm
    taps = _GDN_SHORT_CONV

    def body(x_ref, xp_ref, xn_ref, w_ref, dy_ref, dyn_ref, dp_in, dx_ref, dw_ref, buf, dbuf):
        del dp_in
        i = pl.program_id(0)
        last = i == nt - 1
        buf[0:_SHALO, :] = jnp.where(i == 0, 0.0, xp_ref[...])
        buf[_SHALO:_SHALO + tm, :] = x_ref[...]
        buf[_SHALO + tm:, :] = jnp.where(last, 0.0, xn_ref[...])
        conv = _short_taps(buf, w_ref, tm + _SHALO)
        dy_ext = jnp.concatenate([dy_ref[...], jnp.where(last, 0.0, dyn_ref[...])], axis=0)
        for h in range(_GDN_HEADS):
            cols = slice(h * _GDN_DIM, (h + 1) * _GDN_DIM)
            _, pull = jax.vjp(lambda c: _swish_l2(c, normalize, scale), conv[:, cols])
            dbuf[:, cols] = pull(dy_ext[:, cols])[0]
        acc = jnp.zeros((tm, _GDN_W), _F32)
        for t in range(taps):
            o = taps - 1 - t
            acc = acc + w_ref[t:t + 1, :] * dbuf[o:o + tm, :]
        dx_ref[...] = acc.astype(dx_ref.dtype)

        @pl.when(i == 0)
        def _():
            dw_ref[...] = jnp.zeros_like(dw_ref)

        base = _SHALO - (taps - 1)
        for t in range(taps):
            dw_ref[t:t + 1, :] += jnp.sum(dbuf[0:tm, :] * buf[base + t:base + t + tm, :], axis=0, keepdims=True)

    cur = lambda col: pl.BlockSpec((tm, _GDN_W), lambda i: (i, col))
    nxt = lambda col: pl.BlockSpec((_SHALO, _GDN_W), lambda i: (jnp.minimum((i + 1) * per, s // _SHALO - 1), col))
    return pl.pallas_call(
        body, name=f"gdn_conv_bwd_{grp}", grid=(nt,),
        in_specs=[cur(grp), pl.BlockSpec((_SHALO, _GDN_W), lambda i: (jnp.maximum(i * per - 1, 0), grp)), nxt(grp),
                  pl.BlockSpec((taps, _GDN_W), lambda i: (0, grp)), cur(0), nxt(0), _ANY],
        out_specs=[cur(grp), pl.BlockSpec((_SHALO, _GDN_W), lambda i: (0, 0))],
        out_shape=[jax.ShapeDtypeStruct(dproj.shape, dproj.dtype), jax.ShapeDtypeStruct((_SHALO, _GDN_W), _F32)],
        scratch_shapes=[pltpu.VMEM((tm + 2 * _SHALO, _GDN_W), _F32), pltpu.VMEM((tm + _SHALO, _GDN_W), _F32)],
        input_output_aliases={6: 0}, compiler_params=_params(("arbitrary",)),
    )(proj, proj, proj, conv_w, dy, dy, dproj)


def _softplus(z):
    return jnp.maximum(z, 0.0) + jnp.log(1.0 + jnp.exp(-jnp.abs(z)))


def _gates(ba, a_lane, dt_lane):
    lane = lax.broadcasted_iota(jnp.int32, ba.shape, 1)
    g = -jnp.exp(a_lane) * _softplus(ba + dt_lane)
    return jnp.where(lane < _GDN_HEADS, jax.nn.sigmoid(ba), jnp.where(lane < 2 * _GDN_HEADS, g, 0.0))


def _gates_fwd(ba, a_lane, dt_lane):
    return _rowmap("gdn_gates_fwd", _gates, [(ba, 128, 0)], [a_lane, dt_lane], [(128, 128, 0, _F32)],
                   rows=ba.shape[0], tm=_TM)[0]


def _gates_bwd(ba, a_lane, dt_lane, dgates):
    def body(bav, dg8, av, dv):
        _, pull = jax.vjp(_gates, bav, av, dv)
        return pull(jnp.sum(dg8, axis=0))

    return _rowmap("gdn_gates_bwd", body, [(ba, 128, 0), (dgates, 128, 0)], [a_lane, dt_lane], [(128, 128, 0, _BF)],
                   [(1, 128), (1, 128)], rows=ba.shape[0], tm=_TM)


def _out_gate(o, z, g):
    return (_rms(o, g) * _silu(z)).astype(_BF)


def _head_cols():
    return [slice(h * _GDN_DIM, (h + 1) * _GDN_DIM) for h in range(_GDN_HEADS)]


def _out_gate_fwd(o, proj, norm_g):
    def body(ov, zv, gv):
        return jnp.concatenate([_out_gate(ov[:, cl], zv[:, cl], gv) for cl in _head_cols()], axis=1)

    return _rowmap("gdn_out_gate_fwd", body, [(o, _GDN_W, 0), (proj, _GDN_W, 3)], [norm_g], [(_GDN_W, _GDN_W, 0, _BF)],
                   rows=o.shape[0], tm=_CONV_TM)[0]


def _out_gate_bwd(o, proj, norm_g, dy):
    def body(ov, zv, dyv, gv):
        d_o, d_z, d_g = [], [], jnp.zeros_like(gv)
        for cl in _head_cols():
            _, pull = jax.vjp(_out_gate, ov[:, cl], zv[:, cl], gv)
            g_o, g_z, g_g = pull(dyv[:, cl].astype(_BF))
            d_o.append(g_o)
            d_z.append(g_z)
            d_g = d_g + g_g
        return jnp.concatenate(d_o, axis=1), jnp.concatenate(d_z, axis=1), d_g

    return _rowmap("gdn_out_gate_bwd", body, [(o, _GDN_W, 0), (proj, _GDN_W, 3), (dy, _GDN_W, 0)], [norm_g],
                   [(_GDN_W, _GDN_W, 0, _F32), (proj.shape[1], _GDN_W, 3, _BF)], [(1, _GDN_DIM)],
                   rows=o.shape[0], tm=_CONV_TM)


def _plain_in(name, xn, w, layer, tn):
    s, d = xn.shape
    n = w.shape[2]
    return _mm(name, xn, w, grid=(n // tn, s // _TM), a_spec=pl.BlockSpec((_TM, d), lambda j, i: (i, 0)),
               b_spec=pl.BlockSpec((None, d, tn), lambda j, i: (layer, 0, j)), dn=_NN,
               out_shape=jax.ShapeDtypeStruct((s, n), _F32), out_spec=pl.BlockSpec((_TM, tn), lambda j, i: (i, j)))


def _plain_dx(name, dz, w, layer, tn, norm=None):
    s, n = dz.shape
    d = w.shape[1]
    row = pl.BlockSpec((_TM, d), lambda i, j: (i, 0))
    kw = {}
    if norm is not None:
        extra, h, gamma, dh_out = norm
        kw = _norm_bwd_args(h, dh_out, gamma, _TM)
        kw.update(extras=(extra, *kw["extras"]), extra_specs=(row, *kw["extra_specs"]),
                  epi=lambda acc, e, hv, dhv, gv: _norm_bwd_epi(acc + e, hv, dhv, gv))
    return _mm(name, dz, w, grid=(s // _TM, n // tn), a_spec=pl.BlockSpec((_TM, tn), lambda i, j: (i, j)),
               b_spec=pl.BlockSpec((None, d, tn), lambda i, j: (layer, 0, j)), dn=_NT, red=1, acc_shape=(_TM, d),
               out_shape=jax.ShapeDtypeStruct((s, d), _F32), out_spec=row, **kw)


def _plain_dw(name, xn, dz, tn):
    s, d = xn.shape
    n = dz.shape[1]
    return _mm(name, xn, dz, grid=(n // tn, s // _TM), a_spec=pl.BlockSpec((_TM, d), lambda j, i: (i, 0)),
               b_spec=pl.BlockSpec((_TM, tn), lambda j, i: (i, j)), dn=_TN, red=1, acc_shape=(d, tn),
               out_shape=jax.ShapeDtypeStruct((d, n), _BF), out_spec=pl.BlockSpec((d, tn), lambda j, i: (0, j)))


_GDN_QK_SCALE = _GDN_DIM ** -0.5


def _gdn_fwd(h, gamma, w_qkvz, w_ba, w_out, conv_w, a_lane, dt_lane, norm_g, li):
    hn = _rmsnorm_fwd(h, gamma)
    proj = _plain_in("gdn_in", hn, w_qkvz, li, 1024)
    ba = _plain_in("gdn_in_ba", hn, w_ba, li, 128)
    q = _gdn_conv_fwd(proj, conv_w, 0, True, _GDN_QK_SCALE)
    k = _gdn_conv_fwd(proj, conv_w, 1, True, 1.0)
    v = _gdn_conv_fwd(proj, conv_w, 2, False, 1.0)
    gates = _gates_fwd(ba, a_lane, dt_lane)
    o, states = _gdn_chunk_fwd(q, k, v, gates)
    y = _out_gate_fwd(o, proj, norm_g)
    h_new = _proj_residual("gdn_out", y, w_out, li, h, 1.0)
    return h_new, (h, hn, proj, ba, q, k, v, gates, states, o, y)


def _gdn_bwd(dh, saved, gamma, w_qkvz, w_ba, w_out, conv_w, a_lane, dt_lane, norm_g, li, n_ba):
    h, hn, proj, ba, q, k, v, gates, states, o, y = saved
    dy = _proj_out_dy("gdn_dy", dh, w_out, li)
    d_w_out = _wgrad_rows("gdn_dwout", y, dh, 1.0)
    do, dproj, d_norm_g = _out_gate_bwd(o, proj, norm_g, dy)
    dq, dk, dv, dgates = _gdn_chunk_bwd(q, k, v, gates, states, do)
    dproj, dcw_q = _gdn_conv_bwd(proj, conv_w, 0, True, _GDN_QK_SCALE, dq, dproj)
    dproj, dcw_k = _gdn_conv_bwd(proj, conv_w, 1, True, 1.0, dk, dproj)
    dproj, dcw_v = _gdn_conv_bwd(proj, conv_w, 2, False, 1.0, dv, dproj)
    dba, d_a, d_dt = _gates_bwd(ba, a_lane, dt_lane, dgates)
    dhn_ba = _plain_dx("gdn_dhn_ba", dba, w_ba, li, 128)
    dh_in, dgamma = _plain_dx("gdn_dhn", dproj, w_qkvz, li, 1024, norm=(dhn_ba, h, gamma, dh))
    dw_qkvz = _plain_dw("gdn_dwin", hn, dproj, 1024)
    dw_ba = _plain_dw("gdn_dwin_ba", hn, dba, 128)
    d_conv_w = jnp.concatenate([dcw_q, dcw_k, dcw_v], axis=1)[:_GDN_SHORT_CONV]
    d_w_in = jnp.concatenate([dw_qkvz, dw_ba[:, :n_ba]], axis=1)
    d_w_in = jnp.transpose(d_w_in.reshape(d_w_in.shape[0], 4, -1), (1, 0, 2))
    return dh_in, dgamma, (d_conv_w, d_a, d_dt, d_norm_g), d_w_in, d_w_out


_MESH = pl.DeviceIdType.MESH
_ROW_TM = 384


def _place():
    x, y, c = lax.axis_index("x"), lax.axis_index("y"), lax.axis_index("c")
    chips = [(1 - x, y), (x, 1 - y), (1 - x, 1 - y)]
    return x, y, c, chips, [2 * px + py for px, py in chips]


def _remote(src, dst, send_sem, recv_sem, device):
    return pltpu.make_async_remote_copy(src_ref=src, dst_ref=dst, send_sem=send_sem, recv_sem=recv_sem,
                                        device_id=device, device_id_type=_MESH)


def _rows2d(t):
    return t.reshape(-1, t.shape[-1])


def _row_tile(rows):
    for tm in range(min(rows, _ROW_TM) // 8 * 8, 7, -8):
        if rows % tm == 0:
            return tm
    raise ValueError(f"no row tile for {rows} rows")


def _placed_map(name, fn, where, ins, out_rows, cols, out_dtype, out_row, steps, tm, into=None):
    n_in = len(ins)

    def kern(where_ref, *refs):
        del where_ref
        o_ref = refs[n_in + (1 if into is not None else 0)]
        o_ref[...] = fn(*[r[...] for r in refs[:n_in]]).astype(o_ref.dtype)

    in_specs, operands = [], [where]
    for arr, rowfn in ins:
        operands.append(arr)
        if arr.ndim == 3:
            in_specs.append(pl.BlockSpec((arr.shape[0], tm, cols), lambda i, wh, f=rowfn: (0, f(i, wh), 0)))
        else:
            in_specs.append(pl.BlockSpec((tm, cols), lambda i, wh, f=rowfn: (f(i, wh), 0)))
    aliases = {}
    if into is not None:
        operands.append(into)
        in_specs.append(_ANY)
        aliases = {n_in + 1: 0}
    spec = pltpu.PrefetchScalarGridSpec(
        num_scalar_prefetch=1, grid=(steps,), in_specs=in_specs,
        out_specs=pl.BlockSpec((tm, cols), lambda i, wh: (out_row(i, wh), 0)))
    return pl.pallas_call(
        kern, name=name, grid_spec=spec, out_shape=jax.ShapeDtypeStruct((out_rows, cols), out_dtype),
        input_output_aliases=aliases, compiler_params=_params(("arbitrary",)),
    )(*operands)


def _cast_slab(name, t, where):
    n_l, r, cols = t.shape
    tm = _row_tile(r)
    per = r // tm
    buf = _placed_map("cast_" + name, lambda val: val, where, [(_rows2d(t), lambda i, wh: i)], n_l * 4 * r, cols, _BF,
                      lambda i, wh: ((i // per) * 4 + wh[1]) * per + i % per, n_l * per, tm)
    return buf.reshape(n_l, 4, r, cols)


_HBM = pl.BlockSpec(memory_space=pltpu.HBM)
_SEM = pl.BlockSpec(memory_space=pltpu.SEMAPHORE)
_EFFECT = pltpu.SideEffectType.DATAFLOW_SIDE_EFFECTING


def _core_layers(n_layers, c, part):
    half = n_layers // 2
    sub = half // part[1]
    return pl.ds(c * half + part[0] * sub, sub), pl.ds((1 - c) * half + part[0] * sub, sub)


def _ici_copies(refs, send, recv, arrivals, part):
    x, y, c, chips, idx = _place()
    me = 2 * x + y
    res = []
    for t, ref in enumerate(refs):
        mine, _ = _core_layers(ref.shape[0], c, part)
        for k in range(3):
            sem = 3 * t + k
            block = ref.at[mine, idx[k] if arrivals else me]
            res.append(_remote(block, block, send.at[sem], recv.at[sem], (*chips[k], c)))
    return res


def _gather_start(name, bufs, after, part):
    n = len(bufs)

    def body(*refs):
        send, recv = refs[n + 1], refs[n + 2]
        for cp in _ici_copies(refs[:n], send, recv, False, part):
            cp.start()
        refs[2 * n + 3][...] = jnp.zeros((8, 128), _F32)

    res = pl.pallas_call(
        body, name=name,
        out_shape=(pltpu.SemaphoreType.DMA((3 * n,)), pltpu.SemaphoreType.DMA((3 * n,)),
                   *[pltpu.HBM(b.shape, b.dtype) for b in bufs], jax.ShapeDtypeStruct((8, 128), _F32)),
        in_specs=[*[_HBM] * n, _ANY], out_specs=(_SEM, _SEM, *[_HBM] * n, pl.BlockSpec(memory_space=pltpu.VMEM)),
        input_output_aliases={t: 2 + t for t in range(n)},
        compiler_params=pltpu.CompilerParams(has_side_effects=_EFFECT),
    )(*[pltpu.with_memory_space_constraint(b, pltpu.HBM) for b in bufs], after)
    return res[0], res[1], list(res[2:2 + n]), res[2 + n]


def _gather_wait(name, send_sem, recv_sem, bufs, after, part):
    n = len(bufs)

    def body(*refs):
        send, recv = refs[n], refs[n + 1]
        for cp in _ici_copies(refs[:n], send, recv, False, part):
            cp.wait_send()
        for cp in _ici_copies(refs[:n], send, recv, True, part):
            cp.wait_recv()

    res = pl.pallas_call(
        body, name=name, out_shape=tuple(pltpu.HBM(b.shape, b.dtype) for b in bufs),
        in_specs=[*[_HBM] * n, _SEM, _SEM, _ANY], out_specs=tuple([_HBM] * n),
        input_output_aliases={t: t for t in range(n)},
        compiler_params=pltpu.CompilerParams(has_side_effects=_EFFECT),
    )(*bufs, send_sem, recv_sem, after)
    return list(res)


def _gather_forward(name, bufs, part):
    n = len(bufs)

    def body(*refs):
        outs, send, recv = refs[n:2 * n], refs[2 * n], refs[2 * n + 1]
        x, y, c, _, idx = _place()
        copies, arrive = [], []
        for t, ref in enumerate(outs):
            mine, other = _core_layers(ref.shape[0], c, part)
            for k in range(3):
                sem = 3 * t + k
                landed, dst = ref.at[mine, idx[k]], ref.at[other, idx[k]]
                copies.append(_remote(landed, landed, send.at[sem], recv.at[sem], (x, y, 1 - c)))
                arrive.append(_remote(dst, dst, send.at[sem], recv.at[sem], (x, y, 1 - c)))
        for cp in copies:
            cp.start()
        for cp in arrive:
            cp.wait_recv()
        for cp in copies:
            cp.wait_send()

    res = pl.pallas_call(
        body, name=name, in_specs=[_ANY] * n, out_specs=[_ANY] * n,
        out_shape=[jax.ShapeDtypeStruct(b.shape, b.dtype) for b in bufs],
        scratch_shapes=[pltpu.SemaphoreType.DMA((3 * n,)), pltpu.SemaphoreType.DMA((3 * n,))],
        input_output_aliases={t: t for t in range(n)},
    )(*bufs)
    return list(res)


def _rs_sibling(name, gs):
    n = len(gs)

    def body(*refs):
        send, recv = refs[2 * n], refs[2 * n + 1]
        x, y, c, _, _ = _place()
        copies = []
        for t in range(n):
            rh = refs[t].shape[1] // 2
            cp = _remote(refs[t].at[:, pl.ds((1 - c) * rh, rh)], refs[n + t], send.at[t], recv.at[t], (x, y, 1 - c))
            cp.start()
            copies.append(cp)
        for cp in copies:
            cp.wait()

    return pl.pallas_call(
        body, name=name, in_specs=[_ANY] * n, out_specs=[_ANY] * n,
        out_shape=[jax.ShapeDtypeStruct((4, g.shape[1] // 2, g.shape[2]), g.dtype) for g in gs],
        scratch_shapes=[pltpu.SemaphoreType.DMA((n,)), pltpu.SemaphoreType.DMA((n,))],
    )(*gs)


def _chip_copies(sums, lands, send, recv):
    x, y, c, chips, idx = _place()
    return [_remote(sums[t].at[idx[k]], lands[t].at[k], send.at[3 * t + k], recv.at[3 * t + k], (*chips[k], c))
            for t in range(len(sums)) for k in range(3)]


def _rs_chips_start(name, sums, after):
    n = len(sums)
    lands = [lax.empty((3,) + s.shape[1:], s.dtype) for s in sums]

    def body(*refs):
        send, recv = refs[2 * n + 1], refs[2 * n + 2]
        for cp in _chip_copies(refs[:n], refs[n:2 * n], send, recv):
            cp.start()
        refs[4 * n + 3][...] = jnp.zeros((8, 128), _F32)

    both = [*sums, *lands]
    res = pl.pallas_call(
        body, name=name,
        out_shape=(pltpu.SemaphoreType.DMA((3 * n,)), pltpu.SemaphoreType.DMA((3 * n,)),
                   *[pltpu.HBM(b.shape, b.dtype) for b in both], jax.ShapeDtypeStruct((8, 128), _F32)),
        in_specs=[*[_HBM] * (2 * n), _ANY],
        out_specs=(_SEM, _SEM, *[_HBM] * (2 * n), pl.BlockSpec(memory_space=pltpu.VMEM)),
        input_output_aliases={t: 2 + t for t in range(2 * n)},
        compiler_params=pltpu.CompilerParams(has_side_effects=_EFFECT),
    )(*[pltpu.with_memory_space_constraint(b, pltpu.HBM) for b in both], after)
    return res[0], res[1], list(res[2:2 + n]), list(res[2 + n:2 + 2 * n]), res[2 + 2 * n]


def _rs_chips_wait(name, send_sem, recv_sem, sums, lands, after):
    n = len(sums)

    def body(*refs):
        send, recv = refs[2 * n], refs[2 * n + 1]
        copies = _chip_copies(refs[:n], refs[n:2 * n], send, recv)
        for cp in copies:
            cp.wait_send()
        for cp in copies:
            cp.wait_recv()

    both = [*sums, *lands]
    res = pl.pallas_call(
        body, name=name, out_shape=tuple(pltpu.HBM(b.shape, b.dtype) for b in both),
        in_specs=[*[_HBM] * (2 * n), _SEM, _SEM, _ANY], out_specs=tuple([_HBM] * (2 * n)),
        input_output_aliases={t: t for t in range(2 * n)},
        compiler_params=pltpu.CompilerParams(has_side_effects=_EFFECT),
    )(*both, send_sem, recv_sem, after)
    return list(res[:n]), list(res[n:])


def _rs_join(name, totals, layers):
    n = len(totals)

    def body(*refs):
        outs, send, recv = refs[n:2 * n], refs[2 * n], refs[2 * n + 1]
        x, y, c, _, _ = _place()
        copies, arrive = [], []
        for t in range(n):
            rh = outs[t].shape[1] // 2
            mine = outs[t].at[layers[t], pl.ds(c * rh, rh)]
            landing = outs[t].at[layers[t], pl.ds((1 - c) * rh, rh)]
            copies.append(_remote(mine, mine, send.at[t], recv.at[t], (x, y, 1 - c)))
            arrive.append(_remote(landing, landing, send.at[t], recv.at[t], (x, y, 1 - c)))
        for cp in copies:
            cp.start()
        for cp in arrive:
            cp.wait_recv()
        for cp in copies:
            cp.wait_send()

    return pl.pallas_call(
        body, name=name, in_specs=[_ANY] * n, out_specs=[_ANY] * n,
        out_shape=[jax.ShapeDtypeStruct(b.shape, b.dtype) for b in totals],
        scratch_shapes=[pltpu.SemaphoreType.DMA((n,)), pltpu.SemaphoreType.DMA((n,))],
        input_output_aliases={t: t for t in range(n)},
    )(*totals)


def _pair_sum(g, got, where):
    _, r, cols = g.shape
    rh = r // 2
    tm = _row_tile(rh)
    per, perh = r // tm, rh // tm
    out = _placed_map("rs_add_pair", lambda u, w: u.astype(_F32) + w.astype(_F32), where,
                      [(_rows2d(g), lambda i, wh: (i // perh) * per + wh[0] * perh + i % perh),
                       (_rows2d(got), lambda i, wh: i)],
                      4 * rh, cols, _BF, lambda i, wh: i, 4 * perh, tm)
    return out.reshape(4, rh, cols)


def _sum_of_four(pair, land, where, total, layer, n_layers):
    _, rh, cols = pair.shape
    r = 2 * rh
    tm = _row_tile(rh)
    per, perh = r // tm, rh // tm

    def add_four(own, others):
        return ((own.astype(_F32) + others[0].astype(_F32)) + others[1].astype(_F32)) + others[2].astype(_F32)

    out = _placed_map("rs_add_four", add_four, where,
                      [(_rows2d(pair), lambda i, wh: wh[1] * perh + i), (land, lambda i, wh: i)],
                      n_layers * r, cols, _F32, lambda i, wh: layer * per + wh[0] * perh + i, perh, tm,
                      into=None if total is None else _rows2d(total))
    return out.reshape(n_layers, r, cols)


def _all_reduce_small(pack):
    rows, cols = pack.shape

    def body(p_ref, o_ref, land, send, recv):
        x, y, c = lax.axis_index("x"), lax.axis_index("y"), lax.axis_index("c")
        me = 4 * x + 2 * y + c
        land[me] = p_ref[...]
        copies = []
        for k in range(1, 8):
            bx, by, bc = (k >> 2) & 1, (k >> 1) & 1, k & 1
            peer = (1 - x if bx else x, 1 - y if by else y, 1 - c if bc else c)
            cp = _remote(p_ref, land.at[me], send.at[k - 1], recv.at[k - 1], peer)
            cp.start()
            copies.append((cp, 4 * peer[0] + 2 * peer[1] + peer[2]))
        for k, (cp, pid) in enumerate(copies):
            _remote(p_ref, land.at[pid], send.at[k], recv.at[k], (x, y, c)).wait_recv()
        for cp, _ in copies:
            cp.wait_send()
        acc = land[0]
        for d in range(1, 8):
            acc = acc + land[d]
        o_ref[...] = acc

    vm = pl.BlockSpec(memory_space=pltpu.VMEM)
    return pl.pallas_call(
        body, name="all_reduce_small", in_specs=[vm], out_specs=vm,
        out_shape=jax.ShapeDtypeStruct((rows, cols), _F32),
        scratch_shapes=[pltpu.VMEM((8, rows, cols), _F32), pltpu.SemaphoreType.DMA((7,)), pltpu.SemaphoreType.DMA((7,))],
    )(pack)


def _adamw_math(w, g, m, v):
    m2 = _ADAM_B1 * m + (1.0 - _ADAM_B1) * g
    v2 = _ADAM_B2 * v + (1.0 - _ADAM_B2) * (g * g)
    m_hat = m2 / (1.0 - _ADAM_B1 ** _ADAM_STEP)
    v_hat = v2 / (1.0 - _ADAM_B2 ** _ADAM_STEP)
    delta = -_ADAM_LR * (m_hat / (jnp.sqrt(v_hat) + _ADAM_EPS) + _ADAM_WD * w)
    return delta, m2, v2


def _adamw(w, g, m, v, lo=0, hi=None, into=None):
    shape = w.shape
    n_l = shape[0] if w.ndim == 3 else 1
    hi = n_l if hi is None else hi
    w2, g2, m2, v2 = (_rows2d(t) for t in (w, g, m, v))
    rows, cols = w2.shape
    per_layer = rows // n_l
    tm = _row_tile(per_layer)
    first, steps = lo * per_layer // tm, (hi - lo) * per_layer // tm
    n_into = 0 if into is None else 3

    def body(*refs):
        outs = refs[4 + n_into:]
        for ref, val in zip(outs, _adamw_math(*[r[...] for r in refs[:4]])):
            ref[...] = val

    spec = pl.BlockSpec((tm, cols), lambda i: (first + i, 0))
    sds = jax.ShapeDtypeStruct((rows, cols), _F32)
    outs = pl.pallas_call(
        body, name="adamw", grid=(steps,), in_specs=[spec] * 4 + [_ANY] * n_into, out_specs=[spec] * 3,
        out_shape=[sds] * 3, input_output_aliases={4 + k: k for k in range(n_into)},
        compiler_params=_params(("arbitrary",)),
    )(w2, g2, m2, v2, *([] if into is None else [_rows2d(t) for t in into]))
    return tuple(o.reshape(shape) for o in outs)


def kernel(x, positions, ffn1_norm, ffn1_w_in, ffn1_w_out, mix_norm, ffn2_norm, ffn2_w_in, ffn2_w_out, hyb_w_in, hyb_dw_w, hyb_dw_b, hyb_ln_g, hyb_ln_b, hyb_w_out, gdn_w_in, gdn_conv_w, gdn_A_log, gdn_dt_bias, gdn_norm_g, gdn_w_out, final_norm, loss_target, m_ffn1_norm, m_ffn1_w_in, m_ffn1_w_out, m_mix_norm, m_ffn2_norm, m_ffn2_w_in, m_ffn2_w_out, m_hyb_w_in, m_hyb_dw_w, m_hyb_dw_b, m_hyb_ln_g, m_hyb_ln_b, m_hyb_w_out, m_gdn_w_in, m_gdn_conv_w, m_gdn_A_log, m_gdn_dt_bias, m_gdn_norm_g, m_gdn_w_out, m_final_norm, v_ffn1_norm, v_ffn1_w_in, v_ffn1_w_out, v_mix_norm, v_ffn2_norm, v_ffn2_w_in, v_ffn2_w_out, v_hyb_w_in, v_hyb_dw_w, v_hyb_dw_b, v_hyb_ln_g, v_hyb_ln_b, v_hyb_w_out, v_gdn_w_in, v_gdn_conv_w, v_gdn_A_log, v_gdn_dt_bias, v_gdn_norm_g, v_gdn_w_out, v_final_norm):
    w = dict(ffn1_norm=ffn1_norm, ffn1_w_in=ffn1_w_in, ffn1_w_out=ffn1_w_out, mix_norm=mix_norm, ffn2_norm=ffn2_norm, ffn2_w_in=ffn2_w_in, ffn2_w_out=ffn2_w_out, hyb_w_in=hyb_w_in, hyb_dw_w=hyb_dw_w, hyb_dw_b=hyb_dw_b, hyb_ln_g=hyb_ln_g, hyb_ln_b=hyb_ln_b, hyb_w_out=hyb_w_out, gdn_w_in=gdn_w_in, gdn_conv_w=gdn_conv_w, gdn_A_log=gdn_A_log, gdn_dt_bias=gdn_dt_bias, gdn_norm_g=gdn_norm_g, gdn_w_out=gdn_w_out, final_norm=final_norm)
    m = dict(ffn1_norm=m_ffn1_norm, ffn1_w_in=m_ffn1_w_in, ffn1_w_out=m_ffn1_w_out, mix_norm=m_mix_norm, ffn2_norm=m_ffn2_norm, ffn2_w_in=m_ffn2_w_in, ffn2_w_out=m_ffn2_w_out, hyb_w_in=m_hyb_w_in, hyb_dw_w=m_hyb_dw_w, hyb_dw_b=m_hyb_dw_b, hyb_ln_g=m_hyb_ln_g, hyb_ln_b=m_hyb_ln_b, hyb_w_out=m_hyb_w_out, gdn_w_in=m_gdn_w_in, gdn_conv_w=m_gdn_conv_w, gdn_A_log=m_gdn_A_log, gdn_dt_bias=m_gdn_dt_bias, gdn_norm_g=m_gdn_norm_g, gdn_w_out=m_gdn_w_out, final_norm=m_final_norm)
    v = dict(ffn1_norm=v_ffn1_norm, ffn1_w_in=v_ffn1_w_in, ffn1_w_out=v_ffn1_w_out, mix_norm=v_mix_norm, ffn2_norm=v_ffn2_norm, ffn2_w_in=v_ffn2_w_in, ffn2_w_out=v_ffn2_w_out, hyb_w_in=v_hyb_w_in, hyb_dw_w=v_hyb_dw_w, hyb_dw_b=v_hyb_dw_b, hyb_ln_g=v_hyb_ln_g, hyb_ln_b=v_hyb_ln_b, hyb_w_out=v_hyb_w_out, gdn_w_in=v_gdn_w_in, gdn_conv_w=v_gdn_conv_w, gdn_A_log=v_gdn_A_log, gdn_dt_bias=v_gdn_dt_bias, gdn_norm_g=v_gdn_norm_g, gdn_w_out=v_gdn_w_out, final_norm=v_final_norm)
    return _train_step(x, positions, loss_target, w, m, v)


_WEIGHT_ORDER = ("ffn1_norm", "ffn1_w_in", "ffn1_w_out", "mix_norm", "ffn2_norm", "ffn2_w_in", "ffn2_w_out",
                 "hyb_w_in", "hyb_dw_w", "hyb_dw_b", "hyb_ln_g", "hyb_ln_b", "hyb_w_out", "gdn_w_in", "gdn_conv_w",
                 "gdn_A_log", "gdn_dt_bias", "gdn_norm_g", "gdn_w_out", "final_norm")
_BIG = ("ffn1_w_in", "ffn1_w_out", "ffn2_w_in", "ffn2_w_out", "hyb_w_in", "hyb_w_out", "gdn_w_in", "gdn_w_out")
_PACK_COLS = 1024


def _lane16(vec):
    return jnp.zeros((1, 128), _F32).at[0, _GDN_HEADS:2 * _GDN_HEADS].set(vec)


def _pack_rows(parts):
    rows, where, at = [], [], 0
    for p in parts:
        flat = p.reshape(-1).astype(_F32)
        n = -(-flat.shape[0] // (8 * _PACK_COLS)) * 8
        rows.append(jnp.pad(flat, (0, n * _PACK_COLS - flat.shape[0])).reshape(n, _PACK_COLS))
        where.append((at, n, flat.shape[0], p.shape))
        at += n
    return jnp.concatenate(rows, axis=0), where


def _unpack_rows(pack, where):
    return [pack[at:at + n].reshape(-1)[:size].reshape(shape) for at, n, size, shape in where]


def _train_step(x, positions, loss_target, w, m, v):
    n_layers = w["ffn1_norm"].shape[0]
    n_hyb, n_gdn = w["hyb_w_in"].shape[0], w["gdn_w_in"].shape[0]
    d_model = x.shape[-1]
    xh, target = x[0], loss_target[0]
    chip = 2 * lax.axis_index("x") + lax.axis_index("y")
    where = jnp.stack([lax.axis_index("c"), chip]).astype(jnp.int32)

    groups = (("ffn1_w_in", "ffn1_w_out"), ("hyb_w_in", "hyb_w_out"), ("ffn2_w_in", "ffn2_w_out"),
              ("gdn_w_in", "gdn_w_out"))
    steps = [(0, (0, 2)), (1, (0, 1)), (0, (1, 2)), (2, (0, 1)), (3, (0, 1))]
    bufs = [[_cast_slab(n, w[n], where) for n in names] for names in groups]
    pending, token = [], where
    for si, (gi, part) in enumerate(steps):
        send, recv, bufs[gi], token = _gather_start(f"gather_start_{si}", bufs[gi], token, part)
        pending.append((send, recv))
    gdn_cols = w["gdn_w_in"].shape[-1]
    n_qkvz = 4 * _GDN_HEADS * _GDN_DIM
    n_ba = 4 * gdn_cols - n_qkvz
    wv = {}

    def arrive(si, after):
        gi, part = steps[si]
        send, recv = pending[si]
        bufs[gi] = _gather_wait(f"gather_wait_{si}", send, recv, bufs[gi], after, part)
        bufs[gi] = _gather_forward(f"gather_pass_{si}", bufs[gi], part)
        a, b = bufs[gi]
        kind = groups[gi][0]
        if kind == "gdn_w_in":
            gdn_in = jnp.transpose(a, (0, 2, 1, 3)).reshape(n_gdn, d_model, 4 * gdn_cols)
            wv["gdn_qkvz"] = gdn_in[:, :, :n_qkvz]
            wv["gdn_ba"] = jnp.pad(gdn_in[:, :, n_qkvz:], ((0, 0), (0, 0), (0, 128 - n_ba)))
        else:
            wv[kind] = a
        wv[groups[gi][1]] = b.reshape(b.shape[0], -1, d_model)

    def own_columns(shard):
        cols = shard.shape[-1]
        zeros = jnp.zeros(shard.shape[:-1] + (4 * cols,), _F32)
        return lax.dynamic_update_slice_in_dim(zeros, shard, chip * cols, axis=2)

    small_pack, small_where = _pack_rows([own_columns(w["hyb_dw_w"]), own_columns(w["gdn_conv_w"])])
    dw_w, conv_w = _unpack_rows(0.5 * _all_reduce_small(small_pack), small_where)
    dw_w = jnp.pad(dw_w, ((0, 0), (0, _HALO - _CONV_WIDTH), (0, 0)))
    tabs = _rotary_tables(positions)

    def mixer_args(layer):
        i = layer // 2
        if layer % 2 == 0:
            return (wv["hyb_w_in"], wv["hyb_w_out"], dw_w[i], w["hyb_dw_b"][i][None], w["hyb_ln_g"][i][None],
                    w["hyb_ln_b"][i][None], tabs, i)
        return (wv["gdn_qkvz"], wv["gdn_ba"], wv["gdn_w_out"], conv_w[i], _lane16(w["gdn_A_log"][i]),
                _lane16(w["gdn_dt_bias"][i]), w["gdn_norm_g"][i][None], i)

    h = xh
    saved = []
    arrive(0, token)
    for layer in range(n_layers):
        if layer == 1:
            arrive(2, h)
        h, s1 = _ffn_fwd(h, w["ffn1_norm"][layer][None], wv["ffn1_w_in"], wv["ffn1_w_out"], layer)
        gamma = w["mix_norm"][layer][None]
        if layer % 2 == 0:
            if layer == 0:
                arrive(1, h)
            h, sm = _hyb_fwd(h, gamma, *mixer_args(layer))
        else:
            if layer == 1:
                arrive(4, h)
            h, sm = _gdn_fwd(h, gamma, *mixer_args(layer))
        if layer == 0:
            arrive(3, h)
        h, s2 = _ffn_fwd(h, w["ffn2_norm"][layer][None], wv["ffn2_w_in"], wv["ffn2_w_out"], layer)
        saved.append((s1, sm, s2))
    dh, loss_row, d_final = _loss_head(h, w["final_norm"][None], target)

    d_norm = {k: [None] * n_layers for k in ("ffn1_norm", "mix_norm", "ffn2_norm")}
    d_hyb = [None] * n_hyb
    d_gdn = [None] * n_gdn
    grads = {name: None for name in _BIG}
    n_of = {name: w[name].shape[0] for name in _BIG}

    def rs_begin(layer, full, token):
        names = list(full)
        got = _rs_sibling(f"rs_sib_{layer}", [full[n] for n in names])
        sums = [_pair_sum(full[n], g, where) for n, g in zip(names, got)]
        send, recv, sums, lands, token = _rs_chips_start(f"rs_chips_start_{layer}", sums, token)
        return (layer, names, send, recv, sums, lands), token

    def rs_end(state, after):
        layer, names, send, recv, sums, lands = state
        sums, lands = _rs_chips_wait(f"rs_chips_wait_{layer}", send, recv, sums, lands, after)
        index = [layer if n.startswith("ffn") else layer // 2 for n in names]
        for n, li, pair, land in zip(names, index, sums, lands):
            grads[n] = _sum_of_four(pair, land, where, grads[n], li, n_of[n])
        joined = _rs_join(f"rs_join_{layer}", [grads[n] for n in names], index)
        grads.update(zip(names, joined))

    in_flight = None
    for layer in reversed(range(n_layers)):
        s1, sm, s2 = saved[layer]
        full = {}
        dh, d_norm["ffn2_norm"][layer], full["ffn2_w_in"], full["ffn2_w_out"] = _ffn_bwd(
            dh, s2, w["ffn2_norm"][layer][None], wv["ffn2_w_in"], wv["ffn2_w_out"], layer)
        gamma = w["mix_norm"][layer][None]
        if layer % 2 == 0:
            dh, d_norm["mix_norm"][layer], d_hyb[layer // 2], full["hyb_w_in"], full["hyb_w_out"] = _hyb_bwd(
                dh, sm, gamma, *mixer_args(layer))
        else:
            dh, d_norm["mix_norm"][layer], d_gdn[layer // 2], full["gdn_w_in"], full["gdn_w_out"] = _gdn_bwd(
                dh, sm, gamma, *mixer_args(layer), n_ba)
        dh, d_norm["ffn1_norm"][layer], full["ffn1_w_in"], full["ffn1_w_out"] = _ffn_bwd(
            dh, s1, w["ffn1_norm"][layer][None], wv["ffn1_w_in"], wv["ffn1_w_out"], layer)
        if in_flight is not None:
            rs_end(in_flight, dh)
        in_flight, token = rs_begin(layer, full, token)
        dh, token = lax.optimization_barrier((dh, token))
    grad_x = dh[None]

    stack = lambda rows: jnp.concatenate(rows, axis=0)
    misc = jnp.concatenate([stack([g[1] for g in d_gdn])[:, _GDN_HEADS:2 * _GDN_HEADS].reshape(-1),
                            stack([g[2] for g in d_gdn])[:, _GDN_HEADS:2 * _GDN_HEADS].reshape(-1),
                            stack([g[3] for g in d_gdn]).reshape(-1), loss_row[0, :1]])
    n_a = n_gdn * _GDN_HEADS
    parts = [stack(d_norm["ffn1_norm"]), stack(d_norm["mix_norm"]), stack(d_norm["ffn2_norm"]), d_final,
             stack([g[1] for g in d_hyb]), stack([g[2] for g in d_hyb]), stack([g[3] for g in d_hyb]), misc,
             jnp.stack([g[0][:_CONV_WIDTH] for g in d_hyb]), jnp.stack([g[0] for g in d_gdn])]
    pack, pack_where = _pack_rows(parts)
    red = _unpack_rows(_all_reduce_small(pack), pack_where)
    misc = red[7]
    cw = w["hyb_dw_w"].shape[-1]
    gw = w["gdn_conv_w"].shape[-1]
    grads.update(
        ffn1_norm=red[0], mix_norm=red[1], ffn2_norm=red[2], final_norm=red[3].reshape(-1),
        hyb_dw_b=red[4], hyb_ln_g=red[5], hyb_ln_b=red[6],
        gdn_A_log=misc[:n_a].reshape(n_gdn, _GDN_HEADS), gdn_dt_bias=misc[n_a:2 * n_a].reshape(n_gdn, _GDN_HEADS),
        gdn_norm_g=misc[2 * n_a:2 * n_a + n_gdn * _GDN_DIM].reshape(n_gdn, _GDN_DIM),
        hyb_dw_w=lax.dynamic_slice_in_dim(red[8], chip * cw, cw, axis=2),
        gdn_conv_w=lax.dynamic_slice_in_dim(red[9], chip * gw, gw, axis=2))
    loss = misc[2 * n_a + n_gdn * _GDN_DIM]

    small = [n for n in _WEIGHT_ORDER if n not in _BIG]
    packs = [_pack_rows([src[n] for n in small]) for src in (w, grads, m, v)]
    outs = _adamw(*[p[0] for p in packs])
    delta, new_m, new_v = {}, {}, {}
    for res, o in zip((delta, new_m, new_v), outs):
        res.update(dict(zip(small, _unpack_rows(o, packs[0][1]))))
    late = in_flight[1]
    early = {name: _adamw(w[name], grads[name], m[name], v[name], lo=1 if name in late else 0) for name in _BIG}
    early, outs, token = lax.optimization_barrier((early, outs, token))
    rs_end(in_flight, token)
    for name in _BIG:
        res = early[name]
        if name in late:
            res = _adamw(w[name], grads[name], m[name], v[name], lo=0, hi=1, into=res)
        delta[name], new_m[name], new_v[name] = res
    return (loss, grad_x, *[grads[n] for n in _WEIGHT_ORDER], *[delta[n] for n in _WEIGHT_ORDER],
            *[new_m[n] for n in _WEIGHT_ORDER], *[new_v[n] for n in _WEIGHT_ORDER])
```

```python
import functools

import jax
import jax.numpy as jnp
from jax import lax
from jax.experimental import pallas as pl
from jax.experimental.pallas import tpu as pltpu

_BF = jnp.bfloat16
_F32 = jnp.float32
_NORM_EPS = 1e-6
_VMEM_LIMIT_BYTES = 56 * 1024 * 1024

_A_HEADS = 8
_A_HEAD_DIM = 64
_WIN_BLOCK = 128
_DILATIONS = (1, 4, 16)
_CONV_WIDTH = 31
_GDN_HEADS = 8
_GDN_DIM = 128
_GDN_CHUNK = 64
_GDN_SHORT_CONV = 4

_ADAM_LR = 0.001
_ADAM_B1 = 0.9
_ADAM_B2 = 0.999
_ADAM_EPS = 1e-08
_ADAM_WD = 0.01
_ADAM_STEP = 10


def _params(sem):
    return pltpu.CompilerParams(dimension_semantics=sem, vmem_limit_bytes=_VMEM_LIMIT_BYTES)


def _dot(a, b, dn=(((1,), (0,)), ((), ()))):
    return lax.dot_general(a.astype(_BF), b.astype(_BF), dn, preferred_element_type=_F32)


def _dot_nt(a, b):
    return _dot(a, b, (((1,), (1,)), ((), ())))


def _dot_tn(a, b):
    return _dot(a, b, (((0,), (0,)), ((), ())))


_NN = (((1,), (0,)), ((), ()))
_NT = (((1,), (1,)), ((), ()))
_TN = (((0,), (0,)), ((), ()))
_ANY = pl.BlockSpec(memory_space=pl.ANY)


def _mm(name, a, b, *, grid, a_spec, b_spec, dn, out_shape, out_spec, red=None, acc_shape=None,
        pre_a=None, pre_b=None, epi=None, extras=(), extra_specs=(), sum_shape=None):
    n_ex = len(extras)
    n_in = 2 + n_ex
    nk = grid[red] if red is not None else 1
    n_out = 1 if sum_shape is None else 2

    def body(*refs):
        a_ref, b_ref = refs[0], refs[1]
        ex = refs[2:2 + n_ex]
        o_ref = refs[n_in]
        av, bv = a_ref[...], b_ref[...]
        if pre_a is not None:
            av = pre_a(av)
        if pre_b is not None:
            bv = pre_b(bv)
        part = lax.dot_general(av.astype(_BF), bv.astype(_BF), dn, preferred_element_type=_F32)

        def finish(acc):
            res = epi(acc, *[e[...] for e in ex]) if epi is not None else acc
            if sum_shape is not None:
                res, term = res
                s_ref = refs[n_in + 1]
                first = functools.reduce(jnp.logical_and, [pl.program_id(ax) == 0 for ax in range(len(grid)) if ax != red])

                @pl.when(first)
                def _():
                    s_ref[...] = jnp.zeros_like(s_ref)

                s_ref[...] += term
            if isinstance(res, tuple):
                for idx, val in enumerate(res):
                    o_ref[idx] = val.astype(o_ref.dtype)
            else:
                o_ref[...] = res.astype(o_ref.dtype)

        if red is None:
            finish(part)
        else:
            acc_ref = refs[n_in + n_out]
            k = pl.program_id(red)

            @pl.when(k == 0)
            def _():
                acc_ref[...] = part

            @pl.when(k > 0)
            def _():
                acc_ref[...] += part

            @pl.when(k == nk - 1)
            def _():
                finish(acc_ref[...])

    if sum_shape is not None:
        out_shape = [out_shape, jax.ShapeDtypeStruct(sum_shape, _F32)]
        out_spec = [out_spec, pl.BlockSpec(sum_shape, lambda *g: (0,) * len(sum_shape))]
    return pl.pallas_call(
        body, name=name, grid=grid, in_specs=[a_spec, b_spec, *extra_specs], out_specs=out_spec, out_shape=out_shape,
        scratch_shapes=[pltpu.VMEM(acc_shape, _F32)] if red is not None else [],
        compiler_params=_params(("arbitrary",) * len(grid)),
    )(a, b, *extras)


def _rowmap(name, body, tiled, params, tile_outs, param_outs=(), *, rows, tm, ncol=1, col_outer=False,
            into=None):
    nt, npar, nto, npo = len(tiled), len(params), len(tile_outs), len(param_outs)
    assert rows % tm == 0, (name, rows, tm)
    nrow = rows // tm
    grid = (ncol, nrow) if col_outer else (nrow, ncol)

    def ij(g):
        return (g[1], g[0]) if col_outer else (g[0], g[1])

    def cidx(col, j):
        return col(j) if callable(col) else col

    in_specs, operands = [], []
    for arr, width, col in tiled:
        operands.append(arr)
        if arr.ndim == 3:
            in_specs.append(pl.BlockSpec((arr.shape[0], tm, width),
                                         lambda *g, col=col: (0, ij(g)[0], cidx(col, ij(g)[1]))))
        else:
            in_specs.append(pl.BlockSpec((tm, width), lambda *g, col=col: (ij(g)[0], cidx(col, ij(g)[1]))))
    for p in params:
        if isinstance(p, tuple):
            arr, width, col = p
            operands.append(arr)
            in_specs.append(pl.BlockSpec((arr.shape[0], width), lambda *g, col=col: (0, cidx(col, ij(g)[1]))))
        else:
            operands.append(p)
            in_specs.append(pl.BlockSpec(p.shape, lambda *g, nd=p.ndim: (0,) * nd))
    out_shape, out_specs = [], []
    for total, width, col, dt in tile_outs:
        out_shape.append(jax.ShapeDtypeStruct((rows, total), dt))
        out_specs.append(pl.BlockSpec((tm, width), lambda *g, col=col: (ij(g)[0], cidx(col, ij(g)[1]))))
    for shp in param_outs:
        out_shape.append(jax.ShapeDtypeStruct(shp, _F32))
        out_specs.append(pl.BlockSpec(shp, lambda *g, nd=len(shp): (0,) * nd))
    n_in = nt + npar
    aliases = {}
    if into is not None:
        operands.append(into[0])
        in_specs.append(_ANY)
        aliases = {n_in: into[1]}
        n_in += 1

    def kern(*refs):
        vals = [r[...] for r in refs[:nt + npar]]
        touts = refs[n_in:n_in + nto]
        pouts = refs[n_in + nto:]
        res = body(*vals)
        if not isinstance(res, (tuple, list)):
            res = (res,)
        for r, val in zip(touts, res[:nto]):
            r[...] = val.astype(r.dtype)
        if npo:
            @pl.when((pl.program_id(0) == 0) & (pl.program_id(1) == 0))
            def _():
                for r in pouts:
                    r[...] = jnp.zeros_like(r)

            for r, val in zip(pouts, res[nto:]):
                r[...] += val

    return pl.pallas_call(
        kern, name=name, grid=grid, in_specs=in_specs, out_specs=out_specs, out_shape=out_shape,
        input_output_aliases=aliases, compiler_params=_params(("arbitrary", "arbitrary")),
    )(*operands)


def _vjp_body(fn, n_in, diff, n_ct):
    def body(*vals):
        ins, cts = vals[:n_in], vals[n_in:n_in + n_ct]
        outs, pull = jax.vjp(fn, *ins)
        single = not isinstance(outs, (tuple, list))
        if single:
            grads = pull(cts[0].astype(outs.dtype))
        else:
            grads = pull(tuple(c.astype(o.dtype) for c, o in zip(cts, outs)))
        return tuple(grads[i] for i in diff)
    return body


_TM = 512
_TM_WIDE = 1024


def _rms(h, gamma):
    hf = h.astype(_F32)
    return hf * lax.rsqrt(jnp.mean(hf * hf, axis=-1, keepdims=True) + _NORM_EPS) * gamma


def _rmsnorm_fwd(h, gamma):
    s, d = h.shape
    return _rowmap("rmsnorm_fwd", lambda hv, gv: _rms(hv, gv).astype(_BF), [(h, d, 0)], [gamma],
                   [(d, d, 0, _BF)], rows=s, tm=_TM)[0]


def _norm_bwd_epi(dxn, hv, dhv, gv):
    _, pull = jax.vjp(_rms, hv, gv)
    dh, dg = pull(dxn)
    return dhv + dh, dg


def _norm_bwd_args(h, dh_out, gamma, tm):
    d = h.shape[1]
    row = pl.BlockSpec((tm, d), lambda i, j: (i, 0))
    return dict(extras=(h, dh_out, gamma), extra_specs=(row, row, pl.BlockSpec((1, d), lambda i, j: (0, 0))),
                sum_shape=(1, d))


def _silu(x):
    return x * jax.nn.sigmoid(x)


def _ffn_up(h, gamma, w_in, layer):
    s, d = h.shape
    ns = w_in.shape[-1]
    tm = _CONV_TM

    def body(h_ref, gam_ref, w_ref, x_ref, g_ref, u_ref, a_ref):
        xv = _rms(h_ref[...], gam_ref[...]).astype(_BF)
        x_ref[...] = xv
        for half in range(2):
            cols = slice(half * ns, (half + 1) * ns)
            g = jnp.dot(xv, w_ref[half], preferred_element_type=_F32)
            u = jnp.dot(xv, w_ref[half + 2], preferred_element_type=_F32)
            g_ref[:, cols] = g.astype(_BF)
            u_ref[:, cols] = u.astype(_BF)
            a_ref[:, cols] = (_silu(g) * u).astype(_BF)

    row = pl.BlockSpec((tm, d), lambda i: (i, 0))
    out = pl.BlockSpec((tm, 2 * ns), lambda i: (i, 0))
    sds = jax.ShapeDtypeStruct((s, 2 * ns), _BF)
    return pl.pallas_call(
        body, name="ffn_up", grid=(s // tm,),
        in_specs=[row, pl.BlockSpec((1, d), lambda i: (0, 0)), pl.BlockSpec((None, 4, d, ns), lambda i: (layer, 0, 0, 0))],
        out_specs=[row, out, out, out], out_shape=[jax.ShapeDtypeStruct((s, d), _BF), sds, sds, sds],
        compiler_params=_params(("arbitrary",)),
    )(h, gamma, w_in)


def _proj_residual(name, y, w, layer, h, scale):
    s, k = y.shape
    d = w.shape[-1]
    return _mm(name, y, w, grid=(s // _TM,), a_spec=pl.BlockSpec((_TM, k), lambda i: (i, 0)),
               b_spec=pl.BlockSpec((None, k, d), lambda i: (layer, 0, 0)), dn=_NN,
               out_shape=jax.ShapeDtypeStruct((s, d), _F32), out_spec=pl.BlockSpec((_TM, d), lambda i: (i, 0)),
               epi=lambda acc, hv: hv + scale * acc, extras=(h,),
               extra_specs=(pl.BlockSpec((_TM, d), lambda i: (i, 0)),))


def _ffn_dact(dh, w_out, layer, g, u):
    s, d = dh.shape
    f = g.shape[1]
    ns = f // 2

    def epi(da, gv, uv):
        gf, uf = gv.astype(_F32), uv.astype(_F32)
        sg = jax.nn.sigmoid(gf)
        return da * uf * (sg * (1.0 + gf * (1.0 - sg))), da * gf * sg

    tm = _CONV_TM
    gu = pl.BlockSpec((tm, f), lambda i: (i, 0))
    return _mm("ffn_dact", dh, w_out, grid=(s // tm,), a_spec=pl.BlockSpec((tm, d), lambda i: (i, 0)),
               b_spec=pl.BlockSpec((None, f, d), lambda i: (layer, 0, 0)), dn=_NT,
               out_shape=jax.ShapeDtypeStruct((2, s, f), _BF), out_spec=pl.BlockSpec((2, tm, f), lambda i: (0, i, 0)),
               pre_a=lambda v: 0.5 * v, epi=epi, extras=(g, u), extra_specs=(gu, gu))


def _wgrad_rows(name, y, dh, scale):
    s, k = y.shape
    d = dh.shape[1]
    out = _mm(name, y, dh, grid=(s // _TM,), a_spec=pl.BlockSpec((_TM, k), lambda i: (i, 0)),
              b_spec=pl.BlockSpec((_TM, d), lambda i: (i, 0)), dn=_TN, red=0, acc_shape=(k, d),
              out_shape=jax.ShapeDtypeStruct((k, d), _BF), out_spec=pl.BlockSpec((k, d), lambda i: (0, 0)),
              pre_b=(lambda v: scale * v) if scale != 1.0 else None)
    return out.reshape(4, k // 4, d)


def _ffn_dxn(dz, w_in, layer, h, gamma, dh_out):
    _, s, f = dz.shape
    d, ns = w_in.shape[2], w_in.shape[3]
    tm = _TM_WIDE
    return _mm("ffn_dxn", dz, w_in, grid=(s // tm, 4),
               a_spec=pl.BlockSpec((None, tm, ns), lambda i, j: (j // 2, i, j % 2)),
               b_spec=pl.BlockSpec((None, None, d, ns), lambda i, j: (layer, j, 0, 0)), dn=_NT, red=1,
               acc_shape=(tm, d), out_shape=jax.ShapeDtypeStruct((s, d), _F32),
               out_spec=pl.BlockSpec((tm, d), lambda i, j: (i, 0)), epi=_norm_bwd_epi,
               **_norm_bwd_args(h, dh_out, gamma, tm))


def _ffn_dwin(xn, dz):
    s, d = xn.shape
    ns = dz.shape[2] // 2
    tm = _TM_WIDE
    return _mm("ffn_dwin", xn, dz, grid=(4, s // tm), a_spec=pl.BlockSpec((tm, d), lambda j, i: (i, 0)),
               b_spec=pl.BlockSpec((None, tm, ns), lambda j, i: (j // 2, i, j % 2)), dn=_TN, red=1,
               acc_shape=(d, ns), out_shape=jax.ShapeDtypeStruct((4, d, ns), _BF),
               out_spec=pl.BlockSpec((None, d, ns), lambda j, i: (j, 0, 0)))


def _ffn_fwd(h, gamma, w_in, w_out, layer):
    xn, g, u, act = _ffn_up(h, gamma, w_in, layer)
    h_new = _proj_residual("ffn_down", act, w_out, layer, h, 0.5)
    return h_new, (h, xn, g, u, act)


def _ffn_bwd(dh, saved, gamma, w_in, w_out, layer):
    h, xn, g, u, act = saved
    dz = _ffn_dact(dh, w_out, layer, g, u)
    d_w_out = _wgrad_rows("ffn_dwout", act, dh, 0.5)
    dh_in, dgamma = _ffn_dxn(dz, w_in, layer, h, gamma, dh)
    d_w_in = _ffn_dwin(xn, dz)
    return dh_in, dgamma, d_w_in, d_w_out


def _loss_head(h, gamma, target):
    s, d = h.shape

    def body(hv, tv, gv):
        y, pull = jax.vjp(_rms, hv, gv)
        err = y - tv
        loss = 0.5 * jnp.sum(jnp.mean(err * err, axis=-1, keepdims=True), axis=0, keepdims=True)
        dh, dg = pull(err * (1.0 / d))
        return dh, jnp.broadcast_to(loss, (1, 128)), dg

    return _rowmap("loss_head", body, [(h, d, 0), (target, d, 0)], [gamma], [(d, d, 0, _F32)],
                   [(1, 128), (1, d)], rows=s, tm=_TM)


_ROT = _A_HEAD_DIM // 4
_ROPE_THETA = 500000.0
_A_WIDTH = _A_HEADS * _A_HEAD_DIM
_NEG = -1e30


def _rotary_tables(positions):
    inv_freq = jnp.power(jnp.float32(_ROPE_THETA), -jnp.arange(0, _ROT, 2, dtype=_F32) / _ROT)
    ang = positions.reshape(-1, 1).astype(_F32) * inv_freq
    cos, sin = jnp.cos(ang), jnp.sin(ang)
    s = ang.shape[0]
    pad = jnp.zeros((s, _A_HEAD_DIM - _ROT), _F32)
    c_head = jnp.concatenate([cos, cos, pad + 1.0], axis=1)
    s_head = jnp.concatenate([-sin, sin, pad], axis=1)
    c2, s2 = jnp.tile(c_head, (1, 2)), jnp.tile(s_head, (1, 2))
    scale = _A_HEAD_DIM ** -0.5
    tab_c = jnp.stack([scale * c2, c2, jnp.ones_like(c2)])
    tab_s = jnp.stack([scale * s2, s2, jnp.zeros_like(s2)])
    return tab_c, tab_s


def _pair_swap(z):
    lane = lax.broadcasted_iota(jnp.int32, z.shape, 1) % _A_HEAD_DIM
    half = _ROT // 2
    up = jnp.where(lane < _ROT, pltpu.roll(z, half, axis=1), 0.0)
    return jnp.where(lane < half, pltpu.roll(z, 128 - half, axis=1), up)


def _rot_apply(x, c, s, transpose):
    outs = []
    for k in range(x.shape[1] // 128):
        xs = x[:, k * 128:(k + 1) * 128]
        outs.append(c * xs + (_pair_swap(s * xs) if transpose else s * _pair_swap(xs)))
    return jnp.concatenate(outs, axis=1)


def _qkv_prep(proj, tab_c, tab_s):
    s = proj.shape[0]
    w = _A_WIDTH

    def body(x_ref, c_ref, s_ref, o_ref):
        o_ref[...] = _rot_apply(x_ref[...], c_ref[...], s_ref[...], False).astype(_BF)

    tab = pl.BlockSpec((None, _TM, 128), lambda i, j: (j, i, 0))
    blk = pl.BlockSpec((_TM, w), lambda i, j: (i, j))
    return pl.pallas_call(
        body, name="qkv_prep", grid=(s // _TM, 3), in_specs=[blk, tab, tab], out_specs=blk,
        out_shape=jax.ShapeDtypeStruct((s, 3 * w), _BF), compiler_params=_params(("arbitrary", "arbitrary")),
    )(proj, tab_c, tab_s)


def _head_masks():
    lane = lax.broadcasted_iota(jnp.int32, (_WIN_BLOCK, 128), 1)
    return [lane // _A_HEAD_DIM == hh for hh in range(2)]


def _band_masks(first_block):
    qi = lax.broadcasted_iota(jnp.int32, (_WIN_BLOCK, _WIN_BLOCK), 0)
    kj = lax.broadcasted_iota(jnp.int32, (_WIN_BLOCK, _WIN_BLOCK), 1)
    return (kj >= qi) & jnp.logical_not(first_block), kj <= qi


def _attn_fwd(qkv, dil):
    s = qkv.shape[0]
    w = _A_WIDTH
    length = s // dil
    nb = length // _WIN_BLOCK
    view = qkv.reshape(length, dil * 3 * w)

    def body(q_ref, kp_ref, kc_ref, vp_ref, vc_ref, o_ref, l_ref):
        allow_p, allow_c = _band_masks(pl.program_id(1) == 0)
        hm = _head_masks()
        for sl in range(w // 128):
            cols = slice(sl * 128, (sl + 1) * 128)
            q, kp, kc, vp, vc = q_ref[:, cols], kp_ref[:, cols], kc_ref[:, cols], vp_ref[:, cols], vc_ref[:, cols]
            o_s = jnp.zeros((_WIN_BLOCK, 128), _F32)
            l_s = jnp.zeros((_WIN_BLOCK, 128), _F32)
            for hh in range(2):
                qh = jnp.where(hm[hh], q, jnp.zeros_like(q))
                sp = jnp.where(allow_p, _dot_nt(qh, kp), _NEG)
                sc = jnp.where(allow_c, _dot_nt(qh, kc), _NEG)
                m = jnp.maximum(jnp.max(sp, axis=1, keepdims=True), jnp.max(sc, axis=1, keepdims=True))
                pp, pc = jnp.exp(sp - m), jnp.exp(sc - m)
                den = jnp.sum(pp, axis=1, keepdims=True) + jnp.sum(pc, axis=1, keepdims=True)
                oh = (_dot(pp, vp) + _dot(pc, vc)) / den
                o_s = jnp.where(hm[hh], oh, o_s)
                l_s = jnp.where(hm[hh], m + jnp.log(den), l_s)
            o_ref[:, cols] = o_s.astype(o_ref.dtype)
            l_ref[:, cols] = l_s

    def at(col, prev):
        if prev:
            return pl.BlockSpec((_WIN_BLOCK, w), lambda r, n: (jnp.maximum(n - 1, 0), 3 * r + col))
        return pl.BlockSpec((_WIN_BLOCK, w), lambda r, n: (n, 3 * r + col))

    out = pl.BlockSpec((_WIN_BLOCK, w), lambda r, n: (n, r))
    sds = jax.ShapeDtypeStruct((length, dil * w), _F32)
    o, lse = pl.pallas_call(
        body, name=f"attn_fwd_d{dil}", grid=(dil, nb),
        in_specs=[at(0, False), at(1, True), at(1, False), at(2, True), at(2, False)],
        out_specs=[out, out], out_shape=[jax.ShapeDtypeStruct(sds.shape, _BF), sds],
        compiler_params=_params(("arbitrary", "arbitrary")),
    )(view, view, view, view, view)
    return o.reshape(s, w), lse.reshape(s, w)


def _attn_bwd(qkv, o, lse, do, dlse, dil):
    s = qkv.shape[0]
    w = _A_WIDTH
    length = s // dil
    nb = length // _WIN_BLOCK
    view = qkv.reshape(length, dil * 3 * w)
    o, lse, do, dlse = (t.reshape(length, dil * w) for t in (o, lse, do, dlse))

    def body(q_ref, kp_ref, kc_ref, vp_ref, vc_ref, o_ref, l_ref, do_ref, dl_ref, out_ref,
             car_q, car_k, car_v, cur_q, cur_kc, cur_vc, cur_kp, cur_vp):
        n = pl.program_id(1)

        @pl.when(n == 0)
        def _():
            car_q[...] = jnp.zeros_like(car_q)
            car_k[...] = jnp.zeros_like(car_k)
            car_v[...] = jnp.zeros_like(car_v)

        @pl.when(n == nb)
        def _():
            cur_kp[...] = jnp.zeros_like(cur_kp)
            cur_vp[...] = jnp.zeros_like(cur_vp)

        @pl.when(n < nb)
        def _():
            allow_p, allow_c = _band_masks(n == 0)
            hm = _head_masks()
            for sl in range(w // 128):
                cols = slice(sl * 128, (sl + 1) * 128)
                q, kp, kc, vp, vc = q_ref[:, cols], kp_ref[:, cols], kc_ref[:, cols], vp_ref[:, cols], vc_ref[:, cols]
                ov, dov = o_ref[:, cols].astype(_F32), do_ref[:, cols].astype(_F32)
                lv, dlv = l_ref[:, cols], dl_ref[:, cols]
                zero = jnp.zeros((_WIN_BLOCK, 128), _F32)
                dq_s, dkp_s, dkc_s, dvp_s, dvc_s = zero, zero, zero, zero, zero
                for hh in range(2):
                    qh = jnp.where(hm[hh], q, jnp.zeros_like(q))
                    doh = jnp.where(hm[hh], dov, 0.0).astype(_BF)
                    lse_h = jnp.sum(jnp.where(hm[hh], lv, 0.0), axis=1, keepdims=True) * (1.0 / _A_HEAD_DIM)
                    delta = jnp.sum(jnp.where(hm[hh], dov * ov, 0.0), axis=1, keepdims=True)
                    dl = jnp.sum(jnp.where(hm[hh], dlv, 0.0), axis=1, keepdims=True)
                    pp = jnp.exp(jnp.where(allow_p, _dot_nt(qh, kp), _NEG) - lse_h)
                    pc = jnp.exp(jnp.where(allow_c, _dot_nt(qh, kc), _NEG) - lse_h)
                    dsp = (pp * (_dot_nt(doh, vp) - delta + dl)).astype(_BF)
                    dsc = (pc * (_dot_nt(doh, vc) - delta + dl)).astype(_BF)
                    dq_s = dq_s + jnp.where(hm[hh], _dot(dsp, kp) + _dot(dsc, kc), 0.0)
                    dkp_s = dkp_s + _dot_tn(dsp, qh)
                    dkc_s = dkc_s + _dot_tn(dsc, qh)
                    dvp_s = dvp_s + _dot_tn(pp, doh)
                    dvc_s = dvc_s + _dot_tn(pc, doh)
                cur_q[:, cols] = dq_s
                cur_kp[:, cols] = dkp_s
                cur_kc[:, cols] = dkc_s
                cur_vp[:, cols] = dvp_s
                cur_vc[:, cols] = dvc_s

        out_ref[:, 0:w] = car_q[...].astype(out_ref.dtype)
        out_ref[:, w:2 * w] = (car_k[...] + cur_kp[...]).astype(out_ref.dtype)
        out_ref[:, 2 * w:3 * w] = (car_v[...] + cur_vp[...]).astype(out_ref.dtype)

        @pl.when(n < nb)
        def _():
            car_q[...] = cur_q[...]
            car_k[...] = cur_kc[...]
            car_v[...] = cur_vc[...]

    def at(col, prev):
        if prev:
            return pl.BlockSpec((_WIN_BLOCK, w), lambda r, n: (jnp.maximum(jnp.minimum(n, nb - 1) - 1, 0), 3 * r + col))
        return pl.BlockSpec((_WIN_BLOCK, w), lambda r, n: (jnp.minimum(n, nb - 1), 3 * r + col))

    blk = pl.BlockSpec((_WIN_BLOCK, w), lambda r, n: (jnp.minimum(n, nb - 1), r))
    scratch = [pltpu.VMEM((_WIN_BLOCK, w), _F32) for _ in range(8)]
    out = pl.pallas_call(
        body, name=f"attn_bwd_d{dil}", grid=(dil, nb + 1),
        in_specs=[at(0, False), at(1, True), at(1, False), at(2, True), at(2, False), blk, blk, blk, blk],
        out_specs=pl.BlockSpec((_WIN_BLOCK, 3 * w), lambda r, n: (jnp.maximum(n - 1, 0), r)),
        out_shape=jax.ShapeDtypeStruct((length, dil * 3 * w), _BF), scratch_shapes=scratch,
        compiler_params=_params(("arbitrary", "arbitrary")),
    )(view, view, view, view, view, o, lse, do, dlse)
    return out.reshape(s, 3 * w)


def _mix(o1, o2, o3, l1, l2, l3):
    m = jnp.maximum(jnp.maximum(l1, l2), l3)
    e1, e2, e3 = jnp.exp(l1 - m), jnp.exp(l2 - m), jnp.exp(l3 - m)
    return (e1 * o1 + e2 * o2 + e3 * o3) / (e1 + e2 + e3)


def _attn_mix_fwd(outs, lses):
    s = outs[0].shape[0]
    tiles = [(t, _A_WIDTH, 0) for t in (*outs, *lses)]
    return _rowmap("attn_mix_fwd", _mix, tiles, [], [(2 * _A_WIDTH, _A_WIDTH, 0, _BF)], rows=s, tm=_CONV_TM)[0]


def _attn_mix_bwd(outs, lses, dy):
    s = outs[0].shape[0]
    tiles = [(t, _A_WIDTH, 0) for t in (*outs, *lses, dy)]
    body = _vjp_body(_mix, 6, range(6), 1)
    return _rowmap("attn_mix_bwd", body, tiles, [], [(_A_WIDTH, _A_WIDTH, 0, _BF)] * 3 + [(_A_WIDTH, _A_WIDTH, 0, _F32)] * 3, rows=s, tm=_CONV_TM)


_CONV_TM = 256
_HALO = 32


def _ln_silu(pre, g, b):
    mu = jnp.mean(pre, axis=-1, keepdims=True)
    xc = pre - mu
    y = xc * lax.rsqrt(jnp.mean(xc * xc, axis=-1, keepdims=True) + _NORM_EPS)
    return _silu(y * g + b)


def _glu_into(buf, ua_ref, ub_ref, uah_ref, ubh_ref, first):
    halo = uah_ref[...] * jax.nn.sigmoid(ubh_ref[...])
    buf[0:_HALO, :] = jnp.where(first, 0.0, halo)
    buf[_HALO:, :] = ua_ref[...] * jax.nn.sigmoid(ub_ref[...])


_SHIFT_ROWS = _CONV_TM + _HALO - 8


def _shift_copies(sh, buf):
    for b in range(1, 8):
        sh[b - 1] = buf[b:b + _SHIFT_ROWS, :]


def _rows_at(buf, sh, off, rows):
    a, b = divmod(off, 8)
    return buf[8 * a:8 * a + rows, :] if b == 0 else sh[b - 1, 8 * a:8 * a + rows, :]


def _causal_taps(buf, sh, w_ref, rows):
    base = _HALO - (_CONV_WIDTH - 1)
    acc = jnp.zeros((rows, buf.shape[1]), _F32)
    for j in range(_CONV_WIDTH):
        acc = acc + w_ref[j:j + 1, :] * _rows_at(buf, sh, base + j, rows)
    return acc


def _conv_specs(s):
    tm = _CONV_TM
    per = tm // _HALO
    cw = _A_WIDTH
    cur = lambda col: pl.BlockSpec((tm, cw), lambda i: (i, col))
    prev = lambda col: pl.BlockSpec((_HALO, cw), lambda i: (jnp.maximum(i * per - 1, 0), col))
    return tm, per, cw, cur, prev


def _full2(arr):
    return pl.BlockSpec(arr.shape, lambda i: (0, 0))


def _conv_fwd(proj, dw_w, dw_b, ln_g, ln_b, y):
    s = proj.shape[0]
    tm, per, cw, cur, prev = _conv_specs(s)

    def body(ua_ref, ub_ref, uah_ref, ubh_ref, w_ref, b_ref, g_ref, be_ref, y_in, o_ref, buf, sh):
        del y_in
        _glu_into(buf, ua_ref, ub_ref, uah_ref, ubh_ref, pl.program_id(0) == 0)
        _shift_copies(sh, buf)
        pre = _causal_taps(buf, sh, w_ref, tm) + b_ref[...]
        o_ref[...] = _ln_silu(pre, g_ref[...], be_ref[...]).astype(_BF)

    return pl.pallas_call(
        body, name="conv_fwd", grid=(s // tm,),
        in_specs=[cur(3), cur(4), prev(3), prev(4), _full2(dw_w), _full2(dw_b), _full2(ln_g), _full2(ln_b), _ANY],
        out_specs=pl.BlockSpec((tm, cw), lambda i: (i, 1)), out_shape=jax.ShapeDtypeStruct(y.shape, y.dtype),
        scratch_shapes=[pltpu.VMEM((tm + _HALO, cw), _F32), pltpu.VMEM((7, _SHIFT_ROWS, cw), _F32)],
        input_output_aliases={8: 0}, compiler_params=_params(("arbitrary",)),
    )(proj, proj, proj, proj, dw_w, dw_b, ln_g, ln_b, y)


def _conv_bwd_pre(proj, dw_w, dw_b, ln_g, ln_b, dy):
    s = proj.shape[0]
    tm, per, cw, cur, prev = _conv_specs(s)

    def body(ua_ref, ub_ref, uah_ref, ubh_ref, w_ref, b_ref, g_ref, be_ref, dy_ref, dp_ref, dg_ref, dbe_ref, db_ref, buf, sh):
        i = pl.program_id(0)
        _glu_into(buf, ua_ref, ub_ref, uah_ref, ubh_ref, i == 0)
        _shift_copies(sh, buf)
        pre = _causal_taps(buf, sh, w_ref, tm) + b_ref[...]
        _, pull = jax.vjp(_ln_silu, pre, g_ref[...], be_ref[...])
        dpre, dg, dbe = pull(dy_ref[...])
        dp_ref[...] = dpre

        @pl.when(i == 0)
        def _():
            dg_ref[...] = jnp.zeros_like(dg_ref)
            dbe_ref[...] = jnp.zeros_like(dbe_ref)
            db_ref[...] = jnp.zeros_like(db_ref)

        dg_ref[...] += dg
        dbe_ref[...] += dbe
        db_ref[...] += jnp.sum(dpre, axis=0, keepdims=True)

    vec = pl.BlockSpec((1, cw), lambda i: (0, 0))
    vsd = jax.ShapeDtypeStruct((1, cw), _F32)
    return pl.pallas_call(
        body, name="conv_bwd_pre", grid=(s // tm,),
        in_specs=[cur(3), cur(4), prev(3), prev(4), _full2(dw_w), _full2(dw_b), _full2(ln_g), _full2(ln_b), cur(1)],
        out_specs=[pl.BlockSpec((tm, cw), lambda i: (i, 0)), vec, vec, vec],
        out_shape=[jax.ShapeDtypeStruct((s, cw), _F32), vsd, vsd, vsd],
        scratch_shapes=[pltpu.VMEM((tm + _HALO, cw), _F32), pltpu.VMEM((7, _SHIFT_ROWS, cw), _F32)],
        compiler_params=_params(("arbitrary",)),
    )(proj, proj, proj, proj, dw_w, dw_b, ln_g, ln_b, dy)


def _conv_bwd_taps(proj, dw_w, dpre):
    s = proj.shape[0]
    tm, per, cw, cur, prev = _conv_specs(s)
    nt = s // tm

    def body(ua_ref, ub_ref, uah_ref, ubh_ref, w_ref, dp_ref, dpn_ref, dglu_ref, dw_ref, buf, dbuf, sh, dsh):
        i = pl.program_id(0)
        _glu_into(buf, ua_ref, ub_ref, uah_ref, ubh_ref, i == 0)
        dpre = dp_ref[...]
        dbuf[0:tm, :] = dpre
        dbuf[tm:, :] = jnp.where(i == nt - 1, 0.0, dpn_ref[...])
        _shift_copies(sh, buf)
        _shift_copies(dsh, dbuf)
        acc = jnp.zeros((tm, cw), _F32)
        for j in range(_CONV_WIDTH):
            acc = acc + w_ref[j:j + 1, :] * _rows_at(dbuf, dsh, _CONV_WIDTH - 1 - j, tm)
        dglu_ref[...] = acc

        @pl.when(i == 0)
        def _():
            dw_ref[...] = jnp.zeros_like(dw_ref)

        base = _HALO - (_CONV_WIDTH - 1)
        for j in range(_CONV_WIDTH):
            dw_ref[j:j + 1, :] += jnp.sum(dpre * _rows_at(buf, sh, base + j, tm), axis=0, keepdims=True)

    nxt = pl.BlockSpec((_HALO, cw), lambda i: (jnp.minimum((i + 1) * per, s // _HALO - 1), 0))
    return pl.pallas_call(
        body, name="conv_bwd_taps", grid=(nt,),
        in_specs=[cur(3), cur(4), prev(3), prev(4), _full2(dw_w), pl.BlockSpec((tm, cw), lambda i: (i, 0)), nxt],
        out_specs=[pl.BlockSpec((tm, cw), lambda i: (i, 0)), pl.BlockSpec((_HALO, cw), lambda i: (0, 0))],
        out_shape=[jax.ShapeDtypeStruct((s, cw), _F32), jax.ShapeDtypeStruct((_HALO, cw), _F32)],
        scratch_shapes=[pltpu.VMEM((tm + _HALO, cw), _F32), pltpu.VMEM((tm + _HALO, cw), _F32),
                        pltpu.VMEM((7, _SHIFT_ROWS, cw), _F32), pltpu.VMEM((7, _SHIFT_ROWS, cw), _F32)],
        compiler_params=_params(("arbitrary",)),
    )(proj, proj, proj, proj, dw_w, dpre, dpre)


def _hyb_dproj(dqkv, tab_c, tab_s, dglu, proj):
    s = proj.shape[0]
    w = _A_WIDTH

    def body(d1_ref, d2_ref, d3_ref, c_ref, s_ref, dg_ref, ua_ref, ub_ref, o_ref):
        j = pl.program_id(1)

        @pl.when(j < 3)
        def _():
            x = d1_ref[...].astype(_F32) + d2_ref[...].astype(_F32) + d3_ref[...].astype(_F32)
            o_ref[...] = _rot_apply(x, c_ref[...], s_ref[...], True).astype(_BF)

        @pl.when(j == 3)
        def _():
            o_ref[...] = (dg_ref[...] * jax.nn.sigmoid(ub_ref[...])).astype(_BF)

        @pl.when(j == 4)
        def _():
            sg = jax.nn.sigmoid(ub_ref[...])
            o_ref[...] = (dg_ref[...] * ua_ref[...] * sg * (1.0 - sg)).astype(_BF)

    dq = pl.BlockSpec((_TM, w), lambda i, j: (i, jnp.minimum(j, 2)))
    tab = pl.BlockSpec((None, _TM, 128), lambda i, j: (jnp.minimum(j, 2), i, 0))
    col = lambda c: pl.BlockSpec((_TM, w), lambda i, j: (i, c))
    return pl.pallas_call(
        body, name="hyb_dproj", grid=(s // _TM, 5),
        in_specs=[dq, dq, dq, tab, tab, col(0), col(3), col(4)],
        out_specs=pl.BlockSpec((_TM, w), lambda i, j: (i, j)),
        out_shape=jax.ShapeDtypeStruct((s, 5 * w), _BF), compiler_params=_params(("arbitrary", "arbitrary")),
    )(*dqkv, tab_c, tab_s, dglu, proj, proj)


def _proj_in(name, xn, w, layer):
    s, d = xn.shape
    ns = w.shape[-1]
    return _mm(name, xn, w, grid=(4, s // _TM), a_spec=pl.BlockSpec((_TM, d), lambda j, i: (i, 0)),
               b_spec=pl.BlockSpec((None, None, d, ns), lambda j, i: (layer, j, 0, 0)), dn=_NN,
               out_shape=jax.ShapeDtypeStruct((s, 4 * ns), _F32), out_spec=pl.BlockSpec((_TM, ns), lambda j, i: (i, j)))


def _proj_in_dx(name, dz, w, layer, h, gamma, dh_out):
    s = dz.shape[0]
    d, ns = w.shape[2], w.shape[3]
    return _mm(name, dz, w, grid=(s // _TM, 4), a_spec=pl.BlockSpec((_TM, ns), lambda i, j: (i, j)),
               b_spec=pl.BlockSpec((None, None, d, ns), lambda i, j: (layer, j, 0, 0)), dn=_NT, red=1,
               acc_shape=(_TM, d), out_shape=jax.ShapeDtypeStruct((s, d), _F32),
               out_spec=pl.BlockSpec((_TM, d), lambda i, j: (i, 0)), epi=_norm_bwd_epi,
               **_norm_bwd_args(h, dh_out, gamma, _TM))


def _proj_in_dw(name, xn, dz):
    s, d = xn.shape
    ns = dz.shape[1] // 4
    return _mm(name, xn, dz, grid=(4, s // _TM), a_spec=pl.BlockSpec((_TM, d), lambda j, i: (i, 0)),
               b_spec=pl.BlockSpec((_TM, ns), lambda j, i: (i, j)), dn=_TN, red=1, acc_shape=(d, ns),
               out_shape=jax.ShapeDtypeStruct((4, d, ns), _BF),
               out_spec=pl.BlockSpec((None, d, ns), lambda j, i: (j, 0, 0)))


def _proj_out_dy(name, dh, w, layer):
    s, d = dh.shape
    k = w.shape[1]
    return _mm(name, dh, w, grid=(s // _TM,), a_spec=pl.BlockSpec((_TM, d), lambda i: (i, 0)),
               b_spec=pl.BlockSpec((None, k, d), lambda i: (layer, 0, 0)), dn=_NT,
               out_shape=jax.ShapeDtypeStruct((s, k), _F32), out_spec=pl.BlockSpec((_TM, k), lambda i: (i, 0)))


def _hyb_fwd(h, gamma, w_in, w_out, dw_w, dw_b, ln_g, ln_b, tabs, li):
    hn = _rmsnorm_fwd(h, gamma)
    proj = _proj_in("hyb_in", hn, w_in, li)
    qkv = _qkv_prep(proj, *tabs)
    branches = [_attn_fwd(qkv, dil) for dil in _DILATIONS]
    outs, lses = [b[0] for b in branches], [b[1] for b in branches]
    y = _attn_mix_fwd(outs, lses)
    y = _conv_fwd(proj, dw_w, dw_b, ln_g, ln_b, y)
    h_new = _proj_residual("hyb_out", y, w_out, li, h, 1.0)
    return h_new, (h, hn, proj, qkv, outs, lses, y)


def _hyb_bwd(dh, saved, gamma, w_in, w_out, dw_w, dw_b, ln_g, ln_b, tabs, li):
    h, hn, proj, qkv, outs, lses, y = saved
    dy = _proj_out_dy("hyb_dy", dh, w_out, li)
    d_w_out = _wgrad_rows("hyb_dwout", y, dh, 1.0)
    mixg = _attn_mix_bwd(outs, lses, dy)
    dqkv = [_attn_bwd(qkv, outs[b], lses[b], mixg[b], mixg[3 + b], dil) for b, dil in enumerate(_DILATIONS)]
    dpre, d_ln_g, d_ln_b, d_dw_b = _conv_bwd_pre(proj, dw_w, dw_b, ln_g, ln_b, dy)
    dglu, d_dw_w = _conv_bwd_taps(proj, dw_w, dpre)
    dproj = _hyb_dproj(dqkv, *tabs, dglu, proj)
    dh_in, dgamma = _proj_in_dx("hyb_dhn", dproj, w_in, li, h, gamma, dh)
    d_w_in = _proj_in_dw("hyb_dwin", hn, dproj)
    return dh_in, dgamma, (d_dw_w, d_dw_b, d_ln_g, d_ln_b), d_w_in, d_w_out


_GDN_BLOCK = 128
_GDN_HEADS_PER_STEP = 8


def _gdn_chunk(qs, ks, vs, gates, states, heads):
    n = len(qs)
    hs = range(n)
    c = qs[0].shape[0]
    lane = lax.broadcasted_iota(jnp.int32, gates.shape, 1)
    row = lax.broadcasted_iota(jnp.int32, (c, c), 0)
    col = lax.broadcasted_iota(jnp.int32, (c, c), 1)
    rowc = lax.broadcasted_iota(jnp.int32, (c, 1), 0)
    causal = row >= col
    eye = jnp.where(row == col, 1.0, 0.0)
    beta = [jnp.sum(jnp.where(lane == heads[h], gates, 0.0), axis=1, keepdims=True) for h in hs]
    g = [jnp.sum(jnp.where(lane == heads[h] + _GDN_HEADS, gates, 0.0), axis=1, keepdims=True) for h in hs]
    gc_row = [jnp.sum(jnp.where(row <= col, g[h], 0.0), axis=0, keepdims=True) for h in hs]
    gc_col = [jnp.sum(jnp.where(row == col, gc_row[h], 0.0), axis=1, keepdims=True) for h in hs]
    decay = [jnp.where(causal, jnp.exp(jnp.where(causal, gc_col[h] - gc_row[h], 0.0)), 0.0) for h in hs]
    kb = [ks[h] * beta[h] for h in hs]
    l_mat = [jnp.where(row > col, _dot_nt(kb[h], ks[h]) * decay[h], 0.0) for h in hs]
    t_inv = [eye - l_mat[h] for h in hs]
    x = [_dot(l_mat[h], l_mat[h]) for h in hs]
    for it in range(5):
        t_inv = [t_inv[h] + _dot(t_inv[h], x[h]) for h in hs]
        if it < 4:
            x = [_dot(x[h], x[h]) for h in hs]
    eg = [jnp.exp(gc_col[h]) for h in hs]
    u = [_dot(t_inv[h], vs[h] * beta[h]) for h in hs]
    w = [_dot(t_inv[h], kb[h] * eg[h]) for h in hs]
    attn = [jnp.where(causal, _dot_nt(qs[h], ks[h]) * decay[h], 0.0) for h in hs]
    g_last = [jnp.sum(jnp.where(rowc == c - 1, gc_col[h], 0.0), axis=0, keepdims=True) for h in hs]
    k_dec = [ks[h] * jnp.exp(g_last[h] - gc_col[h]) for h in hs]
    v_new = [u[h] - _dot(w[h], states[h]) for h in hs]
    o = [_dot(qs[h] * eg[h], states[h]) + _dot(attn[h], v_new[h]) for h in hs]
    new_states = [states[h] * jnp.exp(g_last[h]) + _dot_tn(k_dec[h], v_new[h]) for h in hs]
    return o, new_states


def _gdn_chunk_fwd(q, k, v, gates):
    s = q.shape[0]
    cb, c = _GDN_BLOCK, _GDN_CHUNK
    nblk, per = s // cb, cb // c

    hps = _GDN_HEADS_PER_STEP

    def body(q_ref, k_ref, v_ref, g_ref, o_ref, st_ref, state):
        @pl.when(pl.program_id(1) == 0)
        def _():
            state[...] = jnp.zeros_like(state)

        heads = [pl.program_id(0) * hps + hh for hh in range(hps)]
        cols = [slice(hh * _GDN_DIM, (hh + 1) * _GDN_DIM) for hh in range(hps)]
        for ci in range(per):
            rows = slice(ci * c, (ci + 1) * c)
            sts = [state[hh] for hh in range(hps)]
            for hh in range(hps):
                st_ref[hh, ci] = sts[hh]
            outs, new = _gdn_chunk([q_ref[rows, cl] for cl in cols], [k_ref[rows, cl] for cl in cols],
                                   [v_ref[rows, cl] for cl in cols], g_ref[rows, :], sts, heads)
            for hh in range(hps):
                o_ref[rows, cols[hh]] = outs[hh]
                state[hh] = new[hh]

    slab = pl.BlockSpec((cb, hps * _GDN_DIM), lambda h, i: (i, h))
    return pl.pallas_call(
        body, name="gdn_chunk_fwd", grid=(_GDN_HEADS // hps, nblk),
        in_specs=[slab, slab, slab, pl.BlockSpec((cb, 128), lambda h, i: (i, 0))],
        out_specs=[slab, pl.BlockSpec((hps, per, _GDN_DIM, _GDN_DIM), lambda h, i: (h, i, 0, 0))],
        out_shape=[jax.ShapeDtypeStruct((s, _GDN_HEADS * _GDN_DIM), _F32),
                   jax.ShapeDtypeStruct((_GDN_HEADS, s // c, _GDN_DIM, _GDN_DIM), _F32)],
        scratch_shapes=[pltpu.VMEM((hps, _GDN_DIM, _GDN_DIM), _F32)],
        compiler_params=_params(("arbitrary", "arbitrary")),
    )(q, k, v, gates)


def _gdn_chunk_bwd(q, k, v, gates, states, do):
    s = q.shape[0]
    cb, c = _GDN_BLOCK, _GDN_CHUNK
    nblk, per = s // cb, cb // c

    hps = _GDN_HEADS_PER_STEP

    def body(q_ref, k_ref, v_ref, g_ref, st_ref, do_ref, dq_ref, dk_ref, dv_ref, dg_ref, dstate):
        @pl.when(pl.program_id(1) == 0)
        def _():
            dstate[...] = jnp.zeros_like(dstate)

        heads = [pl.program_id(0) * hps + hh for hh in range(hps)]
        cols = [slice(hh * _GDN_DIM, (hh + 1) * _GDN_DIM) for hh in range(hps)]
        for ci in reversed(range(per)):
            rows = slice(ci * c, (ci + 1) * c)
            fn = functools.partial(_gdn_chunk, heads=heads)
            _, pull = jax.vjp(fn, [q_ref[rows, cl] for cl in cols], [k_ref[rows, cl] for cl in cols],
                              [v_ref[rows, cl] for cl in cols], g_ref[rows, :], [st_ref[hh, ci] for hh in range(hps)])
            dq, dk, dv, dg, dst = pull(([do_ref[rows, cl] for cl in cols], [dstate[hh] for hh in range(hps)]))
            dg_ref[rows, :] = dg
            for hh in range(hps):
                dq_ref[rows, cols[hh]] = dq[hh]
                dk_ref[rows, cols[hh]] = dk[hh]
                dv_ref[rows, cols[hh]] = dv[hh]
                dstate[hh] = dst[hh]

    slab = pl.BlockSpec((cb, hps * _GDN_DIM), lambda h, i: (nblk - 1 - i, h))
    sds = jax.ShapeDtypeStruct((s, _GDN_HEADS * _GDN_DIM), _F32)
    return pl.pallas_call(
        body, name="gdn_chunk_bwd", grid=(_GDN_HEADS // hps, nblk),
        in_specs=[slab, slab, slab, pl.BlockSpec((cb, 128), lambda h, i: (nblk - 1 - i, 0)),
                  pl.BlockSpec((hps, per, _GDN_DIM, _GDN_DIM), lambda h, i: (h, nblk - 1 - i, 0, 0)), slab],
        out_specs=[slab, slab, slab, pl.BlockSpec((None, cb, 128), lambda h, i: (h, nblk - 1 - i, 0))],
        out_shape=[sds, sds, sds, jax.ShapeDtypeStruct((_GDN_HEADS // hps, s, 128), _F32)],
        scratch_shapes=[pltpu.VMEM((hps, _GDN_DIM, _GDN_DIM), _F32)],
        compiler_params=_params(("arbitrary", "arbitrary")),
    )(q, k, v, gates, states, do)


_SHALO = 8


def _swish_l2(c, normalize, scale):
    y = _silu(c)
    if normalize:
        y = y * lax.rsqrt(jnp.sum(y * y, axis=-1, keepdims=True) + _NORM_EPS) * scale
    return y


def _short_taps(buf, w_ref, rows):
    base = _SHALO - (_GDN_SHORT_CONV - 1)
    acc = jnp.zeros((rows, buf.shape[1]), _F32)
    for t in range(_GDN_SHORT_CONV):
        acc = acc + w_ref[t:t + 1, :] * buf[base + t:base + t + rows, :]
    return acc


_GDN_W = _GDN_HEADS * _GDN_DIM


def _gdn_conv_fwd(proj, conv_w, grp, normalize, scale):
    s = proj.shape[0]
    tm = _CONV_TM
    per = tm // _SHALO

    def body(x_ref, xp_ref, w_ref, o_ref, buf):
        buf[0:_SHALO, :] = jnp.where(pl.program_id(0) == 0, 0.0, xp_ref[...])
        buf[_SHALO:, :] = x_ref[...]
        conv = _short_taps(buf, w_ref, tm)
        for h in range(_GDN_HEADS):
            cols = slice(h * _GDN_DIM, (h + 1) * _GDN_DIM)
            o_ref[:, cols] = _swish_l2(conv[:, cols], normalize, scale)

    return pl.pallas_call(
        body, name=f"gdn_conv_fwd_{grp}", grid=(s // tm,),
        in_specs=[pl.BlockSpec((tm, _GDN_W), lambda i: (i, grp)),
                  pl.BlockSpec((_SHALO, _GDN_W), lambda i: (jnp.maximum(i * per - 1, 0), grp)),
                  pl.BlockSpec((_GDN_SHORT_CONV, _GDN_W), lambda i: (0, grp))],
        out_specs=pl.BlockSpec((tm, _GDN_W), lambda i: (i, 0)),
        out_shape=jax.ShapeDtypeStruct((s, _GDN_W), _F32),
        scratch_shapes=[pltpu.VMEM((tm + _SHALO, _GDN_W), _F32)], compiler_params=_params(("arbitrary",)),
    )(proj, proj, conv_w)


def _gdn_conv_bwd(proj, conv_w, grp, normalize, scale, dy, dproj):
    s = proj.shape[0]
    tm = _CONV_TM
    per = tm // _SHALO
    nt = s // tm
    taps = _GDN_SHORT_CONV

    def body(x_ref, xp_ref, xn_ref, w_ref, dy_ref, dyn_ref, dp_in, dx_ref, dw_ref, buf, dbuf):
        del dp_in
        i = pl.program_id(0)
        last = i == nt - 1
        buf[0:_SHALO, :] = jnp.where(i == 0, 0.0, xp_ref[...])
        buf[_SHALO:_SHALO + tm, :] = x_ref[...]
        buf[_SHALO + tm:, :] = jnp.where(last, 0.0, xn_ref[...])
        conv = _short_taps(buf, w_ref, tm + _SHALO)
        dy_ext = jnp.concatenate([dy_ref[...], jnp.where(last, 0.0, dyn_ref[...])], axis=0)
        for h in range(_GDN_HEADS):
            cols = slice(h * _GDN_DIM, (h + 1) * _GDN_DIM)
            _, pull = jax.vjp(lambda c: _swish_l2(c, normalize, scale), conv[:, cols])
            dbuf[:, cols] = pull(dy_ext[:, cols])[0]
        acc = jnp.zeros((tm, _GDN_W), _F32)
        for t in range(taps):
            o = taps - 1 - t
            acc = acc + w_ref[t:t + 1, :] * dbuf[o:o + tm, :]
        dx_ref[...] = acc.astype(dx_ref.dtype)

        @pl.when(i == 0)
        def _():
            dw_ref[...] = jnp.zeros_like(dw_ref)

        base = _SHALO - (taps - 1)
        for t in range(taps):
            dw_ref[t:t + 1, :] += jnp.sum(dbuf[0:tm, :] * buf[base + t:base + t + tm, :], axis=0, keepdims=True)

    cur = lambda col: pl.BlockSpec((tm, _GDN_W), lambda i: (i, col))
    nxt = lambda col: pl.BlockSpec((_SHALO, _GDN_W), lambda i: (jnp.minimum((i + 1) * per, s // _SHALO - 1), col))
    return pl.pallas_call(
        body, name=f"gdn_conv_bwd_{grp}", grid=(nt,),
        in_specs=[cur(grp), pl.BlockSpec((_SHALO, _GDN_W), lambda i: (jnp.maximum(i * per - 1, 0), grp)), nxt(grp),
                  pl.BlockSpec((taps, _GDN_W), lambda i: (0, grp)), cur(0), nxt(0), _ANY],
        out_specs=[cur(grp), pl.BlockSpec((_SHALO, _GDN_W), lambda i: (0, 0))],
        out_shape=[jax.ShapeDtypeStruct(dproj.shape, dproj.dtype), jax.ShapeDtypeStruct((_SHALO, _GDN_W), _F32)],
        scratch_shapes=[pltpu.VMEM((tm + 2 * _SHALO, _GDN_W), _F32), pltpu.VMEM((tm + _SHALO, _GDN_W), _F32)],
        input_output_aliases={6: 0}, compiler_params=_params(("arbitrary",)),
    )(proj, proj, proj, conv_w, dy, dy, dproj)


def _softplus(z):
    return jnp.maximum(z, 0.0) + jnp.log(1.0 + jnp.exp(-jnp.abs(z)))


def _gates(ba, a_lane, dt_lane):
    lane = lax.broadcasted_iota(jnp.int32, ba.shape, 1)
    g = -jnp.exp(a_lane) * _softplus(ba + dt_lane)
    return jnp.where(lane < _GDN_HEADS, jax.nn.sigmoid(ba), jnp.where(lane < 2 * _GDN_HEADS, g, 0.0))


def _gates_fwd(ba, a_lane, dt_lane):
    return _rowmap("gdn_gates_fwd", _gates, [(ba, 128, 0)], [a_lane, dt_lane], [(128, 128, 0, _F32)],
                   rows=ba.shape[0], tm=_TM)[0]


def _gates_bwd(ba, a_lane, dt_lane, dgates):
    def body(bav, dg8, av, dv):
        _, pull = jax.vjp(_gates, bav, av, dv)
        return pull(jnp.sum(dg8, axis=0))

    return _rowmap("gdn_gates_bwd", body, [(ba, 128, 0), (dgates, 128, 0)], [a_lane, dt_lane], [(128, 128, 0, _BF)],
                   [(1, 128), (1, 128)], rows=ba.shape[0], tm=_TM)


def _out_gate(o, z, g):
    return (_rms(o, g) * _silu(z)).astype(_BF)


def _head_cols():
    return [slice(h * _GDN_DIM, (h + 1) * _GDN_DIM) for h in range(_GDN_HEADS)]


def _out_gate_fwd(o, proj, norm_g):
    def body(ov, zv, gv):
        return jnp.concatenate([_out_gate(ov[:, cl], zv[:, cl], gv) for cl in _head_cols()], axis=1)

    return _rowmap("gdn_out_gate_fwd", body, [(o, _GDN_W, 0), (proj, _GDN_W, 3)], [norm_g], [(_GDN_W, _GDN_W, 0, _BF)],
                   rows=o.shape[0], tm=_CONV_TM)[0]


def _out_gate_bwd(o, proj, norm_g, dy):
    def body(ov, zv, dyv, gv):
        d_o, d_z, d_g = [], [], jnp.zeros_like(gv)
        for cl in _head_cols():
            _, pull = jax.vjp(_out_gate, ov[:, cl], zv[:, cl], gv)
            g_o, g_z, g_g = pull(dyv[:, cl].astype(_BF))
            d_o.append(g_o)
            d_z.append(g_z)
            d_g = d_g + g_g
        return jnp.concatenate(d_o, axis=1), jnp.concatenate(d_z, axis=1), d_g

    return _rowmap("gdn_out_gate_bwd", body, [(o, _GDN_W, 0), (proj, _GDN_W, 3), (dy, _GDN_W, 0)], [norm_g],
                   [(_GDN_W, _GDN_W, 0, _F32), (proj.shape[1], _GDN_W, 3, _BF)], [(1, _GDN_DIM)],
                   rows=o.shape[0], tm=_CONV_TM)


def _plain_in(name, xn, w, layer, tn):
    s, d = xn.shape
    n = w.shape[2]
    return _mm(name, xn, w, grid=(n // tn, s // _TM), a_spec=pl.BlockSpec((_TM, d), lambda j, i: (i, 0)),
               b_spec=pl.BlockSpec((None, d, tn), lambda j, i: (layer, 0, j)), dn=_NN,
               out_shape=jax.ShapeDtypeStruct((s, n), _F32), out_spec=pl.BlockSpec((_TM, tn), lambda j, i: (i, j)))


def _plain_dx(name, dz, w, layer, tn, norm=None):
    s, n = dz.shape
    d = w.shape[1]
    row = pl.BlockSpec((_TM, d), lambda i, j: (i, 0))
    kw = {}
    if norm is not None:
        extra, h, gamma, dh_out = norm
        kw = _norm_bwd_args(h, dh_out, gamma, _TM)
        kw.update(extras=(extra, *kw["extras"]), extra_specs=(row, *kw["extra_specs"]),
                  epi=lambda acc, e, hv, dhv, gv: _norm_bwd_epi(acc + e, hv, dhv, gv))
    return _mm(name, dz, w, grid=(s // _TM, n // tn), a_spec=pl.BlockSpec((_TM, tn), lambda i, j: (i, j)),
               b_spec=pl.BlockSpec((None, d, tn), lambda i, j: (layer, 0, j)), dn=_NT, red=1, acc_shape=(_TM, d),
               out_shape=jax.ShapeDtypeStruct((s, d), _F32), out_spec=row, **kw)


def _plain_dw(name, xn, dz, tn):
    s, d = xn.shape
    n = dz.shape[1]
    return _mm(name, xn, dz, grid=(n // tn, s // _TM), a_spec=pl.BlockSpec((_TM, d), lambda j, i: (i, 0)),
               b_spec=pl.BlockSpec((_TM, tn), lambda j, i: (i, j)), dn=_TN, red=1, acc_shape=(d, tn),
               out_shape=jax.ShapeDtypeStruct((d, n), _BF), out_spec=pl.BlockSpec((d, tn), lambda j, i: (0, j)))


_GDN_QK_SCALE = _GDN_DIM ** -0.5


def _gdn_fwd(h, gamma, w_qkvz, w_ba, w_out, conv_w, a_lane, dt_lane, norm_g, li):
    hn = _rmsnorm_fwd(h, gamma)
    proj = _plain_in("gdn_in", hn, w_qkvz, li, 1024)
    ba = _plain_in("gdn_in_ba", hn, w_ba, li, 128)
    q = _gdn_conv_fwd(proj, conv_w, 0, True, _GDN_QK_SCALE)
    k = _gdn_conv_fwd(proj, conv_w, 1, True, 1.0)
    v = _gdn_conv_fwd(proj, conv_w, 2, False, 1.0)
    gates = _gates_fwd(ba, a_lane, dt_lane)
    o, states = _gdn_chunk_fwd(q, k, v, gates)
    y = _out_gate_fwd(o, proj, norm_g)
    h_new = _proj_residual("gdn_out", y, w_out, li, h, 1.0)
    return h_new, (h, hn, proj, ba, q, k, v, gates, states, o, y)


def _gdn_bwd(dh, saved, gamma, w_qkvz, w_ba, w_out, conv_w, a_lane, dt_lane, norm_g, li, n_ba):
    h, hn, proj, ba, q, k, v, gates, states, o, y = saved
    dy = _proj_out_dy("gdn_dy", dh, w_out, li)
    d_w_out = _wgrad_rows("gdn_dwout", y, dh, 1.0)
    do, dproj, d_norm_g = _out_gate_bwd(o, proj, norm_g, dy)
    dq, dk, dv, dgates = _gdn_chunk_bwd(q, k, v, gates, states, do)
    dproj, dcw_q = _gdn_conv_bwd(proj, conv_w, 0, True, _GDN_QK_SCALE, dq, dproj)
    dproj, dcw_k = _gdn_conv_bwd(proj, conv_w, 1, True, 1.0, dk, dproj)
    dproj, dcw_v = _gdn_conv_bwd(proj, conv_w, 2, False, 1.0, dv, dproj)
    dba, d_a, d_dt = _gates_bwd(ba, a_lane, dt_lane, dgates)
    dhn_ba = _plain_dx("gdn_dhn_ba", dba, w_ba, li, 128)
    dh_in, dgamma = _plain_dx("gdn_dhn", dproj, w_qkvz, li, 1024, norm=(dhn_ba, h, gamma, dh))
    dw_qkvz = _plain_dw("gdn_dwin", hn, dproj, 1024)
    dw_ba = _plain_dw("gdn_dwin_ba", hn, dba, 128)
    d_conv_w = jnp.concatenate([dcw_q, dcw_k, dcw_v], axis=1)[:_GDN_SHORT_CONV]
    d_w_in = jnp.concatenate([dw_qkvz, dw_ba[:, :n_ba]], axis=1)
    d_w_in = jnp.transpose(d_w_in.reshape(d_w_in.shape[0], 4, -1), (1, 0, 2))
    return dh_in, dgamma, (d_conv_w, d_a, d_dt, d_norm_g), d_w_in, d_w_out


_MESH = pl.DeviceIdType.MESH
_ROW_TM = 384


def _place():
    x, y, c = lax.axis_index("x"), lax.axis_index("y"), lax.axis_index("c")
    chips = [(1 - x, y), (x, 1 - y), (1 - x, 1 - y)]
    return x, y, c, chips, [2 * px + py for px, py in chips]


def _remote(src, dst, send_sem, recv_sem, device):
    return pltpu.make_async_remote_copy(src_ref=src, dst_ref=dst, send_sem=send_sem, recv_sem=recv_sem,
                                        device_id=device, device_id_type=_MESH)


def _rows2d(t):
    return t.reshape(-1, t.shape[-1])


def _row_tile(rows):
    for tm in range(min(rows, _ROW_TM) // 8 * 8, 7, -8):
        if rows % tm == 0:
            return tm
    raise ValueError(f"no row tile for {rows} rows")


def _placed_map(name, fn, where, ins, out_rows, cols, out_dtype, out_row, steps, tm, into=None):
    n_in = len(ins)

    def kern(where_ref, *refs):
        del where_ref
        o_ref = refs[n_in + (1 if into is not None else 0)]
        o_ref[...] = fn(*[r[...] for r in refs[:n_in]]).astype(o_ref.dtype)

    in_specs, operands = [], [where]
    for arr, rowfn in ins:
        operands.append(arr)
        if arr.ndim == 3:
            in_specs.append(pl.BlockSpec((arr.shape[0], tm, cols), lambda i, wh, f=rowfn: (0, f(i, wh), 0)))
        else:
            in_specs.append(pl.BlockSpec((tm, cols), lambda i, wh, f=rowfn: (f(i, wh), 0)))
    aliases = {}
    if into is not None:
        operands.append(into)
        in_specs.append(_ANY)
        aliases = {n_in + 1: 0}
    spec = pltpu.PrefetchScalarGridSpec(
        num_scalar_prefetch=1, grid=(steps,), in_specs=in_specs,
        out_specs=pl.BlockSpec((tm, cols), lambda i, wh: (out_row(i, wh), 0)))
    return pl.pallas_call(
        kern, name=name, grid_spec=spec, out_shape=jax.ShapeDtypeStruct((out_rows, cols), out_dtype),
        input_output_aliases=aliases, compiler_params=_params(("arbitrary",)),
    )(*operands)


def _cast_slab(name, t, where):
    n_l, r, cols = t.shape
    tm = _row_tile(r)
    per = r // tm
    buf = _placed_map("cast_" + name, lambda val: val, where, [(_rows2d(t), lambda i, wh: i)], n_l * 4 * r, cols, _BF,
                      lambda i, wh: ((i // per) * 4 + wh[1]) * per + i % per, n_l * per, tm)
    return buf.reshape(n_l, 4, r, cols)


_HBM = pl.BlockSpec(memory_space=pltpu.HBM)
_SEM = pl.BlockSpec(memory_space=pltpu.SEMAPHORE)
_EFFECT = pltpu.SideEffectType.DATAFLOW_SIDE_EFFECTING


def _core_layers(n_layers, c, part):
    half = n_layers // 2
    sub = half // part[1]
    return pl.ds(c * half + part[0] * sub, sub), pl.ds((1 - c) * half + part[0] * sub, sub)


def _ici_copies(refs, send, recv, arrivals, part):
    x, y, c, chips, idx = _place()
    me = 2 * x + y
    res = []
    for t, ref in enumerate(refs):
        mine, _ = _core_layers(ref.shape[0], c, part)
        for k in range(3):
            sem = 3 * t + k
            block = ref.at[mine, idx[k] if arrivals else me]
            res.append(_remote(block, block, send.at[sem], recv.at[sem], (*chips[k], c)))
    return res


def _gather_start(name, bufs, after, part):
    n = len(bufs)

    def body(*refs):
        send, recv = refs[n + 1], refs[n + 2]
        for cp in _ici_copies(refs[:n], send, recv, False, part):
            cp.start()
        refs[2 * n + 3][...] = jnp.zeros((8, 128), _F32)

    res = pl.pallas_call(
        body, name=name,
        out_shape=(pltpu.SemaphoreType.DMA((3 * n,)), pltpu.SemaphoreType.DMA((3 * n,)),
                   *[pltpu.HBM(b.shape, b.dtype) for b in bufs], jax.ShapeDtypeStruct((8, 128), _F32)),
        in_specs=[*[_HBM] * n, _ANY], out_specs=(_SEM, _SEM, *[_HBM] * n, pl.BlockSpec(memory_space=pltpu.VMEM)),
        input_output_aliases={t: 2 + t for t in range(n)},
        compiler_params=pltpu.CompilerParams(has_side_effects=_EFFECT),
    )(*[pltpu.with_memory_space_constraint(b, pltpu.HBM) for b in bufs], after)
    return res[0], res[1], list(res[2:2 + n]), res[2 + n]


def _gather_wait(name, send_sem, recv_sem, bufs, after, part):
    n = len(bufs)

    def body(*refs):
        send, recv = refs[n], refs[n + 1]
        for cp in _ici_copies(refs[:n], send, recv, False, part):
            cp.wait_send()
        for cp in _ici_copies(refs[:n], send, recv, True, part):
            cp.wait_recv()

    res = pl.pallas_call(
        body, name=name, out_shape=tuple(pltpu.HBM(b.shape, b.dtype) for b in bufs),
        in_specs=[*[_HBM] * n, _SEM, _SEM, _ANY], out_specs=tuple([_HBM] * n),
        input_output_aliases={t: t for t in range(n)},
        compiler_params=pltpu.CompilerParams(has_side_effects=_EFFECT),
    )(*bufs, send_sem, recv_sem, after)
    return list(res)


def _gather_forward(name, bufs, part):
    n = len(bufs)

    def body(*refs):
        outs, send, recv = refs[n:2 * n], refs[2 * n], refs[2 * n + 1]
        x, y, c, _, idx = _place()
        copies, arrive = [], []
        for t, ref in enumerate(outs):
            mine, other = _core_layers(ref.shape[0], c, part)
            for k in range(3):
                sem = 3 * t + k
                landed, dst = ref.at[mine, idx[k]], ref.at[other, idx[k]]
                copies.append(_remote(landed, landed, send.at[sem], recv.at[sem], (x, y, 1 - c)))
                arrive.append(_remote(dst, dst, send.at[sem], recv.at[sem], (x, y, 1 - c)))
        for cp in copies:
            cp.start()
        for cp in arrive:
            cp.wait_recv()
        for cp in copies:
            cp.wait_send()

    res = pl.pallas_call(
        body, name=name, in_specs=[_ANY] * n, out_specs=[_ANY] * n,
        out_shape=[jax.ShapeDtypeStruct(b.shape, b.dtype) for b in bufs],
        scratch_shapes=[pltpu.SemaphoreType.DMA((3 * n,)), pltpu.SemaphoreType.DMA((3 * n,))],
        input_output_aliases={t: t for t in range(n)},
    )(*bufs)
    return list(res)


def _rs_sibling(name, gs):
    n = len(gs)

    def body(*refs):
        send, recv = refs[2 * n], refs[2 * n + 1]
        x, y, c, _, _ = _place()
        copies = []
        for t in range(n):
            rh = refs[t].shape[1] // 2
            cp = _remote(refs[t].at[:, pl.ds((1 - c) * rh, rh)], refs[n + t], send.at[t], recv.at[t], (x, y, 1 - c))
            cp.start()
            copies.append(cp)
        for cp in copies:
            cp.wait()

    return pl.pallas_call(
        body, name=name, in_specs=[_ANY] * n, out_specs=[_ANY] * n,
        out_shape=[jax.ShapeDtypeStruct((4, g.shape[1] // 2, g.shape[2]), g.dtype) for g in gs],
        scratch_shapes=[pltpu.SemaphoreType.DMA((n,)), pltpu.SemaphoreType.DMA((n,))],
    )(*gs)


def _chip_copies(sums, lands, send, recv):
    x, y, c, chips, idx = _place()
    return [_remote(sums[t].at[idx[k]], lands[t].at[k], send.at[3 * t + k], recv.at[3 * t + k], (*chips[k], c))
            for t in range(len(sums)) for k in range(3)]


def _rs_chips_start(name, sums, after):
    n = len(sums)
    lands = [lax.empty((3,) + s.shape[1:], s.dtype) for s in sums]

    def body(*refs):
        send, recv = refs[2 * n + 1], refs[2 * n + 2]
        for cp in _chip_copies(refs[:n], refs[n:2 * n], send, recv):
            cp.start()
        refs[4 * n + 3][...] = jnp.zeros((8, 128), _F32)

    both = [*sums, *lands]
    res = pl.pallas_call(
        body, name=name,
        out_shape=(pltpu.SemaphoreType.DMA((3 * n,)), pltpu.SemaphoreType.DMA((3 * n,)),
                   *[pltpu.HBM(b.shape, b.dtype) for b in both], jax.ShapeDtypeStruct((8, 128), _F32)),
        in_specs=[*[_HBM] * (2 * n), _ANY],
        out_specs=(_SEM, _SEM, *[_HBM] * (2 * n), pl.BlockSpec(memory_space=pltpu.VMEM)),
        input_output_aliases={t: 2 + t for t in range(2 * n)},
        compiler_params=pltpu.CompilerParams(has_side_effects=_EFFECT),
    )(*[pltpu.with_memory_space_constraint(b, pltpu.HBM) for b in both], after)
    return res[0], res[1], list(res[2:2 + n]), list(res[2 + n:2 + 2 * n]), res[2 + 2 * n]


def _rs_chips_wait(name, send_sem, recv_sem, sums, lands, after):
    n = len(sums)

    def body(*refs):
        send, recv = refs[2 * n], refs[2 * n + 1]
        copies = _chip_copies(refs[:n], refs[n:2 * n], send, recv)
        for cp in copies:
            cp.wait_send()
        for cp in copies:
            cp.wait_recv()

    both = [*sums, *lands]
    res = pl.pallas_call(
        body, name=name, out_shape=tuple(pltpu.HBM(b.shape, b.dtype) for b in both),
        in_specs=[*[_HBM] * (2 * n), _SEM, _SEM, _ANY], out_specs=tuple([_HBM] * (2 * n)),
        input_output_aliases={t: t for t in range(2 * n)},
        compiler_params=pltpu.CompilerParams(has_side_effects=_EFFECT),
    )(*both, send_sem, recv_sem, after)
    return list(res[:n]), list(res[n:])


def _rs_join(name, totals, layers):
    n = len(totals)

    def body(*refs):
        outs, send, recv = refs[n:2 * n], refs[2 * n], refs[2 * n + 1]
        x, y, c, _, _ = _place()
        copies, arrive = [], []
        for t in range(n):
            rh = outs[t].shape[1] // 2
            mine = outs[t].at[layers[t], pl.ds(c * rh, rh)]
            landing = outs[t].at[layers[t], pl.ds((1 - c) * rh, rh)]
            copies.append(_remote(mine, mine, send.at[t], recv.at[t], (x, y, 1 - c)))
            arrive.append(_remote(landing, landing, send.at[t], recv.at[t], (x, y, 1 - c)))
        for cp in copies:
            cp.start()
        for cp in arrive:
            cp.wait_recv()
        for cp in copies:
            cp.wait_send()

    return pl.pallas_call(
        body, name=name, in_specs=[_ANY] * n, out_specs=[_ANY] * n,
        out_shape=[jax.ShapeDtypeStruct(b.shape, b.dtype) for b in totals],
        scratch_shapes=[pltpu.SemaphoreType.DMA((n,)), pltpu.SemaphoreType.DMA((n,))],
        input_output_aliases={t: t for t in range(n)},
    )(*totals)


def _pair_sum(g, got, where):
    _, r, cols = g.shape
    rh = r // 2
    tm = _row_tile(rh)
    per, perh = r // tm, rh // tm
    out = _placed_map("rs_add_pair", lambda u, w: u.astype(_F32) + w.astype(_F32), where,
                      [(_rows2d(g), lambda i, wh: (i // perh) * per + wh[0] * perh + i % perh),
                       (_rows2d(got), lambda i, wh: i)],
                      4 * rh, cols, _BF, lambda i, wh: i, 4 * perh, tm)
    return out.reshape(4, rh, cols)


def _sum_of_four(pair, land, where, total, layer, n_layers):
    _, rh, cols = pair.shape
    r = 2 * rh
    tm = _row_tile(rh)
    per, perh = r // tm, rh // tm

    def add_four(own, others):
        return ((own.astype(_F32) + others[0].astype(_F32)) + others[1].astype(_F32)) + others[2].astype(_F32)

    out = _placed_map("rs_add_four", add_four, where,
                      [(_rows2d(pair), lambda i, wh: wh[1] * perh + i), (land, lambda i, wh: i)],
                      n_layers * r, cols, _F32, lambda i, wh: layer * per + wh[0] * perh + i, perh, tm,
                      into=None if total is None else _rows2d(total))
    return out.reshape(n_layers, r, cols)


def _all_reduce_small(pack):
    rows, cols = pack.shape

    def body(p_ref, o_ref, land, send, recv):
        x, y, c = lax.axis_index("x"), lax.axis_index("y"), lax.axis_index("c")
        me = 4 * x + 2 * y + c
        land[me] = p_ref[...]
        copies = []
        for k in range(1, 8):
            bx, by, bc = (k >> 2) & 1, (k >> 1) & 1, k & 1
            peer = (1 - x if bx else x, 1 - y if by else y, 1 - c if bc else c)
            cp = _remote(p_ref, land.at[me], send.at[k - 1], recv.at[k - 1], peer)
            cp.start()
            copies.append((cp, 4 * peer[0] + 2 * peer[1] + peer[2]))
        for k, (cp, pid) in enumerate(copies):
            _remote(p_ref, land.at[pid], send.at[k], recv.at[k], (x, y, c)).wait_recv()
        for cp, _ in copies:
            cp.wait_send()
        acc = land[0]
        for d in range(1, 8):
            acc = acc + land[d]
        o_ref[...] = acc

    vm = pl.BlockSpec(memory_space=pltpu.VMEM)
    return pl.pallas_call(
        body, name="all_reduce_small", in_specs=[vm], out_specs=vm,
        out_shape=jax.ShapeDtypeStruct((rows, cols), _F32),
        scratch_shapes=[pltpu.VMEM((8, rows, cols), _F32), pltpu.SemaphoreType.DMA((7,)), pltpu.SemaphoreType.DMA((7,))],
    )(pack)


def _adamw_math(w, g, m, v):
    m2 = _ADAM_B1 * m + (1.0 - _ADAM_B1) * g
    v2 = _ADAM_B2 * v + (1.0 - _ADAM_B2) * (g * g)
    m_hat = m2 / (1.0 - _ADAM_B1 ** _ADAM_STEP)
    v_hat = v2 / (1.0 - _ADAM_B2 ** _ADAM_STEP)
    delta = -_ADAM_LR * (m_hat / (jnp.sqrt(v_hat) + _ADAM_EPS) + _ADAM_WD * w)
    return delta, m2, v2


def _adamw(w, g, m, v, lo=0, hi=None, into=None):
    shape = w.shape
    n_l = shape[0] if w.ndim == 3 else 1
    hi = n_l if hi is None else hi
    w2, g2, m2, v2 = (_rows2d(t) for t in (w, g, m, v))
    rows, cols = w2.shape
    per_layer = rows // n_l
    tm = _row_tile(per_layer)
    first, steps = lo * per_layer // tm, (hi - lo) * per_layer // tm
    n_into = 0 if into is None else 3

    def body(*refs):
        outs = refs[4 + n_into:]
        for ref, val in zip(outs, _adamw_math(*[r[...] for r in refs[:4]])):
            ref[...] = val

    spec = pl.BlockSpec((tm, cols), lambda i: (first + i, 0))
    sds = jax.ShapeDtypeStruct((rows, cols), _F32)
    outs = pl.pallas_call(
        body, name="adamw", grid=(steps,), in_specs=[spec] * 4 + [_ANY] * n_into, out_specs=[spec] * 3,
        out_shape=[sds] * 3, input_output_aliases={4 + k: k for k in range(n_into)},
        compiler_params=_params(("arbitrary",)),
    )(w2, g2, m2, v2, *([] if into is None else [_rows2d(t) for t in into]))
    return tuple(o.reshape(shape) for o in outs)


def kernel(x, positions, ffn1_norm, ffn1_w_in, ffn1_w_out, mix_norm, ffn2_norm, ffn2_w_in, ffn2_w_out, hyb_w_in, hyb_dw_w, hyb_dw_b, hyb_ln_g, hyb_ln_b, hyb_w_out, gdn_w_in, gdn_conv_w, gdn_A_log, gdn_dt_bias, gdn_norm_g, gdn_w_out, final_norm, loss_target, m_ffn1_norm, m_ffn1_w_in, m_ffn1_w_out, m_mix_norm, m_ffn2_norm, m_ffn2_w_in, m_ffn2_w_out, m_hyb_w_in, m_hyb_dw_w, m_hyb_dw_b, m_hyb_ln_g, m_hyb_ln_b, m_hyb_w_out, m_gdn_w_in, m_gdn_conv_w, m_gdn_A_log, m_gdn_dt_bias, m_gdn_norm_g, m_gdn_w_out, m_final_norm, v_ffn1_norm, v_ffn1_w_in, v_ffn1_w_out, v_mix_norm, v_ffn2_norm, v_ffn2_w_in, v_ffn2_w_out, v_hyb_w_in, v_hyb_dw_w, v_hyb_dw_b, v_hyb_ln_g, v_hyb_ln_b, v_hyb_w_out, v_gdn_w_in, v_gdn_conv_w, v_gdn_A_log, v_gdn_dt_bias, v_gdn_norm_g, v_gdn_w_out, v_final_norm):
    w = dict(ffn1_norm=ffn1_norm, ffn1_w_in=ffn1_w_in, ffn1_w_out=ffn1_w_out, mix_norm=mix_norm, ffn2_norm=ffn2_norm, ffn2_w_in=ffn2_w_in, ffn2_w_out=ffn2_w_out, hyb_w_in=hyb_w_in, hyb_dw_w=hyb_dw_w, hyb_dw_b=hyb_dw_b, hyb_ln_g=hyb_ln_g, hyb_ln_b=hyb_ln_b, hyb_w_out=hyb_w_out, gdn_w_in=gdn_w_in, gdn_conv_w=gdn_conv_w, gdn_A_log=gdn_A_log, gdn_dt_bias=gdn_dt_bias, gdn_norm_g=gdn_norm_g, gdn_w_out=gdn_w_out, final_norm=final_norm)
    m = dict(ffn1_norm=m_ffn1_norm, ffn1_w_in=m_ffn1_w_in, ffn1_w_out=m_ffn1_w_out, mix_norm=m_mix_norm, ffn2_norm=m_ffn2_norm, ffn2_w_in=m_ffn2_w_in, ffn2_w_out=m_ffn2_w_out, hyb_w_in=m_hyb_w_in, hyb_dw_w=m_hyb_dw_w, hyb_dw_b=m_hyb_dw_b, hyb_ln_g=m_hyb_ln_g, hyb_ln_b=m_hyb_ln_b, hyb_w_out=m_hyb_w_out, gdn_w_in=m_gdn_w_in, gdn_conv_w=m_gdn_conv_w, gdn_A_log=m_gdn_A_log, gdn_dt_bias=m_gdn_dt_bias, gdn_norm_g=m_gdn_norm_g, gdn_w_out=m_gdn_w_out, final_norm=m_final_norm)
    v = dict(ffn1_norm=v_ffn1_norm, ffn1_w_in=v_ffn1_w_in, ffn1_w_out=v_ffn1_w_out, mix_norm=v_mix_norm, ffn2_norm=v_ffn2_norm, ffn2_w_in=v_ffn2_w_in, ffn2_w_out=v_ffn2_w_out, hyb_w_in=v_hyb_w_in, hyb_dw_w=v_hyb_dw_w, hyb_dw_b=v_hyb_dw_b, hyb_ln_g=v_hyb_ln_g, hyb_ln_b=v_hyb_ln_b, hyb_w_out=v_hyb_w_out, gdn_w_in=v_gdn_w_in, gdn_conv_w=v_gdn_conv_w, gdn_A_log=v_gdn_A_log, gdn_dt_bias=v_gdn_dt_bias, gdn_norm_g=v_gdn_norm_g, gdn_w_out=v_gdn_w_out, final_norm=v_final_norm)
    return _train_step(x, positions, loss_target, w, m, v)


_WEIGHT_ORDER = ("ffn1_norm", "ffn1_w_in", "ffn1_w_out", "mix_norm", "ffn2_norm", "ffn2_w_in", "ffn2_w_out",
                 "hyb_w_in", "hyb_dw_w", "hyb_dw_b", "hyb_ln_g", "hyb_ln_b", "hyb_w_out", "gdn_w_in", "gdn_conv_w",
                 "gdn_A_log", "gdn_dt_bias", "gdn_norm_g", "gdn_w_out", "final_norm")
_BIG = ("ffn1_w_in", "ffn1_w_out", "ffn2_w_in", "ffn2_w_out", "hyb_w_in", "hyb_w_out", "gdn_w_in", "gdn_w_out")
_PACK_COLS = 1024


def _lane16(vec):
    return jnp.zeros((1, 128), _F32).at[0, _GDN_HEADS:2 * _GDN_HEADS].set(vec)


def _pack_rows(parts):
    rows, where, at = [], [], 0
    for p in parts:
        flat = p.reshape(-1).astype(_F32)
        n = -(-flat.shape[0] // (8 * _PACK_COLS)) * 8
        rows.append(jnp.pad(flat, (0, n * _PACK_COLS - flat.shape[0])).reshape(n, _PACK_COLS))
        where.append((at, n, flat.shape[0], p.shape))
        at += n
    return jnp.concatenate(rows, axis=0), where


def _unpack_rows(pack, where):
    return [pack[at:at + n].reshape(-1)[:size].reshape(shape) for at, n, size, shape in where]


def _train_step(x, positions, loss_target, w, m, v):
    n_layers = w["ffn1_norm"].shape[0]
    n_hyb, n_gdn = w["hyb_w_in"].shape[0], w["gdn_w_in"].shape[0]
    d_model = x.shape[-1]
    xh, target = x[0], loss_target[0]
    chip = 2 * lax.axis_index("x") + lax.axis_index("y")
    where = jnp.stack([lax.axis_index("c"), chip]).astype(jnp.int32)

    def own_columns(shard):
        cols = shard.shape[-1]
        zeros = jnp.zeros(shard.shape[:-1] + (4 * cols,), _F32)
        return lax.dynamic_update_slice_in_dim(zeros, shard, chip * cols, axis=2)

    small_pack, small_where = _pack_rows([own_columns(w["hyb_dw_w"]), own_columns(w["gdn_conv_w"])])
    small_full = _all_reduce_small(small_pack)
    dw_w, conv_w = _unpack_rows(0.5 * small_full, small_where)
    dw_w = jnp.pad(dw_w, ((0, 0), (0, _HALO - _CONV_WIDTH), (0, 0)))

    groups = (("ffn1_w_in", "ffn1_w_out"), ("hyb_w_in", "hyb_w_out"), ("ffn2_w_in", "ffn2_w_out"),
              ("gdn_w_in", "gdn_w_out"))
    steps = [(0, (0, 2)), (1, (0, 1)), (0, (1, 2)), (2, (0, 1)), (3, (0, 1))]
    bufs = [[_cast_slab(n, w[n], where) for n in names] for names in groups]
    pending, token = [], small_full
    for si, (gi, part) in enumerate(steps):
        send, recv, bufs[gi], token = _gather_start(f"gather_start_{si}", bufs[gi], token, part)
        pending.append((send, recv))
    gdn_cols = w["gdn_w_in"].shape[-1]
    n_qkvz = 4 * _GDN_HEADS * _GDN_DIM
    n_ba = 4 * gdn_cols - n_qkvz
    wv = {}

    def arrive(si, after):
        gi, part = steps[si]
        send, recv = pending[si]
        bufs[gi] = _gather_wait(f"gather_wait_{si}", send, recv, bufs[gi], after, part)
        bufs[gi] = _gather_forward(f"gather_pass_{si}", bufs[gi], part)
        a, b = bufs[gi]
        kind = groups[gi][0]
        if kind == "gdn_w_in":
            gdn_in = jnp.transpose(a, (0, 2, 1, 3)).reshape(n_gdn, d_model, 4 * gdn_cols)
            wv["gdn_qkvz"] = gdn_in[:, :, :n_qkvz]
            wv["gdn_ba"] = jnp.pad(gdn_in[:, :, n_qkvz:], ((0, 0), (0, 0), (0, 128 - n_ba)))
        else:
            wv[kind] = a
        wv[groups[gi][1]] = b.reshape(b.shape[0], -1, d_model)

    tabs = _rotary_tables(positions)

    def mixer_args(layer):
        i = layer // 2
        if layer % 2 == 0:
            return (wv["hyb_w_in"], wv["hyb_w_out"], dw_w[i], w["hyb_dw_b"][i][None], w["hyb_ln_g"][i][None],
                    w["hyb_ln_b"][i][None], tabs, i)
        return (wv["gdn_qkvz"], wv["gdn_ba"], wv["gdn_w_out"], conv_w[i], _lane16(w["gdn_A_log"][i]),
                _lane16(w["gdn_dt_bias"][i]), w["gdn_norm_g"][i][None], i)

    h = xh
    saved = []
    arrive(0, token)
    for layer in range(n_layers):
        if layer == 1:
            arrive(2, h)
        h, s1 = _ffn_fwd(h, w["ffn1_norm"][layer][None], wv["ffn1_w_in"], wv["ffn1_w_out"], layer)
        gamma = w["mix_norm"][layer][None]
        if layer % 2 == 0:
            if layer == 0:
                arrive(1, h)
            h, sm = _hyb_fwd(h, gamma, *mixer_args(layer))
        else:
            if layer == 1:
                arrive(4, h)
            h, sm = _gdn_fwd(h, gamma, *mixer_args(layer))
        if layer == 0:
            arrive(3, h)
        h, s2 = _ffn_fwd(h, w["ffn2_norm"][layer][None], wv["ffn2_w_in"], wv["ffn2_w_out"], layer)
        saved.append((s1, sm, s2))
    dh, loss_row, d_final = _loss_head(h, w["final_norm"][None], target)

    d_norm = {k: [None] * n_layers for k in ("ffn1_norm", "mix_norm", "ffn2_norm")}
    d_hyb = [None] * n_hyb
    d_gdn = [None] * n_gdn
    grads = {name: None for name in _BIG}
    n_of = {name: w[name].shape[0] for name in _BIG}

    def rs_begin(layer, full, token):
        names = list(full)
        got = _rs_sibling(f"rs_sib_{layer}", [full[n] for n in names])
        sums = [_pair_sum(full[n], g, where) for n, g in zip(names, got)]
        send, recv, sums, lands, token = _rs_chips_start(f"rs_chips_start_{layer}", sums, token)
        return (layer, names, send, recv, sums, lands), token

    def rs_end(state, after):
        layer, names, send, recv, sums, lands = state
        sums, lands = _rs_chips_wait(f"rs_chips_wait_{layer}", send, recv, sums, lands, after)
        index = [layer if n.startswith("ffn") else layer // 2 for n in names]
        for n, li, pair, land in zip(names, index, sums, lands):
            grads[n] = _sum_of_four(pair, land, where, grads[n], li, n_of[n])
        joined = _rs_join(f"rs_join_{layer}", [grads[n] for n in names], index)
        grads.update(zip(names, joined))

    in_flight = None
    for layer in reversed(range(n_layers)):
        s1, sm, s2 = saved[layer]
        full = {}
        dh, d_norm["ffn2_norm"][layer], full["ffn2_w_in"], full["ffn2_w_out"] = _ffn_bwd(
            dh, s2, w["ffn2_norm"][layer][None], wv["ffn2_w_in"], wv["ffn2_w_out"], layer)
        gamma = w["mix_norm"][layer][None]
        if layer % 2 == 0:
            dh, d_norm["mix_norm"][layer], d_hyb[layer // 2], full["hyb_w_in"], full["hyb_w_out"] = _hyb_bwd(
                dh, sm, gamma, *mixer_args(layer))
        else:
            dh, d_norm["mix_norm"][layer], d_gdn[layer // 2], full["gdn_w_in"], full["gdn_w_out"] = _gdn_bwd(
                dh, sm, gamma, *mixer_args(layer), n_ba)
        dh, d_norm["ffn1_norm"][layer], full["ffn1_w_in"], full["ffn1_w_out"] = _ffn_bwd(
            dh, s1, w["ffn1_norm"][layer][None], wv["ffn1_w_in"], wv["ffn1_w_out"], layer)
        if in_flight is not None:
            rs_end(in_flight, dh)
        if layer == 0:
            stack = lambda rows: jnp.concatenate(rows, axis=0)
            misc = jnp.concatenate([stack([g[1] for g in d_gdn])[:, _GDN_HEADS:2 * _GDN_HEADS].reshape(-1),
                                    stack([g[2] for g in d_gdn])[:, _GDN_HEADS:2 * _GDN_HEADS].reshape(-1),
                                    stack([g[3] for g in d_gdn]).reshape(-1), loss_row[0, :1]])
            parts = [stack(d_norm["ffn1_norm"]), stack(d_norm["mix_norm"]), stack(d_norm["ffn2_norm"]), d_final,
                     stack([g[1] for g in d_hyb]), stack([g[2] for g in d_hyb]), stack([g[3] for g in d_hyb]), misc,
                     jnp.stack([g[0][:_CONV_WIDTH] for g in d_hyb]), jnp.stack([g[0] for g in d_gdn])]
            pack, pack_where = _pack_rows(parts)
            token = small_sum = _all_reduce_small(pack)
        in_flight, token = rs_begin(layer, full, token)
        dh, token = lax.optimization_barrier((dh, token))
    grad_x = dh[None]

    red = _unpack_rows(small_sum, pack_where)
    n_a = n_gdn * _GDN_HEADS
    misc = red[7]
    cw = w["hyb_dw_w"].shape[-1]
    gw = w["gdn_conv_w"].shape[-1]
    grads.update(
        ffn1_norm=red[0], mix_norm=red[1], ffn2_norm=red[2], final_norm=red[3].reshape(-1),
        hyb_dw_b=red[4], hyb_ln_g=red[5], hyb_ln_b=red[6],
        gdn_A_log=misc[:n_a].reshape(n_gdn, _GDN_HEADS), gdn_dt_bias=misc[n_a:2 * n_a].reshape(n_gdn, _GDN_HEADS),
        gdn_norm_g=misc[2 * n_a:2 * n_a + n_gdn * _GDN_DIM].reshape(n_gdn, _GDN_DIM),
        hyb_dw_w=lax.dynamic_slice_in_dim(red[8], chip * cw, cw, axis=2),
        gdn_conv_w=lax.dynamic_slice_in_dim(red[9], chip * gw, gw, axis=2))
    loss = misc[2 * n_a + n_gdn * _GDN_DIM]

    small = [n for n in _WEIGHT_ORDER if n not in _BIG]
    packs = [_pack_rows([src[n] for n in small]) for src in (w, grads, m, v)]
    outs = _adamw(*[p[0] for p in packs])
    delta, new_m, new_v = {}, {}, {}
    for res, o in zip((delta, new_m, new_v), outs):
        res.update(dict(zip(small, _unpack_rows(o, packs[0][1]))))
    late = in_flight[1]
    early = {name: _adamw(w[name], grads[name], m[name], v[name], lo=1 if name in late else 0) for name in _BIG}
    early, outs, token = lax.optimization_barrier((early, outs, token))
    rs_end(in_flight, token)
    for name in _BIG:
        res = early[name]
        if name in late:
            res = _adamw(w[name], grads[name], m[name], v[name], lo=0, hi=1, into=res)
        delta[name], new_m[name], new_v[name] = res
    return (loss, grad_x, *[grads[n] for n in _WEIGHT_ORDER], *[delta[n] for n in _WEIGHT_ORDER],
            *[new_m[n] for n in _WEIGHT_ORDER], *[new_v[n] for n in _WEIGHT_ORDER])
```

```python
import functools

import jax
import jax.numpy as jnp
from jax import lax
from jax.experimental import pallas as pl
from jax.experimental.pallas import tpu as pltpu

_BF = jnp.bfloat16
_F32 = jnp.float32
_NORM_EPS = 1e-6
_VMEM_LIMIT_BYTES = 56 * 1024 * 1024

_A_HEADS = 8
_A_HEAD_DIM = 64
_WIN_BLOCK = 128
_DILATIONS = (1, 4, 16)
_CONV_WIDTH = 31
_GDN_HEADS = 8
_GDN_DIM = 128
_GDN_CHUNK = 64
_GDN_SHORT_CONV = 4

_ADAM_LR = 0.001
_ADAM_B1 = 0.9
_ADAM_B2 = 0.999
_ADAM_EPS = 1e-08
_ADAM_WD = 0.01
_ADAM_STEP = 10


def _params(sem):
    return pltpu.CompilerParams(dimension_semantics=sem, vmem_limit_bytes=_VMEM_LIMIT_BYTES)


def _dot(a, b, dn=(((1,), (0,)), ((), ()))):
    return lax.dot_general(a.astype(_BF), b.astype(_BF), dn, preferred_element_type=_F32)


def _dot_nt(a, b):
    return _dot(a, b, (((1,), (1,)), ((), ())))


def _dot_tn(a, b):
    return _dot(a, b, (((0,), (0,)), ((), ())))


_NN = (((1,), (0,)), ((), ()))
_NT = (((1,), (1,)), ((), ()))
_TN = (((0,), (0,)), ((), ()))
_ANY = pl.BlockSpec(memory_space=pl.ANY)


def _mm(name, a, b, *, grid, a_spec, b_spec, dn, out_shape, out_spec, red=None, acc_shape=None,
        pre_a=None, pre_b=None, epi=None, extras=(), extra_specs=(), sum_shape=None):
    n_ex = len(extras)
    n_in = 2 + n_ex
    nk = grid[red] if red is not None else 1
    n_out = 1 if sum_shape is None else 2

    def body(*refs):
        a_ref, b_ref = refs[0], refs[1]
        ex = refs[2:2 + n_ex]
        o_ref = refs[n_in]
        av, bv = a_ref[...], b_ref[...]
        if pre_a is not None:
            av = pre_a(av)
        if pre_b is not None:
            bv = pre_b(bv)
        part = lax.dot_general(av.astype(_BF), bv.astype(_BF), dn, preferred_element_type=_F32)

        def finish(acc):
            res = epi(acc, *[e[...] for e in ex]) if epi is not None else acc
            if sum_shape is not None:
                res, term = res
                s_ref = refs[n_in + 1]
                first = functools.reduce(jnp.logical_and, [pl.program_id(ax) == 0 for ax in range(len(grid)) if ax != red])

                @pl.when(first)
                def _():
                    s_ref[...] = jnp.zeros_like(s_ref)

                s_ref[...] += term
            if isinstance(res, tuple):
                for idx, val in enumerate(res):
                    o_ref[idx] = val.astype(o_ref.dtype)
            else:
                o_ref[...] = res.astype(o_ref.dtype)

        if red is None:
            finish(part)
        else:
            acc_ref = refs[n_in + n_out]
            k = pl.program_id(red)

            @pl.when(k == 0)
            def _():
                acc_ref[...] = part

            @pl.when(k > 0)
            def _():
                acc_ref[...] += part

            @pl.when(k == nk - 1)
            def _():
                finish(acc_ref[...])

    if sum_shape is not None:
        out_shape = [out_shape, jax.ShapeDtypeStruct(sum_shape, _F32)]
        out_spec = [out_spec, pl.BlockSpec(sum_shape, lambda *g: (0,) * len(sum_shape))]
    return pl.pallas_call(
        body, name=name, grid=grid, in_specs=[a_spec, b_spec, *extra_specs], out_specs=out_spec, out_shape=out_shape,
        scratch_shapes=[pltpu.VMEM(acc_shape, _F32)] if red is not None else [],
        compiler_params=_params(("arbitrary",) * len(grid)),
    )(a, b, *extras)


def _rowmap(name, body, tiled, params, tile_outs, param_outs=(), *, rows, tm, ncol=1, col_outer=False,
            into=None):
    nt, npar, nto, npo = len(tiled), len(params), len(tile_outs), len(param_outs)
    assert rows % tm == 0, (name, rows, tm)
    nrow = rows // tm
    grid = (ncol, nrow) if col_outer else (nrow, ncol)

    def ij(g):
        return (g[1], g[0]) if col_outer else (g[0], g[1])

    def cidx(col, j):
        return col(j) if callable(col) else col

    in_specs, operands = [], []
    for arr, width, col in tiled:
        operands.append(arr)
        if arr.ndim == 3:
            in_specs.append(pl.BlockSpec((arr.shape[0], tm, width),
                                         lambda *g, col=col: (0, ij(g)[0], cidx(col, ij(g)[1]))))
        else:
            in_specs.append(pl.BlockSpec((tm, width), lambda *g, col=col: (ij(g)[0], cidx(col, ij(g)[1]))))
    for p in params:
        if isinstance(p, tuple):
            arr, width, col = p
            operands.append(arr)
            in_specs.append(pl.BlockSpec((arr.shape[0], width), lambda *g, col=col: (0, cidx(col, ij(g)[1]))))
        else:
            operands.append(p)
            in_specs.append(pl.BlockSpec(p.shape, lambda *g, nd=p.ndim: (0,) * nd))
    out_shape, out_specs = [], []
    for total, width, col, dt in tile_outs:
        out_shape.append(jax.ShapeDtypeStruct((rows, total), dt))
        out_specs.append(pl.BlockSpec((tm, width), lambda *g, col=col: (ij(g)[0], cidx(col, ij(g)[1]))))
    for shp in param_outs:
        out_shape.append(jax.ShapeDtypeStruct(shp, _F32))
        out_specs.append(pl.BlockSpec(shp, lambda *g, nd=len(shp): (0,) * nd))
    n_in = nt + npar
    aliases = {}
    if into is not None:
        operands.append(into[0])
        in_specs.append(_ANY)
        aliases = {n_in: into[1]}
        n_in += 1

    def kern(*refs):
        vals = [r[...] for r in refs[:nt + npar]]
        touts = refs[n_in:n_in + nto]
        pouts = refs[n_in + nto:]
        res = body(*vals)
        if not isinstance(res, (tuple, list)):
            res = (res,)
        for r, val in zip(touts, res[:nto]):
            r[...] = val.astype(r.dtype)
        if npo:
            @pl.when((pl.program_id(0) == 0) & (pl.program_id(1) == 0))
            def _():
                for r in pouts:
                    r[...] = jnp.zeros_like(r)

            for r, val in zip(pouts, res[nto:]):
                r[...] += val

    return pl.pallas_call(
        kern, name=name, grid=grid, in_specs=in_specs, out_specs=out_specs, out_shape=out_shape,
        input_output_aliases=aliases, compiler_params=_params(("arbitrary", "arbitrary")),
    )(*operands)


def _vjp_body(fn, n_in, diff, n_ct):
    def body(*vals):
        ins, cts = vals[:n_in], vals[n_in:n_in + n_ct]
        outs, pull = jax.vjp(fn, *ins)
        single = not isinstance(outs, (tuple, list))
        if single:
            grads = pull(cts[0].astype(outs.dtype))
        else:
            grads = pull(tuple(c.astype(o.dtype) for c, o in zip(cts, outs)))
        return tuple(grads[i] for i in diff)
    return body


_TM = 512
_TM_WIDE = 1024


def _rms(h, gamma):
    hf = h.astype(_F32)
    return hf * lax.rsqrt(jnp.mean(hf * hf, axis=-1, keepdims=True) + _NORM_EPS) * gamma


def _rmsnorm_fwd(h, gamma):
    s, d = h.shape
    return _rowmap("rmsnorm_fwd", lambda hv, gv: _rms(hv, gv).astype(_BF), [(h, d, 0)], [gamma],
                   [(d, d, 0, _BF)], rows=s, tm=_TM)[0]


def _norm_bwd_epi(dxn, hv, dhv, gv):
    _, pull = jax.vjp(_rms, hv, gv)
    dh, dg = pull(dxn)
    return dhv + dh, dg


def _norm_bwd_args(h, dh_out, gamma, tm):
    d = h.shape[1]
    row = pl.BlockSpec((tm, d), lambda i, j: (i, 0))
    return dict(extras=(h, dh_out, gamma), extra_specs=(row, row, pl.BlockSpec((1, d), lambda i, j: (0, 0))),
                sum_shape=(1, d))


def _silu(x):
    return x * jax.nn.sigmoid(x)


def _ffn_up(h, gamma, w_in, layer):
    s, d = h.shape
    ns = w_in.shape[-1]
    tm = _CONV_TM

    def body(h_ref, gam_ref, w_ref, x_ref, g_ref, u_ref, a_ref):
        xv = _rms(h_ref[...], gam_ref[...]).astype(_BF)
        x_ref[...] = xv
        for half in range(2):
            cols = slice(half * ns, (half + 1) * ns)
            g = jnp.dot(xv, w_ref[half], preferred_element_type=_F32)
            u = jnp.dot(xv, w_ref[half + 2], preferred_element_type=_F32)
            g_ref[:, cols] = g.astype(_BF)
            u_ref[:, cols] = u.astype(_BF)
            a_ref[:, cols] = (_silu(g) * u).astype(_BF)

    row = pl.BlockSpec((tm, d), lambda i: (i, 0))
    out = pl.BlockSpec((tm, 2 * ns), lambda i: (i, 0))
    sds = jax.ShapeDtypeStruct((s, 2 * ns), _BF)
    return pl.pallas_call(
        body, name="ffn_up", grid=(s // tm,),
        in_specs=[row, pl.BlockSpec((1, d), lambda i: (0, 0)), pl.BlockSpec((None, 4, d, ns), lambda i: (layer, 0, 0, 0))],
        out_specs=[row, out, out, out], out_shape=[jax.ShapeDtypeStruct((s, d), _BF), sds, sds, sds],
        compiler_params=_params(("arbitrary",)),
    )(h, gamma, w_in)


def _proj_residual(name, y, w, layer, h, scale):
    s, k = y.shape
    d = w.shape[-1]
    return _mm(name, y, w, grid=(s // _TM,), a_spec=pl.BlockSpec((_TM, k), lambda i: (i, 0)),
               b_spec=pl.BlockSpec((None, k, d), lambda i: (layer, 0, 0)), dn=_NN,
               out_shape=jax.ShapeDtypeStruct((s, d), _F32), out_spec=pl.BlockSpec((_TM, d), lambda i: (i, 0)),
               epi=lambda acc, hv: hv + scale * acc, extras=(h,),
               extra_specs=(pl.BlockSpec((_TM, d), lambda i: (i, 0)),))


def _ffn_dact(dh, w_out, layer, g, u):
    s, d = dh.shape
    f = g.shape[1]
    ns = f // 2

    def epi(da, gv, uv):
        gf, uf = gv.astype(_F32), uv.astype(_F32)
        sg = jax.nn.sigmoid(gf)
        return da * uf * (sg * (1.0 + gf * (1.0 - sg))), da * gf * sg

    tm = _CONV_TM
    gu = pl.BlockSpec((tm, f), lambda i: (i, 0))
    return _mm("ffn_dact", dh, w_out, grid=(s // tm,), a_spec=pl.BlockSpec((tm, d), lambda i: (i, 0)),
               b_spec=pl.BlockSpec((None, f, d), lambda i: (layer, 0, 0)), dn=_NT,
               out_shape=jax.ShapeDtypeStruct((2, s, f), _BF), out_spec=pl.BlockSpec((2, tm, f), lambda i: (0, i, 0)),
               pre_a=lambda v: 0.5 * v, epi=epi, extras=(g, u), extra_specs=(gu, gu))


def _wgrad_rows(name, y, dh, scale):
    s, k = y.shape
    d = dh.shape[1]
    out = _mm(name, y, dh, grid=(s // _TM,), a_spec=pl.BlockSpec((_TM, k), lambda i: (i, 0)),
              b_spec=pl.BlockSpec((_TM, d), lambda i: (i, 0)), dn=_TN, red=0, acc_shape=(k, d),
              out_shape=jax.ShapeDtypeStruct((k, d), _BF), out_spec=pl.BlockSpec((k, d), lambda i: (0, 0)),
              pre_b=(lambda v: scale * v) if scale != 1.0 else None)
    return out.reshape(4, k // 4, d)


def _ffn_dxn(dz, w_in, layer, h, gamma, dh_out):
    _, s, f = dz.shape
    d, ns = w_in.shape[2], w_in.shape[3]
    tm = _TM_WIDE
    return _mm("ffn_dxn", dz, w_in, grid=(s // tm, 4),
               a_spec=pl.BlockSpec((None, tm, ns), lambda i, j: (j // 2, i, j % 2)),
               b_spec=pl.BlockSpec((None, None, d, ns), lambda i, j: (layer, j, 0, 0)), dn=_NT, red=1,
               acc_shape=(tm, d), out_shape=jax.ShapeDtypeStruct((s, d), _F32),
               out_spec=pl.BlockSpec((tm, d), lambda i, j: (i, 0)), epi=_norm_bwd_epi,
               **_norm_bwd_args(h, dh_out, gamma, tm))


def _ffn_dwin(xn, dz):
    s, d = xn.shape
    ns = dz.shape[2] // 2
    tm = _TM_WIDE
    return _mm("ffn_dwin", xn, dz, grid=(4, s // tm), a_spec=pl.BlockSpec((tm, d), lambda j, i: (i, 0)),
               b_spec=pl.BlockSpec((None, tm, ns), lambda j, i: (j // 2, i, j % 2)), dn=_TN, red=1,
               acc_shape=(d, ns), out_shape=jax.ShapeDtypeStruct((4, d, ns), _BF),
               out_spec=pl.BlockSpec((None, d, ns), lambda j, i: (j, 0, 0)))


def _ffn_fwd(h, gamma, w_in, w_out, layer):
    xn, g, u, act = _ffn_up(h, gamma, w_in, layer)
    h_new = _proj_residual("ffn_down", act, w_out, layer, h, 0.5)
    return h_new, (h, xn, g, u, act)


def _ffn_bwd(dh, saved, gamma, w_in, w_out, layer):
    h, xn, g, u, act = saved
    dz = _ffn_dact(dh, w_out, layer, g, u)
    d_w_out = _wgrad_rows("ffn_dwout", act, dh, 0.5)
    dh_in, dgamma = _ffn_dxn(dz, w_in, layer, h, gamma, dh)
    d_w_in = _ffn_dwin(xn, dz)
    return dh_in, dgamma, d_w_in, d_w_out


def _loss_head(h, gamma, target):
    s, d = h.shape

    def body(hv, tv, gv):
        y, pull = jax.vjp(_rms, hv, gv)
        err = y - tv
        loss = 0.5 * jnp.sum(jnp.mean(err * err, axis=-1, keepdims=True), axis=0, keepdims=True)
        dh, dg = pull(err * (1.0 / d))
        return dh, jnp.broadcast_to(loss, (1, 128)), dg

    return _rowmap("loss_head", body, [(h, d, 0), (target, d, 0)], [gamma], [(d, d, 0, _F32)],
                   [(1, 128), (1, d)], rows=s, tm=_TM)


_ROT = _A_HEAD_DIM // 4
_ROPE_THETA = 500000.0
_A_WIDTH = _A_HEADS * _A_HEAD_DIM
_NEG = -1e30


def _rotary_tables(positions):
    inv_freq = jnp.power(jnp.float32(_ROPE_THETA), -jnp.arange(0, _ROT, 2, dtype=_F32) / _ROT)
    ang = positions.reshape(-1, 1).astype(_F32) * inv_freq
    cos, sin = jnp.cos(ang), jnp.sin(ang)
    s = ang.shape[0]
    pad = jnp.zeros((s, _A_HEAD_DIM - _ROT), _F32)
    c_head = jnp.concatenate([cos, cos, pad + 1.0], axis=1)
    s_head = jnp.concatenate([-sin, sin, pad], axis=1)
    c2, s2 = jnp.tile(c_head, (1, 2)), jnp.tile(s_head, (1, 2))
    scale = _A_HEAD_DIM ** -0.5
    tab_c = jnp.stack([scale * c2, c2, jnp.ones_like(c2)])
    tab_s = jnp.stack([scale * s2, s2, jnp.zeros_like(s2)])
    return tab_c, tab_s


def _pair_swap(z):
    lane = lax.broadcasted_iota(jnp.int32, z.shape, 1) % _A_HEAD_DIM
    half = _ROT // 2
    up = jnp.where(lane < _ROT, pltpu.roll(z, half, axis=1), 0.0)
    return jnp.where(lane < half, pltpu.roll(z, 128 - half, axis=1), up)


def _rot_apply(x, c, s, transpose):
    outs = []
    for k in range(x.shape[1] // 128):
        xs = x[:, k * 128:(k + 1) * 128]
        outs.append(c * xs + (_pair_swap(s * xs) if transpose else s * _pair_swap(xs)))
    return jnp.concatenate(outs, axis=1)


def _qkv_prep(proj, tab_c, tab_s):
    s = proj.shape[0]
    w = _A_WIDTH

    def body(x_ref, c_ref, s_ref, o_ref):
        o_ref[...] = _rot_apply(x_ref[...], c_ref[...], s_ref[...], False).astype(_BF)

    tab = pl.BlockSpec((None, _TM, 128), lambda i, j: (j, i, 0))
    blk = pl.BlockSpec((_TM, w), lambda i, j: (i, j))
    return pl.pallas_call(
        body, name="qkv_prep", grid=(s // _TM, 3), in_specs=[blk, tab, tab], out_specs=blk,
        out_shape=jax.ShapeDtypeStruct((s, 3 * w), _BF), compiler_params=_params(("arbitrary", "arbitrary")),
    )(proj, tab_c, tab_s)


def _head_masks():
    lane = lax.broadcasted_iota(jnp.int32, (_WIN_BLOCK, 128), 1)
    return [lane // _A_HEAD_DIM == hh for hh in range(2)]


def _band_masks(first_block):
    qi = lax.broadcasted_iota(jnp.int32, (_WIN_BLOCK, _WIN_BLOCK), 0)
    kj = lax.broadcasted_iota(jnp.int32, (_WIN_BLOCK, _WIN_BLOCK), 1)
    return (kj >= qi) & jnp.logical_not(first_block), kj <= qi


def _attn_fwd(qkv, dil):
    s = qkv.shape[0]
    w = _A_WIDTH
    length = s // dil
    nb = length // _WIN_BLOCK
    view = qkv.reshape(length, dil * 3 * w)

    def body(q_ref, kp_ref, kc_ref, vp_ref, vc_ref, o_ref, l_ref):
        allow_p, allow_c = _band_masks(pl.program_id(1) == 0)
        hm = _head_masks()
        for sl in range(w // 128):
            cols = slice(sl * 128, (sl + 1) * 128)
            q, kp, kc, vp, vc = q_ref[:, cols], kp_ref[:, cols], kc_ref[:, cols], vp_ref[:, cols], vc_ref[:, cols]
            o_s = jnp.zeros((_WIN_BLOCK, 128), _F32)
            l_s = jnp.zeros((_WIN_BLOCK, 128), _F32)
            for hh in range(2):
                qh = jnp.where(hm[hh], q, jnp.zeros_like(q))
                sp = jnp.where(allow_p, _dot_nt(qh, kp), _NEG)
                sc = jnp.where(allow_c, _dot_nt(qh, kc), _NEG)
                m = jnp.maximum(jnp.max(sp, axis=1, keepdims=True), jnp.max(sc, axis=1, keepdims=True))
                pp, pc = jnp.exp(sp - m), jnp.exp(sc - m)
                den = jnp.sum(pp, axis=1, keepdims=True) + jnp.sum(pc, axis=1, keepdims=True)
                oh = (_dot(pp, vp) + _dot(pc, vc)) / den
                o_s = jnp.where(hm[hh], oh, o_s)
                l_s = jnp.where(hm[hh], m + jnp.log(den), l_s)
            o_ref[:, cols] = o_s.astype(o_ref.dtype)
            l_ref[:, cols] = l_s

    def at(col, prev):
        if prev:
            return pl.BlockSpec((_WIN_BLOCK, w), lambda r, n: (jnp.maximum(n - 1, 0), 3 * r + col))
        return pl.BlockSpec((_WIN_BLOCK, w), lambda r, n: (n, 3 * r + col))

    out = pl.BlockSpec((_WIN_BLOCK, w), lambda r, n: (n, r))
    sds = jax.ShapeDtypeStruct((length, dil * w), _F32)
    o, lse = pl.pallas_call(
        body, name=f"attn_fwd_d{dil}", grid=(dil, nb),
        in_specs=[at(0, False), at(1, True), at(1, False), at(2, True), at(2, False)],
        out_specs=[out, out], out_shape=[jax.ShapeDtypeStruct(sds.shape, _BF), sds],
        compiler_params=_params(("arbitrary", "arbitrary")),
    )(view, view, view, view, view)
    return o.reshape(s, w), lse.reshape(s, w)


def _attn_bwd(qkv, o, lse, do, dlse, dil):
    s = qkv.shape[0]
    w = _A_WIDTH
    length = s // dil
    nb = length // _WIN_BLOCK
    view = qkv.reshape(length, dil * 3 * w)
    o, lse, do, dlse = (t.reshape(length, dil * w) for t in (o, lse, do, dlse))

    def body(q_ref, kp_ref, kc_ref, vp_ref, vc_ref, o_ref, l_ref, do_ref, dl_ref, out_ref,
             car_q, car_k, car_v, cur_q, cur_kc, cur_vc, cur_kp, cur_vp):
        n = pl.program_id(1)

        @pl.when(n == 0)
        def _():
            car_q[...] = jnp.zeros_like(car_q)
            car_k[...] = jnp.zeros_like(car_k)
            car_v[...] = jnp.zeros_like(car_v)

        @pl.when(n == nb)
        def _():
            cur_kp[...] = jnp.zeros_like(cur_kp)
            cur_vp[...] = jnp.zeros_like(cur_vp)

        @pl.when(n < nb)
        def _():
            allow_p, allow_c = _band_masks(n == 0)
            hm = _head_masks()
            for sl in range(w // 128):
                cols = slice(sl * 128, (sl + 1) * 128)
                q, kp, kc, vp, vc = q_ref[:, cols], kp_ref[:, cols], kc_ref[:, cols], vp_ref[:, cols], vc_ref[:, cols]
                ov, dov = o_ref[:, cols].astype(_F32), do_ref[:, cols].astype(_F32)
                lv, dlv = l_ref[:, cols], dl_ref[:, cols]
                zero = jnp.zeros((_WIN_BLOCK, 128), _F32)
                dq_s, dkp_s, dkc_s, dvp_s, dvc_s = zero, zero, zero, zero, zero
                for hh in range(2):
                    qh = jnp.where(hm[hh], q, jnp.zeros_like(q))
                    doh = jnp.where(hm[hh], dov, 0.0).astype(_BF)
                    lse_h = jnp.sum(jnp.where(hm[hh], lv, 0.0), axis=1, keepdims=True) * (1.0 / _A_HEAD_DIM)
                    delta = jnp.sum(jnp.where(hm[hh], dov * ov, 0.0), axis=1, keepdims=True)
                    dl = jnp.sum(jnp.where(hm[hh], dlv, 0.0), axis=1, keepdims=True)
                    pp = jnp.exp(jnp.where(allow_p, _dot_nt(qh, kp), _NEG) - lse_h)
                    pc = jnp.exp(jnp.where(allow_c, _dot_nt(qh, kc), _NEG) - lse_h)
                    dsp = (pp * (_dot_nt(doh, vp) - delta + dl)).astype(_BF)
                    dsc = (pc * (_dot_nt(doh, vc) - delta + dl)).astype(_BF)
                    dq_s = dq_s + jnp.where(hm[hh], _dot(dsp, kp) + _dot(dsc, kc), 0.0)
                    dkp_s = dkp_s + _dot_tn(dsp, qh)
                    dkc_s = dkc_s + _dot_tn(dsc, qh)
                    dvp_s = dvp_s + _dot_tn(pp, doh)
                    dvc_s = dvc_s + _dot_tn(pc, doh)
                cur_q[:, cols] = dq_s
                cur_kp[:, cols] = dkp_s
                cur_kc[:, cols] = dkc_s
                cur_vp[:, cols] = dvp_s
                cur_vc[:, cols] = dvc_s

        out_ref[:, 0:w] = car_q[...].astype(out_ref.dtype)
        out_ref[:, w:2 * w] = (car_k[...] + cur_kp[...]).astype(out_ref.dtype)
        out_ref[:, 2 * w:3 * w] = (car_v[...] + cur_vp[...]).astype(out_ref.dtype)

        @pl.when(n < nb)
        def _():
            car_q[...] = cur_q[...]
            car_k[...] = cur_kc[...]
            car_v[...] = cur_vc[...]

    def at(col, prev):
        if prev:
            return pl.BlockSpec((_WIN_BLOCK, w), lambda r, n: (jnp.maximum(jnp.minimum(n, nb - 1) - 1, 0), 3 * r + col))
        return pl.BlockSpec((_WIN_BLOCK, w), lambda r, n: (jnp.minimum(n, nb - 1), 3 * r + col))

    blk = pl.BlockSpec((_WIN_BLOCK, w), lambda r, n: (jnp.minimum(n, nb - 1), r))
    scratch = [pltpu.VMEM((_WIN_BLOCK, w), _F32) for _ in range(8)]
    out = pl.pallas_call(
        body, name=f"attn_bwd_d{dil}", grid=(dil, nb + 1),
        in_specs=[at(0, False), at(1, True), at(1, False), at(2, True), at(2, False), blk, blk, blk, blk],
        out_specs=pl.BlockSpec((_WIN_BLOCK, 3 * w), lambda r, n: (jnp.maximum(n - 1, 0), r)),
        out_shape=jax.ShapeDtypeStruct((length, dil * 3 * w), _BF), scratch_shapes=scratch,
        compiler_params=_params(("arbitrary", "arbitrary")),
    )(view, view, view, view, view, o, lse, do, dlse)
    return out.reshape(s, 3 * w)


def _mix(o1, o2, o3, l1, l2, l3):
    m = jnp.maximum(jnp.maximum(l1, l2), l3)
    e1, e2, e3 = jnp.exp(l1 - m), jnp.exp(l2 - m), jnp.exp(l3 - m)
    return (e1 * o1 + e2 * o2 + e3 * o3) / (e1 + e2 + e3)


def _attn_mix_fwd(outs, lses):
    s = outs[0].shape[0]
    tiles = [(t, _A_WIDTH, 0) for t in (*outs, *lses)]
    return _rowmap("attn_mix_fwd", _mix, tiles, [], [(2 * _A_WIDTH, _A_WIDTH, 0, _BF)], rows=s, tm=_CONV_TM)[0]


def _attn_mix_bwd(outs, lses, dy):
    s = outs[0].shape[0]
    tiles = [(t, _A_WIDTH, 0) for t in (*outs, *lses, dy)]
    body = _vjp_body(_mix, 6, range(6), 1)
    return _rowmap("attn_mix_bwd", body, tiles, [], [(_A_WIDTH, _A_WIDTH, 0, _BF)] * 3 + [(_A_WIDTH, _A_WIDTH, 0, _F32)] * 3, rows=s, tm=_CONV_TM)


_CONV_TM = 256
_HALO = 32


def _ln_silu(pre, g, b):
    mu = jnp.mean(pre, axis=-1, keepdims=True)
    xc = pre - mu
    y = xc * lax.rsqrt(jnp.mean(xc * xc, axis=-1, keepdims=True) + _NORM_EPS)
    return _silu(y * g + b)


def _glu_into(buf, ua_ref, ub_ref, uah_ref, ubh_ref, first):
    halo = uah_ref[...] * jax.nn.sigmoid(ubh_ref[...])
    buf[0:_HALO, :] = jnp.where(first, 0.0, halo)
    buf[_HALO:, :] = ua_ref[...] * jax.nn.sigmoid(ub_ref[...])


_SHIFT_ROWS = _CONV_TM + _HALO - 8


def _shift_copies(sh, buf):
    for b in range(1, 8):
        sh[b - 1] = buf[b:b + _SHIFT_ROWS, :]


def _rows_at(buf, sh, off, rows):
    a, b = divmod(off, 8)
    return buf[8 * a:8 * a + rows, :] if b == 0 else sh[b - 1, 8 * a:8 * a + rows, :]


def _causal_taps(buf, sh, w_ref, rows):
    base = _HALO - (_CONV_WIDTH - 1)
    acc = jnp.zeros((rows, buf.shape[1]), _F32)
    for j in range(_CONV_WIDTH):
        acc = acc + w_ref[j:j + 1, :] * _rows_at(buf, sh, base + j, rows)
    return acc


def _conv_specs(s):
    tm = _CONV_TM
    per = tm // _HALO
    cw = _A_WIDTH
    cur = lambda col: pl.BlockSpec((tm, cw), lambda i: (i, col))
    prev = lambda col: pl.BlockSpec((_HALO, cw), lambda i: (jnp.maximum(i * per - 1, 0), col))
    return tm, per, cw, cur, prev


def _full2(arr):
    return pl.BlockSpec(arr.shape, lambda i: (0, 0))


def _conv_fwd(proj, dw_w, dw_b, ln_g, ln_b, y):
    s = proj.shape[0]
    tm, per, cw, cur, prev = _conv_specs(s)

    def body(ua_ref, ub_ref, uah_ref, ubh_ref, w_ref, b_ref, g_ref, be_ref, y_in, o_ref, buf, sh):
        del y_in
        _glu_into(buf, ua_ref, ub_ref, uah_ref, ubh_ref, pl.program_id(0) == 0)
        _shift_copies(sh, buf)
        pre = _causal_taps(buf, sh, w_ref, tm) + b_ref[...]
        o_ref[...] = _ln_silu(pre, g_ref[...], be_ref[...]).astype(_BF)

    return pl.pallas_call(
        body, name="conv_fwd", grid=(s // tm,),
        in_specs=[cur(3), cur(4), prev(3), prev(4), _full2(dw_w), _full2(dw_b), _full2(ln_g), _full2(ln_b), _ANY],
        out_specs=pl.BlockSpec((tm, cw), lambda i: (i, 1)), out_shape=jax.ShapeDtypeStruct(y.shape, y.dtype),
        scratch_shapes=[pltpu.VMEM((tm + _HALO, cw), _F32), pltpu.VMEM((7, _SHIFT_ROWS, cw), _F32)],
        input_output_aliases={8: 0}, compiler_params=_params(("arbitrary",)),
    )(proj, proj, proj, proj, dw_w, dw_b, ln_g, ln_b, y)


def _conv_bwd_pre(proj, dw_w, dw_b, ln_g, ln_b, dy):
    s = proj.shape[0]
    tm, per, cw, cur, prev = _conv_specs(s)

    def body(ua_ref, ub_ref, uah_ref, ubh_ref, w_ref, b_ref, g_ref, be_ref, dy_ref, dp_ref, dg_ref, dbe_ref, db_ref, buf, sh):
        i = pl.program_id(0)
        _glu_into(buf, ua_ref, ub_ref, uah_ref, ubh_ref, i == 0)
        _shift_copies(sh, buf)
        pre = _causal_taps(buf, sh, w_ref, tm) + b_ref[...]
        _, pull = jax.vjp(_ln_silu, pre, g_ref[...], be_ref[...])
        dpre, dg, dbe = pull(dy_ref[...])
        dp_ref[...] = dpre

        @pl.when(i == 0)
        def _():
            dg_ref[...] = jnp.zeros_like(dg_ref)
            dbe_ref[...] = jnp.zeros_like(dbe_ref)
            db_ref[...] = jnp.zeros_like(db_ref)

        dg_ref[...] += dg
        dbe_ref[...] += dbe
        db_ref[...] += jnp.sum(dpre, axis=0, keepdims=True)

    vec = pl.BlockSpec((1, cw), lambda i: (0, 0))
    vsd = jax.ShapeDtypeStruct((1, cw), _F32)
    return pl.pallas_call(
        body, name="conv_bwd_pre", grid=(s // tm,),
        in_specs=[cur(3), cur(4), prev(3), prev(4), _full2(dw_w), _full2(dw_b), _full2(ln_g), _full2(ln_b), cur(1)],
        out_specs=[pl.BlockSpec((tm, cw), lambda i: (i, 0)), vec, vec, vec],
        out_shape=[jax.ShapeDtypeStruct((s, cw), _F32), vsd, vsd, vsd],
        scratch_shapes=[pltpu.VMEM((tm + _HALO, cw), _F32), pltpu.VMEM((7, _SHIFT_ROWS, cw), _F32)],
        compiler_params=_params(("arbitrary",)),
    )(proj, proj, proj, proj, dw_w, dw_b, ln_g, ln_b, dy)


def _conv_bwd_taps(proj, dw_w, dpre):
    s = proj.shape[0]
    tm, per, cw, cur, prev = _conv_specs(s)
    nt = s // tm

    def body(ua_ref, ub_ref, uah_ref, ubh_ref, w_ref, dp_ref, dpn_ref, dglu_ref, dw_ref, buf, dbuf, sh, dsh):
        i = pl.program_id(0)
        _glu_into(buf, ua_ref, ub_ref, uah_ref, ubh_ref, i == 0)
        dpre = dp_ref[...]
        dbuf[0:tm, :] = dpre
        dbuf[tm:, :] = jnp.where(i == nt - 1, 0.0, dpn_ref[...])
        _shift_copies(sh, buf)
        _shift_copies(dsh, dbuf)
        acc = jnp.zeros((tm, cw), _F32)
        for j in range(_CONV_WIDTH):
            acc = acc + w_ref[j:j + 1, :] * _rows_at(dbuf, dsh, _CONV_WIDTH - 1 - j, tm)
        dglu_ref[...] = acc

        @pl.when(i == 0)
        def _():
            dw_ref[...] = jnp.zeros_like(dw_ref)

        base = _HALO - (_CONV_WIDTH - 1)
        for j in range(_CONV_WIDTH):
            dw_ref[j:j + 1, :] += jnp.sum(dpre * _rows_at(buf, sh, base + j, tm), axis=0, keepdims=True)

    nxt = pl.BlockSpec((_HALO, cw), lambda i: (jnp.minimum((i + 1) * per, s // _HALO - 1), 0))
    return pl.pallas_call(
        body, name="conv_bwd_taps", grid=(nt,),
        in_specs=[cur(3), cur(4), prev(3), prev(4), _full2(dw_w), pl.BlockSpec((tm, cw), lambda i: (i, 0)), nxt],
        out_specs=[pl.BlockSpec((tm, cw), lambda i: (i, 0)), pl.BlockSpec((_HALO, cw), lambda i: (0, 0))],
        out_shape=[jax.ShapeDtypeStruct((s, cw), _F32), jax.ShapeDtypeStruct((_HALO, cw), _F32)],
        scratch_shapes=[pltpu.VMEM((tm + _HALO, cw), _F32), pltpu.VMEM((tm + _HALO, cw), _F32),
                        pltpu.VMEM((7, _SHIFT_ROWS, cw), _F32), pltpu.VMEM((7, _SHIFT_ROWS, cw), _F32)],
        compiler_params=_params(("arbitrary",)),
    )(proj, proj, proj, proj, dw_w, dpre, dpre)


def _hyb_dproj(dqkv, tab_c, tab_s, dglu, proj):
    s = proj.shape[0]
    w = _A_WIDTH

    def body(d1_ref, d2_ref, d3_ref, c_ref, s_ref, dg_ref, ua_ref, ub_ref, o_ref):
        j = pl.program_id(1)

        @pl.when(j < 3)
        def _():
            x = d1_ref[...].astype(_F32) + d2_ref[...].astype(_F32) + d3_ref[...].astype(_F32)
            o_ref[...] = _rot_apply(x, c_ref[...], s_ref[...], True).astype(_BF)

        @pl.when(j == 3)
        def _():
            o_ref[...] = (dg_ref[...] * jax.nn.sigmoid(ub_ref[...])).astype(_BF)

        @pl.when(j == 4)
        def _():
            sg = jax.nn.sigmoid(ub_ref[...])
            o_ref[...] = (dg_ref[...] * ua_ref[...] * sg * (1.0 - sg)).astype(_BF)

    dq = pl.BlockSpec((_TM, w), lambda i, j: (i, jnp.minimum(j, 2)))
    tab = pl.BlockSpec((None, _TM, 128), lambda i, j: (jnp.minimum(j, 2), i, 0))
    col = lambda c: pl.BlockSpec((_TM, w), lambda i, j: (i, c))
    return pl.pallas_call(
        body, name="hyb_dproj", grid=(s // _TM, 5),
        in_specs=[dq, dq, dq, tab, tab, col(0), col(3), col(4)],
        out_specs=pl.BlockSpec((_TM, w), lambda i, j: (i, j)),
        out_shape=jax.ShapeDtypeStruct((s, 5 * w), _BF), compiler_params=_params(("arbitrary", "arbitrary")),
    )(*dqkv, tab_c, tab_s, dglu, proj, proj)


def _proj_in(name, xn, w, layer):
    s, d = xn.shape
    ns = w.shape[-1]
    return _mm(name, xn, w, grid=(4, s // _TM), a_spec=pl.BlockSpec((_TM, d), lambda j, i: (i, 0)),
               b_spec=pl.BlockSpec((None, None, d, ns), lambda j, i: (layer, j, 0, 0)), dn=_NN,
               out_shape=jax.ShapeDtypeStruct((s, 4 * ns), _F32), out_spec=pl.BlockSpec((_TM, ns), lambda j, i: (i, j)))


def _proj_in_dx(name, dz, w, layer, h, gamma, dh_out):
    s = dz.shape[0]
    d, ns = w.shape[2], w.shape[3]
    return _mm(name, dz, w, grid=(s // _TM, 4), a_spec=pl.BlockSpec((_TM, ns), lambda i, j: (i, j)),
               b_spec=pl.BlockSpec((None, None, d, ns), lambda i, j: (layer, j, 0, 0)), dn=_NT, red=1,
               acc_shape=(_TM, d), out_shape=jax.ShapeDtypeStruct((s, d), _F32),
               out_spec=pl.BlockSpec((_TM, d), lambda i, j: (i, 0)), epi=_norm_bwd_epi,
               **_norm_bwd_args(h, dh_out, gamma, _TM))


def _proj_in_dw(name, xn, dz):
    s, d = xn.shape
    ns = dz.shape[1] // 4
    return _mm(name, xn, dz, grid=(4, s // _TM), a_spec=pl.BlockSpec((_TM, d), lambda j, i: (i, 0)),
               b_spec=pl.BlockSpec((_TM, ns), lambda j, i: (i, j)), dn=_TN, red=1, acc_shape=(d, ns),
               out_shape=jax.ShapeDtypeStruct((4, d, ns), _BF),
               out_spec=pl.BlockSpec((None, d, ns), lambda j, i: (j, 0, 0)))


def _proj_out_dy(name, dh, w, layer):
    s, d = dh.shape
    k = w.shape[1]
    return _mm(name, dh, w, grid=(s // _TM,), a_spec=pl.BlockSpec((_TM, d), lambda i: (i, 0)),
               b_spec=pl.BlockSpec((None, k, d), lambda i: (layer, 0, 0)), dn=_NT,
               out_shape=jax.ShapeDtypeStruct((s, k), _F32), out_spec=pl.BlockSpec((_TM, k), lambda i: (i, 0)))


def _hyb_fwd(h, gamma, w_in, w_out, dw_w, dw_b, ln_g, ln_b, tabs, li):
    hn = _rmsnorm_fwd(h, gamma)
    proj = _proj_in("hyb_in", hn, w_in, li)
    qkv = _qkv_prep(proj, *tabs)
    branches = [_attn_fwd(qkv, dil) for dil in _DILATIONS]
    outs, lses = [b[0] for b in branches], [b[1] for b in branches]
    y = _attn_mix_fwd(outs, lses)
    y = _conv_fwd(proj, dw_w, dw_b, ln_g, ln_b, y)
    h_new = _proj_residual("hyb_out", y, w_out, li, h, 1.0)
    return h_new, (h, hn, proj, qkv, outs, lses, y)


def _hyb_bwd(dh, saved, gamma, w_in, w_out, dw_w, dw_b, ln_g, ln_b, tabs, li):
    h, hn, proj, qkv, outs, lses, y = saved
    dy = _proj_out_dy("hyb_dy", dh, w_out, li)
    d_w_out = _wgrad_rows("hyb_dwout", y, dh, 1.0)
    mixg = _attn_mix_bwd(outs, lses, dy)
    dqkv = [_attn_bwd(qkv, outs[b], lses[b], mixg[b], mixg[3 + b], dil) for b, dil in enumerate(_DILATIONS)]
    dpre, d_ln_g, d_ln_b, d_dw_b = _conv_bwd_pre(proj, dw_w, dw_b, ln_g, ln_b, dy)
    dglu, d_dw_w = _conv_bwd_taps(proj, dw_w, dpre)
    dproj = _hyb_dproj(dqkv, *tabs, dglu, proj)
    dh_in, dgamma = _proj_in_dx("hyb_dhn", dproj, w_in, li, h, gamma, dh)
    d_w_in = _proj_in_dw("hyb_dwin", hn, dproj)
    return dh_in, dgamma, (d_dw_w, d_dw_b, d_ln_g, d_ln_b), d_w_in, d_w_out


_GDN_BLOCK = 128
_GDN_HEADS_PER_STEP = 8


def _gdn_chunk(qs, ks, vs, gates, states, heads):
    n = len(qs)
    hs = range(n)
    c = qs[0].shape[0]
    lane = lax.broadcasted_iota(jnp.int32, gates.shape, 1)
    row = lax.broadcasted_iota(jnp.int32, (c, c), 0)
    col = lax.broadcasted_iota(jnp.int32, (c, c), 1)
    rowc = lax.broadcasted_iota(jnp.int32, (c, 1), 0)
    causal = row >= col
    eye = jnp.where(row == col, 1.0, 0.0)
    beta = [jnp.sum(jnp.where(lane == heads[h], gates, 0.0), axis=1, keepdims=True) for h in hs]
    g = [jnp.sum(jnp.where(lane == heads[h] + _GDN_HEADS, gates, 0.0), axis=1, keepdims=True) for h in hs]
    gc_row = [jnp.sum(jnp.where(row <= col, g[h], 0.0), axis=0, keepdims=True) for h in hs]
    gc_col = [jnp.sum(jnp.where(row == col, gc_row[h], 0.0), axis=1, keepdims=True) for h in hs]
    decay = [jnp.where(causal, jnp.exp(jnp.where(causal, gc_col[h] - gc_row[h], 0.0)), 0.0) for h in hs]
    kb = [ks[h] * beta[h] for h in hs]
    l_mat = [jnp.where(row > col, _dot_nt(kb[h], ks[h]) * decay[h], 0.0) for h in hs]
    t_inv = [eye - l_mat[h] for h in hs]
    x = [_dot(l_mat[h], l_mat[h]) for h in hs]
    for it in range(5):
        t_inv = [t_inv[h] + _dot(t_inv[h], x[h]) for h in hs]
        if it < 4:
            x = [_dot(x[h], x[h]) for h in hs]
    eg = [jnp.exp(gc_col[h]) for h in hs]
    u = [_dot(t_inv[h], vs[h] * beta[h]) for h in hs]
    w = [_dot(t_inv[h], kb[h] * eg[h]) for h in hs]
    attn = [jnp.where(causal, _dot_nt(qs[h], ks[h]) * decay[h], 0.0) for h in hs]
    g_last = [jnp.sum(jnp.where(rowc == c - 1, gc_col[h], 0.0), axis=0, keepdims=True) for h in hs]
    k_dec = [ks[h] * jnp.exp(g_last[h] - gc_col[h]) for h in hs]
    v_new = [u[h] - _dot(w[h], states[h]) for h in hs]
    o = [_dot(qs[h] * eg[h], states[h]) + _dot(attn[h], v_new[h]) for h in hs]
    new_states = [states[h] * jnp.exp(g_last[h]) + _dot_tn(k_dec[h], v_new[h]) for h in hs]
    return o, new_states


def _gdn_chunk_fwd(q, k, v, gates):
    s = q.shape[0]
    cb, c = _GDN_BLOCK, _GDN_CHUNK
    nblk, per = s // cb, cb // c

    hps = _GDN_HEADS_PER_STEP

    def body(q_ref, k_ref, v_ref, g_ref, o_ref, st_ref, state):
        @pl.when(pl.program_id(1) == 0)
        def _():
            state[...] = jnp.zeros_like(state)

        heads = [pl.program_id(0) * hps + hh for hh in range(hps)]
        cols = [slice(hh * _GDN_DIM, (hh + 1) * _GDN_DIM) for hh in range(hps)]
        for ci in range(per):
            rows = slice(ci * c, (ci + 1) * c)
            sts = [state[hh] for hh in range(hps)]
            for hh in range(hps):
                st_ref[hh, ci] = sts[hh]
            outs, new = _gdn_chunk([q_ref[rows, cl] for cl in cols], [k_ref[rows, cl] for cl in cols],
                                   [v_ref[rows, cl] for cl in cols], g_ref[rows, :], sts, heads)
            for hh in range(hps):
                o_ref[rows, cols[hh]] = outs[hh]
                state[hh] = new[hh]

    slab = pl.BlockSpec((cb, hps * _GDN_DIM), lambda h, i: (i, h))
    return pl.pallas_call(
        body, name="gdn_chunk_fwd", grid=(_GDN_HEADS // hps, nblk),
        in_specs=[slab, slab, slab, pl.BlockSpec((cb, 128), lambda h, i: (i, 0))],
        out_specs=[slab, pl.BlockSpec((hps, per, _GDN_DIM, _GDN_DIM), lambda h, i: (h, i, 0, 0))],
        out_shape=[jax.ShapeDtypeStruct((s, _GDN_HEADS * _GDN_DIM), _F32),
                   jax.ShapeDtypeStruct((_GDN_HEADS, s // c, _GDN_DIM, _GDN_DIM), _F32)],
        scratch_shapes=[pltpu.VMEM((hps, _GDN_DIM, _GDN_DIM), _F32)],
        compiler_params=_params(("arbitrary", "arbitrary")),
    )(q, k, v, gates)


def _gdn_chunk_bwd(q, k, v, gates, states, do):
    s = q.shape[0]
    cb, c = _GDN_BLOCK, _GDN_CHUNK
    nblk, per = s // cb, cb // c

    hps = _GDN_HEADS_PER_STEP

    def body(q_ref, k_ref, v_ref, g_ref, st_ref, do_ref, dq_ref, dk_ref, dv_ref, dg_ref, dstate):
        @pl.when(pl.program_id(1) == 0)
        def _():
            dstate[...] = jnp.zeros_like(dstate)

        heads = [pl.program_id(0) * hps + hh for hh in range(hps)]
        cols = [slice(hh * _GDN_DIM, (hh + 1) * _GDN_DIM) for hh in range(hps)]
        for ci in reversed(range(per)):
            rows = slice(ci * c, (ci + 1) * c)
            fn = functools.partial(_gdn_chunk, heads=heads)
            _, pull = jax.vjp(fn, [q_ref[rows, cl] for cl in cols], [k_ref[rows, cl] for cl in cols],
                              [v_ref[rows, cl] for cl in cols], g_ref[rows, :], [st_ref[hh, ci] for hh in range(hps)])
            dq, dk, dv, dg, dst = pull(([do_ref[rows, cl] for cl in cols], [dstate[hh] for hh in range(hps)]))
            dg_ref[rows, :] = dg
            for hh in range(hps):
                dq_ref[rows, cols[hh]] = dq[hh]
                dk_ref[rows, cols[hh]] = dk[hh]
                dv_ref[rows, cols[hh]] = dv[hh]
                dstate[hh] = dst[hh]

    slab = pl.BlockSpec((cb, hps * _GDN_DIM), lambda h, i: (nblk - 1 - i, h))
    sds = jax.ShapeDtypeStruct((s, _GDN_HEADS * _GDN_DIM), _F32)
    return pl.pallas_call(
        body, name="gdn_chunk_bwd", grid=(_GDN_HEADS // hps, nblk),
        in_specs=[slab, slab, slab, pl.BlockSpec((cb, 128), lambda h, i: (nblk - 1 - i, 0)),
                  pl.BlockSpec((hps, per, _GDN_DIM, _GDN_DIM), lambda h, i: (h, nblk - 1 - i, 0, 0)), slab],
        out_specs=[slab, slab, slab, pl.BlockSpec((None, cb, 128), lambda h, i: (h, nblk - 1 - i, 0))],
        out_shape=[sds, sds, sds, jax.ShapeDtypeStruct((_GDN_HEADS // hps, s, 128), _F32)],
        scratch_shapes=[pltpu.VMEM((hps, _GDN_DIM, _GDN_DIM), _F32)],
        compiler_params=_params(("arbitrary", "arbitrary")),
    )(q, k, v, gates, states, do)


_SHALO = 8


def _swish_l2(c, normalize, scale):
    y = _silu(c)
    if normalize:
        y = y * lax.rsqrt(jnp.sum(y * y, axis=-1, keepdims=True) + _NORM_EPS) * scale
    return y


def _short_taps(buf, w_ref, rows):
    base = _SHALO - (_GDN_SHORT_CONV - 1)
    acc = jnp.zeros((rows, buf.shape[1]), _F32)
    for t in range(_GDN_SHORT_CONV):
        acc = acc + w_ref[t:t + 1, :] * buf[base + t:base + t + rows, :]
    return acc


_GDN_W = _GDN_HEADS * _GDN_DIM


def _gdn_conv_fwd(proj, conv_w, grp, normalize, scale):
    s = proj.shape[0]
    tm = _CONV_TM
    per = tm // _SHALO

    def body(x_ref, xp_ref, w_ref, o_ref, buf):
        buf[0:_SHALO, :] = jnp.where(pl.program_id(0) == 0, 0.0, xp_ref[...])
        buf[_SHALO:, :] = x_ref[...]
        conv = _short_taps(buf, w_ref, tm)
        for h in range(_GDN_HEADS):
            cols = slice(h * _GDN_DIM, (h + 1) * _GDN_DIM)
            o_ref[:, cols] = _swish_l2(conv[:, cols], normalize, scale)

    return pl.pallas_call(
        body, name=f"gdn_conv_fwd_{grp}", grid=(s // tm,),
        in_specs=[pl.BlockSpec((tm, _GDN_W), lambda i: (i, grp)),
                  pl.BlockSpec((_SHALO, _GDN_W), lambda i: (jnp.maximum(i * per - 1, 0), grp)),
                  pl.BlockSpec((_GDN_SHORT_CONV, _GDN_W), lambda i: (0, grp))],
        out_specs=pl.BlockSpec((tm, _GDN_W), lambda i: (i, 0)),
        out_shape=jax.ShapeDtypeStruct((s, _GDN_W), _F32),
        scratch_shapes=[pltpu.VMEM((tm + _SHALO, _GDN_W), _F32)], compiler_params=_params(("arbitrary",)),
    )(proj, proj, conv_w)


def _gdn_conv_bwd(proj, conv_w, grp, normalize, scale, dy, dproj):
    s = proj.shape[0]
    tm = _CONV_TM
    per = tm // _SHALO
    nt = s // tm
    taps = _GDN_SHORT_CONV

    def body(x_ref, xp_ref, xn_ref, w_ref, dy_ref, dyn_ref, dp_in, dx_ref, dw_ref, buf, dbuf):
        del dp_in
        i = pl.program_id(0)
        last = i == nt - 1
        buf[0:_SHALO, :] = jnp.where(i == 0, 0.0, xp_ref[...])
        buf[_SHALO:_SHALO + tm, :] = x_ref[...]
        buf[_SHALO + tm:, :] = jnp.where(last, 0.0, xn_ref[...])
        conv = _short_taps(buf, w_ref, tm + _SHALO)
        dy_ext = jnp.concatenate([dy_ref[...], jnp.where(last, 0.0, dyn_ref[...])], axis=0)
        for h in range(_GDN_HEADS):
            cols = slice(h * _GDN_DIM, (h + 1) * _GDN_DIM)
            _, pull = jax.vjp(lambda c: _swish_l2(c, normalize, scale), conv[:, cols])
            dbuf[:, cols] = pull(dy_ext[:, cols])[0]
        acc = jnp.zeros((tm, _GDN_W), _F32)
        for t in range(taps):
            o = taps - 1 - t
            acc = acc + w_ref[t:t + 1, :] * dbuf[o:o + tm, :]
        dx_ref[...] = acc.astype(dx_ref.dtype)

        @pl.when(i == 0)
        def _():
            dw_ref[...] = jnp.zeros_like(dw_ref)

        base = _SHALO - (taps - 1)
        for t in range(taps):
            dw_ref[t:t + 1, :] += jnp.sum(dbuf[0:tm, :] * buf[base + t:base + t + tm, :], axis=0, keepdims=True)

    cur = lambda col: pl.BlockSpec((tm, _GDN_W), lambda i: (i, col))
    nxt = lambda col: pl.BlockSpec((_SHALO, _GDN_W), lambda i: (jnp.minimum((i + 1) * per, s // _SHALO - 1), col))
    return pl.pallas_call(
        body, name=f"gdn_conv_bwd_{grp}", grid=(nt,),
        in_specs=[cur(grp), pl.BlockSpec((_SHALO, _GDN_W), lambda i: (jnp.maximum(i * per - 1, 0), grp)), nxt(grp),
                  pl.BlockSpec((taps, _GDN_W), lambda i: (0, grp)), cur(0), nxt(0), _ANY],
        out_specs=[cur(grp), pl.BlockSpec((_SHALO, _GDN_W), lambda i: (0, 0))],
        out_shape=[jax.ShapeDtypeStruct(dproj.shape, dproj.dtype), jax.ShapeDtypeStruct((_SHALO, _GDN_W), _F32)],
        scratch_shapes=[pltpu.VMEM((tm + 2 * _SHALO, _GDN_W), _F32), pltpu.VMEM((tm + _SHALO, _GDN_W), _F32)],
        input_output_aliases={6: 0}, compiler_params=_params(("arbitrary",)),
    )(proj, proj, proj, conv_w, dy, dy, dproj)


def _softplus(z):
    return jnp.maximum(z, 0.0) + jnp.log(1.0 + jnp.exp(-jnp.abs(z)))


def _gates(ba, a_lane, dt_lane):
    lane = lax.broadcasted_iota(jnp.int32, ba.shape, 1)
    g = -jnp.exp(a_lane) * _softplus(ba + dt_lane)
    return jnp.where(lane < _GDN_HEADS, jax.nn.sigmoid(ba), jnp.where(lane < 2 * _GDN_HEADS, g, 0.0))


def _gates_fwd(ba, a_lane, dt_lane):
    return _rowmap("gdn_gates_fwd", _gates, [(ba, 128, 0)], [a_lane, dt_lane], [(128, 128, 0, _F32)],
                   rows=ba.shape[0], tm=_TM)[0]


def _gates_bwd(ba, a_lane, dt_lane, dgates):
    def body(bav, dg8, av, dv):
        _, pull = jax.vjp(_gates, bav, av, dv)
        return pull(jnp.sum(dg8, axis=0))

    return _rowmap("gdn_gates_bwd", body, [(ba, 128, 0), (dgates, 128, 0)], [a_lane, dt_lane], [(128, 128, 0, _BF)],
                   [(1, 128), (1, 128)], rows=ba.shape[0], tm=_TM)


def _out_gate(o, z, g):
    return (_rms(o, g) * _silu(z)).astype(_BF)


def _head_cols():
    return [slice(h * _GDN_DIM, (h + 1) * _GDN_DIM) for h in range(_GDN_HEADS)]


def _out_gate_fwd(o, proj, norm_g):
    def body(ov, zv, gv):
        return jnp.concatenate([_out_gate(ov[:, cl], zv[:, cl], gv) for cl in _head_cols()], axis=1)

    return _rowmap("gdn_out_gate_fwd", body, [(o, _GDN_W, 0), (proj, _GDN_W, 3)], [norm_g], [(_GDN_W, _GDN_W, 0, _BF)],
                   rows=o.shape[0], tm=_CONV_TM)[0]


def _out_gate_bwd(o, proj, norm_g, dy):
    def body(ov, zv, dyv, gv):
        d_o, d_z, d_g = [], [], jnp.zeros_like(gv)
        for cl in _head_cols():
            _, pull = jax.vjp(_out_gate, ov[:, cl], zv[:, cl], gv)
            g_o, g_z, g_g = pull(dyv[:, cl].astype(_BF))
            d_o.append(g_o)
            d_z.append(g_z)
            d_g = d_g + g_g
        return jnp.concatenate(d_o, axis=1), jnp.concatenate(d_z, axis=1), d_g

    return _rowmap("gdn_out_gate_bwd", body, [(o, _GDN_W, 0), (proj, _GDN_W, 3), (dy, _GDN_W, 0)], [norm_g],
                   [(_GDN_W, _GDN_W, 0, _F32), (proj.shape[1], _GDN_W, 3, _BF)], [(1, _GDN_DIM)],
                   rows=o.shape[0], tm=_CONV_TM)


def _plain_in(name, xn, w, layer, tn):
    s, d = xn.shape
    n = w.shape[2]
    return _mm(name, xn, w, grid=(n // tn, s // _TM), a_spec=pl.BlockSpec((_TM, d), lambda j, i: (i, 0)),
               b_spec=pl.BlockSpec((None, d, tn), lambda j, i: (layer, 0, j)), dn=_NN,
               out_shape=jax.ShapeDtypeStruct((s, n), _F32), out_spec=pl.BlockSpec((_TM, tn), lambda j, i: (i, j)))


def _plain_dx(name, dz, w, layer, tn, norm=None):
    s, n = dz.shape
    d = w.shape[1]
    row = pl.BlockSpec((_TM, d), lambda i, j: (i, 0))
    kw = {}
    if norm is not None:
        extra, h, gamma, dh_out = norm
        kw = _norm_bwd_args(h, dh_out, gamma, _TM)
        kw.update(extras=(extra, *kw["extras"]), extra_specs=(row, *kw["extra_specs"]),
                  epi=lambda acc, e, hv, dhv, gv: _norm_bwd_epi(acc + e, hv, dhv, gv))
    return _mm(name, dz, w, grid=(s // _TM, n // tn), a_spec=pl.BlockSpec((_TM, tn), lambda i, j: (i, j)),
               b_spec=pl.BlockSpec((None, d, tn), lambda i, j: (layer, 0, j)), dn=_NT, red=1, acc_shape=(_TM, d),
               out_shape=jax.ShapeDtypeStruct((s, d), _F32), out_spec=row, **kw)


def _plain_dw(name, xn, dz, tn):
    s, d = xn.shape
    n = dz.shape[1]
    return _mm(name, xn, dz, grid=(n // tn, s // _TM), a_spec=pl.BlockSpec((_TM, d), lambda j, i: (i, 0)),
               b_spec=pl.BlockSpec((_TM, tn), lambda j, i: (i, j)), dn=_TN, red=1, acc_shape=(d, tn),
               out_shape=jax.ShapeDtypeStruct((d, n), _BF), out_spec=pl.BlockSpec((d, tn), lambda j, i: (0, j)))


_GDN_QK_SCALE = _GDN_DIM ** -0.5


def _gdn_fwd(h, gamma, w_qkvz, w_ba, w_out, conv_w, a_lane, dt_lane, norm_g, li):
    hn = _rmsnorm_fwd(h, gamma)
    proj = _plain_in("gdn_in", hn, w_qkvz, li, 1024)
    ba = _plain_in("gdn_in_ba", hn, w_ba, li, 128)
    q = _gdn_conv_fwd(proj, conv_w, 0, True, _GDN_QK_SCALE)
    k = _gdn_conv_fwd(proj, conv_w, 1, True, 1.0)
    v = _gdn_conv_fwd(proj, conv_w, 2, False, 1.0)
    gates = _gates_fwd(ba, a_lane, dt_lane)
    o, states = _gdn_chunk_fwd(q, k, v, gates)
    y = _out_gate_fwd(o, proj, norm_g)
    h_new = _proj_residual("gdn_out", y, w_out, li, h, 1.0)
    return h_new, (h, hn, proj, ba, q, k, v, gates, states, o, y)


def _gdn_bwd(dh, saved, gamma, w_qkvz, w_ba, w_out, conv_w, a_lane, dt_lane, norm_g, li, n_ba):
    h, hn, proj, ba, q, k, v, gates, states, o, y = saved
    dy = _proj_out_dy("gdn_dy", dh, w_out, li)
    d_w_out = _wgrad_rows("gdn_dwout", y, dh, 1.0)
    do, dproj, d_norm_g = _out_gate_bwd(o, proj, norm_g, dy)
    dq, dk, dv, dgates = _gdn_chunk_bwd(q, k, v, gates, states, do)
    dproj, dcw_q = _gdn_conv_bwd(proj, conv_w, 0, True, _GDN_QK_SCALE, dq, dproj)
    dproj, dcw_k = _gdn_conv_bwd(proj, conv_w, 1, True, 1.0, dk, dproj)
    dproj, dcw_v = _gdn_conv_bwd(proj, conv_w, 2, False, 1.0, dv, dproj)
    dba, d_a, d_dt = _gates_bwd(ba, a_lane, dt_lane, dgates)
    dhn_ba = _plain_dx("gdn_dhn_ba", dba, w_ba, li, 128)
    dh_in, dgamma = _plain_dx("gdn_dhn", dproj, w_qkvz, li, 1024, norm=(dhn_ba, h, gamma, dh))
    dw_qkvz = _plain_dw("gdn_dwin", hn, dproj, 1024)
    dw_ba = _plain_dw("gdn_dwin_ba", hn, dba, 128)
    d_conv_w = jnp.concatenate([dcw_q, dcw_k, dcw_v], axis=1)[:_GDN_SHORT_CONV]
    d_w_in = jnp.concatenate([dw_qkvz, dw_ba[:, :n_ba]], axis=1)
    d_w_in = jnp.transpose(d_w_in.reshape(d_w_in.shape[0], 4, -1), (1, 0, 2))
    return dh_in, dgamma, (d_conv_w, d_a, d_dt, d_norm_g), d_w_in, d_w_out


_MESH = pl.DeviceIdType.MESH
_ROW_TM = 384


def _place():
    x, y, c = lax.axis_index("x"), lax.axis_index("y"), lax.axis_index("c")
    chips = [(1 - x, y), (x, 1 - y), (1 - x, 1 - y)]
    return x, y, c, chips, [2 * px + py for px, py in chips]


def _remote(src, dst, send_sem, recv_sem, device):
    return pltpu.make_async_remote_copy(src_ref=src, dst_ref=dst, send_sem=send_sem, recv_sem=recv_sem,
                                        device_id=device, device_id_type=_MESH)


def _rows2d(t):
    return t.reshape(-1, t.shape[-1])


def _row_tile(rows):
    for tm in range(min(rows, _ROW_TM) // 8 * 8, 7, -8):
        if rows % tm == 0:
            return tm
    raise ValueError(f"no row tile for {rows} rows")


def _placed_map(name, fn, where, ins, out_rows, cols, out_dtype, out_row, steps, tm, into=None):
    n_in = len(ins)

    def kern(where_ref, *refs):
        del where_ref
        o_ref = refs[n_in + (1 if into is not None else 0)]
        o_ref[...] = fn(*[r[...] for r in refs[:n_in]]).astype(o_ref.dtype)

    in_specs, operands = [], [where]
    for arr, rowfn in ins:
        operands.append(arr)
        if arr.ndim == 3:
            in_specs.append(pl.BlockSpec((arr.shape[0], tm, cols), lambda i, wh, f=rowfn: (0, f(i, wh), 0)))
        else:
            in_specs.append(pl.BlockSpec((tm, cols), lambda i, wh, f=rowfn: (f(i, wh), 0)))
    aliases = {}
    if into is not None:
        operands.append(into)
        in_specs.append(_ANY)
        aliases = {n_in + 1: 0}
    spec = pltpu.PrefetchScalarGridSpec(
        num_scalar_prefetch=1, grid=(steps,), in_specs=in_specs,
        out_specs=pl.BlockSpec((tm, cols), lambda i, wh: (out_row(i, wh), 0)))
    return pl.pallas_call(
        kern, name=name, grid_spec=spec, out_shape=jax.ShapeDtypeStruct((out_rows, cols), out_dtype),
        input_output_aliases=aliases, compiler_params=_params(("arbitrary",)),
    )(*operands)


def _cast_slab(name, t, where):
    n_l, r, cols = t.shape
    tm = _row_tile(r)
    per = r // tm
    buf = _placed_map("cast_" + name, lambda val: val, where, [(_rows2d(t), lambda i, wh: i)], n_l * 4 * r, cols, _BF,
                      lambda i, wh: ((i // per) * 4 + wh[1]) * per + i % per, n_l * per, tm)
    return buf.reshape(n_l, 4, r, cols)


_HBM = pl.BlockSpec(memory_space=pltpu.HBM)
_SEM = pl.BlockSpec(memory_space=pltpu.SEMAPHORE)
_EFFECT = pltpu.SideEffectType.DATAFLOW_SIDE_EFFECTING


def _core_layers(n_layers, c, part):
    half = n_layers // 2
    sub = half // part[1]
    return pl.ds(c * half + part[0] * sub, sub), pl.ds((1 - c) * half + part[0] * sub, sub)


def _ici_copies(refs, send, recv, arrivals, part):
    x, y, c, chips, idx = _place()
    me = 2 * x + y
    res = []
    for t, ref in enumerate(refs):
        mine, _ = _core_layers(ref.shape[0], c, part)
        for k in range(3):
            sem = 3 * t + k
            block = ref.at[mine, idx[k] if arrivals else me]
            res.append(_remote(block, block, send.at[sem], recv.at[sem], (*chips[k], c)))
    return res


def _gather_start(name, bufs, after, part):
    n = len(bufs)

    def body(*refs):
        send, recv = refs[n + 1], refs[n + 2]
        for cp in _ici_copies(refs[:n], send, recv, False, part):
            cp.start()
        refs[2 * n + 3][...] = jnp.zeros((8, 128), _F32)

    res = pl.pallas_call(
        body, name=name,
        out_shape=(pltpu.SemaphoreType.DMA((3 * n,)), pltpu.SemaphoreType.DMA((3 * n,)),
                   *[pltpu.HBM(b.shape, b.dtype) for b in bufs], jax.ShapeDtypeStruct((8, 128), _F32)),
        in_specs=[*[_HBM] * n, _ANY], out_specs=(_SEM, _SEM, *[_HBM] * n, pl.BlockSpec(memory_space=pltpu.VMEM)),
        input_output_aliases={t: 2 + t for t in range(n)},
        compiler_params=pltpu.CompilerParams(has_side_effects=_EFFECT),
    )(*[pltpu.with_memory_space_constraint(b, pltpu.HBM) for b in bufs], after)
    return res[0], res[1], list(res[2:2 + n]), res[2 + n]


def _gather_wait(name, send_sem, recv_sem, bufs, after, part):
    n = len(bufs)

    def body(*refs):
        send, recv = refs[n], refs[n + 1]
        for cp in _ici_copies(refs[:n], send, recv, False, part):
            cp.wait_send()
        for cp in _ici_copies(refs[:n], send, recv, True, part):
            cp.wait_recv()

    res = pl.pallas_call(
        body, name=name, out_shape=tuple(pltpu.HBM(b.shape, b.dtype) for b in bufs),
        in_specs=[*[_HBM] * n, _SEM, _SEM, _ANY], out_specs=tuple([_HBM] * n),
        input_output_aliases={t: t for t in range(n)},
        compiler_params=pltpu.CompilerParams(has_side_effects=_EFFECT),
    )(*bufs, send_sem, recv_sem, after)
    return list(res)


def _gather_forward(name, bufs, part):
    n = len(bufs)

    def body(*refs):
        outs, send, recv = refs[n:2 * n], refs[2 * n], refs[2 * n + 1]
        x, y, c, _, idx = _place()
        copies, arrive = [], []
        for t, ref in enumerate(outs):
            mine, other = _core_layers(ref.shape[0], c, part)
            for k in range(3):
                sem = 3 * t + k
                landed, dst = ref.at[mine, idx[k]], ref.at[other, idx[k]]
                copies.append(_remote(landed, landed, send.at[sem], recv.at[sem], (x, y, 1 - c)))
                arrive.append(_remote(dst, dst, send.at[sem], recv.at[sem], (x, y, 1 - c)))
        for cp in copies:
            cp.start()
        for cp in arrive:
            cp.wait_recv()
        for cp in copies:
            cp.wait_send()

    res = pl.pallas_call(
        body, name=name, in_specs=[_ANY] * n, out_specs=[_ANY] * n,
        out_shape=[jax.ShapeDtypeStruct(b.shape, b.dtype) for b in bufs],
        scratch_shapes=[pltpu.SemaphoreType.DMA((3 * n,)), pltpu.SemaphoreType.DMA((3 * n,))],
        input_output_aliases={t: t for t in range(n)},
    )(*bufs)
    return list(res)


def _rs_sibling(name, gs):
    n = len(gs)

    def body(*refs):
        send, recv = refs[2 * n], refs[2 * n + 1]
        x, y, c, _, _ = _place()
        copies = []
        for t in range(n):
            rh = refs[t].shape[1] // 2
            cp = _remote(refs[t].at[:, pl.ds((1 - c) * rh, rh)], refs[n + t], send.at[t], recv.at[t], (x, y, 1 - c))
            cp.start()
            copies.append(cp)
        for cp in copies:
            cp.wait()

    return pl.pallas_call(
        body, name=name, in_specs=[_ANY] * n, out_specs=[_ANY] * n,
        out_shape=[jax.ShapeDtypeStruct((4, g.shape[1] // 2, g.shape[2]), g.dtype) for g in gs],
        scratch_shapes=[pltpu.SemaphoreType.DMA((n,)), pltpu.SemaphoreType.DMA((n,))],
    )(*gs)


def _chip_copies(sums, lands, send, recv):
    x, y, c, chips, idx = _place()
    return [_remote(sums[t].at[idx[k]], lands[t].at[k], send.at[3 * t + k], recv.at[3 * t + k], (*chips[k], c))
            for t in range(len(sums)) for k in range(3)]


def _rs_chips_start(name, sums, after):
    n = len(sums)
    lands = [lax.empty((3,) + s.shape[1:], s.dtype) for s in sums]

    def body(*refs):
        send, recv = refs[2 * n + 1], refs[2 * n + 2]
        for cp in _chip_copies(refs[:n], refs[n:2 * n], send, recv):
            cp.start()
        refs[4 * n + 3][...] = jnp.zeros((8, 128), _F32)

    both = [*sums, *lands]
    res = pl.pallas_call(
        body, name=name,
        out_shape=(pltpu.SemaphoreType.DMA((3 * n,)), pltpu.SemaphoreType.DMA((3 * n,)),
                   *[pltpu.HBM(b.shape, b.dtype) for b in both], jax.ShapeDtypeStruct((8, 128), _F32)),
        in_specs=[*[_HBM] * (2 * n), _ANY],
        out_specs=(_SEM, _SEM, *[_HBM] * (2 * n), pl.BlockSpec(memory_space=pltpu.VMEM)),
        input_output_aliases={t: 2 + t for t in range(2 * n)},
        compiler_params=pltpu.CompilerParams(has_side_effects=_EFFECT),
    )(*[pltpu.with_memory_space_constraint(b, pltpu.HBM) for b in both], after)
    return res[0], res[1], list(res[2:2 + n]), list(res[2 + n:2 + 2 * n]), res[2 + 2 * n]


def _rs_chips_wait(name, send_sem, recv_sem, sums, lands, after):
    n = len(sums)

    def body(*refs):
        send, recv = refs[2 * n], refs[2 * n + 1]
        copies = _chip_copies(refs[:n], refs[n:2 * n], send, recv)
        for cp in copies:
            cp.wait_send()
        for cp in copies:
            cp.wait_recv()

    both = [*sums, *lands]
    res = pl.pallas_call(
        body, name=name, out_shape=tuple(pltpu.HBM(b.shape, b.dtype) for b in both),
        in_specs=[*[_HBM] * (2 * n), _SEM, _SEM, *[_ANY] * len(after)], out_specs=tuple([_HBM] * (2 * n)),
        input_output_aliases={t: t for t in range(2 * n)},
        compiler_params=pltpu.CompilerParams(has_side_effects=_EFFECT),
    )(*both, send_sem, recv_sem, *after)
    return list(res[:n]), list(res[n:])


def _after(x, token):
    def body(x_ref, t_ref, o_ref):
        del x_ref, t_ref, o_ref

    return pl.pallas_call(
        body, name="after", in_specs=[_ANY, _ANY], out_specs=_ANY, out_shape=jax.ShapeDtypeStruct(x.shape, x.dtype),
        input_output_aliases={0: 0},
    )(x, token)


def _rs_join(name, totals, layers):
    n = len(totals)

    def body(*refs):
        outs, send, recv = refs[n:2 * n], refs[2 * n], refs[2 * n + 1]
        x, y, c, _, _ = _place()
        copies, arrive = [], []
        for t in range(n):
            rh = outs[t].shape[1] // 2
            mine = outs[t].at[layers[t], pl.ds(c * rh, rh)]
            landing = outs[t].at[layers[t], pl.ds((1 - c) * rh, rh)]
            copies.append(_remote(mine, mine, send.at[t], recv.at[t], (x, y, 1 - c)))
            arrive.append(_remote(landing, landing, send.at[t], recv.at[t], (x, y, 1 - c)))
        for cp in copies:
            cp.start()
        for cp in arrive:
            cp.wait_recv()
        for cp in copies:
            cp.wait_send()

    return pl.pallas_call(
        body, name=name, in_specs=[_ANY] * n, out_specs=[_ANY] * n,
        out_shape=[jax.ShapeDtypeStruct(b.shape, b.dtype) for b in totals],
        scratch_shapes=[pltpu.SemaphoreType.DMA((n,)), pltpu.SemaphoreType.DMA((n,))],
        input_output_aliases={t: t for t in range(n)},
    )(*totals)


def _pair_sum(g, got, where):
    _, r, cols = g.shape
    rh = r // 2
    tm = _row_tile(rh)
    per, perh = r // tm, rh // tm
    out = _placed_map("rs_add_pair", lambda u, w: u.astype(_F32) + w.astype(_F32), where,
                      [(_rows2d(g), lambda i, wh: (i // perh) * per + wh[0] * perh + i % perh),
                       (_rows2d(got), lambda i, wh: i)],
                      4 * rh, cols, _BF, lambda i, wh: i, 4 * perh, tm)
    return out.reshape(4, rh, cols)


def _sum_of_four(pair, land, where, total, layer, n_layers):
    _, rh, cols = pair.shape
    r = 2 * rh
    tm = _row_tile(rh)
    per, perh = r // tm, rh // tm

    def add_four(own, others):
        return ((own.astype(_F32) + others[0].astype(_F32)) + others[1].astype(_F32)) + others[2].astype(_F32)

    out = _placed_map("rs_add_four", add_four, where,
                      [(_rows2d(pair), lambda i, wh: wh[1] * perh + i), (land, lambda i, wh: i)],
                      n_layers * r, cols, _F32, lambda i, wh: layer * per + wh[0] * perh + i, perh, tm,
                      into=None if total is None else _rows2d(total))
    return out.reshape(n_layers, r, cols)


def _all_reduce_small(pack):
    rows, cols = pack.shape

    def body(p_ref, o_ref, land, send, recv):
        x, y, c = lax.axis_index("x"), lax.axis_index("y"), lax.axis_index("c")
        me = 4 * x + 2 * y + c
        land[me] = p_ref[...]
        copies = []
        for k in range(1, 8):
            bx, by, bc = (k >> 2) & 1, (k >> 1) & 1, k & 1
            peer = (1 - x if bx else x, 1 - y if by else y, 1 - c if bc else c)
            cp = _remote(p_ref, land.at[me], send.at[k - 1], recv.at[k - 1], peer)
            cp.start()
            copies.append((cp, 4 * peer[0] + 2 * peer[1] + peer[2]))
        for k, (cp, pid) in enumerate(copies):
            _remote(p_ref, land.at[pid], send.at[k], recv.at[k], (x, y, c)).wait_recv()
        for cp, _ in copies:
            cp.wait_send()
        acc = land[0]
        for d in range(1, 8):
            acc = acc + land[d]
        o_ref[...] = acc

    vm = pl.BlockSpec(memory_space=pltpu.VMEM)
    return pl.pallas_call(
        body, name="all_reduce_small", in_specs=[vm], out_specs=vm,
        out_shape=jax.ShapeDtypeStruct((rows, cols), _F32),
        scratch_shapes=[pltpu.VMEM((8, rows, cols), _F32), pltpu.SemaphoreType.DMA((7,)), pltpu.SemaphoreType.DMA((7,))],
    )(pack)


def _adamw_math(w, g, m, v):
    m2 = _ADAM_B1 * m + (1.0 - _ADAM_B1) * g
    v2 = _ADAM_B2 * v + (1.0 - _ADAM_B2) * (g * g)
    m_hat = m2 / (1.0 - _ADAM_B1 ** _ADAM_STEP)
    v_hat = v2 / (1.0 - _ADAM_B2 ** _ADAM_STEP)
    delta = -_ADAM_LR * (m_hat / (jnp.sqrt(v_hat) + _ADAM_EPS) + _ADAM_WD * w)
    return delta, m2, v2


def _adamw(w, g, m, v, lo=0, hi=None, into=None):
    shape = w.shape
    n_l = shape[0] if w.ndim == 3 else 1
    hi = n_l if hi is None else hi
    w2, g2, m2, v2 = (_rows2d(t) for t in (w, g, m, v))
    rows, cols = w2.shape
    per_layer = rows // n_l
    tm = _row_tile(per_layer)
    first, steps = lo * per_layer // tm, (hi - lo) * per_layer // tm
    n_into = 0 if into is None else 3

    def body(*refs):
        outs = refs[4 + n_into:]
        for ref, val in zip(outs, _adamw_math(*[r[...] for r in refs[:4]])):
            ref[...] = val

    spec = pl.BlockSpec((tm, cols), lambda i: (first + i, 0))
    sds = jax.ShapeDtypeStruct((rows, cols), _F32)
    outs = pl.pallas_call(
        body, name="adamw", grid=(steps,), in_specs=[spec] * 4 + [_ANY] * n_into, out_specs=[spec] * 3,
        out_shape=[sds] * 3, input_output_aliases={4 + k: k for k in range(n_into)},
        compiler_params=_params(("arbitrary",)),
    )(w2, g2, m2, v2, *([] if into is None else [_rows2d(t) for t in into]))
    return tuple(o.reshape(shape) for o in outs)


def kernel(x, positions, ffn1_norm, ffn1_w_in, ffn1_w_out, mix_norm, ffn2_norm, ffn2_w_in, ffn2_w_out, hyb_w_in, hyb_dw_w, hyb_dw_b, hyb_ln_g, hyb_ln_b, hyb_w_out, gdn_w_in, gdn_conv_w, gdn_A_log, gdn_dt_bias, gdn_norm_g, gdn_w_out, final_norm, loss_target, m_ffn1_norm, m_ffn1_w_in, m_ffn1_w_out, m_mix_norm, m_ffn2_norm, m_ffn2_w_in, m_ffn2_w_out, m_hyb_w_in, m_hyb_dw_w, m_hyb_dw_b, m_hyb_ln_g, m_hyb_ln_b, m_hyb_w_out, m_gdn_w_in, m_gdn_conv_w, m_gdn_A_log, m_gdn_dt_bias, m_gdn_norm_g, m_gdn_w_out, m_final_norm, v_ffn1_norm, v_ffn1_w_in, v_ffn1_w_out, v_mix_norm, v_ffn2_norm, v_ffn2_w_in, v_ffn2_w_out, v_hyb_w_in, v_hyb_dw_w, v_hyb_dw_b, v_hyb_ln_g, v_hyb_ln_b, v_hyb_w_out, v_gdn_w_in, v_gdn_conv_w, v_gdn_A_log, v_gdn_dt_bias, v_gdn_norm_g, v_gdn_w_out, v_final_norm):
    w = dict(ffn1_norm=ffn1_norm, ffn1_w_in=ffn1_w_in, ffn1_w_out=ffn1_w_out, mix_norm=mix_norm, ffn2_norm=ffn2_norm, ffn2_w_in=ffn2_w_in, ffn2_w_out=ffn2_w_out, hyb_w_in=hyb_w_in, hyb_dw_w=hyb_dw_w, hyb_dw_b=hyb_dw_b, hyb_ln_g=hyb_ln_g, hyb_ln_b=hyb_ln_b, hyb_w_out=hyb_w_out, gdn_w_in=gdn_w_in, gdn_conv_w=gdn_conv_w, gdn_A_log=gdn_A_log, gdn_dt_bias=gdn_dt_bias, gdn_norm_g=gdn_norm_g, gdn_w_out=gdn_w_out, final_norm=final_norm)
    m = dict(ffn1_norm=m_ffn1_norm, ffn1_w_in=m_ffn1_w_in, ffn1_w_out=m_ffn1_w_out, mix_norm=m_mix_norm, ffn2_norm=m_ffn2_norm, ffn2_w_in=m_ffn2_w_in, ffn2_w_out=m_ffn2_w_out, hyb_w_in=m_hyb_w_in, hyb_dw_w=m_hyb_dw_w, hyb_dw_b=m_hyb_dw_b, hyb_ln_g=m_hyb_ln_g, hyb_ln_b=m_hyb_ln_b, hyb_w_out=m_hyb_w_out, gdn_w_in=m_gdn_w_in, gdn_conv_w=m_gdn_conv_w, gdn_A_log=m_gdn_A_log, gdn_dt_bias=m_gdn_dt_bias, gdn_norm_g=m_gdn_norm_g, gdn_w_out=m_gdn_w_out, final_norm=m_final_norm)
    v = dict(ffn1_norm=v_ffn1_norm, ffn1_w_in=v_ffn1_w_in, ffn1_w_out=v_ffn1_w_out, mix_norm=v_mix_norm, ffn2_norm=v_ffn2_norm, ffn2_w_in=v_ffn2_w_in, ffn2_w_out=v_ffn2_w_out, hyb_w_in=v_hyb_w_in, hyb_dw_w=v_hyb_dw_w, hyb_dw_b=v_hyb_dw_b, hyb_ln_g=v_hyb_ln_g, hyb_ln_b=v_hyb_ln_b, hyb_w_out=v_hyb_w_out, gdn_w_in=v_gdn_w_in, gdn_conv_w=v_gdn_conv_w, gdn_A_log=v_gdn_A_log, gdn_dt_bias=v_gdn_dt_bias, gdn_norm_g=v_gdn_norm_g, gdn_w_out=v_gdn_w_out, final_norm=v_final_norm)
    return _train_step(x, positions, loss_target, w, m, v)


_WEIGHT_ORDER = ("ffn1_norm", "ffn1_w_in", "ffn1_w_out", "mix_norm", "ffn2_norm", "ffn2_w_in", "ffn2_w_out",
                 "hyb_w_in", "hyb_dw_w", "hyb_dw_b", "hyb_ln_g", "hyb_ln_b", "hyb_w_out", "gdn_w_in", "gdn_conv_w",
                 "gdn_A_log", "gdn_dt_bias", "gdn_norm_g", "gdn_w_out", "final_norm")
_BIG = ("ffn1_w_in", "ffn1_w_out", "ffn2_w_in", "ffn2_w_out", "hyb_w_in", "hyb_w_out", "gdn_w_in", "gdn_w_out")
_PACK_COLS = 1024


def _lane16(vec):
    return jnp.zeros((1, 128), _F32).at[0, _GDN_HEADS:2 * _GDN_HEADS].set(vec)


def _pack_rows(parts):
    rows, where, at = [], [], 0
    for p in parts:
        flat = p.reshape(-1).astype(_F32)
        n = -(-flat.shape[0] // (8 * _PACK_COLS)) * 8
        rows.append(jnp.pad(flat, (0, n * _PACK_COLS - flat.shape[0])).reshape(n, _PACK_COLS))
        where.append((at, n, flat.shape[0], p.shape))
        at += n
    return jnp.concatenate(rows, axis=0), where


def _unpack_rows(pack, where):
    return [pack[at:at + n].reshape(-1)[:size].reshape(shape) for at, n, size, shape in where]


def _train_step(x, positions, loss_target, w, m, v):
    n_layers = w["ffn1_norm"].shape[0]
    n_hyb, n_gdn = w["hyb_w_in"].shape[0], w["gdn_w_in"].shape[0]
    d_model = x.shape[-1]
    xh, target = x[0], loss_target[0]
    chip = 2 * lax.axis_index("x") + lax.axis_index("y")
    where = jnp.stack([lax.axis_index("c"), chip]).astype(jnp.int32)

    def own_columns(shard):
        cols = shard.shape[-1]
        zeros = jnp.zeros(shard.shape[:-1] + (4 * cols,), _F32)
        return lax.dynamic_update_slice_in_dim(zeros, shard, chip * cols, axis=2)

    small_pack, small_where = _pack_rows([own_columns(w["hyb_dw_w"]), own_columns(w["gdn_conv_w"])])
    small_full = _all_reduce_small(small_pack)
    dw_w, conv_w = _unpack_rows(0.5 * small_full, small_where)
    dw_w = jnp.pad(dw_w, ((0, 0), (0, _HALO - _CONV_WIDTH), (0, 0)))

    groups = (("ffn1_w_in", "ffn1_w_out"), ("hyb_w_in", "hyb_w_out"), ("ffn2_w_in", "ffn2_w_out"),
              ("gdn_w_in", "gdn_w_out"))
    steps = [(0, (0, 2)), (1, (0, 1)), (0, (1, 2)), (2, (0, 1)), (3, (0, 1))]
    bufs = [[_cast_slab(n, w[n], where) for n in names] for names in groups]
    pending, token = [], small_full
    for si, (gi, part) in enumerate(steps):
        send, recv, bufs[gi], token = _gather_start(f"gather_start_{si}", bufs[gi], token, part)
        pending.append((send, recv))
    gdn_cols = w["gdn_w_in"].shape[-1]
    n_qkvz = 4 * _GDN_HEADS * _GDN_DIM
    n_ba = 4 * gdn_cols - n_qkvz
    wv = {}

    def arrive(si, after):
        gi, part = steps[si]
        send, recv = pending[si]
        bufs[gi] = _gather_wait(f"gather_wait_{si}", send, recv, bufs[gi], after, part)
        bufs[gi] = _gather_forward(f"gather_pass_{si}", bufs[gi], part)
        a, b = bufs[gi]
        kind = groups[gi][0]
        if kind == "gdn_w_in":
            gdn_in = jnp.transpose(a, (0, 2, 1, 3)).reshape(n_gdn, d_model, 4 * gdn_cols)
            wv["gdn_qkvz"] = gdn_in[:, :, :n_qkvz]
            wv["gdn_ba"] = jnp.pad(gdn_in[:, :, n_qkvz:], ((0, 0), (0, 0), (0, 128 - n_ba)))
        else:
            wv[kind] = a
        wv[groups[gi][1]] = b.reshape(b.shape[0], -1, d_model)

    tabs = _rotary_tables(positions)

    def mixer_args(layer):
        i = layer // 2
        if layer % 2 == 0:
            return (wv["hyb_w_in"], wv["hyb_w_out"], dw_w[i], w["hyb_dw_b"][i][None], w["hyb_ln_g"][i][None],
                    w["hyb_ln_b"][i][None], tabs, i)
        return (wv["gdn_qkvz"], wv["gdn_ba"], wv["gdn_w_out"], conv_w[i], _lane16(w["gdn_A_log"][i]),
                _lane16(w["gdn_dt_bias"][i]), w["gdn_norm_g"][i][None], i)

    h = xh
    saved = []
    arrive(0, token)
    for layer in range(n_layers):
        if layer == 1:
            arrive(2, h)
        h, s1 = _ffn_fwd(h, w["ffn1_norm"][layer][None], wv["ffn1_w_in"], wv["ffn1_w_out"], layer)
        gamma = w["mix_norm"][layer][None]
        if layer % 2 == 0:
            if layer == 0:
                arrive(1, h)
            h, sm = _hyb_fwd(h, gamma, *mixer_args(layer))
        else:
            if layer == 1:
                arrive(4, h)
            h, sm = _gdn_fwd(h, gamma, *mixer_args(layer))
        if layer == 0:
            arrive(3, h)
        h, s2 = _ffn_fwd(h, w["ffn2_norm"][layer][None], wv["ffn2_w_in"], wv["ffn2_w_out"], layer)
        saved.append((s1, sm, s2))
    dh, loss_row, d_final = _loss_head(h, w["final_norm"][None], target)

    d_norm = {k: [None] * n_layers for k in ("ffn1_norm", "mix_norm", "ffn2_norm")}
    d_hyb = [None] * n_hyb
    d_gdn = [None] * n_gdn
    grads = {name: None for name in _BIG}
    n_of = {name: w[name].shape[0] for name in _BIG}

    def rs_begin(layer, full, token):
        names = list(full)
        got = _rs_sibling(f"rs_sib_{layer}", [full[n] for n in names])
        sums = [_pair_sum(full[n], g, where) for n, g in zip(names, got)]
        send, recv, sums, lands, token = _rs_chips_start(f"rs_chips_start_{layer}", sums, token)
        return (layer, names, send, recv, sums, lands), token

    def rs_end(state, after):
        layer, names, send, recv, sums, lands = state
        sums, lands = _rs_chips_wait(f"rs_chips_wait_{layer}", send, recv, sums, lands, after)
        index = [layer if n.startswith("ffn") else layer // 2 for n in names]
        for n, li, pair, land in zip(names, index, sums, lands):
            grads[n] = _sum_of_four(pair, land, where, grads[n], li, n_of[n])
        joined = _rs_join(f"rs_join_{layer}", [grads[n] for n in names], index)
        grads.update(zip(names, joined))

    in_flight = None
    for layer in reversed(range(n_layers)):
        s1, sm, s2 = saved[layer]
        full = {}
        dh, d_norm["ffn2_norm"][layer], full["ffn2_w_in"], full["ffn2_w_out"] = _ffn_bwd(
            dh, s2, w["ffn2_norm"][layer][None], wv["ffn2_w_in"], wv["ffn2_w_out"], layer)
        gamma = w["mix_norm"][layer][None]
        if layer % 2 == 0:
            dh, d_norm["mix_norm"][layer], d_hyb[layer // 2], full["hyb_w_in"], full["hyb_w_out"] = _hyb_bwd(
                dh, sm, gamma, *mixer_args(layer))
        else:
            dh, d_norm["mix_norm"][layer], d_gdn[layer // 2], full["gdn_w_in"], full["gdn_w_out"] = _gdn_bwd(
                dh, sm, gamma, *mixer_args(layer), n_ba)
        dh, d_norm["ffn1_norm"][layer], full["ffn1_w_in"], full["ffn1_w_out"] = _ffn_bwd(
            dh, s1, w["ffn1_norm"][layer][None], wv["ffn1_w_in"], wv["ffn1_w_out"], layer)
        if in_flight is not None:
            rs_end(in_flight, [dh])
        if layer == 0:
            stack = lambda rows: jnp.concatenate(rows, axis=0)
            misc = jnp.concatenate([stack([g[1] for g in d_gdn])[:, _GDN_HEADS:2 * _GDN_HEADS].reshape(-1),
                                    stack([g[2] for g in d_gdn])[:, _GDN_HEADS:2 * _GDN_HEADS].reshape(-1),
                                    stack([g[3] for g in d_gdn]).reshape(-1), loss_row[0, :1]])
            parts = [stack(d_norm["ffn1_norm"]), stack(d_norm["mix_norm"]), stack(d_norm["ffn2_norm"]), d_final,
                     stack([g[1] for g in d_hyb]), stack([g[2] for g in d_hyb]), stack([g[3] for g in d_hyb]), misc,
                     jnp.stack([g[0][:_CONV_WIDTH] for g in d_hyb]), jnp.stack([g[0] for g in d_gdn])]
            pack, pack_where = _pack_rows(parts)
            token = small_sum = _all_reduce_small(pack)
        in_flight, token = rs_begin(layer, full, token)
        if layer > 0:
            dh = _after(dh, token)
    grad_x = dh[None]

    red = _unpack_rows(small_sum, pack_where)
    n_a = n_gdn * _GDN_HEADS
    misc = red[7]
    cw = w["hyb_dw_w"].shape[-1]
    gw = w["gdn_conv_w"].shape[-1]
    grads.update(
        ffn1_norm=red[0], mix_norm=red[1], ffn2_norm=red[2], final_norm=red[3].reshape(-1),
        hyb_dw_b=red[4], hyb_ln_g=red[5], hyb_ln_b=red[6],
        gdn_A_log=misc[:n_a].reshape(n_gdn, _GDN_HEADS), gdn_dt_bias=misc[n_a:2 * n_a].reshape(n_gdn, _GDN_HEADS),
        gdn_norm_g=misc[2 * n_a:2 * n_a + n_gdn * _GDN_DIM].reshape(n_gdn, _GDN_DIM),
        hyb_dw_w=lax.dynamic_slice_in_dim(red[8], chip * cw, cw, axis=2),
        gdn_conv_w=lax.dynamic_slice_in_dim(red[9], chip * gw, gw, axis=2))
    loss = misc[2 * n_a + n_gdn * _GDN_DIM]

    small = [n for n in _WEIGHT_ORDER if n not in _BIG]
    packs = [_pack_rows([src[n] for n in small]) for src in (w, grads, m, v)]
    outs = _adamw(packs[0][0], _after(packs[1][0], token), packs[2][0], packs[3][0])
    for name in _BIG:
        grads[name] = _after(grads[name], token)
    delta, new_m, new_v = {}, {}, {}
    for res, o in zip((delta, new_m, new_v), outs):
        res.update(dict(zip(small, _unpack_rows(o, packs[0][1]))))
    late = in_flight[1]
    early = {name: _adamw(w[name], grads[name], m[name], v[name], lo=1 if name in late else 0) for name in _BIG}
    rs_end(in_flight, [token, outs[0], *[early[name][0] for name in _BIG]])
    for name in _BIG:
        res = early[name]
        if name in late:
            res = _adamw(w[name], grads[name], m[name], v[name], lo=0, hi=1, into=res)
        delta[name], new_m[name], new_v[name] = res
    return (loss, grad_x, *[grads[n] for n in _WEIGHT_ORDER], *[delta[n] for n in _WEIGHT_ORDER],
            *[new_m[n] for n in _WEIGHT_ORDER], *[new_v[n] for n in _WEIGHT_ORDER])
```

```python
import functools

import jax
import jax.numpy as jnp
from jax import lax
from jax.experimental import pallas as pl
from jax.experimental.pallas import tpu as pltpu

_BF = jnp.bfloat16
_F32 = jnp.float32
_NORM_EPS = 1e-6
_VMEM_LIMIT_BYTES = 56 * 1024 * 1024

_A_HEADS = 8
_A_HEAD_DIM = 64
_WIN_BLOCK = 128
_DILATIONS = (1, 4, 16)
_CONV_WIDTH = 31
_GDN_HEADS = 8
_GDN_DIM = 128
_GDN_CHUNK = 64
_GDN_SHORT_CONV = 4

_ADAM_LR = 0.001
_ADAM_B1 = 0.9
_ADAM_B2 = 0.999
_ADAM_EPS = 1e-08
_ADAM_WD = 0.01
_ADAM_STEP = 10


def _params(sem):
    return pltpu.CompilerParams(dimension_semantics=sem, vmem_limit_bytes=_VMEM_LIMIT_BYTES)


def _dot(a, b, dn=(((1,), (0,)), ((), ()))):
    return lax.dot_general(a.astype(_BF), b.astype(_BF), dn, preferred_element_type=_F32)


def _dot_nt(a, b):
    return _dot(a, b, (((1,), (1,)), ((), ())))


def _dot_tn(a, b):
    return _dot(a, b, (((0,), (0,)), ((), ())))


_NN = (((1,), (0,)), ((), ()))
_NT = (((1,), (1,)), ((), ()))
_TN = (((0,), (0,)), ((), ()))
_ANY = pl.BlockSpec(memory_space=pl.ANY)


def _mm(name, a, b, *, grid, a_spec, b_spec, dn, out_shape, out_spec, red=None, acc_shape=None,
        pre_a=None, pre_b=None, epi=None, extras=(), extra_specs=(), sum_shape=None):
    n_ex = len(extras)
    n_in = 2 + n_ex
    nk = grid[red] if red is not None else 1
    n_out = 1 if sum_shape is None else 2

    def body(*refs):
        a_ref, b_ref = refs[0], refs[1]
        ex = refs[2:2 + n_ex]
        o_ref = refs[n_in]
        av, bv = a_ref[...], b_ref[...]
        if pre_a is not None:
            av = pre_a(av)
        if pre_b is not None:
            bv = pre_b(bv)
        part = lax.dot_general(av.astype(_BF), bv.astype(_BF), dn, preferred_element_type=_F32)

        def finish(acc):
            res = epi(acc, *[e[...] for e in ex]) if epi is not None else acc
            if sum_shape is not None:
                res, term = res
                s_ref = refs[n_in + 1]
                first = functools.reduce(jnp.logical_and, [pl.program_id(ax) == 0 for ax in range(len(grid)) if ax != red])

                @pl.when(first)
                def _():
                    s_ref[...] = jnp.zeros_like(s_ref)

                s_ref[...] += term
            if isinstance(res, tuple):
                for idx, val in enumerate(res):
                    o_ref[idx] = val.astype(o_ref.dtype)
            else:
                o_ref[...] = res.astype(o_ref.dtype)

        if red is None:
            finish(part)
        else:
            acc_ref = refs[n_in + n_out]
            k = pl.program_id(red)

            @pl.when(k == 0)
            def _():
                acc_ref[...] = part

            @pl.when(k > 0)
            def _():
                acc_ref[...] += part

            @pl.when(k == nk - 1)
            def _():
                finish(acc_ref[...])

    if sum_shape is not None:
        out_shape = [out_shape, jax.ShapeDtypeStruct(sum_shape, _F32)]
        out_spec = [out_spec, pl.BlockSpec(sum_shape, lambda *g: (0,) * len(sum_shape))]
    return pl.pallas_call(
        body, name=name, grid=grid, in_specs=[a_spec, b_spec, *extra_specs], out_specs=out_spec, out_shape=out_shape,
        scratch_shapes=[pltpu.VMEM(acc_shape, _F32)] if red is not None else [],
        compiler_params=_params(("arbitrary",) * len(grid)),
    )(a, b, *extras)


def _rowmap(name, body, tiled, params, tile_outs, param_outs=(), *, rows, tm, ncol=1, col_outer=False,
            into=None):
    nt, npar, nto, npo = len(tiled), len(params), len(tile_outs), len(param_outs)
    assert rows % tm == 0, (name, rows, tm)
    nrow = rows // tm
    grid = (ncol, nrow) if col_outer else (nrow, ncol)

    def ij(g):
        return (g[1], g[0]) if col_outer else (g[0], g[1])

    def cidx(col, j):
        return col(j) if callable(col) else col

    in_specs, operands = [], []
    for arr, width, col in tiled:
        operands.append(arr)
        if arr.ndim == 3:
            in_specs.append(pl.BlockSpec((arr.shape[0], tm, width),
                                         lambda *g, col=col: (0, ij(g)[0], cidx(col, ij(g)[1]))))
        else:
            in_specs.append(pl.BlockSpec((tm, width), lambda *g, col=col: (ij(g)[0], cidx(col, ij(g)[1]))))
    for p in params:
        if isinstance(p, tuple):
            arr, width, col = p
            operands.append(arr)
            in_specs.append(pl.BlockSpec((arr.shape[0], width), lambda *g, col=col: (0, cidx(col, ij(g)[1]))))
        else:
            operands.append(p)
            in_specs.append(pl.BlockSpec(p.shape, lambda *g, nd=p.ndim: (0,) * nd))
    out_shape, out_specs = [], []
    for total, width, col, dt in tile_outs:
        out_shape.append(jax.ShapeDtypeStruct((rows, total), dt))
        out_specs.append(pl.BlockSpec((tm, width), lambda *g, col=col: (ij(g)[0], cidx(col, ij(g)[1]))))
    for shp in param_outs:
        out_shape.append(jax.ShapeDtypeStruct(shp, _F32))
        out_specs.append(pl.BlockSpec(shp, lambda *g, nd=len(shp): (0,) * nd))
    n_in = nt + npar
    aliases = {}
    if into is not None:
        operands.append(into[0])
        in_specs.append(_ANY)
        aliases = {n_in: into[1]}
        n_in += 1

    def kern(*refs):
        vals = [r[...] for r in refs[:nt + npar]]
        touts = refs[n_in:n_in + nto]
        pouts = refs[n_in + nto:]
        res = body(*vals)
        if not isinstance(res, (tuple, list)):
            res = (res,)
        for r, val in zip(touts, res[:nto]):
            r[...] = val.astype(r.dtype)
        if npo:
            @pl.when((pl.program_id(0) == 0) & (pl.program_id(1) == 0))
            def _():
                for r in pouts:
                    r[...] = jnp.zeros_like(r)

            for r, val in zip(pouts, res[nto:]):
                r[...] += val

    return pl.pallas_call(
        kern, name=name, grid=grid, in_specs=in_specs, out_specs=out_specs, out_shape=out_shape,
        input_output_aliases=aliases, compiler_params=_params(("arbitrary", "arbitrary")),
    )(*operands)


def _vjp_body(fn, n_in, diff, n_ct):
    def body(*vals):
        ins, cts = vals[:n_in], vals[n_in:n_in + n_ct]
        outs, pull = jax.vjp(fn, *ins)
        single = not isinstance(outs, (tuple, list))
        if single:
            grads = pull(cts[0].astype(outs.dtype))
        else:
            grads = pull(tuple(c.astype(o.dtype) for c, o in zip(cts, outs)))
        return tuple(grads[i] for i in diff)
    return body


_TM = 512
_TM_WIDE = 1024


def _rms(h, gamma):
    hf = h.astype(_F32)
    return hf * lax.rsqrt(jnp.mean(hf * hf, axis=-1, keepdims=True) + _NORM_EPS) * gamma


def _rmsnorm_fwd(h, gamma):
    s, d = h.shape
    return _rowmap("rmsnorm_fwd", lambda hv, gv: _rms(hv, gv).astype(_BF), [(h, d, 0)], [gamma],
                   [(d, d, 0, _BF)], rows=s, tm=_TM)[0]


def _norm_bwd_epi(dxn, hv, dhv, gv):
    _, pull = jax.vjp(_rms, hv, gv)
    dh, dg = pull(dxn)
    return dhv + dh, dg


def _norm_bwd_args(h, dh_out, gamma, tm):
    d = h.shape[1]
    row = pl.BlockSpec((tm, d), lambda i, j: (i, 0))
    return dict(extras=(h, dh_out, gamma), extra_specs=(row, row, pl.BlockSpec((1, d), lambda i, j: (0, 0))),
                sum_shape=(1, d))


def _silu(x):
    return x * jax.nn.sigmoid(x)


def _ffn_up(h, gamma, w_in, layer):
    s, d = h.shape
    ns = w_in.shape[-1]
    tm = _CONV_TM

    def body(h_ref, gam_ref, w_ref, x_ref, g_ref, u_ref, a_ref):
        xv = _rms(h_ref[...], gam_ref[...]).astype(_BF)
        x_ref[...] = xv
        for half in range(2):
            cols = slice(half * ns, (half + 1) * ns)
            g = jnp.dot(xv, w_ref[half], preferred_element_type=_F32)
            u = jnp.dot(xv, w_ref[half + 2], preferred_element_type=_F32)
            g_ref[:, cols] = g.astype(_BF)
            u_ref[:, cols] = u.astype(_BF)
            a_ref[:, cols] = (_silu(g) * u).astype(_BF)

    row = pl.BlockSpec((tm, d), lambda i: (i, 0))
    out = pl.BlockSpec((tm, 2 * ns), lambda i: (i, 0))
    sds = jax.ShapeDtypeStruct((s, 2 * ns), _BF)
    return pl.pallas_call(
        body, name="ffn_up", grid=(s // tm,),
        in_specs=[row, pl.BlockSpec((1, d), lambda i: (0, 0)), pl.BlockSpec((None, 4, d, ns), lambda i: (layer, 0, 0, 0))],
        out_specs=[row, out, out, out], out_shape=[jax.ShapeDtypeStruct((s, d), _BF), sds, sds, sds],
        compiler_params=_params(("arbitrary",)),
    )(h, gamma, w_in)


def _proj_residual(name, y, w, layer, h, scale):
    s, k = y.shape
    d = w.shape[-1]
    return _mm(name, y, w, grid=(s // _TM,), a_spec=pl.BlockSpec((_TM, k), lambda i: (i, 0)),
               b_spec=pl.BlockSpec((None, k, d), lambda i: (layer, 0, 0)), dn=_NN,
               out_shape=jax.ShapeDtypeStruct((s, d), _F32), out_spec=pl.BlockSpec((_TM, d), lambda i: (i, 0)),
               epi=lambda acc, hv: hv + scale * acc, extras=(h,),
               extra_specs=(pl.BlockSpec((_TM, d), lambda i: (i, 0)),))


def _ffn_dact(dh, w_out, layer, g, u):
    s, d = dh.shape
    f = g.shape[1]
    ns = f // 2

    def epi(da, gv, uv):
        gf, uf = gv.astype(_F32), uv.astype(_F32)
        sg = jax.nn.sigmoid(gf)
        return da * uf * (sg * (1.0 + gf * (1.0 - sg))), da * gf * sg

    tm = _CONV_TM
    gu = pl.BlockSpec((tm, f), lambda i: (i, 0))
    return _mm("ffn_dact", dh, w_out, grid=(s // tm,), a_spec=pl.BlockSpec((tm, d), lambda i: (i, 0)),
               b_spec=pl.BlockSpec((None, f, d), lambda i: (layer, 0, 0)), dn=_NT,
               out_shape=jax.ShapeDtypeStruct((2, s, f), _BF), out_spec=pl.BlockSpec((2, tm, f), lambda i: (0, i, 0)),
               pre_a=lambda v: 0.5 * v, epi=epi, extras=(g, u), extra_specs=(gu, gu))


def _wgrad_rows(name, y, dh, scale):
    s, k = y.shape
    d = dh.shape[1]
    out = _mm(name, y, dh, grid=(s // _TM,), a_spec=pl.BlockSpec((_TM, k), lambda i: (i, 0)),
              b_spec=pl.BlockSpec((_TM, d), lambda i: (i, 0)), dn=_TN, red=0, acc_shape=(k, d),
              out_shape=jax.ShapeDtypeStruct((k, d), _BF), out_spec=pl.BlockSpec((k, d), lambda i: (0, 0)),
              pre_b=(lambda v: scale * v) if scale != 1.0 else None)
    return out.reshape(4, k // 4, d)


def _ffn_dxn(dz, w_in, layer, h, gamma, dh_out):
    _, s, f = dz.shape
    d, ns = w_in.shape[2], w_in.shape[3]
    tm = _TM_WIDE
    return _mm("ffn_dxn", dz, w_in, grid=(s // tm, 4),
               a_spec=pl.BlockSpec((None, tm, ns), lambda i, j: (j // 2, i, j % 2)),
               b_spec=pl.BlockSpec((None, None, d, ns), lambda i, j: (layer, j, 0, 0)), dn=_NT, red=1,
               acc_shape=(tm, d), out_shape=jax.ShapeDtypeStruct((s, d), _F32),
               out_spec=pl.BlockSpec((tm, d), lambda i, j: (i, 0)), epi=_norm_bwd_epi,
               **_norm_bwd_args(h, dh_out, gamma, tm))


def _ffn_dwin(xn, dz):
    s, d = xn.shape
    ns = dz.shape[2] // 2
    tm = _TM_WIDE
    return _mm("ffn_dwin", xn, dz, grid=(4, s // tm), a_spec=pl.BlockSpec((tm, d), lambda j, i: (i, 0)),
               b_spec=pl.BlockSpec((None, tm, ns), lambda j, i: (j // 2, i, j % 2)), dn=_TN, red=1,
               acc_shape=(d, ns), out_shape=jax.ShapeDtypeStruct((4, d, ns), _BF),
               out_spec=pl.BlockSpec((None, d, ns), lambda j, i: (j, 0, 0)))


def _ffn_fwd(h, gamma, w_in, w_out, layer):
    xn, g, u, act = _ffn_up(h, gamma, w_in, layer)
    h_new = _proj_residual("ffn_down", act, w_out, layer, h, 0.5)
    return h_new, (h, xn, g, u, act)


def _ffn_bwd(dh, saved, gamma, w_in, w_out, layer):
    h, xn, g, u, act = saved
    dz = _ffn_dact(dh, w_out, layer, g, u)
    d_w_out = _wgrad_rows("ffn_dwout", act, dh, 0.5)
    dh_in, dgamma = _ffn_dxn(dz, w_in, layer, h, gamma, dh)
    d_w_in = _ffn_dwin(xn, dz)
    return dh_in, dgamma, d_w_in, d_w_out


def _loss_head(h, gamma, target):
    s, d = h.shape

    def body(hv, tv, gv):
        y, pull = jax.vjp(_rms, hv, gv)
        err = y - tv
        loss = 0.5 * jnp.sum(jnp.mean(err * err, axis=-1, keepdims=True), axis=0, keepdims=True)
        dh, dg = pull(err * (1.0 / d))
        return dh, jnp.broadcast_to(loss, (1, 128)), dg

    return _rowmap("loss_head", body, [(h, d, 0), (target, d, 0)], [gamma], [(d, d, 0, _F32)],
                   [(1, 128), (1, d)], rows=s, tm=_TM)


_ROT = _A_HEAD_DIM // 4
_ROPE_THETA = 500000.0
_A_WIDTH = _A_HEADS * _A_HEAD_DIM
_NEG = -1e30


def _rotary_tables(positions):
    inv_freq = jnp.power(jnp.float32(_ROPE_THETA), -jnp.arange(0, _ROT, 2, dtype=_F32) / _ROT)
    ang = positions.reshape(-1, 1).astype(_F32) * inv_freq
    cos, sin = jnp.cos(ang), jnp.sin(ang)
    s = ang.shape[0]
    pad = jnp.zeros((s, _A_HEAD_DIM - _ROT), _F32)
    c_head = jnp.concatenate([cos, cos, pad + 1.0], axis=1)
    s_head = jnp.concatenate([-sin, sin, pad], axis=1)
    c2, s2 = jnp.tile(c_head, (1, 2)), jnp.tile(s_head, (1, 2))
    scale = _A_HEAD_DIM ** -0.5
    tab_c = jnp.stack([scale * c2, c2, jnp.ones_like(c2)])
    tab_s = jnp.stack([scale * s2, s2, jnp.zeros_like(s2)])
    return tab_c, tab_s


def _pair_swap(z):
    lane = lax.broadcasted_iota(jnp.int32, z.shape, 1) % _A_HEAD_DIM
    half = _ROT // 2
    up = jnp.where(lane < _ROT, pltpu.roll(z, half, axis=1), 0.0)
    return jnp.where(lane < half, pltpu.roll(z, 128 - half, axis=1), up)


def _rot_apply(x, c, s, transpose):
    outs = []
    for k in range(x.shape[1] // 128):
        xs = x[:, k * 128:(k + 1) * 128]
        outs.append(c * xs + (_pair_swap(s * xs) if transpose else s * _pair_swap(xs)))
    return jnp.concatenate(outs, axis=1)


def _qkv_prep(proj, tab_c, tab_s):
    s = proj.shape[0]
    w = _A_WIDTH

    def body(x_ref, c_ref, s_ref, o_ref):
        o_ref[...] = _rot_apply(x_ref[...], c_ref[...], s_ref[...], False).astype(_BF)

    tab = pl.BlockSpec((None, _TM, 128), lambda i, j: (j, i, 0))
    blk = pl.BlockSpec((_TM, w), lambda i, j: (i, j))
    return pl.pallas_call(
        body, name="qkv_prep", grid=(s // _TM, 3), in_specs=[blk, tab, tab], out_specs=blk,
        out_shape=jax.ShapeDtypeStruct((s, 3 * w), _BF), compiler_params=_params(("arbitrary", "arbitrary")),
    )(proj, tab_c, tab_s)


def _head_masks():
    lane = lax.broadcasted_iota(jnp.int32, (_WIN_BLOCK, 128), 1)
    return [lane // _A_HEAD_DIM == hh for hh in range(2)]


def _band_masks(first_block):
    qi = lax.broadcasted_iota(jnp.int32, (_WIN_BLOCK, _WIN_BLOCK), 0)
    kj = lax.broadcasted_iota(jnp.int32, (_WIN_BLOCK, _WIN_BLOCK), 1)
    return (kj >= qi) & jnp.logical_not(first_block), kj <= qi


def _attn_fwd(qkv, dil):
    s = qkv.shape[0]
    w = _A_WIDTH
    length = s // dil
    nb = length // _WIN_BLOCK
    view = qkv.reshape(length, dil * 3 * w)

    def body(q_ref, kp_ref, kc_ref, vp_ref, vc_ref, o_ref, l_ref):
        allow_p, allow_c = _band_masks(pl.program_id(1) == 0)
        hm = _head_masks()
        for sl in range(w // 128):
            cols = slice(sl * 128, (sl + 1) * 128)
            q, kp, kc, vp, vc = q_ref[:, cols], kp_ref[:, cols], kc_ref[:, cols], vp_ref[:, cols], vc_ref[:, cols]
            o_s = jnp.zeros((_WIN_BLOCK, 128), _F32)
            l_s = jnp.zeros((_WIN_BLOCK, 128), _F32)
            for hh in range(2):
                qh = jnp.where(hm[hh], q, jnp.zeros_like(q))
                sp = jnp.where(allow_p, _dot_nt(qh, kp), _NEG)
                sc = jnp.where(allow_c, _dot_nt(qh, kc), _NEG)
                m = jnp.maximum(jnp.max(sp, axis=1, keepdims=True), jnp.max(sc, axis=1, keepdims=True))
                pp, pc = jnp.exp(sp - m), jnp.exp(sc - m)
                den = jnp.sum(pp, axis=1, keepdims=True) + jnp.sum(pc, axis=1, keepdims=True)
                oh = (_dot(pp, vp) + _dot(pc, vc)) / den
                o_s = jnp.where(hm[hh], oh, o_s)
                l_s = jnp.where(hm[hh], m + jnp.log(den), l_s)
            o_ref[:, cols] = o_s.astype(o_ref.dtype)
            l_ref[:, cols] = l_s

    def at(col, prev):
        if prev:
            return pl.BlockSpec((_WIN_BLOCK, w), lambda r, n: (jnp.maximum(n - 1, 0), 3 * r + col))
        return pl.BlockSpec((_WIN_BLOCK, w), lambda r, n: (n, 3 * r + col))

    out = pl.BlockSpec((_WIN_BLOCK, w), lambda r, n: (n, r))
    sds = jax.ShapeDtypeStruct((length, dil * w), _F32)
    o, lse = pl.pallas_call(
        body, name=f"attn_fwd_d{dil}", grid=(dil, nb),
        in_specs=[at(0, False), at(1, True), at(1, False), at(2, True), at(2, False)],
        out_specs=[out, out], out_shape=[jax.ShapeDtypeStruct(sds.shape, _BF), sds],
        compiler_params=_params(("arbitrary", "arbitrary")),
    )(view, view, view, view, view)
    return o.reshape(s, w), lse.reshape(s, w)


def _attn_bwd(qkv, o, lse, do, dlse, dil):
    s = qkv.shape[0]
    w = _A_WIDTH
    length = s // dil
    nb = length // _WIN_BLOCK
    view = qkv.reshape(length, dil * 3 * w)
    o, lse, do, dlse = (t.reshape(length, dil * w) for t in (o, lse, do, dlse))

    def body(q_ref, kp_ref, kc_ref, vp_ref, vc_ref, o_ref, l_ref, do_ref, dl_ref, out_ref,
             car_q, car_k, car_v, cur_q, cur_kc, cur_vc, cur_kp, cur_vp):
        n = pl.program_id(1)

        @pl.when(n == 0)
        def _():
            car_q[...] = jnp.zeros_like(car_q)
            car_k[...] = jnp.zeros_like(car_k)
            car_v[...] = jnp.zeros_like(car_v)

        @pl.when(n == nb)
        def _():
            cur_kp[...] = jnp.zeros_like(cur_kp)
            cur_vp[...] = jnp.zeros_like(cur_vp)

        @pl.when(n < nb)
        def _():
            allow_p, allow_c = _band_masks(n == 0)
            hm = _head_masks()
            for sl in range(w // 128):
                cols = slice(sl * 128, (sl + 1) * 128)
                q, kp, kc, vp, vc = q_ref[:, cols], kp_ref[:, cols], kc_ref[:, cols], vp_ref[:, cols], vc_ref[:, cols]
                ov, dov = o_ref[:, cols].astype(_F32), do_ref[:, cols].astype(_F32)
                lv, dlv = l_ref[:, cols], dl_ref[:, cols]
                zero = jnp.zeros((_WIN_BLOCK, 128), _F32)
                dq_s, dkp_s, dkc_s, dvp_s, dvc_s = zero, zero, zero, zero, zero
                for hh in range(2):
                    qh = jnp.where(hm[hh], q, jnp.zeros_like(q))
                    doh = jnp.where(hm[hh], dov, 0.0).astype(_BF)
                    lse_h = jnp.sum(jnp.where(hm[hh], lv, 0.0), axis=1, keepdims=True) * (1.0 / _A_HEAD_DIM)
                    delta = jnp.sum(jnp.where(hm[hh], dov * ov, 0.0), axis=1, keepdims=True)
                    dl = jnp.sum(jnp.where(hm[hh], dlv, 0.0), axis=1, keepdims=True)
                    pp = jnp.exp(jnp.where(allow_p, _dot_nt(qh, kp), _NEG) - lse_h)
                    pc = jnp.exp(jnp.where(allow_c, _dot_nt(qh, kc), _NEG) - lse_h)
                    dsp = (pp * (_dot_nt(doh, vp) - delta + dl)).astype(_BF)
                    dsc = (pc * (_dot_nt(doh, vc) - delta + dl)).astype(_BF)
                    dq_s = dq_s + jnp.where(hm[hh], _dot(dsp, kp) + _dot(dsc, kc), 0.0)
                    dkp_s = dkp_s + _dot_tn(dsp, qh)
                    dkc_s = dkc_s + _dot_tn(dsc, qh)
                    dvp_s = dvp_s + _dot_tn(pp, doh)
                    dvc_s = dvc_s + _dot_tn(pc, doh)
                cur_q[:, cols] = dq_s
                cur_kp[:, cols] = dkp_s
                cur_kc[:, cols] = dkc_s
                cur_vp[:, cols] = dvp_s
                cur_vc[:, cols] = dvc_s

        out_ref[:, 0:w] = car_q[...].astype(out_ref.dtype)
        out_ref[:, w:2 * w] = (car_k[...] + cur_kp[...]).astype(out_ref.dtype)
        out_ref[:, 2 * w:3 * w] = (car_v[...] + cur_vp[...]).astype(out_ref.dtype)

        @pl.when(n < nb)
        def _():
            car_q[...] = cur_q[...]
            car_k[...] = cur_kc[...]
            car_v[...] = cur_vc[...]

    def at(col, prev):
        if prev:
            return pl.BlockSpec((_WIN_BLOCK, w), lambda r, n: (jnp.maximum(jnp.minimum(n, nb - 1) - 1, 0), 3 * r + col))
        return pl.BlockSpec((_WIN_BLOCK, w), lambda r, n: (jnp.minimum(n, nb - 1), 3 * r + col))

    blk = pl.BlockSpec((_WIN_BLOCK, w), lambda r, n: (jnp.minimum(n, nb - 1), r))
    scratch = [pltpu.VMEM((_WIN_BLOCK, w), _F32) for _ in range(8)]
    out = pl.pallas_call(
        body, name=f"attn_bwd_d{dil}", grid=(dil, nb + 1),
        in_specs=[at(0, False), at(1, True), at(1, False), at(2, True), at(2, False), blk, blk, blk, blk],
        out_specs=pl.BlockSpec((_WIN_BLOCK, 3 * w), lambda r, n: (jnp.maximum(n - 1, 0), r)),
        out_shape=jax.ShapeDtypeStruct((length, dil * 3 * w), _BF), scratch_shapes=scratch,
        compiler_params=_params(("arbitrary", "arbitrary")),
    )(view, view, view, view, view, o, lse, do, dlse)
    return out.reshape(s, 3 * w)


def _mix(o1, o2, o3, l1, l2, l3):
    m = jnp.maximum(jnp.maximum(l1, l2), l3)
    e1, e2, e3 = jnp.exp(l1 - m), jnp.exp(l2 - m), jnp.exp(l3 - m)
    return (e1 * o1 + e2 * o2 + e3 * o3) / (e1 + e2 + e3)


def _attn_mix_fwd(outs, lses):
    s = outs[0].shape[0]
    tiles = [(t, _A_WIDTH, 0) for t in (*outs, *lses)]
    return _rowmap("attn_mix_fwd", _mix, tiles, [], [(2 * _A_WIDTH, _A_WIDTH, 0, _BF)], rows=s, tm=_CONV_TM)[0]


def _attn_mix_bwd(outs, lses, dy):
    s = outs[0].shape[0]
    tiles = [(t, _A_WIDTH, 0) for t in (*outs, *lses, dy)]
    body = _vjp_body(_mix, 6, range(6), 1)
    return _rowmap("attn_mix_bwd", body, tiles, [], [(_A_WIDTH, _A_WIDTH, 0, _BF)] * 3 + [(_A_WIDTH, _A_WIDTH, 0, _F32)] * 3, rows=s, tm=_CONV_TM)


_CONV_TM = 256
_HALO = 32


def _ln_silu(pre, g, b):
    mu = jnp.mean(pre, axis=-1, keepdims=True)
    xc = pre - mu
    y = xc * lax.rsqrt(jnp.mean(xc * xc, axis=-1, keepdims=True) + _NORM_EPS)
    return _silu(y * g + b)


def _glu_into(buf, ua_ref, ub_ref, uah_ref, ubh_ref, first):
    halo = uah_ref[...] * jax.nn.sigmoid(ubh_ref[...])
    buf[0:_HALO, :] = jnp.where(first, 0.0, halo)
    buf[_HALO:, :] = ua_ref[...] * jax.nn.sigmoid(ub_ref[...])


_SHIFT_ROWS = _CONV_TM + _HALO - 8


def _shift_copies(sh, buf):
    for b in range(1, 8):
        sh[b - 1] = buf[b:b + _SHIFT_ROWS, :]


def _rows_at(buf, sh, off, rows):
    a, b = divmod(off, 8)
    return buf[8 * a:8 * a + rows, :] if b == 0 else sh[b - 1, 8 * a:8 * a + rows, :]


def _causal_taps(buf, sh, w_ref, rows):
    base = _HALO - (_CONV_WIDTH - 1)
    acc = jnp.zeros((rows, buf.shape[1]), _F32)
    for j in range(_CONV_WIDTH):
        acc = acc + w_ref[j:j + 1, :] * _rows_at(buf, sh, base + j, rows)
    return acc


def _conv_specs(s):
    tm = _CONV_TM
    per = tm // _HALO
    cw = _A_WIDTH
    cur = lambda col: pl.BlockSpec((tm, cw), lambda i: (i, col))
    prev = lambda col: pl.BlockSpec((_HALO, cw), lambda i: (jnp.maximum(i * per - 1, 0), col))
    return tm, per, cw, cur, prev


def _full2(arr):
    return pl.BlockSpec(arr.shape, lambda i: (0, 0))


def _conv_fwd(proj, dw_w, dw_b, ln_g, ln_b, y):
    s = proj.shape[0]
    tm, per, cw, cur, prev = _conv_specs(s)

    def body(ua_ref, ub_ref, uah_ref, ubh_ref, w_ref, b_ref, g_ref, be_ref, y_in, o_ref, buf, sh):
        del y_in
        _glu_into(buf, ua_ref, ub_ref, uah_ref, ubh_ref, pl.program_id(0) == 0)
        _shift_copies(sh, buf)
        pre = _causal_taps(buf, sh, w_ref, tm) + b_ref[...]
        o_ref[...] = _ln_silu(pre, g_ref[...], be_ref[...]).astype(_BF)

    return pl.pallas_call(
        body, name="conv_fwd", grid=(s // tm,),
        in_specs=[cur(3), cur(4), prev(3), prev(4), _full2(dw_w), _full2(dw_b), _full2(ln_g), _full2(ln_b), _ANY],
        out_specs=pl.BlockSpec((tm, cw), lambda i: (i, 1)), out_shape=jax.ShapeDtypeStruct(y.shape, y.dtype),
        scratch_shapes=[pltpu.VMEM((tm + _HALO, cw), _F32), pltpu.VMEM((7, _SHIFT_ROWS, cw), _F32)],
        input_output_aliases={8: 0}, compiler_params=_params(("arbitrary",)),
    )(proj, proj, proj, proj, dw_w, dw_b, ln_g, ln_b, y)


def _conv_bwd_pre(proj, dw_w, dw_b, ln_g, ln_b, dy):
    s = proj.shape[0]
    tm, per, cw, cur, prev = _conv_specs(s)

    def body(ua_ref, ub_ref, uah_ref, ubh_ref, w_ref, b_ref, g_ref, be_ref, dy_ref, dp_ref, dg_ref, dbe_ref, db_ref, buf, sh):
        i = pl.program_id(0)
        _glu_into(buf, ua_ref, ub_ref, uah_ref, ubh_ref, i == 0)
        _shift_copies(sh, buf)
        pre = _causal_taps(buf, sh, w_ref, tm) + b_ref[...]
        _, pull = jax.vjp(_ln_silu, pre, g_ref[...], be_ref[...])
        dpre, dg, dbe = pull(dy_ref[...])
        dp_ref[...] = dpre

        @pl.when(i == 0)
        def _():
            dg_ref[...] = jnp.zeros_like(dg_ref)
            dbe_ref[...] = jnp.zeros_like(dbe_ref)
            db_ref[...] = jnp.zeros_like(db_ref)

        dg_ref[...] += dg
        dbe_ref[...] += dbe
        db_ref[...] += jnp.sum(dpre, axis=0, keepdims=True)

    vec = pl.BlockSpec((1, cw), lambda i: (0, 0))
    vsd = jax.ShapeDtypeStruct((1, cw), _F32)
    return pl.pallas_call(
        body, name="conv_bwd_pre", grid=(s // tm,),
        in_specs=[cur(3), cur(4), prev(3), prev(4), _full2(dw_w), _full2(dw_b), _full2(ln_g), _full2(ln_b), cur(1)],
        out_specs=[pl.BlockSpec((tm, cw), lambda i: (i, 0)), vec, vec, vec],
        out_shape=[jax.ShapeDtypeStruct((s, cw), _F32), vsd, vsd, vsd],
        scratch_shapes=[pltpu.VMEM((tm + _HALO, cw), _F32), pltpu.VMEM((7, _SHIFT_ROWS, cw), _F32)],
        compiler_params=_params(("arbitrary",)),
    )(proj, proj, proj, proj, dw_w, dw_b, ln_g, ln_b, dy)


def _conv_bwd_taps(proj, dw_w, dpre):
    s = proj.shape[0]
    tm, per, cw, cur, prev = _conv_specs(s)
    nt = s // tm

    def body(ua_ref, ub_ref, uah_ref, ubh_ref, w_ref, dp_ref, dpn_ref, dglu_ref, dw_ref, buf, dbuf, sh, dsh):
        i = pl.program_id(0)
        _glu_into(buf, ua_ref, ub_ref, uah_ref, ubh_ref, i == 0)
        dpre = dp_ref[...]
        dbuf[0:tm, :] = dpre
        dbuf[tm:, :] = jnp.where(i == nt - 1, 0.0, dpn_ref[...])
        _shift_copies(sh, buf)
        _shift_copies(dsh, dbuf)
        acc = jnp.zeros((tm, cw), _F32)
        for j in range(_CONV_WIDTH):
            acc = acc + w_ref[j:j + 1, :] * _rows_at(dbuf, dsh, _CONV_WIDTH - 1 - j, tm)
        dglu_ref[...] = acc

        @pl.when(i == 0)
        def _():
            dw_ref[...] = jnp.zeros_like(dw_ref)

        base = _HALO - (_CONV_WIDTH - 1)
        for j in range(_CONV_WIDTH):
            dw_ref[j:j + 1, :] += jnp.sum(dpre * _rows_at(buf, sh, base + j, tm), axis=0, keepdims=True)

    nxt = pl.BlockSpec((_HALO, cw), lambda i: (jnp.minimum((i + 1) * per, s // _HALO - 1), 0))
    return pl.pallas_call(
        body, name="conv_bwd_taps", grid=(nt,),
        in_specs=[cur(3), cur(4), prev(3), prev(4), _full2(dw_w), pl.BlockSpec((tm, cw), lambda i: (i, 0)), nxt],
        out_specs=[pl.BlockSpec((tm, cw), lambda i: (i, 0)), pl.BlockSpec((_HALO, cw), lambda i: (0, 0))],
        out_shape=[jax.ShapeDtypeStruct((s, cw), _F32), jax.ShapeDtypeStruct((_HALO, cw), _F32)],
        scratch_shapes=[pltpu.VMEM((tm + _HALO, cw), _F32), pltpu.VMEM((tm + _HALO, cw), _F32),
                        pltpu.VMEM((7, _SHIFT_ROWS, cw), _F32), pltpu.VMEM((7, _SHIFT_ROWS, cw), _F32)],
        compiler_params=_params(("arbitrary",)),
    )(proj, proj, proj, proj, dw_w, dpre, dpre)


def _hyb_dproj(dqkv, tab_c, tab_s, dglu, proj):
    s = proj.shape[0]
    w = _A_WIDTH

    def body(d1_ref, d2_ref, d3_ref, c_ref, s_ref, dg_ref, ua_ref, ub_ref, o_ref):
        j = pl.program_id(1)

        @pl.when(j < 3)
        def _():
            x = d1_ref[...].astype(_F32) + d2_ref[...].astype(_F32) + d3_ref[...].astype(_F32)
            o_ref[...] = _rot_apply(x, c_ref[...], s_ref[...], True).astype(_BF)

        @pl.when(j == 3)
        def _():
            o_ref[...] = (dg_ref[...] * jax.nn.sigmoid(ub_ref[...])).astype(_BF)

        @pl.when(j == 4)
        def _():
            sg = jax.nn.sigmoid(ub_ref[...])
            o_ref[...] = (dg_ref[...] * ua_ref[...] * sg * (1.0 - sg)).astype(_BF)

    dq = pl.BlockSpec((_TM, w), lambda i, j: (i, jnp.minimum(j, 2)))
    tab = pl.BlockSpec((None, _TM, 128), lambda i, j: (jnp.minimum(j, 2), i, 0))
    col = lambda c: pl.BlockSpec((_TM, w), lambda i, j: (i, c))
    return pl.pallas_call(
        body, name="hyb_dproj", grid=(s // _TM, 5),
        in_specs=[dq, dq, dq, tab, tab, col(0), col(3), col(4)],
        out_specs=pl.BlockSpec((_TM, w), lambda i, j: (i, j)),
        out_shape=jax.ShapeDtypeStruct((s, 5 * w), _BF), compiler_params=_params(("arbitrary", "arbitrary")),
    )(*dqkv, tab_c, tab_s, dglu, proj, proj)


def _proj_in(name, xn, w, layer):
    s, d = xn.shape
    ns = w.shape[-1]
    return _mm(name, xn, w, grid=(4, s // _TM), a_spec=pl.BlockSpec((_TM, d), lambda j, i: (i, 0)),
               b_spec=pl.BlockSpec((None, None, d, ns), lambda j, i: (layer, j, 0, 0)), dn=_NN,
               out_shape=jax.ShapeDtypeStruct((s, 4 * ns), _F32), out_spec=pl.BlockSpec((_TM, ns), lambda j, i: (i, j)))


def _proj_in_dx(name, dz, w, layer, h, gamma, dh_out):
    s = dz.shape[0]
    d, ns = w.shape[2], w.shape[3]
    return _mm(name, dz, w, grid=(s // _TM, 4), a_spec=pl.BlockSpec((_TM, ns), lambda i, j: (i, j)),
               b_spec=pl.BlockSpec((None, None, d, ns), lambda i, j: (layer, j, 0, 0)), dn=_NT, red=1,
               acc_shape=(_TM, d), out_shape=jax.ShapeDtypeStruct((s, d), _F32),
               out_spec=pl.BlockSpec((_TM, d), lambda i, j: (i, 0)), epi=_norm_bwd_epi,
               **_norm_bwd_args(h, dh_out, gamma, _TM))


def _proj_in_dw(name, xn, dz):
    s, d = xn.shape
    ns = dz.shape[1] // 4
    return _mm(name, xn, dz, grid=(4, s // _TM), a_spec=pl.BlockSpec((_TM, d), lambda j, i: (i, 0)),
               b_spec=pl.BlockSpec((_TM, ns), lambda j, i: (i, j)), dn=_TN, red=1, acc_shape=(d, ns),
               out_shape=jax.ShapeDtypeStruct((4, d, ns), _BF),
               out_spec=pl.BlockSpec((None, d, ns), lambda j, i: (j, 0, 0)))


def _proj_out_dy(name, dh, w, layer):
    s, d = dh.shape
    k = w.shape[1]
    return _mm(name, dh, w, grid=(s // _TM,), a_spec=pl.BlockSpec((_TM, d), lambda i: (i, 0)),
               b_spec=pl.BlockSpec((None, k, d), lambda i: (layer, 0, 0)), dn=_NT,
               out_shape=jax.ShapeDtypeStruct((s, k), _F32), out_spec=pl.BlockSpec((_TM, k), lambda i: (i, 0)))


def _hyb_fwd(h, gamma, w_in, w_out, dw_w, dw_b, ln_g, ln_b, tabs, li):
    hn = _rmsnorm_fwd(h, gamma)
    proj = _proj_in("hyb_in", hn, w_in, li)
    qkv = _qkv_prep(proj, *tabs)
    branches = [_attn_fwd(qkv, dil) for dil in _DILATIONS]
    outs, lses = [b[0] for b in branches], [b[1] for b in branches]
    y = _attn_mix_fwd(outs, lses)
    y = _conv_fwd(proj, dw_w, dw_b, ln_g, ln_b, y)
    h_new = _proj_residual("hyb_out", y, w_out, li, h, 1.0)
    return h_new, (h, hn, proj, qkv, outs, lses, y)


def _hyb_bwd(dh, saved, gamma, w_in, w_out, dw_w, dw_b, ln_g, ln_b, tabs, li):
    h, hn, proj, qkv, outs, lses, y = saved
    dy = _proj_out_dy("hyb_dy", dh, w_out, li)
    d_w_out = _wgrad_rows("hyb_dwout", y, dh, 1.0)
    mixg = _attn_mix_bwd(outs, lses, dy)
    dqkv = [_attn_bwd(qkv, outs[b], lses[b], mixg[b], mixg[3 + b], dil) for b, dil in enumerate(_DILATIONS)]
    dpre, d_ln_g, d_ln_b, d_dw_b = _conv_bwd_pre(proj, dw_w, dw_b, ln_g, ln_b, dy)
    dglu, d_dw_w = _conv_bwd_taps(proj, dw_w, dpre)
    dproj = _hyb_dproj(dqkv, *tabs, dglu, proj)
    dh_in, dgamma = _proj_in_dx("hyb_dhn", dproj, w_in, li, h, gamma, dh)
    d_w_in = _proj_in_dw("hyb_dwin", hn, dproj)
    return dh_in, dgamma, (d_dw_w, d_dw_b, d_ln_g, d_ln_b), d_w_in, d_w_out


_GDN_BLOCK = 128
_GDN_HEADS_PER_STEP = 8


def _gdn_chunk(qs, ks, vs, gates, states, heads):
    n = len(qs)
    hs = range(n)
    c = qs[0].shape[0]
    lane = lax.broadcasted_iota(jnp.int32, gates.shape, 1)
    row = lax.broadcasted_iota(jnp.int32, (c, c), 0)
    col = lax.broadcasted_iota(jnp.int32, (c, c), 1)
    rowc = lax.broadcasted_iota(jnp.int32, (c, 1), 0)
    causal = row >= col
    eye = jnp.where(row == col, 1.0, 0.0)
    beta = [jnp.sum(jnp.where(lane == heads[h], gates, 0.0), axis=1, keepdims=True) for h in hs]
    g = [jnp.sum(jnp.where(lane == heads[h] + _GDN_HEADS, gates, 0.0), axis=1, keepdims=True) for h in hs]
    gc_row = [jnp.sum(jnp.where(row <= col, g[h], 0.0), axis=0, keepdims=True) for h in hs]
    gc_col = [jnp.sum(jnp.where(row == col, gc_row[h], 0.0), axis=1, keepdims=True) for h in hs]
    decay = [jnp.where(causal, jnp.exp(jnp.where(causal, gc_col[h] - gc_row[h], 0.0)), 0.0) for h in hs]
    kb = [ks[h] * beta[h] for h in hs]
    l_mat = [jnp.where(row > col, _dot_nt(kb[h], ks[h]) * decay[h], 0.0) for h in hs]
    t_inv = [eye - l_mat[h] for h in hs]
    x = [_dot(l_mat[h], l_mat[h]) for h in hs]
    for it in range(5):
        t_inv = [t_inv[h] + _dot(t_inv[h], x[h]) for h in hs]
        if it < 4:
            x = [_dot(x[h], x[h]) for h in hs]
    eg = [jnp.exp(gc_col[h]) for h in hs]
    u = [_dot(t_inv[h], vs[h] * beta[h]) for h in hs]
    w = [_dot(t_inv[h], kb[h] * eg[h]) for h in hs]
    attn = [jnp.where(causal, _dot_nt(qs[h], ks[h]) * decay[h], 0.0) for h in hs]
    g_last = [jnp.sum(jnp.where(rowc == c - 1, gc_col[h], 0.0), axis=0, keepdims=True) for h in hs]
    k_dec = [ks[h] * jnp.exp(g_last[h] - gc_col[h]) for h in hs]
    v_new = [u[h] - _dot(w[h], states[h]) for h in hs]
    o = [_dot(qs[h] * eg[h], states[h]) + _dot(attn[h], v_new[h]) for h in hs]
    new_states = [states[h] * jnp.exp(g_last[h]) + _dot_tn(k_dec[h], v_new[h]) for h in hs]
    return o, new_states


def _gdn_chunk_fwd(q, k, v, gates):
    s = q.shape[0]
    cb, c = _GDN_BLOCK, _GDN_CHUNK
    nblk, per = s // cb, cb // c

    hps = _GDN_HEADS_PER_STEP

    def body(q_ref, k_ref, v_ref, g_ref, o_ref, st_ref, state):
        @pl.when(pl.program_id(1) == 0)
        def _():
            state[...] = jnp.zeros_like(state)

        heads = [pl.program_id(0) * hps + hh for hh in range(hps)]
        cols = [slice(hh * _GDN_DIM, (hh + 1) * _GDN_DIM) for hh in range(hps)]
        for ci in range(per):
            rows = slice(ci * c, (ci + 1) * c)
            sts = [state[hh] for hh in range(hps)]
            for hh in range(hps):
                st_ref[hh, ci] = sts[hh]
            outs, new = _gdn_chunk([q_ref[rows, cl] for cl in cols], [k_ref[rows, cl] for cl in cols],
                                   [v_ref[rows, cl] for cl in cols], g_ref[rows, :], sts, heads)
            for hh in range(hps):
                o_ref[rows, cols[hh]] = outs[hh]
                state[hh] = new[hh]

    slab = pl.BlockSpec((cb, hps * _GDN_DIM), lambda h, i: (i, h))
    return pl.pallas_call(
        body, name="gdn_chunk_fwd", grid=(_GDN_HEADS // hps, nblk),
        in_specs=[slab, slab, slab, pl.BlockSpec((cb, 128), lambda h, i: (i, 0))],
        out_specs=[slab, pl.BlockSpec((hps, per, _GDN_DIM, _GDN_DIM), lambda h, i: (h, i, 0, 0))],
        out_shape=[jax.ShapeDtypeStruct((s, _GDN_HEADS * _GDN_DIM), _F32),
                   jax.ShapeDtypeStruct((_GDN_HEADS, s // c, _GDN_DIM, _GDN_DIM), _F32)],
        scratch_shapes=[pltpu.VMEM((hps, _GDN_DIM, _GDN_DIM), _F32)],
        compiler_params=_params(("arbitrary", "arbitrary")),
    )(q, k, v, gates)


def _gdn_chunk_bwd(q, k, v, gates, states, do):
    s = q.shape[0]
    cb, c = _GDN_BLOCK, _GDN_CHUNK
    nblk, per = s // cb, cb // c

    hps = _GDN_HEADS_PER_STEP

    def body(q_ref, k_ref, v_ref, g_ref, st_ref, do_ref, dq_ref, dk_ref, dv_ref, dg_ref, dstate):
        @pl.when(pl.program_id(1) == 0)
        def _():
            dstate[...] = jnp.zeros_like(dstate)

        heads = [pl.program_id(0) * hps + hh for hh in range(hps)]
        cols = [slice(hh * _GDN_DIM, (hh + 1) * _GDN_DIM) for hh in range(hps)]
        for ci in reversed(range(per)):
            rows = slice(ci * c, (ci + 1) * c)
            fn = functools.partial(_gdn_chunk, heads=heads)
            _, pull = jax.vjp(fn, [q_ref[rows, cl] for cl in cols], [k_ref[rows, cl] for cl in cols],
                              [v_ref[rows, cl] for cl in cols], g_ref[rows, :], [st_ref[hh, ci] for hh in range(hps)])
            dq, dk, dv, dg, dst = pull(([do_ref[rows, cl] for cl in cols], [dstate[hh] for hh in range(hps)]))
            dg_ref[rows, :] = dg
            for hh in range(hps):
                dq_ref[rows, cols[hh]] = dq[hh]
                dk_ref[rows, cols[hh]] = dk[hh]
                dv_ref[rows, cols[hh]] = dv[hh]
                dstate[hh] = dst[hh]

    slab = pl.BlockSpec((cb, hps * _GDN_DIM), lambda h, i: (nblk - 1 - i, h))
    sds = jax.ShapeDtypeStruct((s, _GDN_HEADS * _GDN_DIM), _F32)
    return pl.pallas_call(
        body, name="gdn_chunk_bwd", grid=(_GDN_HEADS // hps, nblk),
        in_specs=[slab, slab, slab, pl.BlockSpec((cb, 128), lambda h, i: (nblk - 1 - i, 0)),
                  pl.BlockSpec((hps, per, _GDN_DIM, _GDN_DIM), lambda h, i: (h, nblk - 1 - i, 0, 0)), slab],
        out_specs=[slab, slab, slab, pl.BlockSpec((None, cb, 128), lambda h, i: (h, nblk - 1 - i, 0))],
        out_shape=[sds, sds, sds, jax.ShapeDtypeStruct((_GDN_HEADS // hps, s, 128), _F32)],
        scratch_shapes=[pltpu.VMEM((hps, _GDN_DIM, _GDN_DIM), _F32)],
        compiler_params=_params(("arbitrary", "arbitrary")),
    )(q, k, v, gates, states, do)


_SHALO = 8


def _swish_l2(c, normalize, scale):
    y = _silu(c)
    if normalize:
        y = y * lax.rsqrt(jnp.sum(y * y, axis=-1, keepdims=True) + _NORM_EPS) * scale
    return y


def _short_taps(buf, w_ref, rows):
    base = _SHALO - (_GDN_SHORT_CONV - 1)
    acc = jnp.zeros((rows, buf.shape[1]), _F32)
    for t in range(_GDN_SHORT_CONV):
        acc = acc + w_ref[t:t + 1, :] * buf[base + t:base + t + rows, :]
    return acc


_GDN_W = _GDN_HEADS * _GDN_DIM


def _gdn_conv_fwd(proj, conv_w, grp, normalize, scale):
    s = proj.shape[0]
    tm = _CONV_TM
    per = tm // _SHALO

    def body(x_ref, xp_ref, w_ref, o_ref, buf):
        buf[0:_SHALO, :] = jnp.where(pl.program_id(0) == 0, 0.0, xp_ref[...])
        buf[_SHALO:, :] = x_ref[...]
        conv = _short_taps(buf, w_ref, tm)
        for h in range(_GDN_HEADS):
            cols = slice(h * _GDN_DIM, (h + 1) * _GDN_DIM)
            o_ref[:, cols] = _swish_l2(conv[:, cols], normalize, scale)

    return pl.pallas_call(
        body, name=f"gdn_conv_fwd_{grp}", grid=(s // tm,),
        in_specs=[pl.BlockSpec((tm, _GDN_W), lambda i: (i, grp)),
                  pl.BlockSpec((_SHALO, _GDN_W), lambda i: (jnp.maximum(i * per - 1, 0), grp)),
                  pl.BlockSpec((_GDN_SHORT_CONV, _GDN_W), lambda i: (0, grp))],
        out_specs=pl.BlockSpec((tm, _GDN_W), lambda i: (i, 0)),
        out_shape=jax.ShapeDtypeStruct((s, _GDN_W), _F32),
        scratch_shapes=[pltpu.VMEM((tm + _SHALO, _GDN_W), _F32)], compiler_params=_params(("arbitrary",)),
    )(proj, proj, conv_w)


def _gdn_conv_bwd(proj, conv_w, grp, normalize, scale, dy, dproj):
    s = proj.shape[0]
    tm = _CONV_TM
    per = tm // _SHALO
    nt = s // tm
    taps = _GDN_SHORT_CONV

    def body(x_ref, xp_ref, xn_ref, w_ref, dy_ref, dyn_ref, dp_in, dx_ref, dw_ref, buf, dbuf):
        del dp_in
        i = pl.program_id(0)
        last = i == nt - 1
        buf[0:_SHALO, :] = jnp.where(i == 0, 0.0, xp_ref[...])
        buf[_SHALO:_SHALO + tm, :] = x_ref[...]
        buf[_SHALO + tm:, :] = jnp.where(last, 0.0, xn_ref[...])
        conv = _short_taps(buf, w_ref, tm + _SHALO)
        dy_ext = jnp.concatenate([dy_ref[...], jnp.where(last, 0.0, dyn_ref[...])], axis=0)
        for h in range(_GDN_HEADS):
            cols = slice(h * _GDN_DIM, (h + 1) * _GDN_DIM)
            _, pull = jax.vjp(lambda c: _swish_l2(c, normalize, scale), conv[:, cols])
            dbuf[:, cols] = pull(dy_ext[:, cols])[0]
        acc = jnp.zeros((tm, _GDN_W), _F32)
        for t in range(taps):
            o = taps - 1 - t
            acc = acc + w_ref[t:t + 1, :] * dbuf[o:o + tm, :]
        dx_ref[...] = acc.astype(dx_ref.dtype)

        @pl.when(i == 0)
        def _():
            dw_ref[...] = jnp.zeros_like(dw_ref)

        base = _SHALO - (taps - 1)
        for t in range(taps):
            dw_ref[t:t + 1, :] += jnp.sum(dbuf[0:tm, :] * buf[base + t:base + t + tm, :], axis=0, keepdims=True)

    cur = lambda col: pl.BlockSpec((tm, _GDN_W), lambda i: (i, col))
    nxt = lambda col: pl.BlockSpec((_SHALO, _GDN_W), lambda i: (jnp.minimum((i + 1) * per, s // _SHALO - 1), col))
    return pl.pallas_call(
        body, name=f"gdn_conv_bwd_{grp}", grid=(nt,),
        in_specs=[cur(grp), pl.BlockSpec((_SHALO, _GDN_W), lambda i: (jnp.maximum(i * per - 1, 0), grp)), nxt(grp),
                  pl.BlockSpec((taps, _GDN_W), lambda i: (0, grp)), cur(0), nxt(0), _ANY],
        out_specs=[cur(grp), pl.BlockSpec((_SHALO, _GDN_W), lambda i: (0, 0))],
        out_shape=[jax.ShapeDtypeStruct(dproj.shape, dproj.dtype), jax.ShapeDtypeStruct((_SHALO, _GDN_W), _F32)],
        scratch_shapes=[pltpu.VMEM((tm + 2 * _SHALO, _GDN_W), _F32), pltpu.VMEM((tm + _SHALO, _GDN_W), _F32)],
        input_output_aliases={6: 0}, compiler_params=_params(("arbitrary",)),
    )(proj, proj, proj, conv_w, dy, dy, dproj)


def _softplus(z):
    return jnp.maximum(z, 0.0) + jnp.log(1.0 + jnp.exp(-jnp.abs(z)))


def _gates(ba, a_lane, dt_lane):
    lane = lax.broadcasted_iota(jnp.int32, ba.shape, 1)
    g = -jnp.exp(a_lane) * _softplus(ba + dt_lane)
    return jnp.where(lane < _GDN_HEADS, jax.nn.sigmoid(ba), jnp.where(lane < 2 * _GDN_HEADS, g, 0.0))


def _gates_fwd(ba, a_lane, dt_lane):
    return _rowmap("gdn_gates_fwd", _gates, [(ba, 128, 0)], [a_lane, dt_lane], [(128, 128, 0, _F32)],
                   rows=ba.shape[0], tm=_TM)[0]


def _gates_bwd(ba, a_lane, dt_lane, dgates):
    def body(bav, dg8, av, dv):
        _, pull = jax.vjp(_gates, bav, av, dv)
        return pull(jnp.sum(dg8, axis=0))

    return _rowmap("gdn_gates_bwd", body, [(ba, 128, 0), (dgates, 128, 0)], [a_lane, dt_lane], [(128, 128, 0, _BF)],
                   [(1, 128), (1, 128)], rows=ba.shape[0], tm=_TM)


def _out_gate(o, z, g):
    return (_rms(o, g) * _silu(z)).astype(_BF)


def _head_cols():
    return [slice(h * _GDN_DIM, (h + 1) * _GDN_DIM) for h in range(_GDN_HEADS)]


def _out_gate_fwd(o, proj, norm_g):
    def body(ov, zv, gv):
        return jnp.concatenate([_out_gate(ov[:, cl], zv[:, cl], gv) for cl in _head_cols()], axis=1)

    return _rowmap("gdn_out_gate_fwd", body, [(o, _GDN_W, 0), (proj, _GDN_W, 3)], [norm_g], [(_GDN_W, _GDN_W, 0, _BF)],
                   rows=o.shape[0], tm=_CONV_TM)[0]


def _out_gate_bwd(o, proj, norm_g, dy):
    def body(ov, zv, dyv, gv):
        d_o, d_z, d_g = [], [], jnp.zeros_like(gv)
        for cl in _head_cols():
            _, pull = jax.vjp(_out_gate, ov[:, cl], zv[:, cl], gv)
            g_o, g_z, g_g = pull(dyv[:, cl].astype(_BF))
            d_o.append(g_o)
            d_z.append(g_z)
            d_g = d_g + g_g
        return jnp.concatenate(d_o, axis=1), jnp.concatenate(d_z, axis=1), d_g

    return _rowmap("gdn_out_gate_bwd", body, [(o, _GDN_W, 0), (proj, _GDN_W, 3), (dy, _GDN_W, 0)], [norm_g],
                   [(_GDN_W, _GDN_W, 0, _F32), (proj.shape[1], _GDN_W, 3, _BF)], [(1, _GDN_DIM)],
                   rows=o.shape[0], tm=_CONV_TM)


def _plain_in(name, xn, w, layer, tn):
    s, d = xn.shape
    n = w.shape[2]
    return _mm(name, xn, w, grid=(n // tn, s // _TM), a_spec=pl.BlockSpec((_TM, d), lambda j, i: (i, 0)),
               b_spec=pl.BlockSpec((None, d, tn), lambda j, i: (layer, 0, j)), dn=_NN,
               out_shape=jax.ShapeDtypeStruct((s, n), _F32), out_spec=pl.BlockSpec((_TM, tn), lambda j, i: (i, j)))


def _plain_dx(name, dz, w, layer, tn, norm=None):
    s, n = dz.shape
    d = w.shape[1]
    row = pl.BlockSpec((_TM, d), lambda i, j: (i, 0))
    kw = {}
    if norm is not None:
        extra, h, gamma, dh_out = norm
        kw = _norm_bwd_args(h, dh_out, gamma, _TM)
        kw.update(extras=(extra, *kw["extras"]), extra_specs=(row, *kw["extra_specs"]),
                  epi=lambda acc, e, hv, dhv, gv: _norm_bwd_epi(acc + e, hv, dhv, gv))
    return _mm(name, dz, w, grid=(s // _TM, n // tn), a_spec=pl.BlockSpec((_TM, tn), lambda i, j: (i, j)),
               b_spec=pl.BlockSpec((None, d, tn), lambda i, j: (layer, 0, j)), dn=_NT, red=1, acc_shape=(_TM, d),
               out_shape=jax.ShapeDtypeStruct((s, d), _F32), out_spec=row, **kw)


def _plain_dw(name, xn, dz, tn):
    s, d = xn.shape
    n = dz.shape[1]
    return _mm(name, xn, dz, grid=(n // tn, s // _TM), a_spec=pl.BlockSpec((_TM, d), lambda j, i: (i, 0)),
               b_spec=pl.BlockSpec((_TM, tn), lambda j, i: (i, j)), dn=_TN, red=1, acc_shape=(d, tn),
               out_shape=jax.ShapeDtypeStruct((d, n), _BF), out_spec=pl.BlockSpec((d, tn), lambda j, i: (0, j)))


_GDN_QK_SCALE = _GDN_DIM ** -0.5


def _gdn_fwd(h, gamma, w_qkvz, w_ba, w_out, conv_w, a_lane, dt_lane, norm_g, li):
    hn = _rmsnorm_fwd(h, gamma)
    proj = _plain_in("gdn_in", hn, w_qkvz, li, 1024)
    ba = _plain_in("gdn_in_ba", hn, w_ba, li, 128)
    q = _gdn_conv_fwd(proj, conv_w, 0, True, _GDN_QK_SCALE)
    k = _gdn_conv_fwd(proj, conv_w, 1, True, 1.0)
    v = _gdn_conv_fwd(proj, conv_w, 2, False, 1.0)
    gates = _gates_fwd(ba, a_lane, dt_lane)
    o, states = _gdn_chunk_fwd(q, k, v, gates)
    y = _out_gate_fwd(o, proj, norm_g)
    h_new = _proj_residual("gdn_out", y, w_out, li, h, 1.0)
    return h_new, (h, hn, proj, ba, q, k, v, gates, states, o, y)


def _gdn_bwd(dh, saved, gamma, w_qkvz, w_ba, w_out, conv_w, a_lane, dt_lane, norm_g, li, n_ba):
    h, hn, proj, ba, q, k, v, gates, states, o, y = saved
    dy = _proj_out_dy("gdn_dy", dh, w_out, li)
    d_w_out = _wgrad_rows("gdn_dwout", y, dh, 1.0)
    do, dproj, d_norm_g = _out_gate_bwd(o, proj, norm_g, dy)
    dq, dk, dv, dgates = _gdn_chunk_bwd(q, k, v, gates, states, do)
    dproj, dcw_q = _gdn_conv_bwd(proj, conv_w, 0, True, _GDN_QK_SCALE, dq, dproj)
    dproj, dcw_k = _gdn_conv_bwd(proj, conv_w, 1, True, 1.0, dk, dproj)
    dproj, dcw_v = _gdn_conv_bwd(proj, conv_w, 2, False, 1.0, dv, dproj)
    dba, d_a, d_dt = _gates_bwd(ba, a_lane, dt_lane, dgates)
    dhn_ba = _plain_dx("gdn_dhn_ba", dba, w_ba, li, 128)
    dh_in, dgamma = _plain_dx("gdn_dhn", dproj, w_qkvz, li, 1024, norm=(dhn_ba, h, gamma, dh))
    dw_qkvz = _plain_dw("gdn_dwin", hn, dproj, 1024)
    dw_ba = _plain_dw("gdn_dwin_ba", hn, dba, 128)
    d_conv_w = jnp.concatenate([dcw_q, dcw_k, dcw_v], axis=1)[:_GDN_SHORT_CONV]
    d_w_in = jnp.concatenate([dw_qkvz, dw_ba[:, :n_ba]], axis=1)
    d_w_in = jnp.transpose(d_w_in.reshape(d_w_in.shape[0], 4, -1), (1, 0, 2))
    return dh_in, dgamma, (d_conv_w, d_a, d_dt, d_norm_g), d_w_in, d_w_out


_MESH = pl.DeviceIdType.MESH
_ROW_TM = 384


def _place():
    x, y, c = lax.axis_index("x"), lax.axis_index("y"), lax.axis_index("c")
    chips = [(1 - x, y), (x, 1 - y), (1 - x, 1 - y)]
    return x, y, c, chips, [2 * px + py for px, py in chips]


def _remote(src, dst, send_sem, recv_sem, device):
    return pltpu.make_async_remote_copy(src_ref=src, dst_ref=dst, send_sem=send_sem, recv_sem=recv_sem,
                                        device_id=device, device_id_type=_MESH)


def _rows2d(t):
    return t.reshape(-1, t.shape[-1])


def _row_tile(rows):
    for tm in range(min(rows, _ROW_TM) // 8 * 8, 7, -8):
        if rows % tm == 0:
            return tm
    raise ValueError(f"no row tile for {rows} rows")


def _placed_map(name, fn, where, ins, out_rows, cols, out_dtype, out_row, steps, tm, into=None, after=None):
    n_in = len(ins)
    n_extra = (into is not None) + (after is not None)

    def kern(where_ref, *refs):
        del where_ref
        o_ref = refs[n_in + n_extra]
        o_ref[...] = fn(*[r[...] for r in refs[:n_in]]).astype(o_ref.dtype)

    in_specs, operands = [], [where]
    for arr, rowfn in ins:
        operands.append(arr)
        if arr.ndim == 3:
            in_specs.append(pl.BlockSpec((arr.shape[0], tm, cols), lambda i, wh, f=rowfn: (0, f(i, wh), 0)))
        else:
            in_specs.append(pl.BlockSpec((tm, cols), lambda i, wh, f=rowfn: (f(i, wh), 0)))
    aliases = {}
    if into is not None:
        operands.append(into)
        in_specs.append(_ANY)
        aliases = {n_in + 1: 0}
    if after is not None:
        operands.append(after)
        in_specs.append(_ANY)
    spec = pltpu.PrefetchScalarGridSpec(
        num_scalar_prefetch=1, grid=(steps,), in_specs=in_specs,
        out_specs=pl.BlockSpec((tm, cols), lambda i, wh: (out_row(i, wh), 0)))
    return pl.pallas_call(
        kern, name=name, grid_spec=spec, out_shape=jax.ShapeDtypeStruct((out_rows, cols), out_dtype),
        input_output_aliases=aliases, compiler_params=_params(("arbitrary",)),
    )(*operands)


def _cast_slab(name, t, where, after):
    n_l, r, cols = t.shape
    tm = _row_tile(r)
    per = r // tm
    buf = _placed_map("cast_" + name, lambda val: val, where, [(_rows2d(t), lambda i, wh: i)], n_l * 4 * r, cols, _BF,
                      lambda i, wh: ((i // per) * 4 + wh[1]) * per + i % per, n_l * per, tm, after=after)
    return buf.reshape(n_l, 4, r, cols)


_HBM = pl.BlockSpec(memory_space=pltpu.HBM)
_SEM = pl.BlockSpec(memory_space=pltpu.SEMAPHORE)
_EFFECT = pltpu.SideEffectType.DATAFLOW_SIDE_EFFECTING


def _core_layers(n_layers, c, part):
    half = n_layers // 2
    sub = half // part[1]
    return pl.ds(c * half + part[0] * sub, sub), pl.ds((1 - c) * half + part[0] * sub, sub)


def _ici_copies(refs, send, recv, arrivals, part):
    x, y, c, chips, idx = _place()
    me = 2 * x + y
    res = []
    for t, ref in enumerate(refs):
        mine, _ = _core_layers(ref.shape[0], c, part)
        for k in range(3):
            sem = 3 * t + k
            block = ref.at[mine, idx[k] if arrivals else me]
            res.append(_remote(block, block, send.at[sem], recv.at[sem], (*chips[k], c)))
    return res


def _gather_start(name, bufs, after, part):
    n = len(bufs)

    def body(*refs):
        send, recv = refs[n + 1], refs[n + 2]
        for cp in _ici_copies(refs[:n], send, recv, False, part):
            cp.start()
        refs[2 * n + 3][...] = jnp.zeros((8, 128), _F32)

    res = pl.pallas_call(
        body, name=name,
        out_shape=(pltpu.SemaphoreType.DMA((3 * n,)), pltpu.SemaphoreType.DMA((3 * n,)),
                   *[pltpu.HBM(b.shape, b.dtype) for b in bufs], jax.ShapeDtypeStruct((8, 128), _F32)),
        in_specs=[*[_HBM] * n, _ANY], out_specs=(_SEM, _SEM, *[_HBM] * n, pl.BlockSpec(memory_space=pltpu.VMEM)),
        input_output_aliases={t: 2 + t for t in range(n)},
        compiler_params=pltpu.CompilerParams(has_side_effects=_EFFECT),
    )(*[pltpu.with_memory_space_constraint(b, pltpu.HBM) for b in bufs], after)
    return res[0], res[1], list(res[2:2 + n]), res[2 + n]


def _gather_wait(name, send_sem, recv_sem, bufs, after, part):
    n = len(bufs)

    def body(*refs):
        send, recv = refs[n], refs[n + 1]
        for cp in _ici_copies(refs[:n], send, recv, False, part):
            cp.wait_send()
        for cp in _ici_copies(refs[:n], send, recv, True, part):
            cp.wait_recv()

    res = pl.pallas_call(
        body, name=name, out_shape=tuple(pltpu.HBM(b.shape, b.dtype) for b in bufs),
        in_specs=[*[_HBM] * n, _SEM, _SEM, _ANY], out_specs=tuple([_HBM] * n),
        input_output_aliases={t: t for t in range(n)},
        compiler_params=pltpu.CompilerParams(has_side_effects=_EFFECT),
    )(*bufs, send_sem, recv_sem, after)
    return list(res)


def _gather_forward(name, bufs, part):
    n = len(bufs)

    def body(*refs):
        outs, send, recv = refs[n:2 * n], refs[2 * n], refs[2 * n + 1]
        x, y, c, _, idx = _place()
        copies, arrive = [], []
        for t, ref in enumerate(outs):
            mine, other = _core_layers(ref.shape[0], c, part)
            for k in range(3):
                sem = 3 * t + k
                landed, dst = ref.at[mine, idx[k]], ref.at[other, idx[k]]
                copies.append(_remote(landed, landed, send.at[sem], recv.at[sem], (x, y, 1 - c)))
                arrive.append(_remote(dst, dst, send.at[sem], recv.at[sem], (x, y, 1 - c)))
        for cp in copies:
            cp.start()
        for cp in arrive:
            cp.wait_recv()
        for cp in copies:
            cp.wait_send()

    res = pl.pallas_call(
        body, name=name, in_specs=[_ANY] * n, out_specs=[_ANY] * n,
        out_shape=[jax.ShapeDtypeStruct(b.shape, b.dtype) for b in bufs],
        scratch_shapes=[pltpu.SemaphoreType.DMA((3 * n,)), pltpu.SemaphoreType.DMA((3 * n,))],
        input_output_aliases={t: t for t in range(n)},
    )(*bufs)
    return list(res)


def _rs_sibling(name, gs):
    n = len(gs)

    def body(*refs):
        send, recv = refs[2 * n], refs[2 * n + 1]
        x, y, c, _, _ = _place()
        copies = []
        for t in range(n):
            rh = refs[t].shape[1] // 2
            cp = _remote(refs[t].at[:, pl.ds((1 - c) * rh, rh)], refs[n + t], send.at[t], recv.at[t], (x, y, 1 - c))
            cp.start()
            copies.append(cp)
        for cp in copies:
            cp.wait()

    return pl.pallas_call(
        body, name=name, in_specs=[_ANY] * n, out_specs=[_ANY] * n,
        out_shape=[jax.ShapeDtypeStruct((4, g.shape[1] // 2, g.shape[2]), g.dtype) for g in gs],
        scratch_shapes=[pltpu.SemaphoreType.DMA((n,)), pltpu.SemaphoreType.DMA((n,))],
    )(*gs)


def _chip_copies(sums, lands, send, recv):
    x, y, c, chips, idx = _place()
    return [_remote(sums[t].at[idx[k]], lands[t].at[k], send.at[3 * t + k], recv.at[3 * t + k], (*chips[k], c))
            for t in range(len(sums)) for k in range(3)]


def _rs_chips_start(name, sums, after):
    n = len(sums)
    lands = [lax.empty((3,) + s.shape[1:], s.dtype) for s in sums]

    def body(*refs):
        send, recv = refs[2 * n + 1], refs[2 * n + 2]
        for cp in _chip_copies(refs[:n], refs[n:2 * n], send, recv):
            cp.start()
        refs[4 * n + 3][...] = jnp.zeros((8, 128), _F32)

    both = [*sums, *lands]
    res = pl.pallas_call(
        body, name=name,
        out_shape=(pltpu.SemaphoreType.DMA((3 * n,)), pltpu.SemaphoreType.DMA((3 * n,)),
                   *[pltpu.HBM(b.shape, b.dtype) for b in both], jax.ShapeDtypeStruct((8, 128), _F32)),
        in_specs=[*[_HBM] * (2 * n), _ANY],
        out_specs=(_SEM, _SEM, *[_HBM] * (2 * n), pl.BlockSpec(memory_space=pltpu.VMEM)),
        input_output_aliases={t: 2 + t for t in range(2 * n)},
        compiler_params=pltpu.CompilerParams(has_side_effects=_EFFECT),
    )(*[pltpu.with_memory_space_constraint(b, pltpu.HBM) for b in both], after)
    return res[0], res[1], list(res[2:2 + n]), list(res[2 + n:2 + 2 * n]), res[2 + 2 * n]


def _rs_chips_wait(name, send_sem, recv_sem, sums, lands, after):
    n = len(sums)

    def body(*refs):
        send, recv = refs[2 * n], refs[2 * n + 1]
        copies = _chip_copies(refs[:n], refs[n:2 * n], send, recv)
        for cp in copies:
            cp.wait_send()
        for cp in copies:
            cp.wait_recv()

    both = [*sums, *lands]
    res = pl.pallas_call(
        body, name=name, out_shape=tuple(pltpu.HBM(b.shape, b.dtype) for b in both),
        in_specs=[*[_HBM] * (2 * n), _SEM, _SEM, *[_ANY] * len(after)], out_specs=tuple([_HBM] * (2 * n)),
        input_output_aliases={t: t for t in range(2 * n)},
        compiler_params=pltpu.CompilerParams(has_side_effects=_EFFECT),
    )(*both, send_sem, recv_sem, *after)
    return list(res[:n]), list(res[n:])


def _after(x, token):
    def body(x_ref, t_ref, o_ref):
        del x_ref, t_ref, o_ref

    return pl.pallas_call(
        body, name="after", in_specs=[_ANY, _ANY], out_specs=_ANY, out_shape=jax.ShapeDtypeStruct(x.shape, x.dtype),
        input_output_aliases={0: 0},
    )(x, token)


def _rs_join(name, totals, layers):
    n = len(totals)

    def body(*refs):
        outs, send, recv = refs[n:2 * n], refs[2 * n], refs[2 * n + 1]
        x, y, c, _, _ = _place()
        copies, arrive = [], []
        for t in range(n):
            rh = outs[t].shape[1] // 2
            mine = outs[t].at[layers[t], pl.ds(c * rh, rh)]
            landing = outs[t].at[layers[t], pl.ds((1 - c) * rh, rh)]
            copies.append(_remote(mine, mine, send.at[t], recv.at[t], (x, y, 1 - c)))
            arrive.append(_remote(landing, landing, send.at[t], recv.at[t], (x, y, 1 - c)))
        for cp in copies:
            cp.start()
        for cp in arrive:
            cp.wait_recv()
        for cp in copies:
            cp.wait_send()

    return pl.pallas_call(
        body, name=name, in_specs=[_ANY] * n, out_specs=[_ANY] * n,
        out_shape=[jax.ShapeDtypeStruct(b.shape, b.dtype) for b in totals],
        scratch_shapes=[pltpu.SemaphoreType.DMA((n,)), pltpu.SemaphoreType.DMA((n,))],
        input_output_aliases={t: t for t in range(n)},
    )(*totals)


def _pair_sum(g, got, where):
    _, r, cols = g.shape
    rh = r // 2
    tm = _row_tile(rh)
    per, perh = r // tm, rh // tm
    out = _placed_map("rs_add_pair", lambda u, w: u.astype(_F32) + w.astype(_F32), where,
                      [(_rows2d(g), lambda i, wh: (i // perh) * per + wh[0] * perh + i % perh),
                       (_rows2d(got), lambda i, wh: i)],
                      4 * rh, cols, _BF, lambda i, wh: i, 4 * perh, tm)
    return out.reshape(4, rh, cols)


def _sum_of_four(pair, land, where, total, layer, n_layers):
    _, rh, cols = pair.shape
    r = 2 * rh
    tm = _row_tile(rh)
    per, perh = r // tm, rh // tm

    def add_four(own, others):
        return ((own.astype(_F32) + others[0].astype(_F32)) + others[1].astype(_F32)) + others[2].astype(_F32)

    out = _placed_map("rs_add_four", add_four, where,
                      [(_rows2d(pair), lambda i, wh: wh[1] * perh + i), (land, lambda i, wh: i)],
                      n_layers * r, cols, _F32, lambda i, wh: layer * per + wh[0] * perh + i, perh, tm,
                      into=None if total is None else _rows2d(total))
    return out.reshape(n_layers, r, cols)


def _all_reduce_small(pack):
    rows, cols = pack.shape

    def body(p_ref, o_ref, land, send, recv):
        x, y, c = lax.axis_index("x"), lax.axis_index("y"), lax.axis_index("c")
        me = 4 * x + 2 * y + c
        land[me] = p_ref[...]
        copies = []
        for k in range(1, 8):
            bx, by, bc = (k >> 2) & 1, (k >> 1) & 1, k & 1
            peer = (1 - x if bx else x, 1 - y if by else y, 1 - c if bc else c)
            cp = _remote(p_ref, land.at[me], send.at[k - 1], recv.at[k - 1], peer)
            cp.start()
            copies.append((cp, 4 * peer[0] + 2 * peer[1] + peer[2]))
        for k, (cp, pid) in enumerate(copies):
            _remote(p_ref, land.at[pid], send.at[k], recv.at[k], (x, y, c)).wait_recv()
        for cp, _ in copies:
            cp.wait_send()
        acc = land[0]
        for d in range(1, 8):
            acc = acc + land[d]
        o_ref[...] = acc

    vm = pl.BlockSpec(memory_space=pltpu.VMEM)
    return pl.pallas_call(
        body, name="all_reduce_small", in_specs=[vm], out_specs=vm,
        out_shape=jax.ShapeDtypeStruct((rows, cols), _F32),
        scratch_shapes=[pltpu.VMEM((8, rows, cols), _F32), pltpu.SemaphoreType.DMA((7,)), pltpu.SemaphoreType.DMA((7,))],
    )(pack)


def _adamw_math(w, g, m, v):
    m2 = _ADAM_B1 * m + (1.0 - _ADAM_B1) * g
    v2 = _ADAM_B2 * v + (1.0 - _ADAM_B2) * (g * g)
    m_hat = m2 / (1.0 - _ADAM_B1 ** _ADAM_STEP)
    v_hat = v2 / (1.0 - _ADAM_B2 ** _ADAM_STEP)
    delta = -_ADAM_LR * (m_hat / (jnp.sqrt(v_hat) + _ADAM_EPS) + _ADAM_WD * w)
    return delta, m2, v2


def _adamw(w, g, m, v, lo=0, hi=None, into=None):
    shape = w.shape
    n_l = shape[0] if w.ndim == 3 else 1
    hi = n_l if hi is None else hi
    w2, g2, m2, v2 = (_rows2d(t) for t in (w, g, m, v))
    rows, cols = w2.shape
    per_layer = rows // n_l
    tm = _row_tile(per_layer)
    first, steps = lo * per_layer // tm, (hi - lo) * per_layer // tm
    n_into = 0 if into is None else 3

    def body(*refs):
        outs = refs[4 + n_into:]
        for ref, val in zip(outs, _adamw_math(*[r[...] for r in refs[:4]])):
            ref[...] = val

    spec = pl.BlockSpec((tm, cols), lambda i: (first + i, 0))
    sds = jax.ShapeDtypeStruct((rows, cols), _F32)
    outs = pl.pallas_call(
        body, name="adamw", grid=(steps,), in_specs=[spec] * 4 + [_ANY] * n_into, out_specs=[spec] * 3,
        out_shape=[sds] * 3, input_output_aliases={4 + k: k for k in range(n_into)},
        compiler_params=_params(("arbitrary",)),
    )(w2, g2, m2, v2, *([] if into is None else [_rows2d(t) for t in into]))
    return tuple(o.reshape(shape) for o in outs)


def kernel(x, positions, ffn1_norm, ffn1_w_in, ffn1_w_out, mix_norm, ffn2_norm, ffn2_w_in, ffn2_w_out, hyb_w_in, hyb_dw_w, hyb_dw_b, hyb_ln_g, hyb_ln_b, hyb_w_out, gdn_w_in, gdn_conv_w, gdn_A_log, gdn_dt_bias, gdn_norm_g, gdn_w_out, final_norm, loss_target, m_ffn1_norm, m_ffn1_w_in, m_ffn1_w_out, m_mix_norm, m_ffn2_norm, m_ffn2_w_in, m_ffn2_w_out, m_hyb_w_in, m_hyb_dw_w, m_hyb_dw_b, m_hyb_ln_g, m_hyb_ln_b, m_hyb_w_out, m_gdn_w_in, m_gdn_conv_w, m_gdn_A_log, m_gdn_dt_bias, m_gdn_norm_g, m_gdn_w_out, m_final_norm, v_ffn1_norm, v_ffn1_w_in, v_ffn1_w_out, v_mix_norm, v_ffn2_norm, v_ffn2_w_in, v_ffn2_w_out, v_hyb_w_in, v_hyb_dw_w, v_hyb_dw_b, v_hyb_ln_g, v_hyb_ln_b, v_hyb_w_out, v_gdn_w_in, v_gdn_conv_w, v_gdn_A_log, v_gdn_dt_bias, v_gdn_norm_g, v_gdn_w_out, v_final_norm):
    w = dict(ffn1_norm=ffn1_norm, ffn1_w_in=ffn1_w_in, ffn1_w_out=ffn1_w_out, mix_norm=mix_norm, ffn2_norm=ffn2_norm, ffn2_w_in=ffn2_w_in, ffn2_w_out=ffn2_w_out, hyb_w_in=hyb_w_in, hyb_dw_w=hyb_dw_w, hyb_dw_b=hyb_dw_b, hyb_ln_g=hyb_ln_g, hyb_ln_b=hyb_ln_b, hyb_w_out=hyb_w_out, gdn_w_in=gdn_w_in, gdn_conv_w=gdn_conv_w, gdn_A_log=gdn_A_log, gdn_dt_bias=gdn_dt_bias, gdn_norm_g=gdn_norm_g, gdn_w_out=gdn_w_out, final_norm=final_norm)
    m = dict(ffn1_norm=m_ffn1_norm, ffn1_w_in=m_ffn1_w_in, ffn1_w_out=m_ffn1_w_out, mix_norm=m_mix_norm, ffn2_norm=m_ffn2_norm, ffn2_w_in=m_ffn2_w_in, ffn2_w_out=m_ffn2_w_out, hyb_w_in=m_hyb_w_in, hyb_dw_w=m_hyb_dw_w, hyb_dw_b=m_hyb_dw_b, hyb_ln_g=m_hyb_ln_g, hyb_ln_b=m_hyb_ln_b, hyb_w_out=m_hyb_w_out, gdn_w_in=m_gdn_w_in, gdn_conv_w=m_gdn_conv_w, gdn_A_log=m_gdn_A_log, gdn_dt_bias=m_gdn_dt_bias, gdn_norm_g=m_gdn_norm_g, gdn_w_out=m_gdn_w_out, final_norm=m_final_norm)
    v = dict(ffn1_norm=v_ffn1_norm, ffn1_w_in=v_ffn1_w_in, ffn1_w_out=v_ffn1_w_out, mix_norm=v_mix_norm, ffn2_norm=v_ffn2_norm, ffn2_w_in=v_ffn2_w_in, ffn2_w_out=v_ffn2_w_out, hyb_w_in=v_hyb_w_in, hyb_dw_w=v_hyb_dw_w, hyb_dw_b=v_hyb_dw_b, hyb_ln_g=v_hyb_ln_g, hyb_ln_b=v_hyb_ln_b, hyb_w_out=v_hyb_w_out, gdn_w_in=v_gdn_w_in, gdn_conv_w=v_gdn_conv_w, gdn_A_log=v_gdn_A_log, gdn_dt_bias=v_gdn_dt_bias, gdn_norm_g=v_gdn_norm_g, gdn_w_out=v_gdn_w_out, final_norm=v_final_norm)
    return _train_step(x, positions, loss_target, w, m, v)


_WEIGHT_ORDER = ("ffn1_norm", "ffn1_w_in", "ffn1_w_out", "mix_norm", "ffn2_norm", "ffn2_w_in", "ffn2_w_out",
                 "hyb_w_in", "hyb_dw_w", "hyb_dw_b", "hyb_ln_g", "hyb_ln_b", "hyb_w_out", "gdn_w_in", "gdn_conv_w",
                 "gdn_A_log", "gdn_dt_bias", "gdn_norm_g", "gdn_w_out", "final_norm")
_BIG = ("ffn1_w_in", "ffn1_w_out", "ffn2_w_in", "ffn2_w_out", "hyb_w_in", "hyb_w_out", "gdn_w_in", "gdn_w_out")
_PACK_COLS = 1024


def _lane16(vec):
    return jnp.zeros((1, 128), _F32).at[0, _GDN_HEADS:2 * _GDN_HEADS].set(vec)


def _pack_rows(parts):
    rows, where, at = [], [], 0
    for p in parts:
        flat = p.reshape(-1).astype(_F32)
        n = -(-flat.shape[0] // (8 * _PACK_COLS)) * 8
        rows.append(jnp.pad(flat, (0, n * _PACK_COLS - flat.shape[0])).reshape(n, _PACK_COLS))
        where.append((at, n, flat.shape[0], p.shape))
        at += n
    return jnp.concatenate(rows, axis=0), where


def _unpack_rows(pack, where):
    return [pack[at:at + n].reshape(-1)[:size].reshape(shape) for at, n, size, shape in where]


def _train_step(x, positions, loss_target, w, m, v):
    n_layers = w["ffn1_norm"].shape[0]
    n_hyb, n_gdn = w["hyb_w_in"].shape[0], w["gdn_w_in"].shape[0]
    d_model = x.shape[-1]
    xh, target = x[0], loss_target[0]
    chip = 2 * lax.axis_index("x") + lax.axis_index("y")
    where = jnp.stack([lax.axis_index("c"), chip]).astype(jnp.int32)

    def own_columns(shard):
        cols = shard.shape[-1]
        zeros = jnp.zeros(shard.shape[:-1] + (4 * cols,), _F32)
        return lax.dynamic_update_slice_in_dim(zeros, shard, chip * cols, axis=2)

    small_pack, small_where = _pack_rows([own_columns(w["hyb_dw_w"]), own_columns(w["gdn_conv_w"])])
    small_full = _all_reduce_small(small_pack)
    dw_w, conv_w = _unpack_rows(0.5 * small_full, small_where)
    dw_w = jnp.pad(dw_w, ((0, 0), (0, _HALO - _CONV_WIDTH), (0, 0)))

    groups = (("ffn1_w_in", "ffn1_w_out"), ("hyb_w_in", "hyb_w_out"), ("ffn2_w_in", "ffn2_w_out"),
              ("gdn_w_in", "gdn_w_out"))
    steps = [(0, (0, 2)), (1, (0, 1)), (0, (1, 2)), (2, (0, 1)), (3, (0, 1))]
    bufs = [None] * len(groups)
    pending, token = [], small_full
    for si, (gi, part) in enumerate(steps):
        if bufs[gi] is None:
            bufs[gi] = [_cast_slab(n, w[n], where, token) for n in groups[gi]]
        send, recv, bufs[gi], token = _gather_start(f"gather_start_{si}", bufs[gi], token, part)
        pending.append((send, recv))
    gdn_cols = w["gdn_w_in"].shape[-1]
    n_qkvz = 4 * _GDN_HEADS * _GDN_DIM
    n_ba = 4 * gdn_cols - n_qkvz
    wv = {}

    def arrive(si, after):
        gi, part = steps[si]
        send, recv = pending[si]
        bufs[gi] = _gather_wait(f"gather_wait_{si}", send, recv, bufs[gi], after, part)
        bufs[gi] = _gather_forward(f"gather_pass_{si}", bufs[gi], part)
        a, b = bufs[gi]
        kind = groups[gi][0]
        if kind == "gdn_w_in":
            gdn_in = jnp.transpose(a, (0, 2, 1, 3)).reshape(n_gdn, d_model, 4 * gdn_cols)
            wv["gdn_qkvz"] = gdn_in[:, :, :n_qkvz]
            wv["gdn_ba"] = jnp.pad(gdn_in[:, :, n_qkvz:], ((0, 0), (0, 0), (0, 128 - n_ba)))
        else:
            wv[kind] = a
        wv[groups[gi][1]] = b.reshape(b.shape[0], -1, d_model)

    tabs = _rotary_tables(positions)

    def mixer_args(layer):
        i = layer // 2
        if layer % 2 == 0:
            return (wv["hyb_w_in"], wv["hyb_w_out"], dw_w[i], w["hyb_dw_b"][i][None], w["hyb_ln_g"][i][None],
                    w["hyb_ln_b"][i][None], tabs, i)
        return (wv["gdn_qkvz"], wv["gdn_ba"], wv["gdn_w_out"], conv_w[i], _lane16(w["gdn_A_log"][i]),
                _lane16(w["gdn_dt_bias"][i]), w["gdn_norm_g"][i][None], i)

    h = xh
    saved = []
    arrive(0, token)
    for layer in range(n_layers):
        if layer == 1:
            arrive(2, h)
        h, s1 = _ffn_fwd(h, w["ffn1_norm"][layer][None], wv["ffn1_w_in"], wv["ffn1_w_out"], layer)
        gamma = w["mix_norm"][layer][None]
        if layer % 2 == 0:
            if layer == 0:
                arrive(1, h)
            h, sm = _hyb_fwd(h, gamma, *mixer_args(layer))
        else:
            if layer == 1:
                arrive(4, h)
            h, sm = _gdn_fwd(h, gamma, *mixer_args(layer))
        if layer == 0:
            arrive(3, h)
        h, s2 = _ffn_fwd(h, w["ffn2_norm"][layer][None], wv["ffn2_w_in"], wv["ffn2_w_out"], layer)
        saved.append((s1, sm, s2))
    dh, loss_row, d_final = _loss_head(h, w["final_norm"][None], target)

    d_norm = {k: [None] * n_layers for k in ("ffn1_norm", "mix_norm", "ffn2_norm")}
    d_hyb = [None] * n_hyb
    d_gdn = [None] * n_gdn
    grads = {name: None for name in _BIG}
    n_of = {name: w[name].shape[0] for name in _BIG}

    def rs_begin(layer, full, token):
        names = list(full)
        got = _rs_sibling(f"rs_sib_{layer}", [full[n] for n in names])
        sums = [_pair_sum(full[n], g, where) for n, g in zip(names, got)]
        send, recv, sums, lands, token = _rs_chips_start(f"rs_chips_start_{layer}", sums, token)
        return (layer, names, send, recv, sums, lands), token

    def rs_end(state, after):
        layer, names, send, recv, sums, lands = state
        sums, lands = _rs_chips_wait(f"rs_chips_wait_{layer}", send, recv, sums, lands, after)
        index = [layer if n.startswith("ffn") else layer // 2 for n in names]
        for n, li, pair, land in zip(names, index, sums, lands):
            grads[n] = _sum_of_four(pair, land, where, grads[n], li, n_of[n])
        joined = _rs_join(f"rs_join_{layer}", [grads[n] for n in names], index)
        grads.update(zip(names, joined))

    in_flight = None
    for layer in reversed(range(n_layers)):
        s1, sm, s2 = saved[layer]
        full = {}
        dh, d_norm["ffn2_norm"][layer], full["ffn2_w_in"], full["ffn2_w_out"] = _ffn_bwd(
            dh, s2, w["ffn2_norm"][layer][None], wv["ffn2_w_in"], wv["ffn2_w_out"], layer)
        gamma = w["mix_norm"][layer][None]
        if layer % 2 == 0:
            dh, d_norm["mix_norm"][layer], d_hyb[layer // 2], full["hyb_w_in"], full["hyb_w_out"] = _hyb_bwd(
                dh, sm, gamma, *mixer_args(layer))
        else:
            dh, d_norm["mix_norm"][layer], d_gdn[layer // 2], full["gdn_w_in"], full["gdn_w_out"] = _gdn_bwd(
                dh, sm, gamma, *mixer_args(layer), n_ba)
        dh, d_norm["ffn1_norm"][layer], full["ffn1_w_in"], full["ffn1_w_out"] = _ffn_bwd(
            dh, s1, w["ffn1_norm"][layer][None], wv["ffn1_w_in"], wv["ffn1_w_out"], layer)
        if in_flight is not None:
            rs_end(in_flight, [dh])
        if layer == 0:
            stack = lambda rows: jnp.concatenate(rows, axis=0)
            misc = jnp.concatenate([stack([g[1] for g in d_gdn])[:, _GDN_HEADS:2 * _GDN_HEADS].reshape(-1),
                                    stack([g[2] for g in d_gdn])[:, _GDN_HEADS:2 * _GDN_HEADS].reshape(-1),
                                    stack([g[3] for g in d_gdn]).reshape(-1), loss_row[0, :1]])
            parts = [stack(d_norm["ffn1_norm"]), stack(d_norm["mix_norm"]), stack(d_norm["ffn2_norm"]), d_final,
                     stack([g[1] for g in d_hyb]), stack([g[2] for g in d_hyb]), stack([g[3] for g in d_hyb]), misc,
                     jnp.stack([g[0][:_CONV_WIDTH] for g in d_hyb]), jnp.stack([g[0] for g in d_gdn])]
            pack, pack_where = _pack_rows(parts)
            token = small_sum = _all_reduce_small(pack)
        in_flight, token = rs_begin(layer, full, token)
        if layer > 0:
            dh = _after(dh, token)
    grad_x = dh[None]

    red = _unpack_rows(small_sum, pack_where)
    n_a = n_gdn * _GDN_HEADS
    misc = red[7]
    cw = w["hyb_dw_w"].shape[-1]
    gw = w["gdn_conv_w"].shape[-1]
    grads.update(
        ffn1_norm=red[0], mix_norm=red[1], ffn2_norm=red[2], final_norm=red[3].reshape(-1),
        hyb_dw_b=red[4], hyb_ln_g=red[5], hyb_ln_b=red[6],
        gdn_A_log=misc[:n_a].reshape(n_gdn, _GDN_HEADS), gdn_dt_bias=misc[n_a:2 * n_a].reshape(n_gdn, _GDN_HEADS),
        gdn_norm_g=misc[2 * n_a:2 * n_a + n_gdn * _GDN_DIM].reshape(n_gdn, _GDN_DIM),
        hyb_dw_w=lax.dynamic_slice_in_dim(red[8], chip * cw, cw, axis=2),
        gdn_conv_w=lax.dynamic_slice_in_dim(red[9], chip * gw, gw, axis=2))
    loss = misc[2 * n_a + n_gdn * _GDN_DIM]

    small = [n for n in _WEIGHT_ORDER if n not in _BIG]
    packs = [_pack_rows([src[n] for n in small]) for src in (w, grads, m, v)]
    outs = _adamw(packs[0][0], _after(packs[1][0], token), packs[2][0], packs[3][0])
    for name in _BIG:
        grads[name] = _after(grads[name], token)
    delta, new_m, new_v = {}, {}, {}
    for res, o in zip((delta, new_m, new_v), outs):
        res.update(dict(zip(small, _unpack_rows(o, packs[0][1]))))
    late = in_flight[1]
    early = {name: _adamw(w[name], grads[name], m[name], v[name], lo=1 if name in late else 0) for name in _BIG}
    rs_end(in_flight, [token, outs[0], *[early[name][0] for name in _BIG]])
    for name in _BIG:
        res = early[name]
        if name in late:
            res = _adamw(w[name], grads[name], m[name], v[name], lo=0, hi=1, into=res)
        delta[name], new_m[name], new_v[name] = res
    return (loss, grad_x, *[grads[n] for n in _WEIGHT_ORDER], *[delta[n] for n in _WEIGHT_ORDER],
            *[new_m[n] for n in _WEIGHT_ORDER], *[new_v[n] for n in _WEIGHT_ORDER])
```

```python
import functools

import jax
import jax.numpy as jnp
from jax import lax
from jax.experimental import pallas as pl
from jax.experimental.pallas import tpu as pltpu

_BF = jnp.bfloat16
_F32 = jnp.float32
_NORM_EPS = 1e-6
_VMEM_LIMIT_BYTES = 56 * 1024 * 1024

_A_HEADS = 8
_A_HEAD_DIM = 64
_WIN_BLOCK = 128
_DILATIONS = (1, 4, 16)
_CONV_WIDTH = 31
_GDN_HEADS = 8
_GDN_DIM = 128
_GDN_CHUNK = 64
_GDN_SHORT_CONV = 4

_ADAM_LR = 0.001
_ADAM_B1 = 0.9
_ADAM_B2 = 0.999
_ADAM_EPS = 1e-08
_ADAM_WD = 0.01
_ADAM_STEP = 10


def _params(sem):
    return pltpu.CompilerParams(dimension_semantics=sem, vmem_limit_bytes=_VMEM_LIMIT_BYTES)


def _dot(a, b, dn=(((1,), (0,)), ((), ()))):
    return lax.dot_general(a.astype(_BF), b.astype(_BF), dn, preferred_element_type=_F32)


def _dot_nt(a, b):
    return _dot(a, b, (((1,), (1,)), ((), ())))


def _dot_tn(a, b):
    return _dot(a, b, (((0,), (0,)), ((), ())))


_NN = (((1,), (0,)), ((), ()))
_NT = (((1,), (1,)), ((), ()))
_TN = (((0,), (0,)), ((), ()))
_ANY = pl.BlockSpec(memory_space=pl.ANY)


def _mm(name, a, b, *, grid, a_spec, b_spec, dn, out_shape, out_spec, red=None, acc_shape=None,
        pre_a=None, pre_b=None, epi=None, extras=(), extra_specs=(), sum_shape=None):
    n_ex = len(extras)
    n_in = 2 + n_ex
    nk = grid[red] if red is not None else 1
    n_out = 1 if sum_shape is None else 2

    def body(*refs):
        a_ref, b_ref = refs[0], refs[1]
        ex = refs[2:2 + n_ex]
        o_ref = refs[n_in]
        av, bv = a_ref[...], b_ref[...]
        if pre_a is not None:
            av = pre_a(av)
        if pre_b is not None:
            bv = pre_b(bv)
        part = lax.dot_general(av.astype(_BF), bv.astype(_BF), dn, preferred_element_type=_F32)

        def finish(acc):
            res = epi(acc, *[e[...] for e in ex]) if epi is not None else acc
            if sum_shape is not None:
                res, term = res
                s_ref = refs[n_in + 1]
                first = functools.reduce(jnp.logical_and, [pl.program_id(ax) == 0 for ax in range(len(grid)) if ax != red])

                @pl.when(first)
                def _():
                    s_ref[...] = jnp.zeros_like(s_ref)

                s_ref[...] += term
            if isinstance(res, tuple):
                for idx, val in enumerate(res):
                    o_ref[idx] = val.astype(o_ref.dtype)
            else:
                o_ref[...] = res.astype(o_ref.dtype)

        if red is None:
            finish(part)
        else:
            acc_ref = refs[n_in + n_out]
            k = pl.program_id(red)

            @pl.when(k == 0)
            def _():
                acc_ref[...] = part

            @pl.when(k > 0)
            def _():
                acc_ref[...] += part

            @pl.when(k == nk - 1)
            def _():
                finish(acc_ref[...])

    if sum_shape is not None:
        out_shape = [out_shape, jax.ShapeDtypeStruct(sum_shape, _F32)]
        out_spec = [out_spec, pl.BlockSpec(sum_shape, lambda *g: (0,) * len(sum_shape))]
    return pl.pallas_call(
        body, name=name, grid=grid, in_specs=[a_spec, b_spec, *extra_specs], out_specs=out_spec, out_shape=out_shape,
        scratch_shapes=[pltpu.VMEM(acc_shape, _F32)] if red is not None else [],
        compiler_params=_params(("arbitrary",) * len(grid)),
    )(a, b, *extras)


def _rowmap(name, body, tiled, params, tile_outs, param_outs=(), *, rows, tm, ncol=1, col_outer=False,
            into=None):
    nt, npar, nto, npo = len(tiled), len(params), len(tile_outs), len(param_outs)
    assert rows % tm == 0, (name, rows, tm)
    nrow = rows // tm
    grid = (ncol, nrow) if col_outer else (nrow, ncol)

    def ij(g):
        return (g[1], g[0]) if col_outer else (g[0], g[1])

    def cidx(col, j):
        return col(j) if callable(col) else col

    in_specs, operands = [], []
    for arr, width, col in tiled:
        operands.append(arr)
        if arr.ndim == 3:
            in_specs.append(pl.BlockSpec((arr.shape[0], tm, width),
                                         lambda *g, col=col: (0, ij(g)[0], cidx(col, ij(g)[1]))))
        else:
            in_specs.append(pl.BlockSpec((tm, width), lambda *g, col=col: (ij(g)[0], cidx(col, ij(g)[1]))))
    for p in params:
        if isinstance(p, tuple):
            arr, width, col = p
            operands.append(arr)
            in_specs.append(pl.BlockSpec((arr.shape[0], width), lambda *g, col=col: (0, cidx(col, ij(g)[1]))))
        else:
            operands.append(p)
            in_specs.append(pl.BlockSpec(p.shape, lambda *g, nd=p.ndim: (0,) * nd))
    out_shape, out_specs = [], []
    for total, width, col, dt in tile_outs:
        out_shape.append(jax.ShapeDtypeStruct((rows, total), dt))
        out_specs.append(pl.BlockSpec((tm, width), lambda *g, col=col: (ij(g)[0], cidx(col, ij(g)[1]))))
    for shp in param_outs:
        out_shape.append(jax.ShapeDtypeStruct(shp, _F32))
        out_specs.append(pl.BlockSpec(shp, lambda *g, nd=len(shp): (0,) * nd))
    n_in = nt + npar
    aliases = {}
    if into is not None:
        operands.append(into[0])
        in_specs.append(_ANY)
        aliases = {n_in: into[1]}
        n_in += 1

    def kern(*refs):
        vals = [r[...] for r in refs[:nt + npar]]
        touts = refs[n_in:n_in + nto]
        pouts = refs[n_in + nto:]
        res = body(*vals)
        if not isinstance(res, (tuple, list)):
            res = (res,)
        for r, val in zip(touts, res[:nto]):
            r[...] = val.astype(r.dtype)
        if npo:
            @pl.when((pl.program_id(0) == 0) & (pl.program_id(1) == 0))
            def _():
                for r in pouts:
                    r[...] = jnp.zeros_like(r)

            for r, val in zip(pouts, res[nto:]):
                r[...] += val

    return pl.pallas_call(
        kern, name=name, grid=grid, in_specs=in_specs, out_specs=out_specs, out_shape=out_shape,
        input_output_aliases=aliases, compiler_params=_params(("arbitrary", "arbitrary")),
    )(*operands)


def _vjp_body(fn, n_in, diff, n_ct):
    def body(*vals):
        ins, cts = vals[:n_in], vals[n_in:n_in + n_ct]
        outs, pull = jax.vjp(fn, *ins)
        single = not isinstance(outs, (tuple, list))
        if single:
            grads = pull(cts[0].astype(outs.dtype))
        else:
            grads = pull(tuple(c.astype(o.dtype) for c, o in zip(cts, outs)))
        return tuple(grads[i] for i in diff)
    return body


_TM = 512
_TM_WIDE = 1024


def _rms(h, gamma):
    hf = h.astype(_F32)
    return hf * lax.rsqrt(jnp.mean(hf * hf, axis=-1, keepdims=True) + _NORM_EPS) * gamma


def _rmsnorm_fwd(h, gamma):
    s, d = h.shape
    return _rowmap("rmsnorm_fwd", lambda hv, gv: _rms(hv, gv).astype(_BF), [(h, d, 0)], [gamma],
                   [(d, d, 0, _BF)], rows=s, tm=_TM)[0]


def _norm_bwd_epi(dxn, hv, dhv, gv):
    _, pull = jax.vjp(_rms, hv, gv)
    dh, dg = pull(dxn)
    return dhv + dh, dg


def _norm_bwd_args(h, dh_out, gamma, tm):
    d = h.shape[1]
    row = pl.BlockSpec((tm, d), lambda i, j: (i, 0))
    return dict(extras=(h, dh_out, gamma), extra_specs=(row, row, pl.BlockSpec((1, d), lambda i, j: (0, 0))),
                sum_shape=(1, d))


def _silu(x):
    return x * jax.nn.sigmoid(x)


def _ffn_up(h, gamma, w_in, layer):
    s, d = h.shape
    ns = w_in.shape[-1]
    tm = _CONV_TM

    def body(h_ref, gam_ref, w_ref, x_ref, g_ref, u_ref, a_ref):
        xv = _rms(h_ref[...], gam_ref[...]).astype(_BF)
        x_ref[...] = xv
        for half in range(2):
            cols = slice(half * ns, (half + 1) * ns)
            g = jnp.dot(xv, w_ref[half], preferred_element_type=_F32)
            u = jnp.dot(xv, w_ref[half + 2], preferred_element_type=_F32)
            g_ref[:, cols] = g.astype(_BF)
            u_ref[:, cols] = u.astype(_BF)
            a_ref[:, cols] = (_silu(g) * u).astype(_BF)

    row = pl.BlockSpec((tm, d), lambda i: (i, 0))
    out = pl.BlockSpec((tm, 2 * ns), lambda i: (i, 0))
    sds = jax.ShapeDtypeStruct((s, 2 * ns), _BF)
    return pl.pallas_call(
        body, name="ffn_up", grid=(s // tm,),
        in_specs=[row, pl.BlockSpec((1, d), lambda i: (0, 0)), pl.BlockSpec((None, 4, d, ns), lambda i: (layer, 0, 0, 0))],
        out_specs=[row, out, out, out], out_shape=[jax.ShapeDtypeStruct((s, d), _BF), sds, sds, sds],
        compiler_params=_params(("arbitrary",)),
    )(h, gamma, w_in)


def _proj_residual(name, y, w, layer, h, scale):
    s, k = y.shape
    d = w.shape[-1]
    return _mm(name, y, w, grid=(s // _TM,), a_spec=pl.BlockSpec((_TM, k), lambda i: (i, 0)),
               b_spec=pl.BlockSpec((None, k, d), lambda i: (layer, 0, 0)), dn=_NN,
               out_shape=jax.ShapeDtypeStruct((s, d), _F32), out_spec=pl.BlockSpec((_TM, d), lambda i: (i, 0)),
               epi=lambda acc, hv: hv + scale * acc, extras=(h,),
               extra_specs=(pl.BlockSpec((_TM, d), lambda i: (i, 0)),))


def _ffn_dact(dh, w_out, layer, g, u):
    s, d = dh.shape
    f = g.shape[1]
    ns = f // 2

    def epi(da, gv, uv):
        gf, uf = gv.astype(_F32), uv.astype(_F32)
        sg = jax.nn.sigmoid(gf)
        return da * uf * (sg * (1.0 + gf * (1.0 - sg))), da * gf * sg

    tm = _CONV_TM
    gu = pl.BlockSpec((tm, f), lambda i: (i, 0))
    return _mm("ffn_dact", dh, w_out, grid=(s // tm,), a_spec=pl.BlockSpec((tm, d), lambda i: (i, 0)),
               b_spec=pl.BlockSpec((None, f, d), lambda i: (layer, 0, 0)), dn=_NT,
               out_shape=jax.ShapeDtypeStruct((2, s, f), _BF), out_spec=pl.BlockSpec((2, tm, f), lambda i: (0, i, 0)),
               pre_a=lambda v: 0.5 * v, epi=epi, extras=(g, u), extra_specs=(gu, gu))


def _wgrad_rows(name, y, dh, scale):
    s, k = y.shape
    d = dh.shape[1]
    out = _mm(name, y, dh, grid=(s // _TM,), a_spec=pl.BlockSpec((_TM, k), lambda i: (i, 0)),
              b_spec=pl.BlockSpec((_TM, d), lambda i: (i, 0)), dn=_TN, red=0, acc_shape=(k, d),
              out_shape=jax.ShapeDtypeStruct((k, d), _BF), out_spec=pl.BlockSpec((k, d), lambda i: (0, 0)),
              pre_b=(lambda v: scale * v) if scale != 1.0 else None)
    return out.reshape(4, k // 4, d)


def _ffn_dxn(dz, w_in, layer, h, gamma, dh_out):
    _, s, f = dz.shape
    d, ns = w_in.shape[2], w_in.shape[3]
    tm = _TM_WIDE
    return _mm("ffn_dxn", dz, w_in, grid=(s // tm, 4),
               a_spec=pl.BlockSpec((None, tm, ns), lambda i, j: (j // 2, i, j % 2)),
               b_spec=pl.BlockSpec((None, None, d, ns), lambda i, j: (layer, j, 0, 0)), dn=_NT, red=1,
               acc_shape=(tm, d), out_shape=jax.ShapeDtypeStruct((s, d), _F32),
               out_spec=pl.BlockSpec((tm, d), lambda i, j: (i, 0)), epi=_norm_bwd_epi,
               **_norm_bwd_args(h, dh_out, gamma, tm))


def _ffn_dwin(xn, dz):
    s, d = xn.shape
    ns = dz.shape[2] // 2
    tm = _TM_WIDE
    return _mm("ffn_dwin", xn, dz, grid=(4, s // tm), a_spec=pl.BlockSpec((tm, d), lambda j, i: (i, 0)),
               b_spec=pl.BlockSpec((None, tm, ns), lambda j, i: (j // 2, i, j % 2)), dn=_TN, red=1,
               acc_shape=(d, ns), out_shape=jax.ShapeDtypeStruct((4, d, ns), _BF),
               out_spec=pl.BlockSpec((None, d, ns), lambda j, i: (j, 0, 0)))


def _ffn_fwd(h, gamma, w_in, w_out, layer):
    xn, g, u, act = _ffn_up(h, gamma, w_in, layer)
    h_new = _proj_residual("ffn_down", act, w_out, layer, h, 0.5)
    return h_new, (h, xn, g, u, act)


def _ffn_bwd(dh, saved, gamma, w_in, w_out, layer):
    h, xn, g, u, act = saved
    dz = _ffn_dact(dh, w_out, layer, g, u)
    d_w_out = _wgrad_rows("ffn_dwout", act, dh, 0.5)
    dh_in, dgamma = _ffn_dxn(dz, w_in, layer, h, gamma, dh)
    d_w_in = _ffn_dwin(xn, dz)
    return dh_in, dgamma, d_w_in, d_w_out


def _loss_head(h, gamma, target):
    s, d = h.shape

    def body(hv, tv, gv):
        y, pull = jax.vjp(_rms, hv, gv)
        err = y - tv
        loss = 0.5 * jnp.sum(jnp.mean(err * err, axis=-1, keepdims=True), axis=0, keepdims=True)
        dh, dg = pull(err * (1.0 / d))
        return dh, jnp.broadcast_to(loss, (1, 128)), dg

    return _rowmap("loss_head", body, [(h, d, 0), (target, d, 0)], [gamma], [(d, d, 0, _F32)],
                   [(1, 128), (1, d)], rows=s, tm=_TM)


_ROT = _A_HEAD_DIM // 4
_ROPE_THETA = 500000.0
_A_WIDTH = _A_HEADS * _A_HEAD_DIM
_NEG = -1e30


def _rotary_tables(positions):
    inv_freq = jnp.power(jnp.float32(_ROPE_THETA), -jnp.arange(0, _ROT, 2, dtype=_F32) / _ROT)
    ang = positions.reshape(-1, 1).astype(_F32) * inv_freq
    cos, sin = jnp.cos(ang), jnp.sin(ang)
    s = ang.shape[0]
    pad = jnp.zeros((s, _A_HEAD_DIM - _ROT), _F32)
    c_head = jnp.concatenate([cos, cos, pad + 1.0], axis=1)
    s_head = jnp.concatenate([-sin, sin, pad], axis=1)
    c2, s2 = jnp.tile(c_head, (1, 2)), jnp.tile(s_head, (1, 2))
    scale = _A_HEAD_DIM ** -0.5
    tab_c = jnp.stack([scale * c2, c2, jnp.ones_like(c2)])
    tab_s = jnp.stack([scale * s2, s2, jnp.zeros_like(s2)])
    return tab_c, tab_s


def _pair_swap(z):
    lane = lax.broadcasted_iota(jnp.int32, z.shape, 1) % _A_HEAD_DIM
    half = _ROT // 2
    up = jnp.where(lane < _ROT, pltpu.roll(z, half, axis=1), 0.0)
    return jnp.where(lane < half, pltpu.roll(z, 128 - half, axis=1), up)


def _rot_apply(x, c, s, transpose):
    outs = []
    for k in range(x.shape[1] // 128):
        xs = x[:, k * 128:(k + 1) * 128]
        outs.append(c * xs + (_pair_swap(s * xs) if transpose else s * _pair_swap(xs)))
    return jnp.concatenate(outs, axis=1)


def _qkv_prep(proj, tab_c, tab_s):
    s = proj.shape[0]
    w = _A_WIDTH

    def body(x_ref, c_ref, s_ref, o_ref):
        o_ref[...] = _rot_apply(x_ref[...], c_ref[...], s_ref[...], False).astype(_BF)

    tab = pl.BlockSpec((None, _TM, 128), lambda i, j: (j, i, 0))
    blk = pl.BlockSpec((_TM, w), lambda i, j: (i, j))
    return pl.pallas_call(
        body, name="qkv_prep", grid=(s // _TM, 3), in_specs=[blk, tab, tab], out_specs=blk,
        out_shape=jax.ShapeDtypeStruct((s, 3 * w), _BF), compiler_params=_params(("arbitrary", "arbitrary")),
    )(proj, tab_c, tab_s)


def _head_masks():
    lane = lax.broadcasted_iota(jnp.int32, (_WIN_BLOCK, 128), 1)
    return [lane // _A_HEAD_DIM == hh for hh in range(2)]


def _band_masks(first_block):
    qi = lax.broadcasted_iota(jnp.int32, (_WIN_BLOCK, _WIN_BLOCK), 0)
    kj = lax.broadcasted_iota(jnp.int32, (_WIN_BLOCK, _WIN_BLOCK), 1)
    return (kj >= qi) & jnp.logical_not(first_block), kj <= qi


def _attn_fwd(qkv, dil):
    s = qkv.shape[0]
    w = _A_WIDTH
    length = s // dil
    nb = length // _WIN_BLOCK
    view = qkv.reshape(length, dil * 3 * w)

    def body(q_ref, kp_ref, kc_ref, vp_ref, vc_ref, o_ref, l_ref):
        allow_p, allow_c = _band_masks(pl.program_id(1) == 0)
        hm = _head_masks()
        for sl in range(w // 128):
            cols = slice(sl * 128, (sl + 1) * 128)
            q, kp, kc, vp, vc = q_ref[:, cols], kp_ref[:, cols], kc_ref[:, cols], vp_ref[:, cols], vc_ref[:, cols]
            o_s = jnp.zeros((_WIN_BLOCK, 128), _F32)
            l_s = jnp.zeros((_WIN_BLOCK, 128), _F32)
            for hh in range(2):
                qh = jnp.where(hm[hh], q, jnp.zeros_like(q))
                sp = jnp.where(allow_p, _dot_nt(qh, kp), _NEG)
                sc = jnp.where(allow_c, _dot_nt(qh, kc), _NEG)
                m = jnp.maximum(jnp.max(sp, axis=1, keepdims=True), jnp.max(sc, axis=1, keepdims=True))
                pp, pc = jnp.exp(sp - m), jnp.exp(sc - m)
                den = jnp.sum(pp, axis=1, keepdims=True) + jnp.sum(pc, axis=1, keepdims=True)
                oh = (_dot(pp, vp) + _dot(pc, vc)) / den
                o_s = jnp.where(hm[hh], oh, o_s)
                l_s = jnp.where(hm[hh], m + jnp.log(den), l_s)
            o_ref[:, cols] = o_s.astype(o_ref.dtype)
            l_ref[:, cols] = l_s

    def at(col, prev):
        if prev:
            return pl.BlockSpec((_WIN_BLOCK, w), lambda r, n: (jnp.maximum(n - 1, 0), 3 * r + col))
        return pl.BlockSpec((_WIN_BLOCK, w), lambda r, n: (n, 3 * r + col))

    out = pl.BlockSpec((_WIN_BLOCK, w), lambda r, n: (n, r))
    sds = jax.ShapeDtypeStruct((length, dil * w), _F32)
    o, lse = pl.pallas_call(
        body, name=f"attn_fwd_d{dil}", grid=(dil, nb),
        in_specs=[at(0, False), at(1, True), at(1, False), at(2, True), at(2, False)],
        out_specs=[out, out], out_shape=[jax.ShapeDtypeStruct(sds.shape, _BF), sds],
        compiler_params=_params(("arbitrary", "arbitrary")),
    )(view, view, view, view, view)
    return o.reshape(s, w), lse.reshape(s, w)


def _attn_bwd(qkv, o, lse, do, dlse, dil):
    s = qkv.shape[0]
    w = _A_WIDTH
    length = s // dil
    nb = length // _WIN_BLOCK
    total = dil * nb
    view = qkv.reshape(length, dil * 3 * w)
    o, lse, do, dlse = (t.reshape(length, dil * w) for t in (o, lse, do, dlse))

    def body(q_ref, kp_ref, kc_ref, vp_ref, vc_ref, o_ref, l_ref, do_ref, dl_ref, out_ref,
             car_q, car_k, car_v, cur_q, cur_kc, cur_vc, cur_kp, cur_vp):
        t = pl.program_id(0)
        working = t < total

        @pl.when(t == 0)
        def _():
            car_q[...] = jnp.zeros_like(car_q)
            car_k[...] = jnp.zeros_like(car_k)
            car_v[...] = jnp.zeros_like(car_v)

        @pl.when(jnp.logical_not(working))
        def _():
            cur_kp[...] = jnp.zeros_like(cur_kp)
            cur_vp[...] = jnp.zeros_like(cur_vp)

        @pl.when(working)
        def _():
            allow_p, allow_c = _band_masks(t % nb == 0)
            hm = _head_masks()
            for sl in range(w // 128):
                cols = slice(sl * 128, (sl + 1) * 128)
                q, kp, kc, vp, vc = q_ref[:, cols], kp_ref[:, cols], kc_ref[:, cols], vp_ref[:, cols], vc_ref[:, cols]
                ov, dov = o_ref[:, cols].astype(_F32), do_ref[:, cols].astype(_F32)
                lv, dlv = l_ref[:, cols], dl_ref[:, cols]
                zero = jnp.zeros((_WIN_BLOCK, 128), _F32)
                dq_s, dkp_s, dkc_s, dvp_s, dvc_s = zero, zero, zero, zero, zero
                for hh in range(2):
                    qh = jnp.where(hm[hh], q, jnp.zeros_like(q))
                    doh = jnp.where(hm[hh], dov, 0.0).astype(_BF)
                    lse_h = jnp.sum(jnp.where(hm[hh], lv, 0.0), axis=1, keepdims=True) * (1.0 / _A_HEAD_DIM)
                    delta = jnp.sum(jnp.where(hm[hh], dov * ov, 0.0), axis=1, keepdims=True)
                    dl = jnp.sum(jnp.where(hm[hh], dlv, 0.0), axis=1, keepdims=True)
                    pp = jnp.exp(jnp.where(allow_p, _dot_nt(qh, kp), _NEG) - lse_h)
                    pc = jnp.exp(jnp.where(allow_c, _dot_nt(qh, kc), _NEG) - lse_h)
                    dsp = (pp * (_dot_nt(doh, vp) - delta + dl)).astype(_BF)
                    dsc = (pc * (_dot_nt(doh, vc) - delta + dl)).astype(_BF)
                    dq_s = dq_s + jnp.where(hm[hh], _dot(dsp, kp) + _dot(dsc, kc), 0.0)
                    dkp_s = dkp_s + _dot_tn(dsp, qh)
                    dkc_s = dkc_s + _dot_tn(dsc, qh)
                    dvp_s = dvp_s + _dot_tn(pp, doh)
                    dvc_s = dvc_s + _dot_tn(pc, doh)
                cur_q[:, cols] = dq_s
                cur_kp[:, cols] = dkp_s
                cur_kc[:, cols] = dkc_s
                cur_vp[:, cols] = dvp_s
                cur_vc[:, cols] = dvc_s

        out_ref[:, 0:w] = car_q[...].astype(out_ref.dtype)
        out_ref[:, w:2 * w] = (car_k[...] + cur_kp[...]).astype(out_ref.dtype)
        out_ref[:, 2 * w:3 * w] = (car_v[...] + cur_vp[...]).astype(out_ref.dtype)

        @pl.when(working)
        def _():
            car_q[...] = cur_q[...]
            car_k[...] = cur_kc[...]
            car_v[...] = cur_vc[...]

    def place(t):
        tc = jnp.minimum(t, total - 1)
        return tc % nb, tc // nb

    def at(col, prev):
        if prev:
            return pl.BlockSpec((_WIN_BLOCK, w), lambda t: (jnp.maximum(place(t)[0] - 1, 0), 3 * place(t)[1] + col))
        return pl.BlockSpec((_WIN_BLOCK, w), lambda t: (place(t)[0], 3 * place(t)[1] + col))

    blk = pl.BlockSpec((_WIN_BLOCK, w), lambda t: place(t))
    scratch = [pltpu.VMEM((_WIN_BLOCK, w), _F32) for _ in range(8)]
    out = pl.pallas_call(
        body, name=f"attn_bwd_d{dil}", grid=(total + 1,),
        in_specs=[at(0, False), at(1, True), at(1, False), at(2, True), at(2, False), blk, blk, blk, blk],
        out_specs=pl.BlockSpec((_WIN_BLOCK, 3 * w), lambda t: place(jnp.maximum(t - 1, 0))),
        out_shape=jax.ShapeDtypeStruct((length, dil * 3 * w), _BF), scratch_shapes=scratch,
        compiler_params=_params(("arbitrary",)),
    )(view, view, view, view, view, o, lse, do, dlse)
    return out.reshape(s, 3 * w)


def _mix(o1, o2, o3, l1, l2, l3):
    m = jnp.maximum(jnp.maximum(l1, l2), l3)
    e1, e2, e3 = jnp.exp(l1 - m), jnp.exp(l2 - m), jnp.exp(l3 - m)
    return (e1 * o1 + e2 * o2 + e3 * o3) / (e1 + e2 + e3)


def _attn_mix_fwd(outs, lses):
    s = outs[0].shape[0]
    tiles = [(t, _A_WIDTH, 0) for t in (*outs, *lses)]
    return _rowmap("attn_mix_fwd", _mix, tiles, [], [(2 * _A_WIDTH, _A_WIDTH, 0, _BF)], rows=s, tm=_CONV_TM)[0]


def _attn_mix_bwd(outs, lses, dy):
    s = outs[0].shape[0]
    tiles = [(t, _A_WIDTH, 0) for t in (*outs, *lses, dy)]
    body = _vjp_body(_mix, 6, range(6), 1)
    return _rowmap("attn_mix_bwd", body, tiles, [], [(_A_WIDTH, _A_WIDTH, 0, _BF)] * 3 + [(_A_WIDTH, _A_WIDTH, 0, _F32)] * 3, rows=s, tm=_CONV_TM)


_CONV_TM = 256
_HALO = 32


def _ln_silu(pre, g, b):
    mu = jnp.mean(pre, axis=-1, keepdims=True)
    xc = pre - mu
    y = xc * lax.rsqrt(jnp.mean(xc * xc, axis=-1, keepdims=True) + _NORM_EPS)
    return _silu(y * g + b)


def _glu_into(buf, ua_ref, ub_ref, uah_ref, ubh_ref, first):
    halo = uah_ref[...] * jax.nn.sigmoid(ubh_ref[...])
    buf[0:_HALO, :] = jnp.where(first, 0.0, halo)
    buf[_HALO:, :] = ua_ref[...] * jax.nn.sigmoid(ub_ref[...])


_SHIFT_ROWS = _CONV_TM + _HALO - 8


def _shift_copies(sh, buf):
    for b in range(1, 8):
        sh[b - 1] = buf[b:b + _SHIFT_ROWS, :]


def _rows_at(buf, sh, off, rows):
    a, b = divmod(off, 8)
    return buf[8 * a:8 * a + rows, :] if b == 0 else sh[b - 1, 8 * a:8 * a + rows, :]


def _causal_taps(buf, sh, w_ref, rows):
    base = _HALO - (_CONV_WIDTH - 1)
    acc = jnp.zeros((rows, buf.shape[1]), _F32)
    for j in range(_CONV_WIDTH):
        acc = acc + w_ref[j:j + 1, :] * _rows_at(buf, sh, base + j, rows)
    return acc


def _conv_specs(s):
    tm = _CONV_TM
    per = tm // _HALO
    cw = _A_WIDTH
    cur = lambda col: pl.BlockSpec((tm, cw), lambda i: (i, col))
    prev = lambda col: pl.BlockSpec((_HALO, cw), lambda i: (jnp.maximum(i * per - 1, 0), col))
    return tm, per, cw, cur, prev


def _full2(arr):
    return pl.BlockSpec(arr.shape, lambda i: (0, 0))


def _conv_fwd(proj, dw_w, dw_b, ln_g, ln_b, y):
    s = proj.shape[0]
    tm, per, cw, cur, prev = _conv_specs(s)

    def body(ua_ref, ub_ref, uah_ref, ubh_ref, w_ref, b_ref, g_ref, be_ref, y_in, o_ref, buf, sh):
        del y_in
        _glu_into(buf, ua_ref, ub_ref, uah_ref, ubh_ref, pl.program_id(0) == 0)
        _shift_copies(sh, buf)
        pre = _causal_taps(buf, sh, w_ref, tm) + b_ref[...]
        o_ref[...] = _ln_silu(pre, g_ref[...], be_ref[...]).astype(_BF)

    return pl.pallas_call(
        body, name="conv_fwd", grid=(s // tm,),
        in_specs=[cur(3), cur(4), prev(3), prev(4), _full2(dw_w), _full2(dw_b), _full2(ln_g), _full2(ln_b), _ANY],
        out_specs=pl.BlockSpec((tm, cw), lambda i: (i, 1)), out_shape=jax.ShapeDtypeStruct(y.shape, y.dtype),
        scratch_shapes=[pltpu.VMEM((tm + _HALO, cw), _F32), pltpu.VMEM((7, _SHIFT_ROWS, cw), _F32)],
        input_output_aliases={8: 0}, compiler_params=_params(("arbitrary",)),
    )(proj, proj, proj, proj, dw_w, dw_b, ln_g, ln_b, y)


def _conv_bwd_pre(proj, dw_w, dw_b, ln_g, ln_b, dy):
    s = proj.shape[0]
    tm, per, cw, cur, prev = _conv_specs(s)

    def body(ua_ref, ub_ref, uah_ref, ubh_ref, w_ref, b_ref, g_ref, be_ref, dy_ref, dp_ref, dg_ref, dbe_ref, db_ref, buf, sh):
        i = pl.program_id(0)
        _glu_into(buf, ua_ref, ub_ref, uah_ref, ubh_ref, i == 0)
        _shift_copies(sh, buf)
        pre = _causal_taps(buf, sh, w_ref, tm) + b_ref[...]
        _, pull = jax.vjp(_ln_silu, pre, g_ref[...], be_ref[...])
        dpre, dg, dbe = pull(dy_ref[...])
        dp_ref[...] = dpre

        @pl.when(i == 0)
        def _():
            dg_ref[...] = jnp.zeros_like(dg_ref)
            dbe_ref[...] = jnp.zeros_like(dbe_ref)
            db_ref[...] = jnp.zeros_like(db_ref)

        dg_ref[...] += dg
        dbe_ref[...] += dbe
        db_ref[...] += jnp.sum(dpre, axis=0, keepdims=True)

    vec = pl.BlockSpec((1, cw), lambda i: (0, 0))
    vsd = jax.ShapeDtypeStruct((1, cw), _F32)
    return pl.pallas_call(
        body, name="conv_bwd_pre", grid=(s // tm,),
        in_specs=[cur(3), cur(4), prev(3), prev(4), _full2(dw_w), _full2(dw_b), _full2(ln_g), _full2(ln_b), cur(1)],
        out_specs=[pl.BlockSpec((tm, cw), lambda i: (i, 0)), vec, vec, vec],
        out_shape=[jax.ShapeDtypeStruct((s, cw), _F32), vsd, vsd, vsd],
        scratch_shapes=[pltpu.VMEM((tm + _HALO, cw), _F32), pltpu.VMEM((7, _SHIFT_ROWS, cw), _F32)],
        compiler_params=_params(("arbitrary",)),
    )(proj, proj, proj, proj, dw_w, dw_b, ln_g, ln_b, dy)


def _conv_bwd_taps(proj, dw_w, dpre):
    s = proj.shape[0]
    tm, per, cw, cur, prev = _conv_specs(s)
    nt = s // tm

    def body(ua_ref, ub_ref, uah_ref, ubh_ref, w_ref, dp_ref, dpn_ref, dglu_ref, dw_ref, buf, dbuf, sh, dsh):
        i = pl.program_id(0)
        _glu_into(buf, ua_ref, ub_ref, uah_ref, ubh_ref, i == 0)
        dpre = dp_ref[...]
        dbuf[0:tm, :] = dpre
        dbuf[tm:, :] = jnp.where(i == nt - 1, 0.0, dpn_ref[...])
        _shift_copies(sh, buf)
        _shift_copies(dsh, dbuf)
        acc = jnp.zeros((tm, cw), _F32)
        for j in range(_CONV_WIDTH):
            acc = acc + w_ref[j:j + 1, :] * _rows_at(dbuf, dsh, _CONV_WIDTH - 1 - j, tm)
        dglu_ref[...] = acc

        @pl.when(i == 0)
        def _():
            dw_ref[...] = jnp.zeros_like(dw_ref)

        base = _HALO - (_CONV_WIDTH - 1)
        for j in range(_CONV_WIDTH):
            dw_ref[j:j + 1, :] += jnp.sum(dpre * _rows_at(buf, sh, base + j, tm), axis=0, keepdims=True)

    nxt = pl.BlockSpec((_HALO, cw), lambda i: (jnp.minimum((i + 1) * per, s // _HALO - 1), 0))
    return pl.pallas_call(
        body, name="conv_bwd_taps", grid=(nt,),
        in_specs=[cur(3), cur(4), prev(3), prev(4), _full2(dw_w), pl.BlockSpec((tm, cw), lambda i: (i, 0)), nxt],
        out_specs=[pl.BlockSpec((tm, cw), lambda i: (i, 0)), pl.BlockSpec((_HALO, cw), lambda i: (0, 0))],
        out_shape=[jax.ShapeDtypeStruct((s, cw), _F32), jax.ShapeDtypeStruct((_HALO, cw), _F32)],
        scratch_shapes=[pltpu.VMEM((tm + _HALO, cw), _F32), pltpu.VMEM((tm + _HALO, cw), _F32),
                        pltpu.VMEM((7, _SHIFT_ROWS, cw), _F32), pltpu.VMEM((7, _SHIFT_ROWS, cw), _F32)],
        compiler_params=_params(("arbitrary",)),
    )(proj, proj, proj, proj, dw_w, dpre, dpre)


def _hyb_dproj(dqkv, tab_c, tab_s, dglu, proj):
    s = proj.shape[0]
    w = _A_WIDTH

    def body(d1_ref, d2_ref, d3_ref, c_ref, s_ref, dg_ref, ua_ref, ub_ref, o_ref):
        j = pl.program_id(1)

        @pl.when(j < 3)
        def _():
            x = d1_ref[...].astype(_F32) + d2_ref[...].astype(_F32) + d3_ref[...].astype(_F32)
            o_ref[...] = _rot_apply(x, c_ref[...], s_ref[...], True).astype(_BF)

        @pl.when(j == 3)
        def _():
            o_ref[...] = (dg_ref[...] * jax.nn.sigmoid(ub_ref[...])).astype(_BF)

        @pl.when(j == 4)
        def _():
            sg = jax.nn.sigmoid(ub_ref[...])
            o_ref[...] = (dg_ref[...] * ua_ref[...] * sg * (1.0 - sg)).astype(_BF)

    dq = pl.BlockSpec((_TM, w), lambda i, j: (i, jnp.minimum(j, 2)))
    tab = pl.BlockSpec((None, _TM, 128), lambda i, j: (jnp.minimum(j, 2), i, 0))
    col = lambda c: pl.BlockSpec((_TM, w), lambda i, j: (i, c))
    return pl.pallas_call(
        body, name="hyb_dproj", grid=(s // _TM, 5),
        in_specs=[dq, dq, dq, tab, tab, col(0), col(3), col(4)],
        out_specs=pl.BlockSpec((_TM, w), lambda i, j: (i, j)),
        out_shape=jax.ShapeDtypeStruct((s, 5 * w), _BF), compiler_params=_params(("arbitrary", "arbitrary")),
    )(*dqkv, tab_c, tab_s, dglu, proj, proj)


def _proj_in(name, xn, w, layer):
    s, d = xn.shape
    ns = w.shape[-1]
    return _mm(name, xn, w, grid=(4, s // _TM), a_spec=pl.BlockSpec((_TM, d), lambda j, i: (i, 0)),
               b_spec=pl.BlockSpec((None, None, d, ns), lambda j, i: (layer, j, 0, 0)), dn=_NN,
               out_shape=jax.ShapeDtypeStruct((s, 4 * ns), _F32), out_spec=pl.BlockSpec((_TM, ns), lambda j, i: (i, j)))


def _proj_in_dx(name, dz, w, layer, h, gamma, dh_out):
    s = dz.shape[0]
    d, ns = w.shape[2], w.shape[3]
    return _mm(name, dz, w, grid=(s // _TM, 4), a_spec=pl.BlockSpec((_TM, ns), lambda i, j: (i, j)),
               b_spec=pl.BlockSpec((None, None, d, ns), lambda i, j: (layer, j, 0, 0)), dn=_NT, red=1,
               acc_shape=(_TM, d), out_shape=jax.ShapeDtypeStruct((s, d), _F32),
               out_spec=pl.BlockSpec((_TM, d), lambda i, j: (i, 0)), epi=_norm_bwd_epi,
               **_norm_bwd_args(h, dh_out, gamma, _TM))


def _proj_in_dw(name, xn, dz):
    s, d = xn.shape
    ns = dz.shape[1] // 4
    return _mm(name, xn, dz, grid=(4, s // _TM), a_spec=pl.BlockSpec((_TM, d), lambda j, i: (i, 0)),
               b_spec=pl.BlockSpec((_TM, ns), lambda j, i: (i, j)), dn=_TN, red=1, acc_shape=(d, ns),
               out_shape=jax.ShapeDtypeStruct((4, d, ns), _BF),
               out_spec=pl.BlockSpec((None, d, ns), lambda j, i: (j, 0, 0)))


def _proj_out_dy(name, dh, w, layer):
    s, d = dh.shape
    k = w.shape[1]
    return _mm(name, dh, w, grid=(s // _TM,), a_spec=pl.BlockSpec((_TM, d), lambda i: (i, 0)),
               b_spec=pl.BlockSpec((None, k, d), lambda i: (layer, 0, 0)), dn=_NT,
               out_shape=jax.ShapeDtypeStruct((s, k), _F32), out_spec=pl.BlockSpec((_TM, k), lambda i: (i, 0)))


def _hyb_fwd(h, gamma, w_in, w_out, dw_w, dw_b, ln_g, ln_b, tabs, li):
    hn = _rmsnorm_fwd(h, gamma)
    proj = _proj_in("hyb_in", hn, w_in, li)
    qkv = _qkv_prep(proj, *tabs)
    branches = [_attn_fwd(qkv, dil) for dil in _DILATIONS]
    outs, lses = [b[0] for b in branches], [b[1] for b in branches]
    y = _attn_mix_fwd(outs, lses)
    y = _conv_fwd(proj, dw_w, dw_b, ln_g, ln_b, y)
    h_new = _proj_residual("hyb_out", y, w_out, li, h, 1.0)
    return h_new, (h, hn, proj, qkv, outs, lses, y)


def _hyb_bwd(dh, saved, gamma, w_in, w_out, dw_w, dw_b, ln_g, ln_b, tabs, li):
    h, hn, proj, qkv, outs, lses, y = saved
    dy = _proj_out_dy("hyb_dy", dh, w_out, li)
    d_w_out = _wgrad_rows("hyb_dwout", y, dh, 1.0)
    mixg = _attn_mix_bwd(outs, lses, dy)
    dqkv = [_attn_bwd(qkv, outs[b], lses[b], mixg[b], mixg[3 + b], dil) for b, dil in enumerate(_DILATIONS)]
    dpre, d_ln_g, d_ln_b, d_dw_b = _conv_bwd_pre(proj, dw_w, dw_b, ln_g, ln_b, dy)
    dglu, d_dw_w = _conv_bwd_taps(proj, dw_w, dpre)
    dproj = _hyb_dproj(dqkv, *tabs, dglu, proj)
    dh_in, dgamma = _proj_in_dx("hyb_dhn", dproj, w_in, li, h, gamma, dh)
    d_w_in = _proj_in_dw("hyb_dwin", hn, dproj)
    return dh_in, dgamma, (d_dw_w, d_dw_b, d_ln_g, d_ln_b), d_w_in, d_w_out


_GDN_BLOCK = 128
_GDN_HEADS_PER_STEP = 8


def _gdn_chunk(qs, ks, vs, gates, states, heads):
    n = len(qs)
    hs = range(n)
    c = qs[0].shape[0]
    lane = lax.broadcasted_iota(jnp.int32, gates.shape, 1)
    row = lax.broadcasted_iota(jnp.int32, (c, c), 0)
    col = lax.broadcasted_iota(jnp.int32, (c, c), 1)
    rowc = lax.broadcasted_iota(jnp.int32, (c, 1), 0)
    causal = row >= col
    eye = jnp.where(row == col, 1.0, 0.0)
    beta = [jnp.sum(jnp.where(lane == heads[h], gates, 0.0), axis=1, keepdims=True) for h in hs]
    g = [jnp.sum(jnp.where(lane == heads[h] + _GDN_HEADS, gates, 0.0), axis=1, keepdims=True) for h in hs]
    gc_row = [jnp.sum(jnp.where(row <= col, g[h], 0.0), axis=0, keepdims=True) for h in hs]
    gc_col = [jnp.sum(jnp.where(row == col, gc_row[h], 0.0), axis=1, keepdims=True) for h in hs]
    decay = [jnp.where(causal, jnp.exp(jnp.where(causal, gc_col[h] - gc_row[h], 0.0)), 0.0) for h in hs]
    kb = [ks[h] * beta[h] for h in hs]
    l_mat = [jnp.where(row > col, _dot_nt(kb[h], ks[h]) * decay[h], 0.0) for h in hs]
    t_inv = [eye - l_mat[h] for h in hs]
    x = [_dot(l_mat[h], l_mat[h]) for h in hs]
    for it in range(5):
        t_inv = [t_inv[h] + _dot(t_inv[h], x[h]) for h in hs]
        if it < 4:
            x = [_dot(x[h], x[h]) for h in hs]
    eg = [jnp.exp(gc_col[h]) for h in hs]
    u = [_dot(t_inv[h], vs[h] * beta[h]) for h in hs]
    w = [_dot(t_inv[h], kb[h] * eg[h]) for h in hs]
    attn = [jnp.where(causal, _dot_nt(qs[h], ks[h]) * decay[h], 0.0) for h in hs]
    g_last = [jnp.sum(jnp.where(rowc == c - 1, gc_col[h], 0.0), axis=0, keepdims=True) for h in hs]
    k_dec = [ks[h] * jnp.exp(g_last[h] - gc_col[h]) for h in hs]
    v_new = [u[h] - _dot(w[h], states[h]) for h in hs]
    o = [_dot(qs[h] * eg[h], states[h]) + _dot(attn[h], v_new[h]) for h in hs]
    new_states = [states[h] * jnp.exp(g_last[h]) + _dot_tn(k_dec[h], v_new[h]) for h in hs]
    return o, new_states


def _gdn_chunk_fwd(q, k, v, gates):
    s = q.shape[0]
    cb, c = _GDN_BLOCK, _GDN_CHUNK
    nblk, per = s // cb, cb // c

    hps = _GDN_HEADS_PER_STEP

    def body(q_ref, k_ref, v_ref, g_ref, o_ref, st_ref, state):
        @pl.when(pl.program_id(1) == 0)
        def _():
            state[...] = jnp.zeros_like(state)

        heads = [pl.program_id(0) * hps + hh for hh in range(hps)]
        cols = [slice(hh * _GDN_DIM, (hh + 1) * _GDN_DIM) for hh in range(hps)]
        for ci in range(per):
            rows = slice(ci * c, (ci + 1) * c)
            sts = [state[hh] for hh in range(hps)]
            for hh in range(hps):
                st_ref[hh, ci] = sts[hh]
            outs, new = _gdn_chunk([q_ref[rows, cl] for cl in cols], [k_ref[rows, cl] for cl in cols],
                                   [v_ref[rows, cl] for cl in cols], g_ref[rows, :], sts, heads)
            for hh in range(hps):
                o_ref[rows, cols[hh]] = outs[hh]
                state[hh] = new[hh]

    slab = pl.BlockSpec((cb, hps * _GDN_DIM), lambda h, i: (i, h))
    return pl.pallas_call(
        body, name="gdn_chunk_fwd", grid=(_GDN_HEADS // hps, nblk),
        in_specs=[slab, slab, slab, pl.BlockSpec((cb, 128), lambda h, i: (i, 0))],
        out_specs=[slab, pl.BlockSpec((hps, per, _GDN_DIM, _GDN_DIM), lambda h, i: (h, i, 0, 0))],
        out_shape=[jax.ShapeDtypeStruct((s, _GDN_HEADS * _GDN_DIM), _F32),
                   jax.ShapeDtypeStruct((_GDN_HEADS, s // c, _GDN_DIM, _GDN_DIM), _F32)],
        scratch_shapes=[pltpu.VMEM((hps, _GDN_DIM, _GDN_DIM), _F32)],
        compiler_params=_params(("arbitrary", "arbitrary")),
    )(q, k, v, gates)


def _gdn_chunk_bwd(q, k, v, gates, states, do):
    s = q.shape[0]
    cb, c = _GDN_BLOCK, _GDN_CHUNK
    nblk, per = s // cb, cb // c

    hps = _GDN_HEADS_PER_STEP

    def body(q_ref, k_ref, v_ref, g_ref, st_ref, do_ref, dq_ref, dk_ref, dv_ref, dg_ref, dstate):
        @pl.when(pl.program_id(1) == 0)
        def _():
            dstate[...] = jnp.zeros_like(dstate)

        heads = [pl.program_id(0) * hps + hh for hh in range(hps)]
        cols = [slice(hh * _GDN_DIM, (hh + 1) * _GDN_DIM) for hh in range(hps)]
        for ci in reversed(range(per)):
            rows = slice(ci * c, (ci + 1) * c)
            fn = functools.partial(_gdn_chunk, heads=heads)
            _, pull = jax.vjp(fn, [q_ref[rows, cl] for cl in cols], [k_ref[rows, cl] for cl in cols],
                              [v_ref[rows, cl] for cl in cols], g_ref[rows, :], [st_ref[hh, ci] for hh in range(hps)])
            dq, dk, dv, dg, dst = pull(([do_ref[rows, cl] for cl in cols], [dstate[hh] for hh in range(hps)]))
            dg_ref[rows, :] = dg
            for hh in range(hps):
                dq_ref[rows, cols[hh]] = dq[hh]
                dk_ref[rows, cols[hh]] = dk[hh]
                dv_ref[rows, cols[hh]] = dv[hh]
                dstate[hh] = dst[hh]

    slab = pl.BlockSpec((cb, hps * _GDN_DIM), lambda h, i: (nblk - 1 - i, h))
    sds = jax.ShapeDtypeStruct((s, _GDN_HEADS * _GDN_DIM), _F32)
    return pl.pallas_call(
        body, name="gdn_chunk_bwd", grid=(_GDN_HEADS // hps, nblk),
        in_specs=[slab, slab, slab, pl.BlockSpec((cb, 128), lambda h, i: (nblk - 1 - i, 0)),
                  pl.BlockSpec((hps, per, _GDN_DIM, _GDN_DIM), lambda h, i: (h, nblk - 1 - i, 0, 0)), slab],
        out_specs=[slab, slab, slab, pl.BlockSpec((None, cb, 128), lambda h, i: (h, nblk - 1 - i, 0))],
        out_shape=[sds, sds, sds, jax.ShapeDtypeStruct((_GDN_HEADS // hps, s, 128), _F32)],
        scratch_shapes=[pltpu.VMEM((hps, _GDN_DIM, _GDN_DIM), _F32)],
        compiler_params=_params(("arbitrary", "arbitrary")),
    )(q, k, v, gates, states, do)


_SHALO = 8


def _swish_l2(c, normalize, scale):
    y = _silu(c)
    if normalize:
        y = y * lax.rsqrt(jnp.sum(y * y, axis=-1, keepdims=True) + _NORM_EPS) * scale
    return y


def _short_taps(buf, w_ref, rows):
    base = _SHALO - (_GDN_SHORT_CONV - 1)
    acc = jnp.zeros((rows, buf.shape[1]), _F32)
    for t in range(_GDN_SHORT_CONV):
        acc = acc + w_ref[t:t + 1, :] * buf[base + t:base + t + rows, :]
    return acc


_GDN_W = _GDN_HEADS * _GDN_DIM


def _gdn_conv_fwd(proj, conv_w, grp, normalize, scale):
    s = proj.shape[0]
    tm = _CONV_TM
    per = tm // _SHALO

    def body(x_ref, xp_ref, w_ref, o_ref, buf):
        buf[0:_SHALO, :] = jnp.where(pl.program_id(0) == 0, 0.0, xp_ref[...])
        buf[_SHALO:, :] = x_ref[...]
        conv = _short_taps(buf, w_ref, tm)
        for h in range(_GDN_HEADS):
            cols = slice(h * _GDN_DIM, (h + 1) * _GDN_DIM)
            o_ref[:, cols] = _swish_l2(conv[:, cols], normalize, scale)

    return pl.pallas_call(
        body, name=f"gdn_conv_fwd_{grp}", grid=(s // tm,),
        in_specs=[pl.BlockSpec((tm, _GDN_W), lambda i: (i, grp)),
                  pl.BlockSpec((_SHALO, _GDN_W), lambda i: (jnp.maximum(i * per - 1, 0), grp)),
                  pl.BlockSpec((_GDN_SHORT_CONV, _GDN_W), lambda i: (0, grp))],
        out_specs=pl.BlockSpec((tm, _GDN_W), lambda i: (i, 0)),
        out_shape=jax.ShapeDtypeStruct((s, _GDN_W), _F32),
        scratch_shapes=[pltpu.VMEM((tm + _SHALO, _GDN_W), _F32)], compiler_params=_params(("arbitrary",)),
    )(proj, proj, conv_w)


def _gdn_conv_bwd(proj, conv_w, grp, normalize, scale, dy, dproj):
    s = proj.shape[0]
    tm = _CONV_TM
    per = tm // _SHALO
    nt = s // tm
    taps = _GDN_SHORT_CONV

    def body(x_ref, xp_ref, xn_ref, w_ref, dy_ref, dyn_ref, dp_in, dx_ref, dw_ref, buf, dbuf):
        del dp_in
        i = pl.program_id(0)
        last = i == nt - 1
        buf[0:_SHALO, :] = jnp.where(i == 0, 0.0, xp_ref[...])
        buf[_SHALO:_SHALO + tm, :] = x_ref[...]
        buf[_SHALO + tm:, :] = jnp.where(last, 0.0, xn_ref[...])
        conv = _short_taps(buf, w_ref, tm + _SHALO)
        dy_ext = jnp.concatenate([dy_ref[...], jnp.where(last, 0.0, dyn_ref[...])], axis=0)
        for h in range(_GDN_HEADS):
            cols = slice(h * _GDN_DIM, (h + 1) * _GDN_DIM)
            _, pull = jax.vjp(lambda c: _swish_l2(c, normalize, scale), conv[:, cols])
            dbuf[:, cols] = pull(dy_ext[:, cols])[0]
        acc = jnp.zeros((tm, _GDN_W), _F32)
        for t in range(taps):
            o = taps - 1 - t
            acc = acc + w_ref[t:t + 1, :] * dbuf[o:o + tm, :]
        dx_ref[...] = acc.astype(dx_ref.dtype)

        @pl.when(i == 0)
        def _():
            dw_ref[...] = jnp.zeros_like(dw_ref)

        base = _SHALO - (taps - 1)
        for t in range(taps):
            dw_ref[t:t + 1, :] += jnp.sum(dbuf[0:tm, :] * buf[base + t:base + t + tm, :], axis=0, keepdims=True)

    cur = lambda col: pl.BlockSpec((tm, _GDN_W), lambda i: (i, col))
    nxt = lambda col: pl.BlockSpec((_SHALO, _GDN_W), lambda i: (jnp.minimum((i + 1) * per, s // _SHALO - 1), col))
    return pl.pallas_call(
        body, name=f"gdn_conv_bwd_{grp}", grid=(nt,),
        in_specs=[cur(grp), pl.BlockSpec((_SHALO, _GDN_W), lambda i: (jnp.maximum(i * per - 1, 0), grp)), nxt(grp),
                  pl.BlockSpec((taps, _GDN_W), lambda i: (0, grp)), cur(0), nxt(0), _ANY],
        out_specs=[cur(grp), pl.BlockSpec((_SHALO, _GDN_W), lambda i: (0, 0))],
        out_shape=[jax.ShapeDtypeStruct(dproj.shape, dproj.dtype), jax.ShapeDtypeStruct((_SHALO, _GDN_W), _F32)],
        scratch_shapes=[pltpu.VMEM((tm + 2 * _SHALO, _GDN_W), _F32), pltpu.VMEM((tm + _SHALO, _GDN_W), _F32)],
        input_output_aliases={6: 0}, compiler_params=_params(("arbitrary",)),
    )(proj, proj, proj, conv_w, dy, dy, dproj)


def _softplus(z):
    return jnp.maximum(z, 0.0) + jnp.log(1.0 + jnp.exp(-jnp.abs(z)))


def _gates(ba, a_lane, dt_lane):
    lane = lax.broadcasted_iota(jnp.int32, ba.shape, 1)
    g = -jnp.exp(a_lane) * _softplus(ba + dt_lane)
    return jnp.where(lane < _GDN_HEADS, jax.nn.sigmoid(ba), jnp.where(lane < 2 * _GDN_HEADS, g, 0.0))


def _gates_fwd(ba, a_lane, dt_lane):
    return _rowmap("gdn_gates_fwd", _gates, [(ba, 128, 0)], [a_lane, dt_lane], [(128, 128, 0, _F32)],
                   rows=ba.shape[0], tm=_TM)[0]


def _gates_bwd(ba, a_lane, dt_lane, dgates):
    def body(bav, dg8, av, dv):
        _, pull = jax.vjp(_gates, bav, av, dv)
        return pull(jnp.sum(dg8, axis=0))

    return _rowmap("gdn_gates_bwd", body, [(ba, 128, 0), (dgates, 128, 0)], [a_lane, dt_lane], [(128, 128, 0, _BF)],
                   [(1, 128), (1, 128)], rows=ba.shape[0], tm=_TM)


def _out_gate(o, z, g):
    return (_rms(o, g) * _silu(z)).astype(_BF)


def _head_cols():
    return [slice(h * _GDN_DIM, (h + 1) * _GDN_DIM) for h in range(_GDN_HEADS)]


def _out_gate_fwd(o, proj, norm_g):
    def body(ov, zv, gv):
        return jnp.concatenate([_out_gate(ov[:, cl], zv[:, cl], gv) for cl in _head_cols()], axis=1)

    return _rowmap("gdn_out_gate_fwd", body, [(o, _GDN_W, 0), (proj, _GDN_W, 3)], [norm_g], [(_GDN_W, _GDN_W, 0, _BF)],
                   rows=o.shape[0], tm=_CONV_TM)[0]


def _out_gate_bwd(o, proj, norm_g, dy):
    def body(ov, zv, dyv, gv):
        d_o, d_z, d_g = [], [], jnp.zeros_like(gv)
        for cl in _head_cols():
            _, pull = jax.vjp(_out_gate, ov[:, cl], zv[:, cl], gv)
            g_o, g_z, g_g = pull(dyv[:, cl].astype(_BF))
            d_o.append(g_o)
            d_z.append(g_z)
            d_g = d_g + g_g
        return jnp.concatenate(d_o, axis=1), jnp.concatenate(d_z, axis=1), d_g

    return _rowmap("gdn_out_gate_bwd", body, [(o, _GDN_W, 0), (proj, _GDN_W, 3), (dy, _GDN_W, 0)], [norm_g],
                   [(_GDN_W, _GDN_W, 0, _F32), (proj.shape[1], _GDN_W, 3, _BF)], [(1, _GDN_DIM)],
                   rows=o.shape[0], tm=_CONV_TM)


def _plain_in(name, xn, w, layer, tn):
    s, d = xn.shape
    n = w.shape[2]
    return _mm(name, xn, w, grid=(n // tn, s // _TM), a_spec=pl.BlockSpec((_TM, d), lambda j, i: (i, 0)),
               b_spec=pl.BlockSpec((None, d, tn), lambda j, i: (layer, 0, j)), dn=_NN,
               out_shape=jax.ShapeDtypeStruct((s, n), _F32), out_spec=pl.BlockSpec((_TM, tn), lambda j, i: (i, j)))


def _plain_dx(name, dz, w, layer, tn, norm=None):
    s, n = dz.shape
    d = w.shape[1]
    row = pl.BlockSpec((_TM, d), lambda i, j: (i, 0))
    kw = {}
    if norm is not None:
        extra, h, gamma, dh_out = norm
        kw = _norm_bwd_args(h, dh_out, gamma, _TM)
        kw.update(extras=(extra, *kw["extras"]), extra_specs=(row, *kw["extra_specs"]),
                  epi=lambda acc, e, hv, dhv, gv: _norm_bwd_epi(acc + e, hv, dhv, gv))
    return _mm(name, dz, w, grid=(s // _TM, n // tn), a_spec=pl.BlockSpec((_TM, tn), lambda i, j: (i, j)),
               b_spec=pl.BlockSpec((None, d, tn), lambda i, j: (layer, 0, j)), dn=_NT, red=1, acc_shape=(_TM, d),
               out_shape=jax.ShapeDtypeStruct((s, d), _F32), out_spec=row, **kw)


def _plain_dw(name, xn, dz, tn):
    s, d = xn.shape
    n = dz.shape[1]
    return _mm(name, xn, dz, grid=(n // tn, s // _TM), a_spec=pl.BlockSpec((_TM, d), lambda j, i: (i, 0)),
               b_spec=pl.BlockSpec((_TM, tn), lambda j, i: (i, j)), dn=_TN, red=1, acc_shape=(d, tn),
               out_shape=jax.ShapeDtypeStruct((d, n), _BF), out_spec=pl.BlockSpec((d, tn), lambda j, i: (0, j)))


_GDN_QK_SCALE = _GDN_DIM ** -0.5


def _gdn_fwd(h, gamma, w_qkvz, w_ba, w_out, conv_w, a_lane, dt_lane, norm_g, li):
    hn = _rmsnorm_fwd(h, gamma)
    proj = _plain_in("gdn_in", hn, w_qkvz, li, 1024)
    ba = _plain_in("gdn_in_ba", hn, w_ba, li, 128)
    q = _gdn_conv_fwd(proj, conv_w, 0, True, _GDN_QK_SCALE)
    k = _gdn_conv_fwd(proj, conv_w, 1, True, 1.0)
    v = _gdn_conv_fwd(proj, conv_w, 2, False, 1.0)
    gates = _gates_fwd(ba, a_lane, dt_lane)
    o, states = _gdn_chunk_fwd(q, k, v, gates)
    y = _out_gate_fwd(o, proj, norm_g)
    h_new = _proj_residual("gdn_out", y, w_out, li, h, 1.0)
    return h_new, (h, hn, proj, ba, q, k, v, gates, states, o, y)


def _gdn_bwd(dh, saved, gamma, w_qkvz, w_ba, w_out, conv_w, a_lane, dt_lane, norm_g, li, n_ba):
    h, hn, proj, ba, q, k, v, gates, states, o, y = saved
    dy = _proj_out_dy("gdn_dy", dh, w_out, li)
    d_w_out = _wgrad_rows("gdn_dwout", y, dh, 1.0)
    do, dproj, d_norm_g = _out_gate_bwd(o, proj, norm_g, dy)
    dq, dk, dv, dgates = _gdn_chunk_bwd(q, k, v, gates, states, do)
    dproj, dcw_q = _gdn_conv_bwd(proj, conv_w, 0, True, _GDN_QK_SCALE, dq, dproj)
    dproj, dcw_k = _gdn_conv_bwd(proj, conv_w, 1, True, 1.0, dk, dproj)
    dproj, dcw_v = _gdn_conv_bwd(proj, conv_w, 2, False, 1.0, dv, dproj)
    dba, d_a, d_dt = _gates_bwd(ba, a_lane, dt_lane, dgates)
    dhn_ba = _plain_dx("gdn_dhn_ba", dba, w_ba, li, 128)
    dh_in, dgamma = _plain_dx("gdn_dhn", dproj, w_qkvz, li, 1024, norm=(dhn_ba, h, gamma, dh))
    dw_qkvz = _plain_dw("gdn_dwin", hn, dproj, 1024)
    dw_ba = _plain_dw("gdn_dwin_ba", hn, dba, 128)
    d_conv_w = jnp.concatenate([dcw_q, dcw_k, dcw_v], axis=1)[:_GDN_SHORT_CONV]
    d_w_in = jnp.concatenate([dw_qkvz, dw_ba[:, :n_ba]], axis=1)
    d_w_in = jnp.transpose(d_w_in.reshape(d_w_in.shape[0], 4, -1), (1, 0, 2))
    return dh_in, dgamma, (d_conv_w, d_a, d_dt, d_norm_g), d_w_in, d_w_out


_MESH = pl.DeviceIdType.MESH
_ROW_TM = 384


def _place():
    x, y, c = lax.axis_index("x"), lax.axis_index("y"), lax.axis_index("c")
    chips = [(1 - x, y), (x, 1 - y), (1 - x, 1 - y)]
    return x, y, c, chips, [2 * px + py for px, py in chips]


def _remote(src, dst, send_sem, recv_sem, device):
    return pltpu.make_async_remote_copy(src_ref=src, dst_ref=dst, send_sem=send_sem, recv_sem=recv_sem,
                                        device_id=device, device_id_type=_MESH)


def _rows2d(t):
    return t.reshape(-1, t.shape[-1])


def _row_tile(rows):
    for tm in range(min(rows, _ROW_TM) // 8 * 8, 7, -8):
        if rows % tm == 0:
            return tm
    raise ValueError(f"no row tile for {rows} rows")


def _placed_map(name, fn, where, ins, out_rows, cols, out_dtype, out_row, steps, tm, into=None, after=None):
    n_in = len(ins)
    n_extra = (into is not None) + (after is not None)

    def kern(where_ref, *refs):
        del where_ref
        o_ref = refs[n_in + n_extra]
        o_ref[...] = fn(*[r[...] for r in refs[:n_in]]).astype(o_ref.dtype)

    in_specs, operands = [], [where]
    for arr, rowfn in ins:
        operands.append(arr)
        if arr.ndim == 3:
            in_specs.append(pl.BlockSpec((arr.shape[0], tm, cols), lambda i, wh, f=rowfn: (0, f(i, wh), 0)))
        else:
            in_specs.append(pl.BlockSpec((tm, cols), lambda i, wh, f=rowfn: (f(i, wh), 0)))
    aliases = {}
    if into is not None:
        operands.append(into)
        in_specs.append(_ANY)
        aliases = {n_in + 1: 0}
    if after is not None:
        operands.append(after)
        in_specs.append(_ANY)
    spec = pltpu.PrefetchScalarGridSpec(
        num_scalar_prefetch=1, grid=(steps,), in_specs=in_specs,
        out_specs=pl.BlockSpec((tm, cols), lambda i, wh: (out_row(i, wh), 0)))
    return pl.pallas_call(
        kern, name=name, grid_spec=spec, out_shape=jax.ShapeDtypeStruct((out_rows, cols), out_dtype),
        input_output_aliases=aliases, compiler_params=_params(("arbitrary",)),
    )(*operands)


def _cast_slab(name, t, where, after):
    n_l, r, cols = t.shape
    tm = _row_tile(r)
    per = r // tm
    buf = _placed_map("cast_" + name, lambda val: val, where, [(_rows2d(t), lambda i, wh: i)], n_l * 4 * r, cols, _BF,
                      lambda i, wh: ((i // per) * 4 + wh[1]) * per + i % per, n_l * per, tm, after=after)
    return buf.reshape(n_l, 4, r, cols)


_HBM = pl.BlockSpec(memory_space=pltpu.HBM)
_SEM = pl.BlockSpec(memory_space=pltpu.SEMAPHORE)
_EFFECT = pltpu.SideEffectType.DATAFLOW_SIDE_EFFECTING


def _core_layers(n_layers, c, part):
    half = n_layers // 2
    sub = half // part[1]
    return pl.ds(c * half + part[0] * sub, sub), pl.ds((1 - c) * half + part[0] * sub, sub)


def _ici_copies(refs, send, recv, arrivals, part):
    x, y, c, chips, idx = _place()
    me = 2 * x + y
    res = []
    for t, ref in enumerate(refs):
        mine, _ = _core_layers(ref.shape[0], c, part)
        for k in range(3):
            sem = 3 * t + k
            block = ref.at[mine, idx[k] if arrivals else me]
            res.append(_remote(block, block, send.at[sem], recv.at[sem], (*chips[k], c)))
    return res


def _gather_start(name, bufs, after, part):
    n = len(bufs)

    def body(*refs):
        send, recv = refs[n + 1], refs[n + 2]
        for cp in _ici_copies(refs[:n], send, recv, False, part):
            cp.start()
        refs[2 * n + 3][...] = jnp.zeros((8, 128), _F32)

    res = pl.pallas_call(
        body, name=name,
        out_shape=(pltpu.SemaphoreType.DMA((3 * n,)), pltpu.SemaphoreType.DMA((3 * n,)),
                   *[pltpu.HBM(b.shape, b.dtype) for b in bufs], jax.ShapeDtypeStruct((8, 128), _F32)),
        in_specs=[*[_HBM] * n, _ANY], out_specs=(_SEM, _SEM, *[_HBM] * n, pl.BlockSpec(memory_space=pltpu.VMEM)),
        input_output_aliases={t: 2 + t for t in range(n)},
        compiler_params=pltpu.CompilerParams(has_side_effects=_EFFECT),
    )(*[pltpu.with_memory_space_constraint(b, pltpu.HBM) for b in bufs], after)
    return res[0], res[1], list(res[2:2 + n]), res[2 + n]


def _gather_wait(name, send_sem, recv_sem, bufs, after, part):
    n = len(bufs)

    def body(*refs):
        send, recv = refs[n], refs[n + 1]
        for cp in _ici_copies(refs[:n], send, recv, False, part):
            cp.wait_send()
        for cp in _ici_copies(refs[:n], send, recv, True, part):
            cp.wait_recv()

    res = pl.pallas_call(
        body, name=name, out_shape=tuple(pltpu.HBM(b.shape, b.dtype) for b in bufs),
        in_specs=[*[_HBM] * n, _SEM, _SEM, _ANY], out_specs=tuple([_HBM] * n),
        input_output_aliases={t: t for t in range(n)},
        compiler_params=pltpu.CompilerParams(has_side_effects=_EFFECT),
    )(*bufs, send_sem, recv_sem, after)
    return list(res)


def _gather_forward(name, bufs, part):
    n = len(bufs)

    def body(*refs):
        outs, send, recv = refs[n:2 * n], refs[2 * n], refs[2 * n + 1]
        x, y, c, _, idx = _place()
        copies, arrive = [], []
        for t, ref in enumerate(outs):
            mine, other = _core_layers(ref.shape[0], c, part)
            for k in range(3):
                sem = 3 * t + k
                landed, dst = ref.at[mine, idx[k]], ref.at[other, idx[k]]
                copies.append(_remote(landed, landed, send.at[sem], recv.at[sem], (x, y, 1 - c)))
                arrive.append(_remote(dst, dst, send.at[sem], recv.at[sem], (x, y, 1 - c)))
        for cp in copies:
            cp.start()
        for cp in arrive:
            cp.wait_recv()
        for cp in copies:
            cp.wait_send()

    res = pl.pallas_call(
        body, name=name, in_specs=[_ANY] * n, out_specs=[_ANY] * n,
        out_shape=[jax.ShapeDtypeStruct(b.shape, b.dtype) for b in bufs],
        scratch_shapes=[pltpu.SemaphoreType.DMA((3 * n,)), pltpu.SemaphoreType.DMA((3 * n,))],
        input_output_aliases={t: t for t in range(n)},
    )(*bufs)
    return list(res)


def _rs_sibling(name, gs):
    n = len(gs)

    def body(*refs):
        send, recv = refs[2 * n], refs[2 * n + 1]
        x, y, c, _, _ = _place()
        copies = []
        for t in range(n):
            rh = refs[t].shape[1] // 2
            cp = _remote(refs[t].at[:, pl.ds((1 - c) * rh, rh)], refs[n + t], send.at[t], recv.at[t], (x, y, 1 - c))
            cp.start()
            copies.append(cp)
        for cp in copies:
            cp.wait()

    return pl.pallas_call(
        body, name=name, in_specs=[_ANY] * n, out_specs=[_ANY] * n,
        out_shape=[jax.ShapeDtypeStruct((4, g.shape[1] // 2, g.shape[2]), g.dtype) for g in gs],
        scratch_shapes=[pltpu.SemaphoreType.DMA((n,)), pltpu.SemaphoreType.DMA((n,))],
    )(*gs)


def _chip_copies(sums, lands, send, recv):
    x, y, c, chips, idx = _place()
    return [_remote(sums[t].at[idx[k]], lands[t].at[k], send.at[3 * t + k], recv.at[3 * t + k], (*chips[k], c))
            for t in range(len(sums)) for k in range(3)]


def _rs_chips_start(name, sums, after):
    n = len(sums)
    lands = [lax.empty((3,) + s.shape[1:], s.dtype) for s in sums]

    def body(*refs):
        send, recv = refs[2 * n + 1], refs[2 * n + 2]
        for cp in _chip_copies(refs[:n], refs[n:2 * n], send, recv):
            cp.start()
        refs[4 * n + 3][...] = jnp.zeros((8, 128), _F32)

    both = [*sums, *lands]
    res = pl.pallas_call(
        body, name=name,
        out_shape=(pltpu.SemaphoreType.DMA((3 * n,)), pltpu.SemaphoreType.DMA((3 * n,)),
                   *[pltpu.HBM(b.shape, b.dtype) for b in both], jax.ShapeDtypeStruct((8, 128), _F32)),
        in_specs=[*[_HBM] * (2 * n), _ANY],
        out_specs=(_SEM, _SEM, *[_HBM] * (2 * n), pl.BlockSpec(memory_space=pltpu.VMEM)),
        input_output_aliases={t: 2 + t for t in range(2 * n)},
        compiler_params=pltpu.CompilerParams(has_side_effects=_EFFECT),
    )(*[pltpu.with_memory_space_constraint(b, pltpu.HBM) for b in both], after)
    return res[0], res[1], list(res[2:2 + n]), list(res[2 + n:2 + 2 * n]), res[2 + 2 * n]


def _rs_chips_wait(name, send_sem, recv_sem, sums, lands, after):
    n = len(sums)

    def body(*refs):
        send, recv = refs[2 * n], refs[2 * n + 1]
        copies = _chip_copies(refs[:n], refs[n:2 * n], send, recv)
        for cp in copies:
            cp.wait_send()
        for cp in copies:
            cp.wait_recv()

    both = [*sums, *lands]
    res = pl.pallas_call(
        body, name=name, out_shape=tuple(pltpu.HBM(b.shape, b.dtype) for b in both),
        in_specs=[*[_HBM] * (2 * n), _SEM, _SEM, *[_ANY] * len(after)], out_specs=tuple([_HBM] * (2 * n)),
        input_output_aliases={t: t for t in range(2 * n)},
        compiler_params=pltpu.CompilerParams(has_side_effects=_EFFECT),
    )(*both, send_sem, recv_sem, *after)
    return list(res[:n]), list(res[n:])


def _after(x, token):
    def body(x_ref, t_ref, o_ref):
        del x_ref, t_ref, o_ref

    return pl.pallas_call(
        body, name="after", in_specs=[_ANY, _ANY], out_specs=_ANY, out_shape=jax.ShapeDtypeStruct(x.shape, x.dtype),
        input_output_aliases={0: 0},
    )(x, token)


def _rs_join(name, totals, layers):
    n = len(totals)

    def body(*refs):
        outs, send, recv = refs[n:2 * n], refs[2 * n], refs[2 * n + 1]
        x, y, c, _, _ = _place()
        copies, arrive = [], []
        for t in range(n):
            rh = outs[t].shape[1] // 2
            mine = outs[t].at[layers[t], pl.ds(c * rh, rh)]
            landing = outs[t].at[layers[t], pl.ds((1 - c) * rh, rh)]
            copies.append(_remote(mine, mine, send.at[t], recv.at[t], (x, y, 1 - c)))
            arrive.append(_remote(landing, landing, send.at[t], recv.at[t], (x, y, 1 - c)))
        for cp in copies:
            cp.start()
        for cp in arrive:
            cp.wait_recv()
        for cp in copies:
            cp.wait_send()

    return pl.pallas_call(
        body, name=name, in_specs=[_ANY] * n, out_specs=[_ANY] * n,
        out_shape=[jax.ShapeDtypeStruct(b.shape, b.dtype) for b in totals],
        scratch_shapes=[pltpu.SemaphoreType.DMA((n,)), pltpu.SemaphoreType.DMA((n,))],
        input_output_aliases={t: t for t in range(n)},
    )(*totals)


def _pair_sum(g, got, where):
    _, r, cols = g.shape
    rh = r // 2
    tm = _row_tile(rh)
    per, perh = r // tm, rh // tm
    out = _placed_map("rs_add_pair", lambda u, w: u.astype(_F32) + w.astype(_F32), where,
                      [(_rows2d(g), lambda i, wh: (i // perh) * per + wh[0] * perh + i % perh),
                       (_rows2d(got), lambda i, wh: i)],
                      4 * rh, cols, _BF, lambda i, wh: i, 4 * perh, tm)
    return out.reshape(4, rh, cols)


def _sum_of_four(pair, land, where, total, layer, n_layers):
    _, rh, cols = pair.shape
    r = 2 * rh
    tm = _row_tile(rh)
    per, perh = r // tm, rh // tm

    def add_four(own, others):
        return ((own.astype(_F32) + others[0].astype(_F32)) + others[1].astype(_F32)) + others[2].astype(_F32)

    out = _placed_map("rs_add_four", add_four, where,
                      [(_rows2d(pair), lambda i, wh: wh[1] * perh + i), (land, lambda i, wh: i)],
                      n_layers * r, cols, _F32, lambda i, wh: layer * per + wh[0] * perh + i, perh, tm,
                      into=None if total is None else _rows2d(total))
    return out.reshape(n_layers, r, cols)


def _all_reduce_small(pack):
    rows, cols = pack.shape

    def body(p_ref, o_ref, land, send, recv):
        x, y, c = lax.axis_index("x"), lax.axis_index("y"), lax.axis_index("c")
        me = 4 * x + 2 * y + c
        land[me] = p_ref[...]
        copies = []
        for k in range(1, 8):
            bx, by, bc = (k >> 2) & 1, (k >> 1) & 1, k & 1
            peer = (1 - x if bx else x, 1 - y if by else y, 1 - c if bc else c)
            cp = _remote(p_ref, land.at[me], send.at[k - 1], recv.at[k - 1], peer)
            cp.start()
            copies.append((cp, 4 * peer[0] + 2 * peer[1] + peer[2]))
        for k, (cp, pid) in enumerate(copies):
            _remote(p_ref, land.at[pid], send.at[k], recv.at[k], (x, y, c)).wait_recv()
        for cp, _ in copies:
            cp.wait_send()
        acc = land[0]
        for d in range(1, 8):
            acc = acc + land[d]
        o_ref[...] = acc

    vm = pl.BlockSpec(memory_space=pltpu.VMEM)
    return pl.pallas_call(
        body, name="all_reduce_small", in_specs=[vm], out_specs=vm,
        out_shape=jax.ShapeDtypeStruct((rows, cols), _F32),
        scratch_shapes=[pltpu.VMEM((8, rows, cols), _F32), pltpu.SemaphoreType.DMA((7,)), pltpu.SemaphoreType.DMA((7,))],
    )(pack)


def _adamw_math(w, g, m, v):
    m2 = _ADAM_B1 * m + (1.0 - _ADAM_B1) * g
    v2 = _ADAM_B2 * v + (1.0 - _ADAM_B2) * (g * g)
    m_hat = m2 / (1.0 - _ADAM_B1 ** _ADAM_STEP)
    v_hat = v2 / (1.0 - _ADAM_B2 ** _ADAM_STEP)
    delta = -_ADAM_LR * (m_hat / (jnp.sqrt(v_hat) + _ADAM_EPS) + _ADAM_WD * w)
    return delta, m2, v2


def _adamw(w, g, m, v, lo=0, hi=None, into=None):
    shape = w.shape
    n_l = shape[0] if w.ndim == 3 else 1
    hi = n_l if hi is None else hi
    w2, g2, m2, v2 = (_rows2d(t) for t in (w, g, m, v))
    rows, cols = w2.shape
    per_layer = rows // n_l
    tm = _row_tile(per_layer)
    first, steps = lo * per_layer // tm, (hi - lo) * per_layer // tm
    n_into = 0 if into is None else 3

    def body(*refs):
        outs = refs[4 + n_into:]
        for ref, val in zip(outs, _adamw_math(*[r[...] for r in refs[:4]])):
            ref[...] = val

    spec = pl.BlockSpec((tm, cols), lambda i: (first + i, 0))
    sds = jax.ShapeDtypeStruct((rows, cols), _F32)
    outs = pl.pallas_call(
        body, name="adamw", grid=(steps,), in_specs=[spec] * 4 + [_ANY] * n_into, out_specs=[spec] * 3,
        out_shape=[sds] * 3, input_output_aliases={4 + k: k for k in range(n_into)},
        compiler_params=_params(("arbitrary",)),
    )(w2, g2, m2, v2, *([] if into is None else [_rows2d(t) for t in into]))
    return tuple(o.reshape(shape) for o in outs)


def kernel(x, positions, ffn1_norm, ffn1_w_in, ffn1_w_out, mix_norm, ffn2_norm, ffn2_w_in, ffn2_w_out, hyb_w_in, hyb_dw_w, hyb_dw_b, hyb_ln_g, hyb_ln_b, hyb_w_out, gdn_w_in, gdn_conv_w, gdn_A_log, gdn_dt_bias, gdn_norm_g, gdn_w_out, final_norm, loss_target, m_ffn1_norm, m_ffn1_w_in, m_ffn1_w_out, m_mix_norm, m_ffn2_norm, m_ffn2_w_in, m_ffn2_w_out, m_hyb_w_in, m_hyb_dw_w, m_hyb_dw_b, m_hyb_ln_g, m_hyb_ln_b, m_hyb_w_out, m_gdn_w_in, m_gdn_conv_w, m_gdn_A_log, m_gdn_dt_bias, m_gdn_norm_g, m_gdn_w_out, m_final_norm, v_ffn1_norm, v_ffn1_w_in, v_ffn1_w_out, v_mix_norm, v_ffn2_norm, v_ffn2_w_in, v_ffn2_w_out, v_hyb_w_in, v_hyb_dw_w, v_hyb_dw_b, v_hyb_ln_g, v_hyb_ln_b, v_hyb_w_out, v_gdn_w_in, v_gdn_conv_w, v_gdn_A_log, v_gdn_dt_bias, v_gdn_norm_g, v_gdn_w_out, v_final_norm):
    w = dict(ffn1_norm=ffn1_norm, ffn1_w_in=ffn1_w_in, ffn1_w_out=ffn1_w_out, mix_norm=mix_norm, ffn2_norm=ffn2_norm, ffn2_w_in=ffn2_w_in, ffn2_w_out=ffn2_w_out, hyb_w_in=hyb_w_in, hyb_dw_w=hyb_dw_w, hyb_dw_b=hyb_dw_b, hyb_ln_g=hyb_ln_g, hyb_ln_b=hyb_ln_b, hyb_w_out=hyb_w_out, gdn_w_in=gdn_w_in, gdn_conv_w=gdn_conv_w, gdn_A_log=gdn_A_log, gdn_dt_bias=gdn_dt_bias, gdn_norm_g=gdn_norm_g, gdn_w_out=gdn_w_out, final_norm=final_norm)
    m = dict(ffn1_norm=m_ffn1_norm, ffn1_w_in=m_ffn1_w_in, ffn1_w_out=m_ffn1_w_out, mix_norm=m_mix_norm, ffn2_norm=m_ffn2_norm, ffn2_w_in=m_ffn2_w_in, ffn2_w_out=m_ffn2_w_out, hyb_w_in=m_hyb_w_in, hyb_dw_w=m_hyb_dw_w, hyb_dw_b=m_hyb_dw_b, hyb_ln_g=m_hyb_ln_g, hyb_ln_b=m_hyb_ln_b, hyb_w_out=m_hyb_w_out, gdn_w_in=m_gdn_w_in, gdn_conv_w=m_gdn_conv_w, gdn_A_log=m_gdn_A_log, gdn_dt_bias=m_gdn_dt_bias, gdn_norm_g=m_gdn_norm_g, gdn_w_out=m_gdn_w_out, final_norm=m_final_norm)
    v = dict(ffn1_norm=v_ffn1_norm, ffn1_w_in=v_ffn1_w_in, ffn1_w_out=v_ffn1_w_out, mix_norm=v_mix_norm, ffn2_norm=v_ffn2_norm, ffn2_w_in=v_ffn2_w_in, ffn2_w_out=v_ffn2_w_out, hyb_w_in=v_hyb_w_in, hyb_dw_w=v_hyb_dw_w, hyb_dw_b=v_hyb_dw_b, hyb_ln_g=v_hyb_ln_g, hyb_ln_b=v_hyb_ln_b, hyb_w_out=v_hyb_w_out, gdn_w_in=v_gdn_w_in, gdn_conv_w=v_gdn_conv_w, gdn_A_log=v_gdn_A_log, gdn_dt_bias=v_gdn_dt_bias, gdn_norm_g=v_gdn_norm_g, gdn_w_out=v_gdn_w_out, final_norm=v_final_norm)
    return _train_step(x, positions, loss_target, w, m, v)


_WEIGHT_ORDER = ("ffn1_norm", "ffn1_w_in", "ffn1_w_out", "mix_norm", "ffn2_norm", "ffn2_w_in", "ffn2_w_out",
                 "hyb_w_in", "hyb_dw_w", "hyb_dw_b", "hyb_ln_g", "hyb_ln_b", "hyb_w_out", "gdn_w_in", "gdn_conv_w",
                 "gdn_A_log", "gdn_dt_bias", "gdn_norm_g", "gdn_w_out", "final_norm")
_BIG = ("ffn1_w_in", "ffn1_w_out", "ffn2_w_in", "ffn2_w_out", "hyb_w_in", "hyb_w_out", "gdn_w_in", "gdn_w_out")
_PACK_COLS = 1024


def _lane16(vec):
    return jnp.zeros((1, 128), _F32).at[0, _GDN_HEADS:2 * _GDN_HEADS].set(vec)


def _pack_rows(parts):
    rows, where, at = [], [], 0
    for p in parts:
        flat = p.reshape(-1).astype(_F32)
        n = -(-flat.shape[0] // (8 * _PACK_COLS)) * 8
        rows.append(jnp.pad(flat, (0, n * _PACK_COLS - flat.shape[0])).reshape(n, _PACK_COLS))
        where.append((at, n, flat.shape[0], p.shape))
        at += n
    return jnp.concatenate(rows, axis=0), where


def _unpack_rows(pack, where):
    return [pack[at:at + n].reshape(-1)[:size].reshape(shape) for at, n, size, shape in where]


def _train_step(x, positions, loss_target, w, m, v):
    n_layers = w["ffn1_norm"].shape[0]
    n_hyb, n_gdn = w["hyb_w_in"].shape[0], w["gdn_w_in"].shape[0]
    d_model = x.shape[-1]
    xh, target = x[0], loss_target[0]
    chip = 2 * lax.axis_index("x") + lax.axis_index("y")
    where = jnp.stack([lax.axis_index("c"), chip]).astype(jnp.int32)

    def own_columns(shard):
        cols = shard.shape[-1]
        zeros = jnp.zeros(shard.shape[:-1] + (4 * cols,), _F32)
        return lax.dynamic_update_slice_in_dim(zeros, shard, chip * cols, axis=2)

    small_pack, small_where = _pack_rows([own_columns(w["hyb_dw_w"]), own_columns(w["gdn_conv_w"])])
    small_full = _all_reduce_small(small_pack)
    dw_w, conv_w = _unpack_rows(0.5 * small_full, small_where)
    dw_w = jnp.pad(dw_w, ((0, 0), (0, _HALO - _CONV_WIDTH), (0, 0)))

    groups = (("ffn1_w_in", "ffn1_w_out"), ("hyb_w_in", "hyb_w_out"), ("ffn2_w_in", "ffn2_w_out"),
              ("gdn_w_in", "gdn_w_out"))
    steps = [(0, (0, 2)), (1, (0, 1)), (0, (1, 2)), (2, (0, 1)), (3, (0, 1))]
    bufs = [None] * len(groups)
    pending, token = [], small_full
    for si, (gi, part) in enumerate(steps):
        if bufs[gi] is None:
            bufs[gi] = [_cast_slab(n, w[n], where, token) for n in groups[gi]]
        send, recv, bufs[gi], token = _gather_start(f"gather_start_{si}", bufs[gi], token, part)
        pending.append((send, recv))
    gdn_cols = w["gdn_w_in"].shape[-1]
    n_qkvz = 4 * _GDN_HEADS * _GDN_DIM
    n_ba = 4 * gdn_cols - n_qkvz
    wv = {}

    def arrive(si, after):
        gi, part = steps[si]
        send, recv = pending[si]
        bufs[gi] = _gather_wait(f"gather_wait_{si}", send, recv, bufs[gi], after, part)
        bufs[gi] = _gather_forward(f"gather_pass_{si}", bufs[gi], part)
        a, b = bufs[gi]
        kind = groups[gi][0]
        if kind == "gdn_w_in":
            gdn_in = jnp.transpose(a, (0, 2, 1, 3)).reshape(n_gdn, d_model, 4 * gdn_cols)
            wv["gdn_qkvz"] = gdn_in[:, :, :n_qkvz]
            wv["gdn_ba"] = jnp.pad(gdn_in[:, :, n_qkvz:], ((0, 0), (0, 0), (0, 128 - n_ba)))
        else:
            wv[kind] = a
        wv[groups[gi][1]] = b.reshape(b.shape[0], -1, d_model)

    tabs = _rotary_tables(positions)

    def mixer_args(layer):
        i = layer // 2
        if layer % 2 == 0:
            return (wv["hyb_w_in"], wv["hyb_w_out"], dw_w[i], w["hyb_dw_b"][i][None], w["hyb_ln_g"][i][None],
                    w["hyb_ln_b"][i][None], tabs, i)
        return (wv["gdn_qkvz"], wv["gdn_ba"], wv["gdn_w_out"], conv_w[i], _lane16(w["gdn_A_log"][i]),
                _lane16(w["gdn_dt_bias"][i]), w["gdn_norm_g"][i][None], i)

    h = xh
    saved = []
    arrive(0, token)
    for layer in range(n_layers):
        if layer == 1:
            arrive(2, h)
        h, s1 = _ffn_fwd(h, w["ffn1_norm"][layer][None], wv["ffn1_w_in"], wv["ffn1_w_out"], layer)
        gamma = w["mix_norm"][layer][None]
        if layer % 2 == 0:
            if layer == 0:
                arrive(1, h)
            h, sm = _hyb_fwd(h, gamma, *mixer_args(layer))
        else:
            if layer == 1:
                arrive(4, h)
            h, sm = _gdn_fwd(h, gamma, *mixer_args(layer))
        if layer == 0:
            arrive(3, h)
        h, s2 = _ffn_fwd(h, w["ffn2_norm"][layer][None], wv["ffn2_w_in"], wv["ffn2_w_out"], layer)
        saved.append((s1, sm, s2))
    dh, loss_row, d_final = _loss_head(h, w["final_norm"][None], target)

    d_norm = {k: [None] * n_layers for k in ("ffn1_norm", "mix_norm", "ffn2_norm")}
    d_hyb = [None] * n_hyb
    d_gdn = [None] * n_gdn
    grads = {name: None for name in _BIG}
    n_of = {name: w[name].shape[0] for name in _BIG}

    def rs_begin(layer, full, token):
        names = list(full)
        got = _rs_sibling(f"rs_sib_{layer}", [full[n] for n in names])
        sums = [_pair_sum(full[n], g, where) for n, g in zip(names, got)]
        send, recv, sums, lands, token = _rs_chips_start(f"rs_chips_start_{layer}", sums, token)
        return (layer, names, send, recv, sums, lands), token

    def rs_end(state, after):
        layer, names, send, recv, sums, lands = state
        sums, lands = _rs_chips_wait(f"rs_chips_wait_{layer}", send, recv, sums, lands, after)
        index = [layer if n.startswith("ffn") else layer // 2 for n in names]
        for n, li, pair, land in zip(names, index, sums, lands):
            grads[n] = _sum_of_four(pair, land, where, grads[n], li, n_of[n])
        joined = _rs_join(f"rs_join_{layer}", [grads[n] for n in names], index)
        grads.update(zip(names, joined))

    in_flight = None
    for layer in reversed(range(n_layers)):
        s1, sm, s2 = saved[layer]
        full = {}
        dh, d_norm["ffn2_norm"][layer], full["ffn2_w_in"], full["ffn2_w_out"] = _ffn_bwd(
            dh, s2, w["ffn2_norm"][layer][None], wv["ffn2_w_in"], wv["ffn2_w_out"], layer)
        gamma = w["mix_norm"][layer][None]
        if layer % 2 == 0:
            dh, d_norm["mix_norm"][layer], d_hyb[layer // 2], full["hyb_w_in"], full["hyb_w_out"] = _hyb_bwd(
                dh, sm, gamma, *mixer_args(layer))
        else:
            dh, d_norm["mix_norm"][layer], d_gdn[layer // 2], full["gdn_w_in"], full["gdn_w_out"] = _gdn_bwd(
                dh, sm, gamma, *mixer_args(layer), n_ba)
        dh, d_norm["ffn1_norm"][layer], full["ffn1_w_in"], full["ffn1_w_out"] = _ffn_bwd(
            dh, s1, w["ffn1_norm"][layer][None], wv["ffn1_w_in"], wv["ffn1_w_out"], layer)
        if in_flight is not None:
            rs_end(in_flight, [dh])
        if layer == 0:
            stack = lambda rows: jnp.concatenate(rows, axis=0)
            misc = jnp.concatenate([stack([g[1] for g in d_gdn])[:, _GDN_HEADS:2 * _GDN_HEADS].reshape(-1),
                                    stack([g[2] for g in d_gdn])[:, _GDN_HEADS:2 * _GDN_HEADS].reshape(-1),
                                    stack([g[3] for g in d_gdn]).reshape(-1), loss_row[0, :1]])
            parts = [stack(d_norm["ffn1_norm"]), stack(d_norm["mix_norm"]), stack(d_norm["ffn2_norm"]), d_final,
                     stack([g[1] for g in d_hyb]), stack([g[2] for g in d_hyb]), stack([g[3] for g in d_hyb]), misc,
                     jnp.stack([g[0][:_CONV_WIDTH] for g in d_hyb]), jnp.stack([g[0] for g in d_gdn])]
            pack, pack_where = _pack_rows(parts)
            token = small_sum = _all_reduce_small(pack)
        in_flight, token = rs_begin(layer, full, token)
        if layer > 0:
            dh = _after(dh, token)
    grad_x = dh[None]

    red = _unpack_rows(small_sum, pack_where)
    n_a = n_gdn * _GDN_HEADS
    misc = red[7]
    cw = w["hyb_dw_w"].shape[-1]
    gw = w["gdn_conv_w"].shape[-1]
    grads.update(
        ffn1_norm=red[0], mix_norm=red[1], ffn2_norm=red[2], final_norm=red[3].reshape(-1),
        hyb_dw_b=red[4], hyb_ln_g=red[5], hyb_ln_b=red[6],
        gdn_A_log=misc[:n_a].reshape(n_gdn, _GDN_HEADS), gdn_dt_bias=misc[n_a:2 * n_a].reshape(n_gdn, _GDN_HEADS),
        gdn_norm_g=misc[2 * n_a:2 * n_a + n_gdn * _GDN_DIM].reshape(n_gdn, _GDN_DIM),
        hyb_dw_w=lax.dynamic_slice_in_dim(red[8], chip * cw, cw, axis=2),
        gdn_conv_w=lax.dynamic_slice_in_dim(red[9], chip * gw, gw, axis=2))
    loss = misc[2 * n_a + n_gdn * _GDN_DIM]

    small = [n for n in _WEIGHT_ORDER if n not in _BIG]
    packs = [_pack_rows([src[n] for n in small]) for src in (w, grads, m, v)]
    outs = _adamw(packs[0][0], _after(packs[1][0], token), packs[2][0], packs[3][0])
    for name in _BIG:
        grads[name] = _after(grads[name], token)
    delta, new_m, new_v = {}, {}, {}
    for res, o in zip((delta, new_m, new_v), outs):
        res.update(dict(zip(small, _unpack_rows(o, packs[0][1]))))
    late = in_flight[1]
    early = {name: _adamw(w[name], grads[name], m[name], v[name], lo=1 if name in late else 0) for name in _BIG}
    rs_end(in_flight, [token, outs[0], *[early[name][0] for name in _BIG]])
    for name in _BIG:
        res = early[name]
        if name in late:
            res = _adamw(w[name], grads[name], m[name], v[name], lo=0, hi=1, into=res)
        delta[name], new_m[name], new_v[name] = res
    return (loss, grad_x, *[grads[n] for n in _WEIGHT_ORDER], *[delta[n] for n in _WEIGHT_ORDER],
            *[new_m[n] for n in _WEIGHT_ORDER], *[new_v[n] for n in _WEIGHT_ORDER])
```

```python
import functools

import jax
import jax.numpy as jnp
from jax import lax
from jax.experimental import pallas as pl
from jax.experimental.pallas import tpu as pltpu

_BF = jnp.bfloat16
_F32 = jnp.float32
_NORM_EPS = 1e-6
_VMEM_LIMIT_BYTES = 56 * 1024 * 1024

_A_HEADS = 8
_A_HEAD_DIM = 64
_WIN_BLOCK = 128
_DILATIONS = (1, 4, 16)
_CONV_WIDTH = 31
_GDN_HEADS = 8
_GDN_DIM = 128
_GDN_CHUNK = 64
_GDN_SHORT_CONV = 4

_ADAM_LR = 0.001
_ADAM_B1 = 0.9
_ADAM_B2 = 0.999
_ADAM_EPS = 1e-08
_ADAM_WD = 0.01
_ADAM_STEP = 10


def _params(sem):
    return pltpu.CompilerParams(dimension_semantics=sem, vmem_limit_bytes=_VMEM_LIMIT_BYTES)


def _dot(a, b, dn=(((1,), (0,)), ((), ()))):
    return lax.dot_general(a.astype(_BF), b.astype(_BF), dn, preferred_element_type=_F32)


def _dot_nt(a, b):
    return _dot(a, b, (((1,), (1,)), ((), ())))


def _dot_tn(a, b):
    return _dot(a, b, (((0,), (0,)), ((), ())))


_NN = (((1,), (0,)), ((), ()))
_NT = (((1,), (1,)), ((), ()))
_TN = (((0,), (0,)), ((), ()))
_ANY = pl.BlockSpec(memory_space=pl.ANY)


def _mm(name, a, b, *, grid, a_spec, b_spec, dn, out_shape, out_spec, red=None, acc_shape=None,
        pre_a=None, pre_b=None, epi=None, extras=(), extra_specs=(), sum_shape=None):
    n_ex = len(extras)
    n_in = 2 + n_ex
    nk = grid[red] if red is not None else 1
    n_out = 1 if sum_shape is None else 2

    def body(*refs):
        a_ref, b_ref = refs[0], refs[1]
        ex = refs[2:2 + n_ex]
        o_ref = refs[n_in]
        av, bv = a_ref[...], b_ref[...]
        if pre_a is not None:
            av = pre_a(av)
        if pre_b is not None:
            bv = pre_b(bv)
        part = lax.dot_general(av.astype(_BF), bv.astype(_BF), dn, preferred_element_type=_F32)

        def finish(acc):
            res = epi(acc, *[e[...] for e in ex]) if epi is not None else acc
            if sum_shape is not None:
                res, term = res
                s_ref = refs[n_in + 1]
                first = functools.reduce(jnp.logical_and, [pl.program_id(ax) == 0 for ax in range(len(grid)) if ax != red])

                @pl.when(first)
                def _():
                    s_ref[...] = jnp.zeros_like(s_ref)

                s_ref[...] += term
            if isinstance(res, tuple):
                for idx, val in enumerate(res):
                    o_ref[idx] = val.astype(o_ref.dtype)
            else:
                o_ref[...] = res.astype(o_ref.dtype)

        if red is None:
            finish(part)
        else:
            acc_ref = refs[n_in + n_out]
            k = pl.program_id(red)

            @pl.when(k == 0)
            def _():
                acc_ref[...] = part

            @pl.when(k > 0)
            def _():
                acc_ref[...] += part

            @pl.when(k == nk - 1)
            def _():
                finish(acc_ref[...])

    if sum_shape is not None:
        out_shape = [out_shape, jax.ShapeDtypeStruct(sum_shape, _F32)]
        out_spec = [out_spec, pl.BlockSpec(sum_shape, lambda *g: (0,) * len(sum_shape))]
    return pl.pallas_call(
        body, name=name, grid=grid, in_specs=[a_spec, b_spec, *extra_specs], out_specs=out_spec, out_shape=out_shape,
        scratch_shapes=[pltpu.VMEM(acc_shape, _F32)] if red is not None else [],
        compiler_params=_params(("arbitrary",) * len(grid)),
    )(a, b, *extras)


def _rowmap(name, body, tiled, params, tile_outs, param_outs=(), *, rows, tm, ncol=1, col_outer=False,
            into=None):
    nt, npar, nto, npo = len(tiled), len(params), len(tile_outs), len(param_outs)
    assert rows % tm == 0, (name, rows, tm)
    nrow = rows // tm
    grid = (ncol, nrow) if col_outer else (nrow, ncol)

    def ij(g):
        return (g[1], g[0]) if col_outer else (g[0], g[1])

    def cidx(col, j):
        return col(j) if callable(col) else col

    in_specs, operands = [], []
    for arr, width, col in tiled:
        operands.append(arr)
        if arr.ndim == 3:
            in_specs.append(pl.BlockSpec((arr.shape[0], tm, width),
                                         lambda *g, col=col: (0, ij(g)[0], cidx(col, ij(g)[1]))))
        else:
            in_specs.append(pl.BlockSpec((tm, width), lambda *g, col=col: (ij(g)[0], cidx(col, ij(g)[1]))))
    for p in params:
        if isinstance(p, tuple):
            arr, width, col = p
            operands.append(arr)
            in_specs.append(pl.BlockSpec((arr.shape[0], width), lambda *g, col=col: (0, cidx(col, ij(g)[1]))))
        else:
            operands.append(p)
            in_specs.append(pl.BlockSpec(p.shape, lambda *g, nd=p.ndim: (0,) * nd))
    out_shape, out_specs = [], []
    for total, width, col, dt in tile_outs:
        out_shape.append(jax.ShapeDtypeStruct((rows, total), dt))
        out_specs.append(pl.BlockSpec((tm, width), lambda *g, col=col: (ij(g)[0], cidx(col, ij(g)[1]))))
    for shp in param_outs:
        out_shape.append(jax.ShapeDtypeStruct(shp, _F32))
        out_specs.append(pl.BlockSpec(shp, lambda *g, nd=len(shp): (0,) * nd))
    n_in = nt + npar
    aliases = {}
    if into is not None:
        operands.append(into[0])
        in_specs.append(_ANY)
        aliases = {n_in: into[1]}
        n_in += 1

    def kern(*refs):
        vals = [r[...] for r in refs[:nt + npar]]
        touts = refs[n_in:n_in + nto]
        pouts = refs[n_in + nto:]
        res = body(*vals)
        if not isinstance(res, (tuple, list)):
            res = (res,)
        for r, val in zip(touts, res[:nto]):
            r[...] = val.astype(r.dtype)
        if npo:
            @pl.when((pl.program_id(0) == 0) & (pl.program_id(1) == 0))
            def _():
                for r in pouts:
                    r[...] = jnp.zeros_like(r)

            for r, val in zip(pouts, res[nto:]):
                r[...] += val

    return pl.pallas_call(
        kern, name=name, grid=grid, in_specs=in_specs, out_specs=out_specs, out_shape=out_shape,
        input_output_aliases=aliases, compiler_params=_params(("arbitrary", "arbitrary")),
    )(*operands)


def _vjp_body(fn, n_in, diff, n_ct):
    def body(*vals):
        ins, cts = vals[:n_in], vals[n_in:n_in + n_ct]
        outs, pull = jax.vjp(fn, *ins)
        single = not isinstance(outs, (tuple, list))
        if single:
            grads = pull(cts[0].astype(outs.dtype))
        else:
            grads = pull(tuple(c.astype(o.dtype) for c, o in zip(cts, outs)))
        return tuple(grads[i] for i in diff)
    return body


_TM = 512
_TM_WIDE = 1024


def _rms(h, gamma):
    hf = h.astype(_F32)
    return hf * lax.rsqrt(jnp.mean(hf * hf, axis=-1, keepdims=True) + _NORM_EPS) * gamma


def _rmsnorm_fwd(h, gamma):
    s, d = h.shape
    return _rowmap("rmsnorm_fwd", lambda hv, gv: _rms(hv, gv).astype(_BF), [(h, d, 0)], [gamma],
                   [(d, d, 0, _BF)], rows=s, tm=_TM)[0]


def _norm_bwd_epi(dxn, hv, dhv, gv):
    _, pull = jax.vjp(_rms, hv, gv)
    dh, dg = pull(dxn)
    return dhv + dh, dg


def _norm_bwd_args(h, dh_out, gamma, tm):
    d = h.shape[1]
    row = pl.BlockSpec((tm, d), lambda i, j: (i, 0))
    return dict(extras=(h, dh_out, gamma), extra_specs=(row, row, pl.BlockSpec((1, d), lambda i, j: (0, 0))),
                sum_shape=(1, d))


def _silu(x):
    return x * jax.nn.sigmoid(x)


def _ffn_up(h, gamma, w_in, layer):
    s, d = h.shape
    ns = w_in.shape[-1]
    tm = _CONV_TM

    def body(h_ref, gam_ref, w_ref, x_ref, g_ref, u_ref, a_ref):
        xv = _rms(h_ref[...], gam_ref[...]).astype(_BF)
        x_ref[...] = xv
        for half in range(2):
            cols = slice(half * ns, (half + 1) * ns)
            g = jnp.dot(xv, w_ref[half], preferred_element_type=_F32)
            u = jnp.dot(xv, w_ref[half + 2], preferred_element_type=_F32)
            g_ref[:, cols] = g.astype(_BF)
            u_ref[:, cols] = u.astype(_BF)
            a_ref[:, cols] = (_silu(g) * u).astype(_BF)

    row = pl.BlockSpec((tm, d), lambda i: (i, 0))
    out = pl.BlockSpec((tm, 2 * ns), lambda i: (i, 0))
    sds = jax.ShapeDtypeStruct((s, 2 * ns), _BF)
    return pl.pallas_call(
        body, name="ffn_up", grid=(s // tm,),
        in_specs=[row, pl.BlockSpec((1, d), lambda i: (0, 0)), pl.BlockSpec((None, 4, d, ns), lambda i: (layer, 0, 0, 0))],
        out_specs=[row, out, out, out], out_shape=[jax.ShapeDtypeStruct((s, d), _BF), sds, sds, sds],
        compiler_params=_params(("arbitrary",)),
    )(h, gamma, w_in)


def _proj_residual(name, y, w, layer, h, scale):
    s, k = y.shape
    d = w.shape[-1]
    return _mm(name, y, w, grid=(s // _TM,), a_spec=pl.BlockSpec((_TM, k), lambda i: (i, 0)),
               b_spec=pl.BlockSpec((None, k, d), lambda i: (layer, 0, 0)), dn=_NN,
               out_shape=jax.ShapeDtypeStruct((s, d), _F32), out_spec=pl.BlockSpec((_TM, d), lambda i: (i, 0)),
               epi=lambda acc, hv: hv + scale * acc, extras=(h,),
               extra_specs=(pl.BlockSpec((_TM, d), lambda i: (i, 0)),))


def _ffn_dact(dh, w_out, layer, g, u):
    s, d = dh.shape
    f = g.shape[1]
    ns = f // 2

    def epi(da, gv, uv):
        gf, uf = gv.astype(_F32), uv.astype(_F32)
        sg = jax.nn.sigmoid(gf)
        return da * uf * (sg * (1.0 + gf * (1.0 - sg))), da * gf * sg

    tm = _CONV_TM
    gu = pl.BlockSpec((tm, f), lambda i: (i, 0))
    return _mm("ffn_dact", dh, w_out, grid=(s // tm,), a_spec=pl.BlockSpec((tm, d), lambda i: (i, 0)),
               b_spec=pl.BlockSpec((None, f, d), lambda i: (layer, 0, 0)), dn=_NT,
               out_shape=jax.ShapeDtypeStruct((2, s, f), _BF), out_spec=pl.BlockSpec((2, tm, f), lambda i: (0, i, 0)),
               pre_a=lambda v: 0.5 * v, epi=epi, extras=(g, u), extra_specs=(gu, gu))


def _wgrad_rows(name, y, dh, scale):
    s, k = y.shape
    d = dh.shape[1]
    out = _mm(name, y, dh, grid=(s // _TM,), a_spec=pl.BlockSpec((_TM, k), lambda i: (i, 0)),
              b_spec=pl.BlockSpec((_TM, d), lambda i: (i, 0)), dn=_TN, red=0, acc_shape=(k, d),
              out_shape=jax.ShapeDtypeStruct((k, d), _BF), out_spec=pl.BlockSpec((k, d), lambda i: (0, 0)),
              pre_b=(lambda v: scale * v) if scale != 1.0 else None)
    return out.reshape(4, k // 4, d)


def _ffn_dxn(dz, w_in, layer, h, gamma, dh_out):
    _, s, f = dz.shape
    d, ns = w_in.shape[2], w_in.shape[3]
    tm = _TM_WIDE
    return _mm("ffn_dxn", dz, w_in, grid=(s // tm, 4),
               a_spec=pl.BlockSpec((None, tm, ns), lambda i, j: (j // 2, i, j % 2)),
               b_spec=pl.BlockSpec((None, None, d, ns), lambda i, j: (layer, j, 0, 0)), dn=_NT, red=1,
               acc_shape=(tm, d), out_shape=jax.ShapeDtypeStruct((s, d), _F32),
               out_spec=pl.BlockSpec((tm, d), lambda i, j: (i, 0)), epi=_norm_bwd_epi,
               **_norm_bwd_args(h, dh_out, gamma, tm))


def _ffn_dwin(xn, dz):
    s, d = xn.shape
    ns = dz.shape[2] // 2
    tm = _TM_WIDE
    return _mm("ffn_dwin", xn, dz, grid=(4, s // tm), a_spec=pl.BlockSpec((tm, d), lambda j, i: (i, 0)),
               b_spec=pl.BlockSpec((None, tm, ns), lambda j, i: (j // 2, i, j % 2)), dn=_TN, red=1,
               acc_shape=(d, ns), out_shape=jax.ShapeDtypeStruct((4, d, ns), _BF),
               out_spec=pl.BlockSpec((None, d, ns), lambda j, i: (j, 0, 0)))


def _ffn_fwd(h, gamma, w_in, w_out, layer):
    xn, g, u, act = _ffn_up(h, gamma, w_in, layer)
    h_new = _proj_residual("ffn_down", act, w_out, layer, h, 0.5)
    return h_new, (h, xn, g, u, act)


def _ffn_bwd(dh, saved, gamma, w_in, w_out, layer):
    h, xn, g, u, act = saved
    dz = _ffn_dact(dh, w_out, layer, g, u)
    d_w_out = _wgrad_rows("ffn_dwout", act, dh, 0.5)
    dh_in, dgamma = _ffn_dxn(dz, w_in, layer, h, gamma, dh)
    d_w_in = _ffn_dwin(xn, dz)
    return dh_in, dgamma, d_w_in, d_w_out


def _loss_head(h, gamma, target):
    s, d = h.shape

    def body(hv, tv, gv):
        y, pull = jax.vjp(_rms, hv, gv)
        err = y - tv
        loss = 0.5 * jnp.sum(jnp.mean(err * err, axis=-1, keepdims=True), axis=0, keepdims=True)
        dh, dg = pull(err * (1.0 / d))
        return dh, jnp.broadcast_to(loss, (1, 128)), dg

    return _rowmap("loss_head", body, [(h, d, 0), (target, d, 0)], [gamma], [(d, d, 0, _F32)],
                   [(1, 128), (1, d)], rows=s, tm=_TM)


_ROT = _A_HEAD_DIM // 4
_ROPE_THETA = 500000.0
_A_WIDTH = _A_HEADS * _A_HEAD_DIM
_NEG = -1e30


def _rotary_tables(positions):
    inv_freq = jnp.power(jnp.float32(_ROPE_THETA), -jnp.arange(0, _ROT, 2, dtype=_F32) / _ROT)
    ang = positions.reshape(-1, 1).astype(_F32) * inv_freq
    cos, sin = jnp.cos(ang), jnp.sin(ang)
    s = ang.shape[0]
    pad = jnp.zeros((s, _A_HEAD_DIM - _ROT), _F32)
    c_head = jnp.concatenate([cos, cos, pad + 1.0], axis=1)
    s_head = jnp.concatenate([-sin, sin, pad], axis=1)
    c2, s2 = jnp.tile(c_head, (1, 2)), jnp.tile(s_head, (1, 2))
    scale = _A_HEAD_DIM ** -0.5
    tab_c = jnp.stack([scale * c2, c2, jnp.ones_like(c2)])
    tab_s = jnp.stack([scale * s2, s2, jnp.zeros_like(s2)])
    return tab_c, tab_s


def _pair_swap(z):
    lane = lax.broadcasted_iota(jnp.int32, z.shape, 1) % _A_HEAD_DIM
    half = _ROT // 2
    up = jnp.where(lane < _ROT, pltpu.roll(z, half, axis=1), 0.0)
    return jnp.where(lane < half, pltpu.roll(z, 128 - half, axis=1), up)


def _rot_apply(x, c, s, transpose):
    outs = []
    for k in range(x.shape[1] // 128):
        xs = x[:, k * 128:(k + 1) * 128]
        outs.append(c * xs + (_pair_swap(s * xs) if transpose else s * _pair_swap(xs)))
    return jnp.concatenate(outs, axis=1)


def _qkv_prep(proj, tab_c, tab_s):
    s = proj.shape[0]
    w = _A_WIDTH

    def body(x_ref, c_ref, s_ref, o_ref):
        o_ref[...] = _rot_apply(x_ref[...], c_ref[...], s_ref[...], False).astype(_BF)

    tab = pl.BlockSpec((None, _TM, 128), lambda i, j: (j, i, 0))
    blk = pl.BlockSpec((_TM, w), lambda i, j: (i, j))
    return pl.pallas_call(
        body, name="qkv_prep", grid=(s // _TM, 3), in_specs=[blk, tab, tab], out_specs=blk,
        out_shape=jax.ShapeDtypeStruct((s, 3 * w), _BF), compiler_params=_params(("arbitrary", "arbitrary")),
    )(proj, tab_c, tab_s)


def _head_masks():
    lane = lax.broadcasted_iota(jnp.int32, (_WIN_BLOCK, 128), 1)
    return [lane // _A_HEAD_DIM == hh for hh in range(2)]


def _band_masks(first_block):
    qi = lax.broadcasted_iota(jnp.int32, (_WIN_BLOCK, _WIN_BLOCK), 0)
    kj = lax.broadcasted_iota(jnp.int32, (_WIN_BLOCK, _WIN_BLOCK), 1)
    return (kj >= qi) & jnp.logical_not(first_block), kj <= qi


def _attn_fwd(qkv, dil):
    s = qkv.shape[0]
    w = _A_WIDTH
    length = s // dil
    nb = length // _WIN_BLOCK
    view = qkv.reshape(length, dil * 3 * w)

    def body(q_ref, kp_ref, kc_ref, vp_ref, vc_ref, o_ref, l_ref):
        kj = lax.broadcasted_iota(jnp.int32, (_WIN_BLOCK, _WIN_BLOCK), 0)
        qi = lax.broadcasted_iota(jnp.int32, (_WIN_BLOCK, _WIN_BLOCK), 1)
        allow_p = (kj >= qi) & jnp.logical_not(pl.program_id(1) == 0)
        allow_c = kj <= qi
        hm = _head_masks()
        for sl in range(w // 128):
            cols = slice(sl * 128, (sl + 1) * 128)
            q, kp, kc, vp, vc = q_ref[:, cols], kp_ref[:, cols], kc_ref[:, cols], vp_ref[:, cols], vc_ref[:, cols]
            o_s = jnp.zeros((_WIN_BLOCK, 128), _F32)
            l_s = jnp.zeros((_WIN_BLOCK, 128), _F32)
            for hh in range(2):
                qh = jnp.where(hm[hh], q, jnp.zeros_like(q))
                sp = jnp.where(allow_p, _dot_nt(kp, qh), _NEG)
                sc = jnp.where(allow_c, _dot_nt(kc, qh), _NEG)
                m = jnp.maximum(jnp.max(sp, axis=0, keepdims=True), jnp.max(sc, axis=0, keepdims=True))
                pp, pc = jnp.exp(sp - m), jnp.exp(sc - m)
                den = jnp.sum(pp, axis=0, keepdims=True) + jnp.sum(pc, axis=0, keepdims=True)
                inv = 1.0 / den
                oh = _dot_tn(pp * inv, vp) + _dot_tn(pc * inv, vc)
                lse = jnp.sum(jnp.where(kj == qi, m + jnp.log(den), 0.0), axis=1, keepdims=True)
                o_s = jnp.where(hm[hh], oh, o_s)
                l_s = jnp.where(hm[hh], lse, l_s)
            o_ref[:, cols] = o_s.astype(o_ref.dtype)
            l_ref[:, cols] = l_s

    def at(col, prev):
        if prev:
            return pl.BlockSpec((_WIN_BLOCK, w), lambda r, n: (jnp.maximum(n - 1, 0), 3 * r + col))
        return pl.BlockSpec((_WIN_BLOCK, w), lambda r, n: (n, 3 * r + col))

    out = pl.BlockSpec((_WIN_BLOCK, w), lambda r, n: (n, r))
    sds = jax.ShapeDtypeStruct((length, dil * w), _F32)
    o, lse = pl.pallas_call(
        body, name=f"attn_fwd_d{dil}", grid=(dil, nb),
        in_specs=[at(0, False), at(1, True), at(1, False), at(2, True), at(2, False)],
        out_specs=[out, out], out_shape=[jax.ShapeDtypeStruct(sds.shape, _BF), sds],
        compiler_params=_params(("arbitrary", "arbitrary")),
    )(view, view, view, view, view)
    return o.reshape(s, w), lse.reshape(s, w)


def _attn_bwd(qkv, o, lse, do, dlse, dil):
    s = qkv.shape[0]
    w = _A_WIDTH
    length = s // dil
    nb = length // _WIN_BLOCK
    total = dil * nb
    view = qkv.reshape(length, dil * 3 * w)
    o, lse, do, dlse = (t.reshape(length, dil * w) for t in (o, lse, do, dlse))

    def body(q_ref, kp_ref, kc_ref, vp_ref, vc_ref, o_ref, l_ref, do_ref, dl_ref, out_ref,
             car_q, car_k, car_v, cur_q, cur_kc, cur_vc, cur_kp, cur_vp):
        t = pl.program_id(0)
        working = t < total

        @pl.when(t == 0)
        def _():
            car_q[...] = jnp.zeros_like(car_q)
            car_k[...] = jnp.zeros_like(car_k)
            car_v[...] = jnp.zeros_like(car_v)

        @pl.when(jnp.logical_not(working))
        def _():
            cur_kp[...] = jnp.zeros_like(cur_kp)
            cur_vp[...] = jnp.zeros_like(cur_vp)

        @pl.when(working)
        def _():
            allow_p, allow_c = _band_masks(t % nb == 0)
            hm = _head_masks()
            for sl in range(w // 128):
                cols = slice(sl * 128, (sl + 1) * 128)
                q, kp, kc, vp, vc = q_ref[:, cols], kp_ref[:, cols], kc_ref[:, cols], vp_ref[:, cols], vc_ref[:, cols]
                ov, dov = o_ref[:, cols].astype(_F32), do_ref[:, cols].astype(_F32)
                lv, dlv = l_ref[:, cols], dl_ref[:, cols]
                zero = jnp.zeros((_WIN_BLOCK, 128), _F32)
                dq_s, dkp_s, dkc_s, dvp_s, dvc_s = zero, zero, zero, zero, zero
                for hh in range(2):
                    qh = jnp.where(hm[hh], q, jnp.zeros_like(q))
                    doh = jnp.where(hm[hh], dov, 0.0).astype(_BF)
                    lse_h = jnp.sum(jnp.where(hm[hh], lv, 0.0), axis=1, keepdims=True) * (1.0 / _A_HEAD_DIM)
                    delta = jnp.sum(jnp.where(hm[hh], dov * ov, 0.0), axis=1, keepdims=True)
                    dl = jnp.sum(jnp.where(hm[hh], dlv, 0.0), axis=1, keepdims=True)
                    pp = jnp.exp(jnp.where(allow_p, _dot_nt(qh, kp), _NEG) - lse_h)
                    pc = jnp.exp(jnp.where(allow_c, _dot_nt(qh, kc), _NEG) - lse_h)
                    dsp = (pp * (_dot_nt(doh, vp) - delta + dl)).astype(_BF)
                    dsc = (pc * (_dot_nt(doh, vc) - delta + dl)).astype(_BF)
                    dq_s = dq_s + jnp.where(hm[hh], _dot(dsp, kp) + _dot(dsc, kc), 0.0)
                    dkp_s = dkp_s + _dot_tn(dsp, qh)
                    dkc_s = dkc_s + _dot_tn(dsc, qh)
                    dvp_s = dvp_s + _dot_tn(pp, doh)
                    dvc_s = dvc_s + _dot_tn(pc, doh)
                cur_q[:, cols] = dq_s
                cur_kp[:, cols] = dkp_s
                cur_kc[:, cols] = dkc_s
                cur_vp[:, cols] = dvp_s
                cur_vc[:, cols] = dvc_s

        out_ref[:, 0:w] = car_q[...].astype(out_ref.dtype)
        out_ref[:, w:2 * w] = (car_k[...] + cur_kp[...]).astype(out_ref.dtype)
        out_ref[:, 2 * w:3 * w] = (car_v[...] + cur_vp[...]).astype(out_ref.dtype)

        @pl.when(working)
        def _():
            car_q[...] = cur_q[...]
            car_k[...] = cur_kc[...]
            car_v[...] = cur_vc[...]

    def place(t):
        tc = jnp.minimum(t, total - 1)
        return tc % nb, tc // nb

    def at(col, prev):
        if prev:
            return pl.BlockSpec((_WIN_BLOCK, w), lambda t: (jnp.maximum(place(t)[0] - 1, 0), 3 * place(t)[1] + col))
        return pl.BlockSpec((_WIN_BLOCK, w), lambda t: (place(t)[0], 3 * place(t)[1] + col))

    blk = pl.BlockSpec((_WIN_BLOCK, w), lambda t: place(t))
    scratch = [pltpu.VMEM((_WIN_BLOCK, w), _F32) for _ in range(8)]
    out = pl.pallas_call(
        body, name=f"attn_bwd_d{dil}", grid=(total + 1,),
        in_specs=[at(0, False), at(1, True), at(1, False), at(2, True), at(2, False), blk, blk, blk, blk],
        out_specs=pl.BlockSpec((_WIN_BLOCK, 3 * w), lambda t: place(jnp.maximum(t - 1, 0))),
        out_shape=jax.ShapeDtypeStruct((length, dil * 3 * w), _BF), scratch_shapes=scratch,
        compiler_params=_params(("arbitrary",)),
    )(view, view, view, view, view, o, lse, do, dlse)
    return out.reshape(s, 3 * w)


def _mix(o1, o2, o3, l1, l2, l3):
    m = jnp.maximum(jnp.maximum(l1, l2), l3)
    e1, e2, e3 = jnp.exp(l1 - m), jnp.exp(l2 - m), jnp.exp(l3 - m)
    return (e1 * o1 + e2 * o2 + e3 * o3) / (e1 + e2 + e3)


def _attn_mix_fwd(outs, lses):
    s = outs[0].shape[0]
    tiles = [(t, _A_WIDTH, 0) for t in (*outs, *lses)]
    return _rowmap("attn_mix_fwd", _mix, tiles, [], [(2 * _A_WIDTH, _A_WIDTH, 0, _BF)], rows=s, tm=_CONV_TM)[0]


def _attn_mix_bwd(outs, lses, dy):
    s = outs[0].shape[0]
    tiles = [(t, _A_WIDTH, 0) for t in (*outs, *lses, dy)]
    body = _vjp_body(_mix, 6, range(6), 1)
    return _rowmap("attn_mix_bwd", body, tiles, [], [(_A_WIDTH, _A_WIDTH, 0, _BF)] * 3 + [(_A_WIDTH, _A_WIDTH, 0, _F32)] * 3, rows=s, tm=_CONV_TM)


_CONV_TM = 256
_HALO = 32


def _ln_silu(pre, g, b):
    mu = jnp.mean(pre, axis=-1, keepdims=True)
    xc = pre - mu
    y = xc * lax.rsqrt(jnp.mean(xc * xc, axis=-1, keepdims=True) + _NORM_EPS)
    return _silu(y * g + b)


def _glu_into(buf, ua_ref, ub_ref, uah_ref, ubh_ref, first):
    halo = uah_ref[...] * jax.nn.sigmoid(ubh_ref[...])
    buf[0:_HALO, :] = jnp.where(first, 0.0, halo)
    buf[_HALO:, :] = ua_ref[...] * jax.nn.sigmoid(ub_ref[...])


_SHIFT_ROWS = _CONV_TM + _HALO - 8


def _shift_copies(sh, buf):
    for b in range(1, 8):
        sh[b - 1] = buf[b:b + _SHIFT_ROWS, :]


def _rows_at(buf, sh, off, rows):
    a, b = divmod(off, 8)
    return buf[8 * a:8 * a + rows, :] if b == 0 else sh[b - 1, 8 * a:8 * a + rows, :]


def _causal_taps(buf, sh, w_ref, rows):
    base = _HALO - (_CONV_WIDTH - 1)
    acc = jnp.zeros((rows, buf.shape[1]), _F32)
    for j in range(_CONV_WIDTH):
        acc = acc + w_ref[j:j + 1, :] * _rows_at(buf, sh, base + j, rows)
    return acc


def _conv_specs(s):
    tm = _CONV_TM
    per = tm // _HALO
    cw = _A_WIDTH
    cur = lambda col: pl.BlockSpec((tm, cw), lambda i: (i, col))
    prev = lambda col: pl.BlockSpec((_HALO, cw), lambda i: (jnp.maximum(i * per - 1, 0), col))
    return tm, per, cw, cur, prev


def _full2(arr):
    return pl.BlockSpec(arr.shape, lambda i: (0, 0))


def _conv_fwd(proj, dw_w, dw_b, ln_g, ln_b, y):
    s = proj.shape[0]
    tm, per, cw, cur, prev = _conv_specs(s)

    def body(ua_ref, ub_ref, uah_ref, ubh_ref, w_ref, b_ref, g_ref, be_ref, y_in, o_ref, buf, sh):
        del y_in
        _glu_into(buf, ua_ref, ub_ref, uah_ref, ubh_ref, pl.program_id(0) == 0)
        _shift_copies(sh, buf)
        pre = _causal_taps(buf, sh, w_ref, tm) + b_ref[...]
        o_ref[...] = _ln_silu(pre, g_ref[...], be_ref[...]).astype(_BF)

    return pl.pallas_call(
        body, name="conv_fwd", grid=(s // tm,),
        in_specs=[cur(3), cur(4), prev(3), prev(4), _full2(dw_w), _full2(dw_b), _full2(ln_g), _full2(ln_b), _ANY],
        out_specs=pl.BlockSpec((tm, cw), lambda i: (i, 1)), out_shape=jax.ShapeDtypeStruct(y.shape, y.dtype),
        scratch_shapes=[pltpu.VMEM((tm + _HALO, cw), _F32), pltpu.VMEM((7, _SHIFT_ROWS, cw), _F32)],
        input_output_aliases={8: 0}, compiler_params=_params(("arbitrary",)),
    )(proj, proj, proj, proj, dw_w, dw_b, ln_g, ln_b, y)


def _conv_bwd_pre(proj, dw_w, dw_b, ln_g, ln_b, dy):
    s = proj.shape[0]
    tm, per, cw, cur, prev = _conv_specs(s)

    def body(ua_ref, ub_ref, uah_ref, ubh_ref, w_ref, b_ref, g_ref, be_ref, dy_ref, dp_ref, dg_ref, dbe_ref, db_ref, buf, sh):
        i = pl.program_id(0)
        _glu_into(buf, ua_ref, ub_ref, uah_ref, ubh_ref, i == 0)
        _shift_copies(sh, buf)
        pre = _causal_taps(buf, sh, w_ref, tm) + b_ref[...]
        _, pull = jax.vjp(_ln_silu, pre, g_ref[...], be_ref[...])
        dpre, dg, dbe = pull(dy_ref[...])
        dp_ref[...] = dpre

        @pl.when(i == 0)
        def _():
            dg_ref[...] = jnp.zeros_like(dg_ref)
            dbe_ref[...] = jnp.zeros_like(dbe_ref)
            db_ref[...] = jnp.zeros_like(db_ref)

        dg_ref[...] += dg
        dbe_ref[...] += dbe
        db_ref[...] += jnp.sum(dpre, axis=0, keepdims=True)

    vec = pl.BlockSpec((1, cw), lambda i: (0, 0))
    vsd = jax.ShapeDtypeStruct((1, cw), _F32)
    return pl.pallas_call(
        body, name="conv_bwd_pre", grid=(s // tm,),
        in_specs=[cur(3), cur(4), prev(3), prev(4), _full2(dw_w), _full2(dw_b), _full2(ln_g), _full2(ln_b), cur(1)],
        out_specs=[pl.BlockSpec((tm, cw), lambda i: (i, 0)), vec, vec, vec],
        out_shape=[jax.ShapeDtypeStruct((s, cw), _F32), vsd, vsd, vsd],
        scratch_shapes=[pltpu.VMEM((tm + _HALO, cw), _F32), pltpu.VMEM((7, _SHIFT_ROWS, cw), _F32)],
        compiler_params=_params(("arbitrary",)),
    )(proj, proj, proj, proj, dw_w, dw_b, ln_g, ln_b, dy)


def _conv_bwd_taps(proj, dw_w, dpre):
    s = proj.shape[0]
    tm, per, cw, cur, prev = _conv_specs(s)
    nt = s // tm

    def body(ua_ref, ub_ref, uah_ref, ubh_ref, w_ref, dp_ref, dpn_ref, dglu_ref, dw_ref, buf, dbuf, sh, dsh):
        i = pl.program_id(0)
        _glu_into(buf, ua_ref, ub_ref, uah_ref, ubh_ref, i == 0)
        dpre = dp_ref[...]
        dbuf[0:tm, :] = dpre
        dbuf[tm:, :] = jnp.where(i == nt - 1, 0.0, dpn_ref[...])
        _shift_copies(sh, buf)
        _shift_copies(dsh, dbuf)
        acc = jnp.zeros((tm, cw), _F32)
        for j in range(_CONV_WIDTH):
            acc = acc + w_ref[j:j + 1, :] * _rows_at(dbuf, dsh, _CONV_WIDTH - 1 - j, tm)
        dglu_ref[...] = acc

        @pl.when(i == 0)
        def _():
            dw_ref[...] = jnp.zeros_like(dw_ref)

        base = _HALO - (_CONV_WIDTH - 1)
        for j in range(_CONV_WIDTH):
            dw_ref[j:j + 1, :] += jnp.sum(dpre * _rows_at(buf, sh, base + j, tm), axis=0, keepdims=True)

    nxt = pl.BlockSpec((_HALO, cw), lambda i: (jnp.minimum((i + 1) * per, s // _HALO - 1), 0))
    return pl.pallas_call(
        body, name="conv_bwd_taps", grid=(nt,),
        in_specs=[cur(3), cur(4), prev(3), prev(4), _full2(dw_w), pl.BlockSpec((tm, cw), lambda i: (i, 0)), nxt],
        out_specs=[pl.BlockSpec((tm, cw), lambda i: (i, 0)), pl.BlockSpec((_HALO, cw), lambda i: (0, 0))],
        out_shape=[jax.ShapeDtypeStruct((s, cw), _F32), jax.ShapeDtypeStruct((_HALO, cw), _F32)],
        scratch_shapes=[pltpu.VMEM((tm + _HALO, cw), _F32), pltpu.VMEM((tm + _HALO, cw), _F32),
                        pltpu.VMEM((7, _SHIFT_ROWS, cw), _F32), pltpu.VMEM((7, _SHIFT_ROWS, cw), _F32)],
        compiler_params=_params(("arbitrary",)),
    )(proj, proj, proj, proj, dw_w, dpre, dpre)


def _hyb_dproj(dqkv, tab_c, tab_s, dglu, proj):
    s = proj.shape[0]
    w = _A_WIDTH

    def body(d1_ref, d2_ref, d3_ref, c_ref, s_ref, dg_ref, ua_ref, ub_ref, o_ref):
        j = pl.program_id(1)

        @pl.when(j < 3)
        def _():
            x = d1_ref[...].astype(_F32) + d2_ref[...].astype(_F32) + d3_ref[...].astype(_F32)
            o_ref[...] = _rot_apply(x, c_ref[...], s_ref[...], True).astype(_BF)

        @pl.when(j == 3)
        def _():
            o_ref[...] = (dg_ref[...] * jax.nn.sigmoid(ub_ref[...])).astype(_BF)

        @pl.when(j == 4)
        def _():
            sg = jax.nn.sigmoid(ub_ref[...])
            o_ref[...] = (dg_ref[...] * ua_ref[...] * sg * (1.0 - sg)).astype(_BF)

    dq = pl.BlockSpec((_TM, w), lambda i, j: (i, jnp.minimum(j, 2)))
    tab = pl.BlockSpec((None, _TM, 128), lambda i, j: (jnp.minimum(j, 2), i, 0))
    col = lambda c: pl.BlockSpec((_TM, w), lambda i, j: (i, c))
    return pl.pallas_call(
        body, name="hyb_dproj", grid=(s // _TM, 5),
        in_specs=[dq, dq, dq, tab, tab, col(0), col(3), col(4)],
        out_specs=pl.BlockSpec((_TM, w), lambda i, j: (i, j)),
        out_shape=jax.ShapeDtypeStruct((s, 5 * w), _BF), compiler_params=_params(("arbitrary", "arbitrary")),
    )(*dqkv, tab_c, tab_s, dglu, proj, proj)


def _proj_in(name, xn, w, layer):
    s, d = xn.shape
    ns = w.shape[-1]
    return _mm(name, xn, w, grid=(4, s // _TM), a_spec=pl.BlockSpec((_TM, d), lambda j, i: (i, 0)),
               b_spec=pl.BlockSpec((None, None, d, ns), lambda j, i: (layer, j, 0, 0)), dn=_NN,
               out_shape=jax.ShapeDtypeStruct((s, 4 * ns), _F32), out_spec=pl.BlockSpec((_TM, ns), lambda j, i: (i, j)))


def _proj_in_dx(name, dz, w, layer, h, gamma, dh_out):
    s = dz.shape[0]
    d, ns = w.shape[2], w.shape[3]
    return _mm(name, dz, w, grid=(s // _TM, 4), a_spec=pl.BlockSpec((_TM, ns), lambda i, j: (i, j)),
               b_spec=pl.BlockSpec((None, None, d, ns), lambda i, j: (layer, j, 0, 0)), dn=_NT, red=1,
               acc_shape=(_TM, d), out_shape=jax.ShapeDtypeStruct((s, d), _F32),
               out_spec=pl.BlockSpec((_TM, d), lambda i, j: (i, 0)), epi=_norm_bwd_epi,
               **_norm_bwd_args(h, dh_out, gamma, _TM))


def _proj_in_dw(name, xn, dz):
    s, d = xn.shape
    ns = dz.shape[1] // 4
    return _mm(name, xn, dz, grid=(4, s // _TM), a_spec=pl.BlockSpec((_TM, d), lambda j, i: (i, 0)),
               b_spec=pl.BlockSpec((_TM, ns), lambda j, i: (i, j)), dn=_TN, red=1, acc_shape=(d, ns),
               out_shape=jax.ShapeDtypeStruct((4, d, ns), _BF),
               out_spec=pl.BlockSpec((None, d, ns), lambda j, i: (j, 0, 0)))


def _proj_out_dy(name, dh, w, layer):
    s, d = dh.shape
    k = w.shape[1]
    return _mm(name, dh, w, grid=(s // _TM,), a_spec=pl.BlockSpec((_TM, d), lambda i: (i, 0)),
               b_spec=pl.BlockSpec((None, k, d), lambda i: (layer, 0, 0)), dn=_NT,
               out_shape=jax.ShapeDtypeStruct((s, k), _F32), out_spec=pl.BlockSpec((_TM, k), lambda i: (i, 0)))


def _hyb_fwd(h, gamma, w_in, w_out, dw_w, dw_b, ln_g, ln_b, tabs, li):
    hn = _rmsnorm_fwd(h, gamma)
    proj = _proj_in("hyb_in", hn, w_in, li)
    qkv = _qkv_prep(proj, *tabs)
    branches = [_attn_fwd(qkv, dil) for dil in _DILATIONS]
    outs, lses = [b[0] for b in branches], [b[1] for b in branches]
    y = _attn_mix_fwd(outs, lses)
    y = _conv_fwd(proj, dw_w, dw_b, ln_g, ln_b, y)
    h_new = _proj_residual("hyb_out", y, w_out, li, h, 1.0)
    return h_new, (h, hn, proj, qkv, outs, lses, y)


def _hyb_bwd(dh, saved, gamma, w_in, w_out, dw_w, dw_b, ln_g, ln_b, tabs, li):
    h, hn, proj, qkv, outs, lses, y = saved
    dy = _proj_out_dy("hyb_dy", dh, w_out, li)
    d_w_out = _wgrad_rows("hyb_dwout", y, dh, 1.0)
    mixg = _attn_mix_bwd(outs, lses, dy)
    dqkv = [_attn_bwd(qkv, outs[b], lses[b], mixg[b], mixg[3 + b], dil) for b, dil in enumerate(_DILATIONS)]
    dpre, d_ln_g, d_ln_b, d_dw_b = _conv_bwd_pre(proj, dw_w, dw_b, ln_g, ln_b, dy)
    dglu, d_dw_w = _conv_bwd_taps(proj, dw_w, dpre)
    dproj = _hyb_dproj(dqkv, *tabs, dglu, proj)
    dh_in, dgamma = _proj_in_dx("hyb_dhn", dproj, w_in, li, h, gamma, dh)
    d_w_in = _proj_in_dw("hyb_dwin", hn, dproj)
    return dh_in, dgamma, (d_dw_w, d_dw_b, d_ln_g, d_ln_b), d_w_in, d_w_out


_GDN_BLOCK = 128
_GDN_HEADS_PER_STEP = 8


def _gdn_chunk(qs, ks, vs, gates, states, heads):
    n = len(qs)
    hs = range(n)
    c = qs[0].shape[0]
    lane = lax.broadcasted_iota(jnp.int32, gates.shape, 1)
    row = lax.broadcasted_iota(jnp.int32, (c, c), 0)
    col = lax.broadcasted_iota(jnp.int32, (c, c), 1)
    rowc = lax.broadcasted_iota(jnp.int32, (c, 1), 0)
    causal = row >= col
    eye = jnp.where(row == col, 1.0, 0.0)
    beta = [jnp.sum(jnp.where(lane == heads[h], gates, 0.0), axis=1, keepdims=True) for h in hs]
    g = [jnp.sum(jnp.where(lane == heads[h] + _GDN_HEADS, gates, 0.0), axis=1, keepdims=True) for h in hs]
    gc_row = [jnp.sum(jnp.where(row <= col, g[h], 0.0), axis=0, keepdims=True) for h in hs]
    gc_col = [jnp.sum(jnp.where(row == col, gc_row[h], 0.0), axis=1, keepdims=True) for h in hs]
    decay = [jnp.where(causal, jnp.exp(jnp.where(causal, gc_col[h] - gc_row[h], 0.0)), 0.0) for h in hs]
    kb = [ks[h] * beta[h] for h in hs]
    l_mat = [jnp.where(row > col, _dot_nt(kb[h], ks[h]) * decay[h], 0.0) for h in hs]
    t_inv = [eye - l_mat[h] for h in hs]
    x = [_dot(l_mat[h], l_mat[h]) for h in hs]
    for it in range(5):
        t_inv = [t_inv[h] + _dot(t_inv[h], x[h]) for h in hs]
        if it < 4:
            x = [_dot(x[h], x[h]) for h in hs]
    eg = [jnp.exp(gc_col[h]) for h in hs]
    u = [_dot(t_inv[h], vs[h] * beta[h]) for h in hs]
    w = [_dot(t_inv[h], kb[h] * eg[h]) for h in hs]
    attn = [jnp.where(causal, _dot_nt(qs[h], ks[h]) * decay[h], 0.0) for h in hs]
    g_last = [jnp.sum(jnp.where(rowc == c - 1, gc_col[h], 0.0), axis=0, keepdims=True) for h in hs]
    k_dec = [ks[h] * jnp.exp(g_last[h] - gc_col[h]) for h in hs]
    v_new = [u[h] - _dot(w[h], states[h]) for h in hs]
    o = [_dot(qs[h] * eg[h], states[h]) + _dot(attn[h], v_new[h]) for h in hs]
    new_states = [states[h] * jnp.exp(g_last[h]) + _dot_tn(k_dec[h], v_new[h]) for h in hs]
    return o, new_states


def _gdn_chunk_fwd(q, k, v, gates):
    s = q.shape[0]
    cb, c = _GDN_BLOCK, _GDN_CHUNK
    nblk, per = s // cb, cb // c

    hps = _GDN_HEADS_PER_STEP

    def body(q_ref, k_ref, v_ref, g_ref, o_ref, st_ref, state):
        @pl.when(pl.program_id(1) == 0)
        def _():
            state[...] = jnp.zeros_like(state)

        heads = [pl.program_id(0) * hps + hh for hh in range(hps)]
        cols = [slice(hh * _GDN_DIM, (hh + 1) * _GDN_DIM) for hh in range(hps)]
        for ci in range(per):
            rows = slice(ci * c, (ci + 1) * c)
            sts = [state[hh] for hh in range(hps)]
            for hh in range(hps):
                st_ref[hh, ci] = sts[hh]
            outs, new = _gdn_chunk([q_ref[rows, cl] for cl in cols], [k_ref[rows, cl] for cl in cols],
                                   [v_ref[rows, cl] for cl in cols], g_ref[rows, :], sts, heads)
            for hh in range(hps):
                o_ref[rows, cols[hh]] = outs[hh]
                state[hh] = new[hh]

    slab = pl.BlockSpec((cb, hps * _GDN_DIM), lambda h, i: (i, h))
    return pl.pallas_call(
        body, name="gdn_chunk_fwd", grid=(_GDN_HEADS // hps, nblk),
        in_specs=[slab, slab, slab, pl.BlockSpec((cb, 128), lambda h, i: (i, 0))],
        out_specs=[slab, pl.BlockSpec((hps, per, _GDN_DIM, _GDN_DIM), lambda h, i: (h, i, 0, 0))],
        out_shape=[jax.ShapeDtypeStruct((s, _GDN_HEADS * _GDN_DIM), _F32),
                   jax.ShapeDtypeStruct((_GDN_HEADS, s // c, _GDN_DIM, _GDN_DIM), _F32)],
        scratch_shapes=[pltpu.VMEM((hps, _GDN_DIM, _GDN_DIM), _F32)],
        compiler_params=_params(("arbitrary", "arbitrary")),
    )(q, k, v, gates)


def _gdn_chunk_bwd(q, k, v, gates, states, do):
    s = q.shape[0]
    cb, c = _GDN_BLOCK, _GDN_CHUNK
    nblk, per = s // cb, cb // c

    hps = _GDN_HEADS_PER_STEP

    def body(q_ref, k_ref, v_ref, g_ref, st_ref, do_ref, dq_ref, dk_ref, dv_ref, dg_ref, dstate):
        @pl.when(pl.program_id(1) == 0)
        def _():
            dstate[...] = jnp.zeros_like(dstate)

        heads = [pl.program_id(0) * hps + hh for hh in range(hps)]
        cols = [slice(hh * _GDN_DIM, (hh + 1) * _GDN_DIM) for hh in range(hps)]
        for ci in reversed(range(per)):
            rows = slice(ci * c, (ci + 1) * c)
            fn = functools.partial(_gdn_chunk, heads=heads)
            _, pull = jax.vjp(fn, [q_ref[rows, cl] for cl in cols], [k_ref[rows, cl] for cl in cols],
                              [v_ref[rows, cl] for cl in cols], g_ref[rows, :], [st_ref[hh, ci] for hh in range(hps)])
            dq, dk, dv, dg, dst = pull(([do_ref[rows, cl] for cl in cols], [dstate[hh] for hh in range(hps)]))
            dg_ref[rows, :] = dg
            for hh in range(hps):
                dq_ref[rows, cols[hh]] = dq[hh]
                dk_ref[rows, cols[hh]] = dk[hh]
                dv_ref[rows, cols[hh]] = dv[hh]
                dstate[hh] = dst[hh]

    slab = pl.BlockSpec((cb, hps * _GDN_DIM), lambda h, i: (nblk - 1 - i, h))
    sds = jax.ShapeDtypeStruct((s, _GDN_HEADS * _GDN_DIM), _F32)
    return pl.pallas_call(
        body, name="gdn_chunk_bwd", grid=(_GDN_HEADS // hps, nblk),
        in_specs=[slab, slab, slab, pl.BlockSpec((cb, 128), lambda h, i: (nblk - 1 - i, 0)),
                  pl.BlockSpec((hps, per, _GDN_DIM, _GDN_DIM), lambda h, i: (h, nblk - 1 - i, 0, 0)), slab],
        out_specs=[slab, slab, slab, pl.BlockSpec((None, cb, 128), lambda h, i: (h, nblk - 1 - i, 0))],
        out_shape=[sds, sds, sds, jax.ShapeDtypeStruct((_GDN_HEADS // hps, s, 128), _F32)],
        scratch_shapes=[pltpu.VMEM((hps, _GDN_DIM, _GDN_DIM), _F32)],
        compiler_params=_params(("arbitrary", "arbitrary")),
    )(q, k, v, gates, states, do)


_SHALO = 8


def _swish_l2(c, normalize, scale):
    y = _silu(c)
    if normalize:
        y = y * lax.rsqrt(jnp.sum(y * y, axis=-1, keepdims=True) + _NORM_EPS) * scale
    return y


def _short_taps(buf, w_ref, rows):
    base = _SHALO - (_GDN_SHORT_CONV - 1)
    acc = jnp.zeros((rows, buf.shape[1]), _F32)
    for t in range(_GDN_SHORT_CONV):
        acc = acc + w_ref[t:t + 1, :] * buf[base + t:base + t + rows, :]
    return acc


_GDN_W = _GDN_HEADS * _GDN_DIM


def _gdn_conv_fwd(proj, conv_w, grp, normalize, scale):
    s = proj.shape[0]
    tm = _CONV_TM
    per = tm // _SHALO

    def body(x_ref, xp_ref, w_ref, o_ref, buf):
        buf[0:_SHALO, :] = jnp.where(pl.program_id(0) == 0, 0.0, xp_ref[...])
        buf[_SHALO:, :] = x_ref[...]
        conv = _short_taps(buf, w_ref, tm)
        for h in range(_GDN_HEADS):
            cols = slice(h * _GDN_DIM, (h + 1) * _GDN_DIM)
            o_ref[:, cols] = _swish_l2(conv[:, cols], normalize, scale)

    return pl.pallas_call(
        body, name=f"gdn_conv_fwd_{grp}", grid=(s // tm,),
        in_specs=[pl.BlockSpec((tm, _GDN_W), lambda i: (i, grp)),
                  pl.BlockSpec((_SHALO, _GDN_W), lambda i: (jnp.maximum(i * per - 1, 0), grp)),
                  pl.BlockSpec((_GDN_SHORT_CONV, _GDN_W), lambda i: (0, grp))],
        out_specs=pl.BlockSpec((tm, _GDN_W), lambda i: (i, 0)),
        out_shape=jax.ShapeDtypeStruct((s, _GDN_W), _F32),
        scratch_shapes=[pltpu.VMEM((tm + _SHALO, _GDN_W), _F32)], compiler_params=_params(("arbitrary",)),
    )(proj, proj, conv_w)


def _gdn_conv_bwd(proj, conv_w, grp, normalize, scale, dy, dproj):
    s = proj.shape[0]
    tm = _CONV_TM
    per = tm // _SHALO
    nt = s // tm
    taps = _GDN_SHORT_CONV

    def body(x_ref, xp_ref, xn_ref, w_ref, dy_ref, dyn_ref, dp_in, dx_ref, dw_ref, buf, dbuf):
        del dp_in
        i = pl.program_id(0)
        last = i == nt - 1
        buf[0:_SHALO, :] = jnp.where(i == 0, 0.0, xp_ref[...])
        buf[_SHALO:_SHALO + tm, :] = x_ref[...]
        buf[_SHALO + tm:, :] = jnp.where(last, 0.0, xn_ref[...])
        conv = _short_taps(buf, w_ref, tm + _SHALO)
        dy_ext = jnp.concatenate([dy_ref[...], jnp.where(last, 0.0, dyn_ref[...])], axis=0)
        for h in range(_GDN_HEADS):
            cols = slice(h * _GDN_DIM, (h + 1) * _GDN_DIM)
            _, pull = jax.vjp(lambda c: _swish_l2(c, normalize, scale), conv[:, cols])
            dbuf[:, cols] = pull(dy_ext[:, cols])[0]
        acc = jnp.zeros((tm, _GDN_W), _F32)
        for t in range(taps):
            o = taps - 1 - t
            acc = acc + w_ref[t:t + 1, :] * dbuf[o:o + tm, :]
        dx_ref[...] = acc.astype(dx_ref.dtype)

        @pl.when(i == 0)
        def _():
            dw_ref[...] = jnp.zeros_like(dw_ref)

        base = _SHALO - (taps - 1)
        for t in range(taps):
            dw_ref[t:t + 1, :] += jnp.sum(dbuf[0:tm, :] * buf[base + t:base + t + tm, :], axis=0, keepdims=True)

    cur = lambda col: pl.BlockSpec((tm, _GDN_W), lambda i: (i, col))
    nxt = lambda col: pl.BlockSpec((_SHALO, _GDN_W), lambda i: (jnp.minimum((i + 1) * per, s // _SHALO - 1), col))
    return pl.pallas_call(
        body, name=f"gdn_conv_bwd_{grp}", grid=(nt,),
        in_specs=[cur(grp), pl.BlockSpec((_SHALO, _GDN_W), lambda i: (jnp.maximum(i * per - 1, 0), grp)), nxt(grp),
                  pl.BlockSpec((taps, _GDN_W), lambda i: (0, grp)), cur(0), nxt(0), _ANY],
        out_specs=[cur(grp), pl.BlockSpec((_SHALO, _GDN_W), lambda i: (0, 0))],
        out_shape=[jax.ShapeDtypeStruct(dproj.shape, dproj.dtype), jax.ShapeDtypeStruct((_SHALO, _GDN_W), _F32)],
        scratch_shapes=[pltpu.VMEM((tm + 2 * _SHALO, _GDN_W), _F32), pltpu.VMEM((tm + _SHALO, _GDN_W), _F32)],
        input_output_aliases={6: 0}, compiler_params=_params(("arbitrary",)),
    )(proj, proj, proj, conv_w, dy, dy, dproj)


def _softplus(z):
    return jnp.maximum(z, 0.0) + jnp.log(1.0 + jnp.exp(-jnp.abs(z)))


def _gates(ba, a_lane, dt_lane):
    lane = lax.broadcasted_iota(jnp.int32, ba.shape, 1)
    g = -jnp.exp(a_lane) * _softplus(ba + dt_lane)
    return jnp.where(lane < _GDN_HEADS, jax.nn.sigmoid(ba), jnp.where(lane < 2 * _GDN_HEADS, g, 0.0))


def _gates_fwd(ba, a_lane, dt_lane):
    return _rowmap("gdn_gates_fwd", _gates, [(ba, 128, 0)], [a_lane, dt_lane], [(128, 128, 0, _F32)],
                   rows=ba.shape[0], tm=_TM)[0]


def _gates_bwd(ba, a_lane, dt_lane, dgates):
    def body(bav, dg8, av, dv):
        _, pull = jax.vjp(_gates, bav, av, dv)
        return pull(jnp.sum(dg8, axis=0))

    return _rowmap("gdn_gates_bwd", body, [(ba, 128, 0), (dgates, 128, 0)], [a_lane, dt_lane], [(128, 128, 0, _BF)],
                   [(1, 128), (1, 128)], rows=ba.shape[0], tm=_TM)


def _out_gate(o, z, g):
    return (_rms(o, g) * _silu(z)).astype(_BF)


def _head_cols():
    return [slice(h * _GDN_DIM, (h + 1) * _GDN_DIM) for h in range(_GDN_HEADS)]


def _out_gate_fwd(o, proj, norm_g):
    def body(ov, zv, gv):
        return jnp.concatenate([_out_gate(ov[:, cl], zv[:, cl], gv) for cl in _head_cols()], axis=1)

    return _rowmap("gdn_out_gate_fwd", body, [(o, _GDN_W, 0), (proj, _GDN_W, 3)], [norm_g], [(_GDN_W, _GDN_W, 0, _BF)],
                   rows=o.shape[0], tm=_CONV_TM)[0]


def _out_gate_bwd(o, proj, norm_g, dy):
    def body(ov, zv, dyv, gv):
        d_o, d_z, d_g = [], [], jnp.zeros_like(gv)
        for cl in _head_cols():
            _, pull = jax.vjp(_out_gate, ov[:, cl], zv[:, cl], gv)
            g_o, g_z, g_g = pull(dyv[:, cl].astype(_BF))
            d_o.append(g_o)
            d_z.append(g_z)
            d_g = d_g + g_g
        return jnp.concatenate(d_o, axis=1), jnp.concatenate(d_z, axis=1), d_g

    return _rowmap("gdn_out_gate_bwd", body, [(o, _GDN_W, 0), (proj, _GDN_W, 3), (dy, _GDN_W, 0)], [norm_g],
                   [(_GDN_W, _GDN_W, 0, _F32), (proj.shape[1], _GDN_W, 3, _BF)], [(1, _GDN_DIM)],
                   rows=o.shape[0], tm=_CONV_TM)


def _plain_in(name, xn, w, layer, tn):
    s, d = xn.shape
    n = w.shape[2]
    return _mm(name, xn, w, grid=(n // tn, s // _TM), a_spec=pl.BlockSpec((_TM, d), lambda j, i: (i, 0)),
               b_spec=pl.BlockSpec((None, d, tn), lambda j, i: (layer, 0, j)), dn=_NN,
               out_shape=jax.ShapeDtypeStruct((s, n), _F32), out_spec=pl.BlockSpec((_TM, tn), lambda j, i: (i, j)))


def _plain_dx(name, dz, w, layer, tn, norm=None):
    s, n = dz.shape
    d = w.shape[1]
    row = pl.BlockSpec((_TM, d), lambda i, j: (i, 0))
    kw = {}
    if norm is not None:
        extra, h, gamma, dh_out = norm
        kw = _norm_bwd_args(h, dh_out, gamma, _TM)
        kw.update(extras=(extra, *kw["extras"]), extra_specs=(row, *kw["extra_specs"]),
                  epi=lambda acc, e, hv, dhv, gv: _norm_bwd_epi(acc + e, hv, dhv, gv))
    return _mm(name, dz, w, grid=(s // _TM, n // tn), a_spec=pl.BlockSpec((_TM, tn), lambda i, j: (i, j)),
               b_spec=pl.BlockSpec((None, d, tn), lambda i, j: (layer, 0, j)), dn=_NT, red=1, acc_shape=(_TM, d),
               out_shape=jax.ShapeDtypeStruct((s, d), _F32), out_spec=row, **kw)


def _plain_dw(name, xn, dz, tn):
    s, d = xn.shape
    n = dz.shape[1]
    return _mm(name, xn, dz, grid=(n // tn, s // _TM), a_spec=pl.BlockSpec((_TM, d), lambda j, i: (i, 0)),
               b_spec=pl.BlockSpec((_TM, tn), lambda j, i: (i, j)), dn=_TN, red=1, acc_shape=(d, tn),
               out_shape=jax.ShapeDtypeStruct((d, n), _BF), out_spec=pl.BlockSpec((d, tn), lambda j, i: (0, j)))


_GDN_QK_SCALE = _GDN_DIM ** -0.5


def _gdn_fwd(h, gamma, w_qkvz, w_ba, w_out, conv_w, a_lane, dt_lane, norm_g, li):
    hn = _rmsnorm_fwd(h, gamma)
    proj = _plain_in("gdn_in", hn, w_qkvz, li, 1024)
    ba = _plain_in("gdn_in_ba", hn, w_ba, li, 128)
    q = _gdn_conv_fwd(proj, conv_w, 0, True, _GDN_QK_SCALE)
    k = _gdn_conv_fwd(proj, conv_w, 1, True, 1.0)
    v = _gdn_conv_fwd(proj, conv_w, 2, False, 1.0)
    gates = _gates_fwd(ba, a_lane, dt_lane)
    o, states = _gdn_chunk_fwd(q, k, v, gates)
    y = _out_gate_fwd(o, proj, norm_g)
    h_new = _proj_residual("gdn_out", y, w_out, li, h, 1.0)
    return h_new, (h, hn, proj, ba, q, k, v, gates, states, o, y)


def _gdn_bwd(dh, saved, gamma, w_qkvz, w_ba, w_out, conv_w, a_lane, dt_lane, norm_g, li, n_ba):
    h, hn, proj, ba, q, k, v, gates, states, o, y = saved
    dy = _proj_out_dy("gdn_dy", dh, w_out, li)
    d_w_out = _wgrad_rows("gdn_dwout", y, dh, 1.0)
    do, dproj, d_norm_g = _out_gate_bwd(o, proj, norm_g, dy)
    dq, dk, dv, dgates = _gdn_chunk_bwd(q, k, v, gates, states, do)
    dproj, dcw_q = _gdn_conv_bwd(proj, conv_w, 0, True, _GDN_QK_SCALE, dq, dproj)
    dproj, dcw_k = _gdn_conv_bwd(proj, conv_w, 1, True, 1.0, dk, dproj)
    dproj, dcw_v = _gdn_conv_bwd(proj, conv_w, 2, False, 1.0, dv, dproj)
    dba, d_a, d_dt = _gates_bwd(ba, a_lane, dt_lane, dgates)
    dhn_ba = _plain_dx("gdn_dhn_ba", dba, w_ba, li, 128)
    dh_in, dgamma = _plain_dx("gdn_dhn", dproj, w_qkvz, li, 1024, norm=(dhn_ba, h, gamma, dh))
    dw_qkvz = _plain_dw("gdn_dwin", hn, dproj, 1024)
    dw_ba = _plain_dw("gdn_dwin_ba", hn, dba, 128)
    d_conv_w = jnp.concatenate([dcw_q, dcw_k, dcw_v], axis=1)[:_GDN_SHORT_CONV]
    d_w_in = jnp.concatenate([dw_qkvz, dw_ba[:, :n_ba]], axis=1)
    d_w_in = jnp.transpose(d_w_in.reshape(d_w_in.shape[0], 4, -1), (1, 0, 2))
    return dh_in, dgamma, (d_conv_w, d_a, d_dt, d_norm_g), d_w_in, d_w_out


_MESH = pl.DeviceIdType.MESH
_ROW_TM = 384


def _place():
    x, y, c = lax.axis_index("x"), lax.axis_index("y"), lax.axis_index("c")
    chips = [(1 - x, y), (x, 1 - y), (1 - x, 1 - y)]
    return x, y, c, chips, [2 * px + py for px, py in chips]


def _remote(src, dst, send_sem, recv_sem, device):
    return pltpu.make_async_remote_copy(src_ref=src, dst_ref=dst, send_sem=send_sem, recv_sem=recv_sem,
                                        device_id=device, device_id_type=_MESH)


def _rows2d(t):
    return t.reshape(-1, t.shape[-1])


def _row_tile(rows):
    for tm in range(min(rows, _ROW_TM) // 8 * 8, 7, -8):
        if rows % tm == 0:
            return tm
    raise ValueError(f"no row tile for {rows} rows")


def _placed_map(name, fn, where, ins, out_rows, cols, out_dtype, out_row, steps, tm, into=None, after=None):
    n_in = len(ins)
    n_extra = (into is not None) + (after is not None)

    def kern(where_ref, *refs):
        del where_ref
        o_ref = refs[n_in + n_extra]
        o_ref[...] = fn(*[r[...] for r in refs[:n_in]]).astype(o_ref.dtype)

    in_specs, operands = [], [where]
    for arr, rowfn in ins:
        operands.append(arr)
        if arr.ndim == 3:
            in_specs.append(pl.BlockSpec((arr.shape[0], tm, cols), lambda i, wh, f=rowfn: (0, f(i, wh), 0)))
        else:
            in_specs.append(pl.BlockSpec((tm, cols), lambda i, wh, f=rowfn: (f(i, wh), 0)))
    aliases = {}
    if into is not None:
        operands.append(into)
        in_specs.append(_ANY)
        aliases = {n_in + 1: 0}
    if after is not None:
        operands.append(after)
        in_specs.append(_ANY)
    spec = pltpu.PrefetchScalarGridSpec(
        num_scalar_prefetch=1, grid=(steps,), in_specs=in_specs,
        out_specs=pl.BlockSpec((tm, cols), lambda i, wh: (out_row(i, wh), 0)))
    return pl.pallas_call(
        kern, name=name, grid_spec=spec, out_shape=jax.ShapeDtypeStruct((out_rows, cols), out_dtype),
        input_output_aliases=aliases, compiler_params=_params(("arbitrary",)),
    )(*operands)


def _cast_slab(name, t, where, after):
    n_l, r, cols = t.shape
    tm = _row_tile(r)
    per = r // tm
    buf = _placed_map("cast_" + name, lambda val: val, where, [(_rows2d(t), lambda i, wh: i)], n_l * 4 * r, cols, _BF,
                      lambda i, wh: ((i // per) * 4 + wh[1]) * per + i % per, n_l * per, tm, after=after)
    return buf.reshape(n_l, 4, r, cols)


_HBM = pl.BlockSpec(memory_space=pltpu.HBM)
_SEM = pl.BlockSpec(memory_space=pltpu.SEMAPHORE)
_EFFECT = pltpu.SideEffectType.DATAFLOW_SIDE_EFFECTING


def _core_layers(n_layers, c, part):
    half = n_layers // 2
    sub = half // part[1]
    return pl.ds(c * half + part[0] * sub, sub), pl.ds((1 - c) * half + part[0] * sub, sub)


def _ici_copies(refs, send, recv, arrivals, part):
    x, y, c, chips, idx = _place()
    me = 2 * x + y
    res = []
    for t, ref in enumerate(refs):
        mine, _ = _core_layers(ref.shape[0], c, part)
        for k in range(3):
            sem = 3 * t + k
            block = ref.at[mine, idx[k] if arrivals else me]
            res.append(_remote(block, block, send.at[sem], recv.at[sem], (*chips[k], c)))
    return res


def _gather_start(name, bufs, after, part):
    n = len(bufs)

    def body(*refs):
        send, recv = refs[n + 1], refs[n + 2]
        for cp in _ici_copies(refs[:n], send, recv, False, part):
            cp.start()
        refs[2 * n + 3][...] = jnp.zeros((8, 128), _F32)

    res = pl.pallas_call(
        body, name=name,
        out_shape=(pltpu.SemaphoreType.DMA((3 * n,)), pltpu.SemaphoreType.DMA((3 * n,)),
                   *[pltpu.HBM(b.shape, b.dtype) for b in bufs], jax.ShapeDtypeStruct((8, 128), _F32)),
        in_specs=[*[_HBM] * n, _ANY], out_specs=(_SEM, _SEM, *[_HBM] * n, pl.BlockSpec(memory_space=pltpu.VMEM)),
        input_output_aliases={t: 2 + t for t in range(n)},
        compiler_params=pltpu.CompilerParams(has_side_effects=_EFFECT),
    )(*[pltpu.with_memory_space_constraint(b, pltpu.HBM) for b in bufs], after)
    return res[0], res[1], list(res[2:2 + n]), res[2 + n]


def _gather_wait(name, send_sem, recv_sem, bufs, after, part):
    n = len(bufs)

    def body(*refs):
        send, recv = refs[n], refs[n + 1]
        for cp in _ici_copies(refs[:n], send, recv, False, part):
            cp.wait_send()
        for cp in _ici_copies(refs[:n], send, recv, True, part):
            cp.wait_recv()

    res = pl.pallas_call(
        body, name=name, out_shape=tuple(pltpu.HBM(b.shape, b.dtype) for b in bufs),
        in_specs=[*[_HBM] * n, _SEM, _SEM, _ANY], out_specs=tuple([_HBM] * n),
        input_output_aliases={t: t for t in range(n)},
        compiler_params=pltpu.CompilerParams(has_side_effects=_EFFECT),
    )(*bufs, send_sem, recv_sem, after)
    return list(res)


def _gather_forward(name, bufs, part):
    n = len(bufs)

    def body(*refs):
        outs, send, recv = refs[n:2 * n], refs[2 * n], refs[2 * n + 1]
        x, y, c, _, idx = _place()
        copies, arrive = [], []
        for t, ref in enumerate(outs):
            mine, other = _core_layers(ref.shape[0], c, part)
            for k in range(3):
                sem = 3 * t + k
                landed, dst = ref.at[mine, idx[k]], ref.at[other, idx[k]]
                copies.append(_remote(landed, landed, send.at[sem], recv.at[sem], (x, y, 1 - c)))
                arrive.append(_remote(dst, dst, send.at[sem], recv.at[sem], (x, y, 1 - c)))
        for cp in copies:
            cp.start()
        for cp in arrive:
            cp.wait_recv()
        for cp in copies:
            cp.wait_send()

    res = pl.pallas_call(
        body, name=name, in_specs=[_ANY] * n, out_specs=[_ANY] * n,
        out_shape=[jax.ShapeDtypeStruct(b.shape, b.dtype) for b in bufs],
        scratch_shapes=[pltpu.SemaphoreType.DMA((3 * n,)), pltpu.SemaphoreType.DMA((3 * n,))],
        input_output_aliases={t: t for t in range(n)},
    )(*bufs)
    return list(res)


def _rs_sibling(name, gs):
    n = len(gs)

    def body(*refs):
        send, recv = refs[2 * n], refs[2 * n + 1]
        x, y, c, _, _ = _place()
        copies = []
        for t in range(n):
            rh = refs[t].shape[1] // 2
            cp = _remote(refs[t].at[:, pl.ds((1 - c) * rh, rh)], refs[n + t], send.at[t], recv.at[t], (x, y, 1 - c))
            cp.start()
            copies.append(cp)
        for cp in copies:
            cp.wait()

    return pl.pallas_call(
        body, name=name, in_specs=[_ANY] * n, out_specs=[_ANY] * n,
        out_shape=[jax.ShapeDtypeStruct((4, g.shape[1] // 2, g.shape[2]), g.dtype) for g in gs],
        scratch_shapes=[pltpu.SemaphoreType.DMA((n,)), pltpu.SemaphoreType.DMA((n,))],
    )(*gs)


def _chip_copies(sums, lands, send, recv):
    x, y, c, chips, idx = _place()
    return [_remote(sums[t].at[idx[k]], lands[t].at[k], send.at[3 * t + k], recv.at[3 * t + k], (*chips[k], c))
            for t in range(len(sums)) for k in range(3)]


def _rs_chips_start(name, sums, after):
    n = len(sums)
    lands = [lax.empty((3,) + s.shape[1:], s.dtype) for s in sums]

    def body(*refs):
        send, recv = refs[2 * n + 1], refs[2 * n + 2]
        for cp in _chip_copies(refs[:n], refs[n:2 * n], send, recv):
            cp.start()
        refs[4 * n + 3][...] = jnp.zeros((8, 128), _F32)

    both = [*sums, *lands]
    res = pl.pallas_call(
        body, name=name,
        out_shape=(pltpu.SemaphoreType.DMA((3 * n,)), pltpu.SemaphoreType.DMA((3 * n,)),
                   *[pltpu.HBM(b.shape, b.dtype) for b in both], jax.ShapeDtypeStruct((8, 128), _F32)),
        in_specs=[*[_HBM] * (2 * n), _ANY],
        out_specs=(_SEM, _SEM, *[_HBM] * (2 * n), pl.BlockSpec(memory_space=pltpu.VMEM)),
        input_output_aliases={t: 2 + t for t in range(2 * n)},
        compiler_params=pltpu.CompilerParams(has_side_effects=_EFFECT),
    )(*[pltpu.with_memory_space_constraint(b, pltpu.HBM) for b in both], after)
    return res[0], res[1], list(res[2:2 + n]), list(res[2 + n:2 + 2 * n]), res[2 + 2 * n]


def _rs_chips_wait(name, send_sem, recv_sem, sums, lands, after):
    n = len(sums)

    def body(*refs):
        send, recv = refs[2 * n], refs[2 * n + 1]
        copies = _chip_copies(refs[:n], refs[n:2 * n], send, recv)
        for cp in copies:
            cp.wait_send()
        for cp in copies:
            cp.wait_recv()

    both = [*sums, *lands]
    res = pl.pallas_call(
        body, name=name, out_shape=tuple(pltpu.HBM(b.shape, b.dtype) for b in both),
        in_specs=[*[_HBM] * (2 * n), _SEM, _SEM, *[_ANY] * len(after)], out_specs=tuple([_HBM] * (2 * n)),
        input_output_aliases={t: t for t in range(2 * n)},
        compiler_params=pltpu.CompilerParams(has_side_effects=_EFFECT),
    )(*both, send_sem, recv_sem, *after)
    return list(res[:n]), list(res[n:])


def _after(x, token):
    def body(x_ref, t_ref, o_ref):
        del x_ref, t_ref, o_ref

    return pl.pallas_call(
        body, name="after", in_specs=[_ANY, _ANY], out_specs=_ANY, out_shape=jax.ShapeDtypeStruct(x.shape, x.dtype),
        input_output_aliases={0: 0},
    )(x, token)


def _rs_join(name, totals, layers):
    n = len(totals)

    def body(*refs):
        outs, send, recv = refs[n:2 * n], refs[2 * n], refs[2 * n + 1]
        x, y, c, _, _ = _place()
        copies, arrive = [], []
        for t in range(n):
            rh = outs[t].shape[1] // 2
            mine = outs[t].at[layers[t], pl.ds(c * rh, rh)]
            landing = outs[t].at[layers[t], pl.ds((1 - c) * rh, rh)]
            copies.append(_remote(mine, mine, send.at[t], recv.at[t], (x, y, 1 - c)))
            arrive.append(_remote(landing, landing, send.at[t], recv.at[t], (x, y, 1 - c)))
        for cp in copies:
            cp.start()
        for cp in arrive:
            cp.wait_recv()
        for cp in copies:
            cp.wait_send()

    return pl.pallas_call(
        body, name=name, in_specs=[_ANY] * n, out_specs=[_ANY] * n,
        out_shape=[jax.ShapeDtypeStruct(b.shape, b.dtype) for b in totals],
        scratch_shapes=[pltpu.SemaphoreType.DMA((n,)), pltpu.SemaphoreType.DMA((n,))],
        input_output_aliases={t: t for t in range(n)},
    )(*totals)


def _pair_sum(g, got, where):
    _, r, cols = g.shape
    rh = r // 2
    tm = _row_tile(rh)
    per, perh = r // tm, rh // tm
    out = _placed_map("rs_add_pair", lambda u, w: u.astype(_F32) + w.astype(_F32), where,
                      [(_rows2d(g), lambda i, wh: (i // perh) * per + wh[0] * perh + i % perh),
                       (_rows2d(got), lambda i, wh: i)],
                      4 * rh, cols, _BF, lambda i, wh: i, 4 * perh, tm)
    return out.reshape(4, rh, cols)


def _sum_of_four(pair, land, where, total, layer, n_layers):
    _, rh, cols = pair.shape
    r = 2 * rh
    tm = _row_tile(rh)
    per, perh = r // tm, rh // tm

    def add_four(own, others):
        return ((own.astype(_F32) + others[0].astype(_F32)) + others[1].astype(_F32)) + others[2].astype(_F32)

    out = _placed_map("rs_add_four", add_four, where,
                      [(_rows2d(pair), lambda i, wh: wh[1] * perh + i), (land, lambda i, wh: i)],
                      n_layers * r, cols, _F32, lambda i, wh: layer * per + wh[0] * perh + i, perh, tm,
                      into=None if total is None else _rows2d(total))
    return out.reshape(n_layers, r, cols)


def _all_reduce_small(pack):
    rows, cols = pack.shape

    def body(p_ref, o_ref, land, send, recv):
        x, y, c = lax.axis_index("x"), lax.axis_index("y"), lax.axis_index("c")
        me = 4 * x + 2 * y + c
        land[me] = p_ref[...]
        copies = []
        for k in range(1, 8):
            bx, by, bc = (k >> 2) & 1, (k >> 1) & 1, k & 1
            peer = (1 - x if bx else x, 1 - y if by else y, 1 - c if bc else c)
            cp = _remote(p_ref, land.at[me], send.at[k - 1], recv.at[k - 1], peer)
            cp.start()
            copies.append((cp, 4 * peer[0] + 2 * peer[1] + peer[2]))
        for k, (cp, pid) in enumerate(copies):
            _remote(p_ref, land.at[pid], send.at[k], recv.at[k], (x, y, c)).wait_recv()
        for cp, _ in copies:
            cp.wait_send()
        acc = land[0]
        for d in range(1, 8):
            acc = acc + land[d]
        o_ref[...] = acc

    vm = pl.BlockSpec(memory_space=pltpu.VMEM)
    return pl.pallas_call(
        body, name="all_reduce_small", in_specs=[vm], out_specs=vm,
        out_shape=jax.ShapeDtypeStruct((rows, cols), _F32),
        scratch_shapes=[pltpu.VMEM((8, rows, cols), _F32), pltpu.SemaphoreType.DMA((7,)), pltpu.SemaphoreType.DMA((7,))],
    )(pack)


def _adamw_math(w, g, m, v):
    m2 = _ADAM_B1 * m + (1.0 - _ADAM_B1) * g
    v2 = _ADAM_B2 * v + (1.0 - _ADAM_B2) * (g * g)
    m_hat = m2 / (1.0 - _ADAM_B1 ** _ADAM_STEP)
    v_hat = v2 / (1.0 - _ADAM_B2 ** _ADAM_STEP)
    delta = -_ADAM_LR * (m_hat / (jnp.sqrt(v_hat) + _ADAM_EPS) + _ADAM_WD * w)
    return delta, m2, v2


def _adamw(w, g, m, v, lo=0, hi=None, into=None):
    shape = w.shape
    n_l = shape[0] if w.ndim == 3 else 1
    hi = n_l if hi is None else hi
    w2, g2, m2, v2 = (_rows2d(t) for t in (w, g, m, v))
    rows, cols = w2.shape
    per_layer = rows // n_l
    tm = _row_tile(per_layer)
    first, steps = lo * per_layer // tm, (hi - lo) * per_layer // tm
    n_into = 0 if into is None else 3

    def body(*refs):
        outs = refs[4 + n_into:]
        for ref, val in zip(outs, _adamw_math(*[r[...] for r in refs[:4]])):
            ref[...] = val

    spec = pl.BlockSpec((tm, cols), lambda i: (first + i, 0))
    sds = jax.ShapeDtypeStruct((rows, cols), _F32)
    outs = pl.pallas_call(
        body, name="adamw", grid=(steps,), in_specs=[spec] * 4 + [_ANY] * n_into, out_specs=[spec] * 3,
        out_shape=[sds] * 3, input_output_aliases={4 + k: k for k in range(n_into)},
        compiler_params=_params(("arbitrary",)),
    )(w2, g2, m2, v2, *([] if into is None else [_rows2d(t) for t in into]))
    return tuple(o.reshape(shape) for o in outs)


def kernel(x, positions, ffn1_norm, ffn1_w_in, ffn1_w_out, mix_norm, ffn2_norm, ffn2_w_in, ffn2_w_out, hyb_w_in, hyb_dw_w, hyb_dw_b, hyb_ln_g, hyb_ln_b, hyb_w_out, gdn_w_in, gdn_conv_w, gdn_A_log, gdn_dt_bias, gdn_norm_g, gdn_w_out, final_norm, loss_target, m_ffn1_norm, m_ffn1_w_in, m_ffn1_w_out, m_mix_norm, m_ffn2_norm, m_ffn2_w_in, m_ffn2_w_out, m_hyb_w_in, m_hyb_dw_w, m_hyb_dw_b, m_hyb_ln_g, m_hyb_ln_b, m_hyb_w_out, m_gdn_w_in, m_gdn_conv_w, m_gdn_A_log, m_gdn_dt_bias, m_gdn_norm_g, m_gdn_w_out, m_final_norm, v_ffn1_norm, v_ffn1_w_in, v_ffn1_w_out, v_mix_norm, v_ffn2_norm, v_ffn2_w_in, v_ffn2_w_out, v_hyb_w_in, v_hyb_dw_w, v_hyb_dw_b, v_hyb_ln_g, v_hyb_ln_b, v_hyb_w_out, v_gdn_w_in, v_gdn_conv_w, v_gdn_A_log, v_gdn_dt_bias, v_gdn_norm_g, v_gdn_w_out, v_final_norm):
    w = dict(ffn1_norm=ffn1_norm, ffn1_w_in=ffn1_w_in, ffn1_w_out=ffn1_w_out, mix_norm=mix_norm, ffn2_norm=ffn2_norm, ffn2_w_in=ffn2_w_in, ffn2_w_out=ffn2_w_out, hyb_w_in=hyb_w_in, hyb_dw_w=hyb_dw_w, hyb_dw_b=hyb_dw_b, hyb_ln_g=hyb_ln_g, hyb_ln_b=hyb_ln_b, hyb_w_out=hyb_w_out, gdn_w_in=gdn_w_in, gdn_conv_w=gdn_conv_w, gdn_A_log=gdn_A_log, gdn_dt_bias=gdn_dt_bias, gdn_norm_g=gdn_norm_g, gdn_w_out=gdn_w_out, final_norm=final_norm)
    m = dict(ffn1_norm=m_ffn1_norm, ffn1_w_in=m_ffn1_w_in, ffn1_w_out=m_ffn1_w_out, mix_norm=m_mix_norm, ffn2_norm=m_ffn2_norm, ffn2_w_in=m_ffn2_w_in, ffn2_w_out=m_ffn2_w_out, hyb_w_in=m_hyb_w_in, hyb_dw_w=m_hyb_dw_w, hyb_dw_b=m_hyb_dw_b, hyb_ln_g=m_hyb_ln_g, hyb_ln_b=m_hyb_ln_b, hyb_w_out=m_hyb_w_out, gdn_w_in=m_gdn_w_in, gdn_conv_w=m_gdn_conv_w, gdn_A_log=m_gdn_A_log, gdn_dt_bias=m_gdn_dt_bias, gdn_norm_g=m_gdn_norm_g, gdn_w_out=m_gdn_w_out, final_norm=m_final_norm)
    v = dict(ffn1_norm=v_ffn1_norm, ffn1_w_in=v_ffn1_w_in, ffn1_w_out=v_ffn1_w_out, mix_norm=v_mix_norm, ffn2_norm=v_ffn2_norm, ffn2_w_in=v_ffn2_w_in, ffn2_w_out=v_ffn2_w_out, hyb_w_in=v_hyb_w_in, hyb_dw_w=v_hyb_dw_w, hyb_dw_b=v_hyb_dw_b, hyb_ln_g=v_hyb_ln_g, hyb_ln_b=v_hyb_ln_b, hyb_w_out=v_hyb_w_out, gdn_w_in=v_gdn_w_in, gdn_conv_w=v_gdn_conv_w, gdn_A_log=v_gdn_A_log, gdn_dt_bias=v_gdn_dt_bias, gdn_norm_g=v_gdn_norm_g, gdn_w_out=v_gdn_w_out, final_norm=v_final_norm)
    return _train_step(x, positions, loss_target, w, m, v)


_WEIGHT_ORDER = ("ffn1_norm", "ffn1_w_in", "ffn1_w_out", "mix_norm", "ffn2_norm", "ffn2_w_in", "ffn2_w_out",
                 "hyb_w_in", "hyb_dw_w", "hyb_dw_b", "hyb_ln_g", "hyb_ln_b", "hyb_w_out", "gdn_w_in", "gdn_conv_w",
                 "gdn_A_log", "gdn_dt_bias", "gdn_norm_g", "gdn_w_out", "final_norm")
_BIG = ("ffn1_w_in", "ffn1_w_out", "ffn2_w_in", "ffn2_w_out", "hyb_w_in", "hyb_w_out", "gdn_w_in", "gdn_w_out")
_PACK_COLS = 1024


def _lane16(vec):
    return jnp.zeros((1, 128), _F32).at[0, _GDN_HEADS:2 * _GDN_HEADS].set(vec)


def _pack_rows(parts):
    rows, where, at = [], [], 0
    for p in parts:
        flat = p.reshape(-1).astype(_F32)
        n = -(-flat.shape[0] // (8 * _PACK_COLS)) * 8
        rows.append(jnp.pad(flat, (0, n * _PACK_COLS - flat.shape[0])).reshape(n, _PACK_COLS))
        where.append((at, n, flat.shape[0], p.shape))
        at += n
    return jnp.concatenate(rows, axis=0), where


def _unpack_rows(pack, where):
    return [pack[at:at + n].reshape(-1)[:size].reshape(shape) for at, n, size, shape in where]


def _train_step(x, positions, loss_target, w, m, v):
    n_layers = w["ffn1_norm"].shape[0]
    n_hyb, n_gdn = w["hyb_w_in"].shape[0], w["gdn_w_in"].shape[0]
    d_model = x.shape[-1]
    xh, target = x[0], loss_target[0]
    chip = 2 * lax.axis_index("x") + lax.axis_index("y")
    where = jnp.stack([lax.axis_index("c"), chip]).astype(jnp.int32)

    def own_columns(shard):
        cols = shard.shape[-1]
        zeros = jnp.zeros(shard.shape[:-1] + (4 * cols,), _F32)
        return lax.dynamic_update_slice_in_dim(zeros, shard, chip * cols, axis=2)

    small_pack, small_where = _pack_rows([own_columns(w["hyb_dw_w"]), own_columns(w["gdn_conv_w"])])
    small_full = _all_reduce_small(small_pack)
    dw_w, conv_w = _unpack_rows(0.5 * small_full, small_where)
    dw_w = jnp.pad(dw_w, ((0, 0), (0, _HALO - _CONV_WIDTH), (0, 0)))

    groups = (("ffn1_w_in", "ffn1_w_out"), ("hyb_w_in", "hyb_w_out"), ("ffn2_w_in", "ffn2_w_out"),
              ("gdn_w_in", "gdn_w_out"))
    steps = [(0, (0, 2)), (1, (0, 1)), (0, (1, 2)), (2, (0, 1)), (3, (0, 1))]
    bufs = [None] * len(groups)
    pending, token = [], small_full
    for si, (gi, part) in enumerate(steps):
        if bufs[gi] is None:
            bufs[gi] = [_cast_slab(n, w[n], where, token) for n in groups[gi]]
        send, recv, bufs[gi], token = _gather_start(f"gather_start_{si}", bufs[gi], token, part)
        pending.append((send, recv))
    gdn_cols = w["gdn_w_in"].shape[-1]
    n_qkvz = 4 * _GDN_HEADS * _GDN_DIM
    n_ba = 4 * gdn_cols - n_qkvz
    wv = {}

    def arrive(si, after):
        gi, part = steps[si]
        send, recv = pending[si]
        bufs[gi] = _gather_wait(f"gather_wait_{si}", send, recv, bufs[gi], after, part)
        bufs[gi] = _gather_forward(f"gather_pass_{si}", bufs[gi], part)
        a, b = bufs[gi]
        kind = groups[gi][0]
        if kind == "gdn_w_in":
            gdn_in = jnp.transpose(a, (0, 2, 1, 3)).reshape(n_gdn, d_model, 4 * gdn_cols)
            wv["gdn_qkvz"] = gdn_in[:, :, :n_qkvz]
            wv["gdn_ba"] = jnp.pad(gdn_in[:, :, n_qkvz:], ((0, 0), (0, 0), (0, 128 - n_ba)))
        else:
            wv[kind] = a
        wv[groups[gi][1]] = b.reshape(b.shape[0], -1, d_model)

    tabs = _rotary_tables(positions)

    def mixer_args(layer):
        i = layer // 2
        if layer % 2 == 0:
            return (wv["hyb_w_in"], wv["hyb_w_out"], dw_w[i], w["hyb_dw_b"][i][None], w["hyb_ln_g"][i][None],
                    w["hyb_ln_b"][i][None], tabs, i)
        return (wv["gdn_qkvz"], wv["gdn_ba"], wv["gdn_w_out"], conv_w[i], _lane16(w["gdn_A_log"][i]),
                _lane16(w["gdn_dt_bias"][i]), w["gdn_norm_g"][i][None], i)

    h = xh
    saved = []
    arrive(0, token)
    for layer in range(n_layers):
        if layer == 1:
            arrive(2, h)
        h, s1 = _ffn_fwd(h, w["ffn1_norm"][layer][None], wv["ffn1_w_in"], wv["ffn1_w_out"], layer)
        gamma = w["mix_norm"][layer][None]
        if layer % 2 == 0:
            if layer == 0:
                arrive(1, h)
            h, sm = _hyb_fwd(h, gamma, *mixer_args(layer))
        else:
            if layer == 1:
                arrive(4, h)
            h, sm = _gdn_fwd(h, gamma, *mixer_args(layer))
        if layer == 0:
            arrive(3, h)
        h, s2 = _ffn_fwd(h, w["ffn2_norm"][layer][None], wv["ffn2_w_in"], wv["ffn2_w_out"], layer)
        saved.append((s1, sm, s2))
    dh, loss_row, d_final = _loss_head(h, w["final_norm"][None], target)

    d_norm = {k: [None] * n_layers for k in ("ffn1_norm", "mix_norm", "ffn2_norm")}
    d_hyb = [None] * n_hyb
    d_gdn = [None] * n_gdn
    grads = {name: None for name in _BIG}
    n_of = {name: w[name].shape[0] for name in _BIG}

    def rs_begin(layer, full, token):
        names = list(full)
        got = _rs_sibling(f"rs_sib_{layer}", [full[n] for n in names])
        sums = [_pair_sum(full[n], g, where) for n, g in zip(names, got)]
        send, recv, sums, lands, token = _rs_chips_start(f"rs_chips_start_{layer}", sums, token)
        return (layer, names, send, recv, sums, lands), token

    def rs_end(state, after):
        layer, names, send, recv, sums, lands = state
        sums, lands = _rs_chips_wait(f"rs_chips_wait_{layer}", send, recv, sums, lands, after)
        index = [layer if n.startswith("ffn") else layer // 2 for n in names]
        for n, li, pair, land in zip(names, index, sums, lands):
            grads[n] = _sum_of_four(pair, land, where, grads[n], li, n_of[n])
        joined = _rs_join(f"rs_join_{layer}", [grads[n] for n in names], index)
        grads.update(zip(names, joined))

    in_flight = None
    for layer in reversed(range(n_layers)):
        s1, sm, s2 = saved[layer]
        full = {}
        dh, d_norm["ffn2_norm"][layer], full["ffn2_w_in"], full["ffn2_w_out"] = _ffn_bwd(
            dh, s2, w["ffn2_norm"][layer][None], wv["ffn2_w_in"], wv["ffn2_w_out"], layer)
        gamma = w["mix_norm"][layer][None]
        if layer % 2 == 0:
            dh, d_norm["mix_norm"][layer], d_hyb[layer // 2], full["hyb_w_in"], full["hyb_w_out"] = _hyb_bwd(
                dh, sm, gamma, *mixer_args(layer))
        else:
            dh, d_norm["mix_norm"][layer], d_gdn[layer // 2], full["gdn_w_in"], full["gdn_w_out"] = _gdn_bwd(
                dh, sm, gamma, *mixer_args(layer), n_ba)
        dh, d_norm["ffn1_norm"][layer], full["ffn1_w_in"], full["ffn1_w_out"] = _ffn_bwd(
            dh, s1, w["ffn1_norm"][layer][None], wv["ffn1_w_in"], wv["ffn1_w_out"], layer)
        if in_flight is not None:
            rs_end(in_flight, [dh])
        if layer == 0:
            stack = lambda rows: jnp.concatenate(rows, axis=0)
            misc = jnp.concatenate([stack([g[1] for g in d_gdn])[:, _GDN_HEADS:2 * _GDN_HEADS].reshape(-1),
                                    stack([g[2] for g in d_gdn])[:, _GDN_HEADS:2 * _GDN_HEADS].reshape(-1),
                                    stack([g[3] for g in d_gdn]).reshape(-1), loss_row[0, :1]])
            parts = [stack(d_norm["ffn1_norm"]), stack(d_norm["mix_norm"]), stack(d_norm["ffn2_norm"]), d_final,
                     stack([g[1] for g in d_hyb]), stack([g[2] for g in d_hyb]), stack([g[3] for g in d_hyb]), misc,
                     jnp.stack([g[0][:_CONV_WIDTH] for g in d_hyb]), jnp.stack([g[0] for g in d_gdn])]
            pack, pack_where = _pack_rows(parts)
            token = small_sum = _all_reduce_small(pack)
        in_flight, token = rs_begin(layer, full, token)
        if layer > 0:
            dh = _after(dh, token)
    grad_x = dh[None]

    red = _unpack_rows(small_sum, pack_where)
    n_a = n_gdn * _GDN_HEADS
    misc = red[7]
    cw = w["hyb_dw_w"].shape[-1]
    gw = w["gdn_conv_w"].shape[-1]
    grads.update(
        ffn1_norm=red[0], mix_norm=red[1], ffn2_norm=red[2], final_norm=red[3].reshape(-1),
        hyb_dw_b=red[4], hyb_ln_g=red[5], hyb_ln_b=red[6],
        gdn_A_log=misc[:n_a].reshape(n_gdn, _GDN_HEADS), gdn_dt_bias=misc[n_a:2 * n_a].reshape(n_gdn, _GDN_HEADS),
        gdn_norm_g=misc[2 * n_a:2 * n_a + n_gdn * _GDN_DIM].reshape(n_gdn, _GDN_DIM),
        hyb_dw_w=lax.dynamic_slice_in_dim(red[8], chip * cw, cw, axis=2),
        gdn_conv_w=lax.dynamic_slice_in_dim(red[9], chip * gw, gw, axis=2))
    loss = misc[2 * n_a + n_gdn * _GDN_DIM]

    small = [n for n in _WEIGHT_ORDER if n not in _BIG]
    packs = [_pack_rows([src[n] for n in small]) for src in (w, grads, m, v)]
    outs = _adamw(packs[0][0], _after(packs[1][0], token), packs[2][0], packs[3][0])
    for name in _BIG:
        grads[name] = _after(grads[name], token)
    delta, new_m, new_v = {}, {}, {}
    for res, o in zip((delta, new_m, new_v), outs):
        res.update(dict(zip(small, _unpack_rows(o, packs[0][1]))))
    late = in_flight[1]
    early = {name: _adamw(w[name], grads[name], m[name], v[name], lo=1 if name in late else 0) for name in _BIG}
    rs_end(in_flight, [token, outs[0], *[early[name][0] for name in _BIG]])
    for name in _BIG:
        res = early[name]
        if name in late:
            res = _adamw(w[name], grads[name], m[name], v[name], lo=0, hi=1, into=res)
        delta[name], new_m[name], new_v[name] = res
    return (loss, grad_x, *[grads[n] for n in _WEIGHT_ORDER], *[delta[n] for n in _WEIGHT_ORDER],
            *[new_m[n] for n in _WEIGHT_ORDER], *[new_v[n] for n in _WEIGHT_ORDER])
```

```python
import functools

import jax
import jax.numpy as jnp
from jax import lax
from jax.experimental import pallas as pl
from jax.experimental.pallas import tpu as pltpu

_BF = jnp.bfloat16
_F32 = jnp.float32
_NORM_EPS = 1e-6
_VMEM_LIMIT_BYTES = 56 * 1024 * 1024

_A_HEADS = 8
_A_HEAD_DIM = 64
_WIN_BLOCK = 128
_DILATIONS = (1, 4, 16)
_CONV_WIDTH = 31
_GDN_HEADS = 8
_GDN_DIM = 128
_GDN_CHUNK = 64
_GDN_SHORT_CONV = 4

_ADAM_LR = 0.001
_ADAM_B1 = 0.9
_ADAM_B2 = 0.999
_ADAM_EPS = 1e-08
_ADAM_WD = 0.01
_ADAM_STEP = 10


def _params(sem):
    return pltpu.CompilerParams(dimension_semantics=sem, vmem_limit_bytes=_VMEM_LIMIT_BYTES)


def _dot(a, b, dn=(((1,), (0,)), ((), ()))):
    return lax.dot_general(a.astype(_BF), b.astype(_BF), dn, preferred_element_type=_F32)


def _dot_nt(a, b):
    return _dot(a, b, (((1,), (1,)), ((), ())))


def _dot_tn(a, b):
    return _dot(a, b, (((0,), (0,)), ((), ())))


_NN = (((1,), (0,)), ((), ()))
_NT = (((1,), (1,)), ((), ()))
_TN = (((0,), (0,)), ((), ()))
_ANY = pl.BlockSpec(memory_space=pl.ANY)


def _mm(name, a, b, *, grid, a_spec, b_spec, dn, out_shape, out_spec, red=None, acc_shape=None,
        pre_a=None, pre_b=None, epi=None, extras=(), extra_specs=(), sum_shape=None):
    n_ex = len(extras)
    n_in = 2 + n_ex
    nk = grid[red] if red is not None else 1
    n_out = 1 if sum_shape is None else 2

    def body(*refs):
        a_ref, b_ref = refs[0], refs[1]
        ex = refs[2:2 + n_ex]
        o_ref = refs[n_in]
        av, bv = a_ref[...], b_ref[...]
        if pre_a is not None:
            av = pre_a(av)
        if pre_b is not None:
            bv = pre_b(bv)
        part = lax.dot_general(av.astype(_BF), bv.astype(_BF), dn, preferred_element_type=_F32)

        def finish(acc):
            res = epi(acc, *[e[...] for e in ex]) if epi is not None else acc
            if sum_shape is not None:
                res, term = res
                s_ref = refs[n_in + 1]
                first = functools.reduce(jnp.logical_and, [pl.program_id(ax) == 0 for ax in range(len(grid)) if ax != red])

                @pl.when(first)
                def _():
                    s_ref[...] = jnp.zeros_like(s_ref)

                s_ref[...] += term
            if isinstance(res, tuple):
                for idx, val in enumerate(res):
                    o_ref[idx] = val.astype(o_ref.dtype)
            else:
                o_ref[...] = res.astype(o_ref.dtype)

        if red is None:
            finish(part)
        else:
            acc_ref = refs[n_in + n_out]
            k = pl.program_id(red)

            @pl.when(k == 0)
            def _():
                acc_ref[...] = part

            @pl.when(k > 0)
            def _():
                acc_ref[...] += part

            @pl.when(k == nk - 1)
            def _():
                finish(acc_ref[...])

    if sum_shape is not None:
        out_shape = [out_shape, jax.ShapeDtypeStruct(sum_shape, _F32)]
        out_spec = [out_spec, pl.BlockSpec(sum_shape, lambda *g: (0,) * len(sum_shape))]
    return pl.pallas_call(
        body, name=name, grid=grid, in_specs=[a_spec, b_spec, *extra_specs], out_specs=out_spec, out_shape=out_shape,
        scratch_shapes=[pltpu.VMEM(acc_shape, _F32)] if red is not None else [],
        compiler_params=_params(("arbitrary",) * len(grid)),
    )(a, b, *extras)


def _rowmap(name, body, tiled, params, tile_outs, param_outs=(), *, rows, tm, ncol=1, col_outer=False,
            into=None):
    nt, npar, nto, npo = len(tiled), len(params), len(tile_outs), len(param_outs)
    assert rows % tm == 0, (name, rows, tm)
    nrow = rows // tm
    grid = (ncol, nrow) if col_outer else (nrow, ncol)

    def ij(g):
        return (g[1], g[0]) if col_outer else (g[0], g[1])

    def cidx(col, j):
        return col(j) if callable(col) else col

    in_specs, operands = [], []
    for arr, width, col in tiled:
        operands.append(arr)
        if arr.ndim == 3:
            in_specs.append(pl.BlockSpec((arr.shape[0], tm, width),
                                         lambda *g, col=col: (0, ij(g)[0], cidx(col, ij(g)[1]))))
        else:
            in_specs.append(pl.BlockSpec((tm, width), lambda *g, col=col: (ij(g)[0], cidx(col, ij(g)[1]))))
    for p in params:
        if isinstance(p, tuple):
            arr, width, col = p
            operands.append(arr)
            in_specs.append(pl.BlockSpec((arr.shape[0], width), lambda *g, col=col: (0, cidx(col, ij(g)[1]))))
        else:
            operands.append(p)
            in_specs.append(pl.BlockSpec(p.shape, lambda *g, nd=p.ndim: (0,) * nd))
    out_shape, out_specs = [], []
    for total, width, col, dt in tile_outs:
        out_shape.append(jax.ShapeDtypeStruct((rows, total), dt))
        out_specs.append(pl.BlockSpec((tm, width), lambda *g, col=col: (ij(g)[0], cidx(col, ij(g)[1]))))
    for shp in param_outs:
        out_shape.append(jax.ShapeDtypeStruct(shp, _F32))
        out_specs.append(pl.BlockSpec(shp, lambda *g, nd=len(shp): (0,) * nd))
    n_in = nt + npar
    aliases = {}
    if into is not None:
        operands.append(into[0])
        in_specs.append(_ANY)
        aliases = {n_in: into[1]}
        n_in += 1

    def kern(*refs):
        vals = [r[...] for r in refs[:nt + npar]]
        touts = refs[n_in:n_in + nto]
        pouts = refs[n_in + nto:]
        res = body(*vals)
        if not isinstance(res, (tuple, list)):
            res = (res,)
        for r, val in zip(touts, res[:nto]):
            r[...] = val.astype(r.dtype)
        if npo:
            @pl.when((pl.program_id(0) == 0) & (pl.program_id(1) == 0))
            def _():
                for r in pouts:
                    r[...] = jnp.zeros_like(r)

            for r, val in zip(pouts, res[nto:]):
                r[...] += val

    return pl.pallas_call(
        kern, name=name, grid=grid, in_specs=in_specs, out_specs=out_specs, out_shape=out_shape,
        input_output_aliases=aliases, compiler_params=_params(("arbitrary", "arbitrary")),
    )(*operands)


def _vjp_body(fn, n_in, diff, n_ct):
    def body(*vals):
        ins, cts = vals[:n_in], vals[n_in:n_in + n_ct]
        outs, pull = jax.vjp(fn, *ins)
        single = not isinstance(outs, (tuple, list))
        if single:
            grads = pull(cts[0].astype(outs.dtype))
        else:
            grads = pull(tuple(c.astype(o.dtype) for c, o in zip(cts, outs)))
        return tuple(grads[i] for i in diff)
    return body


_TM = 512
_TM_WIDE = 1024


def _rms(h, gamma):
    hf = h.astype(_F32)
    return hf * lax.rsqrt(jnp.mean(hf * hf, axis=-1, keepdims=True) + _NORM_EPS) * gamma


def _rmsnorm_fwd(h, gamma):
    s, d = h.shape
    return _rowmap("rmsnorm_fwd", lambda hv, gv: _rms(hv, gv).astype(_BF), [(h, d, 0)], [gamma],
                   [(d, d, 0, _BF)], rows=s, tm=_TM)[0]


def _norm_bwd_epi(dxn, hv, dhv, gv):
    _, pull = jax.vjp(_rms, hv, gv)
    dh, dg = pull(dxn)
    return dhv + dh, dg


def _norm_bwd_args(h, dh_out, gamma, tm):
    d = h.shape[1]
    row = pl.BlockSpec((tm, d), lambda i, j: (i, 0))
    return dict(extras=(h, dh_out, gamma), extra_specs=(row, row, pl.BlockSpec((1, d), lambda i, j: (0, 0))),
                sum_shape=(1, d))


def _silu(x):
    return x * jax.nn.sigmoid(x)


def _ffn_up(h, gamma, w_in, layer):
    s, d = h.shape
    ns = w_in.shape[-1]
    tm = _TM

    def body(h_ref, gam_ref, w_ref, x_ref, g_ref, u_ref, a_ref):
        xv = _rms(h_ref[...], gam_ref[...]).astype(_BF)
        x_ref[...] = xv
        for half in range(2):
            cols = slice(half * ns, (half + 1) * ns)
            g = jnp.dot(xv, w_ref[half], preferred_element_type=_F32)
            u = jnp.dot(xv, w_ref[half + 2], preferred_element_type=_F32)
            g_ref[:, cols] = g.astype(_BF)
            u_ref[:, cols] = u.astype(_BF)
            a_ref[:, cols] = (_silu(g) * u).astype(_BF)

    row = pl.BlockSpec((tm, d), lambda i: (i, 0))
    out = pl.BlockSpec((tm, 2 * ns), lambda i: (i, 0))
    sds = jax.ShapeDtypeStruct((s, 2 * ns), _BF)
    return pl.pallas_call(
        body, name="ffn_up", grid=(s // tm,),
        in_specs=[row, pl.BlockSpec((1, d), lambda i: (0, 0)),
                  pl.BlockSpec((None, 4, d, ns), lambda i: (layer, 0, 0, 0), pipeline_mode=pl.Buffered(1))],
        out_specs=[row, out, out, out], out_shape=[jax.ShapeDtypeStruct((s, d), _BF), sds, sds, sds],
        compiler_params=_params(("arbitrary",)),
    )(h, gamma, w_in)


def _proj_residual(name, y, w, layer, h, scale):
    s, k = y.shape
    d = w.shape[-1]
    return _mm(name, y, w, grid=(s // _TM,), a_spec=pl.BlockSpec((_TM, k), lambda i: (i, 0)),
               b_spec=pl.BlockSpec((None, k, d), lambda i: (layer, 0, 0)), dn=_NN,
               out_shape=jax.ShapeDtypeStruct((s, d), _F32), out_spec=pl.BlockSpec((_TM, d), lambda i: (i, 0)),
               epi=lambda acc, hv: hv + scale * acc, extras=(h,),
               extra_specs=(pl.BlockSpec((_TM, d), lambda i: (i, 0)),))


def _ffn_dact(dh, w_out, layer, g, u):
    s, d = dh.shape
    f = g.shape[1]
    ns = f // 2

    def epi(da, gv, uv):
        gf, uf = gv.astype(_F32), uv.astype(_F32)
        sg = jax.nn.sigmoid(gf)
        return da * uf * (sg * (1.0 + gf * (1.0 - sg))), da * gf * sg

    tm = _CONV_TM
    gu = pl.BlockSpec((tm, f), lambda i: (i, 0))
    return _mm("ffn_dact", dh, w_out, grid=(s // tm,), a_spec=pl.BlockSpec((tm, d), lambda i: (i, 0)),
               b_spec=pl.BlockSpec((None, f, d), lambda i: (layer, 0, 0)), dn=_NT,
               out_shape=jax.ShapeDtypeStruct((2, s, f), _BF), out_spec=pl.BlockSpec((2, tm, f), lambda i: (0, i, 0)),
               pre_a=lambda v: 0.5 * v, epi=epi, extras=(g, u), extra_specs=(gu, gu))


def _wgrad_rows(name, y, dh, scale):
    s, k = y.shape
    d = dh.shape[1]
    out = _mm(name, y, dh, grid=(s // _TM,), a_spec=pl.BlockSpec((_TM, k), lambda i: (i, 0)),
              b_spec=pl.BlockSpec((_TM, d), lambda i: (i, 0)), dn=_TN, red=0, acc_shape=(k, d),
              out_shape=jax.ShapeDtypeStruct((k, d), _BF), out_spec=pl.BlockSpec((k, d), lambda i: (0, 0)),
              pre_b=(lambda v: scale * v) if scale != 1.0 else None)
    return out.reshape(4, k // 4, d)


def _ffn_dxn(dz, w_in, layer, h, gamma, dh_out):
    _, s, f = dz.shape
    d, ns = w_in.shape[2], w_in.shape[3]
    tm = _TM_WIDE
    return _mm("ffn_dxn", dz, w_in, grid=(s // tm, 4),
               a_spec=pl.BlockSpec((None, tm, ns), lambda i, j: (j // 2, i, j % 2)),
               b_spec=pl.BlockSpec((None, None, d, ns), lambda i, j: (layer, j, 0, 0)), dn=_NT, red=1,
               acc_shape=(tm, d), out_shape=jax.ShapeDtypeStruct((s, d), _F32),
               out_spec=pl.BlockSpec((tm, d), lambda i, j: (i, 0)), epi=_norm_bwd_epi,
               **_norm_bwd_args(h, dh_out, gamma, tm))


def _ffn_dwin(xn, dz):
    s, d = xn.shape
    ns = dz.shape[2] // 2
    tm = _TM_WIDE
    return _mm("ffn_dwin", xn, dz, grid=(4, s // tm), a_spec=pl.BlockSpec((tm, d), lambda j, i: (i, 0)),
               b_spec=pl.BlockSpec((None, tm, ns), lambda j, i: (j // 2, i, j % 2)), dn=_TN, red=1,
               acc_shape=(d, ns), out_shape=jax.ShapeDtypeStruct((4, d, ns), _BF),
               out_spec=pl.BlockSpec((None, d, ns), lambda j, i: (j, 0, 0)))


def _ffn_fwd(h, gamma, w_in, w_out, layer):
    xn, g, u, act = _ffn_up(h, gamma, w_in, layer)
    h_new = _proj_residual("ffn_down", act, w_out, layer, h, 0.5)
    return h_new, (h, xn, g, u, act)


def _ffn_bwd(dh, saved, gamma, w_in, w_out, layer):
    h, xn, g, u, act = saved
    dz = _ffn_dact(dh, w_out, layer, g, u)
    d_w_out = _wgrad_rows("ffn_dwout", act, dh, 0.5)
    dh_in, dgamma = _ffn_dxn(dz, w_in, layer, h, gamma, dh)
    d_w_in = _ffn_dwin(xn, dz)
    return dh_in, dgamma, d_w_in, d_w_out


def _loss_head(h, gamma, target):
    s, d = h.shape

    def body(hv, tv, gv):
        y, pull = jax.vjp(_rms, hv, gv)
        err = y - tv
        loss = 0.5 * jnp.sum(jnp.mean(err * err, axis=-1, keepdims=True), axis=0, keepdims=True)
        dh, dg = pull(err * (1.0 / d))
        return dh, jnp.broadcast_to(loss, (1, 128)), dg

    return _rowmap("loss_head", body, [(h, d, 0), (target, d, 0)], [gamma], [(d, d, 0, _F32)],
                   [(1, 128), (1, d)], rows=s, tm=_TM)


_ROT = _A_HEAD_DIM // 4
_ROPE_THETA = 500000.0
_A_WIDTH = _A_HEADS * _A_HEAD_DIM
_NEG = -1e30


def _rotary_tables(positions):
    inv_freq = jnp.power(jnp.float32(_ROPE_THETA), -jnp.arange(0, _ROT, 2, dtype=_F32) / _ROT)
    ang = positions.reshape(-1, 1).astype(_F32) * inv_freq
    cos, sin = jnp.cos(ang), jnp.sin(ang)
    s = ang.shape[0]
    pad = jnp.zeros((s, _A_HEAD_DIM - _ROT), _F32)
    c_head = jnp.concatenate([cos, cos, pad + 1.0], axis=1)
    s_head = jnp.concatenate([-sin, sin, pad], axis=1)
    c2, s2 = jnp.tile(c_head, (1, 2)), jnp.tile(s_head, (1, 2))
    scale = _A_HEAD_DIM ** -0.5
    tab_c = jnp.stack([scale * c2, c2, jnp.ones_like(c2)])
    tab_s = jnp.stack([scale * s2, s2, jnp.zeros_like(s2)])
    return tab_c, tab_s


def _pair_swap(z):
    lane = lax.broadcasted_iota(jnp.int32, z.shape, 1) % _A_HEAD_DIM
    half = _ROT // 2
    up = jnp.where(lane < _ROT, pltpu.roll(z, half, axis=1), 0.0)
    return jnp.where(lane < half, pltpu.roll(z, 128 - half, axis=1), up)


def _rot_apply(x, c, s, transpose):
    outs = []
    for k in range(x.shape[1] // 128):
        xs = x[:, k * 128:(k + 1) * 128]
        outs.append(c * xs + (_pair_swap(s * xs) if transpose else s * _pair_swap(xs)))
    return jnp.concatenate(outs, axis=1)


def _qkv_prep(proj, tab_c, tab_s):
    s = proj.shape[0]
    w = _A_WIDTH

    def body(x_ref, c_ref, s_ref, o_ref):
        o_ref[...] = _rot_apply(x_ref[...], c_ref[...], s_ref[...], False).astype(_BF)

    tab = pl.BlockSpec((None, _TM, 128), lambda i, j: (j, i, 0))
    blk = pl.BlockSpec((_TM, w), lambda i, j: (i, j))
    return pl.pallas_call(
        body, name="qkv_prep", grid=(s // _TM, 3), in_specs=[blk, tab, tab], out_specs=blk,
        out_shape=jax.ShapeDtypeStruct((s, 3 * w), _BF), compiler_params=_params(("arbitrary", "arbitrary")),
    )(proj, tab_c, tab_s)


def _head_masks():
    lane = lax.broadcasted_iota(jnp.int32, (_WIN_BLOCK, 128), 1)
    return [lane // _A_HEAD_DIM == hh for hh in range(2)]


def _band_masks(first_block):
    qi = lax.broadcasted_iota(jnp.int32, (_WIN_BLOCK, _WIN_BLOCK), 0)
    kj = lax.broadcasted_iota(jnp.int32, (_WIN_BLOCK, _WIN_BLOCK), 1)
    return (kj >= qi) & jnp.logical_not(first_block), kj <= qi


def _attn_fwd(qkv, dil):
    s = qkv.shape[0]
    w = _A_WIDTH
    length = s // dil
    nb = length // _WIN_BLOCK
    view = qkv.reshape(length, dil * 3 * w)

    def body(q_ref, kp_ref, kc_ref, vp_ref, vc_ref, o_ref, l_ref):
        allow_p, allow_c = _band_masks(pl.program_id(1) == 0)
        hm = _head_masks()
        for sl in range(w // 128):
            cols = slice(sl * 128, (sl + 1) * 128)
            q, kp, kc, vp, vc = q_ref[:, cols], kp_ref[:, cols], kc_ref[:, cols], vp_ref[:, cols], vc_ref[:, cols]
            o_s = jnp.zeros((_WIN_BLOCK, 128), _F32)
            l_s = jnp.zeros((_WIN_BLOCK, 128), _F32)
            for hh in range(2):
                qh = jnp.where(hm[hh], q, jnp.zeros_like(q))
                sp = jnp.where(allow_p, _dot_nt(qh, kp), _NEG)
                sc = jnp.where(allow_c, _dot_nt(qh, kc), _NEG)
                m = jnp.maximum(jnp.max(sp, axis=1, keepdims=True), jnp.max(sc, axis=1, keepdims=True))
                pp, pc = jnp.exp(sp - m), jnp.exp(sc - m)
                den = jnp.sum(pp, axis=1, keepdims=True) + jnp.sum(pc, axis=1, keepdims=True)
                oh = (_dot(pp, vp) + _dot(pc, vc)) / den
                o_s = jnp.where(hm[hh], oh, o_s)
                l_s = jnp.where(hm[hh], m + jnp.log(den), l_s)
            o_ref[:, cols] = o_s.astype(o_ref.dtype)
            l_ref[:, cols] = l_s

    def at(col, prev):
        if prev:
            return pl.BlockSpec((_WIN_BLOCK, w), lambda r, n: (jnp.maximum(n - 1, 0), 3 * r + col))
        return pl.BlockSpec((_WIN_BLOCK, w), lambda r, n: (n, 3 * r + col))

    out = pl.BlockSpec((_WIN_BLOCK, w), lambda r, n: (n, r))
    sds = jax.ShapeDtypeStruct((length, dil * w), _F32)
    o, lse = pl.pallas_call(
        body, name=f"attn_fwd_d{dil}", grid=(dil, nb),
        in_specs=[at(0, False), at(1, True), at(1, False), at(2, True), at(2, False)],
        out_specs=[out, out], out_shape=[jax.ShapeDtypeStruct(sds.shape, _BF), sds],
        compiler_params=_params(("arbitrary", "arbitrary")),
    )(view, view, view, view, view)
    return o.reshape(s, w), lse.reshape(s, w)


def _attn_bwd(qkv, o, lse, do, dlse, dil):
    s = qkv.shape[0]
    w = _A_WIDTH
    length = s // dil
    nb = length // _WIN_BLOCK
    total = dil * nb
    view = qkv.reshape(length, dil * 3 * w)
    o, lse, do, dlse = (t.reshape(length, dil * w) for t in (o, lse, do, dlse))

    def body(q_ref, kp_ref, kc_ref, vp_ref, vc_ref, o_ref, l_ref, do_ref, dl_ref, out_ref,
             car_q, car_k, car_v, cur_q, cur_kc, cur_vc, cur_kp, cur_vp):
        t = pl.program_id(0)
        working = t < total

        @pl.when(t == 0)
        def _():
            car_q[...] = jnp.zeros_like(car_q)
            car_k[...] = jnp.zeros_like(car_k)
            car_v[...] = jnp.zeros_like(car_v)

        @pl.when(jnp.logical_not(working))
        def _():
            cur_kp[...] = jnp.zeros_like(cur_kp)
            cur_vp[...] = jnp.zeros_like(cur_vp)

        @pl.when(working)
        def _():
            allow_p, allow_c = _band_masks(t % nb == 0)
            hm = _head_masks()
            for sl in range(w // 128):
                cols = slice(sl * 128, (sl + 1) * 128)
                q, kp, kc, vp, vc = q_ref[:, cols], kp_ref[:, cols], kc_ref[:, cols], vp_ref[:, cols], vc_ref[:, cols]
                ov, dov = o_ref[:, cols].astype(_F32), do_ref[:, cols].astype(_F32)
                lv, dlv = l_ref[:, cols], dl_ref[:, cols]
                zero = jnp.zeros((_WIN_BLOCK, 128), _F32)
                dq_s, dkp_s, dkc_s, dvp_s, dvc_s = zero, zero, zero, zero, zero
                for hh in range(2):
                    qh = jnp.where(hm[hh], q, jnp.zeros_like(q))
                    doh = jnp.where(hm[hh], dov, 0.0).astype(_BF)
                    lse_h = jnp.sum(jnp.where(hm[hh], lv, 0.0), axis=1, keepdims=True) * (1.0 / _A_HEAD_DIM)
                    delta = jnp.sum(jnp.where(hm[hh], dov * ov, 0.0), axis=1, keepdims=True)
                    dl = jnp.sum(jnp.where(hm[hh], dlv, 0.0), axis=1, keepdims=True)
                    pp = jnp.exp(jnp.where(allow_p, _dot_nt(qh, kp), _NEG) - lse_h)
                    pc = jnp.exp(jnp.where(allow_c, _dot_nt(qh, kc), _NEG) - lse_h)
                    dsp = (pp * (_dot_nt(doh, vp) - delta + dl)).astype(_BF)
                    dsc = (pc * (_dot_nt(doh, vc) - delta + dl)).astype(_BF)
                    dq_s = dq_s + jnp.where(hm[hh], _dot(dsp, kp) + _dot(dsc, kc), 0.0)
                    dkp_s = dkp_s + _dot_tn(dsp, qh)
                    dkc_s = dkc_s + _dot_tn(dsc, qh)
                    dvp_s = dvp_s + _dot_tn(pp, doh)
                    dvc_s = dvc_s + _dot_tn(pc, doh)
                cur_q[:, cols] = dq_s
                cur_kp[:, cols] = dkp_s
                cur_kc[:, cols] = dkc_s
                cur_vp[:, cols] = dvp_s
                cur_vc[:, cols] = dvc_s

        out_ref[:, 0:w] = car_q[...].astype(out_ref.dtype)
        out_ref[:, w:2 * w] = (car_k[...] + cur_kp[...]).astype(out_ref.dtype)
        out_ref[:, 2 * w:3 * w] = (car_v[...] + cur_vp[...]).astype(out_ref.dtype)

        @pl.when(working)
        def _():
            car_q[...] = cur_q[...]
            car_k[...] = cur_kc[...]
            car_v[...] = cur_vc[...]

    def place(t):
        tc = jnp.minimum(t, total - 1)
        return tc % nb, tc // nb

    def at(col, prev):
        if prev:
            return pl.BlockSpec((_WIN_BLOCK, w), lambda t: (jnp.maximum(place(t)[0] - 1, 0), 3 * place(t)[1] + col))
        return pl.BlockSpec((_WIN_BLOCK, w), lambda t: (place(t)[0], 3 * place(t)[1] + col))

    blk = pl.BlockSpec((_WIN_BLOCK, w), lambda t: place(t))
    scratch = [pltpu.VMEM((_WIN_BLOCK, w), _F32) for _ in range(8)]
    out = pl.pallas_call(
        body, name=f"attn_bwd_d{dil}", grid=(total + 1,),
        in_specs=[at(0, False), at(1, True), at(1, False), at(2, True), at(2, False), blk, blk, blk, blk],
        out_specs=pl.BlockSpec((_WIN_BLOCK, 3 * w), lambda t: place(jnp.maximum(t - 1, 0))),
        out_shape=jax.ShapeDtypeStruct((length, dil * 3 * w), _BF), scratch_shapes=scratch,
        compiler_params=_params(("arbitrary",)),
    )(view, view, view, view, view, o, lse, do, dlse)
    return out.reshape(s, 3 * w)


def _mix(o1, o2, o3, l1, l2, l3):
    m = jnp.maximum(jnp.maximum(l1, l2), l3)
    e1, e2, e3 = jnp.exp(l1 - m), jnp.exp(l2 - m), jnp.exp(l3 - m)
    return (e1 * o1 + e2 * o2 + e3 * o3) / (e1 + e2 + e3)


def _attn_mix_fwd(outs, lses):
    s = outs[0].shape[0]
    tiles = [(t, _A_WIDTH, 0) for t in (*outs, *lses)]
    return _rowmap("attn_mix_fwd", _mix, tiles, [], [(2 * _A_WIDTH, _A_WIDTH, 0, _BF)], rows=s, tm=_CONV_TM)[0]


def _attn_mix_bwd(outs, lses, dy):
    s = outs[0].shape[0]
    tiles = [(t, _A_WIDTH, 0) for t in (*outs, *lses, dy)]
    body = _vjp_body(_mix, 6, range(6), 1)
    return _rowmap("attn_mix_bwd", body, tiles, [], [(_A_WIDTH, _A_WIDTH, 0, _BF)] * 3 + [(_A_WIDTH, _A_WIDTH, 0, _F32)] * 3, rows=s, tm=_CONV_TM)


_CONV_TM = 256
_HALO = 32


def _ln_silu(pre, g, b):
    mu = jnp.mean(pre, axis=-1, keepdims=True)
    xc = pre - mu
    y = xc * lax.rsqrt(jnp.mean(xc * xc, axis=-1, keepdims=True) + _NORM_EPS)
    return _silu(y * g + b)


def _glu_into(buf, ua_ref, ub_ref, uah_ref, ubh_ref, first):
    halo = uah_ref[...] * jax.nn.sigmoid(ubh_ref[...])
    buf[0:_HALO, :] = jnp.where(first, 0.0, halo)
    buf[_HALO:, :] = ua_ref[...] * jax.nn.sigmoid(ub_ref[...])


_SHIFT_ROWS = _CONV_TM + _HALO - 8


def _shift_copies(sh, buf):
    for b in range(1, 8):
        sh[b - 1] = buf[b:b + _SHIFT_ROWS, :]


def _rows_at(buf, sh, off, rows):
    a, b = divmod(off, 8)
    return buf[8 * a:8 * a + rows, :] if b == 0 else sh[b - 1, 8 * a:8 * a + rows, :]


def _causal_taps(buf, sh, w_ref, rows):
    base = _HALO - (_CONV_WIDTH - 1)
    acc = jnp.zeros((rows, buf.shape[1]), _F32)
    for j in range(_CONV_WIDTH):
        acc = acc + w_ref[j:j + 1, :] * _rows_at(buf, sh, base + j, rows)
    return acc


def _conv_specs(s):
    tm = _CONV_TM
    per = tm // _HALO
    cw = _A_WIDTH
    cur = lambda col: pl.BlockSpec((tm, cw), lambda i: (i, col))
    prev = lambda col: pl.BlockSpec((_HALO, cw), lambda i: (jnp.maximum(i * per - 1, 0), col))
    return tm, per, cw, cur, prev


def _full2(arr):
    return pl.BlockSpec(arr.shape, lambda i: (0, 0))


def _conv_fwd(proj, dw_w, dw_b, ln_g, ln_b, y):
    s = proj.shape[0]
    tm, per, cw, cur, prev = _conv_specs(s)

    def body(ua_ref, ub_ref, uah_ref, ubh_ref, w_ref, b_ref, g_ref, be_ref, y_in, o_ref, buf, sh):
        del y_in
        _glu_into(buf, ua_ref, ub_ref, uah_ref, ubh_ref, pl.program_id(0) == 0)
        _shift_copies(sh, buf)
        pre = _causal_taps(buf, sh, w_ref, tm) + b_ref[...]
        o_ref[...] = _ln_silu(pre, g_ref[...], be_ref[...]).astype(_BF)

    return pl.pallas_call(
        body, name="conv_fwd", grid=(s // tm,),
        in_specs=[cur(3), cur(4), prev(3), prev(4), _full2(dw_w), _full2(dw_b), _full2(ln_g), _full2(ln_b), _ANY],
        out_specs=pl.BlockSpec((tm, cw), lambda i: (i, 1)), out_shape=jax.ShapeDtypeStruct(y.shape, y.dtype),
        scratch_shapes=[pltpu.VMEM((tm + _HALO, cw), _F32), pltpu.VMEM((7, _SHIFT_ROWS, cw), _F32)],
        input_output_aliases={8: 0}, compiler_params=_params(("arbitrary",)),
    )(proj, proj, proj, proj, dw_w, dw_b, ln_g, ln_b, y)


def _conv_bwd_pre(proj, dw_w, dw_b, ln_g, ln_b, dy):
    s = proj.shape[0]
    tm, per, cw, cur, prev = _conv_specs(s)

    def body(ua_ref, ub_ref, uah_ref, ubh_ref, w_ref, b_ref, g_ref, be_ref, dy_ref, dp_ref, dg_ref, dbe_ref, db_ref, buf, sh):
        i = pl.program_id(0)
        _glu_into(buf, ua_ref, ub_ref, uah_ref, ubh_ref, i == 0)
        _shift_copies(sh, buf)
        pre = _causal_taps(buf, sh, w_ref, tm) + b_ref[...]
        _, pull = jax.vjp(_ln_silu, pre, g_ref[...], be_ref[...])
        dpre, dg, dbe = pull(dy_ref[...])
        dp_ref[...] = dpre

        @pl.when(i == 0)
        def _():
            dg_ref[...] = jnp.zeros_like(dg_ref)
            dbe_ref[...] = jnp.zeros_like(dbe_ref)
            db_ref[...] = jnp.zeros_like(db_ref)

        dg_ref[...] += dg
        dbe_ref[...] += dbe
        db_ref[...] += jnp.sum(dpre, axis=0, keepdims=True)

    vec = pl.BlockSpec((1, cw), lambda i: (0, 0))
    vsd = jax.ShapeDtypeStruct((1, cw), _F32)
    return pl.pallas_call(
        body, name="conv_bwd_pre", grid=(s // tm,),
        in_specs=[cur(3), cur(4), prev(3), prev(4), _full2(dw_w), _full2(dw_b), _full2(ln_g), _full2(ln_b), cur(1)],
        out_specs=[pl.BlockSpec((tm, cw), lambda i: (i, 0)), vec, vec, vec],
        out_shape=[jax.ShapeDtypeStruct((s, cw), _F32), vsd, vsd, vsd],
        scratch_shapes=[pltpu.VMEM((tm + _HALO, cw), _F32), pltpu.VMEM((7, _SHIFT_ROWS, cw), _F32)],
        compiler_params=_params(("arbitrary",)),
    )(proj, proj, proj, proj, dw_w, dw_b, ln_g, ln_b, dy)


def _conv_bwd_taps(proj, dw_w, dpre):
    s = proj.shape[0]
    tm, per, cw, cur, prev = _conv_specs(s)
    nt = s // tm

    def body(ua_ref, ub_ref, uah_ref, ubh_ref, w_ref, dp_ref, dpn_ref, dglu_ref, dw_ref, buf, dbuf, sh, dsh):
        i = pl.program_id(0)
        _glu_into(buf, ua_ref, ub_ref, uah_ref, ubh_ref, i == 0)
        dpre = dp_ref[...]
        dbuf[0:tm, :] = dpre
        dbuf[tm:, :] = jnp.where(i == nt - 1, 0.0, dpn_ref[...])
        _shift_copies(sh, buf)
        _shift_copies(dsh, dbuf)
        acc = jnp.zeros((tm, cw), _F32)
        for j in range(_CONV_WIDTH):
            acc = acc + w_ref[j:j + 1, :] * _rows_at(dbuf, dsh, _CONV_WIDTH - 1 - j, tm)
        dglu_ref[...] = acc

        @pl.when(i == 0)
        def _():
            dw_ref[...] = jnp.zeros_like(dw_ref)

        base = _HALO - (_CONV_WIDTH - 1)
        for j in range(_CONV_WIDTH):
            dw_ref[j:j + 1, :] += jnp.sum(dpre * _rows_at(buf, sh, base + j, tm), axis=0, keepdims=True)

    nxt = pl.BlockSpec((_HALO, cw), lambda i: (jnp.minimum((i + 1) * per, s // _HALO - 1), 0))
    return pl.pallas_call(
        body, name="conv_bwd_taps", grid=(nt,),
        in_specs=[cur(3), cur(4), prev(3), prev(4), _full2(dw_w), pl.BlockSpec((tm, cw), lambda i: (i, 0)), nxt],
        out_specs=[pl.BlockSpec((tm, cw), lambda i: (i, 0)), pl.BlockSpec((_HALO, cw), lambda i: (0, 0))],
        out_shape=[jax.ShapeDtypeStruct((s, cw), _F32), jax.ShapeDtypeStruct((_HALO, cw), _F32)],
        scratch_shapes=[pltpu.VMEM((tm + _HALO, cw), _F32), pltpu.VMEM((tm + _HALO, cw), _F32),
                        pltpu.VMEM((7, _SHIFT_ROWS, cw), _F32), pltpu.VMEM((7, _SHIFT_ROWS, cw), _F32)],
        compiler_params=_params(("arbitrary",)),
    )(proj, proj, proj, proj, dw_w, dpre, dpre)


def _hyb_dproj(dqkv, tab_c, tab_s, dglu, proj):
    s = proj.shape[0]
    w = _A_WIDTH

    def body(d1_ref, d2_ref, d3_ref, c_ref, s_ref, dg_ref, ua_ref, ub_ref, o_ref):
        j = pl.program_id(1)

        @pl.when(j < 3)
        def _():
            x = d1_ref[...].astype(_F32) + d2_ref[...].astype(_F32) + d3_ref[...].astype(_F32)
            o_ref[...] = _rot_apply(x, c_ref[...], s_ref[...], True).astype(_BF)

        @pl.when(j == 3)
        def _():
            o_ref[...] = (dg_ref[...] * jax.nn.sigmoid(ub_ref[...])).astype(_BF)

        @pl.when(j == 4)
        def _():
            sg = jax.nn.sigmoid(ub_ref[...])
            o_ref[...] = (dg_ref[...] * ua_ref[...] * sg * (1.0 - sg)).astype(_BF)

    dq = pl.BlockSpec((_TM, w), lambda i, j: (i, jnp.minimum(j, 2)))
    tab = pl.BlockSpec((None, _TM, 128), lambda i, j: (jnp.minimum(j, 2), i, 0))
    col = lambda c: pl.BlockSpec((_TM, w), lambda i, j: (i, c))
    return pl.pallas_call(
        body, name="hyb_dproj", grid=(s // _TM, 5),
        in_specs=[dq, dq, dq, tab, tab, col(0), col(3), col(4)],
        out_specs=pl.BlockSpec((_TM, w), lambda i, j: (i, j)),
        out_shape=jax.ShapeDtypeStruct((s, 5 * w), _BF), compiler_params=_params(("arbitrary", "arbitrary")),
    )(*dqkv, tab_c, tab_s, dglu, proj, proj)


def _proj_in(name, xn, w, layer):
    s, d = xn.shape
    ns = w.shape[-1]
    return _mm(name, xn, w, grid=(4, s // _TM), a_spec=pl.BlockSpec((_TM, d), lambda j, i: (i, 0)),
               b_spec=pl.BlockSpec((None, None, d, ns), lambda j, i: (layer, j, 0, 0)), dn=_NN,
               out_shape=jax.ShapeDtypeStruct((s, 4 * ns), _F32), out_spec=pl.BlockSpec((_TM, ns), lambda j, i: (i, j)))


def _proj_in_dx(name, dz, w, layer, h, gamma, dh_out):
    s = dz.shape[0]
    d, ns = w.shape[2], w.shape[3]
    return _mm(name, dz, w, grid=(s // _TM, 4), a_spec=pl.BlockSpec((_TM, ns), lambda i, j: (i, j)),
               b_spec=pl.BlockSpec((None, None, d, ns), lambda i, j: (layer, j, 0, 0)), dn=_NT, red=1,
               acc_shape=(_TM, d), out_shape=jax.ShapeDtypeStruct((s, d), _F32),
               out_spec=pl.BlockSpec((_TM, d), lambda i, j: (i, 0)), epi=_norm_bwd_epi,
               **_norm_bwd_args(h, dh_out, gamma, _TM))


def _proj_in_dw(name, xn, dz):
    s, d = xn.shape
    ns = dz.shape[1] // 4
    return _mm(name, xn, dz, grid=(4, s // _TM), a_spec=pl.BlockSpec((_TM, d), lambda j, i: (i, 0)),
               b_spec=pl.BlockSpec((_TM, ns), lambda j, i: (i, j)), dn=_TN, red=1, acc_shape=(d, ns),
               out_shape=jax.ShapeDtypeStruct((4, d, ns), _BF),
               out_spec=pl.BlockSpec((None, d, ns), lambda j, i: (j, 0, 0)))


def _proj_out_dy(name, dh, w, layer):
    s, d = dh.shape
    k = w.shape[1]
    return _mm(name, dh, w, grid=(s // _TM,), a_spec=pl.BlockSpec((_TM, d), lambda i: (i, 0)),
               b_spec=pl.BlockSpec((None, k, d), lambda i: (layer, 0, 0)), dn=_NT,
               out_shape=jax.ShapeDtypeStruct((s, k), _F32), out_spec=pl.BlockSpec((_TM, k), lambda i: (i, 0)))


def _hyb_fwd(h, gamma, w_in, w_out, dw_w, dw_b, ln_g, ln_b, tabs, li):
    hn = _rmsnorm_fwd(h, gamma)
    proj = _proj_in("hyb_in", hn, w_in, li)
    qkv = _qkv_prep(proj, *tabs)
    branches = [_attn_fwd(qkv, dil) for dil in _DILATIONS]
    outs, lses = [b[0] for b in branches], [b[1] for b in branches]
    y = _attn_mix_fwd(outs, lses)
    y = _conv_fwd(proj, dw_w, dw_b, ln_g, ln_b, y)
    h_new = _proj_residual("hyb_out", y, w_out, li, h, 1.0)
    return h_new, (h, hn, proj, qkv, outs, lses, y)


def _hyb_bwd(dh, saved, gamma, w_in, w_out, dw_w, dw_b, ln_g, ln_b, tabs, li):
    h, hn, proj, qkv, outs, lses, y = saved
    dy = _proj_out_dy("hyb_dy", dh, w_out, li)
    d_w_out = _wgrad_rows("hyb_dwout", y, dh, 1.0)
    mixg = _attn_mix_bwd(outs, lses, dy)
    dqkv = [_attn_bwd(qkv, outs[b], lses[b], mixg[b], mixg[3 + b], dil) for b, dil in enumerate(_DILATIONS)]
    dpre, d_ln_g, d_ln_b, d_dw_b = _conv_bwd_pre(proj, dw_w, dw_b, ln_g, ln_b, dy)
    dglu, d_dw_w = _conv_bwd_taps(proj, dw_w, dpre)
    dproj = _hyb_dproj(dqkv, *tabs, dglu, proj)
    dh_in, dgamma = _proj_in_dx("hyb_dhn", dproj, w_in, li, h, gamma, dh)
    d_w_in = _proj_in_dw("hyb_dwin", hn, dproj)
    return dh_in, dgamma, (d_dw_w, d_dw_b, d_ln_g, d_ln_b), d_w_in, d_w_out


_GDN_BLOCK = 128
_GDN_HEADS_PER_STEP = 8


def _gdn_chunk(qs, ks, vs, gates, states, heads):
    n = len(qs)
    hs = range(n)
    c = qs[0].shape[0]
    lane = lax.broadcasted_iota(jnp.int32, gates.shape, 1)
    row = lax.broadcasted_iota(jnp.int32, (c, c), 0)
    col = lax.broadcasted_iota(jnp.int32, (c, c), 1)
    rowc = lax.broadcasted_iota(jnp.int32, (c, 1), 0)
    causal = row >= col
    eye = jnp.where(row == col, 1.0, 0.0)
    beta = [jnp.sum(jnp.where(lane == heads[h], gates, 0.0), axis=1, keepdims=True) for h in hs]
    g = [jnp.sum(jnp.where(lane == heads[h] + _GDN_HEADS, gates, 0.0), axis=1, keepdims=True) for h in hs]
    gc_row = [jnp.sum(jnp.where(row <= col, g[h], 0.0), axis=0, keepdims=True) for h in hs]
    gc_col = [jnp.sum(jnp.where(row == col, gc_row[h], 0.0), axis=1, keepdims=True) for h in hs]
    decay = [jnp.where(causal, jnp.exp(jnp.where(causal, gc_col[h] - gc_row[h], 0.0)), 0.0) for h in hs]
    kb = [ks[h] * beta[h] for h in hs]
    l_mat = [jnp.where(row > col, _dot_nt(kb[h], ks[h]) * decay[h], 0.0) for h in hs]
    t_inv = [eye - l_mat[h] for h in hs]
    x = [_dot(l_mat[h], l_mat[h]) for h in hs]
    for it in range(5):
        t_inv = [t_inv[h] + _dot(t_inv[h], x[h]) for h in hs]
        if it < 4:
            x = [_dot(x[h], x[h]) for h in hs]
    eg = [jnp.exp(gc_col[h]) for h in hs]
    u = [_dot(t_inv[h], vs[h] * beta[h]) for h in hs]
    w = [_dot(t_inv[h], kb[h] * eg[h]) for h in hs]
    attn = [jnp.where(causal, _dot_nt(qs[h], ks[h]) * decay[h], 0.0) for h in hs]
    g_last = [jnp.sum(jnp.where(rowc == c - 1, gc_col[h], 0.0), axis=0, keepdims=True) for h in hs]
    k_dec = [ks[h] * jnp.exp(g_last[h] - gc_col[h]) for h in hs]
    v_new = [u[h] - _dot(w[h], states[h]) for h in hs]
    o = [_dot(qs[h] * eg[h], states[h]) + _dot(attn[h], v_new[h]) for h in hs]
    new_states = [states[h] * jnp.exp(g_last[h]) + _dot_tn(k_dec[h], v_new[h]) for h in hs]
    return o, new_states


def _gdn_chunk_fwd(q, k, v, gates):
    s = q.shape[0]
    cb, c = _GDN_BLOCK, _GDN_CHUNK
    nblk, per = s // cb, cb // c

    hps = _GDN_HEADS_PER_STEP

    def body(q_ref, k_ref, v_ref, g_ref, o_ref, st_ref, state):
        @pl.when(pl.program_id(1) == 0)
        def _():
            state[...] = jnp.zeros_like(state)

        heads = [pl.program_id(0) * hps + hh for hh in range(hps)]
        cols = [slice(hh * _GDN_DIM, (hh + 1) * _GDN_DIM) for hh in range(hps)]
        for ci in range(per):
            rows = slice(ci * c, (ci + 1) * c)
            sts = [state[hh] for hh in range(hps)]
            for hh in range(hps):
                st_ref[hh, ci] = sts[hh]
            outs, new = _gdn_chunk([q_ref[rows, cl] for cl in cols], [k_ref[rows, cl] for cl in cols],
                                   [v_ref[rows, cl] for cl in cols], g_ref[rows, :], sts, heads)
            for hh in range(hps):
                o_ref[rows, cols[hh]] = outs[hh]
                state[hh] = new[hh]

    slab = pl.BlockSpec((cb, hps * _GDN_DIM), lambda h, i: (i, h))
    return pl.pallas_call(
        body, name="gdn_chunk_fwd", grid=(_GDN_HEADS // hps, nblk),
        in_specs=[slab, slab, slab, pl.BlockSpec((cb, 128), lambda h, i: (i, 0))],
        out_specs=[slab, pl.BlockSpec((hps, per, _GDN_DIM, _GDN_DIM), lambda h, i: (h, i, 0, 0))],
        out_shape=[jax.ShapeDtypeStruct((s, _GDN_HEADS * _GDN_DIM), _F32),
                   jax.ShapeDtypeStruct((_GDN_HEADS, s // c, _GDN_DIM, _GDN_DIM), _F32)],
        scratch_shapes=[pltpu.VMEM((hps, _GDN_DIM, _GDN_DIM), _F32)],
        compiler_params=_params(("arbitrary", "arbitrary")),
    )(q, k, v, gates)


def _gdn_chunk_bwd(q, k, v, gates, states, do):
    s = q.shape[0]
    cb, c = _GDN_BLOCK, _GDN_CHUNK
    nblk, per = s // cb, cb // c

    hps = _GDN_HEADS_PER_STEP

    def body(q_ref, k_ref, v_ref, g_ref, st_ref, do_ref, dq_ref, dk_ref, dv_ref, dg_ref, dstate):
        @pl.when(pl.program_id(1) == 0)
        def _():
            dstate[...] = jnp.zeros_like(dstate)

        heads = [pl.program_id(0) * hps + hh for hh in range(hps)]
        cols = [slice(hh * _GDN_DIM, (hh + 1) * _GDN_DIM) for hh in range(hps)]
        for ci in reversed(range(per)):
            rows = slice(ci * c, (ci + 1) * c)
            fn = functools.partial(_gdn_chunk, heads=heads)
            _, pull = jax.vjp(fn, [q_ref[rows, cl] for cl in cols], [k_ref[rows, cl] for cl in cols],
                              [v_ref[rows, cl] for cl in cols], g_ref[rows, :], [st_ref[hh, ci] for hh in range(hps)])
            dq, dk, dv, dg, dst = pull(([do_ref[rows, cl] for cl in cols], [dstate[hh] for hh in range(hps)]))
            dg_ref[rows, :] = dg
            for hh in range(hps):
                dq_ref[rows, cols[hh]] = dq[hh]
                dk_ref[rows, cols[hh]] = dk[hh]
                dv_ref[rows, cols[hh]] = dv[hh]
                dstate[hh] = dst[hh]

    slab = pl.BlockSpec((cb, hps * _GDN_DIM), lambda h, i: (nblk - 1 - i, h))
    sds = jax.ShapeDtypeStruct((s, _GDN_HEADS * _GDN_DIM), _F32)
    return pl.pallas_call(
        body, name="gdn_chunk_bwd", grid=(_GDN_HEADS // hps, nblk),
        in_specs=[slab, slab, slab, pl.BlockSpec((cb, 128), lambda h, i: (nblk - 1 - i, 0)),
                  pl.BlockSpec((hps, per, _GDN_DIM, _GDN_DIM), lambda h, i: (h, nblk - 1 - i, 0, 0)), slab],
        out_specs=[slab, slab, slab, pl.BlockSpec((None, cb, 128), lambda h, i: (h, nblk - 1 - i, 0))],
        out_shape=[sds, sds, sds, jax.ShapeDtypeStruct((_GDN_HEADS // hps, s, 128), _F32)],
        scratch_shapes=[pltpu.VMEM((hps, _GDN_DIM, _GDN_DIM), _F32)],
        compiler_params=_params(("arbitrary", "arbitrary")),
    )(q, k, v, gates, states, do)


_SHALO = 8


def _swish_l2(c, normalize, scale):
    y = _silu(c)
    if normalize:
        y = y * lax.rsqrt(jnp.sum(y * y, axis=-1, keepdims=True) + _NORM_EPS) * scale
    return y


def _short_taps(buf, w_ref, rows):
    base = _SHALO - (_GDN_SHORT_CONV - 1)
    acc = jnp.zeros((rows, buf.shape[1]), _F32)
    for t in range(_GDN_SHORT_CONV):
        acc = acc + w_ref[t:t + 1, :] * buf[base + t:base + t + rows, :]
    return acc


_GDN_W = _GDN_HEADS * _GDN_DIM


def _gdn_conv_fwd(proj, conv_w, grp, normalize, scale):
    s = proj.shape[0]
    tm = _CONV_TM
    per = tm // _SHALO

    def body(x_ref, xp_ref, w_ref, o_ref, buf):
        buf[0:_SHALO, :] = jnp.where(pl.program_id(0) == 0, 0.0, xp_ref[...])
        buf[_SHALO:, :] = x_ref[...]
        conv = _short_taps(buf, w_ref, tm)
        for h in range(_GDN_HEADS):
            cols = slice(h * _GDN_DIM, (h + 1) * _GDN_DIM)
            o_ref[:, cols] = _swish_l2(conv[:, cols], normalize, scale)

    return pl.pallas_call(
        body, name=f"gdn_conv_fwd_{grp}", grid=(s // tm,),
        in_specs=[pl.BlockSpec((tm, _GDN_W), lambda i: (i, grp)),
                  pl.BlockSpec((_SHALO, _GDN_W), lambda i: (jnp.maximum(i * per - 1, 0), grp)),
                  pl.BlockSpec((_GDN_SHORT_CONV, _GDN_W), lambda i: (0, grp))],
        out_specs=pl.BlockSpec((tm, _GDN_W), lambda i: (i, 0)),
        out_shape=jax.ShapeDtypeStruct((s, _GDN_W), _F32),
        scratch_shapes=[pltpu.VMEM((tm + _SHALO, _GDN_W), _F32)], compiler_params=_params(("arbitrary",)),
    )(proj, proj, conv_w)


def _gdn_conv_bwd(proj, conv_w, grp, normalize, scale, dy, dproj):
    s = proj.shape[0]
    tm = _CONV_TM
    per = tm // _SHALO
    nt = s // tm
    taps = _GDN_SHORT_CONV

    def body(x_ref, xp_ref, xn_ref, w_ref, dy_ref, dyn_ref, dp_in, dx_ref, dw_ref, buf, dbuf):
        del dp_in
        i = pl.program_id(0)
        last = i == nt - 1
        buf[0:_SHALO, :] = jnp.where(i == 0, 0.0, xp_ref[...])
        buf[_SHALO:_SHALO + tm, :] = x_ref[...]
        buf[_SHALO + tm:, :] = jnp.where(last, 0.0, xn_ref[...])
        conv = _short_taps(buf, w_ref, tm + _SHALO)
        dy_ext = jnp.concatenate([dy_ref[...], jnp.where(last, 0.0, dyn_ref[...])], axis=0)
        for h in range(_GDN_HEADS):
            cols = slice(h * _GDN_DIM, (h + 1) * _GDN_DIM)
            _, pull = jax.vjp(lambda c: _swish_l2(c, normalize, scale), conv[:, cols])
            dbuf[:, cols] = pull(dy_ext[:, cols])[0]
        acc = jnp.zeros((tm, _GDN_W), _F32)
        for t in range(taps):
            o = taps - 1 - t
            acc = acc + w_ref[t:t + 1, :] * dbuf[o:o + tm, :]
        dx_ref[...] = acc.astype(dx_ref.dtype)

        @pl.when(i == 0)
        def _():
            dw_ref[...] = jnp.zeros_like(dw_ref)

        base = _SHALO - (taps - 1)
        for t in range(taps):
            dw_ref[t:t + 1, :] += jnp.sum(dbuf[0:tm, :] * buf[base + t:base + t + tm, :], axis=0, keepdims=True)

    cur = lambda col: pl.BlockSpec((tm, _GDN_W), lambda i: (i, col))
    nxt = lambda col: pl.BlockSpec((_SHALO, _GDN_W), lambda i: (jnp.minimum((i + 1) * per, s // _SHALO - 1), col))
    return pl.pallas_call(
        body, name=f"gdn_conv_bwd_{grp}", grid=(nt,),
        in_specs=[cur(grp), pl.BlockSpec((_SHALO, _GDN_W), lambda i: (jnp.maximum(i * per - 1, 0), grp)), nxt(grp),
                  pl.BlockSpec((taps, _GDN_W), lambda i: (0, grp)), cur(0), nxt(0), _ANY],
        out_specs=[cur(grp), pl.BlockSpec((_SHALO, _GDN_W), lambda i: (0, 0))],
        out_shape=[jax.ShapeDtypeStruct(dproj.shape, dproj.dtype), jax.ShapeDtypeStruct((_SHALO, _GDN_W), _F32)],
        scratch_shapes=[pltpu.VMEM((tm + 2 * _SHALO, _GDN_W), _F32), pltpu.VMEM((tm + _SHALO, _GDN_W), _F32)],
        input_output_aliases={6: 0}, compiler_params=_params(("arbitrary",)),
    )(proj, proj, proj, conv_w, dy, dy, dproj)


def _softplus(z):
    return jnp.maximum(z, 0.0) + jnp.log(1.0 + jnp.exp(-jnp.abs(z)))


def _gates(ba, a_lane, dt_lane):
    lane = lax.broadcasted_iota(jnp.int32, ba.shape, 1)
    g = -jnp.exp(a_lane) * _softplus(ba + dt_lane)
    return jnp.where(lane < _GDN_HEADS, jax.nn.sigmoid(ba), jnp.where(lane < 2 * _GDN_HEADS, g, 0.0))


def _gates_fwd(ba, a_lane, dt_lane):
    return _rowmap("gdn_gates_fwd", _gates, [(ba, 128, 0)], [a_lane, dt_lane], [(128, 128, 0, _F32)],
                   rows=ba.shape[0], tm=_TM)[0]


def _gates_bwd(ba, a_lane, dt_lane, dgates):
    def body(bav, dg8, av, dv):
        _, pull = jax.vjp(_gates, bav, av, dv)
        return pull(jnp.sum(dg8, axis=0))

    return _rowmap("gdn_gates_bwd", body, [(ba, 128, 0), (dgates, 128, 0)], [a_lane, dt_lane], [(128, 128, 0, _BF)],
                   [(1, 128), (1, 128)], rows=ba.shape[0], tm=_TM)


def _out_gate(o, z, g):
    return (_rms(o, g) * _silu(z)).astype(_BF)


def _head_cols():
    return [slice(h * _GDN_DIM, (h + 1) * _GDN_DIM) for h in range(_GDN_HEADS)]


def _out_gate_fwd(o, proj, norm_g):
    def body(ov, zv, gv):
        return jnp.concatenate([_out_gate(ov[:, cl], zv[:, cl], gv) for cl in _head_cols()], axis=1)

    return _rowmap("gdn_out_gate_fwd", body, [(o, _GDN_W, 0), (proj, _GDN_W, 3)], [norm_g], [(_GDN_W, _GDN_W, 0, _BF)],
                   rows=o.shape[0], tm=_CONV_TM)[0]


def _out_gate_bwd(o, proj, norm_g, dy):
    def body(ov, zv, dyv, gv):
        d_o, d_z, d_g = [], [], jnp.zeros_like(gv)
        for cl in _head_cols():
            _, pull = jax.vjp(_out_gate, ov[:, cl], zv[:, cl], gv)
            g_o, g_z, g_g = pull(dyv[:, cl].astype(_BF))
            d_o.append(g_o)
            d_z.append(g_z)
            d_g = d_g + g_g
        return jnp.concatenate(d_o, axis=1), jnp.concatenate(d_z, axis=1), d_g

    return _rowmap("gdn_out_gate_bwd", body, [(o, _GDN_W, 0), (proj, _GDN_W, 3), (dy, _GDN_W, 0)], [norm_g],
                   [(_GDN_W, _GDN_W, 0, _F32), (proj.shape[1], _GDN_W, 3, _BF)], [(1, _GDN_DIM)],
                   rows=o.shape[0], tm=_CONV_TM)


def _plain_in(name, xn, w, layer, tn):
    s, d = xn.shape
    n = w.shape[2]
    return _mm(name, xn, w, grid=(n // tn, s // _TM), a_spec=pl.BlockSpec((_TM, d), lambda j, i: (i, 0)),
               b_spec=pl.BlockSpec((None, d, tn), lambda j, i: (layer, 0, j)), dn=_NN,
               out_shape=jax.ShapeDtypeStruct((s, n), _F32), out_spec=pl.BlockSpec((_TM, tn), lambda j, i: (i, j)))


def _plain_dx(name, dz, w, layer, tn, norm=None):
    s, n = dz.shape
    d = w.shape[1]
    row = pl.BlockSpec((_TM, d), lambda i, j: (i, 0))
    kw = {}
    if norm is not None:
        extra, h, gamma, dh_out = norm
        kw = _norm_bwd_args(h, dh_out, gamma, _TM)
        kw.update(extras=(extra, *kw["extras"]), extra_specs=(row, *kw["extra_specs"]),
                  epi=lambda acc, e, hv, dhv, gv: _norm_bwd_epi(acc + e, hv, dhv, gv))
    return _mm(name, dz, w, grid=(s // _TM, n // tn), a_spec=pl.BlockSpec((_TM, tn), lambda i, j: (i, j)),
               b_spec=pl.BlockSpec((None, d, tn), lambda i, j: (layer, 0, j)), dn=_NT, red=1, acc_shape=(_TM, d),
               out_shape=jax.ShapeDtypeStruct((s, d), _F32), out_spec=row, **kw)


def _plain_dw(name, xn, dz, tn):
    s, d = xn.shape
    n = dz.shape[1]
    return _mm(name, xn, dz, grid=(n // tn, s // _TM), a_spec=pl.BlockSpec((_TM, d), lambda j, i: (i, 0)),
               b_spec=pl.BlockSpec((_TM, tn), lambda j, i: (i, j)), dn=_TN, red=1, acc_shape=(d, tn),
               out_shape=jax.ShapeDtypeStruct((d, n), _BF), out_spec=pl.BlockSpec((d, tn), lambda j, i: (0, j)))


_GDN_QK_SCALE = _GDN_DIM ** -0.5


def _gdn_fwd(h, gamma, w_qkvz, w_ba, w_out, conv_w, a_lane, dt_lane, norm_g, li):
    hn = _rmsnorm_fwd(h, gamma)
    proj = _plain_in("gdn_in", hn, w_qkvz, li, 1024)
    ba = _plain_in("gdn_in_ba", hn, w_ba, li, 128)
    q = _gdn_conv_fwd(proj, conv_w, 0, True, _GDN_QK_SCALE)
    k = _gdn_conv_fwd(proj, conv_w, 1, True, 1.0)
    v = _gdn_conv_fwd(proj, conv_w, 2, False, 1.0)
    gates = _gates_fwd(ba, a_lane, dt_lane)
    o, states = _gdn_chunk_fwd(q, k, v, gates)
    y = _out_gate_fwd(o, proj, norm_g)
    h_new = _proj_residual("gdn_out", y, w_out, li, h, 1.0)
    return h_new, (h, hn, proj, ba, q, k, v, gates, states, o, y)


def _gdn_bwd(dh, saved, gamma, w_qkvz, w_ba, w_out, conv_w, a_lane, dt_lane, norm_g, li, n_ba):
    h, hn, proj, ba, q, k, v, gates, states, o, y = saved
    dy = _proj_out_dy("gdn_dy", dh, w_out, li)
    d_w_out = _wgrad_rows("gdn_dwout", y, dh, 1.0)
    do, dproj, d_norm_g = _out_gate_bwd(o, proj, norm_g, dy)
    dq, dk, dv, dgates = _gdn_chunk_bwd(q, k, v, gates, states, do)
    dproj, dcw_q = _gdn_conv_bwd(proj, conv_w, 0, True, _GDN_QK_SCALE, dq, dproj)
    dproj, dcw_k = _gdn_conv_bwd(proj, conv_w, 1, True, 1.0, dk, dproj)
    dproj, dcw_v = _gdn_conv_bwd(proj, conv_w, 2, False, 1.0, dv, dproj)
    dba, d_a, d_dt = _gates_bwd(ba, a_lane, dt_lane, dgates)
    dhn_ba = _plain_dx("gdn_dhn_ba", dba, w_ba, li, 128)
    dh_in, dgamma = _plain_dx("gdn_dhn", dproj, w_qkvz, li, 1024, norm=(dhn_ba, h, gamma, dh))
    dw_qkvz = _plain_dw("gdn_dwin", hn, dproj, 1024)
    dw_ba = _plain_dw("gdn_dwin_ba", hn, dba, 128)
    d_conv_w = jnp.concatenate([dcw_q, dcw_k, dcw_v], axis=1)[:_GDN_SHORT_CONV]
    d_w_in = jnp.concatenate([dw_qkvz, dw_ba[:, :n_ba]], axis=1)
    d_w_in = jnp.transpose(d_w_in.reshape(d_w_in.shape[0], 4, -1), (1, 0, 2))
    return dh_in, dgamma, (d_conv_w, d_a, d_dt, d_norm_g), d_w_in, d_w_out


_MESH = pl.DeviceIdType.MESH
_ROW_TM = 384


def _place():
    x, y, c = lax.axis_index("x"), lax.axis_index("y"), lax.axis_index("c")
    chips = [(1 - x, y), (x, 1 - y), (1 - x, 1 - y)]
    return x, y, c, chips, [2 * px + py for px, py in chips]


def _remote(src, dst, send_sem, recv_sem, device):
    return pltpu.make_async_remote_copy(src_ref=src, dst_ref=dst, send_sem=send_sem, recv_sem=recv_sem,
                                        device_id=device, device_id_type=_MESH)


def _rows2d(t):
    return t.reshape(-1, t.shape[-1])


def _row_tile(rows):
    for tm in range(min(rows, _ROW_TM) // 8 * 8, 7, -8):
        if rows % tm == 0:
            return tm
    raise ValueError(f"no row tile for {rows} rows")


def _placed_map(name, fn, where, ins, out_rows, cols, out_dtype, out_row, steps, tm, into=None, after=None):
    n_in = len(ins)
    n_extra = (into is not None) + (after is not None)

    def kern(where_ref, *refs):
        del where_ref
        o_ref = refs[n_in + n_extra]
        o_ref[...] = fn(*[r[...] for r in refs[:n_in]]).astype(o_ref.dtype)

    in_specs, operands = [], [where]
    for arr, rowfn in ins:
        operands.append(arr)
        if arr.ndim == 3:
            in_specs.append(pl.BlockSpec((arr.shape[0], tm, cols), lambda i, wh, f=rowfn: (0, f(i, wh), 0)))
        else:
            in_specs.append(pl.BlockSpec((tm, cols), lambda i, wh, f=rowfn: (f(i, wh), 0)))
    aliases = {}
    if into is not None:
        operands.append(into)
        in_specs.append(_ANY)
        aliases = {n_in + 1: 0}
    if after is not None:
        operands.append(after)
        in_specs.append(_ANY)
    spec = pltpu.PrefetchScalarGridSpec(
        num_scalar_prefetch=1, grid=(steps,), in_specs=in_specs,
        out_specs=pl.BlockSpec((tm, cols), lambda i, wh: (out_row(i, wh), 0)))
    return pl.pallas_call(
        kern, name=name, grid_spec=spec, out_shape=jax.ShapeDtypeStruct((out_rows, cols), out_dtype),
        input_output_aliases=aliases, compiler_params=_params(("arbitrary",)),
    )(*operands)


def _cast_slab(name, t, where, after):
    n_l, r, cols = t.shape
    tm = _row_tile(r)
    per = r // tm
    buf = _placed_map("cast_" + name, lambda val: val, where, [(_rows2d(t), lambda i, wh: i)], n_l * 4 * r, cols, _BF,
                      lambda i, wh: ((i // per) * 4 + wh[1]) * per + i % per, n_l * per, tm, after=after)
    return buf.reshape(n_l, 4, r, cols)


_HBM = pl.BlockSpec(memory_space=pltpu.HBM)
_SEM = pl.BlockSpec(memory_space=pltpu.SEMAPHORE)
_EFFECT = pltpu.SideEffectType.DATAFLOW_SIDE_EFFECTING


def _core_layers(n_layers, c, part):
    half = n_layers // 2
    sub = half // part[1]
    return pl.ds(c * half + part[0] * sub, sub), pl.ds((1 - c) * half + part[0] * sub, sub)


def _ici_copies(refs, send, recv, arrivals, part):
    x, y, c, chips, idx = _place()
    me = 2 * x + y
    res = []
    for t, ref in enumerate(refs):
        mine, _ = _core_layers(ref.shape[0], c, part)
        for k in range(3):
            sem = 3 * t + k
            block = ref.at[mine, idx[k] if arrivals else me]
            res.append(_remote(block, block, send.at[sem], recv.at[sem], (*chips[k], c)))
    return res


def _gather_start(name, bufs, after, part):
    n = len(bufs)

    def body(*refs):
        send, recv = refs[n + 1], refs[n + 2]
        for cp in _ici_copies(refs[:n], send, recv, False, part):
            cp.start()
        refs[2 * n + 3][...] = jnp.zeros((8, 128), _F32)

    res = pl.pallas_call(
        body, name=name,
        out_shape=(pltpu.SemaphoreType.DMA((3 * n,)), pltpu.SemaphoreType.DMA((3 * n,)),
                   *[pltpu.HBM(b.shape, b.dtype) for b in bufs], jax.ShapeDtypeStruct((8, 128), _F32)),
        in_specs=[*[_HBM] * n, _ANY], out_specs=(_SEM, _SEM, *[_HBM] * n, pl.BlockSpec(memory_space=pltpu.VMEM)),
        input_output_aliases={t: 2 + t for t in range(n)},
        compiler_params=pltpu.CompilerParams(has_side_effects=_EFFECT),
    )(*[pltpu.with_memory_space_constraint(b, pltpu.HBM) for b in bufs], after)
    return res[0], res[1], list(res[2:2 + n]), res[2 + n]


def _gather_wait(name, send_sem, recv_sem, bufs, after, part):
    n = len(bufs)

    def body(*refs):
        send, recv = refs[n], refs[n + 1]
        for cp in _ici_copies(refs[:n], send, recv, False, part):
            cp.wait_send()
        for cp in _ici_copies(refs[:n], send, recv, True, part):
            cp.wait_recv()

    res = pl.pallas_call(
        body, name=name, out_shape=tuple(pltpu.HBM(b.shape, b.dtype) for b in bufs),
        in_specs=[*[_HBM] * n, _SEM, _SEM, _ANY], out_specs=tuple([_HBM] * n),
        input_output_aliases={t: t for t in range(n)},
        compiler_params=pltpu.CompilerParams(has_side_effects=_EFFECT),
    )(*bufs, send_sem, recv_sem, after)
    return list(res)


def _gather_forward(name, bufs, part):
    n = len(bufs)

    def body(*refs):
        outs, send, recv = refs[n:2 * n], refs[2 * n], refs[2 * n + 1]
        x, y, c, _, idx = _place()
        copies, arrive = [], []
        for t, ref in enumerate(outs):
            mine, other = _core_layers(ref.shape[0], c, part)
            for k in range(3):
                sem = 3 * t + k
                landed, dst = ref.at[mine, idx[k]], ref.at[other, idx[k]]
                copies.append(_remote(landed, landed, send.at[sem], recv.at[sem], (x, y, 1 - c)))
                arrive.append(_remote(dst, dst, send.at[sem], recv.at[sem], (x, y, 1 - c)))
        for cp in copies:
            cp.start()
        for cp in arrive:
            cp.wait_recv()
        for cp in copies:
            cp.wait_send()

    res = pl.pallas_call(
        body, name=name, in_specs=[_ANY] * n, out_specs=[_ANY] * n,
        out_shape=[jax.ShapeDtypeStruct(b.shape, b.dtype) for b in bufs],
        scratch_shapes=[pltpu.SemaphoreType.DMA((3 * n,)), pltpu.SemaphoreType.DMA((3 * n,))],
        input_output_aliases={t: t for t in range(n)},
    )(*bufs)
    return list(res)


def _rs_sibling(name, gs):
    n = len(gs)

    def body(*refs):
        send, recv = refs[2 * n], refs[2 * n + 1]
        x, y, c, _, _ = _place()
        copies = []
        for t in range(n):
            rh = refs[t].shape[1] // 2
            cp = _remote(refs[t].at[:, pl.ds((1 - c) * rh, rh)], refs[n + t], send.at[t], recv.at[t], (x, y, 1 - c))
            cp.start()
            copies.append(cp)
        for cp in copies:
            cp.wait()

    return pl.pallas_call(
        body, name=name, in_specs=[_ANY] * n, out_specs=[_ANY] * n,
        out_shape=[jax.ShapeDtypeStruct((4, g.shape[1] // 2, g.shape[2]), g.dtype) for g in gs],
        scratch_shapes=[pltpu.SemaphoreType.DMA((n,)), pltpu.SemaphoreType.DMA((n,))],
    )(*gs)


def _chip_copies(sums, lands, send, recv):
    x, y, c, chips, idx = _place()
    return [_remote(sums[t].at[idx[k]], lands[t].at[k], send.at[3 * t + k], recv.at[3 * t + k], (*chips[k], c))
            for t in range(len(sums)) for k in range(3)]


def _rs_chips_start(name, sums, after):
    n = len(sums)
    lands = [lax.empty((3,) + s.shape[1:], s.dtype) for s in sums]

    def body(*refs):
        send, recv = refs[2 * n + 1], refs[2 * n + 2]
        for cp in _chip_copies(refs[:n], refs[n:2 * n], send, recv):
            cp.start()
        refs[4 * n + 3][...] = jnp.zeros((8, 128), _F32)

    both = [*sums, *lands]
    res = pl.pallas_call(
        body, name=name,
        out_shape=(pltpu.SemaphoreType.DMA((3 * n,)), pltpu.SemaphoreType.DMA((3 * n,)),
                   *[pltpu.HBM(b.shape, b.dtype) for b in both], jax.ShapeDtypeStruct((8, 128), _F32)),
        in_specs=[*[_HBM] * (2 * n), _ANY],
        out_specs=(_SEM, _SEM, *[_HBM] * (2 * n), pl.BlockSpec(memory_space=pltpu.VMEM)),
        input_output_aliases={t: 2 + t for t in range(2 * n)},
        compiler_params=pltpu.CompilerParams(has_side_effects=_EFFECT),
    )(*[pltpu.with_memory_space_constraint(b, pltpu.HBM) for b in both], after)
    return res[0], res[1], list(res[2:2 + n]), list(res[2 + n:2 + 2 * n]), res[2 + 2 * n]


def _rs_chips_wait(name, send_sem, recv_sem, sums, lands, after):
    n = len(sums)

    def body(*refs):
        send, recv = refs[2 * n], refs[2 * n + 1]
        copies = _chip_copies(refs[:n], refs[n:2 * n], send, recv)
        for cp in copies:
            cp.wait_send()
        for cp in copies:
            cp.wait_recv()

    both = [*sums, *lands]
    res = pl.pallas_call(
        body, name=name, out_shape=tuple(pltpu.HBM(b.shape, b.dtype) for b in both),
        in_specs=[*[_HBM] * (2 * n), _SEM, _SEM, *[_ANY] * len(after)], out_specs=tuple([_HBM] * (2 * n)),
        input_output_aliases={t: t for t in range(2 * n)},
        compiler_params=pltpu.CompilerParams(has_side_effects=_EFFECT),
    )(*both, send_sem, recv_sem, *after)
    return list(res[:n]), list(res[n:])


def _after(x, token):
    def body(x_ref, t_ref, o_ref):
        del x_ref, t_ref, o_ref

    return pl.pallas_call(
        body, name="after", in_specs=[_ANY, _ANY], out_specs=_ANY, out_shape=jax.ShapeDtypeStruct(x.shape, x.dtype),
        input_output_aliases={0: 0},
    )(x, token)


def _rs_join(name, totals, layers):
    n = len(totals)

    def body(*refs):
        outs, send, recv = refs[n:2 * n], refs[2 * n], refs[2 * n + 1]
        x, y, c, _, _ = _place()
        copies, arrive = [], []
        for t in range(n):
            rh = outs[t].shape[1] // 2
            mine = outs[t].at[layers[t], pl.ds(c * rh, rh)]
            landing = outs[t].at[layers[t], pl.ds((1 - c) * rh, rh)]
            copies.append(_remote(mine, mine, send.at[t], recv.at[t], (x, y, 1 - c)))
            arrive.append(_remote(landing, landing, send.at[t], recv.at[t], (x, y, 1 - c)))
        for cp in copies:
            cp.start()
        for cp in arrive:
            cp.wait_recv()
        for cp in copies:
            cp.wait_send()

    return pl.pallas_call(
        body, name=name, in_specs=[_ANY] * n, out_specs=[_ANY] * n,
        out_shape=[jax.ShapeDtypeStruct(b.shape, b.dtype) for b in totals],
        scratch_shapes=[pltpu.SemaphoreType.DMA((n,)), pltpu.SemaphoreType.DMA((n,))],
        input_output_aliases={t: t for t in range(n)},
    )(*totals)


def _pair_sum(g, got, where):
    _, r, cols = g.shape
    rh = r // 2
    tm = _row_tile(rh)
    per, perh = r // tm, rh // tm
    out = _placed_map("rs_add_pair", lambda u, w: u.astype(_F32) + w.astype(_F32), where,
                      [(_rows2d(g), lambda i, wh: (i // perh) * per + wh[0] * perh + i % perh),
                       (_rows2d(got), lambda i, wh: i)],
                      4 * rh, cols, _BF, lambda i, wh: i, 4 * perh, tm)
    return out.reshape(4, rh, cols)


def _sum_of_four(pair, land, where, total, layer, n_layers):
    _, rh, cols = pair.shape
    r = 2 * rh
    tm = _row_tile(rh)
    per, perh = r // tm, rh // tm

    def add_four(own, others):
        return ((own.astype(_F32) + others[0].astype(_F32)) + others[1].astype(_F32)) + others[2].astype(_F32)

    out = _placed_map("rs_add_four", add_four, where,
                      [(_rows2d(pair), lambda i, wh: wh[1] * perh + i), (land, lambda i, wh: i)],
                      n_layers * r, cols, _F32, lambda i, wh: layer * per + wh[0] * perh + i, perh, tm,
                      into=None if total is None else _rows2d(total))
    return out.reshape(n_layers, r, cols)


def _all_reduce_small(pack):
    rows, cols = pack.shape

    def body(p_ref, o_ref, land, send, recv):
        x, y, c = lax.axis_index("x"), lax.axis_index("y"), lax.axis_index("c")
        me = 4 * x + 2 * y + c
        land[me] = p_ref[...]
        copies = []
        for k in range(1, 8):
            bx, by, bc = (k >> 2) & 1, (k >> 1) & 1, k & 1
            peer = (1 - x if bx else x, 1 - y if by else y, 1 - c if bc else c)
            cp = _remote(p_ref, land.at[me], send.at[k - 1], recv.at[k - 1], peer)
            cp.start()
            copies.append((cp, 4 * peer[0] + 2 * peer[1] + peer[2]))
        for k, (cp, pid) in enumerate(copies):
            _remote(p_ref, land.at[pid], send.at[k], recv.at[k], (x, y, c)).wait_recv()
        for cp, _ in copies:
            cp.wait_send()
        acc = land[0]
        for d in range(1, 8):
            acc = acc + land[d]
        o_ref[...] = acc

    vm = pl.BlockSpec(memory_space=pltpu.VMEM)
    return pl.pallas_call(
        body, name="all_reduce_small", in_specs=[vm], out_specs=vm,
        out_shape=jax.ShapeDtypeStruct((rows, cols), _F32),
        scratch_shapes=[pltpu.VMEM((8, rows, cols), _F32), pltpu.SemaphoreType.DMA((7,)), pltpu.SemaphoreType.DMA((7,))],
    )(pack)


def _adamw_math(w, g, m, v):
    m2 = _ADAM_B1 * m + (1.0 - _ADAM_B1) * g
    v2 = _ADAM_B2 * v + (1.0 - _ADAM_B2) * (g * g)
    m_hat = m2 / (1.0 - _ADAM_B1 ** _ADAM_STEP)
    v_hat = v2 / (1.0 - _ADAM_B2 ** _ADAM_STEP)
    delta = -_ADAM_LR * (m_hat / (jnp.sqrt(v_hat) + _ADAM_EPS) + _ADAM_WD * w)
    return delta, m2, v2


def _adamw(w, g, m, v, lo=0, hi=None, into=None):
    shape = w.shape
    n_l = shape[0] if w.ndim == 3 else 1
    hi = n_l if hi is None else hi
    w2, g2, m2, v2 = (_rows2d(t) for t in (w, g, m, v))
    rows, cols = w2.shape
    per_layer = rows // n_l
    tm = _row_tile(per_layer)
    first, steps = lo * per_layer // tm, (hi - lo) * per_layer // tm
    n_into = 0 if into is None else 3

    def body(*refs):
        outs = refs[4 + n_into:]
        for ref, val in zip(outs, _adamw_math(*[r[...] for r in refs[:4]])):
            ref[...] = val

    spec = pl.BlockSpec((tm, cols), lambda i: (first + i, 0))
    sds = jax.ShapeDtypeStruct((rows, cols), _F32)
    outs = pl.pallas_call(
        body, name="adamw", grid=(steps,), in_specs=[spec] * 4 + [_ANY] * n_into, out_specs=[spec] * 3,
        out_shape=[sds] * 3, input_output_aliases={4 + k: k for k in range(n_into)},
        compiler_params=_params(("arbitrary",)),
    )(w2, g2, m2, v2, *([] if into is None else [_rows2d(t) for t in into]))
    return tuple(o.reshape(shape) for o in outs)


def kernel(x, positions, ffn1_norm, ffn1_w_in, ffn1_w_out, mix_norm, ffn2_norm, ffn2_w_in, ffn2_w_out, hyb_w_in, hyb_dw_w, hyb_dw_b, hyb_ln_g, hyb_ln_b, hyb_w_out, gdn_w_in, gdn_conv_w, gdn_A_log, gdn_dt_bias, gdn_norm_g, gdn_w_out, final_norm, loss_target, m_ffn1_norm, m_ffn1_w_in, m_ffn1_w_out, m_mix_norm, m_ffn2_norm, m_ffn2_w_in, m_ffn2_w_out, m_hyb_w_in, m_hyb_dw_w, m_hyb_dw_b, m_hyb_ln_g, m_hyb_ln_b, m_hyb_w_out, m_gdn_w_in, m_gdn_conv_w, m_gdn_A_log, m_gdn_dt_bias, m_gdn_norm_g, m_gdn_w_out, m_final_norm, v_ffn1_norm, v_ffn1_w_in, v_ffn1_w_out, v_mix_norm, v_ffn2_norm, v_ffn2_w_in, v_ffn2_w_out, v_hyb_w_in, v_hyb_dw_w, v_hyb_dw_b, v_hyb_ln_g, v_hyb_ln_b, v_hyb_w_out, v_gdn_w_in, v_gdn_conv_w, v_gdn_A_log, v_gdn_dt_bias, v_gdn_norm_g, v_gdn_w_out, v_final_norm):
    w = dict(ffn1_norm=ffn1_norm, ffn1_w_in=ffn1_w_in, ffn1_w_out=ffn1_w_out, mix_norm=mix_norm, ffn2_norm=ffn2_norm, ffn2_w_in=ffn2_w_in, ffn2_w_out=ffn2_w_out, hyb_w_in=hyb_w_in, hyb_dw_w=hyb_dw_w, hyb_dw_b=hyb_dw_b, hyb_ln_g=hyb_ln_g, hyb_ln_b=hyb_ln_b, hyb_w_out=hyb_w_out, gdn_w_in=gdn_w_in, gdn_conv_w=gdn_conv_w, gdn_A_log=gdn_A_log, gdn_dt_bias=gdn_dt_bias, gdn_norm_g=gdn_norm_g, gdn_w_out=gdn_w_out, final_norm=final_norm)
    m = dict(ffn1_norm=m_ffn1_norm, ffn1_w_in=m_ffn1_w_in, ffn1_w_out=m_ffn1_w_out, mix_norm=m_mix_norm, ffn2_norm=m_ffn2_norm, ffn2_w_in=m_ffn2_w_in, ffn2_w_out=m_ffn2_w_out, hyb_w_in=m_hyb_w_in, hyb_dw_w=m_hyb_dw_w, hyb_dw_b=m_hyb_dw_b, hyb_ln_g=m_hyb_ln_g, hyb_ln_b=m_hyb_ln_b, hyb_w_out=m_hyb_w_out, gdn_w_in=m_gdn_w_in, gdn_conv_w=m_gdn_conv_w, gdn_A_log=m_gdn_A_log, gdn_dt_bias=m_gdn_dt_bias, gdn_norm_g=m_gdn_norm_g, gdn_w_out=m_gdn_w_out, final_norm=m_final_norm)
    v = dict(ffn1_norm=v_ffn1_norm, ffn1_w_in=v_ffn1_w_in, ffn1_w_out=v_ffn1_w_out, mix_norm=v_mix_norm, ffn2_norm=v_ffn2_norm, ffn2_w_in=v_ffn2_w_in, ffn2_w_out=v_ffn2_w_out, hyb_w_in=v_hyb_w_in, hyb_dw_w=v_hyb_dw_w, hyb_dw_b=v_hyb_dw_b, hyb_ln_g=v_hyb_ln_g, hyb_ln_b=v_hyb_ln_b, hyb_w_out=v_hyb_w_out, gdn_w_in=v_gdn_w_in, gdn_conv_w=v_gdn_conv_w, gdn_A_log=v_gdn_A_log, gdn_dt_bias=v_gdn_dt_bias, gdn_norm_g=v_gdn_norm_g, gdn_w_out=v_gdn_w_out, final_norm=v_final_norm)
    return _train_step(x, positions, loss_target, w, m, v)


_WEIGHT_ORDER = ("ffn1_norm", "ffn1_w_in", "ffn1_w_out", "mix_norm", "ffn2_norm", "ffn2_w_in", "ffn2_w_out",
                 "hyb_w_in", "hyb_dw_w", "hyb_dw_b", "hyb_ln_g", "hyb_ln_b", "hyb_w_out", "gdn_w_in", "gdn_conv_w",
                 "gdn_A_log", "gdn_dt_bias", "gdn_norm_g", "gdn_w_out", "final_norm")
_BIG = ("ffn1_w_in", "ffn1_w_out", "ffn2_w_in", "ffn2_w_out", "hyb_w_in", "hyb_w_out", "gdn_w_in", "gdn_w_out")
_PACK_COLS = 1024


def _lane16(vec):
    return jnp.zeros((1, 128), _F32).at[0, _GDN_HEADS:2 * _GDN_HEADS].set(vec)


def _pack_rows(parts):
    rows, where, at = [], [], 0
    for p in parts:
        flat = p.reshape(-1).astype(_F32)
        n = -(-flat.shape[0] // (8 * _PACK_COLS)) * 8
        rows.append(jnp.pad(flat, (0, n * _PACK_COLS - flat.shape[0])).reshape(n, _PACK_COLS))
        where.append((at, n, flat.shape[0], p.shape))
        at += n
    return jnp.concatenate(rows, axis=0), where


def _unpack_rows(pack, where):
    return [pack[at:at + n].reshape(-1)[:size].reshape(shape) for at, n, size, shape in where]


def _train_step(x, positions, loss_target, w, m, v):
    n_layers = w["ffn1_norm"].shape[0]
    n_hyb, n_gdn = w["hyb_w_in"].shape[0], w["gdn_w_in"].shape[0]
    d_model = x.shape[-1]
    xh, target = x[0], loss_target[0]
    chip = 2 * lax.axis_index("x") + lax.axis_index("y")
    where = jnp.stack([lax.axis_index("c"), chip]).astype(jnp.int32)

    def own_columns(shard):
        cols = shard.shape[-1]
        zeros = jnp.zeros(shard.shape[:-1] + (4 * cols,), _F32)
        return lax.dynamic_update_slice_in_dim(zeros, shard, chip * cols, axis=2)

    small_pack, small_where = _pack_rows([own_columns(w["hyb_dw_w"]), own_columns(w["gdn_conv_w"])])
    small_full = _all_reduce_small(small_pack)
    dw_w, conv_w = _unpack_rows(0.5 * small_full, small_where)
    dw_w = jnp.pad(dw_w, ((0, 0), (0, _HALO - _CONV_WIDTH), (0, 0)))

    groups = (("ffn1_w_in", "ffn1_w_out"), ("hyb_w_in", "hyb_w_out"), ("ffn2_w_in", "ffn2_w_out"),
              ("gdn_w_in", "gdn_w_out"))
    steps = [(0, (0, 2)), (1, (0, 1)), (0, (1, 2)), (2, (0, 1)), (3, (0, 1))]
    bufs = [None] * len(groups)
    pending, token = [], small_full
    for si, (gi, part) in enumerate(steps):
        if bufs[gi] is None:
            bufs[gi] = [_cast_slab(n, w[n], where, token) for n in groups[gi]]
        send, recv, bufs[gi], token = _gather_start(f"gather_start_{si}", bufs[gi], token, part)
        pending.append((send, recv))
    gdn_cols = w["gdn_w_in"].shape[-1]
    n_qkvz = 4 * _GDN_HEADS * _GDN_DIM
    n_ba = 4 * gdn_cols - n_qkvz
    wv = {}

    def arrive(si, after):
        gi, part = steps[si]
        send, recv = pending[si]
        bufs[gi] = _gather_wait(f"gather_wait_{si}", send, recv, bufs[gi], after, part)
        bufs[gi] = _gather_forward(f"gather_pass_{si}", bufs[gi], part)
        a, b = bufs[gi]
        kind = groups[gi][0]
        if kind == "gdn_w_in":
            gdn_in = jnp.transpose(a, (0, 2, 1, 3)).reshape(n_gdn, d_model, 4 * gdn_cols)
            wv["gdn_qkvz"] = gdn_in[:, :, :n_qkvz]
            wv["gdn_ba"] = jnp.pad(gdn_in[:, :, n_qkvz:], ((0, 0), (0, 0), (0, 128 - n_ba)))
        else:
            wv[kind] = a
        wv[groups[gi][1]] = b.reshape(b.shape[0], -1, d_model)

    tabs = _rotary_tables(positions)

    def mixer_args(layer):
        i = layer // 2
        if layer % 2 == 0:
            return (wv["hyb_w_in"], wv["hyb_w_out"], dw_w[i], w["hyb_dw_b"][i][None], w["hyb_ln_g"][i][None],
                    w["hyb_ln_b"][i][None], tabs, i)
        return (wv["gdn_qkvz"], wv["gdn_ba"], wv["gdn_w_out"], conv_w[i], _lane16(w["gdn_A_log"][i]),
                _lane16(w["gdn_dt_bias"][i]), w["gdn_norm_g"][i][None], i)

    h = xh
    saved = []
    arrive(0, token)
    for layer in range(n_layers):
        if layer == 1:
            arrive(2, h)
        h, s1 = _ffn_fwd(h, w["ffn1_norm"][layer][None], wv["ffn1_w_in"], wv["ffn1_w_out"], layer)
        gamma = w["mix_norm"][layer][None]
        if layer % 2 == 0:
            if layer == 0:
                arrive(1, h)
            h, sm = _hyb_fwd(h, gamma, *mixer_args(layer))
        else:
            if layer == 1:
                arrive(4, h)
            h, sm = _gdn_fwd(h, gamma, *mixer_args(layer))
        if layer == 0:
            arrive(3, h)
        h, s2 = _ffn_fwd(h, w["ffn2_norm"][layer][None], wv["ffn2_w_in"], wv["ffn2_w_out"], layer)
        saved.append((s1, sm, s2))
    dh, loss_row, d_final = _loss_head(h, w["final_norm"][None], target)

    d_norm = {k: [None] * n_layers for k in ("ffn1_norm", "mix_norm", "ffn2_norm")}
    d_hyb = [None] * n_hyb
    d_gdn = [None] * n_gdn
    grads = {name: None for name in _BIG}
    n_of = {name: w[name].shape[0] for name in _BIG}

    def rs_begin(layer, full, token):
        names = list(full)
        got = _rs_sibling(f"rs_sib_{layer}", [full[n] for n in names])
        sums = [_pair_sum(full[n], g, where) for n, g in zip(names, got)]
        send, recv, sums, lands, token = _rs_chips_start(f"rs_chips_start_{layer}", sums, token)
        return (layer, names, send, recv, sums, lands), token

    def rs_end(state, after):
        layer, names, send, recv, sums, lands = state
        sums, lands = _rs_chips_wait(f"rs_chips_wait_{layer}", send, recv, sums, lands, after)
        index = [layer if n.startswith("ffn") else layer // 2 for n in names]
        for n, li, pair, land in zip(names, index, sums, lands):
            grads[n] = _sum_of_four(pair, land, where, grads[n], li, n_of[n])
        joined = _rs_join(f"rs_join_{layer}", [grads[n] for n in names], index)
        grads.update(zip(names, joined))

    in_flight = None
    for layer in reversed(range(n_layers)):
        s1, sm, s2 = saved[layer]
        full = {}
        dh, d_norm["ffn2_norm"][layer], full["ffn2_w_in"], full["ffn2_w_out"] = _ffn_bwd(
            dh, s2, w["ffn2_norm"][layer][None], wv["ffn2_w_in"], wv["ffn2_w_out"], layer)
        gamma = w["mix_norm"][layer][None]
        if layer % 2 == 0:
            dh, d_norm["mix_norm"][layer], d_hyb[layer // 2], full["hyb_w_in"], full["hyb_w_out"] = _hyb_bwd(
                dh, sm, gamma, *mixer_args(layer))
        else:
            dh, d_norm["mix_norm"][layer], d_gdn[layer // 2], full["gdn_w_in"], full["gdn_w_out"] = _gdn_bwd(
                dh, sm, gamma, *mixer_args(layer), n_ba)
        dh, d_norm["ffn1_norm"][layer], full["ffn1_w_in"], full["ffn1_w_out"] = _ffn_bwd(
            dh, s1, w["ffn1_norm"][layer][None], wv["ffn1_w_in"], wv["ffn1_w_out"], layer)
        if in_flight is not None:
            rs_end(in_flight, [dh])
        if layer == 0:
            stack = lambda rows: jnp.concatenate(rows, axis=0)
            misc = jnp.concatenate([stack([g[1] for g in d_gdn])[:, _GDN_HEADS:2 * _GDN_HEADS].reshape(-1),
                                    stack([g[2] for g in d_gdn])[:, _GDN_HEADS:2 * _GDN_HEADS].reshape(-1),
                                    stack([g[3] for g in d_gdn]).reshape(-1), loss_row[0, :1]])
            parts = [stack(d_norm["ffn1_norm"]), stack(d_norm["mix_norm"]), stack(d_norm["ffn2_norm"]), d_final,
                     stack([g[1] for g in d_hyb]), stack([g[2] for g in d_hyb]), stack([g[3] for g in d_hyb]), misc,
                     jnp.stack([g[0][:_CONV_WIDTH] for g in d_hyb]), jnp.stack([g[0] for g in d_gdn])]
            pack, pack_where = _pack_rows(parts)
            token = small_sum = _all_reduce_small(pack)
        in_flight, token = rs_begin(layer, full, token)
        if layer > 0:
            dh = _after(dh, token)
    grad_x = dh[None]

    red = _unpack_rows(small_sum, pack_where)
    n_a = n_gdn * _GDN_HEADS
    misc = red[7]
    cw = w["hyb_dw_w"].shape[-1]
    gw = w["gdn_conv_w"].shape[-1]
    grads.update(
        ffn1_norm=red[0], mix_norm=red[1], ffn2_norm=red[2], final_norm=red[3].reshape(-1),
        hyb_dw_b=red[4], hyb_ln_g=red[5], hyb_ln_b=red[6],
        gdn_A_log=misc[:n_a].reshape(n_gdn, _GDN_HEADS), gdn_dt_bias=misc[n_a:2 * n_a].reshape(n_gdn, _GDN_HEADS),
        gdn_norm_g=misc[2 * n_a:2 * n_a + n_gdn * _GDN_DIM].reshape(n_gdn, _GDN_DIM),
        hyb_dw_w=lax.dynamic_slice_in_dim(red[8], chip * cw, cw, axis=2),
        gdn_conv_w=lax.dynamic_slice_in_dim(red[9], chip * gw, gw, axis=2))
    loss = misc[2 * n_a + n_gdn * _GDN_DIM]

    small = [n for n in _WEIGHT_ORDER if n not in _BIG]
    packs = [_pack_rows([src[n] for n in small]) for src in (w, grads, m, v)]
    outs = _adamw(packs[0][0], _after(packs[1][0], token), packs[2][0], packs[3][0])
    for name in _BIG:
        grads[name] = _after(grads[name], token)
    delta, new_m, new_v = {}, {}, {}
    for res, o in zip((delta, new_m, new_v), outs):
        res.update(dict(zip(small, _unpack_rows(o, packs[0][1]))))
    late = in_flight[1]
    early = {name: _adamw(w[name], grads[name], m[name], v[name], lo=1 if name in late else 0) for name in _BIG}
    rs_end(in_flight, [token, outs[0], *[early[name][0] for name in _BIG]])
    for name in _BIG:
        res = early[name]
        if name in late:
            res = _adamw(w[name], grads[name], m[name], v[name], lo=0, hi=1, into=res)
        delta[name], new_m[name], new_v[name] = res
    return (loss, grad_x, *[grads[n] for n in _WEIGHT_ORDER], *[delta[n] for n in _WEIGHT_ORDER],
            *[new_m[n] for n in _WEIGHT_ORDER], *[new_v[n] for n in _WEIGHT_ORDER])
```
